```python
import jax, jax.numpy as jnp
from jax import lax
import numpy as np

D_MODEL = 1024
BATCH = 8
SEQ = 4096
DEPTH = 1

SB_HEADS = 8
SB_HEAD_DIM = 64
SB_WIDTH = SB_HEADS * SB_HEAD_DIM
SB_QBLOCK = 128
DN_HEADS = 8
DN_KEY_DIM = 64
DN_VAL_DIM = 128
DN_QK_WIDTH = DN_HEADS * DN_KEY_DIM
DN_V_WIDTH = DN_HEADS * DN_VAL_DIM
DN_CONV_CH = 2 * DN_QK_WIDTH + DN_V_WIDTH
DN_CONV_WIDTH = 4
DN_CHUNK = 64
D_FF = 2816
FFN_CONV_WIDTH = 3
NORM_EPS = 1e-6
L2_EPS = 1e-6
IN_SIZES = (3 * SB_WIDTH, DN_CONV_CH, DN_V_WIDTH, DN_HEADS, DN_HEADS, 2 * D_MODEL)
IN_WIDTH = int(sum(IN_SIZES))
IN_SPLITS = tuple(int(s) for s in np.cumsum(IN_SIZES)[:-1])

kernel_name = "hybrid_stickbreak_gdn_convffn_block"


def rmsnorm(x, w):
    xf = x.astype(jnp.float32)
    y = xf * lax.rsqrt(jnp.mean(xf * xf, axis=-1, keepdims=True) + NORM_EPS)
    return (y * w.astype(jnp.float32)).astype(x.dtype)


def modulate(h, shift, scale):
    return h * (1.0 + scale) + shift


def causal_dwconv(x, w):
    K, C = w.shape
    return lax.conv_general_dilated(
        x, w[:, None, :], window_strides=(1,), padding=[(K - 1, 0)],
        dimension_numbers=("NWC", "WIO", "NWC"), feature_group_count=C)


def stick_breaking_attention(q, k, v):
    S = q.shape[1]
    scale = q.shape[-1] ** -0.5
    outs = []
    for blk in range(S // SB_QBLOCK):
        s0 = blk * SB_QBLOCK
        end = s0 + SB_QBLOCK
        z = jnp.einsum("bthd,bshd->bhts", q[:, s0:end], k[:, :end]).astype(jnp.float32) * scale
        t_pos = s0 + jnp.arange(SB_QBLOCK)
        s_pos = jnp.arange(end)
        causal = s_pos[None, :] < t_pos[:, None]
        log_1m_beta = jnp.where(causal, jax.nn.log_sigmoid(-z), 0.0)
        log_stick = lax.cumsum(log_1m_beta, axis=3, reverse=True) - log_1m_beta
        A = jnp.where(causal, jnp.exp(jax.nn.log_sigmoid(z) + log_stick), 0.0)
        outs.append(jnp.einsum("bhts,bshd->bthd", A.astype(v.dtype), v[:, :end]))
    return jnp.concatenate(outs, axis=1)


def _l2norm(t):
    return t * lax.rsqrt(jnp.sum(t * t, axis=-1, keepdims=True) + L2_EPS)


def gated_delta_rule(q, k, v, g, beta):
    out_dtype = v.dtype
    f32 = jnp.float32
    b, s, h, dk = q.shape
    dv = v.shape[-1]
    C = DN_CHUNK
    n = s // C
    q = _l2norm(q.astype(f32)) * (dk ** -0.5)
    k = _l2norm(k.astype(f32))
    v = v.astype(f32)

    def to_chunks(t):
        return t.reshape(b, n, C, h, t.shape[-1]).transpose(0, 3, 1, 2, 4)

    qc, kc, vc = to_chunks(q), to_chunks(k), to_chunks(v)
    gcum = jnp.cumsum(g.astype(f32).reshape(b, n, C, h).transpose(0, 3, 1, 2), axis=-1)
    bc = beta.astype(f32).reshape(b, n, C, h).transpose(0, 3, 1, 2)[..., None]
    incl = jnp.tril(jnp.ones((C, C), bool))
    strict = jnp.tril(jnp.ones((C, C), bool), -1)
    diff = gcum[..., :, None] - gcum[..., None, :]
    decay = jnp.where(incl, jnp.exp(jnp.where(incl, diff, 0.0)), 0.0)
    kb = kc * bc
    L = jnp.where(strict, jnp.einsum("bhnid,bhnjd->bhnij", kb, kc) * decay, 0.0)
    eye = jnp.eye(C, dtype=f32)
    T = lax.linalg.triangular_solve(eye + L, jnp.broadcast_to(eye, L.shape),
                                    left_side=True, lower=True, unit_diagonal=True)
    u = jnp.einsum("bhnij,bhnjd->bhnid", T, vc * bc)
    w = jnp.einsum("bhnij,bhnjd->bhnid", T, kb * jnp.exp(gcum)[..., None])
    intra = jnp.where(incl, jnp.einsum("bhnid,bhnjd->bhnij", qc, kc) * decay, 0.0)

    def step(state, inp):
        q_i, k_i, u_i, w_i, g_i, a_i = inp
        v_new = u_i - jnp.einsum("bhck,bhkv->bhcv", w_i, state)
        o = (jnp.einsum("bhck,bhkv->bhcv", q_i * jnp.exp(g_i)[..., None], state)
             + jnp.einsum("bhij,bhjv->bhiv", a_i, v_new))
        g_last = g_i[..., -1:]
        k_dec = k_i * jnp.exp(g_last - g_i)[..., None]
        state = state * jnp.exp(g_last)[..., None] + jnp.einsum("bhck,bhcv->bhkv", k_dec, v_new)
        return state, o

    mv = lambda t: jnp.moveaxis(t, 2, 0)
    xs = (mv(qc), mv(kc), mv(u), mv(w), mv(gcum), mv(intra))
    state0 = jnp.zeros((b, h, dk, dv), f32)
    _, o = lax.scan(step, state0, xs)
    o = o.transpose(1, 0, 3, 2, 4).reshape(b, s, h, dv)
    return o.astype(out_dtype)


def _fwd_setup_inputs(seed: int = 0) -> dict:
    key = jax.random.key(seed)
    ks = jax.random.split(key, 24)
    f32 = jnp.float32
    nrm = lambda k, shape, fan_in: jax.random.normal(k, shape, f32) * (fan_in ** -0.5)
    gain = lambda k, shape: 1.0 + 0.02 * jax.random.normal(k, shape, f32)
    x = jax.random.normal(ks[0], (BATCH, SEQ, D_MODEL), f32)
    c = jax.random.normal(ks[1], (BATCH, D_MODEL), f32)
    w_ada = nrm(ks[2], (DEPTH, D_MODEL, 6 * D_MODEL), D_MODEL)
    b_ada = 0.02 * jax.random.normal(ks[3], (DEPTH, 6 * D_MODEL), f32)
    norm1_w = gain(ks[4], (DEPTH, D_MODEL))
    w_in = nrm(ks[5], (DEPTH, D_MODEL, IN_WIDTH), D_MODEL)
    dn_conv_w = nrm(ks[6], (DEPTH, DN_CONV_WIDTH, DN_CONV_CH), DN_CONV_WIDTH)
    dn_A_log = jnp.log(jax.random.uniform(ks[7], (DEPTH, DN_HEADS), f32, 1.0, 16.0))
    dt = jnp.exp(jax.random.uniform(ks[8], (DEPTH, DN_HEADS), f32, jnp.log(0.001), jnp.log(0.1)))
    dn_dt_bias = dt + jnp.log(-jnp.expm1(-dt))
    dn_norm_w = gain(ks[9], (DEPTH, DN_VAL_DIM))
    w_proj_sb = nrm(ks[10], (DEPTH, SB_WIDTH, D_MODEL), SB_WIDTH)
    w_proj_dn = nrm(ks[11], (DEPTH, DN_V_WIDTH, D_MODEL), DN_V_WIDTH)
    w_out = nrm(ks[12], (DEPTH, D_MODEL, D_MODEL), D_MODEL)
    norm2_w = gain(ks[13], (DEPTH, D_MODEL))
    w_ffn_in = nrm(ks[14], (DEPTH, D_MODEL, 2 * D_FF), D_MODEL)
    ffn_conv_w = nrm(ks[15], (DEPTH, FFN_CONV_WIDTH, 2 * D_FF), FFN_CONV_WIDTH)
    ffn_conv_b = 0.02 * jax.random.normal(ks[16], (DEPTH, 2 * D_FF), f32)
    w_ffn_out = nrm(ks[17], (DEPTH, D_FF, D_MODEL), D_FF)
    final_norm_w = gain(ks[18], (D_MODEL,))
    return {"x": x, "c": c, "w_ada": w_ada, "b_ada": b_ada, "norm1_w": norm1_w, "w_in": w_in,
            "dn_conv_w": dn_conv_w, "dn_A_log": dn_A_log, "dn_dt_bias": dn_dt_bias,
            "dn_norm_w": dn_norm_w, "w_proj_sb": w_proj_sb, "w_proj_dn": w_proj_dn, "w_out": w_out,
            "norm2_w": norm2_w, "w_ffn_in": w_ffn_in, "ffn_conv_w": ffn_conv_w,
            "ffn_conv_b": ffn_conv_b, "w_ffn_out": w_ffn_out, "final_norm_w": final_norm_w}


def _fwd_reference(x, c, w_ada, b_ada, norm1_w, w_in, dn_conv_w, dn_A_log, dn_dt_bias, dn_norm_w,
              w_proj_sb, w_proj_dn, w_out, norm2_w, w_ffn_in, ffn_conv_w, ffn_conv_b, w_ffn_out,
              final_norm_w):
    B, S, _ = x.shape
    c_act = jax.nn.silu(c)
    for l in range(DEPTH):
        mod = (c_act @ w_ada[l] + b_ada[l])[:, None, :]
        shift1, scale1, gate1, shift2, scale2, gate2 = jnp.split(mod, 6, axis=-1)

        h = modulate(rmsnorm(x, norm1_w[l]), shift1, scale1)
        proj = h @ w_in[l]
        sb_qkv, dn_qkv, dn_z, dn_b, dn_a, br_gates = jnp.split(proj, IN_SPLITS, axis=-1)

        sb_q, sb_k, sb_v = jnp.split(sb_qkv.reshape(B, S, 3, SB_HEADS, SB_HEAD_DIM), 3, axis=2)
        o_a = stick_breaking_attention(sb_q[:, :, 0], sb_k[:, :, 0], sb_v[:, :, 0]).reshape(B, S, SB_WIDTH)

        dn_qkv = jax.nn.silu(causal_dwconv(dn_qkv, dn_conv_w[l]))
        dq, dk_, dv_ = jnp.split(dn_qkv, (DN_QK_WIDTH, 2 * DN_QK_WIDTH), axis=-1)
        dq = dq.reshape(B, S, DN_HEADS, DN_KEY_DIM)
        dk_ = dk_.reshape(B, S, DN_HEADS, DN_KEY_DIM)
        dv_ = dv_.reshape(B, S, DN_HEADS, DN_VAL_DIM)
        beta = jax.nn.sigmoid(dn_b)
        g = -jnp.exp(dn_A_log[l]) * jax.nn.softplus(dn_a + dn_dt_bias[l])
        o_b = gated_delta_rule(dq, dk_, dv_, g, beta)
        o_b = rmsnorm(o_b, dn_norm_w[l]) * jax.nn.silu(dn_z.reshape(B, S, DN_HEADS, DN_VAL_DIM))
        o_b = o_b.reshape(B, S, DN_V_WIDTH)

        gate_a, gate_b = jnp.split(jax.nn.sigmoid(br_gates), 2, axis=-1)
        merged = gate_a * (o_a @ w_proj_sb[l]) + gate_b * (o_b @ w_proj_dn[l])
        x = x + gate1 * (merged @ w_out[l])

        h2 = modulate(rmsnorm(x, norm2_w[l]), shift2, scale2)
        u = causal_dwconv(h2 @ w_ffn_in[l], ffn_conv_w[l]) + ffn_conv_b[l]
        u_gate, u_up = jnp.split(u, 2, axis=-1)
        x = x + gate2 * ((jax.nn.silu(u_gate) * u_up) @ w_ffn_out[l])
    return rmsnorm(x, final_norm_w)


import jax as _jax
import jax.numpy as _jnp

TWIN_FORMAT = 'train_step'
FWD_PARAMS = ['x', 'c', 'w_ada', 'b_ada', 'norm1_w', 'w_in', 'dn_conv_w', 'dn_A_log', 'dn_dt_bias', 'dn_norm_w', 'w_proj_sb', 'w_proj_dn', 'w_out', 'norm2_w', 'w_ffn_in', 'ffn_conv_w', 'ffn_conv_b', 'w_ffn_out', 'final_norm_w']
TWIN_WEIGHTS = ['w_ada', 'b_ada', 'norm1_w', 'w_in', 'dn_conv_w', 'dn_A_log', 'dn_dt_bias', 'dn_norm_w', 'w_proj_sb', 'w_proj_dn', 'w_out', 'norm2_w', 'w_ffn_in', 'ffn_conv_w', 'ffn_conv_b', 'w_ffn_out', 'final_norm_w']
TWIN_DIFF_INPUT = 'x'
TWIN_INPUTS = ['x', 'c', 'w_ada', 'b_ada', 'norm1_w', 'w_in', 'dn_conv_w', 'dn_A_log', 'dn_dt_bias', 'dn_norm_w', 'w_proj_sb', 'w_proj_dn', 'w_out', 'norm2_w', 'w_ffn_in', 'ffn_conv_w', 'ffn_conv_b', 'w_ffn_out', 'final_norm_w', 'loss_target', 'm_w_ada', 'm_b_ada', 'm_norm1_w', 'm_w_in', 'm_dn_conv_w', 'm_dn_A_log', 'm_dn_dt_bias', 'm_dn_norm_w', 'm_w_proj_sb', 'm_w_proj_dn', 'm_w_out', 'm_norm2_w', 'm_w_ffn_in', 'm_ffn_conv_w', 'm_ffn_conv_b', 'm_w_ffn_out', 'm_final_norm_w', 'v_w_ada', 'v_b_ada', 'v_norm1_w', 'v_w_in', 'v_dn_conv_w', 'v_dn_A_log', 'v_dn_dt_bias', 'v_dn_norm_w', 'v_w_proj_sb', 'v_w_proj_dn', 'v_w_out', 'v_norm2_w', 'v_w_ffn_in', 'v_ffn_conv_w', 'v_ffn_conv_b', 'v_w_ffn_out', 'v_final_norm_w']
TWIN_OUTPUTS = ['loss', 'grad_x', 'grad_w_ada', 'grad_b_ada', 'grad_norm1_w', 'grad_w_in', 'grad_dn_conv_w', 'grad_dn_A_log', 'grad_dn_dt_bias', 'grad_dn_norm_w', 'grad_w_proj_sb', 'grad_w_proj_dn', 'grad_w_out', 'grad_norm2_w', 'grad_w_ffn_in', 'grad_ffn_conv_w', 'grad_ffn_conv_b', 'grad_w_ffn_out', 'grad_final_norm_w', 'delta_w_ada', 'delta_b_ada', 'delta_norm1_w', 'delta_w_in', 'delta_dn_conv_w', 'delta_dn_A_log', 'delta_dn_dt_bias', 'delta_dn_norm_w', 'delta_w_proj_sb', 'delta_w_proj_dn', 'delta_w_out', 'delta_norm2_w', 'delta_w_ffn_in', 'delta_ffn_conv_w', 'delta_ffn_conv_b', 'delta_w_ffn_out', 'delta_final_norm_w', 'new_m_w_ada', 'new_m_b_ada', 'new_m_norm1_w', 'new_m_w_in', 'new_m_dn_conv_w', 'new_m_dn_A_log', 'new_m_dn_dt_bias', 'new_m_dn_norm_w', 'new_m_w_proj_sb', 'new_m_w_proj_dn', 'new_m_w_out', 'new_m_norm2_w', 'new_m_w_ffn_in', 'new_m_ffn_conv_w', 'new_m_ffn_conv_b', 'new_m_w_ffn_out', 'new_m_final_norm_w', 'new_v_w_ada', 'new_v_b_ada', 'new_v_norm1_w', 'new_v_w_in', 'new_v_dn_conv_w', 'new_v_dn_A_log', 'new_v_dn_dt_bias', 'new_v_dn_norm_w', 'new_v_w_proj_sb', 'new_v_w_proj_dn', 'new_v_w_out', 'new_v_norm2_w', 'new_v_w_ffn_in', 'new_v_ffn_conv_w', 'new_v_ffn_conv_b', 'new_v_w_ffn_out', 'new_v_final_norm_w']
TWIN_LEAF_KINDS = {'loss': 'loss', 'grad_x': 'grad_x', 'grad_w_ada': 'grad_w', 'grad_b_ada': 'grad_w', 'grad_norm1_w': 'grad_w', 'grad_w_in': 'grad_w', 'grad_dn_conv_w': 'grad_w', 'grad_dn_A_log': 'grad_w', 'grad_dn_dt_bias': 'grad_w', 'grad_dn_norm_w': 'grad_w', 'grad_w_proj_sb': 'grad_w', 'grad_w_proj_dn': 'grad_w', 'grad_w_out': 'grad_w', 'grad_norm2_w': 'grad_w', 'grad_w_ffn_in': 'grad_w', 'grad_ffn_conv_w': 'grad_w', 'grad_ffn_conv_b': 'grad_w', 'grad_w_ffn_out': 'grad_w', 'grad_final_norm_w': 'grad_w', 'delta_w_ada': 'delta_w', 'delta_b_ada': 'delta_w', 'delta_norm1_w': 'delta_w', 'delta_w_in': 'delta_w', 'delta_dn_conv_w': 'delta_w', 'delta_dn_A_log': 'delta_w', 'delta_dn_dt_bias': 'delta_w', 'delta_dn_norm_w': 'delta_w', 'delta_w_proj_sb': 'delta_w', 'delta_w_proj_dn': 'delta_w', 'delta_w_out': 'delta_w', 'delta_norm2_w': 'delta_w', 'delta_w_ffn_in': 'delta_w', 'delta_ffn_conv_w': 'delta_w', 'delta_ffn_conv_b': 'delta_w', 'delta_w_ffn_out': 'delta_w', 'delta_final_norm_w': 'delta_w', 'new_m_w_ada': 'new_m', 'new_m_b_ada': 'new_m', 'new_m_norm1_w': 'new_m', 'new_m_w_in': 'new_m', 'new_m_dn_conv_w': 'new_m', 'new_m_dn_A_log': 'new_m', 'new_m_dn_dt_bias': 'new_m', 'new_m_dn_norm_w': 'new_m', 'new_m_w_proj_sb': 'new_m', 'new_m_w_proj_dn': 'new_m', 'new_m_w_out': 'new_m', 'new_m_norm2_w': 'new_m', 'new_m_w_ffn_in': 'new_m', 'new_m_ffn_conv_w': 'new_m', 'new_m_ffn_conv_b': 'new_m', 'new_m_w_ffn_out': 'new_m', 'new_m_final_norm_w': 'new_m', 'new_v_w_ada': 'new_v', 'new_v_b_ada': 'new_v', 'new_v_norm1_w': 'new_v', 'new_v_w_in': 'new_v', 'new_v_dn_conv_w': 'new_v', 'new_v_dn_A_log': 'new_v', 'new_v_dn_dt_bias': 'new_v', 'new_v_dn_norm_w': 'new_v', 'new_v_w_proj_sb': 'new_v', 'new_v_w_proj_dn': 'new_v', 'new_v_w_out': 'new_v', 'new_v_norm2_w': 'new_v', 'new_v_w_ffn_in': 'new_v', 'new_v_ffn_conv_w': 'new_v', 'new_v_ffn_conv_b': 'new_v', 'new_v_w_ffn_out': 'new_v', 'new_v_final_norm_w': 'new_v'}


def _forward(args):
    return _fwd_reference(*[args[k] for k in FWD_PARAMS])


def _output_shape():
    def fwd():
        inp = _fwd_setup_inputs(0)
        return _fwd_reference(*[inp[k] for k in FWD_PARAMS])
    out = _jax.eval_shape(fwd)
    return out.shape, out.dtype

N_MICROBATCH = 1
ADAM_LR = 0.001
ADAM_B1 = 0.9
ADAM_B2 = 0.999
ADAM_EPS = 1e-08
ADAM_WD = 0.01
ADAM_STEP = 10
PER_EXAMPLE_BATCH_AXIS = {'x': 0, 'c': 0, 'loss_target': 0}
SHARED_INPUTS = []
_WEIGHT_DTYPES = {'w_ada': _jnp.float32, 'b_ada': _jnp.float32, 'norm1_w': _jnp.float32, 'w_in': _jnp.float32, 'dn_conv_w': _jnp.float32, 'dn_A_log': _jnp.float32, 'dn_dt_bias': _jnp.float32, 'dn_norm_w': _jnp.float32, 'w_proj_sb': _jnp.float32, 'w_proj_dn': _jnp.float32, 'w_out': _jnp.float32, 'norm2_w': _jnp.float32, 'w_ffn_in': _jnp.float32, 'ffn_conv_w': _jnp.float32, 'ffn_conv_b': _jnp.float32, 'w_ffn_out': _jnp.float32, 'final_norm_w': _jnp.float32}
MOMENT_SCALE = {'w_ada': 8.077946e-02, 'b_ada': 1.412345e-01, 'norm1_w': 1.187396e-01, 'w_in': 4.881705e-02, 'dn_conv_w': 5.169876e-02, 'dn_A_log': 1.927757e-01, 'dn_dt_bias': 1.878762e-01, 'dn_norm_w': 1.333497e-01, 'w_proj_sb': 6.262752e-02, 'w_proj_dn': 4.927540e-02, 'w_out': 7.944382e-02, 'norm2_w': 1.269401e-01, 'w_ffn_in': 5.801722e-02, 'ffn_conv_w': 5.721230e-02, 'ffn_conv_b': 4.267174e-02, 'w_ffn_out': 9.586635e-02, 'final_norm_w': 3.250467e+01}


def _to_microbatches(a, axis):
    t = _jnp.moveaxis(a, axis, 0)
    t = t.reshape((N_MICROBATCH, t.shape[0] // N_MICROBATCH) + t.shape[1:])
    return _jnp.moveaxis(t, 1, axis + 1)


def setup_inputs(seed: int = 0) -> dict:
    inp = _fwd_setup_inputs(seed)
    key = _jax.random.fold_in(_jax.random.key(seed), 7919)
    shape, _ = _output_shape()
    out = dict(inp)
    out["loss_target"] = _jax.random.normal(_jax.random.fold_in(key, 0), shape, _jnp.float32)
    for i, name in enumerate(TWIN_WEIGHTS):
        w = inp[name].astype(_jnp.float32)
        if MOMENT_SCALE is None:
            s = _jnp.sqrt(_jnp.mean(_jnp.square(w)) + 1e-30)
        else:
            s = MOMENT_SCALE[name]
        km, kv = _jax.random.split(_jax.random.fold_in(key, i + 1))
        out[name] = w
        out["m_" + name] = s * _jax.random.normal(km, w.shape, _jnp.float32)
        out["v_" + name] = (s * s) * _jax.random.uniform(kv, w.shape, _jnp.float32, 0.5, 1.5)
    if N_MICROBATCH > 1:
        for name, axis in PER_EXAMPLE_BATCH_AXIS.items():
            out[name] = _to_microbatches(out[name], axis)
    return {'x': out['x'], 'c': out['c'], 'w_ada': out['w_ada'], 'b_ada': out['b_ada'], 'norm1_w': out['norm1_w'], 'w_in': out['w_in'], 'dn_conv_w': out['dn_conv_w'], 'dn_A_log': out['dn_A_log'], 'dn_dt_bias': out['dn_dt_bias'], 'dn_norm_w': out['dn_norm_w'], 'w_proj_sb': out['w_proj_sb'], 'w_proj_dn': out['w_proj_dn'], 'w_out': out['w_out'], 'norm2_w': out['norm2_w'], 'w_ffn_in': out['w_ffn_in'], 'ffn_conv_w': out['ffn_conv_w'], 'ffn_conv_b': out['ffn_conv_b'], 'w_ffn_out': out['w_ffn_out'], 'final_norm_w': out['final_norm_w'], 'loss_target': out['loss_target'], 'm_w_ada': out['m_w_ada'], 'm_b_ada': out['m_b_ada'], 'm_norm1_w': out['m_norm1_w'], 'm_w_in': out['m_w_in'], 'm_dn_conv_w': out['m_dn_conv_w'], 'm_dn_A_log': out['m_dn_A_log'], 'm_dn_dt_bias': out['m_dn_dt_bias'], 'm_dn_norm_w': out['m_dn_norm_w'], 'm_w_proj_sb': out['m_w_proj_sb'], 'm_w_proj_dn': out['m_w_proj_dn'], 'm_w_out': out['m_w_out'], 'm_norm2_w': out['m_norm2_w'], 'm_w_ffn_in': out['m_w_ffn_in'], 'm_ffn_conv_w': out['m_ffn_conv_w'], 'm_ffn_conv_b': out['m_ffn_conv_b'], 'm_w_ffn_out': out['m_w_ffn_out'], 'm_final_norm_w': out['m_final_norm_w'], 'v_w_ada': out['v_w_ada'], 'v_b_ada': out['v_b_ada'], 'v_norm1_w': out['v_norm1_w'], 'v_w_in': out['v_w_in'], 'v_dn_conv_w': out['v_dn_conv_w'], 'v_dn_A_log': out['v_dn_A_log'], 'v_dn_dt_bias': out['v_dn_dt_bias'], 'v_dn_norm_w': out['v_dn_norm_w'], 'v_w_proj_sb': out['v_w_proj_sb'], 'v_w_proj_dn': out['v_w_proj_dn'], 'v_w_out': out['v_w_out'], 'v_norm2_w': out['v_norm2_w'], 'v_w_ffn_in': out['v_w_ffn_in'], 'v_ffn_conv_w': out['v_ffn_conv_w'], 'v_ffn_conv_b': out['v_ffn_conv_b'], 'v_w_ffn_out': out['v_w_ffn_out'], 'v_final_norm_w': out['v_final_norm_w']}


def _loss(weights, diff, rest, loss_target):
    with _jax.named_scope("forward"):
        args = {**rest, TWIN_DIFF_INPUT: diff, **{k: w.astype(_WEIGHT_DTYPES[k]) for k, w in weights.items()}}
        y = _forward(args)
    with _jax.named_scope("loss_head"):
        err = _jnp.square(y.astype(_jnp.float32) - loss_target)
        return 0.5 * _jnp.sum(_jnp.mean(err, axis=-1)) if err.ndim else 0.5 * err


def _adamw(w, g, m, v):
    m = ADAM_B1 * m + (1.0 - ADAM_B1) * g
    v = ADAM_B2 * v + (1.0 - ADAM_B2) * _jnp.square(g)
    m_hat = m / (1.0 - ADAM_B1 ** ADAM_STEP)
    v_hat = v / (1.0 - ADAM_B2 ** ADAM_STEP)
    delta = -ADAM_LR * (m_hat / (_jnp.sqrt(v_hat) + ADAM_EPS) + ADAM_WD * w)
    return delta, m, v


def reference(x, c, w_ada, b_ada, norm1_w, w_in, dn_conv_w, dn_A_log, dn_dt_bias, dn_norm_w, w_proj_sb, w_proj_dn, w_out, norm2_w, w_ffn_in, ffn_conv_w, ffn_conv_b, w_ffn_out, final_norm_w, loss_target, m_w_ada, m_b_ada, m_norm1_w, m_w_in, m_dn_conv_w, m_dn_A_log, m_dn_dt_bias, m_dn_norm_w, m_w_proj_sb, m_w_proj_dn, m_w_out, m_norm2_w, m_w_ffn_in, m_ffn_conv_w, m_ffn_conv_b, m_w_ffn_out, m_final_norm_w, v_w_ada, v_b_ada, v_norm1_w, v_w_in, v_dn_conv_w, v_dn_A_log, v_dn_dt_bias, v_dn_norm_w, v_w_proj_sb, v_w_proj_dn, v_w_out, v_norm2_w, v_w_ffn_in, v_ffn_conv_w, v_ffn_conv_b, v_w_ffn_out, v_final_norm_w):
    given = dict(x=x, c=c, w_ada=w_ada, b_ada=b_ada, norm1_w=norm1_w, w_in=w_in, dn_conv_w=dn_conv_w, dn_A_log=dn_A_log, dn_dt_bias=dn_dt_bias, dn_norm_w=dn_norm_w, w_proj_sb=w_proj_sb, w_proj_dn=w_proj_dn, w_out=w_out, norm2_w=norm2_w, w_ffn_in=w_ffn_in, ffn_conv_w=ffn_conv_w, ffn_conv_b=ffn_conv_b, w_ffn_out=w_ffn_out, final_norm_w=final_norm_w, loss_target=loss_target, m_w_ada=m_w_ada, m_b_ada=m_b_ada, m_norm1_w=m_norm1_w, m_w_in=m_w_in, m_dn_conv_w=m_dn_conv_w, m_dn_A_log=m_dn_A_log, m_dn_dt_bias=m_dn_dt_bias, m_dn_norm_w=m_dn_norm_w, m_w_proj_sb=m_w_proj_sb, m_w_proj_dn=m_w_proj_dn, m_w_out=m_w_out, m_norm2_w=m_norm2_w, m_w_ffn_in=m_w_ffn_in, m_ffn_conv_w=m_ffn_conv_w, m_ffn_conv_b=m_ffn_conv_b, m_w_ffn_out=m_w_ffn_out, m_final_norm_w=m_final_norm_w, v_w_ada=v_w_ada, v_b_ada=v_b_ada, v_norm1_w=v_norm1_w, v_w_in=v_w_in, v_dn_conv_w=v_dn_conv_w, v_dn_A_log=v_dn_A_log, v_dn_dt_bias=v_dn_dt_bias, v_dn_norm_w=v_dn_norm_w, v_w_proj_sb=v_w_proj_sb, v_w_proj_dn=v_w_proj_dn, v_w_out=v_w_out, v_norm2_w=v_norm2_w, v_w_ffn_in=v_w_ffn_in, v_ffn_conv_w=v_ffn_conv_w, v_ffn_conv_b=v_ffn_conv_b, v_w_ffn_out=v_w_ffn_out, v_final_norm_w=v_final_norm_w)
    weights = {n: given[n] for n in TWIN_WEIGHTS}
    shared = {n: given[n] for n in SHARED_INPUTS}
    per_example = {n: given[n] for n in ['x', 'c']}
    grad_fn = _jax.value_and_grad(_loss, argnums=(0, 1))

    def one_microbatch(ex, loss_target):
        ex = dict(ex)
        diff = ex.pop(TWIN_DIFF_INPUT)
        return grad_fn(weights, diff, {**shared, **ex}, loss_target)

    if N_MICROBATCH == 1:
        loss, (grad_w, grad_x) = one_microbatch(per_example, given["loss_target"])
    else:
        def body(carry, xs):
            loss_sum, grad_sum = carry
            l_k, (gw_k, gx_k) = one_microbatch(xs[0], xs[1])
            with _jax.named_scope("update"):
                return (loss_sum + l_k, _jax.tree.map(_jnp.add, grad_sum, gw_k)), gx_k

        init = (_jnp.zeros((), _jnp.float32), _jax.tree.map(_jnp.zeros_like, weights))
        (loss, grad_w), grad_x = _jax.lax.scan(body, init, (per_example, given["loss_target"]))
    with _jax.named_scope("update"):
        delta_w, new_m, new_v = {}, {}, {}
        for n in TWIN_WEIGHTS:
            delta_w[n], new_m[n], new_v[n] = _adamw(weights[n], grad_w[n], given["m_" + n], given["v_" + n])
    return (loss, grad_x, *[grad_w[n] for n in TWIN_WEIGHTS], *[delta_w[n] for n in TWIN_WEIGHTS],
            *[new_m[n] for n in TWIN_WEIGHTS], *[new_v[n] for n in TWIN_WEIGHTS])
```

```python
import functools

import jax
import jax.numpy as jnp
from jax import lax
from jax.experimental import pallas as pl
from jax.experimental.pallas import tpu as pltpu

f32 = jnp.float32
bf16 = jnp.bfloat16

D_MODEL = 1024
SB_HEADS = 8
SB_HEAD_DIM = 64
SB_WIDTH = SB_HEADS * SB_HEAD_DIM
SB_QBLOCK = 128
DN_HEADS = 8
DN_KEY_DIM = 64
DN_VAL_DIM = 128
DN_QK_WIDTH = DN_HEADS * DN_KEY_DIM
DN_V_WIDTH = DN_HEADS * DN_VAL_DIM
DN_CONV_CH = 2 * DN_QK_WIDTH + DN_V_WIDTH
DN_CONV_WIDTH = 4
DN_CHUNK = 64
D_FF = 2816
FFN_CONV_WIDTH = 3
NORM_EPS = 1e-6
L2_EPS = 1e-6
ADAM_LR = 0.001
ADAM_B1 = 0.9
ADAM_B2 = 0.999
ADAM_EPS = 1e-08
ADAM_WD = 0.01
ADAM_STEP = 10

N_DEV = 8
MESH = pl.DeviceIdType.MESH

LANES = 128
SUBLANES = 8
VMEM_LIMIT = 48 * 1024 * 1024

OFF_GA = 0
OFF_GB = D_MODEL
OFF_DN = 2 * D_MODEL
OFF_Z = OFF_DN + DN_CONV_CH
OFF_SBQ = OFF_Z + DN_V_WIDTH
OFF_SBK = OFF_SBQ + SB_WIDTH
OFF_SBV = OFF_SBK + SB_WIDTH
MAIN_WIDTH = OFF_SBV + SB_WIDTH

TM = 256
TCONV_R = 256
TCONV_C = 256


def _cparams(sem=None):
    return pltpu.CompilerParams(dimension_semantics=sem, vmem_limit_bytes=VMEM_LIMIT)


def _pick(n, cands):
    for c in cands:
        if n % c == 0:
            return c
    return n


def _my_pos():
    return lax.axis_index("x"), lax.axis_index("y"), lax.axis_index("c")


def _flip(v, bit):
    return 1 - v if bit else v


def _all_gather(arrs, name):
    n = len(arrs)

    def body(*refs):
        ins, outs = refs[:n], refs[n:2 * n]
        send_sems, recv_sems, local_sems = refs[2 * n:]
        x, y, c = _my_pos()
        me, sibling = (x, y, c), (x, y, 1 - c)
        chips = [(1 - x, y), (x, 1 - y), (1 - x, 1 - y)]

        def slot(out, pos):
            return out.at[4 * pos[0] + 2 * pos[1] + pos[2]]

        def copy(a, k, block, to, src=None):
            return pltpu.make_async_remote_copy(
                src_ref=slot(outs[a], block) if src is None else src, dst_ref=slot(outs[a], block),
                send_sem=send_sems.at[a, k], recv_sem=recv_sems.at[a, k], device_id=to, device_id_type=MESH)

        started = []
        mine = []
        for a in range(n):
            cp = pltpu.make_async_copy(ins[a], slot(outs[a], me), local_sems.at[a])
            cp.start()
            mine.append(cp)
            first = [copy(a, 0, me, sibling, src=ins[a])]
            first += [copy(a, 1 + j, me, (*chip, c), src=ins[a]) for j, chip in enumerate(chips)]
            for cp in first:
                cp.start()
            started += first
        for a in range(n):
            for j, chip in enumerate(chips):
                copy(a, 1 + j, (*chip, c), me).wait_recv()
                fwd = copy(a, 4 + j, (*chip, c), sibling)
                fwd.start()
                started.append(fwd)
        for a in range(n):
            copy(a, 0, sibling, me).wait_recv()
            for j, chip in enumerate(chips):
                copy(a, 4 + j, (*chip, 1 - c), me).wait_recv()
        for cp in started:
            cp.wait_send()
        for cp in mine:
            cp.wait()

    any_spec = pl.BlockSpec(memory_space=pl.ANY)
    return pl.pallas_call(
        body, name=name,
        out_shape=[jax.ShapeDtypeStruct((N_DEV,) + a.shape, a.dtype) for a in arrs],
        in_specs=[any_spec] * n, out_specs=[any_spec] * n,
        scratch_shapes=[pltpu.SemaphoreType.DMA((n, 7)), pltpu.SemaphoreType.DMA((n, 7)),
                        pltpu.SemaphoreType.DMA((n,))],
    )(*arrs)


def _all_to_all(arrs, name):
    n = len(arrs)

    def body(*refs):
        ins, outs = refs[:n], refs[n:2 * n]
        send_sems, recv_sems, local_sems = refs[2 * n:]
        x, y, c = _my_pos()
        me_idx = 4 * x + 2 * y + c
        copies = []
        for a in range(n):
            cp = pltpu.make_async_copy(ins[a].at[me_idx], outs[a].at[me_idx], local_sems.at[a])
            cp.start()
            copies.append(cp)
        rdmas = []
        for a in range(n):
            for m in range(1, N_DEV):
                peer = (_flip(x, m & 4), _flip(y, m & 2), _flip(c, m & 1))
                peer_idx = 4 * peer[0] + 2 * peer[1] + peer[2]
                send = pltpu.make_async_remote_copy(
                    src_ref=ins[a].at[peer_idx], dst_ref=outs[a].at[me_idx],
                    send_sem=send_sems.at[a, m - 1], recv_sem=recv_sems.at[a, m - 1],
                    device_id=peer, device_id_type=MESH)
                send.start()
                recv = pltpu.make_async_remote_copy(
                    src_ref=ins[a].at[peer_idx], dst_ref=outs[a].at[peer_idx],
                    send_sem=send_sems.at[a, m - 1], recv_sem=recv_sems.at[a, m - 1],
                    device_id=peer, device_id_type=MESH)
                rdmas.append((send, recv))
        for send, recv in rdmas:
            recv.wait_recv()
        for send, recv in rdmas:
            send.wait_send()
        for cp in copies:
            cp.wait()

    any_spec = pl.BlockSpec(memory_space=pl.ANY)
    return pl.pallas_call(
        body, name=name,
        out_shape=[jax.ShapeDtypeStruct(a.shape, a.dtype) for a in arrs],
        in_specs=[any_spec] * n, out_specs=[any_spec] * n,
        scratch_shapes=[pltpu.SemaphoreType.DMA((n, 7)), pltpu.SemaphoreType.DMA((n, 7)),
                        pltpu.SemaphoreType.DMA((n,))],
    )(*arrs)


def _mm(a, b, *, ta=False, tb=False, out_dtype=f32, name):
    (k_dim, m_dim) = a.shape if ta else a.shape[::-1]
    (n_dim, kb_dim) = b.shape if tb else b.shape[::-1]
    assert k_dim == kb_dim, (a.shape, b.shape, ta, tb)
    tm = _pick(m_dim, (512, 256, 128))
    tn = _pick(n_dim, (512, 256, 128))
    tk = _pick(k_dim, (1024, 512, 256, 128))
    nk = k_dim // tk
    dims = (((0 if ta else 1,), (1 if tb else 0,)), ((), ()))

    def body(a_ref, b_ref, o_ref, acc_ref):
        k = pl.program_id(2)

        @pl.when(k == 0)
        def _():
            acc_ref[...] = jnp.zeros_like(acc_ref)

        acc_ref[...] += lax.dot_general(a_ref[...].astype(bf16), b_ref[...].astype(bf16), dims,
                                        preferred_element_type=f32)

        @pl.when(k == nk - 1)
        def _():
            o_ref[...] = acc_ref[...].astype(out_dtype)

    a_spec = pl.BlockSpec((tk, tm), lambda i, j, k: (k, i)) if ta else pl.BlockSpec((tm, tk), lambda i, j, k: (i, k))
    b_spec = pl.BlockSpec((tn, tk), lambda i, j, k: (j, k)) if tb else pl.BlockSpec((tk, tn), lambda i, j, k: (k, j))
    return pl.pallas_call(
        body, name=name, grid=(m_dim // tm, n_dim // tn, nk),
        in_specs=[a_spec, b_spec], out_specs=pl.BlockSpec((tm, tn), lambda i, j, k: (i, j)),
        out_shape=jax.ShapeDtypeStruct((m_dim, n_dim), out_dtype),
        scratch_shapes=[pltpu.VMEM((tm, tn), f32)],
        compiler_params=_cparams(("parallel", "parallel", "arbitrary")),
    )(a, b)


def _win(t):
    return t if isinstance(t, tuple) else (t, t.shape[1], 0)


def _tile_spec(width, cb, tm):
    return pl.BlockSpec((tm, width), lambda i: (i, cb))


def _param_spec(p):
    return pl.BlockSpec(p.shape, lambda i: (0, 0))


def _stage_fwd(f, params, tiles, out_dtypes, name):
    tiles = [_win(t) for t in tiles]
    rows = tiles[0][0].shape[0]
    tm = min(TM, rows)
    avals = jax.eval_shape(f, *[jax.ShapeDtypeStruct(p.shape, f32) for p in params],
                           *[jax.ShapeDtypeStruct((tm, w), f32) for _, w, _ in tiles])
    n_p, n_t = len(params), len(tiles)

    def body(*refs):
        p = [r[...] for r in refs[:n_p]]
        t = [r[...].astype(f32) for r in refs[n_p:n_p + n_t]]
        for o_ref, val in zip(refs[n_p + n_t:], f(*p, *t)):
            o_ref[...] = val.astype(o_ref.dtype)

    return pl.pallas_call(
        body, name=name, grid=(rows // tm,),
        in_specs=[_param_spec(p) for p in params] + [_tile_spec(w, cb, tm) for _, w, cb in tiles],
        out_specs=[_tile_spec(a.shape[1], 0, tm) for a in avals],
        out_shape=[jax.ShapeDtypeStruct((rows, a.shape[1]), dt) for a, dt in zip(avals, out_dtypes)],
        compiler_params=_cparams(("parallel",)),
    )(*params, *[t[0] for t in tiles])


def _stage_bwd(f, params, tiles, cts, grad_dtypes, name, residual=None):
    tiles = [_win(t) for t in tiles]
    rows = tiles[0][0].shape[0]
    tm = min(TM, rows)
    cts = [list(g) if isinstance(g, (list, tuple)) else [g] for g in cts]
    flat_cts = [a for g in cts for a in g]
    n_p, n_t, n_c = len(params), len(tiles), len(flat_cts)
    has_res = residual is not None
    want = [j for j, dt in enumerate(grad_dtypes) if dt is not None]

    def body(*refs):
        i = pl.program_id(0)
        p = [r[...] for r in refs[:n_p]]
        t = [r[...].astype(f32) for r in refs[n_p:n_p + n_t]]
        ct_vals = [r[...].astype(f32) for r in refs[n_p + n_t:n_p + n_t + n_c]]
        ct, at = [], 0
        for g in cts:
            ct.append(functools.reduce(jnp.add, ct_vals[at:at + len(g)]))
            at += len(g)
        ct = tuple(ct)
        pos = n_p + n_t + n_c
        res_ref = refs[pos] if has_res else None
        pos += 1 if has_res else 0
        dp_refs = refs[pos:pos + n_p]
        dt_refs = refs[pos + n_p:]
        _, vjp = jax.vjp(f, *p, *t)
        grads = vjp(ct)

        @pl.when(i == 0)
        def _():
            for r in dp_refs:
                r[...] = jnp.zeros_like(r)

        for r, g in zip(dp_refs, grads[:n_p]):
            r[...] += g
        for r, j in zip(dt_refs, want):
            g = grads[n_p + j]
            if has_res and j == residual[0]:
                g = g + res_ref[...].astype(f32)
            r[...] = g.astype(r.dtype)

    in_arrays = list(params) + [t[0] for t in tiles] + flat_cts
    in_specs = ([_param_spec(p) for p in params] + [_tile_spec(w, cb, tm) for _, w, cb in tiles]
                + [_tile_spec(c.shape[1], 0, tm) for c in flat_cts])
    if has_res:
        in_arrays.append(residual[1])
        in_specs.append(_tile_spec(residual[1].shape[1], 0, tm))
    out_shape = ([jax.ShapeDtypeStruct(p.shape, f32) for p in params]
                 + [jax.ShapeDtypeStruct((rows, tiles[j][1]), grad_dtypes[j]) for j in want])
    out_specs = [_param_spec(p) for p in params] + [_tile_spec(tiles[j][1], 0, tm) for j in want]
    outs = pl.pallas_call(
        body, name=name, grid=(rows // tm,), in_specs=in_specs, out_specs=out_specs, out_shape=out_shape,
        compiler_params=_cparams(("arbitrary",)),
    )(*in_arrays)
    return outs[:n_p], outs[n_p:]


def _rms(x, w):
    return x * lax.rsqrt(jnp.mean(x * x, axis=-1, keepdims=True) + NORM_EPS) * w


def _f_normmod(w, shift, scale, x):
    return (_rms(x, w) * (1.0 + scale) + shift,)


def _f_merge(ga, gb, pa, pb):
    return (jax.nn.sigmoid(ga) * pa + jax.nn.sigmoid(gb) * pb,)


def _f_residual(gate, x, branch):
    return (x + gate * branch,)


def _f_loss(gate, wf, x1, fo, target):
    y = _rms(x1 + gate * fo, wf)
    err = jnp.square(y - target)
    return (0.5 * jnp.sum(jnp.mean(err, axis=-1, keepdims=True), axis=0, keepdims=True),)


def _loss_and_grads(gate2, wf, x1, fo, target):
    rows, d = x1.shape
    tm = min(TM, rows)

    def body(g_ref, w_ref, x_ref, fo_ref, t_ref, loss_ref, dg_ref, dw_ref, dx_ref, dfo_ref):
        i = pl.program_id(0)
        (val,), vjp = jax.vjp(_f_loss, g_ref[...], w_ref[...], x_ref[...], fo_ref[...], t_ref[...])
        dg, dw, dx, dfo, _ = vjp((jnp.ones((1, 1), f32),))

        @pl.when(i == 0)
        def _():
            loss_ref[...] = jnp.zeros_like(loss_ref)
            dg_ref[...] = jnp.zeros_like(dg_ref)
            dw_ref[...] = jnp.zeros_like(dw_ref)

        loss_ref[...] += jnp.broadcast_to(val, loss_ref.shape)
        dg_ref[...] += dg
        dw_ref[...] += dw
        dx_ref[...] = dx
        dfo_ref[...] = dfo.astype(bf16)

    vec = pl.BlockSpec((1, d), lambda i: (0, 0))
    tile = pl.BlockSpec((tm, d), lambda i: (i, 0))
    return pl.pallas_call(
        body, name="loss_fwd_bwd", grid=(rows // tm,),
        in_specs=[vec, vec, tile, tile, tile],
        out_specs=[pl.BlockSpec((1, LANES), lambda i: (0, 0)), vec, vec, tile, tile],
        out_shape=[jax.ShapeDtypeStruct((1, LANES), f32), jax.ShapeDtypeStruct((1, d), f32),
                   jax.ShapeDtypeStruct((1, d), f32), jax.ShapeDtypeStruct((rows, d), f32),
                   jax.ShapeDtypeStruct((rows, d), bf16)],
        compiler_params=_cparams(("arbitrary",)),
    )(gate2, wf, x1, fo, target)


def _softplus(z):
    return jnp.maximum(z, 0.0) + jnp.log(1.0 + jnp.exp(-jnp.abs(z)))


def _split_dot(a, m):
    hi = a.astype(bf16)
    lo = (a - hi.astype(f32)).astype(bf16)
    return jnp.dot(hi, m, preferred_element_type=f32) + jnp.dot(lo, m, preferred_element_type=f32)


def _sb_tile(qh, k, t_pos, s_pos, run, suffix_m):
    z = lax.dot_general(qh, k, (((1,), (1,)), ((), ())), preferred_element_type=f32)
    causal = s_pos < t_pos
    l = jnp.where(causal, -_softplus(z), 0.0)
    stick = _split_dot(l, suffix_m) + run
    a = jnp.where(causal, jnp.exp(z + l + stick), 0.0)
    return z, l, causal, a


def _suffix_matrix(n):
    r = lax.broadcasted_iota(jnp.int32, (n, n), 0)
    c = lax.broadcasted_iota(jnp.int32, (n, n), 1)
    return (r > c).astype(bf16)


def _prefix_matrix(n):
    r = lax.broadcasted_iota(jnp.int32, (n, n), 0)
    c = lax.broadcasted_iota(jnp.int32, (n, n), 1)
    return (r < c).astype(bf16)


def _head_masks():
    lane = lax.broadcasted_iota(jnp.int32, (1, LANES), 1)
    return [(lane < SB_HEAD_DIM).astype(f32), (lane >= SB_HEAD_DIM).astype(f32)]


def _sb_attention_fwd(proj):
    rows = proj.shape[0]
    bq = SB_QBLOCK
    nq = rows // bq
    assert nq <= LANES, "one lane per key block"
    npair = SB_WIDTH // LANES
    scale = SB_HEAD_DIM ** -0.5

    def body(q_ref, k_ref, v_ref, o_ref, runs_ref):
        qi = pl.program_id(1)
        q = q_ref[...]
        suffix_m = _suffix_matrix(bq)
        t_pos = qi * bq + lax.broadcasted_iota(jnp.int32, (bq, 1), 0)
        lane = lax.broadcasted_iota(jnp.int32, (1, LANES), 1)
        out = jnp.zeros((bq, LANES), f32)
        for hh, mask in enumerate(_head_masks()):
            qh = (q * mask * scale).astype(bf16)

            cols = slice(hh * LANES, (hh + 1) * LANES)
            runs_ref[:, cols] = jnp.zeros((bq, LANES), f32)

            def step(i, carry, qh=qh, cols=cols):
                run, acc = carry
                kb = qi - i
                r0 = pl.multiple_of(kb * bq, bq)
                k = k_ref[pl.ds(r0, bq), :].astype(bf16)
                v = v_ref[pl.ds(r0, bq), :].astype(bf16)
                s_pos = kb * bq + lax.broadcasted_iota(jnp.int32, (1, bq), 1)
                _, l, _, a = _sb_tile(qh, k, t_pos, s_pos, run, suffix_m)
                acc = acc + jnp.dot(a.astype(bf16), v, preferred_element_type=f32)
                runs_ref[:, cols] = jnp.where(lane == kb, run, runs_ref[:, cols])
                return run + jnp.sum(l, axis=1, keepdims=True), acc

            _, acc = lax.fori_loop(0, qi + 1, step, (jnp.zeros((bq, 1), f32), jnp.zeros((bq, LANES), f32)))
            out = out + acc * mask
        o_ref[...] = out

    return pl.pallas_call(
        body, name="sb_attn_fwd", grid=(npair, nq),
        in_specs=[pl.BlockSpec((bq, LANES), lambda p, i: (i, OFF_SBQ // LANES + p)),
                  pl.BlockSpec((rows, LANES), lambda p, i: (0, OFF_SBK // LANES + p)),
                  pl.BlockSpec((rows, LANES), lambda p, i: (0, OFF_SBV // LANES + p))],
        out_specs=[pl.BlockSpec((bq, LANES), lambda p, i: (i, p)),
                   pl.BlockSpec((bq, 2 * LANES), lambda p, i: (i, p))],
        out_shape=[jax.ShapeDtypeStruct((rows, SB_WIDTH), f32),
                   jax.ShapeDtypeStruct((rows, SB_HEADS * LANES), f32)],
        compiler_params=_cparams(("parallel", "arbitrary")),
    )(proj, proj, proj)


def _sb_attention_bwd(proj, runs, do):
    rows = proj.shape[0]
    bq = SB_QBLOCK
    nq = rows // bq
    npair = SB_WIDTH // LANES
    scale = SB_HEAD_DIM ** -0.5

    def body(q_ref, k_ref, v_ref, runs_ref, do_ref, dq_ref, dk_ref, dv_ref):
        qi = pl.program_id(1)

        @pl.when(qi == 0)
        def _():
            dk_ref[...] = jnp.zeros_like(dk_ref)
            dv_ref[...] = jnp.zeros_like(dv_ref)

        q = q_ref[...]
        do_blk = do_ref[...]
        qb = q.astype(bf16)
        dob = do_blk.astype(bf16)
        suffix_m = _suffix_matrix(bq)
        prefix_m = _prefix_matrix(bq)
        t_pos = qi * bq + lax.broadcasted_iota(jnp.int32, (bq, 1), 0)
        lane = lax.broadcasted_iota(jnp.int32, (1, LANES), 1)
        dq_out = jnp.zeros((bq, LANES), f32)
        for hh, mask in enumerate(_head_masks()):
            qh = (q * mask * scale).astype(bf16)
            doh = (do_blk * mask).astype(bf16)
            runs = runs_ref[:, hh * LANES:(hh + 1) * LANES]

            def step(kb, carry, qh=qh, doh=doh, mask=mask, runs=runs):
                pref, dq_acc = carry
                r0 = pl.multiple_of(kb * bq, bq)
                k = k_ref[pl.ds(r0, bq), :].astype(bf16)
                v = v_ref[pl.ds(r0, bq), :].astype(bf16)
                s_pos = kb * bq + lax.broadcasted_iota(jnp.int32, (1, bq), 1)
                run = jnp.sum(jnp.where(lane == kb, runs, 0.0), axis=1, keepdims=True)
                z, l, causal, a = _sb_tile(qh, k, t_pos, s_pos, run, suffix_m)
                da = lax.dot_general(doh, v, (((1,), (1,)), ((), ())), preferred_element_type=f32)
                p = da * a
                prefix_excl = _split_dot(p, prefix_m) + pref
                beta = jnp.exp(z + l)
                dz = jnp.where(causal, p * jnp.exp(l) - beta * prefix_excl, 0.0) * scale
                dzb = dz.astype(bf16)
                dq_acc = dq_acc + jnp.dot(dzb, k, preferred_element_type=f32)
                tn = (((0,), (0,)), ((), ()))
                dk_ref[pl.ds(r0, bq), :] += lax.dot_general(dzb, qb, tn, preferred_element_type=f32) * mask
                dv_ref[pl.ds(r0, bq), :] += lax.dot_general(a.astype(bf16), dob, tn, preferred_element_type=f32) * mask
                return pref + jnp.sum(p, axis=1, keepdims=True), dq_acc

            _, dq_acc = lax.fori_loop(0, qi + 1, step, (jnp.zeros((bq, 1), f32), jnp.zeros((bq, LANES), f32)))
            dq_out = dq_out + dq_acc * mask
        dq_ref[...] = dq_out.astype(dq_ref.dtype)

    blk = pl.BlockSpec((bq, LANES), lambda p, i: (i, p))
    full = pl.BlockSpec((rows, LANES), lambda p, i: (0, p))
    return pl.pallas_call(
        body, name="sb_attn_bwd", grid=(npair, nq),
        in_specs=[pl.BlockSpec((bq, LANES), lambda p, i: (i, OFF_SBQ // LANES + p)),
                  pl.BlockSpec((rows, LANES), lambda p, i: (0, OFF_SBK // LANES + p)),
                  pl.BlockSpec((rows, LANES), lambda p, i: (0, OFF_SBV // LANES + p)),
                  pl.BlockSpec((bq, 2 * LANES), lambda p, i: (i, p)), blk],
        out_specs=[blk, full, full],
        out_shape=[jax.ShapeDtypeStruct((rows, SB_WIDTH), bf16), jax.ShapeDtypeStruct((rows, SB_WIDTH), f32),
                   jax.ShapeDtypeStruct((rows, SB_WIDTH), f32)],
        compiler_params=_cparams(("parallel", "arbitrary")),
    )(proj, proj, proj, runs, do)


def _shift_down(x, prev8, j):
    if j == 0:
        return x
    r = pltpu.roll(x, j, axis=0)
    row8 = lax.broadcasted_iota(jnp.int32, prev8.shape, 0)
    head = jnp.where(row8 < j, pltpu.roll(prev8, j, axis=0), r[0:SUBLANES])
    return jnp.concatenate([head, r[SUBLANES:]], axis=0)


def _shift_up(x, next8, j):
    if j == 0:
        return x
    n = x.shape[0]
    r = pltpu.roll(x, n - j, axis=0)
    row8 = lax.broadcasted_iota(jnp.int32, next8.shape, 0)
    tail = jnp.where(row8 >= SUBLANES - j, pltpu.roll(next8, SUBLANES - j, axis=0), r[n - SUBLANES:n])
    return jnp.concatenate([r[:n - SUBLANES], tail], axis=0)


def _conv(x, prev8, w):
    k_taps = w.shape[0]
    out = x * w[k_taps - 1:k_taps, :]
    for j in range(1, k_taps):
        out = out + _shift_down(x, prev8, j) * w[k_taps - 1 - j:k_taps - j, :]
    return out


def _conv_tiles(rows):
    tr = min(TCONV_R, rows)
    return tr, rows // tr, tr // SUBLANES


def _prev_spec(tc, cb0, r8):
    return pl.BlockSpec((SUBLANES, tc), lambda j, i: (jnp.maximum(i * r8 - 1, 0), cb0 + j))


def _silu(x):
    return x * jax.nn.sigmoid(x)


def _dsilu(x):
    s = jax.nn.sigmoid(x)
    return s * (1.0 + x * (1.0 - s))


def _dn_conv_fwd(proj, w):
    rows = proj.shape[0]
    tr, nr, r8 = _conv_tiles(rows)
    tc = TCONV_C
    cb0 = OFF_DN // tc

    def body(x_ref, p_ref, w_ref, o_ref):
        prev = jnp.where(pl.program_id(1) == 0, 0.0, p_ref[...])
        o_ref[...] = _silu(_conv(x_ref[...], prev, w_ref[...]))

    return pl.pallas_call(
        body, name="dn_conv_fwd", grid=(DN_CONV_CH // tc, nr),
        in_specs=[pl.BlockSpec((tr, tc), lambda j, i: (i, cb0 + j)), _prev_spec(tc, cb0, r8),
                  pl.BlockSpec((DN_CONV_WIDTH, tc), lambda j, i: (0, j))],
        out_specs=pl.BlockSpec((tr, tc), lambda j, i: (i, j)),
        out_shape=jax.ShapeDtypeStruct((rows, DN_CONV_CH), f32),
        compiler_params=_cparams(("parallel", "parallel")),
    )(proj, proj, w)


def _dn_conv_bwd_act(proj, w, dact):
    rows = proj.shape[0]
    tr, nr, r8 = _conv_tiles(rows)
    tc = TCONV_C
    cb0 = OFF_DN // tc

    def body(x_ref, p_ref, w_ref, d_ref, o_ref):
        prev = jnp.where(pl.program_id(1) == 0, 0.0, p_ref[...])
        o_ref[...] = d_ref[...] * _dsilu(_conv(x_ref[...], prev, w_ref[...]))

    return pl.pallas_call(
        body, name="dn_conv_bwd_act", grid=(DN_CONV_CH // tc, nr),
        in_specs=[pl.BlockSpec((tr, tc), lambda j, i: (i, cb0 + j)), _prev_spec(tc, cb0, r8),
                  pl.BlockSpec((DN_CONV_WIDTH, tc), lambda j, i: (0, j)),
                  pl.BlockSpec((tr, tc), lambda j, i: (i, j))],
        out_specs=pl.BlockSpec((tr, tc), lambda j, i: (i, j)),
        out_shape=jax.ShapeDtypeStruct((rows, DN_CONV_CH), f32),
        compiler_params=_cparams(("parallel", "parallel")),
    )(proj, proj, w, dact)


def _ffn_conv_fwd(u_pre, w, b):
    rows = u_pre.shape[0]
    tr, nr, r8 = _conv_tiles(rows)
    tc = TCONV_C
    nct = D_FF // tc

    def body(xg_ref, pg_ref, xu_ref, pu_ref, wg_ref, wu_ref, bg_ref, bu_ref, o_ref):
        first = pl.program_id(1) == 0
        ug = _conv(xg_ref[...], jnp.where(first, 0.0, pg_ref[...]), wg_ref[...]) + bg_ref[...]
        uu = _conv(xu_ref[...], jnp.where(first, 0.0, pu_ref[...]), wu_ref[...]) + bu_ref[...]
        o_ref[...] = (_silu(ug) * uu).astype(o_ref.dtype)

    def x_spec(off):
        return pl.BlockSpec((tr, tc), lambda j, i: (i, off + j))

    def w_spec(k, off):
        return pl.BlockSpec((k, tc), lambda j, i: (0, off + j))

    return pl.pallas_call(
        body, name="ffn_conv_fwd", grid=(nct, nr),
        in_specs=[x_spec(0), _prev_spec(tc, 0, r8), x_spec(nct), _prev_spec(tc, nct, r8),
                  w_spec(FFN_CONV_WIDTH, 0), w_spec(FFN_CONV_WIDTH, nct), w_spec(1, 0), w_spec(1, nct)],
        out_specs=pl.BlockSpec((tr, tc), lambda j, i: (i, j)),
        out_shape=jax.ShapeDtypeStruct((rows, D_FF), bf16),
        compiler_params=_cparams(("parallel", "parallel")),
    )(u_pre, u_pre, u_pre, u_pre, w, w, b, b)


def _ffn_conv_bwd_act(u_pre, w, b, dact):
    rows = u_pre.shape[0]
    tr, nr, r8 = _conv_tiles(rows)
    tc = TCONV_C
    nct = D_FF // tc

    def body(xg_ref, pg_ref, xu_ref, pu_ref, wg_ref, wu_ref, bg_ref, bu_ref, d_ref,
             dug_ref, duu_ref, dbg_ref, dbu_ref):
        i = pl.program_id(1)
        first = i == 0
        ug = _conv(xg_ref[...], jnp.where(first, 0.0, pg_ref[...]), wg_ref[...]) + bg_ref[...]
        uu = _conv(xu_ref[...], jnp.where(first, 0.0, pu_ref[...]), wu_ref[...]) + bu_ref[...]
        d = d_ref[...]
        dug = d * uu * _dsilu(ug)
        duu = d * _silu(ug)
        dug_ref[...] = dug
        duu_ref[...] = duu

        @pl.when(first)
        def _():
            dbg_ref[...] = jnp.zeros_like(dbg_ref)
            dbu_ref[...] = jnp.zeros_like(dbu_ref)

        dbg_ref[...] += jnp.sum(dug, axis=0, keepdims=True)
        dbu_ref[...] += jnp.sum(duu, axis=0, keepdims=True)

    def x_spec(off):
        return pl.BlockSpec((tr, tc), lambda j, i: (i, off + j))

    def w_spec(k, off):
        return pl.BlockSpec((k, tc), lambda j, i: (0, off + j))

    tile = pl.BlockSpec((tr, tc), lambda j, i: (i, j))
    vec = pl.BlockSpec((1, tc), lambda j, i: (0, j))
    dug, duu, dbg, dbu = pl.pallas_call(
        body, name="ffn_conv_bwd_act", grid=(nct, nr),
        in_specs=[x_spec(0), _prev_spec(tc, 0, r8), x_spec(nct), _prev_spec(tc, nct, r8),
                  w_spec(FFN_CONV_WIDTH, 0), w_spec(FFN_CONV_WIDTH, nct), w_spec(1, 0), w_spec(1, nct), tile],
        out_specs=[tile, tile, vec, vec],
        out_shape=[jax.ShapeDtypeStruct((rows, D_FF), f32), jax.ShapeDtypeStruct((rows, D_FF), f32),
                   jax.ShapeDtypeStruct((1, D_FF), f32), jax.ShapeDtypeStruct((1, D_FF), f32)],
        compiler_params=_cparams(("parallel", "arbitrary")),
    )(u_pre, u_pre, u_pre, u_pre, w, w, b, b, dact)
    return dug, duu, dbg, dbu


def _conv_bwd(dy, x, x_cb0, w, name):
    rows, ch = dy.shape
    k_taps = w.shape[0]
    tr, nr, r8 = _conv_tiles(rows)
    tc = TCONV_C
    last8 = rows // SUBLANES - 1

    def body(dy_ref, nx_ref, x_ref, p_ref, w_ref, dx_ref, dw_ref):
        i = pl.program_id(1)
        dyv = dy_ref[...]
        nxt = jnp.where(i == nr - 1, 0.0, nx_ref[...])
        prev = jnp.where(i == 0, 0.0, p_ref[...])
        xv = x_ref[...].astype(f32)
        wv = w_ref[...]

        @pl.when(i == 0)
        def _():
            dw_ref[...] = jnp.zeros_like(dw_ref)

        dx = dyv * wv[k_taps - 1:k_taps, :]
        dw_ref[k_taps - 1:k_taps, :] += jnp.sum(dyv * xv, axis=0, keepdims=True)
        for j in range(1, k_taps):
            dx = dx + _shift_up(dyv, nxt, j) * wv[k_taps - 1 - j:k_taps - j, :]
            dw_ref[k_taps - 1 - j:k_taps - j, :] += jnp.sum(dyv * _shift_down(xv, prev, j), axis=0, keepdims=True)
        dx_ref[...] = dx.astype(dx_ref.dtype)

    tile = pl.BlockSpec((tr, tc), lambda j, i: (i, j))
    return pl.pallas_call(
        body, name=name, grid=(ch // tc, nr),
        in_specs=[tile,
                  pl.BlockSpec((SUBLANES, tc), lambda j, i: (jnp.minimum((i + 1) * r8, last8), j)),
                  pl.BlockSpec((tr, tc), lambda j, i: (i, x_cb0 + j)), _prev_spec(tc, x_cb0, r8),
                  pl.BlockSpec((k_taps, tc), lambda j, i: (0, j))],
        out_specs=[tile, pl.BlockSpec((k_taps, tc), lambda j, i: (0, j))],
        out_shape=[jax.ShapeDtypeStruct((rows, ch), bf16), jax.ShapeDtypeStruct((k_taps, ch), f32)],
        compiler_params=_cparams(("parallel", "arbitrary")),
    )(dy, dy, x, x, w)


def _hdot(a, b):
    return jnp.dot(a, b, preferred_element_type=f32, precision=lax.Precision.HIGHEST)


def _bdot(a, b):
    return jnp.dot(a.astype(bf16), b.astype(bf16), preferred_element_type=f32)


def _bdot_nt(a, b):
    return lax.dot_general(a.astype(bf16), b.astype(bf16), (((1,), (1,)), ((), ())), preferred_element_type=f32)


def _bdot_tn(a, b):
    return lax.dot_general(a.astype(bf16), b.astype(bf16), (((0,), (0,)), ((), ())), preferred_element_type=f32)


def _unit_lower_inverse(low, eye, blk):
    d = jnp.where(blk, low, 0.0)
    e = low - d
    d2 = _hdot(d, d)
    d4 = _hdot(d2, d2)
    d8 = _hdot(d4, d4)
    dinv = _hdot(_hdot(_hdot(eye - d, eye + d2), eye + d4), eye + d8)
    n = _hdot(dinv, e)
    ninv = _hdot(eye - n, eye + _hdot(n, n))
    return _hdot(ninv, dinv)


def _gdn_pair(a_log, dt_bias, norm_w, qp, kp, v0, v1, z0, z1, ba, state, *, pair):
    ch = DN_CHUNK
    r = lax.broadcasted_iota(jnp.int32, (ch, ch), 0)
    c = lax.broadcasted_iota(jnp.int32, (ch, ch), 1)
    incl = r >= c
    strict = r > c
    eye = (r == c).astype(f32)
    blk = jnp.right_shift(r, 4) == jnp.right_shift(c, 4)
    ones = jnp.ones((ch, ch), f32)
    lane = lax.broadcasted_iota(jnp.int32, (1, LANES), 1)
    row_c = lax.broadcasted_iota(jnp.int32, (ch, 1), 0)
    row_s = lax.broadcasted_iota(jnp.int32, (LANES, 1), 0)
    sb = state.astype(bf16)
    outs = []
    new_state = jnp.zeros_like(state)
    for hh, (v, z) in enumerate(((v0, z0), (v1, z1))):
        head = 2 * pair + hh
        m = (jnp.right_shift(lane, 6) == hh).astype(f32)
        pick = lambda arr, idx: jnp.sum(jnp.where(lane == idx, arr, 0.0), axis=1, keepdims=True)
        beta = jax.nn.sigmoid(pick(ba, head))
        g = -jnp.exp(pick(a_log, head)) * _softplus(pick(ba, DN_HEADS + head) + pick(dt_bias, head))
        gc = _hdot(incl.astype(f32), g)
        gr = _hdot(ones, eye * gc)
        decay = jnp.where(incl, jnp.exp(jnp.where(incl, gc - gr, 0.0)), 0.0)
        q = qp * m
        k = kp * m
        qn = q * lax.rsqrt(jnp.sum(q * q, axis=1, keepdims=True) + L2_EPS) * (DN_KEY_DIM ** -0.5)
        kn = k * lax.rsqrt(jnp.sum(k * k, axis=1, keepdims=True) + L2_EPS)
        kb = kn * beta
        low = jnp.where(strict, _bdot_nt(kb, kn) * decay, 0.0)
        t = _unit_lower_inverse(low, eye, blk)
        u = _bdot(t, v * beta)
        w = _bdot(t, kb * jnp.exp(gc))
        intra = jnp.where(incl, _bdot_nt(qn, kn) * decay, 0.0)
        v_new = u - jnp.dot(w.astype(bf16), sb, preferred_element_type=f32)
        o = jnp.dot((qn * jnp.exp(gc)).astype(bf16), sb, preferred_element_type=f32) + _bdot(intra, v_new)
        g_last = jnp.sum(jnp.where(row_c == ch - 1, gc, 0.0), axis=0, keepdims=True)
        k_dec = kn * jnp.exp(g_last - gc)
        rows_h = (jnp.right_shift(row_s, 6) == hh).astype(f32)
        new_state = new_state + state * (rows_h * jnp.exp(g_last)) + _bdot_tn(k_dec, v_new)
        o_n = o * lax.rsqrt(jnp.mean(o * o, axis=1, keepdims=True) + NORM_EPS) * norm_w
        outs.append(o_n * _silu(z))
    return outs[0], outs[1], new_state


def _gdn_specs(rows, reverse):
    n = rows // DN_CHUNK
    idx = (lambda i: n - 1 - i) if reverse else (lambda i: i)
    vec = pl.BlockSpec((1, LANES), lambda i: (0, 0))
    qkv = pl.BlockSpec((DN_CHUNK, DN_CONV_CH), lambda i: (idx(i), 0))
    z = pl.BlockSpec((DN_CHUNK, DN_V_WIDTH), lambda i: (idx(i), OFF_Z // DN_V_WIDTH))
    ba = pl.BlockSpec((DN_CHUNK, LANES), lambda i: (idx(i), 0))
    wide = pl.BlockSpec((DN_CHUNK, DN_V_WIDTH), lambda i: (idx(i), 0))
    st = pl.BlockSpec((1, DN_HEADS * DN_KEY_DIM, LANES), lambda i: (idx(i), 0, 0))
    return n, vec, qkv, z, ba, wide, st


def _gdn_slices(p):
    q = slice(p * LANES, (p + 1) * LANES)
    k = slice(DN_QK_WIDTH + p * LANES, DN_QK_WIDTH + (p + 1) * LANES)
    v0 = slice(2 * DN_QK_WIDTH + 2 * p * LANES, 2 * DN_QK_WIDTH + (2 * p + 1) * LANES)
    v1 = slice(2 * DN_QK_WIDTH + (2 * p + 1) * LANES, 2 * DN_QK_WIDTH + (2 * p + 2) * LANES)
    h0 = slice(2 * p * LANES, (2 * p + 1) * LANES)
    h1 = slice((2 * p + 1) * LANES, (2 * p + 2) * LANES)
    return q, k, v0, v1, h0, h1


def _gdn_fwd(a_log, dt_bias, norm_w, qkv_act, proj, ba):
    rows = qkv_act.shape[0]
    n, vec, qkv_s, z_s, ba_s, wide, st_s = _gdn_specs(rows, False)
    npair = DN_HEADS // 2

    def body(al_ref, dt_ref, nw_ref, qkv_ref, z_ref, ba_ref, o_ref, st_ref, state):
        @pl.when(pl.program_id(0) == 0)
        def _():
            state[...] = jnp.zeros_like(state)

        for p in range(npair):
            q, k, v0, v1, h0, h1 = _gdn_slices(p)
            rs = slice(p * LANES, (p + 1) * LANES)
            s_in = state[rs, :]
            st_ref[0, rs, :] = s_in
            o0, o1, s_out = _gdn_pair(al_ref[...], dt_ref[...], nw_ref[...], qkv_ref[:, q], qkv_ref[:, k],
                                      qkv_ref[:, v0], qkv_ref[:, v1], z_ref[:, h0], z_ref[:, h1], ba_ref[...],
                                      s_in, pair=p)
            o_ref[:, h0] = o0.astype(o_ref.dtype)
            o_ref[:, h1] = o1.astype(o_ref.dtype)
            state[rs, :] = s_out

    return pl.pallas_call(
        body, name="gdn_fwd", grid=(n,),
        in_specs=[vec, vec, vec, qkv_s, z_s, ba_s], out_specs=[wide, st_s],
        out_shape=[jax.ShapeDtypeStruct((rows, DN_V_WIDTH), bf16),
                   jax.ShapeDtypeStruct((n, DN_HEADS * DN_KEY_DIM, LANES), f32)],
        scratch_shapes=[pltpu.VMEM((DN_HEADS * DN_KEY_DIM, LANES), f32)],
        compiler_params=_cparams(("arbitrary",)),
    )(a_log, dt_bias, norm_w, qkv_act, proj, ba)


def _gdn_bwd(a_log, dt_bias, norm_w, qkv_act, proj, ba, states, do):
    rows = qkv_act.shape[0]
    n, vec, qkv_s, z_s, ba_s, wide, st_s = _gdn_specs(rows, True)
    npair = DN_HEADS // 2

    def body(al_ref, dt_ref, nw_ref, qkv_ref, z_ref, ba_ref, st_ref, do_ref,
             dal_ref, ddt_ref, dnw_ref, dqkv_ref, dz_ref, dba_ref, dstate):
        @pl.when(pl.program_id(0) == 0)
        def _():
            dstate[...] = jnp.zeros_like(dstate)
            dal_ref[...] = jnp.zeros_like(dal_ref)
            ddt_ref[...] = jnp.zeros_like(ddt_ref)
            dnw_ref[...] = jnp.zeros_like(dnw_ref)

        dba = jnp.zeros((DN_CHUNK, LANES), f32)
        for p in range(npair):
            q, k, v0, v1, h0, h1 = _gdn_slices(p)
            rs = slice(p * LANES, (p + 1) * LANES)
            _, vjp = jax.vjp(functools.partial(_gdn_pair, pair=p), al_ref[...], dt_ref[...], nw_ref[...],
                             qkv_ref[:, q], qkv_ref[:, k], qkv_ref[:, v0], qkv_ref[:, v1],
                             z_ref[:, h0], z_ref[:, h1], ba_ref[...], st_ref[0, rs, :])
            dal, ddt, dnw, dq, dk, dv0, dv1, dz0, dz1, dba_p, ds = vjp((do_ref[:, h0], do_ref[:, h1], dstate[rs, :]))
            dal_ref[...] += dal
            ddt_ref[...] += ddt
            dnw_ref[...] += dnw
            dqkv_ref[:, q] = dq
            dqkv_ref[:, k] = dk
            dqkv_ref[:, v0] = dv0
            dqkv_ref[:, v1] = dv1
            dz_ref[:, h0] = dz0.astype(dz_ref.dtype)
            dz_ref[:, h1] = dz1.astype(dz_ref.dtype)
            dba = dba + dba_p
            dstate[rs, :] = ds
        dba_ref[...] = dba

    return pl.pallas_call(
        body, name="gdn_bwd", grid=(n,),
        in_specs=[vec, vec, vec, qkv_s, z_s, ba_s, st_s, wide],
        out_specs=[vec, vec, vec, qkv_s, wide, ba_s],
        out_shape=[jax.ShapeDtypeStruct((1, LANES), f32)] * 3
        + [jax.ShapeDtypeStruct((rows, DN_CONV_CH), f32), jax.ShapeDtypeStruct((rows, DN_V_WIDTH), bf16),
           jax.ShapeDtypeStruct((rows, LANES), f32)],
        scratch_shapes=[pltpu.VMEM((DN_HEADS * DN_KEY_DIM, LANES), f32)],
        compiler_params=_cparams(("arbitrary",)),
    )(a_log, dt_bias, norm_w, qkv_act, proj, ba, states, do)


def _ada_fwd(c_all, w_loc, b_loc):
    def body(c_ref, w_ref, b_ref, o_ref):
        o_ref[...] = _bdot(_silu(c_ref[...]), w_ref[...]) + b_ref[...]

    return pl.pallas_call(body, name="ada_fwd", out_shape=jax.ShapeDtypeStruct((c_all.shape[0], w_loc.shape[1]), f32),
                          compiler_params=_cparams())(c_all, w_loc, b_loc)


def _ada_bwd(c_all, dmod_cols):
    def body(c_ref, d_ref, o_ref):
        o_ref[...] = _bdot_tn(_silu(c_ref[...]), d_ref[...])

    return pl.pallas_call(body, name="ada_bwd",
                          out_shape=jax.ShapeDtypeStruct((c_all.shape[1], dmod_cols.shape[1]), f32),
                          compiler_params=_cparams())(c_all, dmod_cols)


def _sum_devices(parts):
    def body(p_ref, o_ref):
        acc = p_ref[0:1, :]
        for d in range(1, N_DEV):
            acc = acc + p_ref[d:d + 1, :]
        o_ref[...] = acc

    return pl.pallas_call(body, name="sum_small", out_shape=jax.ShapeDtypeStruct((1, parts.shape[1]), f32),
                          compiler_params=_cparams())(parts)


def _adam_math(w, g, m, v):
    m2 = ADAM_B1 * m + (1.0 - ADAM_B1) * g
    v2 = ADAM_B2 * v + (1.0 - ADAM_B2) * jnp.square(g)
    m_hat = m2 / (1.0 - ADAM_B1 ** ADAM_STEP)
    v_hat = v2 / (1.0 - ADAM_B2 ** ADAM_STEP)
    delta = -ADAM_LR * (m_hat / (jnp.sqrt(v_hat) + ADAM_EPS) + ADAM_WD * w)
    return delta, m2, v2


def _row_tile(rows):
    return _pick(rows, (256, 128, 64, 32, 16, 8))


def _adamw(w, g, m, v, name):
    rows, cols = w.shape
    tr = _row_tile(rows)

    def body(w_ref, g_ref, m_ref, v_ref, d_ref, m2_ref, v2_ref):
        d_ref[...], m2_ref[...], v2_ref[...] = _adam_math(w_ref[...], g_ref[...], m_ref[...], v_ref[...])

    tile = pl.BlockSpec((tr, cols), lambda i: (i, 0))
    return pl.pallas_call(body, name=name, grid=(rows // tr,), in_specs=[tile] * 4, out_specs=[tile] * 3,
                          out_shape=[jax.ShapeDtypeStruct(w.shape, f32)] * 3,
                          compiler_params=_cparams(("parallel",)))(w, g, m, v)


def _sum_adamw(parts, w, m, v, name):
    rows, cols = w.shape
    tr = _row_tile(rows)

    def body(p_ref, w_ref, m_ref, v_ref, g_ref, d_ref, m2_ref, v2_ref):
        g = p_ref[0].astype(f32)
        for d in range(1, N_DEV):
            g = g + p_ref[d].astype(f32)
        g_ref[...] = g
        d_ref[...], m2_ref[...], v2_ref[...] = _adam_math(w_ref[...], g, m_ref[...], v_ref[...])

    tile = pl.BlockSpec((tr, cols), lambda i: (i, 0))
    return pl.pallas_call(body, name=name, grid=(rows // tr,),
                          in_specs=[pl.BlockSpec((N_DEV, tr, cols), lambda i: (0, i, 0)), tile, tile, tile],
                          out_specs=[tile] * 4, out_shape=[jax.ShapeDtypeStruct(w.shape, f32)] * 4,
                          compiler_params=_cparams(("parallel",)))(parts, w, m, v)


def _pad_lanes(a, width):
    return jnp.pad(a, ((0, 0), (0, width - a.shape[1])))


def _cols_by_device(full):
    r, c = full.shape
    return jnp.moveaxis(full.reshape(r, N_DEV, c // N_DEV), 1, 0)


def _cols_from_devices(parts):
    d, r, n = parts.shape
    return jnp.moveaxis(parts, 0, 1).reshape(r, d * n)


def kernel(x, c, w_ada, b_ada, norm1_w, w_in, dn_conv_w, dn_A_log, dn_dt_bias, dn_norm_w, w_proj_sb, w_proj_dn, w_out, norm2_w, w_ffn_in, ffn_conv_w, ffn_conv_b, w_ffn_out, final_norm_w, loss_target, m_w_ada, m_b_ada, m_norm1_w, m_w_in, m_dn_conv_w, m_dn_A_log, m_dn_dt_bias, m_dn_norm_w, m_w_proj_sb, m_w_proj_dn, m_w_out, m_norm2_w, m_w_ffn_in, m_ffn_conv_w, m_ffn_conv_b, m_w_ffn_out, m_final_norm_w, v_w_ada, v_b_ada, v_norm1_w, v_w_in, v_dn_conv_w, v_dn_A_log, v_dn_dt_bias, v_dn_norm_w, v_w_proj_sb, v_w_proj_dn, v_w_out, v_norm2_w, v_w_ffn_in, v_ffn_conv_w, v_ffn_conv_b, v_w_ffn_out, v_final_norm_w):
    d = D_MODEL
    me = 4 * lax.axis_index("x") + 2 * lax.axis_index("y") + lax.axis_index("c")
    xs = x[0]
    target = loss_target[0]
    n_ada = w_ada.shape[2]
    n_dnc = dn_conv_w.shape[2]
    n_ffc = ffn_conv_w.shape[2]

    small = jnp.concatenate([c, dn_conv_w[0].reshape(1, -1), ffn_conv_w[0].reshape(1, -1)], axis=1)
    small = _pad_lanes(small, -(-small.shape[1] // LANES) * LANES)
    (small_g, w_in_g, w_psb_g, w_pdn_g, w_out_g, w_fin_g, w_fout_g) = _all_gather(
        [small, w_in[0].astype(bf16), w_proj_sb[0].astype(bf16), w_proj_dn[0].astype(bf16),
         w_out[0].astype(bf16), w_ffn_in[0].astype(bf16), w_ffn_out[0].astype(bf16)], "gather_weights")
    small_g = small_g[:, 0, :]
    c_all = small_g[:, :d]
    dn_cw = _cols_from_devices(small_g[:, d:d + DN_CONV_WIDTH * n_dnc].reshape(N_DEV, DN_CONV_WIDTH, n_dnc))
    o2 = d + DN_CONV_WIDTH * n_dnc
    ffn_cw = _cols_from_devices(small_g[:, o2:o2 + FFN_CONV_WIDTH * n_ffc].reshape(N_DEV, FFN_CONV_WIDTH, n_ffc))

    w_in_full = _cols_from_devices(w_in_g)
    r_sb, r_dn, r_z = 3 * SB_WIDTH, 3 * SB_WIDTH + DN_CONV_CH, 3 * SB_WIDTH + DN_CONV_CH + DN_V_WIDTH
    r_g = r_z + 2 * DN_HEADS
    w_main = jnp.concatenate([w_in_full[:, r_g:], w_in_full[:, r_sb:r_dn], w_in_full[:, r_dn:r_z],
                              w_in_full[:, :r_sb]], axis=1)
    w_ba = _pad_lanes(w_in_full[:, r_z:r_g], LANES)
    w_psb = _cols_from_devices(w_psb_g)
    w_pdn = w_pdn_g.reshape(DN_V_WIDTH, d)
    w_o = w_out_g.reshape(d, d)
    w_fin = _cols_from_devices(w_fin_g)
    w_fout = w_fout_g.reshape(D_FF, d)

    b_loc = lax.dynamic_slice(b_ada, (0, me * n_ada), (1, n_ada))
    mod_part = _ada_fwd(c_all, w_ada[0], b_loc)
    (mod_g,) = _all_gather([mod_part], "gather_mod")
    mod = lax.dynamic_index_in_dim(mod_g, me, axis=1, keepdims=False).reshape(1, N_DEV * n_ada)
    shift1, scale1, gate1, shift2, scale2, gate2 = [mod[:, i * d:(i + 1) * d] for i in range(6)]

    a_log = _pad_lanes(dn_A_log, LANES)
    dt_b = _pad_lanes(dn_dt_bias, LANES)

    (h,) = _stage_fwd(_f_normmod, [norm1_w, shift1, scale1], [xs], [bf16], "norm1_fwd")
    proj = _mm(h, w_main, name="in_proj")
    ba = _mm(h, w_ba, name="in_proj_ba")
    o_a, sb_runs = _sb_attention_fwd(proj)
    qkv_act = _dn_conv_fwd(proj, dn_cw)
    o_b, states = _gdn_fwd(a_log, dt_b, dn_norm_w, qkv_act, proj, ba)
    pa = _mm(o_a, w_psb, name="proj_sb")
    pb = _mm(o_b, w_pdn, name="proj_dn")
    gates = [(proj, d, OFF_GA // d), (proj, d, OFF_GB // d)]
    (merged,) = _stage_fwd(_f_merge, [], gates + [pa, pb], [bf16], "merge_fwd")
    ao = _mm(merged, w_o, name="out_proj")
    (x1,) = _stage_fwd(_f_residual, [gate1], [xs, ao], [f32], "resid1_fwd")
    (h2,) = _stage_fwd(_f_normmod, [norm2_w, shift2, scale2], [x1], [bf16], "norm2_fwd")
    u_pre = _mm(h2, w_fin, name="ffn_in")
    act = _ffn_conv_fwd(u_pre, ffn_cw, ffn_conv_b)
    fo = _mm(act, w_fout, name="ffn_out")

    loss_p, d_gate2, d_wf, dx2, dfo = _loss_and_grads(gate2, final_norm_w.reshape(1, d), x1, fo, target)
    dact = _mm(dfo, w_fout, tb=True, name="ffn_out_dx")
    g_w_fout = _mm(act, dfo, ta=True, name="ffn_out_dw")
    dug, duu, dbg, dbu = _ffn_conv_bwd_act(u_pre, ffn_cw, ffn_conv_b, dact)
    du = jnp.concatenate([dug, duu], axis=1)
    du_pre, d_ffn_cw = _conv_bwd(du, u_pre, 0, ffn_cw, "ffn_conv_bwd")
    dh2 = _mm(du_pre, w_fin, tb=True, name="ffn_in_dx")
    g_w_fin = _mm(h2, du_pre, ta=True, name="ffn_in_dw")
    (d_n2w, d_shift2, d_scale2), (dx1,) = _stage_bwd(
        _f_normmod, [norm2_w, shift2, scale2], [x1], [dh2], [f32], "norm2_bwd", residual=(0, dx2))
    (d_gate1,), (dao,) = _stage_bwd(_f_residual, [gate1], [xs, ao], [dx1], [None, bf16], "resid1_bwd")
    dmerged = _mm(dao, w_o, tb=True, name="out_proj_dx")
    g_w_o = _mm(merged, dao, ta=True, name="out_proj_dw")
    _, (dga, dgb, dpa, dpb) = _stage_bwd(_f_merge, [], gates + [pa, pb], [dmerged], [bf16] * 4, "merge_bwd")
    do_a = _mm(dpa, w_psb, tb=True, name="proj_sb_dx")
    g_w_psb = _mm(o_a, dpa, ta=True, name="proj_sb_dw")
    do_b = _mm(dpb, w_pdn, tb=True, name="proj_dn_dx")
    g_w_pdn = _mm(o_b, dpb, ta=True, name="proj_dn_dw")
    dq, dk, dv = _sb_attention_bwd(proj, sb_runs, do_a)
    d_alog, d_dtb, d_dnw, dqkv_act, dz, dba = _gdn_bwd(a_log, dt_b, dn_norm_w, qkv_act, proj, ba, states, do_b)
    d_conv_out = _dn_conv_bwd_act(proj, dn_cw, dqkv_act)
    d_dn_pre, d_dn_cw = _conv_bwd(d_conv_out, proj, OFF_DN // TCONV_C, dn_cw, "dn_conv_bwd")
    dproj = jnp.concatenate([dga, dgb, d_dn_pre, dz, dq, dk.astype(bf16), dv.astype(bf16)], axis=1)
    dh = _mm(dproj, w_main, tb=True, name="in_proj_dx")
    dh_ba = _mm(dba, w_ba, tb=True, name="in_proj_ba_dx")
    g_w_main = _mm(h, dproj, ta=True, name="in_proj_dw")
    g_w_ba = _mm(h, dba, ta=True, name="in_proj_ba_dw")
    (d_n1w, d_shift1, d_scale1), (grad_x,) = _stage_bwd(
        _f_normmod, [norm1_w, shift1, scale1], [xs], [[dh, dh_ba]], [f32], "norm1_bwd", residual=(0, dx1))

    dmod = jnp.concatenate([d_shift1, d_scale1, d_gate1, d_shift2, d_scale2, d_gate2], axis=1)
    d_ffn_cb = jnp.concatenate([dbg, dbu], axis=1)
    small_parts = jnp.concatenate(
        [loss_p, dmod, d_n1w, d_alog, d_dtb, d_dnw, d_n2w, d_ffn_cb, d_wf,
         d_dn_cw.reshape(1, -1), d_ffn_cw.reshape(1, -1)], axis=1)
    (small_parts_g,) = _all_gather([small_parts], "gather_small_grads")
    tot = _sum_devices(small_parts_g[:, 0, :])
    offs = {}
    pos = 0
    for nm, width in (("loss", LANES), ("b_ada", 6 * d), ("norm1_w", d), ("dn_A_log", LANES), ("dn_dt_bias", LANES),
                      ("dn_norm_w", LANES), ("norm2_w", d), ("ffn_conv_b", 2 * D_FF), ("final_norm_w", d),
                      ("dn_conv_w", DN_CONV_WIDTH * DN_CONV_CH), ("ffn_conv_w", FFN_CONV_WIDTH * 2 * D_FF)):
        offs[nm] = (pos, width)
        pos += width
    seg = lambda nm: tot[:, offs[nm][0]:offs[nm][0] + offs[nm][1]]
    loss = tot[0, 0]
    g_b_ada = seg("b_ada")
    g_norm1 = seg("norm1_w")
    g_alog = seg("dn_A_log")[:, :DN_HEADS]
    g_dtb = seg("dn_dt_bias")[:, :DN_HEADS]
    g_dnw = seg("dn_norm_w")
    g_norm2 = seg("norm2_w")
    g_ffn_cb = seg("ffn_conv_b")
    g_fnw = seg("final_norm_w")
    g_dn_cw = lax.dynamic_slice(seg("dn_conv_w").reshape(DN_CONV_WIDTH, DN_CONV_CH), (0, me * n_dnc),
                                (DN_CONV_WIDTH, n_dnc))
    g_ffn_cw = lax.dynamic_slice(seg("ffn_conv_w").reshape(FFN_CONV_WIDTH, 2 * D_FF), (0, me * n_ffc),
                                 (FFN_CONV_WIDTH, n_ffc))

    dmod_all = small_parts_g[:, 0, offs["b_ada"][0]:offs["b_ada"][0] + 6 * d]
    g_w_ada = _ada_bwd(c_all, lax.dynamic_slice(dmod_all, (0, me * n_ada), (N_DEV, n_ada)))

    def pack(parts):
        flat = [p.reshape(1, -1) for p in parts]
        flat = [_pad_lanes(p, -(-p.shape[1] // LANES) * LANES) for p in flat]
        return jnp.concatenate(flat, axis=1), [p.shape[1] for p in flat]

    small_names_g = [g_b_ada, g_norm1, g_alog, g_dtb, g_dnw, g_norm2, g_ffn_cb, g_fnw, g_dn_cw, g_ffn_cw]
    small_w = [b_ada, norm1_w, dn_A_log, dn_dt_bias, dn_norm_w, norm2_w, ffn_conv_b, final_norm_w, dn_conv_w[0], ffn_conv_w[0]]
    small_m = [m_b_ada, m_norm1_w, m_dn_A_log, m_dn_dt_bias, m_dn_norm_w, m_norm2_w, m_ffn_conv_b, m_final_norm_w, m_dn_conv_w[0], m_ffn_conv_w[0]]
    small_v = [v_b_ada, v_norm1_w, v_dn_A_log, v_dn_dt_bias, v_dn_norm_w, v_norm2_w, v_ffn_conv_b, v_final_norm_w, v_dn_conv_w[0], v_ffn_conv_w[0]]
    pg, widths = pack(small_names_g)
    pw, _ = pack(small_w)
    pm, _ = pack(small_m)
    pv, _ = pack(small_v)
    s_delta, s_m, s_v = _adamw(pw, pg, pm, pv, "adamw_small")

    def unpack(flat):
        out, pos = [], 0
        for ref_arr, width in zip(small_w, widths):
            out.append(flat[:, pos:pos + ref_arr.size].reshape(ref_arr.shape))
            pos += width
        return out

    small_grads = [g.reshape(w_.shape) for g, w_ in zip(small_names_g, small_w)]
    small_delta, small_newm, small_newv = unpack(s_delta), unpack(s_m), unpack(s_v)

    ada_delta, ada_m, ada_v = _adamw(w_ada[0], g_w_ada, m_w_ada[0], v_w_ada[0], "adamw_ada")

    g_w_in_full = jnp.concatenate([g_w_main[:, OFF_SBQ:], g_w_main[:, OFF_DN:OFF_Z], g_w_main[:, OFF_Z:OFF_SBQ],
                                   g_w_ba[:, :2 * DN_HEADS], g_w_main[:, :OFF_DN]], axis=1)
    sends = [_cols_by_device(g_w_in_full).astype(bf16),
             _cols_by_device(g_w_psb).astype(bf16),
             g_w_pdn.reshape(N_DEV, DN_V_WIDTH // N_DEV, d).astype(bf16),
             g_w_o.reshape(N_DEV, d // N_DEV, d).astype(bf16),
             _cols_by_device(g_w_fin).astype(bf16),
             g_w_fout.reshape(N_DEV, D_FF // N_DEV, d).astype(bf16)]
    recv = _all_to_all(sends, "exchange_grads")
    big = {}
    for nm, parts, w_, m_, v_ in (("w_in", recv[0], w_in, m_w_in, v_w_in),
                                  ("w_proj_sb", recv[1], w_proj_sb, m_w_proj_sb, v_w_proj_sb),
                                  ("w_proj_dn", recv[2], w_proj_dn, m_w_proj_dn, v_w_proj_dn),
                                  ("w_out", recv[3], w_out, m_w_out, v_w_out),
                                  ("w_ffn_in", recv[4], w_ffn_in, m_w_ffn_in, v_w_ffn_in),
                                  ("w_ffn_out", recv[5], w_ffn_out, m_w_ffn_out, v_w_ffn_out)):
        big[nm] = [t[None] for t in _sum_adamw(parts, w_[0], m_[0], v_[0], "adamw_" + nm)]

    sg = dict(zip(["b_ada", "norm1_w", "dn_A_log", "dn_dt_bias", "dn_norm_w", "norm2_w", "ffn_conv_b", "final_norm_w",
                   "dn_conv_w", "ffn_conv_w"], range(10)))

    def small_out(table, nm):
        val = table[sg[nm]]
        return val[None] if nm in ("dn_conv_w", "ffn_conv_w") else val

    order = ["w_ada", "b_ada", "norm1_w", "w_in", "dn_conv_w", "dn_A_log", "dn_dt_bias", "dn_norm_w", "w_proj_sb",
             "w_proj_dn", "w_out", "norm2_w", "w_ffn_in", "ffn_conv_w", "ffn_conv_b", "w_ffn_out", "final_norm_w"]
    groups = []
    for k, small_table in enumerate((small_grads, small_delta, small_newm, small_newv)):
        row = []
        for nm in order:
            if nm == "w_ada":
                row.append((g_w_ada, ada_delta, ada_m, ada_v)[k][None])
            elif nm in big:
                row.append(big[nm][k])
            else:
                row.append(small_out(small_table, nm))
        groups.append(row)
    return (loss, grad_x[None], *groups[0], *groups[1], *groups[2], *groups[3])
```

```python
import functools

import jax
import jax.numpy as jnp
from jax import lax
from jax.experimental import pallas as pl
from jax.experimental.pallas import tpu as pltpu

f32 = jnp.float32
bf16 = jnp.bfloat16

D_MODEL = 1024
SB_HEADS = 8
SB_HEAD_DIM = 64
SB_WIDTH = SB_HEADS * SB_HEAD_DIM
SB_QBLOCK = 128
DN_HEADS = 8
DN_KEY_DIM = 64
DN_VAL_DIM = 128
DN_QK_WIDTH = DN_HEADS * DN_KEY_DIM
DN_V_WIDTH = DN_HEADS * DN_VAL_DIM
DN_CONV_CH = 2 * DN_QK_WIDTH + DN_V_WIDTH
DN_CONV_WIDTH = 4
DN_CHUNK = 64
D_FF = 2816
FFN_CONV_WIDTH = 3
NORM_EPS = 1e-6
L2_EPS = 1e-6
ADAM_LR = 0.001
ADAM_B1 = 0.9
ADAM_B2 = 0.999
ADAM_EPS = 1e-08
ADAM_WD = 0.01
ADAM_STEP = 10

N_DEV = 8
MESH = pl.DeviceIdType.MESH

LANES = 128
SUBLANES = 8
VMEM_LIMIT = 48 * 1024 * 1024

OFF_GA = 0
OFF_GB = D_MODEL
OFF_DN = 2 * D_MODEL
OFF_Z = OFF_DN + DN_CONV_CH
OFF_SBQ = OFF_Z + DN_V_WIDTH
OFF_SBK = OFF_SBQ + SB_WIDTH
OFF_SBV = OFF_SBK + SB_WIDTH
MAIN_WIDTH = OFF_SBV + SB_WIDTH

TM = 256
TCONV_R = 256
TCONV_C = 256


def _cparams(sem=None):
    return pltpu.CompilerParams(dimension_semantics=sem, vmem_limit_bytes=VMEM_LIMIT)


def _pick(n, cands):
    for c in cands:
        if n % c == 0:
            return c
    return n


def _my_pos():
    return lax.axis_index("x"), lax.axis_index("y"), lax.axis_index("c")


def _flip(v, bit):
    return 1 - v if bit else v


def _all_gather(arrs, name):
    n = len(arrs)

    def body(*refs):
        ins, outs = refs[:n], refs[n:2 * n]
        send_sems, recv_sems, local_sems = refs[2 * n:]
        x, y, c = _my_pos()
        me, sibling = (x, y, c), (x, y, 1 - c)
        chips = [(1 - x, y), (x, 1 - y), (1 - x, 1 - y)]

        def slot(out, pos):
            return out.at[4 * pos[0] + 2 * pos[1] + pos[2]]

        def copy(a, k, block, to, src=None):
            return pltpu.make_async_remote_copy(
                src_ref=slot(outs[a], block) if src is None else src, dst_ref=slot(outs[a], block),
                send_sem=send_sems.at[a, k], recv_sem=recv_sems.at[a, k], device_id=to, device_id_type=MESH)

        started = []
        mine = []
        for a in range(n):
            cp = pltpu.make_async_copy(ins[a], slot(outs[a], me), local_sems.at[a])
            cp.start()
            mine.append(cp)
            first = [copy(a, 0, me, sibling, src=ins[a])]
            first += [copy(a, 1 + j, me, (*chip, c), src=ins[a]) for j, chip in enumerate(chips)]
            for cp in first:
                cp.start()
            started += first
        for a in range(n):
            for j, chip in enumerate(chips):
                copy(a, 1 + j, (*chip, c), me).wait_recv()
                fwd = copy(a, 4 + j, (*chip, c), sibling)
                fwd.start()
                started.append(fwd)
        for a in range(n):
            copy(a, 0, sibling, me).wait_recv()
            for j, chip in enumerate(chips):
                copy(a, 4 + j, (*chip, 1 - c), me).wait_recv()
        for cp in started:
            cp.wait_send()
        for cp in mine:
            cp.wait()

    any_spec = pl.BlockSpec(memory_space=pl.ANY)
    return pl.pallas_call(
        body, name=name,
        out_shape=[jax.ShapeDtypeStruct((N_DEV,) + a.shape, a.dtype) for a in arrs],
        in_specs=[any_spec] * n, out_specs=[any_spec] * n,
        scratch_shapes=[pltpu.SemaphoreType.DMA((n, 7)), pltpu.SemaphoreType.DMA((n, 7)),
                        pltpu.SemaphoreType.DMA((n,))],
    )(*arrs)


def _all_to_all(arrs, name):
    n = len(arrs)

    def body(*refs):
        ins, outs = refs[:n], refs[n:2 * n]
        send_sems, recv_sems, local_sems = refs[2 * n:]
        x, y, c = _my_pos()
        me_idx = 4 * x + 2 * y + c
        copies = []
        for a in range(n):
            cp = pltpu.make_async_copy(ins[a].at[me_idx], outs[a].at[me_idx], local_sems.at[a])
            cp.start()
            copies.append(cp)
        rdmas = []
        for a in range(n):
            for m in range(1, N_DEV):
                peer = (_flip(x, m & 4), _flip(y, m & 2), _flip(c, m & 1))
                peer_idx = 4 * peer[0] + 2 * peer[1] + peer[2]
                send = pltpu.make_async_remote_copy(
                    src_ref=ins[a].at[peer_idx], dst_ref=outs[a].at[me_idx],
                    send_sem=send_sems.at[a, m - 1], recv_sem=recv_sems.at[a, m - 1],
                    device_id=peer, device_id_type=MESH)
                send.start()
                recv = pltpu.make_async_remote_copy(
                    src_ref=ins[a].at[peer_idx], dst_ref=outs[a].at[peer_idx],
                    send_sem=send_sems.at[a, m - 1], recv_sem=recv_sems.at[a, m - 1],
                    device_id=peer, device_id_type=MESH)
                rdmas.append((send, recv))
        for send, recv in rdmas:
            recv.wait_recv()
        for send, recv in rdmas:
            send.wait_send()
        for cp in copies:
            cp.wait()

    any_spec = pl.BlockSpec(memory_space=pl.ANY)
    return pl.pallas_call(
        body, name=name,
        out_shape=[jax.ShapeDtypeStruct(a.shape, a.dtype) for a in arrs],
        in_specs=[any_spec] * n, out_specs=[any_spec] * n,
        scratch_shapes=[pltpu.SemaphoreType.DMA((n, 7)), pltpu.SemaphoreType.DMA((n, 7)),
                        pltpu.SemaphoreType.DMA((n,))],
    )(*arrs)


def _mm(a, b, *, ta=False, tb=False, out_dtype=f32, name):
    (k_dim, m_dim) = a.shape if ta else a.shape[::-1]
    (n_dim, kb_dim) = b.shape if tb else b.shape[::-1]
    assert k_dim == kb_dim, (a.shape, b.shape, ta, tb)
    tm = _pick(m_dim, (512, 256, 128))
    tn = _pick(n_dim, (512, 256, 128))
    tk = _pick(k_dim, (1024, 512, 256, 128))
    nk = k_dim // tk
    dims = (((0 if ta else 1,), (1 if tb else 0,)), ((), ()))

    def body(a_ref, b_ref, o_ref, acc_ref):
        k = pl.program_id(2)

        @pl.when(k == 0)
        def _():
            acc_ref[...] = jnp.zeros_like(acc_ref)

        acc_ref[...] += lax.dot_general(a_ref[...].astype(bf16), b_ref[...].astype(bf16), dims,
                                        preferred_element_type=f32)

        @pl.when(k == nk - 1)
        def _():
            o_ref[...] = acc_ref[...].astype(out_dtype)

    a_spec = pl.BlockSpec((tk, tm), lambda i, j, k: (k, i)) if ta else pl.BlockSpec((tm, tk), lambda i, j, k: (i, k))
    b_spec = pl.BlockSpec((tn, tk), lambda i, j, k: (j, k)) if tb else pl.BlockSpec((tk, tn), lambda i, j, k: (k, j))
    return pl.pallas_call(
        body, name=name, grid=(m_dim // tm, n_dim // tn, nk),
        in_specs=[a_spec, b_spec], out_specs=pl.BlockSpec((tm, tn), lambda i, j, k: (i, j)),
        out_shape=jax.ShapeDtypeStruct((m_dim, n_dim), out_dtype),
        scratch_shapes=[pltpu.VMEM((tm, tn), f32)],
        compiler_params=_cparams(("parallel", "parallel", "arbitrary")),
    )(a, b)


def _win(t):
    return t if isinstance(t, tuple) else (t, t.shape[1], 0)


def _tile_spec(width, cb, tm):
    return pl.BlockSpec((tm, width), lambda i: (i, cb))


def _param_spec(p):
    return pl.BlockSpec(p.shape, lambda i: (0, 0))


def _stage_fwd(f, params, tiles, out_dtypes, name):
    tiles = [_win(t) for t in tiles]
    rows = tiles[0][0].shape[0]
    tm = min(TM, rows)
    avals = jax.eval_shape(f, *[jax.ShapeDtypeStruct(p.shape, f32) for p in params],
                           *[jax.ShapeDtypeStruct((tm, w), f32) for _, w, _ in tiles])
    n_p, n_t = len(params), len(tiles)

    def body(*refs):
        p = [r[...] for r in refs[:n_p]]
        t = [r[...].astype(f32) for r in refs[n_p:n_p + n_t]]
        for o_ref, val in zip(refs[n_p + n_t:], f(*p, *t)):
            o_ref[...] = val.astype(o_ref.dtype)

    return pl.pallas_call(
        body, name=name, grid=(rows // tm,),
        in_specs=[_param_spec(p) for p in params] + [_tile_spec(w, cb, tm) for _, w, cb in tiles],
        out_specs=[_tile_spec(a.shape[1], 0, tm) for a in avals],
        out_shape=[jax.ShapeDtypeStruct((rows, a.shape[1]), dt) for a, dt in zip(avals, out_dtypes)],
        compiler_params=_cparams(("parallel",)),
    )(*params, *[t[0] for t in tiles])


def _stage_bwd(f, params, tiles, cts, grad_dtypes, name, residual=None):
    tiles = [_win(t) for t in tiles]
    rows = tiles[0][0].shape[0]
    tm = min(TM, rows)
    cts = [list(g) if isinstance(g, (list, tuple)) else [g] for g in cts]
    flat_cts = [a for g in cts for a in g]
    n_p, n_t, n_c = len(params), len(tiles), len(flat_cts)
    has_res = residual is not None
    want = [j for j, dt in enumerate(grad_dtypes) if dt is not None]

    def body(*refs):
        i = pl.program_id(0)
        p = [r[...] for r in refs[:n_p]]
        t = [r[...].astype(f32) for r in refs[n_p:n_p + n_t]]
        ct_vals = [r[...].astype(f32) for r in refs[n_p + n_t:n_p + n_t + n_c]]
        ct, at = [], 0
        for g in cts:
            ct.append(functools.reduce(jnp.add, ct_vals[at:at + len(g)]))
            at += len(g)
        ct = tuple(ct)
        pos = n_p + n_t + n_c
        res_ref = refs[pos] if has_res else None
        pos += 1 if has_res else 0
        dp_refs = refs[pos:pos + n_p]
        dt_refs = refs[pos + n_p:]
        _, vjp = jax.vjp(f, *p, *t)
        grads = vjp(ct)

        @pl.when(i == 0)
        def _():
            for r in dp_refs:
                r[...] = jnp.zeros_like(r)

        for r, g in zip(dp_refs, grads[:n_p]):
            r[...] += g
        for r, j in zip(dt_refs, want):
            g = grads[n_p + j]
            if has_res and j == residual[0]:
                g = g + res_ref[...].astype(f32)
            r[...] = g.astype(r.dtype)

    in_arrays = list(params) + [t[0] for t in tiles] + flat_cts
    in_specs = ([_param_spec(p) for p in params] + [_tile_spec(w, cb, tm) for _, w, cb in tiles]
                + [_tile_spec(c.shape[1], 0, tm) for c in flat_cts])
    if has_res:
        in_arrays.append(residual[1])
        in_specs.append(_tile_spec(residual[1].shape[1], 0, tm))
    out_shape = ([jax.ShapeDtypeStruct(p.shape, f32) for p in params]
                 + [jax.ShapeDtypeStruct((rows, tiles[j][1]), grad_dtypes[j]) for j in want])
    out_specs = [_param_spec(p) for p in params] + [_tile_spec(tiles[j][1], 0, tm) for j in want]
    outs = pl.pallas_call(
        body, name=name, grid=(rows // tm,), in_specs=in_specs, out_specs=out_specs, out_shape=out_shape,
        compiler_params=_cparams(("arbitrary",)),
    )(*in_arrays)
    return outs[:n_p], outs[n_p:]


def _rms(x, w):
    return x * lax.rsqrt(jnp.mean(x * x, axis=-1, keepdims=True) + NORM_EPS) * w


def _f_normmod(w, shift, scale, x):
    return (_rms(x, w) * (1.0 + scale) + shift,)


def _f_merge(ga, gb, pa, pb):
    return (jax.nn.sigmoid(ga) * pa + jax.nn.sigmoid(gb) * pb,)


def _f_residual(gate, x, branch):
    return (x + gate * branch,)


def _f_loss(gate, wf, x1, fo, target):
    y = _rms(x1 + gate * fo, wf)
    err = jnp.square(y - target)
    return (0.5 * jnp.sum(jnp.mean(err, axis=-1, keepdims=True), axis=0, keepdims=True),)


def _loss_and_grads(gate2, wf, x1, fo, target):
    rows, d = x1.shape
    tm = min(TM, rows)

    def body(g_ref, w_ref, x_ref, fo_ref, t_ref, loss_ref, dg_ref, dw_ref, dx_ref, dfo_ref):
        i = pl.program_id(0)
        (val,), vjp = jax.vjp(_f_loss, g_ref[...], w_ref[...], x_ref[...], fo_ref[...], t_ref[...])
        dg, dw, dx, dfo, _ = vjp((jnp.ones((1, 1), f32),))

        @pl.when(i == 0)
        def _():
            loss_ref[...] = jnp.zeros_like(loss_ref)
            dg_ref[...] = jnp.zeros_like(dg_ref)
            dw_ref[...] = jnp.zeros_like(dw_ref)

        loss_ref[...] += jnp.broadcast_to(val, loss_ref.shape)
        dg_ref[...] += dg
        dw_ref[...] += dw
        dx_ref[...] = dx
        dfo_ref[...] = dfo.astype(bf16)

    vec = pl.BlockSpec((1, d), lambda i: (0, 0))
    tile = pl.BlockSpec((tm, d), lambda i: (i, 0))
    return pl.pallas_call(
        body, name="loss_fwd_bwd", grid=(rows // tm,),
        in_specs=[vec, vec, tile, tile, tile],
        out_specs=[pl.BlockSpec((1, LANES), lambda i: (0, 0)), vec, vec, tile, tile],
        out_shape=[jax.ShapeDtypeStruct((1, LANES), f32), jax.ShapeDtypeStruct((1, d), f32),
                   jax.ShapeDtypeStruct((1, d), f32), jax.ShapeDtypeStruct((rows, d), f32),
                   jax.ShapeDtypeStruct((rows, d), bf16)],
        compiler_params=_cparams(("arbitrary",)),
    )(gate2, wf, x1, fo, target)


def _softplus(z):
    return jnp.maximum(z, 0.0) + jnp.log(1.0 + jnp.exp(-jnp.abs(z)))


def _split_dot(a, m):
    hi = a.astype(bf16)
    lo = (a - hi.astype(f32)).astype(bf16)
    return jnp.dot(hi, m, preferred_element_type=f32) + jnp.dot(lo, m, preferred_element_type=f32)


def _suffix_matrix(n):
    r = lax.broadcasted_iota(jnp.int32, (n, n), 0)
    c = lax.broadcasted_iota(jnp.int32, (n, n), 1)
    return (r > c).astype(bf16)


def _head_masks():
    lane = lax.broadcasted_iota(jnp.int32, (1, LANES), 1)
    return [(lane < SB_HEAD_DIM).astype(f32), (lane >= SB_HEAD_DIM).astype(f32)]


def _sb_prepare(proj):
    def f(k, v):
        lane = lax.broadcasted_iota(jnp.int32, (1, SB_WIDTH), 1)
        m0 = (jnp.bitwise_and(lane, LANES - 1) < SB_HEAD_DIM).astype(f32)
        m1 = 1.0 - m0
        return k, k * m0, k * m1, v, v * m0, v * m1

    wins = [(proj, SB_WIDTH, OFF_SBK // SB_WIDTH), (proj, SB_WIDTH, OFF_SBV // SB_WIDTH)]
    return _stage_fwd(f, [], wins, [bf16] * 6, "sb_prepare")


def _stack_heads(x):
    m0, m1 = _head_masks()
    return jnp.concatenate([x * m0, x * m1], axis=0)


def _sb_logits(qst, k, t_pos2, kb, bq, masked):
    z = lax.dot_general(qst, k, (((1,), (1,)), ((), ())), preferred_element_type=f32)
    l = -_softplus(z)
    if masked:
        s_pos = kb * bq + lax.broadcasted_iota(jnp.int32, (1, bq), 1)
        causal = s_pos < t_pos2
        l = jnp.where(causal, l, 0.0)
    else:
        causal = None
    return z, l, causal


def _sb_attention_fwd2(proj, k16, v0_16, v1_16):
    rows = proj.shape[0]
    bq = SB_QBLOCK
    nq = rows // bq
    assert nq <= LANES, "one lane per key block"
    npair = SB_WIDTH // LANES
    scale = SB_HEAD_DIM ** -0.5

    def body(q_ref, k_ref, v0_ref, v1_ref, o_ref, runs_ref):
        qi = pl.program_id(1)
        qst = (_stack_heads(q_ref[...]) * scale).astype(bf16)
        r = lax.broadcasted_iota(jnp.int32, (bq, 2 * bq), 0)
        c = lax.broadcasted_iota(jnp.int32, (bq, 2 * bq), 1)
        m2 = jnp.logical_or(r > c, c >= bq).astype(bf16)
        t_pos = qi * bq + lax.broadcasted_iota(jnp.int32, (bq, 1), 0)
        t_pos2 = jnp.concatenate([t_pos, t_pos], axis=0)
        lane = lax.broadcasted_iota(jnp.int32, (1, LANES), 1)
        runs_ref[...] = jnp.zeros_like(runs_ref)

        def tile(kb, run, acc, masked):
            r0 = pl.multiple_of(kb * bq, bq)
            z, l, causal = _sb_logits(qst, k_ref[pl.ds(r0, bq), :], t_pos2, kb, bq, masked)
            cs2 = _split_dot(l, m2)
            a = jnp.exp(z + l + cs2[:, :bq] + run)
            if masked:
                a = jnp.where(causal, a, 0.0)
            ab = a.astype(bf16)
            acc = (acc + jnp.dot(ab[:bq], v0_ref[pl.ds(r0, bq), :], preferred_element_type=f32)
                   + jnp.dot(ab[bq:], v1_ref[pl.ds(r0, bq), :], preferred_element_type=f32))
            runs_ref[:, :LANES] = jnp.where(lane == kb, run[:bq], runs_ref[:, :LANES])
            runs_ref[:, LANES:] = jnp.where(lane == kb, run[bq:], runs_ref[:, LANES:])
            return run + cs2[:, bq:], acc

        run, acc = tile(qi, jnp.zeros((2 * bq, bq), f32), jnp.zeros((bq, LANES), f32), True)

        def two(i, carry):
            run, acc = carry
            kb = qi - 1 - 2 * i
            run, acc = tile(kb, run, acc, False)
            return tile(kb - 1, run, acc, False)

        run, acc = lax.fori_loop(0, qi // 2, two, (run, acc))
        run, acc = lax.cond(qi % 2 == 1, lambda rc: tile(0, rc[0], rc[1], False), lambda rc: rc, (run, acc))
        o_ref[...] = acc

    kv = pl.BlockSpec((rows, LANES), lambda p, i: (0, p))
    return pl.pallas_call(
        body, name="sb_attn_fwd", grid=(npair, nq),
        in_specs=[pl.BlockSpec((bq, LANES), lambda p, i: (i, OFF_SBQ // LANES + p)), kv, kv, kv],
        out_specs=[pl.BlockSpec((bq, LANES), lambda p, i: (i, p)),
                   pl.BlockSpec((bq, 2 * LANES), lambda p, i: (i, p))],
        out_shape=[jax.ShapeDtypeStruct((rows, SB_WIDTH), f32),
                   jax.ShapeDtypeStruct((rows, SB_HEADS * LANES), f32)],
        compiler_params=_cparams(("parallel", "arbitrary")),
    )(proj, k16, v0_16, v1_16)


def _sb_attention_bwd2(proj, k16, k0_16, k1_16, v16, runs, do):
    rows = proj.shape[0]
    bq = SB_QBLOCK
    nq = rows // bq
    npair = SB_WIDTH // LANES
    scale = SB_HEAD_DIM ** -0.5
    tn = (((0,), (0,)), ((), ()))
    nt = (((1,), (1,)), ((), ()))

    def body(q_ref, k_ref, k0_ref, k1_ref, v_ref, runs_ref, do_ref, dq_ref, dk_ref, dv_ref):
        qi = pl.program_id(1)

        @pl.when(qi == 0)
        def _():
            dk_ref[...] = jnp.zeros_like(dk_ref)
            dv_ref[...] = jnp.zeros_like(dv_ref)

        qst = (_stack_heads(q_ref[...]) * scale).astype(bf16)
        dost = _stack_heads(do_ref[...]).astype(bf16)
        runs = jnp.concatenate([runs_ref[:, :LANES], runs_ref[:, LANES:]], axis=0)
        r = lax.broadcasted_iota(jnp.int32, (bq, 2 * bq), 0)
        c = lax.broadcasted_iota(jnp.int32, (bq, 2 * bq), 1)
        suffix_m = _suffix_matrix(bq)
        m2 = jnp.logical_or(r < c, c >= bq).astype(bf16)
        t_pos = qi * bq + lax.broadcasted_iota(jnp.int32, (bq, 1), 0)
        t_pos2 = jnp.concatenate([t_pos, t_pos], axis=0)
        lane = lax.broadcasted_iota(jnp.int32, (1, LANES), 1)

        def tile(kb, pref, dq_acc, masked):
            r0 = pl.multiple_of(kb * bq, bq)
            rows_k = pl.ds(r0, bq)
            z, l, causal = _sb_logits(qst, k_ref[rows_k, :], t_pos2, kb, bq, masked)
            run = jnp.sum(jnp.where(lane == kb, runs, 0.0), axis=1, keepdims=True)
            a = jnp.exp(z + l + _split_dot(l, suffix_m) + run)
            if masked:
                a = jnp.where(causal, a, 0.0)
            da = lax.dot_general(dost, v_ref[rows_k, :], nt, preferred_element_type=f32)
            p = da * a
            pc2 = _split_dot(p, m2)
            dz = p * jnp.exp(l) - jnp.exp(z + l) * (pc2[:, :bq] + pref)
            if masked:
                dz = jnp.where(causal, dz, 0.0)
            dzb = dz.astype(bf16)
            dq_acc = (dq_acc + jnp.dot(dzb[:bq], k0_ref[rows_k, :], preferred_element_type=f32)
                      + jnp.dot(dzb[bq:], k1_ref[rows_k, :], preferred_element_type=f32))
            dk_ref[rows_k, :] += lax.dot_general(dzb, qst, tn, preferred_element_type=f32)
            dv_ref[rows_k, :] += lax.dot_general(a.astype(bf16), dost, tn, preferred_element_type=f32)
            return pref + pc2[:, bq:], dq_acc

        def two(i, carry):
            pref, dq_acc = carry
            pref, dq_acc = tile(2 * i, pref, dq_acc, False)
            return tile(2 * i + 1, pref, dq_acc, False)

        carry = (jnp.zeros((2 * bq, bq), f32), jnp.zeros((bq, LANES), f32))
        carry = lax.fori_loop(0, qi // 2, two, carry)
        carry = lax.cond(qi % 2 == 1, lambda pc: tile(qi - 1, pc[0], pc[1], False), lambda pc: pc, carry)
        _, dq_acc = tile(qi, carry[0], carry[1], True)
        dq_ref[...] = (dq_acc * scale).astype(dq_ref.dtype)

    blk = pl.BlockSpec((bq, LANES), lambda p, i: (i, p))
    full = pl.BlockSpec((rows, LANES), lambda p, i: (0, p))
    return pl.pallas_call(
        body, name="sb_attn_bwd", grid=(npair, nq),
        in_specs=[pl.BlockSpec((bq, LANES), lambda p, i: (i, OFF_SBQ // LANES + p)), full, full, full, full,
                  pl.BlockSpec((bq, 2 * LANES), lambda p, i: (i, p)), blk],
        out_specs=[blk, full, full],
        out_shape=[jax.ShapeDtypeStruct((rows, SB_WIDTH), bf16), jax.ShapeDtypeStruct((rows, SB_WIDTH), f32),
                   jax.ShapeDtypeStruct((rows, SB_WIDTH), f32)],
        compiler_params=_cparams(("parallel", "arbitrary")),
    )(proj, k16, k0_16, k1_16, v16, runs, do)


def _shift_down(x, prev8, j):
    if j == 0:
        return x
    r = pltpu.roll(x, j, axis=0)
    row8 = lax.broadcasted_iota(jnp.int32, prev8.shape, 0)
    head = jnp.where(row8 < j, pltpu.roll(prev8, j, axis=0), r[0:SUBLANES])
    return jnp.concatenate([head, r[SUBLANES:]], axis=0)


def _shift_up(x, next8, j):
    if j == 0:
        return x
    n = x.shape[0]
    r = pltpu.roll(x, n - j, axis=0)
    row8 = lax.broadcasted_iota(jnp.int32, next8.shape, 0)
    tail = jnp.where(row8 >= SUBLANES - j, pltpu.roll(next8, SUBLANES - j, axis=0), r[n - SUBLANES:n])
    return jnp.concatenate([r[:n - SUBLANES], tail], axis=0)


def _conv(x, prev8, w):
    k_taps = w.shape[0]
    out = x * w[k_taps - 1:k_taps, :]
    for j in range(1, k_taps):
        out = out + _shift_down(x, prev8, j) * w[k_taps - 1 - j:k_taps - j, :]
    return out


def _conv_tiles(rows):
    tr = min(TCONV_R, rows)
    return tr, rows // tr, tr // SUBLANES


def _prev_spec(tc, cb0, r8):
    return pl.BlockSpec((SUBLANES, tc), lambda j, i: (jnp.maximum(i * r8 - 1, 0), cb0 + j))


def _silu(x):
    return x * jax.nn.sigmoid(x)


def _dsilu(x):
    s = jax.nn.sigmoid(x)
    return s * (1.0 + x * (1.0 - s))


def _dn_conv_fwd(proj, w):
    rows = proj.shape[0]
    tr, nr, r8 = _conv_tiles(rows)
    tc = TCONV_C
    cb0 = OFF_DN // tc

    def body(x_ref, p_ref, w_ref, o_ref):
        prev = jnp.where(pl.program_id(1) == 0, 0.0, p_ref[...])
        o_ref[...] = _silu(_conv(x_ref[...], prev, w_ref[...]))

    return pl.pallas_call(
        body, name="dn_conv_fwd", grid=(DN_CONV_CH // tc, nr),
        in_specs=[pl.BlockSpec((tr, tc), lambda j, i: (i, cb0 + j)), _prev_spec(tc, cb0, r8),
                  pl.BlockSpec((DN_CONV_WIDTH, tc), lambda j, i: (0, j))],
        out_specs=pl.BlockSpec((tr, tc), lambda j, i: (i, j)),
        out_shape=jax.ShapeDtypeStruct((rows, DN_CONV_CH), f32),
        compiler_params=_cparams(("parallel", "parallel")),
    )(proj, proj, w)


def _dn_conv_bwd_act(proj, w, dact):
    rows = proj.shape[0]
    tr, nr, r8 = _conv_tiles(rows)
    tc = TCONV_C
    cb0 = OFF_DN // tc

    def body(x_ref, p_ref, w_ref, d_ref, o_ref):
        prev = jnp.where(pl.program_id(1) == 0, 0.0, p_ref[...])
        o_ref[...] = d_ref[...] * _dsilu(_conv(x_ref[...], prev, w_ref[...]))

    return pl.pallas_call(
        body, name="dn_conv_bwd_act", grid=(DN_CONV_CH // tc, nr),
        in_specs=[pl.BlockSpec((tr, tc), lambda j, i: (i, cb0 + j)), _prev_spec(tc, cb0, r8),
                  pl.BlockSpec((DN_CONV_WIDTH, tc), lambda j, i: (0, j)),
                  pl.BlockSpec((tr, tc), lambda j, i: (i, j))],
        out_specs=pl.BlockSpec((tr, tc), lambda j, i: (i, j)),
        out_shape=jax.ShapeDtypeStruct((rows, DN_CONV_CH), f32),
        compiler_params=_cparams(("parallel", "parallel")),
    )(proj, proj, w, dact)


def _ffn_conv_fwd(u_pre, w, b):
    rows = u_pre.shape[0]
    tr, nr, r8 = _conv_tiles(rows)
    tc = TCONV_C
    nct = D_FF // tc

    def body(xg_ref, pg_ref, xu_ref, pu_ref, wg_ref, wu_ref, bg_ref, bu_ref, o_ref):
        first = pl.program_id(1) == 0
        ug = _conv(xg_ref[...], jnp.where(first, 0.0, pg_ref[...]), wg_ref[...]) + bg_ref[...]
        uu = _conv(xu_ref[...], jnp.where(first, 0.0, pu_ref[...]), wu_ref[...]) + bu_ref[...]
        o_ref[...] = (_silu(ug) * uu).astype(o_ref.dtype)

    def x_spec(off):
        return pl.BlockSpec((tr, tc), lambda j, i: (i, off + j))

    def w_spec(k, off):
        return pl.BlockSpec((k, tc), lambda j, i: (0, off + j))

    return pl.pallas_call(
        body, name="ffn_conv_fwd", grid=(nct, nr),
        in_specs=[x_spec(0), _prev_spec(tc, 0, r8), x_spec(nct), _prev_spec(tc, nct, r8),
                  w_spec(FFN_CONV_WIDTH, 0), w_spec(FFN_CONV_WIDTH, nct), w_spec(1, 0), w_spec(1, nct)],
        out_specs=pl.BlockSpec((tr, tc), lambda j, i: (i, j)),
        out_shape=jax.ShapeDtypeStruct((rows, D_FF), bf16),
        compiler_params=_cparams(("parallel", "parallel")),
    )(u_pre, u_pre, u_pre, u_pre, w, w, b, b)


def _ffn_conv_bwd_act(u_pre, w, b, dact):
    rows = u_pre.shape[0]
    tr, nr, r8 = _conv_tiles(rows)
    tc = TCONV_C
    nct = D_FF // tc

    def body(xg_ref, pg_ref, xu_ref, pu_ref, wg_ref, wu_ref, bg_ref, bu_ref, d_ref,
             dug_ref, duu_ref, dbg_ref, dbu_ref):
        i = pl.program_id(1)
        first = i == 0
        ug = _conv(xg_ref[...], jnp.where(first, 0.0, pg_ref[...]), wg_ref[...]) + bg_ref[...]
        uu = _conv(xu_ref[...], jnp.where(first, 0.0, pu_ref[...]), wu_ref[...]) + bu_ref[...]
        d = d_ref[...]
        dug = d * uu * _dsilu(ug)
        duu = d * _silu(ug)
        dug_ref[...] = dug
        duu_ref[...] = duu

        @pl.when(first)
        def _():
            dbg_ref[...] = jnp.zeros_like(dbg_ref)
            dbu_ref[...] = jnp.zeros_like(dbu_ref)

        dbg_ref[...] += jnp.sum(dug, axis=0, keepdims=True)
        dbu_ref[...] += jnp.sum(duu, axis=0, keepdims=True)

    def x_spec(off):
        return pl.BlockSpec((tr, tc), lambda j, i: (i, off + j))

    def w_spec(k, off):
        return pl.BlockSpec((k, tc), lambda j, i: (0, off + j))

    tile = pl.BlockSpec((tr, tc), lambda j, i: (i, j))
    vec = pl.BlockSpec((1, tc), lambda j, i: (0, j))
    dug, duu, dbg, dbu = pl.pallas_call(
        body, name="ffn_conv_bwd_act", grid=(nct, nr),
        in_specs=[x_spec(0), _prev_spec(tc, 0, r8), x_spec(nct), _prev_spec(tc, nct, r8),
                  w_spec(FFN_CONV_WIDTH, 0), w_spec(FFN_CONV_WIDTH, nct), w_spec(1, 0), w_spec(1, nct), tile],
        out_specs=[tile, tile, vec, vec],
        out_shape=[jax.ShapeDtypeStruct((rows, D_FF), f32), jax.ShapeDtypeStruct((rows, D_FF), f32),
                   jax.ShapeDtypeStruct((1, D_FF), f32), jax.ShapeDtypeStruct((1, D_FF), f32)],
        compiler_params=_cparams(("parallel", "arbitrary")),
    )(u_pre, u_pre, u_pre, u_pre, w, w, b, b, dact)
    return dug, duu, dbg, dbu


def _conv_bwd(dy, x, x_cb0, w, name):
    rows, ch = dy.shape
    k_taps = w.shape[0]
    tr, nr, r8 = _conv_tiles(rows)
    tc = TCONV_C
    last8 = rows // SUBLANES - 1

    def body(dy_ref, nx_ref, x_ref, p_ref, w_ref, dx_ref, dw_ref):
        i = pl.program_id(1)
        dyv = dy_ref[...]
        nxt = jnp.where(i == nr - 1, 0.0, nx_ref[...])
        prev = jnp.where(i == 0, 0.0, p_ref[...])
        xv = x_ref[...].astype(f32)
        wv = w_ref[...]

        @pl.when(i == 0)
        def _():
            dw_ref[...] = jnp.zeros_like(dw_ref)

        dx = dyv * wv[k_taps - 1:k_taps, :]
        dw_ref[k_taps - 1:k_taps, :] += jnp.sum(dyv * xv, axis=0, keepdims=True)
        for j in range(1, k_taps):
            dx = dx + _shift_up(dyv, nxt, j) * wv[k_taps - 1 - j:k_taps - j, :]
            dw_ref[k_taps - 1 - j:k_taps - j, :] += jnp.sum(dyv * _shift_down(xv, prev, j), axis=0, keepdims=True)
        dx_ref[...] = dx.astype(dx_ref.dtype)

    tile = pl.BlockSpec((tr, tc), lambda j, i: (i, j))
    return pl.pallas_call(
        body, name=name, grid=(ch // tc, nr),
        in_specs=[tile,
                  pl.BlockSpec((SUBLANES, tc), lambda j, i: (jnp.minimum((i + 1) * r8, last8), j)),
                  pl.BlockSpec((tr, tc), lambda j, i: (i, x_cb0 + j)), _prev_spec(tc, x_cb0, r8),
                  pl.BlockSpec((k_taps, tc), lambda j, i: (0, j))],
        out_specs=[tile, pl.BlockSpec((k_taps, tc), lambda j, i: (0, j))],
        out_shape=[jax.ShapeDtypeStruct((rows, ch), bf16), jax.ShapeDtypeStruct((k_taps, ch), f32)],
        compiler_params=_cparams(("parallel", "arbitrary")),
    )(dy, dy, x, x, w)


def _hdot(a, b):
    return jnp.dot(a, b, preferred_element_type=f32, precision=lax.Precision.HIGH)


def _xdot(a, b):
    return jnp.dot(a, b, preferred_element_type=f32, precision=lax.Precision.HIGHEST)


def _bdot(a, b):
    return jnp.dot(a.astype(bf16), b.astype(bf16), preferred_element_type=f32)


def _bdot_nt(a, b):
    return lax.dot_general(a.astype(bf16), b.astype(bf16), (((1,), (1,)), ((), ())), preferred_element_type=f32)


def _bdot_tn(a, b):
    return lax.dot_general(a.astype(bf16), b.astype(bf16), (((0,), (0,)), ((), ())), preferred_element_type=f32)


def _unit_lower_inverse(low, eye, blk):
    d = jnp.where(blk, low, 0.0)
    e = low - d
    d2 = _hdot(d, d)
    d4 = _hdot(d2, d2)
    d8 = _hdot(d4, d4)
    dinv = _hdot(_hdot(_hdot(eye - d, eye + d2), eye + d4), eye + d8)
    n = _hdot(dinv, e)
    ninv = _hdot(eye - n, eye + _hdot(n, n))
    return _hdot(ninv, dinv)


GDN_GROUP = 4
GDN_ROWS = GDN_GROUP * DN_CHUNK
GDN_QK_LANES = GDN_GROUP * DN_KEY_DIM


def _gdn_group(a_log, dt_bias, norm_w, qg, kg, vst, zst, ba, state, *, group):
    n = GDN_ROWS
    r = lax.broadcasted_iota(jnp.int32, (n, n), 0)
    c = lax.broadcasted_iota(jnp.int32, (n, n), 1)
    same_head = jnp.right_shift(r, 6) == jnp.right_shift(c, 6)
    incl = jnp.logical_and(same_head, r >= c)
    strict = jnp.logical_and(same_head, r > c)
    eye = (r == c).astype(f32)
    blk = jnp.right_shift(r, 4) == jnp.right_shift(c, 4)
    own_lanes = same_head.astype(f32)
    lane = lax.broadcasted_iota(jnp.int32, (1, LANES), 1)
    pick = lambda arr, idx: jnp.sum(jnp.where(lane == idx, arr, 0.0), axis=1, keepdims=True)
    heads = [GDN_GROUP * group + h for h in range(GDN_GROUP)]
    beta = jnp.concatenate([jax.nn.sigmoid(pick(ba, hd)) for hd in heads], axis=0)
    g = jnp.concatenate([-jnp.exp(pick(a_log, hd)) * _softplus(pick(ba, DN_HEADS + hd) + pick(dt_bias, hd))
                         for hd in heads], axis=0)
    gc = _xdot(incl.astype(f32), g)
    g_last = _xdot(same_head.astype(f32), g)
    gr = _hdot(jnp.ones((n, n), f32), eye * gc)
    decay = jnp.where(incl, jnp.exp(jnp.where(incl, gc - gr, 0.0)), 0.0)
    q = jnp.concatenate([qg] * GDN_GROUP, axis=0) * own_lanes
    k = jnp.concatenate([kg] * GDN_GROUP, axis=0) * own_lanes
    qn = q * lax.rsqrt(jnp.sum(q * q, axis=1, keepdims=True) + L2_EPS) * (DN_KEY_DIM ** -0.5)
    kn = k * lax.rsqrt(jnp.sum(k * k, axis=1, keepdims=True) + L2_EPS)
    kb = kn * beta
    low = jnp.where(strict, _bdot_nt(kb, kn) * decay, 0.0)
    t = _unit_lower_inverse(low, eye, blk)
    u = _bdot(t, vst * beta)
    w = _bdot(t, kb * jnp.exp(gc))
    intra = jnp.where(incl, _bdot_nt(qn, kn) * decay, 0.0)
    sb = state.astype(bf16)
    v_new = u - jnp.dot(w.astype(bf16), sb, preferred_element_type=f32)
    o = jnp.dot((qn * jnp.exp(gc)).astype(bf16), sb, preferred_element_type=f32) + _bdot(intra, v_new)
    new_state = state * jnp.exp(g_last) + _bdot_tn(kn * jnp.exp(g_last - gc), v_new)
    o_n = o * lax.rsqrt(jnp.mean(o * o, axis=1, keepdims=True) + NORM_EPS) * norm_w
    return o_n * _silu(zst), new_state


def _gdn_specs(rows, reverse):
    n = rows // DN_CHUNK
    idx = (lambda i: n - 1 - i) if reverse else (lambda i: i)
    vec = pl.BlockSpec((1, LANES), lambda i: (0, 0))
    qkv = pl.BlockSpec((DN_CHUNK, DN_CONV_CH), lambda i: (idx(i), 0))
    z = pl.BlockSpec((DN_CHUNK, DN_V_WIDTH), lambda i: (idx(i), OFF_Z // DN_V_WIDTH))
    ba = pl.BlockSpec((DN_CHUNK, LANES), lambda i: (idx(i), 0))
    wide = pl.BlockSpec((DN_CHUNK, DN_V_WIDTH), lambda i: (idx(i), 0))
    st = pl.BlockSpec((1, DN_HEADS * DN_KEY_DIM, LANES), lambda i: (idx(i), 0, 0))
    return n, vec, qkv, z, ba, wide, st


def _gdn_slices(grp):
    q = slice(grp * GDN_QK_LANES, (grp + 1) * GDN_QK_LANES)
    k = slice(DN_QK_WIDTH + grp * GDN_QK_LANES, DN_QK_WIDTH + (grp + 1) * GDN_QK_LANES)
    heads = [slice((GDN_GROUP * grp + h) * LANES, (GDN_GROUP * grp + h + 1) * LANES) for h in range(GDN_GROUP)]
    vs = [slice(2 * DN_QK_WIDTH + s.start, 2 * DN_QK_WIDTH + s.stop) for s in heads]
    return q, k, vs, heads


def _stack_cols(ref, cols):
    return jnp.concatenate([ref[:, s] for s in cols], axis=0)


def _gdn_fwd(a_log, dt_bias, norm_w, qkv_act, proj, ba):
    rows = qkv_act.shape[0]
    n, vec, qkv_s, z_s, ba_s, wide, st_s = _gdn_specs(rows, False)

    def body(al_ref, dt_ref, nw_ref, qkv_ref, z_ref, ba_ref, o_ref, st_ref, state):
        @pl.when(pl.program_id(0) == 0)
        def _():
            state[...] = jnp.zeros_like(state)

        for grp in range(DN_HEADS // GDN_GROUP):
            q, k, vs, heads = _gdn_slices(grp)
            rs = slice(grp * GDN_ROWS, (grp + 1) * GDN_ROWS)
            s_in = state[rs, :]
            st_ref[0, rs, :] = s_in
            o_st, s_out = _gdn_group(al_ref[...], dt_ref[...], nw_ref[...], qkv_ref[:, q], qkv_ref[:, k],
                                     _stack_cols(qkv_ref, vs), _stack_cols(z_ref, heads), ba_ref[...], s_in,
                                     group=grp)
            for h, s in enumerate(heads):
                o_ref[:, s] = o_st[h * DN_CHUNK:(h + 1) * DN_CHUNK].astype(o_ref.dtype)
            state[rs, :] = s_out

    return pl.pallas_call(
        body, name="gdn_fwd", grid=(n,),
        in_specs=[vec, vec, vec, qkv_s, z_s, ba_s], out_specs=[wide, st_s],
        out_shape=[jax.ShapeDtypeStruct((rows, DN_V_WIDTH), bf16),
                   jax.ShapeDtypeStruct((n, DN_HEADS * DN_KEY_DIM, LANES), f32)],
        scratch_shapes=[pltpu.VMEM((DN_HEADS * DN_KEY_DIM, LANES), f32)],
        compiler_params=_cparams(("arbitrary",)),
    )(a_log, dt_bias, norm_w, qkv_act, proj, ba)


def _gdn_bwd(a_log, dt_bias, norm_w, qkv_act, proj, ba, states, do):
    rows = qkv_act.shape[0]
    n, vec, qkv_s, z_s, ba_s, wide, st_s = _gdn_specs(rows, True)

    def body(al_ref, dt_ref, nw_ref, qkv_ref, z_ref, ba_ref, st_ref, do_ref,
             dal_ref, ddt_ref, dnw_ref, dqkv_ref, dz_ref, dba_ref, dstate):
        @pl.when(pl.program_id(0) == 0)
        def _():
            dstate[...] = jnp.zeros_like(dstate)
            dal_ref[...] = jnp.zeros_like(dal_ref)
            ddt_ref[...] = jnp.zeros_like(ddt_ref)
            dnw_ref[...] = jnp.zeros_like(dnw_ref)

        dba = jnp.zeros((DN_CHUNK, LANES), f32)
        for grp in range(DN_HEADS // GDN_GROUP):
            q, k, vs, heads = _gdn_slices(grp)
            rs = slice(grp * GDN_ROWS, (grp + 1) * GDN_ROWS)
            _, vjp = jax.vjp(functools.partial(_gdn_group, group=grp), al_ref[...], dt_ref[...], nw_ref[...],
                             qkv_ref[:, q], qkv_ref[:, k], _stack_cols(qkv_ref, vs), _stack_cols(z_ref, heads),
                             ba_ref[...], st_ref[0, rs, :])
            dal, ddt, dnw, dq, dk, dv_st, dz_st, dba_g, ds = vjp((_stack_cols(do_ref, heads), dstate[rs, :]))
            dal_ref[...] += dal
            ddt_ref[...] += ddt
            dnw_ref[...] += dnw
            dqkv_ref[:, q] = dq
            dqkv_ref[:, k] = dk
            for h, (sv, sh) in enumerate(zip(vs, heads)):
                rows_h = slice(h * DN_CHUNK, (h + 1) * DN_CHUNK)
                dqkv_ref[:, sv] = dv_st[rows_h]
                dz_ref[:, sh] = dz_st[rows_h].astype(dz_ref.dtype)
            dba = dba + dba_g
            dstate[rs, :] = ds
        dba_ref[...] = dba

    return pl.pallas_call(
        body, name="gdn_bwd", grid=(n,),
        in_specs=[vec, vec, vec, qkv_s, z_s, ba_s, st_s, wide],
        out_specs=[vec, vec, vec, qkv_s, wide, ba_s],
        out_shape=[jax.ShapeDtypeStruct((1, LANES), f32)] * 3
        + [jax.ShapeDtypeStruct((rows, DN_CONV_CH), f32), jax.ShapeDtypeStruct((rows, DN_V_WIDTH), bf16),
           jax.ShapeDtypeStruct((rows, LANES), f32)],
        scratch_shapes=[pltpu.VMEM((DN_HEADS * DN_KEY_DIM, LANES), f32)],
        compiler_params=_cparams(("arbitrary",)),
    )(a_log, dt_bias, norm_w, qkv_act, proj, ba, states, do)


def _ada_fwd(c_all, w_loc, b_loc):
    def body(c_ref, w_ref, b_ref, o_ref):
        o_ref[...] = _bdot(_silu(c_ref[...]), w_ref[...]) + b_ref[...]

    return pl.pallas_call(body, name="ada_fwd", out_shape=jax.ShapeDtypeStruct((c_all.shape[0], w_loc.shape[1]), f32),
                          compiler_params=_cparams())(c_all, w_loc, b_loc)


def _ada_bwd(c_all, dmod_cols):
    def body(c_ref, d_ref, o_ref):
        o_ref[...] = _bdot_tn(_silu(c_ref[...]), d_ref[...])

    return pl.pallas_call(body, name="ada_bwd",
                          out_shape=jax.ShapeDtypeStruct((c_all.shape[1], dmod_cols.shape[1]), f32),
                          compiler_params=_cparams())(c_all, dmod_cols)


def _sum_devices(parts):
    def body(p_ref, o_ref):
        acc = p_ref[0:1, :]
        for d in range(1, N_DEV):
            acc = acc + p_ref[d:d + 1, :]
        o_ref[...] = acc

    return pl.pallas_call(body, name="sum_small", out_shape=jax.ShapeDtypeStruct((1, parts.shape[1]), f32),
                          compiler_params=_cparams())(parts)


def _adam_math(w, g, m, v):
    m2 = ADAM_B1 * m + (1.0 - ADAM_B1) * g
    v2 = ADAM_B2 * v + (1.0 - ADAM_B2) * jnp.square(g)
    m_hat = m2 / (1.0 - ADAM_B1 ** ADAM_STEP)
    v_hat = v2 / (1.0 - ADAM_B2 ** ADAM_STEP)
    delta = -ADAM_LR * (m_hat / (jnp.sqrt(v_hat) + ADAM_EPS) + ADAM_WD * w)
    return delta, m2, v2


def _row_tile(rows):
    return _pick(rows, (256, 128, 64, 32, 16, 8))


def _adamw(w, g, m, v, name):
    rows, cols = w.shape
    tr = _row_tile(rows)

    def body(w_ref, g_ref, m_ref, v_ref, d_ref, m2_ref, v2_ref):
        d_ref[...], m2_ref[...], v2_ref[...] = _adam_math(w_ref[...], g_ref[...], m_ref[...], v_ref[...])

    tile = pl.BlockSpec((tr, cols), lambda i: (i, 0))
    return pl.pallas_call(body, name=name, grid=(rows // tr,), in_specs=[tile] * 4, out_specs=[tile] * 3,
                          out_shape=[jax.ShapeDtypeStruct(w.shape, f32)] * 3,
                          compiler_params=_cparams(("parallel",)))(w, g, m, v)


def _sum_adamw(parts, w, m, v, name):
    rows, cols = w.shape
    tr = _row_tile(rows)

    def body(p_ref, w_ref, m_ref, v_ref, g_ref, d_ref, m2_ref, v2_ref):
        g = p_ref[0].astype(f32)
        for d in range(1, N_DEV):
            g = g + p_ref[d].astype(f32)
        g_ref[...] = g
        d_ref[...], m2_ref[...], v2_ref[...] = _adam_math(w_ref[...], g, m_ref[...], v_ref[...])

    tile = pl.BlockSpec((tr, cols), lambda i: (i, 0))
    return pl.pallas_call(body, name=name, grid=(rows // tr,),
                          in_specs=[pl.BlockSpec((N_DEV, tr, cols), lambda i: (0, i, 0)), tile, tile, tile],
                          out_specs=[tile] * 4, out_shape=[jax.ShapeDtypeStruct(w.shape, f32)] * 4,
                          compiler_params=_cparams(("parallel",)))(parts, w, m, v)


def _pad_lanes(a, width):
    return jnp.pad(a, ((0, 0), (0, width - a.shape[1])))


def _cols_by_device(full):
    r, c = full.shape
    return jnp.moveaxis(full.reshape(r, N_DEV, c // N_DEV), 1, 0)


def _cols_from_devices(parts):
    d, r, n = parts.shape
    return jnp.moveaxis(parts, 0, 1).reshape(r, d * n)


def kernel(x, c, w_ada, b_ada, norm1_w, w_in, dn_conv_w, dn_A_log, dn_dt_bias, dn_norm_w, w_proj_sb, w_proj_dn, w_out, norm2_w, w_ffn_in, ffn_conv_w, ffn_conv_b, w_ffn_out, final_norm_w, loss_target, m_w_ada, m_b_ada, m_norm1_w, m_w_in, m_dn_conv_w, m_dn_A_log, m_dn_dt_bias, m_dn_norm_w, m_w_proj_sb, m_w_proj_dn, m_w_out, m_norm2_w, m_w_ffn_in, m_ffn_conv_w, m_ffn_conv_b, m_w_ffn_out, m_final_norm_w, v_w_ada, v_b_ada, v_norm1_w, v_w_in, v_dn_conv_w, v_dn_A_log, v_dn_dt_bias, v_dn_norm_w, v_w_proj_sb, v_w_proj_dn, v_w_out, v_norm2_w, v_w_ffn_in, v_ffn_conv_w, v_ffn_conv_b, v_w_ffn_out, v_final_norm_w):
    d = D_MODEL
    me = 4 * lax.axis_index("x") + 2 * lax.axis_index("y") + lax.axis_index("c")
    xs = x[0]
    target = loss_target[0]
    n_ada = w_ada.shape[2]
    n_dnc = dn_conv_w.shape[2]
    n_ffc = ffn_conv_w.shape[2]

    small = jnp.concatenate([c, dn_conv_w[0].reshape(1, -1), ffn_conv_w[0].reshape(1, -1)], axis=1)
    small = _pad_lanes(small, -(-small.shape[1] // LANES) * LANES)
    (small_g, w_in_g, w_psb_g, w_pdn_g, w_out_g, w_fin_g, w_fout_g) = _all_gather(
        [small, w_in[0].astype(bf16), w_proj_sb[0].astype(bf16), w_proj_dn[0].astype(bf16),
         w_out[0].astype(bf16), w_ffn_in[0].astype(bf16), w_ffn_out[0].astype(bf16)], "gather_weights")
    small_g = small_g[:, 0, :]
    c_all = small_g[:, :d]
    dn_cw = _cols_from_devices(small_g[:, d:d + DN_CONV_WIDTH * n_dnc].reshape(N_DEV, DN_CONV_WIDTH, n_dnc))
    o2 = d + DN_CONV_WIDTH * n_dnc
    ffn_cw = _cols_from_devices(small_g[:, o2:o2 + FFN_CONV_WIDTH * n_ffc].reshape(N_DEV, FFN_CONV_WIDTH, n_ffc))

    w_in_full = _cols_from_devices(w_in_g)
    r_sb, r_dn, r_z = 3 * SB_WIDTH, 3 * SB_WIDTH + DN_CONV_CH, 3 * SB_WIDTH + DN_CONV_CH + DN_V_WIDTH
    r_g = r_z + 2 * DN_HEADS
    w_main = jnp.concatenate([w_in_full[:, r_g:], w_in_full[:, r_sb:r_dn], w_in_full[:, r_dn:r_z],
                              w_in_full[:, :r_sb]], axis=1)
    w_ba = _pad_lanes(w_in_full[:, r_z:r_g], LANES)
    w_psb = _cols_from_devices(w_psb_g)
    w_pdn = w_pdn_g.reshape(DN_V_WIDTH, d)
    w_o = w_out_g.reshape(d, d)
    w_fin = _cols_from_devices(w_fin_g)
    w_fout = w_fout_g.reshape(D_FF, d)

    b_loc = lax.dynamic_slice(b_ada, (0, me * n_ada), (1, n_ada))
    mod_part = _ada_fwd(c_all, w_ada[0], b_loc)
    (mod_g,) = _all_gather([mod_part], "gather_mod")
    mod = lax.dynamic_index_in_dim(mod_g, me, axis=1, keepdims=False).reshape(1, N_DEV * n_ada)
    shift1, scale1, gate1, shift2, scale2, gate2 = [mod[:, i * d:(i + 1) * d] for i in range(6)]

    a_log = _pad_lanes(dn_A_log, LANES)
    dt_b = _pad_lanes(dn_dt_bias, LANES)

    (h,) = _stage_fwd(_f_normmod, [norm1_w, shift1, scale1], [xs], [bf16], "norm1_fwd")
    proj = _mm(h, w_main, name="in_proj")
    ba = _mm(h, w_ba, name="in_proj_ba")
    k16, k0_16, k1_16, v16, v0_16, v1_16 = _sb_prepare(proj)
    o_a, sb_runs = _sb_attention_fwd2(proj, k16, v0_16, v1_16)
    qkv_act = _dn_conv_fwd(proj, dn_cw)
    o_b, states = _gdn_fwd(a_log, dt_b, dn_norm_w, qkv_act, proj, ba)
    pa = _mm(o_a, w_psb, name="proj_sb")
    pb = _mm(o_b, w_pdn, name="proj_dn")
    gates = [(proj, d, OFF_GA // d), (proj, d, OFF_GB // d)]
    (merged,) = _stage_fwd(_f_merge, [], gates + [pa, pb], [bf16], "merge_fwd")
    ao = _mm(merged, w_o, name="out_proj")
    (x1,) = _stage_fwd(_f_residual, [gate1], [xs, ao], [f32], "resid1_fwd")
    (h2,) = _stage_fwd(_f_normmod, [norm2_w, shift2, scale2], [x1], [bf16], "norm2_fwd")
    u_pre = _mm(h2, w_fin, name="ffn_in")
    act = _ffn_conv_fwd(u_pre, ffn_cw, ffn_conv_b)
    fo = _mm(act, w_fout, name="ffn_out")

    loss_p, d_gate2, d_wf, dx2, dfo = _loss_and_grads(gate2, final_norm_w.reshape(1, d), x1, fo, target)
    dact = _mm(dfo, w_fout, tb=True, name="ffn_out_dx")
    g_w_fout = _mm(act, dfo, ta=True, name="ffn_out_dw")
    dug, duu, dbg, dbu = _ffn_conv_bwd_act(u_pre, ffn_cw, ffn_conv_b, dact)
    du = jnp.concatenate([dug, duu], axis=1)
    du_pre, d_ffn_cw = _conv_bwd(du, u_pre, 0, ffn_cw, "ffn_conv_bwd")
    dh2 = _mm(du_pre, w_fin, tb=True, name="ffn_in_dx")
    g_w_fin = _mm(h2, du_pre, ta=True, name="ffn_in_dw")
    (d_n2w, d_shift2, d_scale2), (dx1,) = _stage_bwd(
        _f_normmod, [norm2_w, shift2, scale2], [x1], [dh2], [f32], "norm2_bwd", residual=(0, dx2))
    (d_gate1,), (dao,) = _stage_bwd(_f_residual, [gate1], [xs, ao], [dx1], [None, bf16], "resid1_bwd")
    dmerged = _mm(dao, w_o, tb=True, name="out_proj_dx")
    g_w_o = _mm(merged, dao, ta=True, name="out_proj_dw")
    _, (dga, dgb, dpa, dpb) = _stage_bwd(_f_merge, [], gates + [pa, pb], [dmerged], [bf16] * 4, "merge_bwd")
    do_a = _mm(dpa, w_psb, tb=True, name="proj_sb_dx")
    g_w_psb = _mm(o_a, dpa, ta=True, name="proj_sb_dw")
    do_b = _mm(dpb, w_pdn, tb=True, name="proj_dn_dx")
    g_w_pdn = _mm(o_b, dpb, ta=True, name="proj_dn_dw")
    dq, dk, dv = _sb_attention_bwd2(proj, k16, k0_16, k1_16, v16, sb_runs, do_a)
    d_alog, d_dtb, d_dnw, dqkv_act, dz, dba = _gdn_bwd(a_log, dt_b, dn_norm_w, qkv_act, proj, ba, states, do_b)
    d_conv_out = _dn_conv_bwd_act(proj, dn_cw, dqkv_act)
    d_dn_pre, d_dn_cw = _conv_bwd(d_conv_out, proj, OFF_DN // TCONV_C, dn_cw, "dn_conv_bwd")
    dproj = jnp.concatenate([dga, dgb, d_dn_pre, dz, dq, dk.astype(bf16), dv.astype(bf16)], axis=1)
    dh = _mm(dproj, w_main, tb=True, name="in_proj_dx")
    dh_ba = _mm(dba, w_ba, tb=True, name="in_proj_ba_dx")
    g_w_main = _mm(h, dproj, ta=True, name="in_proj_dw")
    g_w_ba = _mm(h, dba, ta=True, name="in_proj_ba_dw")
    (d_n1w, d_shift1, d_scale1), (grad_x,) = _stage_bwd(
        _f_normmod, [norm1_w, shift1, scale1], [xs], [[dh, dh_ba]], [f32], "norm1_bwd", residual=(0, dx1))

    dmod = jnp.concatenate([d_shift1, d_scale1, d_gate1, d_shift2, d_scale2, d_gate2], axis=1)
    d_ffn_cb = jnp.concatenate([dbg, dbu], axis=1)
    small_parts = jnp.concatenate(
        [loss_p, dmod, d_n1w, d_alog, d_dtb, d_dnw, d_n2w, d_ffn_cb, d_wf,
         d_dn_cw.reshape(1, -1), d_ffn_cw.reshape(1, -1)], axis=1)
    (small_parts_g,) = _all_gather([small_parts], "gather_small_grads")
    tot = _sum_devices(small_parts_g[:, 0, :])
    offs = {}
    pos = 0
    for nm, width in (("loss", LANES), ("b_ada", 6 * d), ("norm1_w", d), ("dn_A_log", LANES), ("dn_dt_bias", LANES),
                      ("dn_norm_w", LANES), ("norm2_w", d), ("ffn_conv_b", 2 * D_FF), ("final_norm_w", d),
                      ("dn_conv_w", DN_CONV_WIDTH * DN_CONV_CH), ("ffn_conv_w", FFN_CONV_WIDTH * 2 * D_FF)):
        offs[nm] = (pos, width)
        pos += width
    seg = lambda nm: tot[:, offs[nm][0]:offs[nm][0] + offs[nm][1]]
    loss = tot[0, 0]
    g_b_ada = seg("b_ada")
    g_norm1 = seg("norm1_w")
    g_alog = seg("dn_A_log")[:, :DN_HEADS]
    g_dtb = seg("dn_dt_bias")[:, :DN_HEADS]
    g_dnw = seg("dn_norm_w")
    g_norm2 = seg("norm2_w")
    g_ffn_cb = seg("ffn_conv_b")
    g_fnw = seg("final_norm_w")
    g_dn_cw = lax.dynamic_slice(seg("dn_conv_w").reshape(DN_CONV_WIDTH, DN_CONV_CH), (0, me * n_dnc),
                                (DN_CONV_WIDTH, n_dnc))
    g_ffn_cw = lax.dynamic_slice(seg("ffn_conv_w").reshape(FFN_CONV_WIDTH, 2 * D_FF), (0, me * n_ffc),
                                 (FFN_CONV_WIDTH, n_ffc))

    dmod_all = small_parts_g[:, 0, offs["b_ada"][0]:offs["b_ada"][0] + 6 * d]
    g_w_ada = _ada_bwd(c_all, lax.dynamic_slice(dmod_all, (0, me * n_ada), (N_DEV, n_ada)))

    def pack(parts):
        flat = [p.reshape(1, -1) for p in parts]
        flat = [_pad_lanes(p, -(-p.shape[1] // LANES) * LANES) for p in flat]
        return jnp.concatenate(flat, axis=1), [p.shape[1] for p in flat]

    small_names_g = [g_b_ada, g_norm1, g_alog, g_dtb, g_dnw, g_norm2, g_ffn_cb, g_fnw, g_dn_cw, g_ffn_cw]
    small_w = [b_ada, norm1_w, dn_A_log, dn_dt_bias, dn_norm_w, norm2_w, ffn_conv_b, final_norm_w, dn_conv_w[0], ffn_conv_w[0]]
    small_m = [m_b_ada, m_norm1_w, m_dn_A_log, m_dn_dt_bias, m_dn_norm_w, m_norm2_w, m_ffn_conv_b, m_final_norm_w, m_dn_conv_w[0], m_ffn_conv_w[0]]
    small_v = [v_b_ada, v_norm1_w, v_dn_A_log, v_dn_dt_bias, v_dn_norm_w, v_norm2_w, v_ffn_conv_b, v_final_norm_w, v_dn_conv_w[0], v_ffn_conv_w[0]]
    pg, widths = pack(small_names_g)
    pw, _ = pack(small_w)
    pm, _ = pack(small_m)
    pv, _ = pack(small_v)
    s_delta, s_m, s_v = _adamw(pw, pg, pm, pv, "adamw_small")

    def unpack(flat):
        out, pos = [], 0
        for ref_arr, width in zip(small_w, widths):
            out.append(flat[:, pos:pos + ref_arr.size].reshape(ref_arr.shape))
            pos += width
        return out

    small_grads = [g.reshape(w_.shape) for g, w_ in zip(small_names_g, small_w)]
    small_delta, small_newm, small_newv = unpack(s_delta), unpack(s_m), unpack(s_v)

    ada_delta, ada_m, ada_v = _adamw(w_ada[0], g_w_ada, m_w_ada[0], v_w_ada[0], "adamw_ada")

    g_w_in_full = jnp.concatenate([g_w_main[:, OFF_SBQ:], g_w_main[:, OFF_DN:OFF_Z], g_w_main[:, OFF_Z:OFF_SBQ],
                                   g_w_ba[:, :2 * DN_HEADS], g_w_main[:, :OFF_DN]], axis=1)
    sends = [_cols_by_device(g_w_in_full).astype(bf16),
             _cols_by_device(g_w_psb).astype(bf16),
             g_w_pdn.reshape(N_DEV, DN_V_WIDTH // N_DEV, d).astype(bf16),
             g_w_o.reshape(N_DEV, d // N_DEV, d).astype(bf16),
             _cols_by_device(g_w_fin).astype(bf16),
             g_w_fout.reshape(N_DEV, D_FF // N_DEV, d).astype(bf16)]
    recv = _all_to_all(sends, "exchange_grads")
    big = {}
    for nm, parts, w_, m_, v_ in (("w_in", recv[0], w_in, m_w_in, v_w_in),
                                  ("w_proj_sb", recv[1], w_proj_sb, m_w_proj_sb, v_w_proj_sb),
                                  ("w_proj_dn", recv[2], w_proj_dn, m_w_proj_dn, v_w_proj_dn),
                                  ("w_out", recv[3], w_out, m_w_out, v_w_out),
                                  ("w_ffn_in", recv[4], w_ffn_in, m_w_ffn_in, v_w_ffn_in),
                                  ("w_ffn_out", recv[5], w_ffn_out, m_w_ffn_out, v_w_ffn_out)):
        big[nm] = [t[None] for t in _sum_adamw(parts, w_[0], m_[0], v_[0], "adamw_" + nm)]

    sg = dict(zip(["b_ada", "norm1_w", "dn_A_log", "dn_dt_bias", "dn_norm_w", "norm2_w", "ffn_conv_b", "final_norm_w",
                   "dn_conv_w", "ffn_conv_w"], range(10)))

    def small_out(table, nm):
        val = table[sg[nm]]
        return val[None] if nm in ("dn_conv_w", "ffn_conv_w") else val

    order = ["w_ada", "b_ada", "norm1_w", "w_in", "dn_conv_w", "dn_A_log", "dn_dt_bias", "dn_norm_w", "w_proj_sb",
             "w_proj_dn", "w_out", "norm2_w", "w_ffn_in", "ffn_conv_w", "ffn_conv_b", "w_ffn_out", "final_norm_w"]
    groups = []
    for k, small_table in enumerate((small_grads, small_delta, small_newm, small_newv)):
        row = []
        for nm in order:
            if nm == "w_ada":
                row.append((g_w_ada, ada_delta, ada_m, ada_v)[k][None])
            elif nm in big:
                row.append(big[nm][k])
            else:
                row.append(small_out(small_table, nm))
        groups.append(row)
    return (loss, grad_x[None], *groups[0], *groups[1], *groups[2], *groups[3])
```

```python
import functools

import jax
import jax.numpy as jnp
from jax import lax
from jax.experimental import pallas as pl
from jax.experimental.pallas import tpu as pltpu

f32 = jnp.float32
bf16 = jnp.bfloat16

D_MODEL = 1024
SB_HEADS = 8
SB_HEAD_DIM = 64
SB_WIDTH = SB_HEADS * SB_HEAD_DIM
SB_QBLOCK = 128
DN_HEADS = 8
DN_KEY_DIM = 64
DN_VAL_DIM = 128
DN_QK_WIDTH = DN_HEADS * DN_KEY_DIM
DN_V_WIDTH = DN_HEADS * DN_VAL_DIM
DN_CONV_CH = 2 * DN_QK_WIDTH + DN_V_WIDTH
DN_CONV_WIDTH = 4
DN_CHUNK = 64
D_FF = 2816
FFN_CONV_WIDTH = 3
NORM_EPS = 1e-6
L2_EPS = 1e-6
ADAM_LR = 0.001
ADAM_B1 = 0.9
ADAM_B2 = 0.999
ADAM_EPS = 1e-08
ADAM_WD = 0.01
ADAM_STEP = 10

N_DEV = 8
MESH = pl.DeviceIdType.MESH

LANES = 128
SUBLANES = 8
VMEM_LIMIT = 48 * 1024 * 1024

OFF_GA = 0
OFF_GB = D_MODEL
OFF_DN = 2 * D_MODEL
OFF_Z = OFF_DN + DN_CONV_CH
OFF_SBQ = OFF_Z + DN_V_WIDTH
OFF_SBK = OFF_SBQ + SB_WIDTH
OFF_SBV = OFF_SBK + SB_WIDTH
MAIN_WIDTH = OFF_SBV + SB_WIDTH

TM = 256
TCONV_R = 256
TCONV_C = 256
SB_PAIRS_PER_STEP = 2


def _cparams(sem=None):
    return pltpu.CompilerParams(dimension_semantics=sem, vmem_limit_bytes=VMEM_LIMIT)


def _pick(n, cands):
    for c in cands:
        if n % c == 0:
            return c
    return n


def _my_pos():
    return lax.axis_index("x"), lax.axis_index("y"), lax.axis_index("c")


def _flip(v, bit):
    return 1 - v if bit else v


def _all_gather(arrs, name):
    n = len(arrs)

    def body(*refs):
        ins, outs = refs[:n], refs[n:2 * n]
        send_sems, recv_sems, local_sems = refs[2 * n:]
        x, y, c = _my_pos()
        me, sibling = (x, y, c), (x, y, 1 - c)
        chips = [(1 - x, y), (x, 1 - y), (1 - x, 1 - y)]

        def slot(out, pos):
            return out.at[4 * pos[0] + 2 * pos[1] + pos[2]]

        def copy(a, k, block, to, src=None):
            return pltpu.make_async_remote_copy(
                src_ref=slot(outs[a], block) if src is None else src, dst_ref=slot(outs[a], block),
                send_sem=send_sems.at[a, k], recv_sem=recv_sems.at[a, k], device_id=to, device_id_type=MESH)

        started = []
        mine = []
        for a in range(n):
            cp = pltpu.make_async_copy(ins[a], slot(outs[a], me), local_sems.at[a])
            cp.start()
            mine.append(cp)
            first = [copy(a, 0, me, sibling, src=ins[a])]
            first += [copy(a, 1 + j, me, (*chip, c), src=ins[a]) for j, chip in enumerate(chips)]
            for cp in first:
                cp.start()
            started += first
        for a in range(n):
            for j, chip in enumerate(chips):
                copy(a, 1 + j, (*chip, c), me).wait_recv()
                fwd = copy(a, 4 + j, (*chip, c), sibling)
                fwd.start()
                started.append(fwd)
        for a in range(n):
            copy(a, 0, sibling, me).wait_recv()
            for j, chip in enumerate(chips):
                copy(a, 4 + j, (*chip, 1 - c), me).wait_recv()
        for cp in started:
            cp.wait_send()
        for cp in mine:
            cp.wait()

    any_spec = pl.BlockSpec(memory_space=pl.ANY)
    return pl.pallas_call(
        body, name=name,
        out_shape=[jax.ShapeDtypeStruct((N_DEV,) + a.shape, a.dtype) for a in arrs],
        in_specs=[any_spec] * n, out_specs=[any_spec] * n,
        scratch_shapes=[pltpu.SemaphoreType.DMA((n, 7)), pltpu.SemaphoreType.DMA((n, 7)),
                        pltpu.SemaphoreType.DMA((n,))],
    )(*arrs)


def _all_to_all(arrs, name):
    n = len(arrs)

    def body(*refs):
        ins, outs = refs[:n], refs[n:2 * n]
        send_sems, recv_sems, local_sems = refs[2 * n:]
        x, y, c = _my_pos()
        me_idx = 4 * x + 2 * y + c
        copies = []
        for a in range(n):
            cp = pltpu.make_async_copy(ins[a].at[me_idx], outs[a].at[me_idx], local_sems.at[a])
            cp.start()
            copies.append(cp)
        rdmas = []
        for a in range(n):
            for m in range(1, N_DEV):
                peer = (_flip(x, m & 4), _flip(y, m & 2), _flip(c, m & 1))
                peer_idx = 4 * peer[0] + 2 * peer[1] + peer[2]
                send = pltpu.make_async_remote_copy(
                    src_ref=ins[a].at[peer_idx], dst_ref=outs[a].at[me_idx],
                    send_sem=send_sems.at[a, m - 1], recv_sem=recv_sems.at[a, m - 1],
                    device_id=peer, device_id_type=MESH)
                send.start()
                recv = pltpu.make_async_remote_copy(
                    src_ref=ins[a].at[peer_idx], dst_ref=outs[a].at[peer_idx],
                    send_sem=send_sems.at[a, m - 1], recv_sem=recv_sems.at[a, m - 1],
                    device_id=peer, device_id_type=MESH)
                rdmas.append((send, recv))
        for send, recv in rdmas:
            recv.wait_recv()
        for send, recv in rdmas:
            send.wait_send()
        for cp in copies:
            cp.wait()

    any_spec = pl.BlockSpec(memory_space=pl.ANY)
    return pl.pallas_call(
        body, name=name,
        out_shape=[jax.ShapeDtypeStruct(a.shape, a.dtype) for a in arrs],
        in_specs=[any_spec] * n, out_specs=[any_spec] * n,
        scratch_shapes=[pltpu.SemaphoreType.DMA((n, 7)), pltpu.SemaphoreType.DMA((n, 7)),
                        pltpu.SemaphoreType.DMA((n,))],
    )(*arrs)


def _mm(a, b, *, ta=False, tb=False, out_dtype=f32, name):
    (k_dim, m_dim) = a.shape if ta else a.shape[::-1]
    (n_dim, kb_dim) = b.shape if tb else b.shape[::-1]
    assert k_dim == kb_dim, (a.shape, b.shape, ta, tb)
    tm = _pick(m_dim, (512, 256, 128))
    tn = _pick(n_dim, (512, 256, 128))
    tk = _pick(k_dim, (1024, 512, 256, 128))
    nk = k_dim // tk
    dims = (((0 if ta else 1,), (1 if tb else 0,)), ((), ()))

    def body(a_ref, b_ref, o_ref, acc_ref):
        k = pl.program_id(2)

        @pl.when(k == 0)
        def _():
            acc_ref[...] = jnp.zeros_like(acc_ref)

        acc_ref[...] += lax.dot_general(a_ref[...].astype(bf16), b_ref[...].astype(bf16), dims,
                                        preferred_element_type=f32)

        @pl.when(k == nk - 1)
        def _():
            o_ref[...] = acc_ref[...].astype(out_dtype)

    a_spec = pl.BlockSpec((tk, tm), lambda i, j, k: (k, i)) if ta else pl.BlockSpec((tm, tk), lambda i, j, k: (i, k))
    b_spec = pl.BlockSpec((tn, tk), lambda i, j, k: (j, k)) if tb else pl.BlockSpec((tk, tn), lambda i, j, k: (k, j))
    return pl.pallas_call(
        body, name=name, grid=(m_dim // tm, n_dim // tn, nk),
        in_specs=[a_spec, b_spec], out_specs=pl.BlockSpec((tm, tn), lambda i, j, k: (i, j)),
        out_shape=jax.ShapeDtypeStruct((m_dim, n_dim), out_dtype),
        scratch_shapes=[pltpu.VMEM((tm, tn), f32)],
        compiler_params=_cparams(("parallel", "parallel", "arbitrary")),
    )(a, b)


def _win(t):
    return t if isinstance(t, tuple) else (t, t.shape[1], 0)


def _tile_spec(width, cb, tm):
    return pl.BlockSpec((tm, width), lambda i: (i, cb))


def _param_spec(p):
    return pl.BlockSpec(p.shape, lambda i: (0, 0))


def _stage_fwd(f, params, tiles, out_dtypes, name):
    tiles = [_win(t) for t in tiles]
    rows = tiles[0][0].shape[0]
    tm = min(TM, rows)
    avals = jax.eval_shape(f, *[jax.ShapeDtypeStruct(p.shape, f32) for p in params],
                           *[jax.ShapeDtypeStruct((tm, w), f32) for _, w, _ in tiles])
    n_p, n_t = len(params), len(tiles)

    def body(*refs):
        p = [r[...] for r in refs[:n_p]]
        t = [r[...].astype(f32) for r in refs[n_p:n_p + n_t]]
        for o_ref, val in zip(refs[n_p + n_t:], f(*p, *t)):
            o_ref[...] = val.astype(o_ref.dtype)

    return pl.pallas_call(
        body, name=name, grid=(rows // tm,),
        in_specs=[_param_spec(p) for p in params] + [_tile_spec(w, cb, tm) for _, w, cb in tiles],
        out_specs=[_tile_spec(a.shape[1], 0, tm) for a in avals],
        out_shape=[jax.ShapeDtypeStruct((rows, a.shape[1]), dt) for a, dt in zip(avals, out_dtypes)],
        compiler_params=_cparams(("parallel",)),
    )(*params, *[t[0] for t in tiles])


def _stage_bwd(f, params, tiles, cts, grad_dtypes, name, residual=None):
    tiles = [_win(t) for t in tiles]
    rows = tiles[0][0].shape[0]
    tm = min(TM, rows)
    cts = [list(g) if isinstance(g, (list, tuple)) else [g] for g in cts]
    flat_cts = [a for g in cts for a in g]
    n_p, n_t, n_c = len(params), len(tiles), len(flat_cts)
    has_res = residual is not None
    want = [j for j, dt in enumerate(grad_dtypes) if dt is not None]

    def body(*refs):
        i = pl.program_id(0)
        p = [r[...] for r in refs[:n_p]]
        t = [r[...].astype(f32) for r in refs[n_p:n_p + n_t]]
        ct_vals = [r[...].astype(f32) for r in refs[n_p + n_t:n_p + n_t + n_c]]
        ct, at = [], 0
        for g in cts:
            ct.append(functools.reduce(jnp.add, ct_vals[at:at + len(g)]))
            at += len(g)
        ct = tuple(ct)
        pos = n_p + n_t + n_c
        res_ref = refs[pos] if has_res else None
        pos += 1 if has_res else 0
        dp_refs = refs[pos:pos + n_p]
        dt_refs = refs[pos + n_p:]
        _, vjp = jax.vjp(f, *p, *t)
        grads = vjp(ct)

        @pl.when(i == 0)
        def _():
            for r in dp_refs:
                r[...] = jnp.zeros_like(r)

        for r, g in zip(dp_refs, grads[:n_p]):
            r[...] += g
        for r, j in zip(dt_refs, want):
            g = grads[n_p + j]
            if has_res and j == residual[0]:
                g = g + res_ref[...].astype(f32)
            r[...] = g.astype(r.dtype)

    in_arrays = list(params) + [t[0] for t in tiles] + flat_cts
    in_specs = ([_param_spec(p) for p in params] + [_tile_spec(w, cb, tm) for _, w, cb in tiles]
                + [_tile_spec(c.shape[1], 0, tm) for c in flat_cts])
    if has_res:
        in_arrays.append(residual[1])
        in_specs.append(_tile_spec(residual[1].shape[1], 0, tm))
    out_shape = ([jax.ShapeDtypeStruct(p.shape, f32) for p in params]
                 + [jax.ShapeDtypeStruct((rows, tiles[j][1]), grad_dtypes[j]) for j in want])
    out_specs = [_param_spec(p) for p in params] + [_tile_spec(tiles[j][1], 0, tm) for j in want]
    outs = pl.pallas_call(
        body, name=name, grid=(rows // tm,), in_specs=in_specs, out_specs=out_specs, out_shape=out_shape,
        compiler_params=_cparams(("arbitrary",)),
    )(*in_arrays)
    return outs[:n_p], outs[n_p:]


def _rms(x, w):
    return x * lax.rsqrt(jnp.mean(x * x, axis=-1, keepdims=True) + NORM_EPS) * w


def _f_normmod(w, shift, scale, x):
    return (_rms(x, w) * (1.0 + scale) + shift,)


def _f_merge(ga, gb, pa, pb):
    return (jax.nn.sigmoid(ga) * pa + jax.nn.sigmoid(gb) * pb,)


def _f_residual(gate, x, branch):
    return (x + gate * branch,)


def _f_loss(gate, wf, x1, fo, target):
    y = _rms(x1 + gate * fo, wf)
    err = jnp.square(y - target)
    return (0.5 * jnp.sum(jnp.mean(err, axis=-1, keepdims=True), axis=0, keepdims=True),)


def _loss_and_grads(gate2, wf, x1, fo, target):
    rows, d = x1.shape
    tm = min(TM, rows)

    def body(g_ref, w_ref, x_ref, fo_ref, t_ref, loss_ref, dg_ref, dw_ref, dx_ref, dfo_ref):
        i = pl.program_id(0)
        (val,), vjp = jax.vjp(_f_loss, g_ref[...], w_ref[...], x_ref[...], fo_ref[...], t_ref[...])
        dg, dw, dx, dfo, _ = vjp((jnp.ones((1, 1), f32),))

        @pl.when(i == 0)
        def _():
            loss_ref[...] = jnp.zeros_like(loss_ref)
            dg_ref[...] = jnp.zeros_like(dg_ref)
            dw_ref[...] = jnp.zeros_like(dw_ref)

        loss_ref[...] += jnp.broadcast_to(val, loss_ref.shape)
        dg_ref[...] += dg
        dw_ref[...] += dw
        dx_ref[...] = dx
        dfo_ref[...] = dfo.astype(bf16)

    vec = pl.BlockSpec((1, d), lambda i: (0, 0))
    tile = pl.BlockSpec((tm, d), lambda i: (i, 0))
    return pl.pallas_call(
        body, name="loss_fwd_bwd", grid=(rows // tm,),
        in_specs=[vec, vec, tile, tile, tile],
        out_specs=[pl.BlockSpec((1, LANES), lambda i: (0, 0)), vec, vec, tile, tile],
        out_shape=[jax.ShapeDtypeStruct((1, LANES), f32), jax.ShapeDtypeStruct((1, d), f32),
                   jax.ShapeDtypeStruct((1, d), f32), jax.ShapeDtypeStruct((rows, d), f32),
                   jax.ShapeDtypeStruct((rows, d), bf16)],
        compiler_params=_cparams(("arbitrary",)),
    )(gate2, wf, x1, fo, target)


def _softplus(z):
    return jnp.maximum(z, 0.0) + jnp.log(1.0 + jnp.exp(-jnp.abs(z)))


def _split_dot(a, m):
    hi = a.astype(bf16)
    lo = (a - hi.astype(f32)).astype(bf16)
    return jnp.dot(hi, m, preferred_element_type=f32) + jnp.dot(lo, m, preferred_element_type=f32)


def _suffix_matrix(n):
    r = lax.broadcasted_iota(jnp.int32, (n, n), 0)
    c = lax.broadcasted_iota(jnp.int32, (n, n), 1)
    return (r > c).astype(bf16)


def _head_masks():
    lane = lax.broadcasted_iota(jnp.int32, (1, LANES), 1)
    return [(lane < SB_HEAD_DIM).astype(f32), (lane >= SB_HEAD_DIM).astype(f32)]


def _sb_prepare(proj):
    def f(k, v):
        lane = lax.broadcasted_iota(jnp.int32, (1, SB_WIDTH), 1)
        m0 = (jnp.bitwise_and(lane, LANES - 1) < SB_HEAD_DIM).astype(f32)
        m1 = 1.0 - m0
        return k, k * m0, k * m1, v, v * m0, v * m1

    wins = [(proj, SB_WIDTH, OFF_SBK // SB_WIDTH), (proj, SB_WIDTH, OFF_SBV // SB_WIDTH)]
    return _stage_fwd(f, [], wins, [bf16] * 6, "sb_prepare")


def _stack_heads(x):
    m0, m1 = _head_masks()
    return jnp.concatenate([x * m0, x * m1], axis=0)


def _sb_logits(qst, k, t_pos2, kb, bq, masked):
    z = lax.dot_general(qst, k, (((1,), (1,)), ((), ())), preferred_element_type=f32)
    l = -_softplus(z)
    if masked:
        s_pos = kb * bq + lax.broadcasted_iota(jnp.int32, (1, bq), 1)
        causal = s_pos < t_pos2
        l = jnp.where(causal, l, 0.0)
    else:
        causal = None
    return z, l, causal


def _sb_attention_fwd2(proj, k16, v0_16, v1_16):
    rows = proj.shape[0]
    bq = SB_QBLOCK
    nq = rows // bq
    assert nq <= LANES, "one lane per key block"
    npair = SB_WIDTH // LANES
    scale = SB_HEAD_DIM ** -0.5

    npp = SB_PAIRS_PER_STEP
    wq = npp * LANES

    def body(q_ref, k_ref, v0_ref, v1_ref, o_ref, runs_ref):
        qi = pl.program_id(1)
        pairs = [slice(pp * LANES, (pp + 1) * LANES) for pp in range(npp)]
        qst = [(_stack_heads(q_ref[:, s]) * scale).astype(bf16) for s in pairs]
        r = lax.broadcasted_iota(jnp.int32, (bq, 2 * bq), 0)
        c = lax.broadcasted_iota(jnp.int32, (bq, 2 * bq), 1)
        m2 = jnp.logical_or(r > c, c >= bq).astype(bf16)
        t_pos = qi * bq + lax.broadcasted_iota(jnp.int32, (bq, 1), 0)
        t_pos2 = jnp.concatenate([t_pos, t_pos], axis=0)
        lane = lax.broadcasted_iota(jnp.int32, (1, LANES), 1)
        runs_ref[...] = jnp.zeros_like(runs_ref)

        def tiles(kbs, carry, masked):
            jobs = [(pp, kb) for kb in kbs for pp in range(npp)]
            rows_k = [pl.ds(pl.multiple_of(kb * bq, bq), bq) for _, kb in jobs]
            zl = [_sb_logits(qst[pp], k_ref[rk, pairs[pp]], t_pos2, kb, bq, masked) for (pp, kb), rk in zip(jobs, rows_k)]
            cs = [_split_dot(l, m2) for _, l, _ in zl]
            run = [cr[0] for cr in carry]
            acc = [cr[1] for cr in carry]
            probs = []
            for (pp, kb), (z, l, causal), cs2 in zip(jobs, zl, cs):
                a = jnp.exp(z + l + cs2[:, :bq] + run[pp])
                if masked:
                    a = jnp.where(causal, a, 0.0)
                probs.append(a.astype(bf16))
                for hh in range(2):
                    cols = slice((2 * pp + hh) * LANES, (2 * pp + hh + 1) * LANES)
                    runs_ref[:, cols] = jnp.where(lane == kb, run[pp][hh * bq:(hh + 1) * bq], runs_ref[:, cols])
                run[pp] = run[pp] + cs2[:, bq:]
            for (pp, kb), rk, ab in zip(jobs, rows_k, probs):
                acc[pp] = (acc[pp] + jnp.dot(ab[:bq], v0_ref[rk, pairs[pp]], preferred_element_type=f32)
                           + jnp.dot(ab[bq:], v1_ref[rk, pairs[pp]], preferred_element_type=f32))
            return tuple(zip(run, acc))

        zero = (jnp.zeros((2 * bq, bq), f32), jnp.zeros((bq, LANES), f32))
        carry = tiles([qi], (zero,) * npp, True)
        carry = lax.fori_loop(0, qi // 2, lambda i, cr: tiles([qi - 1 - 2 * i, qi - 2 - 2 * i], cr, False), carry)
        carry = lax.cond(qi % 2 == 1, lambda cr: tiles([0], cr, False), lambda cr: cr, carry)
        for pp in range(npp):
            o_ref[:, pairs[pp]] = carry[pp][1]

    kv = pl.BlockSpec((rows, wq), lambda p, i: (0, p))
    return pl.pallas_call(
        body, name="sb_attn_fwd", grid=(npair // npp, nq),
        in_specs=[pl.BlockSpec((bq, wq), lambda p, i: (i, OFF_SBQ // wq + p)), kv, kv, kv],
        out_specs=[pl.BlockSpec((bq, wq), lambda p, i: (i, p)),
                   pl.BlockSpec((bq, 2 * wq), lambda p, i: (i, p))],
        out_shape=[jax.ShapeDtypeStruct((rows, SB_WIDTH), f32),
                   jax.ShapeDtypeStruct((rows, SB_HEADS * LANES), f32)],
        compiler_params=_cparams(("parallel", "arbitrary")),
    )(proj, k16, v0_16, v1_16)


def _sb_attention_bwd2(proj, k16, k0_16, k1_16, v16, runs, do):
    rows = proj.shape[0]
    bq = SB_QBLOCK
    nq = rows // bq
    npair = SB_WIDTH // LANES
    scale = SB_HEAD_DIM ** -0.5
    tn = (((0,), (0,)), ((), ()))
    nt = (((1,), (1,)), ((), ()))

    npp = SB_PAIRS_PER_STEP
    wq = npp * LANES

    def body(q_ref, k_ref, k0_ref, k1_ref, v_ref, runs_ref, do_ref, dq_ref, dk_ref, dv_ref):
        qi = pl.program_id(1)

        @pl.when(qi == 0)
        def _():
            dk_ref[...] = jnp.zeros_like(dk_ref)
            dv_ref[...] = jnp.zeros_like(dv_ref)

        pairs = [slice(pp * LANES, (pp + 1) * LANES) for pp in range(npp)]
        qst = [(_stack_heads(q_ref[:, s]) * scale).astype(bf16) for s in pairs]
        dost = [_stack_heads(do_ref[:, s]).astype(bf16) for s in pairs]
        runs = [jnp.concatenate([runs_ref[:, 2 * pp * LANES:(2 * pp + 1) * LANES],
                                 runs_ref[:, (2 * pp + 1) * LANES:(2 * pp + 2) * LANES]], axis=0) for pp in range(npp)]
        r = lax.broadcasted_iota(jnp.int32, (bq, 2 * bq), 0)
        c = lax.broadcasted_iota(jnp.int32, (bq, 2 * bq), 1)
        suffix_m = _suffix_matrix(bq)
        m2 = jnp.logical_or(r < c, c >= bq).astype(bf16)
        t_pos = qi * bq + lax.broadcasted_iota(jnp.int32, (bq, 1), 0)
        t_pos2 = jnp.concatenate([t_pos, t_pos], axis=0)
        lane = lax.broadcasted_iota(jnp.int32, (1, LANES), 1)

        def tiles(kbs, carry, masked):
            jobs = [(pp, kb) for kb in kbs for pp in range(npp)]
            rows_k = [pl.ds(pl.multiple_of(kb * bq, bq), bq) for _, kb in jobs]
            zl = [_sb_logits(qst[pp], k_ref[rk, pairs[pp]], t_pos2, kb, bq, masked) for (pp, kb), rk in zip(jobs, rows_k)]
            das = [lax.dot_general(dost[pp], v_ref[rk, pairs[pp]], nt, preferred_element_type=f32)
                   for (pp, kb), rk in zip(jobs, rows_k)]
            sticks = [_split_dot(l, suffix_m) for _, l, _ in zl]
            probs, ps = [], []
            for (pp, kb), (z, l, causal), stick, da in zip(jobs, zl, sticks, das):
                run = jnp.sum(jnp.where(lane == kb, runs[pp], 0.0), axis=1, keepdims=True)
                a = jnp.exp(z + l + stick + run)
                if masked:
                    a = jnp.where(causal, a, 0.0)
                probs.append(a.astype(bf16))
                ps.append(da * a)
            pcs = [_split_dot(p, m2) for p in ps]
            pref = [cr[0] for cr in carry]
            dq_acc = [cr[1] for cr in carry]
            dzs = []
            for (pp, kb), (z, l, causal), p, pc2 in zip(jobs, zl, ps, pcs):
                dz = p * jnp.exp(l) - jnp.exp(z + l) * (pc2[:, :bq] + pref[pp])
                if masked:
                    dz = jnp.where(causal, dz, 0.0)
                dzs.append(dz.astype(bf16))
                pref[pp] = pref[pp] + pc2[:, bq:]
            for (pp, kb), rk, dzb, ab in zip(jobs, rows_k, dzs, probs):
                cols = pairs[pp]
                dq_acc[pp] = (dq_acc[pp] + jnp.dot(dzb[:bq], k0_ref[rk, cols], preferred_element_type=f32)
                              + jnp.dot(dzb[bq:], k1_ref[rk, cols], preferred_element_type=f32))
                dk_ref[rk, cols] += lax.dot_general(dzb, qst[pp], tn, preferred_element_type=f32)
                dv_ref[rk, cols] += lax.dot_general(ab, dost[pp], tn, preferred_element_type=f32)
            return tuple(zip(pref, dq_acc))

        zero = (jnp.zeros((2 * bq, bq), f32), jnp.zeros((bq, LANES), f32))
        carry = lax.fori_loop(0, qi // 2, lambda i, cr: tiles([2 * i, 2 * i + 1], cr, False), (zero,) * npp)
        carry = lax.cond(qi % 2 == 1, lambda cr: tiles([qi - 1], cr, False), lambda cr: cr, carry)
        carry = tiles([qi], carry, True)
        for pp in range(npp):
            dq_ref[:, pairs[pp]] = (carry[pp][1] * scale).astype(dq_ref.dtype)

    blk = pl.BlockSpec((bq, wq), lambda p, i: (i, p))
    full = pl.BlockSpec((rows, wq), lambda p, i: (0, p))
    return pl.pallas_call(
        body, name="sb_attn_bwd", grid=(npair // npp, nq),
        in_specs=[pl.BlockSpec((bq, wq), lambda p, i: (i, OFF_SBQ // wq + p)), full, full, full, full,
                  pl.BlockSpec((bq, 2 * wq), lambda p, i: (i, p)), blk],
        out_specs=[blk, full, full],
        out_shape=[jax.ShapeDtypeStruct((rows, SB_WIDTH), bf16), jax.ShapeDtypeStruct((rows, SB_WIDTH), f32),
                   jax.ShapeDtypeStruct((rows, SB_WIDTH), f32)],
        compiler_params=_cparams(("parallel", "arbitrary")),
    )(proj, k16, k0_16, k1_16, v16, runs, do)


def _shift_down(x, prev8, j):
    if j == 0:
        return x
    r = pltpu.roll(x, j, axis=0)
    row8 = lax.broadcasted_iota(jnp.int32, prev8.shape, 0)
    head = jnp.where(row8 < j, pltpu.roll(prev8, j, axis=0), r[0:SUBLANES])
    return jnp.concatenate([head, r[SUBLANES:]], axis=0)


def _shift_up(x, next8, j):
    if j == 0:
        return x
    n = x.shape[0]
    r = pltpu.roll(x, n - j, axis=0)
    row8 = lax.broadcasted_iota(jnp.int32, next8.shape, 0)
    tail = jnp.where(row8 >= SUBLANES - j, pltpu.roll(next8, SUBLANES - j, axis=0), r[n - SUBLANES:n])
    return jnp.concatenate([r[:n - SUBLANES], tail], axis=0)


def _conv(x, prev8, w):
    k_taps = w.shape[0]
    out = x * w[k_taps - 1:k_taps, :]
    for j in range(1, k_taps):
        out = out + _shift_down(x, prev8, j) * w[k_taps - 1 - j:k_taps - j, :]
    return out


def _conv_tiles(rows):
    tr = min(TCONV_R, rows)
    return tr, rows // tr, tr // SUBLANES


def _prev_spec(tc, cb0, r8):
    return pl.BlockSpec((SUBLANES, tc), lambda j, i: (jnp.maximum(i * r8 - 1, 0), cb0 + j))


def _silu(x):
    return x * jax.nn.sigmoid(x)


def _dsilu(x):
    s = jax.nn.sigmoid(x)
    return s * (1.0 + x * (1.0 - s))


def _dn_conv_fwd(proj, w):
    rows = proj.shape[0]
    tr, nr, r8 = _conv_tiles(rows)
    tc = TCONV_C
    cb0 = OFF_DN // tc

    def body(x_ref, p_ref, w_ref, o_ref):
        prev = jnp.where(pl.program_id(1) == 0, 0.0, p_ref[...])
        o_ref[...] = _silu(_conv(x_ref[...], prev, w_ref[...]))

    return pl.pallas_call(
        body, name="dn_conv_fwd", grid=(DN_CONV_CH // tc, nr),
        in_specs=[pl.BlockSpec((tr, tc), lambda j, i: (i, cb0 + j)), _prev_spec(tc, cb0, r8),
                  pl.BlockSpec((DN_CONV_WIDTH, tc), lambda j, i: (0, j))],
        out_specs=pl.BlockSpec((tr, tc), lambda j, i: (i, j)),
        out_shape=jax.ShapeDtypeStruct((rows, DN_CONV_CH), f32),
        compiler_params=_cparams(("parallel", "parallel")),
    )(proj, proj, w)


def _dn_conv_bwd_act(proj, w, dact):
    rows = proj.shape[0]
    tr, nr, r8 = _conv_tiles(rows)
    tc = TCONV_C
    cb0 = OFF_DN // tc

    def body(x_ref, p_ref, w_ref, d_ref, o_ref):
        prev = jnp.where(pl.program_id(1) == 0, 0.0, p_ref[...])
        o_ref[...] = d_ref[...] * _dsilu(_conv(x_ref[...], prev, w_ref[...]))

    return pl.pallas_call(
        body, name="dn_conv_bwd_act", grid=(DN_CONV_CH // tc, nr),
        in_specs=[pl.BlockSpec((tr, tc), lambda j, i: (i, cb0 + j)), _prev_spec(tc, cb0, r8),
                  pl.BlockSpec((DN_CONV_WIDTH, tc), lambda j, i: (0, j)),
                  pl.BlockSpec((tr, tc), lambda j, i: (i, j))],
        out_specs=pl.BlockSpec((tr, tc), lambda j, i: (i, j)),
        out_shape=jax.ShapeDtypeStruct((rows, DN_CONV_CH), f32),
        compiler_params=_cparams(("parallel", "parallel")),
    )(proj, proj, w, dact)


def _ffn_conv_fwd(u_pre, w, b):
    rows = u_pre.shape[0]
    tr, nr, r8 = _conv_tiles(rows)
    tc = TCONV_C
    nct = D_FF // tc

    def body(xg_ref, pg_ref, xu_ref, pu_ref, wg_ref, wu_ref, bg_ref, bu_ref, o_ref):
        first = pl.program_id(1) == 0
        ug = _conv(xg_ref[...], jnp.where(first, 0.0, pg_ref[...]), wg_ref[...]) + bg_ref[...]
        uu = _conv(xu_ref[...], jnp.where(first, 0.0, pu_ref[...]), wu_ref[...]) + bu_ref[...]
        o_ref[...] = (_silu(ug) * uu).astype(o_ref.dtype)

    def x_spec(off):
        return pl.BlockSpec((tr, tc), lambda j, i: (i, off + j))

    def w_spec(k, off):
        return pl.BlockSpec((k, tc), lambda j, i: (0, off + j))

    return pl.pallas_call(
        body, name="ffn_conv_fwd", grid=(nct, nr),
        in_specs=[x_spec(0), _prev_spec(tc, 0, r8), x_spec(nct), _prev_spec(tc, nct, r8),
                  w_spec(FFN_CONV_WIDTH, 0), w_spec(FFN_CONV_WIDTH, nct), w_spec(1, 0), w_spec(1, nct)],
        out_specs=pl.BlockSpec((tr, tc), lambda j, i: (i, j)),
        out_shape=jax.ShapeDtypeStruct((rows, D_FF), bf16),
        compiler_params=_cparams(("parallel", "parallel")),
    )(u_pre, u_pre, u_pre, u_pre, w, w, b, b)


def _ffn_conv_bwd_act(u_pre, w, b, dact):
    rows = u_pre.shape[0]
    tr, nr, r8 = _conv_tiles(rows)
    tc = TCONV_C
    nct = D_FF // tc

    def body(xg_ref, pg_ref, xu_ref, pu_ref, wg_ref, wu_ref, bg_ref, bu_ref, d_ref,
             dug_ref, duu_ref, dbg_ref, dbu_ref):
        i = pl.program_id(1)
        first = i == 0
        ug = _conv(xg_ref[...], jnp.where(first, 0.0, pg_ref[...]), wg_ref[...]) + bg_ref[...]
        uu = _conv(xu_ref[...], jnp.where(first, 0.0, pu_ref[...]), wu_ref[...]) + bu_ref[...]
        d = d_ref[...]
        dug = d * uu * _dsilu(ug)
        duu = d * _silu(ug)
        dug_ref[...] = dug
        duu_ref[...] = duu

        @pl.when(first)
        def _():
            dbg_ref[...] = jnp.zeros_like(dbg_ref)
            dbu_ref[...] = jnp.zeros_like(dbu_ref)

        dbg_ref[...] += jnp.sum(dug, axis=0, keepdims=True)
        dbu_ref[...] += jnp.sum(duu, axis=0, keepdims=True)

    def x_spec(off):
        return pl.BlockSpec((tr, tc), lambda j, i: (i, off + j))

    def w_spec(k, off):
        return pl.BlockSpec((k, tc), lambda j, i: (0, off + j))

    tile = pl.BlockSpec((tr, tc), lambda j, i: (i, j))
    vec = pl.BlockSpec((1, tc), lambda j, i: (0, j))
    dug, duu, dbg, dbu = pl.pallas_call(
        body, name="ffn_conv_bwd_act", grid=(nct, nr),
        in_specs=[x_spec(0), _prev_spec(tc, 0, r8), x_spec(nct), _prev_spec(tc, nct, r8),
                  w_spec(FFN_CONV_WIDTH, 0), w_spec(FFN_CONV_WIDTH, nct), w_spec(1, 0), w_spec(1, nct), tile],
        out_specs=[tile, tile, vec, vec],
        out_shape=[jax.ShapeDtypeStruct((rows, D_FF), f32), jax.ShapeDtypeStruct((rows, D_FF), f32),
                   jax.ShapeDtypeStruct((1, D_FF), f32), jax.ShapeDtypeStruct((1, D_FF), f32)],
        compiler_params=_cparams(("parallel", "arbitrary")),
    )(u_pre, u_pre, u_pre, u_pre, w, w, b, b, dact)
    return dug, duu, dbg, dbu


def _conv_bwd(dy, x, x_cb0, w, name):
    rows, ch = dy.shape
    k_taps = w.shape[0]
    tr, nr, r8 = _conv_tiles(rows)
    tc = TCONV_C
    last8 = rows // SUBLANES - 1

    def body(dy_ref, nx_ref, x_ref, p_ref, w_ref, dx_ref, dw_ref):
        i = pl.program_id(1)
        dyv = dy_ref[...]
        nxt = jnp.where(i == nr - 1, 0.0, nx_ref[...])
        prev = jnp.where(i == 0, 0.0, p_ref[...])
        xv = x_ref[...].astype(f32)
        wv = w_ref[...]

        @pl.when(i == 0)
        def _():
            dw_ref[...] = jnp.zeros_like(dw_ref)

        dx = dyv * wv[k_taps - 1:k_taps, :]
        dw_ref[k_taps - 1:k_taps, :] += jnp.sum(dyv * xv, axis=0, keepdims=True)
        for j in range(1, k_taps):
            dx = dx + _shift_up(dyv, nxt, j) * wv[k_taps - 1 - j:k_taps - j, :]
            dw_ref[k_taps - 1 - j:k_taps - j, :] += jnp.sum(dyv * _shift_down(xv, prev, j), axis=0, keepdims=True)
        dx_ref[...] = dx.astype(dx_ref.dtype)

    tile = pl.BlockSpec((tr, tc), lambda j, i: (i, j))
    return pl.pallas_call(
        body, name=name, grid=(ch // tc, nr),
        in_specs=[tile,
                  pl.BlockSpec((SUBLANES, tc), lambda j, i: (jnp.minimum((i + 1) * r8, last8), j)),
                  pl.BlockSpec((tr, tc), lambda j, i: (i, x_cb0 + j)), _prev_spec(tc, x_cb0, r8),
                  pl.BlockSpec((k_taps, tc), lambda j, i: (0, j))],
        out_specs=[tile, pl.BlockSpec((k_taps, tc), lambda j, i: (0, j))],
        out_shape=[jax.ShapeDtypeStruct((rows, ch), bf16), jax.ShapeDtypeStruct((k_taps, ch), f32)],
        compiler_params=_cparams(("parallel", "arbitrary")),
    )(dy, dy, x, x, w)


def _hdot(a, b):
    return jnp.dot(a, b, preferred_element_type=f32, precision=lax.Precision.HIGH)


def _xdot(a, b):
    return jnp.dot(a, b, preferred_element_type=f32, precision=lax.Precision.HIGHEST)


def _bdot(a, b):
    return jnp.dot(a.astype(bf16), b.astype(bf16), preferred_element_type=f32)


def _bdot_nt(a, b):
    return lax.dot_general(a.astype(bf16), b.astype(bf16), (((1,), (1,)), ((), ())), preferred_element_type=f32)


def _bdot_tn(a, b):
    return lax.dot_general(a.astype(bf16), b.astype(bf16), (((0,), (0,)), ((), ())), preferred_element_type=f32)


GDN_GROUP = 4
GDN_NGROUPS = DN_HEADS // GDN_GROUP
GDN_ROWS = GDN_GROUP * DN_CHUNK
GDN_QK_LANES = GDN_GROUP * DN_KEY_DIM
GDN_LOGIT_LANE = DN_HEADS


def _inverse_impl(lows):
    n = lows[0].shape[0]
    r = lax.broadcasted_iota(jnp.int32, (n, n), 0)
    c = lax.broadcasted_iota(jnp.int32, (n, n), 1)
    eye = (r == c).astype(f32)
    blk = jnp.right_shift(r, 4) == jnp.right_shift(c, 4)
    d = [jnp.where(blk, low, 0.0) for low in lows]
    e = [low - x for low, x in zip(lows, d)]
    d2 = [_bdot(x, x) for x in d]
    d4 = [_bdot(x, x) for x in d2]
    d8 = [_bdot(x, x) for x in d4]
    p = [_bdot(eye - x, eye + y) for x, y in zip(d, d2)]
    p = [_bdot(x, eye + y) for x, y in zip(p, d4)]
    dinv = [_bdot(x, eye + y) for x, y in zip(p, d8)]
    nn = [_bdot(x, y) for x, y in zip(dinv, e)]
    n2 = [_bdot(x, x) for x in nn]
    ninv = [_bdot(eye - x, eye + y) for x, y in zip(nn, n2)]
    t = [_bdot(x, y) for x, y in zip(ninv, dinv)]
    for _ in range(2):
        res = [eye - x - _hdot(low, x) for low, x in zip(lows, t)]
        t = [x + _bdot(x, y) for x, y in zip(t, res)]
    return tuple(t)


@jax.custom_vjp
def _unit_lower_inverses(lows):
    return _inverse_impl(lows)


def _unit_lower_inverses_fwd(lows):
    t = _inverse_impl(lows)
    return t, t


def _unit_lower_inverses_bwd(t, ct):
    tn = (((0,), (0,)), ((), ()))
    nt = (((1,), (1,)), ((), ()))
    left = [lax.dot_general(x, g, tn, preferred_element_type=f32, precision=lax.Precision.HIGH) for x, g in zip(t, ct)]
    return (tuple(-lax.dot_general(x, y, nt, preferred_element_type=f32, precision=lax.Precision.HIGH)
                  for x, y in zip(left, t)),)


_unit_lower_inverses.defvjp(_unit_lower_inverses_fwd, _unit_lower_inverses_bwd)


def _gdn_chunk(a_log, dt_bias, norm_w, ba, *per_group):
    ng = GDN_NGROUPS
    qgs, kgs, vsts, zsts, states = [per_group[i * ng:(i + 1) * ng] for i in range(5)]
    groups = range(ng)
    n = GDN_ROWS
    r = lax.broadcasted_iota(jnp.int32, (n, n), 0)
    c = lax.broadcasted_iota(jnp.int32, (n, n), 1)
    same_head = jnp.right_shift(r, 6) == jnp.right_shift(c, 6)
    incl = jnp.logical_and(same_head, r >= c)
    strict = jnp.logical_and(same_head, r > c)
    eye = (r == c).astype(f32)
    ones = jnp.ones((n, n), f32)
    own_lanes = same_head.astype(f32)
    lane = lax.broadcasted_iota(jnp.int32, (1, LANES), 1)
    pick = lambda arr, idx: jnp.sum(jnp.where(lane == idx, arr, 0.0), axis=1, keepdims=True)
    heads = [[GDN_GROUP * g + h for h in range(GDN_GROUP)] for g in groups]
    rc = lax.broadcasted_iota(jnp.int32, (DN_CHUNK, DN_CHUNK), 0)
    cc = lax.broadcasted_iota(jnp.int32, (DN_CHUNK, DN_CHUNK), 1)

    g_all = -jnp.exp(a_log) * _softplus(ba + dt_bias)
    gc_all = _xdot((rc >= cc).astype(f32), g_all)
    gl_all = jnp.sum(g_all, axis=0, keepdims=True)
    beta = [jnp.concatenate([jax.nn.sigmoid(pick(ba, hd)) for hd in heads[g]], axis=0) for g in groups]
    gc = [jnp.concatenate([pick(gc_all, GDN_LOGIT_LANE + hd) for hd in heads[g]], axis=0) for g in groups]
    g_last = [jnp.concatenate([jnp.broadcast_to(pick(gl_all, GDN_LOGIT_LANE + hd), (DN_CHUNK, 1)) for hd in heads[g]],
                              axis=0) for g in groups]
    gr = [_hdot(ones, eye * gc[g]) for g in groups]
    decay = [jnp.where(incl, jnp.exp(jnp.where(incl, gc[g] - gr[g], 0.0)), 0.0) for g in groups]
    q = [jnp.concatenate([qgs[g]] * GDN_GROUP, axis=0) * own_lanes for g in groups]
    k = [jnp.concatenate([kgs[g]] * GDN_GROUP, axis=0) * own_lanes for g in groups]
    qn = [x * lax.rsqrt(jnp.sum(x * x, axis=1, keepdims=True) + L2_EPS) * (DN_KEY_DIM ** -0.5) for x in q]
    kn = [x * lax.rsqrt(jnp.sum(x * x, axis=1, keepdims=True) + L2_EPS) for x in k]
    kb = [kn[g] * beta[g] for g in groups]
    low = [jnp.where(strict, _bdot_nt(kb[g], kn[g]) * decay[g], 0.0) for g in groups]
    intra = [jnp.where(incl, _bdot_nt(qn[g], kn[g]) * decay[g], 0.0) for g in groups]
    t = _unit_lower_inverses(tuple(low))
    u = [_bdot(t[g], vsts[g] * beta[g]) for g in groups]
    w = [_bdot(t[g], kb[g] * jnp.exp(gc[g])) for g in groups]
    sb = [s.astype(bf16) for s in states]
    v_new = [u[g] - jnp.dot(w[g].astype(bf16), sb[g], preferred_element_type=f32) for g in groups]
    o = [jnp.dot((qn[g] * jnp.exp(gc[g])).astype(bf16), sb[g], preferred_element_type=f32) for g in groups]
    o = [o[g] + _bdot(intra[g], v_new[g]) for g in groups]
    new_state = [states[g] * jnp.exp(g_last[g]) + _bdot_tn(kn[g] * jnp.exp(g_last[g] - gc[g]), v_new[g])
                 for g in groups]
    o_n = [x * lax.rsqrt(jnp.mean(x * x, axis=1, keepdims=True) + NORM_EPS) * norm_w for x in o]
    return tuple(o_n[g] * _silu(zsts[g]) for g in groups) + tuple(new_state)


def _gdn_specs(rows, reverse):
    n = rows // DN_CHUNK
    idx = (lambda i: n - 1 - i) if reverse else (lambda i: i)
    vec = pl.BlockSpec((1, LANES), lambda i: (0, 0))
    qkv = pl.BlockSpec((DN_CHUNK, DN_CONV_CH), lambda i: (idx(i), 0))
    z = pl.BlockSpec((DN_CHUNK, DN_V_WIDTH), lambda i: (idx(i), OFF_Z // DN_V_WIDTH))
    ba = pl.BlockSpec((DN_CHUNK, LANES), lambda i: (idx(i), 0))
    wide = pl.BlockSpec((DN_CHUNK, DN_V_WIDTH), lambda i: (idx(i), 0))
    st = pl.BlockSpec((1, DN_HEADS * DN_KEY_DIM, LANES), lambda i: (idx(i), 0, 0))
    return n, vec, qkv, z, ba, wide, st


def _gdn_slices(grp):
    q = slice(grp * GDN_QK_LANES, (grp + 1) * GDN_QK_LANES)
    k = slice(DN_QK_WIDTH + grp * GDN_QK_LANES, DN_QK_WIDTH + (grp + 1) * GDN_QK_LANES)
    heads = [slice((GDN_GROUP * grp + h) * LANES, (GDN_GROUP * grp + h + 1) * LANES) for h in range(GDN_GROUP)]
    vs = [slice(2 * DN_QK_WIDTH + s.start, 2 * DN_QK_WIDTH + s.stop) for s in heads]
    return q, k, vs, heads


def _stack_cols(ref, cols):
    return jnp.concatenate([ref[:, s] for s in cols], axis=0)


def _gdn_operands(qkv_ref, z_ref, state_rows):
    sl = [_gdn_slices(grp) for grp in range(GDN_NGROUPS)]
    return ([qkv_ref[:, q] for q, _, _, _ in sl] + [qkv_ref[:, k] for _, k, _, _ in sl]
            + [_stack_cols(qkv_ref, vs) for _, _, vs, _ in sl] + [_stack_cols(z_ref, heads) for _, _, _, heads in sl]
            + [state_rows[grp * GDN_ROWS:(grp + 1) * GDN_ROWS, :] for grp in range(GDN_NGROUPS)])


def _gdn_fwd(a_log, dt_bias, norm_w, qkv_act, proj, ba):
    rows = qkv_act.shape[0]
    n, vec, qkv_s, z_s, ba_s, wide, st_s = _gdn_specs(rows, False)

    def body(al_ref, dt_ref, nw_ref, qkv_ref, z_ref, ba_ref, o_ref, st_ref, state):
        @pl.when(pl.program_id(0) == 0)
        def _():
            state[...] = jnp.zeros_like(state)

        st_ref[0] = state[...]
        out = _gdn_chunk(al_ref[...], dt_ref[...], nw_ref[...], ba_ref[...], *_gdn_operands(qkv_ref, z_ref, state))
        for grp in range(GDN_NGROUPS):
            _, _, _, heads = _gdn_slices(grp)
            for h, s in enumerate(heads):
                o_ref[:, s] = out[grp][h * DN_CHUNK:(h + 1) * DN_CHUNK].astype(o_ref.dtype)
            state[grp * GDN_ROWS:(grp + 1) * GDN_ROWS, :] = out[GDN_NGROUPS + grp]

    return pl.pallas_call(
        body, name="gdn_fwd", grid=(n,),
        in_specs=[vec, vec, vec, qkv_s, z_s, ba_s], out_specs=[wide, st_s],
        out_shape=[jax.ShapeDtypeStruct((rows, DN_V_WIDTH), bf16),
                   jax.ShapeDtypeStruct((n, DN_HEADS * DN_KEY_DIM, LANES), f32)],
        scratch_shapes=[pltpu.VMEM((DN_HEADS * DN_KEY_DIM, LANES), f32)],
        compiler_params=_cparams(("arbitrary",)),
    )(a_log, dt_bias, norm_w, qkv_act, proj, ba)


def _gdn_bwd(a_log, dt_bias, norm_w, qkv_act, proj, ba, states, do):
    rows = qkv_act.shape[0]
    n, vec, qkv_s, z_s, ba_s, wide, st_s = _gdn_specs(rows, True)

    def body(al_ref, dt_ref, nw_ref, qkv_ref, z_ref, ba_ref, st_ref, do_ref,
             dal_ref, ddt_ref, dnw_ref, dqkv_ref, dz_ref, dba_ref, dstate):
        @pl.when(pl.program_id(0) == 0)
        def _():
            dstate[...] = jnp.zeros_like(dstate)
            dal_ref[...] = jnp.zeros_like(dal_ref)
            ddt_ref[...] = jnp.zeros_like(ddt_ref)
            dnw_ref[...] = jnp.zeros_like(dnw_ref)

        ng = GDN_NGROUPS
        _, vjp = jax.vjp(_gdn_chunk, al_ref[...], dt_ref[...], nw_ref[...], ba_ref[...],
                         *_gdn_operands(qkv_ref, z_ref, st_ref[0]))
        cts = tuple(_stack_cols(do_ref, _gdn_slices(grp)[3]) for grp in range(ng))
        cts += tuple(dstate[grp * GDN_ROWS:(grp + 1) * GDN_ROWS, :] for grp in range(ng))
        grads = vjp(cts)
        dal_ref[...] += grads[0]
        ddt_ref[...] += grads[1]
        dnw_ref[...] += grads[2]
        dba_ref[...] = grads[3]
        dqs, dks, dvs, dzs, dss = [grads[4 + i * ng:4 + (i + 1) * ng] for i in range(5)]
        for grp in range(ng):
            q, k, vs, heads = _gdn_slices(grp)
            dqkv_ref[:, q] = dqs[grp]
            dqkv_ref[:, k] = dks[grp]
            for h, (sv, sh) in enumerate(zip(vs, heads)):
                rows_h = slice(h * DN_CHUNK, (h + 1) * DN_CHUNK)
                dqkv_ref[:, sv] = dvs[grp][rows_h]
                dz_ref[:, sh] = dzs[grp][rows_h].astype(dz_ref.dtype)
            dstate[grp * GDN_ROWS:(grp + 1) * GDN_ROWS, :] = dss[grp]

    return pl.pallas_call(
        body, name="gdn_bwd", grid=(n,),
        in_specs=[vec, vec, vec, qkv_s, z_s, ba_s, st_s, wide],
        out_specs=[vec, vec, vec, qkv_s, wide, ba_s],
        out_shape=[jax.ShapeDtypeStruct((1, LANES), f32)] * 3
        + [jax.ShapeDtypeStruct((rows, DN_CONV_CH), f32), jax.ShapeDtypeStruct((rows, DN_V_WIDTH), bf16),
           jax.ShapeDtypeStruct((rows, LANES), f32)],
        scratch_shapes=[pltpu.VMEM((DN_HEADS * DN_KEY_DIM, LANES), f32)],
        compiler_params=_cparams(("arbitrary",)),
    )(a_log, dt_bias, norm_w, qkv_act, proj, ba, states, do)


def _ada_fwd(c_all, w_loc, b_loc):
    def body(c_ref, w_ref, b_ref, o_ref):
        o_ref[...] = _bdot(_silu(c_ref[...]), w_ref[...]) + b_ref[...]

    return pl.pallas_call(body, name="ada_fwd", out_shape=jax.ShapeDtypeStruct((c_all.shape[0], w_loc.shape[1]), f32),
                          compiler_params=_cparams())(c_all, w_loc, b_loc)


def _ada_bwd(c_all, dmod_cols):
    def body(c_ref, d_ref, o_ref):
        o_ref[...] = _bdot_tn(_silu(c_ref[...]), d_ref[...])

    return pl.pallas_call(body, name="ada_bwd",
                          out_shape=jax.ShapeDtypeStruct((c_all.shape[1], dmod_cols.shape[1]), f32),
                          compiler_params=_cparams())(c_all, dmod_cols)


def _sum_devices(parts):
    def body(p_ref, o_ref):
        acc = p_ref[0:1, :]
        for d in range(1, N_DEV):
            acc = acc + p_ref[d:d + 1, :]
        o_ref[...] = acc

    return pl.pallas_call(body, name="sum_small", out_shape=jax.ShapeDtypeStruct((1, parts.shape[1]), f32),
                          compiler_params=_cparams())(parts)


def _adam_math(w, g, m, v):
    m2 = ADAM_B1 * m + (1.0 - ADAM_B1) * g
    v2 = ADAM_B2 * v + (1.0 - ADAM_B2) * jnp.square(g)
    m_hat = m2 / (1.0 - ADAM_B1 ** ADAM_STEP)
    v_hat = v2 / (1.0 - ADAM_B2 ** ADAM_STEP)
    delta = -ADAM_LR * (m_hat / (jnp.sqrt(v_hat) + ADAM_EPS) + ADAM_WD * w)
    return delta, m2, v2


def _row_tile(rows):
    return _pick(rows, (256, 128, 64, 32, 16, 8))


def _adamw(w, g, m, v, name):
    rows, cols = w.shape
    tr = _row_tile(rows)

    def body(w_ref, g_ref, m_ref, v_ref, d_ref, m2_ref, v2_ref):
        d_ref[...], m2_ref[...], v2_ref[...] = _adam_math(w_ref[...], g_ref[...], m_ref[...], v_ref[...])

    tile = pl.BlockSpec((tr, cols), lambda i: (i, 0))
    return pl.pallas_call(body, name=name, grid=(rows // tr,), in_specs=[tile] * 4, out_specs=[tile] * 3,
                          out_shape=[jax.ShapeDtypeStruct(w.shape, f32)] * 3,
                          compiler_params=_cparams(("parallel",)))(w, g, m, v)


def _sum_adamw(parts, w, m, v, name):
    rows, cols = w.shape
    tr = _row_tile(rows)

    def body(p_ref, w_ref, m_ref, v_ref, g_ref, d_ref, m2_ref, v2_ref):
        g = p_ref[0].astype(f32)
        for d in range(1, N_DEV):
            g = g + p_ref[d].astype(f32)
        g_ref[...] = g
        d_ref[...], m2_ref[...], v2_ref[...] = _adam_math(w_ref[...], g, m_ref[...], v_ref[...])

    tile = pl.BlockSpec((tr, cols), lambda i: (i, 0))
    return pl.pallas_call(body, name=name, grid=(rows // tr,),
                          in_specs=[pl.BlockSpec((N_DEV, tr, cols), lambda i: (0, i, 0)), tile, tile, tile],
                          out_specs=[tile] * 4, out_shape=[jax.ShapeDtypeStruct(w.shape, f32)] * 4,
                          compiler_params=_cparams(("parallel",)))(parts, w, m, v)


def _pad_lanes(a, width):
    return jnp.pad(a, ((0, 0), (0, width - a.shape[1])))


def _cols_by_device(full):
    r, c = full.shape
    return jnp.moveaxis(full.reshape(r, N_DEV, c // N_DEV), 1, 0)


def _cols_from_devices(parts):
    d, r, n = parts.shape
    return jnp.moveaxis(parts, 0, 1).reshape(r, d * n)


def kernel(x, c, w_ada, b_ada, norm1_w, w_in, dn_conv_w, dn_A_log, dn_dt_bias, dn_norm_w, w_proj_sb, w_proj_dn, w_out, norm2_w, w_ffn_in, ffn_conv_w, ffn_conv_b, w_ffn_out, final_norm_w, loss_target, m_w_ada, m_b_ada, m_norm1_w, m_w_in, m_dn_conv_w, m_dn_A_log, m_dn_dt_bias, m_dn_norm_w, m_w_proj_sb, m_w_proj_dn, m_w_out, m_norm2_w, m_w_ffn_in, m_ffn_conv_w, m_ffn_conv_b, m_w_ffn_out, m_final_norm_w, v_w_ada, v_b_ada, v_norm1_w, v_w_in, v_dn_conv_w, v_dn_A_log, v_dn_dt_bias, v_dn_norm_w, v_w_proj_sb, v_w_proj_dn, v_w_out, v_norm2_w, v_w_ffn_in, v_ffn_conv_w, v_ffn_conv_b, v_w_ffn_out, v_final_norm_w):
    d = D_MODEL
    me = 4 * lax.axis_index("x") + 2 * lax.axis_index("y") + lax.axis_index("c")
    xs = x[0]
    target = loss_target[0]
    n_ada = w_ada.shape[2]
    n_dnc = dn_conv_w.shape[2]
    n_ffc = ffn_conv_w.shape[2]

    small = jnp.concatenate([c, dn_conv_w[0].reshape(1, -1), ffn_conv_w[0].reshape(1, -1)], axis=1)
    small = _pad_lanes(small, -(-small.shape[1] // LANES) * LANES)
    (small_g, w_in_g, w_psb_g, w_pdn_g, w_out_g, w_fin_g, w_fout_g) = _all_gather(
        [small, w_in[0].astype(bf16), w_proj_sb[0].astype(bf16), w_proj_dn[0].astype(bf16),
         w_out[0].astype(bf16), w_ffn_in[0].astype(bf16), w_ffn_out[0].astype(bf16)], "gather_weights")
    small_g = small_g[:, 0, :]
    c_all = small_g[:, :d]
    dn_cw = _cols_from_devices(small_g[:, d:d + DN_CONV_WIDTH * n_dnc].reshape(N_DEV, DN_CONV_WIDTH, n_dnc))
    o2 = d + DN_CONV_WIDTH * n_dnc
    ffn_cw = _cols_from_devices(small_g[:, o2:o2 + FFN_CONV_WIDTH * n_ffc].reshape(N_DEV, FFN_CONV_WIDTH, n_ffc))

    w_in_full = _cols_from_devices(w_in_g)
    r_sb, r_dn, r_z = 3 * SB_WIDTH, 3 * SB_WIDTH + DN_CONV_CH, 3 * SB_WIDTH + DN_CONV_CH + DN_V_WIDTH
    r_g = r_z + 2 * DN_HEADS
    w_main = jnp.concatenate([w_in_full[:, r_g:], w_in_full[:, r_sb:r_dn], w_in_full[:, r_dn:r_z],
                              w_in_full[:, :r_sb]], axis=1)
    w_ba = _pad_lanes(w_in_full[:, r_z:r_g], LANES)
    w_psb = _cols_from_devices(w_psb_g)
    w_pdn = w_pdn_g.reshape(DN_V_WIDTH, d)
    w_o = w_out_g.reshape(d, d)
    w_fin = _cols_from_devices(w_fin_g)
    w_fout = w_fout_g.reshape(D_FF, d)

    b_loc = lax.dynamic_slice(b_ada, (0, me * n_ada), (1, n_ada))
    mod_part = _ada_fwd(c_all, w_ada[0], b_loc)
    (mod_g,) = _all_gather([mod_part], "gather_mod")
    mod = lax.dynamic_index_in_dim(mod_g, me, axis=1, keepdims=False).reshape(1, N_DEV * n_ada)
    shift1, scale1, gate1, shift2, scale2, gate2 = [mod[:, i * d:(i + 1) * d] for i in range(6)]

    logit_lanes = ((0, 0), (GDN_LOGIT_LANE, LANES - GDN_LOGIT_LANE - DN_HEADS))
    a_log = jnp.pad(dn_A_log, logit_lanes)
    dt_b = jnp.pad(dn_dt_bias, logit_lanes)

    (h,) = _stage_fwd(_f_normmod, [norm1_w, shift1, scale1], [xs], [bf16], "norm1_fwd")
    proj = _mm(h, w_main, name="in_proj")
    ba = _mm(h, w_ba, name="in_proj_ba")
    k16, k0_16, k1_16, v16, v0_16, v1_16 = _sb_prepare(proj)
    o_a, sb_runs = _sb_attention_fwd2(proj, k16, v0_16, v1_16)
    qkv_act = _dn_conv_fwd(proj, dn_cw)
    o_b, states = _gdn_fwd(a_log, dt_b, dn_norm_w, qkv_act, proj, ba)
    pa = _mm(o_a, w_psb, name="proj_sb")
    pb = _mm(o_b, w_pdn, name="proj_dn")
    gates = [(proj, d, OFF_GA // d), (proj, d, OFF_GB // d)]
    (merged,) = _stage_fwd(_f_merge, [], gates + [pa, pb], [bf16], "merge_fwd")
    ao = _mm(merged, w_o, name="out_proj")
    (x1,) = _stage_fwd(_f_residual, [gate1], [xs, ao], [f32], "resid1_fwd")
    (h2,) = _stage_fwd(_f_normmod, [norm2_w, shift2, scale2], [x1], [bf16], "norm2_fwd")
    u_pre = _mm(h2, w_fin, name="ffn_in")
    act = _ffn_conv_fwd(u_pre, ffn_cw, ffn_conv_b)
    fo = _mm(act, w_fout, name="ffn_out")

    loss_p, d_gate2, d_wf, dx2, dfo = _loss_and_grads(gate2, final_norm_w.reshape(1, d), x1, fo, target)
    dact = _mm(dfo, w_fout, tb=True, name="ffn_out_dx")
    g_w_fout = _mm(act, dfo, ta=True, name="ffn_out_dw")
    dug, duu, dbg, dbu = _ffn_conv_bwd_act(u_pre, ffn_cw, ffn_conv_b, dact)
    du = jnp.concatenate([dug, duu], axis=1)
    du_pre, d_ffn_cw = _conv_bwd(du, u_pre, 0, ffn_cw, "ffn_conv_bwd")
    dh2 = _mm(du_pre, w_fin, tb=True, name="ffn_in_dx")
    g_w_fin = _mm(h2, du_pre, ta=True, name="ffn_in_dw")
    (d_n2w, d_shift2, d_scale2), (dx1,) = _stage_bwd(
        _f_normmod, [norm2_w, shift2, scale2], [x1], [dh2], [f32], "norm2_bwd", residual=(0, dx2))
    (d_gate1,), (dao,) = _stage_bwd(_f_residual, [gate1], [xs, ao], [dx1], [None, bf16], "resid1_bwd")
    dmerged = _mm(dao, w_o, tb=True, name="out_proj_dx")
    g_w_o = _mm(merged, dao, ta=True, name="out_proj_dw")
    _, (dga, dgb, dpa, dpb) = _stage_bwd(_f_merge, [], gates + [pa, pb], [dmerged], [bf16] * 4, "merge_bwd")
    do_a = _mm(dpa, w_psb, tb=True, name="proj_sb_dx")
    g_w_psb = _mm(o_a, dpa, ta=True, name="proj_sb_dw")
    do_b = _mm(dpb, w_pdn, tb=True, name="proj_dn_dx")
    g_w_pdn = _mm(o_b, dpb, ta=True, name="proj_dn_dw")
    dq, dk, dv = _sb_attention_bwd2(proj, k16, k0_16, k1_16, v16, sb_runs, do_a)
    d_alog, d_dtb, d_dnw, dqkv_act, dz, dba = _gdn_bwd(a_log, dt_b, dn_norm_w, qkv_act, proj, ba, states, do_b)
    d_conv_out = _dn_conv_bwd_act(proj, dn_cw, dqkv_act)
    d_dn_pre, d_dn_cw = _conv_bwd(d_conv_out, proj, OFF_DN // TCONV_C, dn_cw, "dn_conv_bwd")
    dproj = jnp.concatenate([dga, dgb, d_dn_pre, dz, dq, dk.astype(bf16), dv.astype(bf16)], axis=1)
    dh = _mm(dproj, w_main, tb=True, name="in_proj_dx")
    dh_ba = _mm(dba, w_ba, tb=True, name="in_proj_ba_dx")
    g_w_main = _mm(h, dproj, ta=True, name="in_proj_dw")
    g_w_ba = _mm(h, dba, ta=True, name="in_proj_ba_dw")
    (d_n1w, d_shift1, d_scale1), (grad_x,) = _stage_bwd(
        _f_normmod, [norm1_w, shift1, scale1], [xs], [[dh, dh_ba]], [f32], "norm1_bwd", residual=(0, dx1))

    dmod = jnp.concatenate([d_shift1, d_scale1, d_gate1, d_shift2, d_scale2, d_gate2], axis=1)
    d_ffn_cb = jnp.concatenate([dbg, dbu], axis=1)
    small_parts = jnp.concatenate(
        [loss_p, dmod, d_n1w, d_alog, d_dtb, d_dnw, d_n2w, d_ffn_cb, d_wf,
         d_dn_cw.reshape(1, -1), d_ffn_cw.reshape(1, -1)], axis=1)
    (small_parts_g,) = _all_gather([small_parts], "gather_small_grads")
    tot = _sum_devices(small_parts_g[:, 0, :])
    offs = {}
    pos = 0
    for nm, width in (("loss", LANES), ("b_ada", 6 * d), ("norm1_w", d), ("dn_A_log", LANES), ("dn_dt_bias", LANES),
                      ("dn_norm_w", LANES), ("norm2_w", d), ("ffn_conv_b", 2 * D_FF), ("final_norm_w", d),
                      ("dn_conv_w", DN_CONV_WIDTH * DN_CONV_CH), ("ffn_conv_w", FFN_CONV_WIDTH * 2 * D_FF)):
        offs[nm] = (pos, width)
        pos += width
    seg = lambda nm: tot[:, offs[nm][0]:offs[nm][0] + offs[nm][1]]
    loss = tot[0, 0]
    g_b_ada = seg("b_ada")
    g_norm1 = seg("norm1_w")
    g_alog = seg("dn_A_log")[:, GDN_LOGIT_LANE:GDN_LOGIT_LANE + DN_HEADS]
    g_dtb = seg("dn_dt_bias")[:, GDN_LOGIT_LANE:GDN_LOGIT_LANE + DN_HEADS]
    g_dnw = seg("dn_norm_w")
    g_norm2 = seg("norm2_w")
    g_ffn_cb = seg("ffn_conv_b")
    g_fnw = seg("final_norm_w")
    g_dn_cw = lax.dynamic_slice(seg("dn_conv_w").reshape(DN_CONV_WIDTH, DN_CONV_CH), (0, me * n_dnc),
                                (DN_CONV_WIDTH, n_dnc))
    g_ffn_cw = lax.dynamic_slice(seg("ffn_conv_w").reshape(FFN_CONV_WIDTH, 2 * D_FF), (0, me * n_ffc),
                                 (FFN_CONV_WIDTH, n_ffc))

    dmod_all = small_parts_g[:, 0, offs["b_ada"][0]:offs["b_ada"][0] + 6 * d]
    g_w_ada = _ada_bwd(c_all, lax.dynamic_slice(dmod_all, (0, me * n_ada), (N_DEV, n_ada)))

    def pack(parts):
        flat = [p.reshape(1, -1) for p in parts]
        flat = [_pad_lanes(p, -(-p.shape[1] // LANES) * LANES) for p in flat]
        return jnp.concatenate(flat, axis=1), [p.shape[1] for p in flat]

    small_names_g = [g_b_ada, g_norm1, g_alog, g_dtb, g_dnw, g_norm2, g_ffn_cb, g_fnw, g_dn_cw, g_ffn_cw]
    small_w = [b_ada, norm1_w, dn_A_log, dn_dt_bias, dn_norm_w, norm2_w, ffn_conv_b, final_norm_w, dn_conv_w[0], ffn_conv_w[0]]
    small_m = [m_b_ada, m_norm1_w, m_dn_A_log, m_dn_dt_bias, m_dn_norm_w, m_norm2_w, m_ffn_conv_b, m_final_norm_w, m_dn_conv_w[0], m_ffn_conv_w[0]]
    small_v = [v_b_ada, v_norm1_w, v_dn_A_log, v_dn_dt_bias, v_dn_norm_w, v_norm2_w, v_ffn_conv_b, v_final_norm_w, v_dn_conv_w[0], v_ffn_conv_w[0]]
    pg, widths = pack(small_names_g)
    pw, _ = pack(small_w)
    pm, _ = pack(small_m)
    pv, _ = pack(small_v)
    s_delta, s_m, s_v = _adamw(pw, pg, pm, pv, "adamw_small")

    def unpack(flat):
        out, pos = [], 0
        for ref_arr, width in zip(small_w, widths):
            out.append(flat[:, pos:pos + ref_arr.size].reshape(ref_arr.shape))
            pos += width
        return out

    small_grads = [g.reshape(w_.shape) for g, w_ in zip(small_names_g, small_w)]
    small_delta, small_newm, small_newv = unpack(s_delta), unpack(s_m), unpack(s_v)

    ada_delta, ada_m, ada_v = _adamw(w_ada[0], g_w_ada, m_w_ada[0], v_w_ada[0], "adamw_ada")

    g_w_in_full = jnp.concatenate([g_w_main[:, OFF_SBQ:], g_w_main[:, OFF_DN:OFF_Z], g_w_main[:, OFF_Z:OFF_SBQ],
                                   g_w_ba[:, :2 * DN_HEADS], g_w_main[:, :OFF_DN]], axis=1)
    sends = [_cols_by_device(g_w_in_full).astype(bf16),
             _cols_by_device(g_w_psb).astype(bf16),
             g_w_pdn.reshape(N_DEV, DN_V_WIDTH // N_DEV, d).astype(bf16),
             g_w_o.reshape(N_DEV, d // N_DEV, d).astype(bf16),
             _cols_by_device(g_w_fin).astype(bf16),
             g_w_fout.reshape(N_DEV, D_FF // N_DEV, d).astype(bf16)]
    recv = _all_to_all(sends, "exchange_grads")
    big = {}
    for nm, parts, w_, m_, v_ in (("w_in", recv[0], w_in, m_w_in, v_w_in),
                                  ("w_proj_sb", recv[1], w_proj_sb, m_w_proj_sb, v_w_proj_sb),
                                  ("w_proj_dn", recv[2], w_proj_dn, m_w_proj_dn, v_w_proj_dn),
                                  ("w_out", recv[3], w_out, m_w_out, v_w_out),
                                  ("w_ffn_in", recv[4], w_ffn_in, m_w_ffn_in, v_w_ffn_in),
                                  ("w_ffn_out", recv[5], w_ffn_out, m_w_ffn_out, v_w_ffn_out)):
        big[nm] = [t[None] for t in _sum_adamw(parts, w_[0], m_[0], v_[0], "adamw_" + nm)]

    sg = dict(zip(["b_ada", "norm1_w", "dn_A_log", "dn_dt_bias", "dn_norm_w", "norm2_w", "ffn_conv_b", "final_norm_w",
                   "dn_conv_w", "ffn_conv_w"], range(10)))

    def small_out(table, nm):
        val = table[sg[nm]]
        return val[None] if nm in ("dn_conv_w", "ffn_conv_w") else val

    order = ["w_ada", "b_ada", "norm1_w", "w_in", "dn_conv_w", "dn_A_log", "dn_dt_bias", "dn_norm_w", "w_proj_sb",
             "w_proj_dn", "w_out", "norm2_w", "w_ffn_in", "ffn_conv_w", "ffn_conv_b", "w_ffn_out", "final_norm_w"]
    groups = []
    for k, small_table in enumerate((small_grads, small_delta, small_newm, small_newv)):
        row = []
        for nm in order:
            if nm == "w_ada":
                row.append((g_w_ada, ada_delta, ada_m, ada_v)[k][None])
            elif nm in big:
                row.append(big[nm][k])
            else:
                row.append(small_out(small_table, nm))
        groups.append(row)
    return (loss, grad_x[None], *groups[0], *groups[1], *groups[2], *groups[3])
```

```python
import functools

import jax
import jax.numpy as jnp
from jax import lax
from jax.experimental import pallas as pl
from jax.experimental.pallas import tpu as pltpu

f32 = jnp.float32
bf16 = jnp.bfloat16

D_MODEL = 1024
SB_HEADS = 8
SB_HEAD_DIM = 64
SB_WIDTH = SB_HEADS * SB_HEAD_DIM
SB_QBLOCK = 128
DN_HEADS = 8
DN_KEY_DIM = 64
DN_VAL_DIM = 128
DN_QK_WIDTH = DN_HEADS * DN_KEY_DIM
DN_V_WIDTH = DN_HEADS * DN_VAL_DIM
DN_CONV_CH = 2 * DN_QK_WIDTH + DN_V_WIDTH
DN_CONV_WIDTH = 4
DN_CHUNK = 64
D_FF = 2816
FFN_CONV_WIDTH = 3
NORM_EPS = 1e-6
L2_EPS = 1e-6
ADAM_LR = 0.001
ADAM_B1 = 0.9
ADAM_B2 = 0.999
ADAM_EPS = 1e-08
ADAM_WD = 0.01
ADAM_STEP = 10

N_DEV = 8
MESH = pl.DeviceIdType.MESH

LANES = 128
SUBLANES = 8
VMEM_LIMIT = 48 * 1024 * 1024

OFF_GA = 0
OFF_GB = D_MODEL
OFF_DN = 2 * D_MODEL
OFF_Z = OFF_DN + DN_CONV_CH
OFF_SBQ = OFF_Z + DN_V_WIDTH
OFF_SBK = OFF_SBQ + SB_WIDTH
OFF_SBV = OFF_SBK + SB_WIDTH
MAIN_WIDTH = OFF_SBV + SB_WIDTH

TM = 256
TCONV_R = 512
TCONV_C = 512
TCONV_FF = D_FF // 2
SB_PAIRS_PER_STEP = 2


def _cparams(sem=None):
    return pltpu.CompilerParams(dimension_semantics=sem, vmem_limit_bytes=VMEM_LIMIT)


def _pick(n, cands):
    for c in cands:
        if n % c == 0:
            return c
    return n


def _my_pos():
    return lax.axis_index("x"), lax.axis_index("y"), lax.axis_index("c")


def _flip(v, bit):
    return 1 - v if bit else v


def _all_gather(arrs, name):
    n = len(arrs)

    def body(*refs):
        ins, outs = refs[:n], refs[n:2 * n]
        send_sems, recv_sems, local_sems = refs[2 * n:]
        x, y, c = _my_pos()
        me, sibling = (x, y, c), (x, y, 1 - c)
        chips = [(1 - x, y), (x, 1 - y), (1 - x, 1 - y)]

        def slot(out, pos):
            return out.at[4 * pos[0] + 2 * pos[1] + pos[2]]

        def copy(a, k, block, to, src=None):
            return pltpu.make_async_remote_copy(
                src_ref=slot(outs[a], block) if src is None else src, dst_ref=slot(outs[a], block),
                send_sem=send_sems.at[a, k], recv_sem=recv_sems.at[a, k], device_id=to, device_id_type=MESH)

        started = []
        mine = []
        for a in range(n):
            cp = pltpu.make_async_copy(ins[a], slot(outs[a], me), local_sems.at[a])
            cp.start()
            mine.append(cp)
            first = [copy(a, 0, me, sibling, src=ins[a])]
            first += [copy(a, 1 + j, me, (*chip, c), src=ins[a]) for j, chip in enumerate(chips)]
            for cp in first:
                cp.start()
            started += first
        for a in range(n):
            for j, chip in enumerate(chips):
                copy(a, 1 + j, (*chip, c), me).wait_recv()
                fwd = copy(a, 4 + j, (*chip, c), sibling)
                fwd.start()
                started.append(fwd)
        for a in range(n):
            copy(a, 0, sibling, me).wait_recv()
            for j, chip in enumerate(chips):
                copy(a, 4 + j, (*chip, 1 - c), me).wait_recv()
        for cp in started:
            cp.wait_send()
        for cp in mine:
            cp.wait()

    any_spec = pl.BlockSpec(memory_space=pl.ANY)
    return pl.pallas_call(
        body, name=name,
        out_shape=[jax.ShapeDtypeStruct((N_DEV,) + a.shape, a.dtype) for a in arrs],
        in_specs=[any_spec] * n, out_specs=[any_spec] * n,
        scratch_shapes=[pltpu.SemaphoreType.DMA((n, 7)), pltpu.SemaphoreType.DMA((n, 7)),
                        pltpu.SemaphoreType.DMA((n,))],
    )(*arrs)


def _all_to_all(arrs, name):
    n = len(arrs)

    def body(*refs):
        ins, outs = refs[:n], refs[n:2 * n]
        send_sems, recv_sems, local_sems = refs[2 * n:]
        x, y, c = _my_pos()
        me_idx = 4 * x + 2 * y + c
        copies = []
        for a in range(n):
            cp = pltpu.make_async_copy(ins[a].at[me_idx], outs[a].at[me_idx], local_sems.at[a])
            cp.start()
            copies.append(cp)
        rdmas = []
        for a in range(n):
            for m in range(1, N_DEV):
                peer = (_flip(x, m & 4), _flip(y, m & 2), _flip(c, m & 1))
                peer_idx = 4 * peer[0] + 2 * peer[1] + peer[2]
                send = pltpu.make_async_remote_copy(
                    src_ref=ins[a].at[peer_idx], dst_ref=outs[a].at[me_idx],
                    send_sem=send_sems.at[a, m - 1], recv_sem=recv_sems.at[a, m - 1],
                    device_id=peer, device_id_type=MESH)
                send.start()
                recv = pltpu.make_async_remote_copy(
                    src_ref=ins[a].at[peer_idx], dst_ref=outs[a].at[peer_idx],
                    send_sem=send_sems.at[a, m - 1], recv_sem=recv_sems.at[a, m - 1],
                    device_id=peer, device_id_type=MESH)
                rdmas.append((send, recv))
        for send, recv in rdmas:
            recv.wait_recv()
        for send, recv in rdmas:
            send.wait_send()
        for cp in copies:
            cp.wait()

    any_spec = pl.BlockSpec(memory_space=pl.ANY)
    return pl.pallas_call(
        body, name=name,
        out_shape=[jax.ShapeDtypeStruct(a.shape, a.dtype) for a in arrs],
        in_specs=[any_spec] * n, out_specs=[any_spec] * n,
        scratch_shapes=[pltpu.SemaphoreType.DMA((n, 7)), pltpu.SemaphoreType.DMA((n, 7)),
                        pltpu.SemaphoreType.DMA((n,))],
    )(*arrs)


MM_BLOCK_BYTES = 4 * 1024 * 1024


def _mm_tiles(m_dim, n_dim, k_dim, a_bytes, b_bytes):
    tm = _pick(m_dim, (1024, 512, 256, 128))
    tn = _pick(n_dim, (512, 256, 128))
    tk = k_dim
    if k_dim % LANES == 0:
        units = k_dim // LANES
        fits = [u for u in range(1, units + 1) if units % u == 0
                and u * LANES * max(tm * a_bytes, tn * b_bytes) <= MM_BLOCK_BYTES]
        tk = max(fits) * LANES
    return tm, tn, tk


def _mm(a, b, *, ta=False, tb=False, name):
    (k_dim, m_dim) = a.shape if ta else a.shape[::-1]
    (n_dim, kb_dim) = b.shape if tb else b.shape[::-1]
    assert k_dim == kb_dim, (a.shape, b.shape, ta, tb)
    tm, tn, tk = _mm_tiles(m_dim, n_dim, k_dim, a.dtype.itemsize, b.dtype.itemsize)
    nk = k_dim // tk
    dims = (((0 if ta else 1,), (1 if tb else 0,)), ((), ()))

    def body(a_ref, b_ref, o_ref):
        part = lax.dot_general(a_ref[...].astype(bf16), b_ref[...].astype(bf16), dims, preferred_element_type=f32)
        if nk == 1:
            o_ref[...] = part
        else:
            k = pl.program_id(2)

            @pl.when(k == 0)
            def _():
                o_ref[...] = part

            @pl.when(k > 0)
            def _():
                o_ref[...] += part

    a_spec = pl.BlockSpec((tk, tm), lambda i, j, k: (k, i)) if ta else pl.BlockSpec((tm, tk), lambda i, j, k: (i, k))
    b_spec = pl.BlockSpec((tn, tk), lambda i, j, k: (j, k)) if tb else pl.BlockSpec((tk, tn), lambda i, j, k: (k, j))
    return pl.pallas_call(
        body, name=name, grid=(m_dim // tm, n_dim // tn, nk),
        in_specs=[a_spec, b_spec], out_specs=pl.BlockSpec((tm, tn), lambda i, j, k: (i, j)),
        out_shape=jax.ShapeDtypeStruct((m_dim, n_dim), f32),
        compiler_params=_cparams(("parallel", "parallel", "arbitrary")),
    )(a, b)


def _win(t):
    return t if isinstance(t, tuple) else (t, t.shape[1], 0)


def _tile_spec(width, cb, tm):
    return pl.BlockSpec((tm, width), lambda i: (i, cb))


def _param_spec(p):
    return pl.BlockSpec(p.shape, lambda i: (0, 0))


def _stage_fwd(f, params, tiles, out_dtypes, name):
    tiles = [_win(t) for t in tiles]
    rows = tiles[0][0].shape[0]
    tm = min(TM, rows)
    avals = jax.eval_shape(f, *[jax.ShapeDtypeStruct(p.shape, f32) for p in params],
                           *[jax.ShapeDtypeStruct((tm, w), f32) for _, w, _ in tiles])
    n_p, n_t = len(params), len(tiles)

    def body(*refs):
        p = [r[...] for r in refs[:n_p]]
        t = [r[...].astype(f32) for r in refs[n_p:n_p + n_t]]
        for o_ref, val in zip(refs[n_p + n_t:], f(*p, *t)):
            o_ref[...] = val.astype(o_ref.dtype)

    return pl.pallas_call(
        body, name=name, grid=(rows // tm,),
        in_specs=[_param_spec(p) for p in params] + [_tile_spec(w, cb, tm) for _, w, cb in tiles],
        out_specs=[_tile_spec(a.shape[1], 0, tm) for a in avals],
        out_shape=[jax.ShapeDtypeStruct((rows, a.shape[1]), dt) for a, dt in zip(avals, out_dtypes)],
        compiler_params=_cparams(("parallel",)),
    )(*params, *[t[0] for t in tiles])


def _stage_bwd(f, params, tiles, cts, grad_dtypes, name, residual=None):
    tiles = [_win(t) for t in tiles]
    rows = tiles[0][0].shape[0]
    tm = min(TM, rows)
    cts = [list(g) if isinstance(g, (list, tuple)) else [g] for g in cts]
    flat_cts = [a for g in cts for a in g]
    n_p, n_t, n_c = len(params), len(tiles), len(flat_cts)
    has_res = residual is not None
    want = [j for j, dt in enumerate(grad_dtypes) if dt is not None]

    def body(*refs):
        i = pl.program_id(0)
        p = [r[...] for r in refs[:n_p]]
        t = [r[...].astype(f32) for r in refs[n_p:n_p + n_t]]
        ct_vals = [r[...].astype(f32) for r in refs[n_p + n_t:n_p + n_t + n_c]]
        ct, at = [], 0
        for g in cts:
            ct.append(functools.reduce(jnp.add, ct_vals[at:at + len(g)]))
            at += len(g)
        ct = tuple(ct)
        pos = n_p + n_t + n_c
        res_ref = refs[pos] if has_res else None
        pos += 1 if has_res else 0
        dp_refs = refs[pos:pos + n_p]
        dt_refs = refs[pos + n_p:]
        _, vjp = jax.vjp(f, *p, *t)
        grads = vjp(ct)

        @pl.when(i == 0)
        def _():
            for r in dp_refs:
                r[...] = jnp.zeros_like(r)

        for r, g in zip(dp_refs, grads[:n_p]):
            r[...] += g
        for r, j in zip(dt_refs, want):
            g = grads[n_p + j]
            if has_res and j == residual[0]:
                g = g + res_ref[...].astype(f32)
            r[...] = g.astype(r.dtype)

    in_arrays = list(params) + [t[0] for t in tiles] + flat_cts
    in_specs = ([_param_spec(p) for p in params] + [_tile_spec(w, cb, tm) for _, w, cb in tiles]
                + [_tile_spec(c.shape[1], 0, tm) for c in flat_cts])
    if has_res:
        in_arrays.append(residual[1])
        in_specs.append(_tile_spec(residual[1].shape[1], 0, tm))
    out_shape = ([jax.ShapeDtypeStruct(p.shape, f32) for p in params]
                 + [jax.ShapeDtypeStruct((rows, tiles[j][1]), grad_dtypes[j]) for j in want])
    out_specs = [_param_spec(p) for p in params] + [_tile_spec(tiles[j][1], 0, tm) for j in want]
    outs = pl.pallas_call(
        body, name=name, grid=(rows // tm,), in_specs=in_specs, out_specs=out_specs, out_shape=out_shape,
        compiler_params=_cparams(("arbitrary",)),
    )(*in_arrays)
    return outs[:n_p], outs[n_p:]


def _rms(x, w):
    return x * lax.rsqrt(jnp.mean(x * x, axis=-1, keepdims=True) + NORM_EPS) * w


def _f_normmod(w, shift, scale, x):
    return (_rms(x, w) * (1.0 + scale) + shift,)


def _f_merge(ga, gb, pa, pb):
    return (jax.nn.sigmoid(ga) * pa + jax.nn.sigmoid(gb) * pb,)


def _f_residual(gate, x, branch):
    return (x + gate * branch,)


def _f_loss(gate, wf, x1, fo, target):
    y = _rms(x1 + gate * fo, wf)
    err = jnp.square(y - target)
    return (0.5 * jnp.sum(jnp.mean(err, axis=-1, keepdims=True), axis=0, keepdims=True),)


def _loss_and_grads(gate2, wf, x1, fo, target):
    rows, d = x1.shape
    tm = min(TM, rows)

    def body(g_ref, w_ref, x_ref, fo_ref, t_ref, loss_ref, dg_ref, dw_ref, dx_ref, dfo_ref):
        i = pl.program_id(0)
        (val,), vjp = jax.vjp(_f_loss, g_ref[...], w_ref[...], x_ref[...], fo_ref[...], t_ref[...])
        dg, dw, dx, dfo, _ = vjp((jnp.ones((1, 1), f32),))

        @pl.when(i == 0)
        def _():
            loss_ref[...] = jnp.zeros_like(loss_ref)
            dg_ref[...] = jnp.zeros_like(dg_ref)
            dw_ref[...] = jnp.zeros_like(dw_ref)

        loss_ref[...] += jnp.broadcast_to(val, loss_ref.shape)
        dg_ref[...] += dg
        dw_ref[...] += dw
        dx_ref[...] = dx
        dfo_ref[...] = dfo.astype(bf16)

    vec = pl.BlockSpec((1, d), lambda i: (0, 0))
    tile = pl.BlockSpec((tm, d), lambda i: (i, 0))
    return pl.pallas_call(
        body, name="loss_fwd_bwd", grid=(rows // tm,),
        in_specs=[vec, vec, tile, tile, tile],
        out_specs=[pl.BlockSpec((1, LANES), lambda i: (0, 0)), vec, vec, tile, tile],
        out_shape=[jax.ShapeDtypeStruct((1, LANES), f32), jax.ShapeDtypeStruct((1, d), f32),
                   jax.ShapeDtypeStruct((1, d), f32), jax.ShapeDtypeStruct((rows, d), f32),
                   jax.ShapeDtypeStruct((rows, d), bf16)],
        compiler_params=_cparams(("arbitrary",)),
    )(gate2, wf, x1, fo, target)


def _softplus(z):
    return jnp.maximum(z, 0.0) + jnp.log(1.0 + jnp.exp(-jnp.abs(z)))


def _split_dot(a, m):
    hi = a.astype(bf16)
    lo = (a - hi.astype(f32)).astype(bf16)
    return jnp.dot(hi, m, preferred_element_type=f32) + jnp.dot(lo, m, preferred_element_type=f32)


def _suffix_matrix(n):
    r = lax.broadcasted_iota(jnp.int32, (n, n), 0)
    c = lax.broadcasted_iota(jnp.int32, (n, n), 1)
    return (r > c).astype(bf16)


def _head_masks():
    lane = lax.broadcasted_iota(jnp.int32, (1, LANES), 1)
    return [(lane < SB_HEAD_DIM).astype(f32), (lane >= SB_HEAD_DIM).astype(f32)]


def _sb_prepare(proj):
    def f(k, v):
        lane = lax.broadcasted_iota(jnp.int32, (1, SB_WIDTH), 1)
        m0 = (jnp.bitwise_and(lane, LANES - 1) < SB_HEAD_DIM).astype(f32)
        m1 = 1.0 - m0
        return k, k * m0, k * m1, v, v * m0, v * m1

    wins = [(proj, SB_WIDTH, OFF_SBK // SB_WIDTH), (proj, SB_WIDTH, OFF_SBV // SB_WIDTH)]
    return _stage_fwd(f, [], wins, [bf16] * 6, "sb_prepare")


def _stack_heads(x):
    m0, m1 = _head_masks()
    return jnp.concatenate([x * m0, x * m1], axis=0)


def _sb_logits(qst, k, t_pos2, kb, bq, masked):
    z = lax.dot_general(qst, k, (((1,), (1,)), ((), ())), preferred_element_type=f32)
    l = -_softplus(z)
    if masked:
        s_pos = kb * bq + lax.broadcasted_iota(jnp.int32, (1, bq), 1)
        causal = s_pos < t_pos2
        l = jnp.where(causal, l, 0.0)
    else:
        causal = None
    return z, l, causal


def _sb_attention_fwd2(proj, k16, v0_16, v1_16):
    rows = proj.shape[0]
    bq = SB_QBLOCK
    nq = rows // bq
    assert nq <= LANES, "one lane per key block"
    npair = SB_WIDTH // LANES
    scale = SB_HEAD_DIM ** -0.5

    npp = SB_PAIRS_PER_STEP
    wq = npp * LANES

    def body(q_ref, k_ref, v0_ref, v1_ref, o_ref, runs_ref):
        qi = pl.program_id(1)
        pairs = [slice(pp * LANES, (pp + 1) * LANES) for pp in range(npp)]
        qst = [(_stack_heads(q_ref[:, s]) * scale).astype(bf16) for s in pairs]
        r = lax.broadcasted_iota(jnp.int32, (bq, 2 * bq), 0)
        c = lax.broadcasted_iota(jnp.int32, (bq, 2 * bq), 1)
        m2 = jnp.logical_or(r > c, c >= bq).astype(bf16)
        t_pos = qi * bq + lax.broadcasted_iota(jnp.int32, (bq, 1), 0)
        t_pos2 = jnp.concatenate([t_pos, t_pos], axis=0)
        lane = lax.broadcasted_iota(jnp.int32, (1, LANES), 1)
        runs_ref[...] = jnp.zeros_like(runs_ref)

        def tiles(kbs, carry, masked):
            jobs = [(pp, kb) for kb in kbs for pp in range(npp)]
            rows_k = [pl.ds(pl.multiple_of(kb * bq, bq), bq) for _, kb in jobs]
            zl = [_sb_logits(qst[pp], k_ref[rk, pairs[pp]], t_pos2, kb, bq, masked) for (pp, kb), rk in zip(jobs, rows_k)]
            cs = [_split_dot(l, m2) for _, l, _ in zl]
            run = [cr[0] for cr in carry]
            acc = [cr[1] for cr in carry]
            probs = []
            for (pp, kb), (z, l, causal), cs2 in zip(jobs, zl, cs):
                a = jnp.exp(z + l + cs2[:, :bq] + run[pp])
                if masked:
                    a = jnp.where(causal, a, 0.0)
                probs.append(a.astype(bf16))
                for hh in range(2):
                    cols = slice((2 * pp + hh) * LANES, (2 * pp + hh + 1) * LANES)
                    runs_ref[:, cols] = jnp.where(lane == kb, run[pp][hh * bq:(hh + 1) * bq], runs_ref[:, cols])
                run[pp] = run[pp] + cs2[:, bq:]
            for (pp, kb), rk, ab in zip(jobs, rows_k, probs):
                acc[pp] = (acc[pp] + jnp.dot(ab[:bq], v0_ref[rk, pairs[pp]], preferred_element_type=f32)
                           + jnp.dot(ab[bq:], v1_ref[rk, pairs[pp]], preferred_element_type=f32))
            return tuple(zip(run, acc))

        zero = (jnp.zeros((2 * bq, bq), f32), jnp.zeros((bq, LANES), f32))
        carry = tiles([qi], (zero,) * npp, True)
        carry = lax.fori_loop(0, qi // 2, lambda i, cr: tiles([qi - 1 - 2 * i, qi - 2 - 2 * i], cr, False), carry)
        carry = lax.cond(qi % 2 == 1, lambda cr: tiles([0], cr, False), lambda cr: cr, carry)
        for pp in range(npp):
            o_ref[:, pairs[pp]] = carry[pp][1]

    kv = pl.BlockSpec((rows, wq), lambda p, i: (0, p))
    return pl.pallas_call(
        body, name="sb_attn_fwd", grid=(npair // npp, nq),
        in_specs=[pl.BlockSpec((bq, wq), lambda p, i: (i, OFF_SBQ // wq + p)), kv, kv, kv],
        out_specs=[pl.BlockSpec((bq, wq), lambda p, i: (i, p)),
                   pl.BlockSpec((bq, 2 * wq), lambda p, i: (i, p))],
        out_shape=[jax.ShapeDtypeStruct((rows, SB_WIDTH), f32),
                   jax.ShapeDtypeStruct((rows, SB_HEADS * LANES), f32)],
        compiler_params=_cparams(("parallel", "arbitrary")),
    )(proj, k16, v0_16, v1_16)


def _sb_attention_bwd2(proj, k16, k0_16, k1_16, v16, runs, do):
    rows = proj.shape[0]
    bq = SB_QBLOCK
    nq = rows // bq
    npair = SB_WIDTH // LANES
    scale = SB_HEAD_DIM ** -0.5
    tn = (((0,), (0,)), ((), ()))
    nt = (((1,), (1,)), ((), ()))

    npp = SB_PAIRS_PER_STEP
    wq = npp * LANES

    def body(q_ref, k_ref, k0_ref, k1_ref, v_ref, runs_ref, do_ref, dq_ref, dk_ref, dv_ref):
        qi = pl.program_id(1)

        @pl.when(qi == 0)
        def _():
            dk_ref[...] = jnp.zeros_like(dk_ref)
            dv_ref[...] = jnp.zeros_like(dv_ref)

        pairs = [slice(pp * LANES, (pp + 1) * LANES) for pp in range(npp)]
        qst = [(_stack_heads(q_ref[:, s]) * scale).astype(bf16) for s in pairs]
        dost = [_stack_heads(do_ref[:, s]).astype(bf16) for s in pairs]
        runs = [jnp.concatenate([runs_ref[:, 2 * pp * LANES:(2 * pp + 1) * LANES],
                                 runs_ref[:, (2 * pp + 1) * LANES:(2 * pp + 2) * LANES]], axis=0) for pp in range(npp)]
        r = lax.broadcasted_iota(jnp.int32, (bq, 2 * bq), 0)
        c = lax.broadcasted_iota(jnp.int32, (bq, 2 * bq), 1)
        suffix_m = _suffix_matrix(bq)
        m2 = jnp.logical_or(r < c, c >= bq).astype(bf16)
        t_pos = qi * bq + lax.broadcasted_iota(jnp.int32, (bq, 1), 0)
        t_pos2 = jnp.concatenate([t_pos, t_pos], axis=0)
        lane = lax.broadcasted_iota(jnp.int32, (1, LANES), 1)

        def tiles(kbs, carry, masked):
            jobs = [(pp, kb) for kb in kbs for pp in range(npp)]
            rows_k = [pl.ds(pl.multiple_of(kb * bq, bq), bq) for _, kb in jobs]
            zl = [_sb_logits(qst[pp], k_ref[rk, pairs[pp]], t_pos2, kb, bq, masked) for (pp, kb), rk in zip(jobs, rows_k)]
            das = [lax.dot_general(dost[pp], v_ref[rk, pairs[pp]], nt, preferred_element_type=f32)
                   for (pp, kb), rk in zip(jobs, rows_k)]
            sticks = [_split_dot(l, suffix_m) for _, l, _ in zl]
            probs, ps = [], []
            for (pp, kb), (z, l, causal), stick, da in zip(jobs, zl, sticks, das):
                run = jnp.sum(jnp.where(lane == kb, runs[pp], 0.0), axis=1, keepdims=True)
                a = jnp.exp(z + l + stick + run)
                if masked:
                    a = jnp.where(causal, a, 0.0)
                probs.append(a.astype(bf16))
                ps.append(da * a)
            pcs = [_split_dot(p, m2) for p in ps]
            pref = [cr[0] for cr in carry]
            dq_acc = [cr[1] for cr in carry]
            dzs = []
            for (pp, kb), (z, l, causal), p, pc2 in zip(jobs, zl, ps, pcs):
                dz = p * jnp.exp(l) - jnp.exp(z + l) * (pc2[:, :bq] + pref[pp])
                if masked:
                    dz = jnp.where(causal, dz, 0.0)
                dzs.append(dz.astype(bf16))
                pref[pp] = pref[pp] + pc2[:, bq:]
            for (pp, kb), rk, dzb, ab in zip(jobs, rows_k, dzs, probs):
                cols = pairs[pp]
                dq_acc[pp] = (dq_acc[pp] + jnp.dot(dzb[:bq], k0_ref[rk, cols], preferred_element_type=f32)
                              + jnp.dot(dzb[bq:], k1_ref[rk, cols], preferred_element_type=f32))
                dk_ref[rk, cols] += lax.dot_general(dzb, qst[pp], tn, preferred_element_type=f32)
                dv_ref[rk, cols] += lax.dot_general(ab, dost[pp], tn, preferred_element_type=f32)
            return tuple(zip(pref, dq_acc))

        zero = (jnp.zeros((2 * bq, bq), f32), jnp.zeros((bq, LANES), f32))
        carry = lax.fori_loop(0, qi // 2, lambda i, cr: tiles([2 * i, 2 * i + 1], cr, False), (zero,) * npp)
        carry = lax.cond(qi % 2 == 1, lambda cr: tiles([qi - 1], cr, False), lambda cr: cr, carry)
        carry = tiles([qi], carry, True)
        for pp in range(npp):
            dq_ref[:, pairs[pp]] = (carry[pp][1] * scale).astype(dq_ref.dtype)

    blk = pl.BlockSpec((bq, wq), lambda p, i: (i, p))
    full = pl.BlockSpec((rows, wq), lambda p, i: (0, p))
    return pl.pallas_call(
        body, name="sb_attn_bwd", grid=(npair // npp, nq),
        in_specs=[pl.BlockSpec((bq, wq), lambda p, i: (i, OFF_SBQ // wq + p)), full, full, full, full,
                  pl.BlockSpec((bq, 2 * wq), lambda p, i: (i, p)), blk],
        out_specs=[blk, full, full],
        out_shape=[jax.ShapeDtypeStruct((rows, SB_WIDTH), bf16), jax.ShapeDtypeStruct((rows, SB_WIDTH), f32),
                   jax.ShapeDtypeStruct((rows, SB_WIDTH), f32)],
        compiler_params=_cparams(("parallel", "arbitrary")),
    )(proj, k16, k0_16, k1_16, v16, runs, do)


def _shift_down(x, prev8, j):
    if j == 0:
        return x
    r = pltpu.roll(x, j, axis=0)
    row8 = lax.broadcasted_iota(jnp.int32, prev8.shape, 0)
    head = jnp.where(row8 < j, pltpu.roll(prev8, j, axis=0), r[0:SUBLANES])
    return jnp.concatenate([head, r[SUBLANES:]], axis=0)


def _shift_up(x, next8, j):
    if j == 0:
        return x
    n = x.shape[0]
    r = pltpu.roll(x, n - j, axis=0)
    row8 = lax.broadcasted_iota(jnp.int32, next8.shape, 0)
    tail = jnp.where(row8 >= SUBLANES - j, pltpu.roll(next8, SUBLANES - j, axis=0), r[n - SUBLANES:n])
    return jnp.concatenate([r[:n - SUBLANES], tail], axis=0)


def _conv(x, prev8, w):
    k_taps = w.shape[0]
    out = x * w[k_taps - 1:k_taps, :]
    for j in range(1, k_taps):
        out = out + _shift_down(x, prev8, j) * w[k_taps - 1 - j:k_taps - j, :]
    return out


def _conv_tiles(rows, tr_max=TCONV_R):
    tr = min(tr_max, rows)
    return tr, rows // tr, tr // SUBLANES


def _prev_spec(tc, cb0, r8):
    return pl.BlockSpec((SUBLANES, tc), lambda j, i: (jnp.maximum(i * r8 - 1, 0), cb0 + j))


def _silu(x):
    return x * jax.nn.sigmoid(x)


def _dsilu(x):
    s = jax.nn.sigmoid(x)
    return s * (1.0 + x * (1.0 - s))


def _dn_conv_fwd(proj, w):
    rows = proj.shape[0]
    tr, nr, r8 = _conv_tiles(rows)
    tc = TCONV_C
    cb0 = OFF_DN // tc

    def body(x_ref, p_ref, w_ref, o_ref):
        prev = jnp.where(pl.program_id(1) == 0, 0.0, p_ref[...])
        o_ref[...] = _silu(_conv(x_ref[...], prev, w_ref[...]))

    return pl.pallas_call(
        body, name="dn_conv_fwd", grid=(DN_CONV_CH // tc, nr),
        in_specs=[pl.BlockSpec((tr, tc), lambda j, i: (i, cb0 + j)), _prev_spec(tc, cb0, r8),
                  pl.BlockSpec((DN_CONV_WIDTH, tc), lambda j, i: (0, j))],
        out_specs=pl.BlockSpec((tr, tc), lambda j, i: (i, j)),
        out_shape=jax.ShapeDtypeStruct((rows, DN_CONV_CH), f32),
        compiler_params=_cparams(("parallel", "parallel")),
    )(proj, proj, w)


def _dn_conv_bwd_act(proj, w, dact):
    rows = proj.shape[0]
    tr, nr, r8 = _conv_tiles(rows)
    tc = TCONV_C
    cb0 = OFF_DN // tc

    def body(x_ref, p_ref, w_ref, d_ref, o_ref):
        prev = jnp.where(pl.program_id(1) == 0, 0.0, p_ref[...])
        o_ref[...] = d_ref[...] * _dsilu(_conv(x_ref[...], prev, w_ref[...]))

    return pl.pallas_call(
        body, name="dn_conv_bwd_act", grid=(DN_CONV_CH // tc, nr),
        in_specs=[pl.BlockSpec((tr, tc), lambda j, i: (i, cb0 + j)), _prev_spec(tc, cb0, r8),
                  pl.BlockSpec((DN_CONV_WIDTH, tc), lambda j, i: (0, j)),
                  pl.BlockSpec((tr, tc), lambda j, i: (i, j))],
        out_specs=pl.BlockSpec((tr, tc), lambda j, i: (i, j)),
        out_shape=jax.ShapeDtypeStruct((rows, DN_CONV_CH), f32),
        compiler_params=_cparams(("parallel", "parallel")),
    )(proj, proj, w, dact)


def _ffn_conv_fwd(u_pre, w, b):
    rows = u_pre.shape[0]
    tr, nr, r8 = _conv_tiles(rows, TCONV_R // 2)
    tc = TCONV_FF
    nct = D_FF // tc

    def body(xg_ref, pg_ref, xu_ref, pu_ref, wg_ref, wu_ref, bg_ref, bu_ref, o_ref):
        first = pl.program_id(1) == 0
        ug = _conv(xg_ref[...], jnp.where(first, 0.0, pg_ref[...]), wg_ref[...]) + bg_ref[...]
        uu = _conv(xu_ref[...], jnp.where(first, 0.0, pu_ref[...]), wu_ref[...]) + bu_ref[...]
        o_ref[...] = (_silu(ug) * uu).astype(o_ref.dtype)

    def x_spec(off):
        return pl.BlockSpec((tr, tc), lambda j, i: (i, off + j))

    def w_spec(k, off):
        return pl.BlockSpec((k, tc), lambda j, i: (0, off + j))

    return pl.pallas_call(
        body, name="ffn_conv_fwd", grid=(nct, nr),
        in_specs=[x_spec(0), _prev_spec(tc, 0, r8), x_spec(nct), _prev_spec(tc, nct, r8),
                  w_spec(FFN_CONV_WIDTH, 0), w_spec(FFN_CONV_WIDTH, nct), w_spec(1, 0), w_spec(1, nct)],
        out_specs=pl.BlockSpec((tr, tc), lambda j, i: (i, j)),
        out_shape=jax.ShapeDtypeStruct((rows, D_FF), bf16),
        compiler_params=_cparams(("parallel", "parallel")),
    )(u_pre, u_pre, u_pre, u_pre, w, w, b, b)


def _ffn_conv_bwd_act(u_pre, w, b, dact):
    rows = u_pre.shape[0]
    tr, nr, r8 = _conv_tiles(rows, TCONV_R // 2)
    tc = TCONV_FF
    nct = D_FF // tc

    def body(xg_ref, pg_ref, xu_ref, pu_ref, wg_ref, wu_ref, bg_ref, bu_ref, d_ref,
             du_ref, dbg_ref, dbu_ref):
        i = pl.program_id(1)
        first = i == 0
        ug = _conv(xg_ref[...], jnp.where(first, 0.0, pg_ref[...]), wg_ref[...]) + bg_ref[...]
        uu = _conv(xu_ref[...], jnp.where(first, 0.0, pu_ref[...]), wu_ref[...]) + bu_ref[...]
        d = d_ref[...]
        dug = d * uu * _dsilu(ug)
        duu = d * _silu(ug)
        du_ref[0] = dug
        du_ref[1] = duu

        @pl.when(first)
        def _():
            dbg_ref[...] = jnp.zeros_like(dbg_ref)
            dbu_ref[...] = jnp.zeros_like(dbu_ref)

        dbg_ref[...] += jnp.sum(dug, axis=0, keepdims=True)
        dbu_ref[...] += jnp.sum(duu, axis=0, keepdims=True)

    def x_spec(off):
        return pl.BlockSpec((tr, tc), lambda j, i: (i, off + j))

    def w_spec(k, off):
        return pl.BlockSpec((k, tc), lambda j, i: (0, off + j))

    tile = pl.BlockSpec((tr, tc), lambda j, i: (i, j))
    vec = pl.BlockSpec((1, tc), lambda j, i: (0, j))
    return pl.pallas_call(
        body, name="ffn_conv_bwd_act", grid=(nct, nr),
        in_specs=[x_spec(0), _prev_spec(tc, 0, r8), x_spec(nct), _prev_spec(tc, nct, r8),
                  w_spec(FFN_CONV_WIDTH, 0), w_spec(FFN_CONV_WIDTH, nct), w_spec(1, 0), w_spec(1, nct), tile],
        out_specs=[pl.BlockSpec((2, tr, tc), lambda j, i: (0, i, j)), vec, vec],
        out_shape=[jax.ShapeDtypeStruct((2, rows, D_FF), f32),
                   jax.ShapeDtypeStruct((1, D_FF), f32), jax.ShapeDtypeStruct((1, D_FF), f32)],
        compiler_params=_cparams(("parallel", "arbitrary")),
    )(u_pre, u_pre, u_pre, u_pre, w, w, b, b, dact)


def _conv_bwd(dy, x, x_cb0, w, name):
    k_taps = w.shape[0]
    split = dy.ndim == 3
    rows = dy.shape[-2]
    ch = dy.shape[-1] * (2 if split else 1)
    tc = TCONV_FF if split else TCONV_C
    tr, nr, r8 = _conv_tiles(rows, TCONV_R // 2 if split else TCONV_R)
    per_half = dy.shape[-1] // tc
    last8 = rows // SUBLANES - 1

    def body(dy_ref, nx_ref, x_ref, p_ref, w_ref, dx_ref, dw_ref):
        i = pl.program_id(1)
        dyv = dy_ref[...]
        nxt = jnp.where(i == nr - 1, 0.0, nx_ref[...])
        prev = jnp.where(i == 0, 0.0, p_ref[...])
        xv = x_ref[...].astype(f32)
        wv = w_ref[...]

        @pl.when(i == 0)
        def _():
            dw_ref[...] = jnp.zeros_like(dw_ref)

        dx = dyv * wv[k_taps - 1:k_taps, :]
        dw_ref[k_taps - 1:k_taps, :] += jnp.sum(dyv * xv, axis=0, keepdims=True)
        for j in range(1, k_taps):
            dx = dx + _shift_up(dyv, nxt, j) * wv[k_taps - 1 - j:k_taps - j, :]
            dw_ref[k_taps - 1 - j:k_taps - j, :] += jnp.sum(dyv * _shift_down(xv, prev, j), axis=0, keepdims=True)
        dx_ref[...] = dx.astype(dx_ref.dtype)

    tile = pl.BlockSpec((tr, tc), lambda j, i: (i, j))
    if split:
        dy_spec = pl.BlockSpec((None, tr, tc), lambda j, i: (j // per_half, i, j % per_half))
        next_spec = pl.BlockSpec((None, SUBLANES, tc),
                                 lambda j, i: (j // per_half, jnp.minimum((i + 1) * r8, last8), j % per_half))
    else:
        dy_spec = tile
        next_spec = pl.BlockSpec((SUBLANES, tc), lambda j, i: (jnp.minimum((i + 1) * r8, last8), j))
    return pl.pallas_call(
        body, name=name, grid=(ch // tc, nr),
        in_specs=[dy_spec, next_spec,
                  pl.BlockSpec((tr, tc), lambda j, i: (i, x_cb0 + j)), _prev_spec(tc, x_cb0, r8),
                  pl.BlockSpec((k_taps, tc), lambda j, i: (0, j))],
        out_specs=[tile, pl.BlockSpec((k_taps, tc), lambda j, i: (0, j))],
        out_shape=[jax.ShapeDtypeStruct((rows, ch), bf16), jax.ShapeDtypeStruct((k_taps, ch), f32)],
        compiler_params=_cparams(("parallel", "arbitrary")),
    )(dy, dy, x, x, w)


def _hdot(a, b):
    return jnp.dot(a, b, preferred_element_type=f32, precision=lax.Precision.HIGH)


def _xdot(a, b):
    return jnp.dot(a, b, preferred_element_type=f32, precision=lax.Precision.HIGHEST)


def _bdot(a, b):
    return jnp.dot(a.astype(bf16), b.astype(bf16), preferred_element_type=f32)


def _bdot_nt(a, b):
    return lax.dot_general(a.astype(bf16), b.astype(bf16), (((1,), (1,)), ((), ())), preferred_element_type=f32)


def _bdot_tn(a, b):
    return lax.dot_general(a.astype(bf16), b.astype(bf16), (((0,), (0,)), ((), ())), preferred_element_type=f32)


GDN_GROUP = 4
GDN_NGROUPS = DN_HEADS // GDN_GROUP
GDN_ROWS = GDN_GROUP * DN_CHUNK
GDN_QK_LANES = GDN_GROUP * DN_KEY_DIM
GDN_LOGIT_LANE = DN_HEADS


def _inverse_impl(lows):
    n = lows[0].shape[0]
    r = lax.broadcasted_iota(jnp.int32, (n, n), 0)
    c = lax.broadcasted_iota(jnp.int32, (n, n), 1)
    eye = (r == c).astype(f32)
    blk = jnp.right_shift(r, 4) == jnp.right_shift(c, 4)
    d = [jnp.where(blk, low, 0.0) for low in lows]
    e = [low - x for low, x in zip(lows, d)]
    d2 = [_bdot(x, x) for x in d]
    d4 = [_bdot(x, x) for x in d2]
    d8 = [_bdot(x, x) for x in d4]
    p = [_bdot(eye - x, eye + y) for x, y in zip(d, d2)]
    p = [_bdot(x, eye + y) for x, y in zip(p, d4)]
    dinv = [_bdot(x, eye + y) for x, y in zip(p, d8)]
    nn = [_bdot(x, y) for x, y in zip(dinv, e)]
    n2 = [_bdot(x, x) for x in nn]
    ninv = [_bdot(eye - x, eye + y) for x, y in zip(nn, n2)]
    t = [_bdot(x, y) for x, y in zip(ninv, dinv)]
    for _ in range(2):
        res = [eye - x - _hdot(low, x) for low, x in zip(lows, t)]
        t = [x + _bdot(x, y) for x, y in zip(t, res)]
    return tuple(t)


@jax.custom_vjp
def _unit_lower_inverses(lows):
    return _inverse_impl(lows)


def _unit_lower_inverses_fwd(lows):
    t = _inverse_impl(lows)
    return t, t


def _unit_lower_inverses_bwd(t, ct):
    tn = (((0,), (0,)), ((), ()))
    nt = (((1,), (1,)), ((), ()))
    left = [lax.dot_general(x, g, tn, preferred_element_type=f32, precision=lax.Precision.HIGH) for x, g in zip(t, ct)]
    return (tuple(-lax.dot_general(x, y, nt, preferred_element_type=f32, precision=lax.Precision.HIGH)
                  for x, y in zip(left, t)),)


_unit_lower_inverses.defvjp(_unit_lower_inverses_fwd, _unit_lower_inverses_bwd)


def _gdn_chunk(a_log, dt_bias, norm_w, ba, *per_group):
    ng = GDN_NGROUPS
    qgs, kgs, vsts, zsts, states = [per_group[i * ng:(i + 1) * ng] for i in range(5)]
    groups = range(ng)
    n = GDN_ROWS
    r = lax.broadcasted_iota(jnp.int32, (n, n), 0)
    c = lax.broadcasted_iota(jnp.int32, (n, n), 1)
    same_head = jnp.right_shift(r, 6) == jnp.right_shift(c, 6)
    incl = jnp.logical_and(same_head, r >= c)
    strict = jnp.logical_and(same_head, r > c)
    eye = (r == c).astype(f32)
    ones = jnp.ones((n, n), f32)
    own_lanes = same_head.astype(f32)
    lane = lax.broadcasted_iota(jnp.int32, (1, LANES), 1)
    pick = lambda arr, idx: jnp.sum(jnp.where(lane == idx, arr, 0.0), axis=1, keepdims=True)
    heads = [[GDN_GROUP * g + h for h in range(GDN_GROUP)] for g in groups]
    rc = lax.broadcasted_iota(jnp.int32, (DN_CHUNK, DN_CHUNK), 0)
    cc = lax.broadcasted_iota(jnp.int32, (DN_CHUNK, DN_CHUNK), 1)

    g_all = -jnp.exp(a_log) * _softplus(ba + dt_bias)
    gc_all = _xdot((rc >= cc).astype(f32), g_all)
    gl_all = jnp.sum(g_all, axis=0, keepdims=True)
    beta = [jnp.concatenate([jax.nn.sigmoid(pick(ba, hd)) for hd in heads[g]], axis=0) for g in groups]
    gc = [jnp.concatenate([pick(gc_all, GDN_LOGIT_LANE + hd) for hd in heads[g]], axis=0) for g in groups]
    g_last = [jnp.concatenate([jnp.broadcast_to(pick(gl_all, GDN_LOGIT_LANE + hd), (DN_CHUNK, 1)) for hd in heads[g]],
                              axis=0) for g in groups]
    gr = [_hdot(ones, eye * gc[g]) for g in groups]
    decay = [jnp.where(incl, jnp.exp(jnp.where(incl, gc[g] - gr[g], 0.0)), 0.0) for g in groups]
    q = [jnp.concatenate([qgs[g]] * GDN_GROUP, axis=0) * own_lanes for g in groups]
    k = [jnp.concatenate([kgs[g]] * GDN_GROUP, axis=0) * own_lanes for g in groups]
    qn = [x * lax.rsqrt(jnp.sum(x * x, axis=1, keepdims=True) + L2_EPS) * (DN_KEY_DIM ** -0.5) for x in q]
    kn = [x * lax.rsqrt(jnp.sum(x * x, axis=1, keepdims=True) + L2_EPS) for x in k]
    kb = [kn[g] * beta[g] for g in groups]
    low = [jnp.where(strict, _bdot_nt(kb[g], kn[g]) * decay[g], 0.0) for g in groups]
    intra = [jnp.where(incl, _bdot_nt(qn[g], kn[g]) * decay[g], 0.0) for g in groups]
    t = _unit_lower_inverses(tuple(low))
    u = [_bdot(t[g], vsts[g] * beta[g]) for g in groups]
    w = [_bdot(t[g], kb[g] * jnp.exp(gc[g])) for g in groups]
    sb = [s.astype(bf16) for s in states]
    v_new = [u[g] - jnp.dot(w[g].astype(bf16), sb[g], preferred_element_type=f32) for g in groups]
    o = [jnp.dot((qn[g] * jnp.exp(gc[g])).astype(bf16), sb[g], preferred_element_type=f32) for g in groups]
    o = [o[g] + _bdot(intra[g], v_new[g]) for g in groups]
    new_state = [states[g] * jnp.exp(g_last[g]) + _bdot_tn(kn[g] * jnp.exp(g_last[g] - gc[g]), v_new[g])
                 for g in groups]
    o_n = [x * lax.rsqrt(jnp.mean(x * x, axis=1, keepdims=True) + NORM_EPS) * norm_w for x in o]
    return tuple(o_n[g] * _silu(zsts[g]) for g in groups) + tuple(new_state)


def _gdn_specs(rows, reverse):
    n = rows // DN_CHUNK
    idx = (lambda i: n - 1 - i) if reverse else (lambda i: i)
    vec = pl.BlockSpec((1, LANES), lambda i: (0, 0))
    qkv = pl.BlockSpec((DN_CHUNK, DN_CONV_CH), lambda i: (idx(i), 0))
    z = pl.BlockSpec((DN_CHUNK, DN_V_WIDTH), lambda i: (idx(i), OFF_Z // DN_V_WIDTH))
    ba = pl.BlockSpec((DN_CHUNK, LANES), lambda i: (idx(i), 0))
    wide = pl.BlockSpec((DN_CHUNK, DN_V_WIDTH), lambda i: (idx(i), 0))
    st = pl.BlockSpec((1, DN_HEADS * DN_KEY_DIM, LANES), lambda i: (idx(i), 0, 0))
    return n, vec, qkv, z, ba, wide, st


def _gdn_slices(grp):
    q = slice(grp * GDN_QK_LANES, (grp + 1) * GDN_QK_LANES)
    k = slice(DN_QK_WIDTH + grp * GDN_QK_LANES, DN_QK_WIDTH + (grp + 1) * GDN_QK_LANES)
    heads = [slice((GDN_GROUP * grp + h) * LANES, (GDN_GROUP * grp + h + 1) * LANES) for h in range(GDN_GROUP)]
    vs = [slice(2 * DN_QK_WIDTH + s.start, 2 * DN_QK_WIDTH + s.stop) for s in heads]
    return q, k, vs, heads


def _stack_cols(ref, cols):
    return jnp.concatenate([ref[:, s] for s in cols], axis=0)


def _gdn_operands(qkv_ref, z_ref, state_rows):
    sl = [_gdn_slices(grp) for grp in range(GDN_NGROUPS)]
    return ([qkv_ref[:, q] for q, _, _, _ in sl] + [qkv_ref[:, k] for _, k, _, _ in sl]
            + [_stack_cols(qkv_ref, vs) for _, _, vs, _ in sl] + [_stack_cols(z_ref, heads) for _, _, _, heads in sl]
            + [state_rows[grp * GDN_ROWS:(grp + 1) * GDN_ROWS, :] for grp in range(GDN_NGROUPS)])


def _gdn_fwd(a_log, dt_bias, norm_w, qkv_act, proj, ba):
    rows = qkv_act.shape[0]
    n, vec, qkv_s, z_s, ba_s, wide, st_s = _gdn_specs(rows, False)

    def body(al_ref, dt_ref, nw_ref, qkv_ref, z_ref, ba_ref, o_ref, st_ref, state):
        @pl.when(pl.program_id(0) == 0)
        def _():
            state[...] = jnp.zeros_like(state)

        st_ref[0] = state[...]
        out = _gdn_chunk(al_ref[...], dt_ref[...], nw_ref[...], ba_ref[...], *_gdn_operands(qkv_ref, z_ref, state))
        for grp in range(GDN_NGROUPS):
            _, _, _, heads = _gdn_slices(grp)
            for h, s in enumerate(heads):
                o_ref[:, s] = out[grp][h * DN_CHUNK:(h + 1) * DN_CHUNK].astype(o_ref.dtype)
            state[grp * GDN_ROWS:(grp + 1) * GDN_ROWS, :] = out[GDN_NGROUPS + grp]

    return pl.pallas_call(
        body, name="gdn_fwd", grid=(n,),
        in_specs=[vec, vec, vec, qkv_s, z_s, ba_s], out_specs=[wide, st_s],
        out_shape=[jax.ShapeDtypeStruct((rows, DN_V_WIDTH), bf16),
                   jax.ShapeDtypeStruct((n, DN_HEADS * DN_KEY_DIM, LANES), f32)],
        scratch_shapes=[pltpu.VMEM((DN_HEADS * DN_KEY_DIM, LANES), f32)],
        compiler_params=_cparams(("arbitrary",)),
    )(a_log, dt_bias, norm_w, qkv_act, proj, ba)


def _gdn_bwd(a_log, dt_bias, norm_w, qkv_act, proj, ba, states, do):
    rows = qkv_act.shape[0]
    n, vec, qkv_s, z_s, ba_s, wide, st_s = _gdn_specs(rows, True)

    def body(al_ref, dt_ref, nw_ref, qkv_ref, z_ref, ba_ref, st_ref, do_ref,
             dal_ref, ddt_ref, dnw_ref, dqkv_ref, dz_ref, dba_ref, dstate):
        @pl.when(pl.program_id(0) == 0)
        def _():
            dstate[...] = jnp.zeros_like(dstate)
            dal_ref[...] = jnp.zeros_like(dal_ref)
            ddt_ref[...] = jnp.zeros_like(ddt_ref)
            dnw_ref[...] = jnp.zeros_like(dnw_ref)

        ng = GDN_NGROUPS
        _, vjp = jax.vjp(_gdn_chunk, al_ref[...], dt_ref[...], nw_ref[...], ba_ref[...],
                         *_gdn_operands(qkv_ref, z_ref, st_ref[0]))
        cts = tuple(_stack_cols(do_ref, _gdn_slices(grp)[3]) for grp in range(ng))
        cts += tuple(dstate[grp * GDN_ROWS:(grp + 1) * GDN_ROWS, :] for grp in range(ng))
        grads = vjp(cts)
        dal_ref[...] += grads[0]
        ddt_ref[...] += grads[1]
        dnw_ref[...] += grads[2]
        dba_ref[...] = grads[3]
        dqs, dks, dvs, dzs, dss = [grads[4 + i * ng:4 + (i + 1) * ng] for i in range(5)]
        for grp in range(ng):
            q, k, vs, heads = _gdn_slices(grp)
            dqkv_ref[:, q] = dqs[grp]
            dqkv_ref[:, k] = dks[grp]
            for h, (sv, sh) in enumerate(zip(vs, heads)):
                rows_h = slice(h * DN_CHUNK, (h + 1) * DN_CHUNK)
                dqkv_ref[:, sv] = dvs[grp][rows_h]
                dz_ref[:, sh] = dzs[grp][rows_h].astype(dz_ref.dtype)
            dstate[grp * GDN_ROWS:(grp + 1) * GDN_ROWS, :] = dss[grp]

    return pl.pallas_call(
        body, name="gdn_bwd", grid=(n,),
        in_specs=[vec, vec, vec, qkv_s, z_s, ba_s, st_s, wide],
        out_specs=[vec, vec, vec, qkv_s, wide, ba_s],
        out_shape=[jax.ShapeDtypeStruct((1, LANES), f32)] * 3
        + [jax.ShapeDtypeStruct((rows, DN_CONV_CH), f32), jax.ShapeDtypeStruct((rows, DN_V_WIDTH), bf16),
           jax.ShapeDtypeStruct((rows, LANES), f32)],
        scratch_shapes=[pltpu.VMEM((DN_HEADS * DN_KEY_DIM, LANES), f32)],
        compiler_params=_cparams(("arbitrary",)),
    )(a_log, dt_bias, norm_w, qkv_act, proj, ba, states, do)


def _ada_fwd(c_all, w_loc, b_loc):
    def body(c_ref, w_ref, b_ref, o_ref):
        o_ref[...] = _bdot(_silu(c_ref[...]), w_ref[...]) + b_ref[...]

    return pl.pallas_call(body, name="ada_fwd", out_shape=jax.ShapeDtypeStruct((c_all.shape[0], w_loc.shape[1]), f32),
                          compiler_params=_cparams())(c_all, w_loc, b_loc)


def _ada_bwd(c_all, dmod_cols):
    def body(c_ref, d_ref, o_ref):
        o_ref[...] = _bdot_tn(_silu(c_ref[...]), d_ref[...])

    return pl.pallas_call(body, name="ada_bwd",
                          out_shape=jax.ShapeDtypeStruct((c_all.shape[1], dmod_cols.shape[1]), f32),
                          compiler_params=_cparams())(c_all, dmod_cols)


def _sum_devices(parts):
    def body(p_ref, o_ref):
        acc = p_ref[0:1, :]
        for d in range(1, N_DEV):
            acc = acc + p_ref[d:d + 1, :]
        o_ref[...] = acc

    return pl.pallas_call(body, name="sum_small", out_shape=jax.ShapeDtypeStruct((1, parts.shape[1]), f32),
                          compiler_params=_cparams())(parts)


def _adam_math(w, g, m, v):
    m2 = ADAM_B1 * m + (1.0 - ADAM_B1) * g
    v2 = ADAM_B2 * v + (1.0 - ADAM_B2) * jnp.square(g)
    m_hat = m2 / (1.0 - ADAM_B1 ** ADAM_STEP)
    v_hat = v2 / (1.0 - ADAM_B2 ** ADAM_STEP)
    delta = -ADAM_LR * (m_hat / (jnp.sqrt(v_hat) + ADAM_EPS) + ADAM_WD * w)
    return delta, m2, v2


def _row_tile(rows):
    return _pick(rows, (256, 128, 64, 32, 16, 8))


def _adamw(w, g, m, v, name):
    rows, cols = w.shape
    tr = _row_tile(rows)

    def body(w_ref, g_ref, m_ref, v_ref, d_ref, m2_ref, v2_ref):
        d_ref[...], m2_ref[...], v2_ref[...] = _adam_math(w_ref[...], g_ref[...], m_ref[...], v_ref[...])

    tile = pl.BlockSpec((tr, cols), lambda i: (i, 0))
    return pl.pallas_call(body, name=name, grid=(rows // tr,), in_specs=[tile] * 4, out_specs=[tile] * 3,
                          out_shape=[jax.ShapeDtypeStruct(w.shape, f32)] * 3,
                          compiler_params=_cparams(("parallel",)))(w, g, m, v)


def _sum_adamw(parts, w, m, v, name):
    rows, cols = w.shape
    tr = _row_tile(rows)

    def body(p_ref, w_ref, m_ref, v_ref, g_ref, d_ref, m2_ref, v2_ref):
        g = p_ref[0].astype(f32)
        for d in range(1, N_DEV):
            g = g + p_ref[d].astype(f32)
        g_ref[...] = g
        d_ref[...], m2_ref[...], v2_ref[...] = _adam_math(w_ref[...], g, m_ref[...], v_ref[...])

    tile = pl.BlockSpec((tr, cols), lambda i: (i, 0))
    return pl.pallas_call(body, name=name, grid=(rows // tr,),
                          in_specs=[pl.BlockSpec((N_DEV, tr, cols), lambda i: (0, i, 0)), tile, tile, tile],
                          out_specs=[tile] * 4, out_shape=[jax.ShapeDtypeStruct(w.shape, f32)] * 4,
                          compiler_params=_cparams(("parallel",)))(parts, w, m, v)


def _pad_lanes(a, width):
    return jnp.pad(a, ((0, 0), (0, width - a.shape[1])))


def _cols_by_device(full):
    r, c = full.shape
    return jnp.moveaxis(full.reshape(r, N_DEV, c // N_DEV), 1, 0)


def _cols_from_devices(parts):
    d, r, n = parts.shape
    return jnp.moveaxis(parts, 0, 1).reshape(r, d * n)


def kernel(x, c, w_ada, b_ada, norm1_w, w_in, dn_conv_w, dn_A_log, dn_dt_bias, dn_norm_w, w_proj_sb, w_proj_dn, w_out, norm2_w, w_ffn_in, ffn_conv_w, ffn_conv_b, w_ffn_out, final_norm_w, loss_target, m_w_ada, m_b_ada, m_norm1_w, m_w_in, m_dn_conv_w, m_dn_A_log, m_dn_dt_bias, m_dn_norm_w, m_w_proj_sb, m_w_proj_dn, m_w_out, m_norm2_w, m_w_ffn_in, m_ffn_conv_w, m_ffn_conv_b, m_w_ffn_out, m_final_norm_w, v_w_ada, v_b_ada, v_norm1_w, v_w_in, v_dn_conv_w, v_dn_A_log, v_dn_dt_bias, v_dn_norm_w, v_w_proj_sb, v_w_proj_dn, v_w_out, v_norm2_w, v_w_ffn_in, v_ffn_conv_w, v_ffn_conv_b, v_w_ffn_out, v_final_norm_w):
    d = D_MODEL
    me = 4 * lax.axis_index("x") + 2 * lax.axis_index("y") + lax.axis_index("c")
    xs = x[0]
    target = loss_target[0]
    n_ada = w_ada.shape[2]
    n_dnc = dn_conv_w.shape[2]
    n_ffc = ffn_conv_w.shape[2]

    small = jnp.concatenate([c, dn_conv_w[0].reshape(1, -1), ffn_conv_w[0].reshape(1, -1)], axis=1)
    small = _pad_lanes(small, -(-small.shape[1] // LANES) * LANES)
    (small_g, w_in_g, w_psb_g, w_pdn_g, w_out_g, w_fin_g, w_fout_g) = _all_gather(
        [small, w_in[0].astype(bf16), w_proj_sb[0].astype(bf16), w_proj_dn[0].astype(bf16),
         w_out[0].astype(bf16), w_ffn_in[0].astype(bf16), w_ffn_out[0].astype(bf16)], "gather_weights")
    small_g = small_g[:, 0, :]
    c_all = small_g[:, :d]
    dn_cw = _cols_from_devices(small_g[:, d:d + DN_CONV_WIDTH * n_dnc].reshape(N_DEV, DN_CONV_WIDTH, n_dnc))
    o2 = d + DN_CONV_WIDTH * n_dnc
    ffn_cw = _cols_from_devices(small_g[:, o2:o2 + FFN_CONV_WIDTH * n_ffc].reshape(N_DEV, FFN_CONV_WIDTH, n_ffc))

    w_in_full = _cols_from_devices(w_in_g)
    r_sb, r_dn, r_z = 3 * SB_WIDTH, 3 * SB_WIDTH + DN_CONV_CH, 3 * SB_WIDTH + DN_CONV_CH + DN_V_WIDTH
    r_g = r_z + 2 * DN_HEADS
    w_main = jnp.concatenate([w_in_full[:, r_g:], w_in_full[:, r_sb:r_dn], w_in_full[:, r_dn:r_z],
                              w_in_full[:, :r_sb]], axis=1)
    w_ba = _pad_lanes(w_in_full[:, r_z:r_g], LANES)
    w_psb = _cols_from_devices(w_psb_g)
    w_pdn = w_pdn_g.reshape(DN_V_WIDTH, d)
    w_o = w_out_g.reshape(d, d)
    w_fin = _cols_from_devices(w_fin_g)
    w_fout = w_fout_g.reshape(D_FF, d)

    b_loc = lax.dynamic_slice(b_ada, (0, me * n_ada), (1, n_ada))
    mod_part = _ada_fwd(c_all, w_ada[0], b_loc)
    (mod_g,) = _all_gather([mod_part], "gather_mod")
    mod = lax.dynamic_index_in_dim(mod_g, me, axis=1, keepdims=False).reshape(1, N_DEV * n_ada)
    shift1, scale1, gate1, shift2, scale2, gate2 = [mod[:, i * d:(i + 1) * d] for i in range(6)]

    logit_lanes = ((0, 0), (GDN_LOGIT_LANE, LANES - GDN_LOGIT_LANE - DN_HEADS))
    a_log = jnp.pad(dn_A_log, logit_lanes)
    dt_b = jnp.pad(dn_dt_bias, logit_lanes)

    (h,) = _stage_fwd(_f_normmod, [norm1_w, shift1, scale1], [xs], [bf16], "norm1_fwd")
    proj = _mm(h, w_main, name="in_proj")
    ba = _mm(h, w_ba, name="in_proj_ba")
    k16, k0_16, k1_16, v16, v0_16, v1_16 = _sb_prepare(proj)
    o_a, sb_runs = _sb_attention_fwd2(proj, k16, v0_16, v1_16)
    qkv_act = _dn_conv_fwd(proj, dn_cw)
    o_b, states = _gdn_fwd(a_log, dt_b, dn_norm_w, qkv_act, proj, ba)
    pa = _mm(o_a, w_psb, name="proj_sb")
    pb = _mm(o_b, w_pdn, name="proj_dn")
    gates = [(proj, d, OFF_GA // d), (proj, d, OFF_GB // d)]
    (merged,) = _stage_fwd(_f_merge, [], gates + [pa, pb], [bf16], "merge_fwd")
    ao = _mm(merged, w_o, name="out_proj")
    (x1,) = _stage_fwd(_f_residual, [gate1], [xs, ao], [f32], "resid1_fwd")
    (h2,) = _stage_fwd(_f_normmod, [norm2_w, shift2, scale2], [x1], [bf16], "norm2_fwd")
    u_pre = _mm(h2, w_fin, name="ffn_in")
    act = _ffn_conv_fwd(u_pre, ffn_cw, ffn_conv_b)
    fo = _mm(act, w_fout, name="ffn_out")

    loss_p, d_gate2, d_wf, dx2, dfo = _loss_and_grads(gate2, final_norm_w.reshape(1, d), x1, fo, target)
    dact = _mm(dfo, w_fout, tb=True, name="ffn_out_dx")
    g_w_fout = _mm(act, dfo, ta=True, name="ffn_out_dw")
    du, dbg, dbu = _ffn_conv_bwd_act(u_pre, ffn_cw, ffn_conv_b, dact)
    du_pre, d_ffn_cw = _conv_bwd(du, u_pre, 0, ffn_cw, "ffn_conv_bwd")
    dh2 = _mm(du_pre, w_fin, tb=True, name="ffn_in_dx")
    g_w_fin = _mm(h2, du_pre, ta=True, name="ffn_in_dw")
    (d_n2w, d_shift2, d_scale2), (dx1,) = _stage_bwd(
        _f_normmod, [norm2_w, shift2, scale2], [x1], [dh2], [f32], "norm2_bwd", residual=(0, dx2))
    (d_gate1,), (dao,) = _stage_bwd(_f_residual, [gate1], [xs, ao], [dx1], [None, bf16], "resid1_bwd")
    dmerged = _mm(dao, w_o, tb=True, name="out_proj_dx")
    g_w_o = _mm(merged, dao, ta=True, name="out_proj_dw")
    _, (dga, dgb, dpa, dpb) = _stage_bwd(_f_merge, [], gates + [pa, pb], [dmerged], [bf16] * 4, "merge_bwd")
    do_a = _mm(dpa, w_psb, tb=True, name="proj_sb_dx")
    g_w_psb = _mm(o_a, dpa, ta=True, name="proj_sb_dw")
    do_b = _mm(dpb, w_pdn, tb=True, name="proj_dn_dx")
    g_w_pdn = _mm(o_b, dpb, ta=True, name="proj_dn_dw")
    dq, dk, dv = _sb_attention_bwd2(proj, k16, k0_16, k1_16, v16, sb_runs, do_a)
    d_alog, d_dtb, d_dnw, dqkv_act, dz, dba = _gdn_bwd(a_log, dt_b, dn_norm_w, qkv_act, proj, ba, states, do_b)
    d_conv_out = _dn_conv_bwd_act(proj, dn_cw, dqkv_act)
    d_dn_pre, d_dn_cw = _conv_bwd(d_conv_out, proj, OFF_DN // TCONV_C, dn_cw, "dn_conv_bwd")
    dproj = jnp.concatenate([dga, dgb, d_dn_pre, dz, dq, dk.astype(bf16), dv.astype(bf16)], axis=1)
    dh = _mm(dproj, w_main, tb=True, name="in_proj_dx")
    dh_ba = _mm(dba, w_ba, tb=True, name="in_proj_ba_dx")
    g_w_main = _mm(h, dproj, ta=True, name="in_proj_dw")
    g_w_ba = _mm(h, dba, ta=True, name="in_proj_ba_dw")
    (d_n1w, d_shift1, d_scale1), (grad_x,) = _stage_bwd(
        _f_normmod, [norm1_w, shift1, scale1], [xs], [[dh, dh_ba]], [f32], "norm1_bwd", residual=(0, dx1))

    dmod = jnp.concatenate([d_shift1, d_scale1, d_gate1, d_shift2, d_scale2, d_gate2], axis=1)
    d_ffn_cb = jnp.concatenate([dbg, dbu], axis=1)
    small_parts = jnp.concatenate(
        [loss_p, dmod, d_n1w, d_alog, d_dtb, d_dnw, d_n2w, d_ffn_cb, d_wf,
         d_dn_cw.reshape(1, -1), d_ffn_cw.reshape(1, -1)], axis=1)
    (small_parts_g,) = _all_gather([small_parts], "gather_small_grads")
    tot = _sum_devices(small_parts_g[:, 0, :])
    offs = {}
    pos = 0
    for nm, width in (("loss", LANES), ("b_ada", 6 * d), ("norm1_w", d), ("dn_A_log", LANES), ("dn_dt_bias", LANES),
                      ("dn_norm_w", LANES), ("norm2_w", d), ("ffn_conv_b", 2 * D_FF), ("final_norm_w", d),
                      ("dn_conv_w", DN_CONV_WIDTH * DN_CONV_CH), ("ffn_conv_w", FFN_CONV_WIDTH * 2 * D_FF)):
        offs[nm] = (pos, width)
        pos += width
    seg = lambda nm: tot[:, offs[nm][0]:offs[nm][0] + offs[nm][1]]
    loss = tot[0, 0]
    g_b_ada = seg("b_ada")
    g_norm1 = seg("norm1_w")
    g_alog = seg("dn_A_log")[:, GDN_LOGIT_LANE:GDN_LOGIT_LANE + DN_HEADS]
    g_dtb = seg("dn_dt_bias")[:, GDN_LOGIT_LANE:GDN_LOGIT_LANE + DN_HEADS]
    g_dnw = seg("dn_norm_w")
    g_norm2 = seg("norm2_w")
    g_ffn_cb = seg("ffn_conv_b")
    g_fnw = seg("final_norm_w")
    g_dn_cw = lax.dynamic_slice(seg("dn_conv_w").reshape(DN_CONV_WIDTH, DN_CONV_CH), (0, me * n_dnc),
                                (DN_CONV_WIDTH, n_dnc))
    g_ffn_cw = lax.dynamic_slice(seg("ffn_conv_w").reshape(FFN_CONV_WIDTH, 2 * D_FF), (0, me * n_ffc),
                                 (FFN_CONV_WIDTH, n_ffc))

    dmod_all = small_parts_g[:, 0, offs["b_ada"][0]:offs["b_ada"][0] + 6 * d]
    g_w_ada = _ada_bwd(c_all, lax.dynamic_slice(dmod_all, (0, me * n_ada), (N_DEV, n_ada)))

    def pack(parts):
        flat = [p.reshape(1, -1) for p in parts]
        flat = [_pad_lanes(p, -(-p.shape[1] // LANES) * LANES) for p in flat]
        return jnp.concatenate(flat, axis=1), [p.shape[1] for p in flat]

    small_names_g = [g_b_ada, g_norm1, g_alog, g_dtb, g_dnw, g_norm2, g_ffn_cb, g_fnw, g_dn_cw, g_ffn_cw]
    small_w = [b_ada, norm1_w, dn_A_log, dn_dt_bias, dn_norm_w, norm2_w, ffn_conv_b, final_norm_w, dn_conv_w[0], ffn_conv_w[0]]
    small_m = [m_b_ada, m_norm1_w, m_dn_A_log, m_dn_dt_bias, m_dn_norm_w, m_norm2_w, m_ffn_conv_b, m_final_norm_w, m_dn_conv_w[0], m_ffn_conv_w[0]]
    small_v = [v_b_ada, v_norm1_w, v_dn_A_log, v_dn_dt_bias, v_dn_norm_w, v_norm2_w, v_ffn_conv_b, v_final_norm_w, v_dn_conv_w[0], v_ffn_conv_w[0]]
    pg, widths = pack(small_names_g)
    pw, _ = pack(small_w)
    pm, _ = pack(small_m)
    pv, _ = pack(small_v)
    s_delta, s_m, s_v = _adamw(pw, pg, pm, pv, "adamw_small")

    def unpack(flat):
        out, pos = [], 0
        for ref_arr, width in zip(small_w, widths):
            out.append(flat[:, pos:pos + ref_arr.size].reshape(ref_arr.shape))
            pos += width
        return out

    small_grads = [g.reshape(w_.shape) for g, w_ in zip(small_names_g, small_w)]
    small_delta, small_newm, small_newv = unpack(s_delta), unpack(s_m), unpack(s_v)

    ada_delta, ada_m, ada_v = _adamw(w_ada[0], g_w_ada, m_w_ada[0], v_w_ada[0], "adamw_ada")

    g_w_in_full = jnp.concatenate([g_w_main[:, OFF_SBQ:], g_w_main[:, OFF_DN:OFF_Z], g_w_main[:, OFF_Z:OFF_SBQ],
                                   g_w_ba[:, :2 * DN_HEADS], g_w_main[:, :OFF_DN]], axis=1)
    sends = [_cols_by_device(g_w_in_full).astype(bf16),
             _cols_by_device(g_w_psb).astype(bf16),
             g_w_pdn.reshape(N_DEV, DN_V_WIDTH // N_DEV, d).astype(bf16),
             g_w_o.reshape(N_DEV, d // N_DEV, d).astype(bf16),
             _cols_by_device(g_w_fin).astype(bf16),
             g_w_fout.reshape(N_DEV, D_FF // N_DEV, d).astype(bf16)]
    recv = _all_to_all(sends, "exchange_grads")
    big = {}
    for nm, parts, w_, m_, v_ in (("w_in", recv[0], w_in, m_w_in, v_w_in),
                                  ("w_proj_sb", recv[1], w_proj_sb, m_w_proj_sb, v_w_proj_sb),
                                  ("w_proj_dn", recv[2], w_proj_dn, m_w_proj_dn, v_w_proj_dn),
                                  ("w_out", recv[3], w_out, m_w_out, v_w_out),
                                  ("w_ffn_in", recv[4], w_ffn_in, m_w_ffn_in, v_w_ffn_in),
                                  ("w_ffn_out", recv[5], w_ffn_out, m_w_ffn_out, v_w_ffn_out)):
        big[nm] = [t[None] for t in _sum_adamw(parts, w_[0], m_[0], v_[0], "adamw_" + nm)]

    sg = dict(zip(["b_ada", "norm1_w", "dn_A_log", "dn_dt_bias", "dn_norm_w", "norm2_w", "ffn_conv_b", "final_norm_w",
                   "dn_conv_w", "ffn_conv_w"], range(10)))

    def small_out(table, nm):
        val = table[sg[nm]]
        return val[None] if nm in ("dn_conv_w", "ffn_conv_w") else val

    order = ["w_ada", "b_ada", "norm1_w", "w_in", "dn_conv_w", "dn_A_log", "dn_dt_bias", "dn_norm_w", "w_proj_sb",
             "w_proj_dn", "w_out", "norm2_w", "w_ffn_in", "ffn_conv_w", "ffn_conv_b", "w_ffn_out", "final_norm_w"]
    groups = []
    for k, small_table in enumerate((small_grads, small_delta, small_newm, small_newv)):
        row = []
        for nm in order:
            if nm == "w_ada":
                row.append((g_w_ada, ada_delta, ada_m, ada_v)[k][None])
            elif nm in big:
                row.append(big[nm][k])
            else:
                row.append(small_out(small_table, nm))
        groups.append(row)
    return (loss, grad_x[None], *groups[0], *groups[1], *groups[2], *groups[3])
```

```python
import functools

import jax
import jax.numpy as jnp
from jax import lax
from jax.experimental import pallas as pl
from jax.experimental.pallas import tpu as pltpu

f32 = jnp.float32
bf16 = jnp.bfloat16

D_MODEL = 1024
SB_HEADS = 8
SB_HEAD_DIM = 64
SB_WIDTH = SB_HEADS * SB_HEAD_DIM
SB_QBLOCK = 128
DN_HEADS = 8
DN_KEY_DIM = 64
DN_VAL_DIM = 128
DN_QK_WIDTH = DN_HEADS * DN_KEY_DIM
DN_V_WIDTH = DN_HEADS * DN_VAL_DIM
DN_CONV_CH = 2 * DN_QK_WIDTH + DN_V_WIDTH
DN_CONV_WIDTH = 4
DN_CHUNK = 64
D_FF = 2816
FFN_CONV_WIDTH = 3
NORM_EPS = 1e-6
L2_EPS = 1e-6
ADAM_LR = 0.001
ADAM_B1 = 0.9
ADAM_B2 = 0.999
ADAM_EPS = 1e-08
ADAM_WD = 0.01
ADAM_STEP = 10

N_DEV = 8
MESH = pl.DeviceIdType.MESH

LANES = 128
SUBLANES = 8
VMEM_LIMIT = 48 * 1024 * 1024

OFF_GA = 0
OFF_GB = D_MODEL
OFF_DN = 2 * D_MODEL
OFF_Z = OFF_DN + DN_CONV_CH
OFF_SBQ = OFF_Z + DN_V_WIDTH
OFF_SBK = OFF_SBQ + SB_WIDTH
OFF_SBV = OFF_SBK + SB_WIDTH
MAIN_WIDTH = OFF_SBV + SB_WIDTH

TM = 256
TCONV_R = 512
TCONV_C = 512
TCONV_FF = D_FF // 2
SB_PAIRS_PER_STEP = 2


def _cparams(sem=None):
    return pltpu.CompilerParams(dimension_semantics=sem, vmem_limit_bytes=VMEM_LIMIT)


def _pick(n, cands):
    for c in cands:
        if n % c == 0:
            return c
    return n


def _my_pos():
    return lax.axis_index("x"), lax.axis_index("y"), lax.axis_index("c")


def _flip(v, bit):
    return 1 - v if bit else v


def _comm_scratch(n):
    return [pltpu.SemaphoreType.DMA((n, 7)), pltpu.SemaphoreType.DMA((n, 7)), pltpu.SemaphoreType.DMA((n,))]


def _gather_protocol(ins, outs, send_sems, recv_sems, local_sems):
    n = len(ins)
    x, y, c = _my_pos()
    me, sibling = (x, y, c), (x, y, 1 - c)
    chips = [(1 - x, y), (x, 1 - y), (1 - x, 1 - y)]

    def slot(out, pos):
        return out.at[4 * pos[0] + 2 * pos[1] + pos[2]]

    def copy(a, k, block, to, src=None):
        return pltpu.make_async_remote_copy(
            src_ref=slot(outs[a], block) if src is None else src, dst_ref=slot(outs[a], block),
            send_sem=send_sems.at[a, k], recv_sem=recv_sems.at[a, k], device_id=to, device_id_type=MESH)

    def local(a):
        return pltpu.make_async_copy(ins[a], slot(outs[a], me), local_sems.at[a])

    def first(a):
        return [copy(a, 0, me, sibling, src=ins[a])] + [copy(a, 1 + j, me, (*chip, c), src=ins[a])
                                                         for j, chip in enumerate(chips)]

    def start():
        for a in range(n):
            local(a).start()
            for cp in first(a):
                cp.start()

    def finish():
        forwards = []
        for a in range(n):
            for j, chip in enumerate(chips):
                copy(a, 1 + j, (*chip, c), me).wait_recv()
                fwd = copy(a, 4 + j, (*chip, c), sibling)
                fwd.start()
                forwards.append(fwd)
        for a in range(n):
            copy(a, 0, sibling, me).wait_recv()
            for j, chip in enumerate(chips):
                copy(a, 4 + j, (*chip, 1 - c), me).wait_recv()
        for a in range(n):
            for cp in first(a):
                cp.wait_send()
        for cp in forwards:
            cp.wait_send()
        for a in range(n):
            local(a).wait()

    return start, finish


def _exchange_protocol(ins, outs, send_sems, recv_sems, local_sems):
    n = len(ins)
    x, y, c = _my_pos()
    me_idx = 4 * x + 2 * y + c

    def local(a):
        return pltpu.make_async_copy(ins[a].at[me_idx], outs[a].at[me_idx], local_sems.at[a])

    def copies(a, m):
        peer = (_flip(x, m & 4), _flip(y, m & 2), _flip(c, m & 1))
        peer_idx = 4 * peer[0] + 2 * peer[1] + peer[2]
        sems = dict(send_sem=send_sems.at[a, m - 1], recv_sem=recv_sems.at[a, m - 1], device_id=peer,
                    device_id_type=MESH)
        send = pltpu.make_async_remote_copy(src_ref=ins[a].at[peer_idx], dst_ref=outs[a].at[me_idx], **sems)
        recv = pltpu.make_async_remote_copy(src_ref=ins[a].at[peer_idx], dst_ref=outs[a].at[peer_idx], **sems)
        return send, recv

    def start():
        for a in range(n):
            local(a).start()
            for m in range(1, N_DEV):
                copies(a, m)[0].start()

    def finish():
        for a in range(n):
            for m in range(1, N_DEV):
                copies(a, m)[1].wait_recv()
        for a in range(n):
            for m in range(1, N_DEV):
                copies(a, m)[0].wait_send()
            local(a).wait()

    return start, finish


def _collective_call(protocol, arrs, out_shapes, name):
    n = len(arrs)

    def body(*refs):
        start, finish = protocol(refs[:n], refs[n:2 * n], *refs[2 * n:])
        start()
        finish()

    any_spec = pl.BlockSpec(memory_space=pl.ANY)
    return pl.pallas_call(body, name=name, out_shape=out_shapes, in_specs=[any_spec] * n, out_specs=[any_spec] * n,
                          scratch_shapes=_comm_scratch(n))(*arrs)


def _gathered_shapes(arrs):
    return [jax.ShapeDtypeStruct((N_DEV,) + a.shape, a.dtype) for a in arrs]


def _all_gather(arrs, name):
    return _collective_call(_gather_protocol, arrs, _gathered_shapes(arrs), name)


def _all_to_all(arrs, name):
    return _collective_call(_exchange_protocol, arrs, [jax.ShapeDtypeStruct(a.shape, a.dtype) for a in arrs], name)


MM_BLOCK_BYTES = 4 * 1024 * 1024


def _mm_tiles(m_dim, n_dim, k_dim, a_bytes, b_bytes):
    tm = _pick(m_dim, (1024, 512, 256, 128))
    tn = _pick(n_dim, (512, 256, 128))
    tk = k_dim
    if k_dim % LANES == 0:
        units = k_dim // LANES
        fits = [u for u in range(1, units + 1) if units % u == 0
                and u * LANES * max(tm * a_bytes, tn * b_bytes) <= MM_BLOCK_BYTES]
        tk = max(fits) * LANES
    return tm, tn, tk


def _mm(a, b, *, ta=False, tb=False, name):
    (k_dim, m_dim) = a.shape if ta else a.shape[::-1]
    (n_dim, kb_dim) = b.shape if tb else b.shape[::-1]
    assert k_dim == kb_dim, (a.shape, b.shape, ta, tb)
    tm, tn, tk = _mm_tiles(m_dim, n_dim, k_dim, a.dtype.itemsize, b.dtype.itemsize)
    nk = k_dim // tk
    dims = (((0 if ta else 1,), (1 if tb else 0,)), ((), ()))

    def body(a_ref, b_ref, o_ref):
        part = lax.dot_general(a_ref[...].astype(bf16), b_ref[...].astype(bf16), dims, preferred_element_type=f32)
        if nk == 1:
            o_ref[...] = part
        else:
            k = pl.program_id(2)

            @pl.when(k == 0)
            def _():
                o_ref[...] = part

            @pl.when(k > 0)
            def _():
                o_ref[...] += part

    a_spec = pl.BlockSpec((tk, tm), lambda i, j, k: (k, i)) if ta else pl.BlockSpec((tm, tk), lambda i, j, k: (i, k))
    b_spec = pl.BlockSpec((tn, tk), lambda i, j, k: (j, k)) if tb else pl.BlockSpec((tk, tn), lambda i, j, k: (k, j))
    return pl.pallas_call(
        body, name=name, grid=(m_dim // tm, n_dim // tn, nk),
        in_specs=[a_spec, b_spec], out_specs=pl.BlockSpec((tm, tn), lambda i, j, k: (i, j)),
        out_shape=jax.ShapeDtypeStruct((m_dim, n_dim), f32),
        compiler_params=_cparams(("parallel", "parallel", "arbitrary")),
    )(a, b)


def _win(t):
    return t if isinstance(t, tuple) else (t, t.shape[1], 0)


def _tile_spec(width, cb, tm):
    return pl.BlockSpec((tm, width), lambda i: (i, cb))


def _param_spec(p):
    return pl.BlockSpec(p.shape, lambda i: (0, 0))


def _stage_fwd(f, params, tiles, out_dtypes, name):
    tiles = [_win(t) for t in tiles]
    rows = tiles[0][0].shape[0]
    tm = min(TM, rows)
    avals = jax.eval_shape(f, *[jax.ShapeDtypeStruct(p.shape, f32) for p in params],
                           *[jax.ShapeDtypeStruct((tm, w), f32) for _, w, _ in tiles])
    n_p, n_t = len(params), len(tiles)

    def body(*refs):
        p = [r[...] for r in refs[:n_p]]
        t = [r[...].astype(f32) for r in refs[n_p:n_p + n_t]]
        for o_ref, val in zip(refs[n_p + n_t:], f(*p, *t)):
            o_ref[...] = val.astype(o_ref.dtype)

    return pl.pallas_call(
        body, name=name, grid=(rows // tm,),
        in_specs=[_param_spec(p) for p in params] + [_tile_spec(w, cb, tm) for _, w, cb in tiles],
        out_specs=[_tile_spec(a.shape[1], 0, tm) for a in avals],
        out_shape=[jax.ShapeDtypeStruct((rows, a.shape[1]), dt) for a, dt in zip(avals, out_dtypes)],
        compiler_params=_cparams(("parallel",)),
    )(*params, *[t[0] for t in tiles])


def _stage_bwd(f, params, tiles, cts, grad_dtypes, name, residual=None):
    tiles = [_win(t) for t in tiles]
    rows = tiles[0][0].shape[0]
    tm = min(TM, rows)
    cts = [list(g) if isinstance(g, (list, tuple)) else [g] for g in cts]
    flat_cts = [a for g in cts for a in g]
    n_p, n_t, n_c = len(params), len(tiles), len(flat_cts)
    has_res = residual is not None
    want = [j for j, dt in enumerate(grad_dtypes) if dt is not None]

    def body(*refs):
        i = pl.program_id(0)
        p = [r[...] for r in refs[:n_p]]
        t = [r[...].astype(f32) for r in refs[n_p:n_p + n_t]]
        ct_vals = [r[...].astype(f32) for r in refs[n_p + n_t:n_p + n_t + n_c]]
        ct, at = [], 0
        for g in cts:
            ct.append(functools.reduce(jnp.add, ct_vals[at:at + len(g)]))
            at += len(g)
        ct = tuple(ct)
        pos = n_p + n_t + n_c
        res_ref = refs[pos] if has_res else None
        pos += 1 if has_res else 0
        dp_refs = refs[pos:pos + n_p]
        dt_refs = refs[pos + n_p:]
        _, vjp = jax.vjp(f, *p, *t)
        grads = vjp(ct)

        @pl.when(i == 0)
        def _():
            for r in dp_refs:
                r[...] = jnp.zeros_like(r)

        for r, g in zip(dp_refs, grads[:n_p]):
            r[...] += g
        for r, j in zip(dt_refs, want):
            g = grads[n_p + j]
            if has_res and j == residual[0]:
                g = g + res_ref[...].astype(f32)
            r[...] = g.astype(r.dtype)

    in_arrays = list(params) + [t[0] for t in tiles] + flat_cts
    in_specs = ([_param_spec(p) for p in params] + [_tile_spec(w, cb, tm) for _, w, cb in tiles]
                + [_tile_spec(c.shape[1], 0, tm) for c in flat_cts])
    if has_res:
        in_arrays.append(residual[1])
        in_specs.append(_tile_spec(residual[1].shape[1], 0, tm))
    out_shape = ([jax.ShapeDtypeStruct(p.shape, f32) for p in params]
                 + [jax.ShapeDtypeStruct((rows, tiles[j][1]), grad_dtypes[j]) for j in want])
    out_specs = [_param_spec(p) for p in params] + [_tile_spec(tiles[j][1], 0, tm) for j in want]
    outs = pl.pallas_call(
        body, name=name, grid=(rows // tm,), in_specs=in_specs, out_specs=out_specs, out_shape=out_shape,
        compiler_params=_cparams(("arbitrary",)),
    )(*in_arrays)
    return outs[:n_p], outs[n_p:]


def _rms(x, w):
    return x * lax.rsqrt(jnp.mean(x * x, axis=-1, keepdims=True) + NORM_EPS) * w


def _f_normmod(w, shift, scale, x):
    return (_rms(x, w) * (1.0 + scale) + shift,)


def _f_merge(ga, gb, pa, pb):
    return (jax.nn.sigmoid(ga) * pa + jax.nn.sigmoid(gb) * pb,)


def _f_residual(gate, x, branch):
    return (x + gate * branch,)


def _f_loss(gate, wf, x1, fo, target):
    y = _rms(x1 + gate * fo, wf)
    err = jnp.square(y - target)
    return (0.5 * jnp.sum(jnp.mean(err, axis=-1, keepdims=True), axis=0, keepdims=True),)


def _loss_and_grads(gate2, wf, x1, fo, target):
    rows, d = x1.shape
    tm = min(TM, rows)

    def body(g_ref, w_ref, x_ref, fo_ref, t_ref, loss_ref, dg_ref, dw_ref, dx_ref, dfo_ref):
        i = pl.program_id(0)
        (val,), vjp = jax.vjp(_f_loss, g_ref[...], w_ref[...], x_ref[...], fo_ref[...], t_ref[...])
        dg, dw, dx, dfo, _ = vjp((jnp.ones((1, 1), f32),))

        @pl.when(i == 0)
        def _():
            loss_ref[...] = jnp.zeros_like(loss_ref)
            dg_ref[...] = jnp.zeros_like(dg_ref)
            dw_ref[...] = jnp.zeros_like(dw_ref)

        loss_ref[...] += jnp.broadcast_to(val, loss_ref.shape)
        dg_ref[...] += dg
        dw_ref[...] += dw
        dx_ref[...] = dx
        dfo_ref[...] = dfo.astype(bf16)

    vec = pl.BlockSpec((1, d), lambda i: (0, 0))
    tile = pl.BlockSpec((tm, d), lambda i: (i, 0))
    return pl.pallas_call(
        body, name="loss_fwd_bwd", grid=(rows // tm,),
        in_specs=[vec, vec, tile, tile, tile],
        out_specs=[pl.BlockSpec((1, LANES), lambda i: (0, 0)), vec, vec, tile, tile],
        out_shape=[jax.ShapeDtypeStruct((1, LANES), f32), jax.ShapeDtypeStruct((1, d), f32),
                   jax.ShapeDtypeStruct((1, d), f32), jax.ShapeDtypeStruct((rows, d), f32),
                   jax.ShapeDtypeStruct((rows, d), bf16)],
        compiler_params=_cparams(("arbitrary",)),
    )(gate2, wf, x1, fo, target)


def _softplus(z):
    return jnp.maximum(z, 0.0) + jnp.log(1.0 + jnp.exp(-jnp.abs(z)))


def _split_dot(a, m):
    hi = a.astype(bf16)
    lo = (a - hi.astype(f32)).astype(bf16)
    return jnp.dot(hi, m, preferred_element_type=f32) + jnp.dot(lo, m, preferred_element_type=f32)


def _suffix_matrix(n):
    r = lax.broadcasted_iota(jnp.int32, (n, n), 0)
    c = lax.broadcasted_iota(jnp.int32, (n, n), 1)
    return (r > c).astype(bf16)


def _head_masks():
    lane = lax.broadcasted_iota(jnp.int32, (1, LANES), 1)
    return [(lane < SB_HEAD_DIM).astype(f32), (lane >= SB_HEAD_DIM).astype(f32)]


def _sb_prepare(proj):
    def f(k, v):
        lane = lax.broadcasted_iota(jnp.int32, (1, SB_WIDTH), 1)
        m0 = (jnp.bitwise_and(lane, LANES - 1) < SB_HEAD_DIM).astype(f32)
        m1 = 1.0 - m0
        return k, k * m0, k * m1, v, v * m0, v * m1

    wins = [(proj, SB_WIDTH, OFF_SBK // SB_WIDTH), (proj, SB_WIDTH, OFF_SBV // SB_WIDTH)]
    return _stage_fwd(f, [], wins, [bf16] * 6, "sb_prepare")


def _stack_heads(x):
    m0, m1 = _head_masks()
    return jnp.concatenate([x * m0, x * m1], axis=0)


def _sb_logits(qst, k, t_pos2, kb, bq, masked):
    z = lax.dot_general(qst, k, (((1,), (1,)), ((), ())), preferred_element_type=f32)
    l = -_softplus(z)
    if masked:
        s_pos = kb * bq + lax.broadcasted_iota(jnp.int32, (1, bq), 1)
        causal = s_pos < t_pos2
        l = jnp.where(causal, l, 0.0)
    else:
        causal = None
    return z, l, causal


class _SideComm:
    def __init__(self, protocol, arrs, out_shapes):
        self.protocol, self.arrs, self.out_shapes = protocol, list(arrs), list(out_shapes)
        self.n = len(self.arrs)

    def specs(self):
        return [pl.BlockSpec(memory_space=pl.ANY)] * self.n

    def run(self, in_refs, out_refs, sems, first, last, compute):
        start, finish = self.protocol(in_refs, out_refs, *sems)
        pl.when(first)(start)
        compute()
        pl.when(last)(finish)


def _grid_ends(grid):
    i0, i1 = pl.program_id(0), pl.program_id(1)
    return (jnp.logical_and(i0 == 0, i1 == 0), jnp.logical_and(i0 == grid[0] - 1, i1 == grid[1] - 1))


def _sb_attention_fwd2(proj, k16, v0_16, v1_16, side=None):
    rows = proj.shape[0]
    bq = SB_QBLOCK
    nq = rows // bq
    assert nq <= LANES, "one lane per key block"
    npair = SB_WIDTH // LANES
    scale = SB_HEAD_DIM ** -0.5

    npp = SB_PAIRS_PER_STEP
    wq = npp * LANES
    grid = (npair // npp, nq)
    ns = side.n if side else 0

    def body(q_ref, k_ref, v0_ref, v1_ref, *rest):
        o_ref, runs_ref = rest[ns], rest[ns + 1]
        if side:
            side.run(rest[:ns], rest[ns + 2:2 * ns + 2], rest[2 * ns + 2:], *_grid_ends(grid),
                     lambda: compute(q_ref, k_ref, v0_ref, v1_ref, o_ref, runs_ref))
        else:
            compute(q_ref, k_ref, v0_ref, v1_ref, o_ref, runs_ref)

    def compute(q_ref, k_ref, v0_ref, v1_ref, o_ref, runs_ref):
        qi = pl.program_id(1)
        pairs = [slice(pp * LANES, (pp + 1) * LANES) for pp in range(npp)]
        qst = [(_stack_heads(q_ref[:, s]) * scale).astype(bf16) for s in pairs]
        r = lax.broadcasted_iota(jnp.int32, (bq, 2 * bq), 0)
        c = lax.broadcasted_iota(jnp.int32, (bq, 2 * bq), 1)
        m2 = jnp.logical_or(r > c, c >= bq).astype(bf16)
        t_pos = qi * bq + lax.broadcasted_iota(jnp.int32, (bq, 1), 0)
        t_pos2 = jnp.concatenate([t_pos, t_pos], axis=0)
        lane = lax.broadcasted_iota(jnp.int32, (1, LANES), 1)
        runs_ref[...] = jnp.zeros_like(runs_ref)

        def tiles(kbs, carry, masked):
            jobs = [(pp, kb) for kb in kbs for pp in range(npp)]
            rows_k = [pl.ds(pl.multiple_of(kb * bq, bq), bq) for _, kb in jobs]
            zl = [_sb_logits(qst[pp], k_ref[rk, pairs[pp]], t_pos2, kb, bq, masked) for (pp, kb), rk in zip(jobs, rows_k)]
            cs = [_split_dot(l, m2) for _, l, _ in zl]
            run = [cr[0] for cr in carry]
            acc = [cr[1] for cr in carry]
            probs = []
            for (pp, kb), (z, l, causal), cs2 in zip(jobs, zl, cs):
                a = jnp.exp(z + l + cs2[:, :bq] + run[pp])
                if masked:
                    a = jnp.where(causal, a, 0.0)
                probs.append(a.astype(bf16))
                for hh in range(2):
                    cols = slice((2 * pp + hh) * LANES, (2 * pp + hh + 1) * LANES)
                    runs_ref[:, cols] = jnp.where(lane == kb, run[pp][hh * bq:(hh + 1) * bq], runs_ref[:, cols])
                run[pp] = run[pp] + cs2[:, bq:]
            for (pp, kb), rk, ab in zip(jobs, rows_k, probs):
                acc[pp] = (acc[pp] + jnp.dot(ab[:bq], v0_ref[rk, pairs[pp]], preferred_element_type=f32)
                           + jnp.dot(ab[bq:], v1_ref[rk, pairs[pp]], preferred_element_type=f32))
            return tuple(zip(run, acc))

        zero = (jnp.zeros((2 * bq, bq), f32), jnp.zeros((bq, LANES), f32))
        carry = tiles([qi], (zero,) * npp, True)
        carry = lax.fori_loop(0, qi // 2, lambda i, cr: tiles([qi - 1 - 2 * i, qi - 2 - 2 * i], cr, False), carry)
        carry = lax.cond(qi % 2 == 1, lambda cr: tiles([0], cr, False), lambda cr: cr, carry)
        for pp in range(npp):
            o_ref[:, pairs[pp]] = carry[pp][1]

    kv = pl.BlockSpec((rows, wq), lambda p, i: (0, p))
    return pl.pallas_call(
        body, name="sb_attn_fwd", grid=grid,
        in_specs=[pl.BlockSpec((bq, wq), lambda p, i: (i, OFF_SBQ // wq + p)), kv, kv, kv] + (side.specs() if side else []),
        out_specs=[pl.BlockSpec((bq, wq), lambda p, i: (i, p)),
                   pl.BlockSpec((bq, 2 * wq), lambda p, i: (i, p))] + (side.specs() if side else []),
        out_shape=[jax.ShapeDtypeStruct((rows, SB_WIDTH), f32),
                   jax.ShapeDtypeStruct((rows, SB_HEADS * LANES), f32)] + (side.out_shapes if side else []),
        scratch_shapes=_comm_scratch(ns) if side else [],
        compiler_params=_cparams(("arbitrary", "arbitrary")),
    )(proj, k16, v0_16, v1_16, *(side.arrs if side else []))


def _sb_attention_bwd2(proj, k16, k0_16, k1_16, v16, runs, do, side=None):
    rows = proj.shape[0]
    bq = SB_QBLOCK
    nq = rows // bq
    npair = SB_WIDTH // LANES
    scale = SB_HEAD_DIM ** -0.5
    tn = (((0,), (0,)), ((), ()))
    nt = (((1,), (1,)), ((), ()))

    npp = SB_PAIRS_PER_STEP
    wq = npp * LANES
    grid = (npair // npp, nq)
    ns = side.n if side else 0

    def body(q_ref, k_ref, k0_ref, k1_ref, v_ref, runs_ref, do_ref, *rest):
        outs = rest[ns:ns + 3]
        ins = (q_ref, k_ref, k0_ref, k1_ref, v_ref, runs_ref, do_ref)
        if side:
            side.run(rest[:ns], rest[ns + 3:2 * ns + 3], rest[2 * ns + 3:], *_grid_ends(grid),
                     lambda: compute(*ins, *outs))
        else:
            compute(*ins, *outs)

    def compute(q_ref, k_ref, k0_ref, k1_ref, v_ref, runs_ref, do_ref, dq_ref, dk_ref, dv_ref):
        qi = pl.program_id(1)

        @pl.when(qi == 0)
        def _():
            dk_ref[...] = jnp.zeros_like(dk_ref)
            dv_ref[...] = jnp.zeros_like(dv_ref)

        pairs = [slice(pp * LANES, (pp + 1) * LANES) for pp in range(npp)]
        qst = [(_stack_heads(q_ref[:, s]) * scale).astype(bf16) for s in pairs]
        dost = [_stack_heads(do_ref[:, s]).astype(bf16) for s in pairs]
        runs = [jnp.concatenate([runs_ref[:, 2 * pp * LANES:(2 * pp + 1) * LANES],
                                 runs_ref[:, (2 * pp + 1) * LANES:(2 * pp + 2) * LANES]], axis=0) for pp in range(npp)]
        r = lax.broadcasted_iota(jnp.int32, (bq, 2 * bq), 0)
        c = lax.broadcasted_iota(jnp.int32, (bq, 2 * bq), 1)
        suffix_m = _suffix_matrix(bq)
        m2 = jnp.logical_or(r < c, c >= bq).astype(bf16)
        t_pos = qi * bq + lax.broadcasted_iota(jnp.int32, (bq, 1), 0)
        t_pos2 = jnp.concatenate([t_pos, t_pos], axis=0)
        lane = lax.broadcasted_iota(jnp.int32, (1, LANES), 1)

        def tiles(kbs, carry, masked):
            jobs = [(pp, kb) for kb in kbs for pp in range(npp)]
            rows_k = [pl.ds(pl.multiple_of(kb * bq, bq), bq) for _, kb in jobs]
            zl = [_sb_logits(qst[pp], k_ref[rk, pairs[pp]], t_pos2, kb, bq, masked) for (pp, kb), rk in zip(jobs, rows_k)]
            das = [lax.dot_general(dost[pp], v_ref[rk, pairs[pp]], nt, preferred_element_type=f32)
                   for (pp, kb), rk in zip(jobs, rows_k)]
            sticks = [_split_dot(l, suffix_m) for _, l, _ in zl]
            probs, ps = [], []
            for (pp, kb), (z, l, causal), stick, da in zip(jobs, zl, sticks, das):
                run = jnp.sum(jnp.where(lane == kb, runs[pp], 0.0), axis=1, keepdims=True)
                a = jnp.exp(z + l + stick + run)
                if masked:
                    a = jnp.where(causal, a, 0.0)
                probs.append(a.astype(bf16))
                ps.append(da * a)
            pcs = [_split_dot(p, m2) for p in ps]
            pref = [cr[0] for cr in carry]
            dq_acc = [cr[1] for cr in carry]
            dzs = []
            for (pp, kb), (z, l, causal), p, pc2 in zip(jobs, zl, ps, pcs):
                dz = p * jnp.exp(l) - jnp.exp(z + l) * (pc2[:, :bq] + pref[pp])
                if masked:
                    dz = jnp.where(causal, dz, 0.0)
                dzs.append(dz.astype(bf16))
                pref[pp] = pref[pp] + pc2[:, bq:]
            for (pp, kb), rk, dzb, ab in zip(jobs, rows_k, dzs, probs):
                cols = pairs[pp]
                dq_acc[pp] = (dq_acc[pp] + jnp.dot(dzb[:bq], k0_ref[rk, cols], preferred_element_type=f32)
                              + jnp.dot(dzb[bq:], k1_ref[rk, cols], preferred_element_type=f32))
                dk_ref[rk, cols] += lax.dot_general(dzb, qst[pp], tn, preferred_element_type=f32)
                dv_ref[rk, cols] += lax.dot_general(ab, dost[pp], tn, preferred_element_type=f32)
            return tuple(zip(pref, dq_acc))

        zero = (jnp.zeros((2 * bq, bq), f32), jnp.zeros((bq, LANES), f32))
        carry = lax.fori_loop(0, qi // 2, lambda i, cr: tiles([2 * i, 2 * i + 1], cr, False), (zero,) * npp)
        carry = lax.cond(qi % 2 == 1, lambda cr: tiles([qi - 1], cr, False), lambda cr: cr, carry)
        carry = tiles([qi], carry, True)
        for pp in range(npp):
            dq_ref[:, pairs[pp]] = (carry[pp][1] * scale).astype(dq_ref.dtype)

    blk = pl.BlockSpec((bq, wq), lambda p, i: (i, p))
    full = pl.BlockSpec((rows, wq), lambda p, i: (0, p))
    return pl.pallas_call(
        body, name="sb_attn_bwd", grid=grid,
        in_specs=[pl.BlockSpec((bq, wq), lambda p, i: (i, OFF_SBQ // wq + p)), full, full, full, full,
                  pl.BlockSpec((bq, 2 * wq), lambda p, i: (i, p)), blk] + (side.specs() if side else []),
        out_specs=[blk, full, full] + (side.specs() if side else []),
        out_shape=[jax.ShapeDtypeStruct((rows, SB_WIDTH), bf16), jax.ShapeDtypeStruct((rows, SB_WIDTH), f32),
                   jax.ShapeDtypeStruct((rows, SB_WIDTH), f32)] + (side.out_shapes if side else []),
        scratch_shapes=_comm_scratch(ns) if side else [],
        compiler_params=_cparams(("arbitrary", "arbitrary")),
    )(proj, k16, k0_16, k1_16, v16, runs, do, *(side.arrs if side else []))


def _shift_down(x, prev8, j):
    if j == 0:
        return x
    r = pltpu.roll(x, j, axis=0)
    row8 = lax.broadcasted_iota(jnp.int32, prev8.shape, 0)
    head = jnp.where(row8 < j, pltpu.roll(prev8, j, axis=0), r[0:SUBLANES])
    return jnp.concatenate([head, r[SUBLANES:]], axis=0)


def _shift_up(x, next8, j):
    if j == 0:
        return x
    n = x.shape[0]
    r = pltpu.roll(x, n - j, axis=0)
    row8 = lax.broadcasted_iota(jnp.int32, next8.shape, 0)
    tail = jnp.where(row8 >= SUBLANES - j, pltpu.roll(next8, SUBLANES - j, axis=0), r[n - SUBLANES:n])
    return jnp.concatenate([r[:n - SUBLANES], tail], axis=0)


def _conv(x, prev8, w):
    k_taps = w.shape[0]
    out = x * w[k_taps - 1:k_taps, :]
    for j in range(1, k_taps):
        out = out + _shift_down(x, prev8, j) * w[k_taps - 1 - j:k_taps - j, :]
    return out


def _conv_tiles(rows, tr_max=TCONV_R):
    tr = min(tr_max, rows)
    return tr, rows // tr, tr // SUBLANES


def _prev_spec(tc, cb0, r8):
    return pl.BlockSpec((SUBLANES, tc), lambda j, i: (jnp.maximum(i * r8 - 1, 0), cb0 + j))


def _silu(x):
    return x * jax.nn.sigmoid(x)


def _dsilu(x):
    s = jax.nn.sigmoid(x)
    return s * (1.0 + x * (1.0 - s))


def _dn_conv_fwd(proj, w):
    rows = proj.shape[0]
    tr, nr, r8 = _conv_tiles(rows)
    tc = TCONV_C
    cb0 = OFF_DN // tc

    def body(x_ref, p_ref, w_ref, o_ref):
        prev = jnp.where(pl.program_id(1) == 0, 0.0, p_ref[...])
        o_ref[...] = _silu(_conv(x_ref[...], prev, w_ref[...]))

    return pl.pallas_call(
        body, name="dn_conv_fwd", grid=(DN_CONV_CH // tc, nr),
        in_specs=[pl.BlockSpec((tr, tc), lambda j, i: (i, cb0 + j)), _prev_spec(tc, cb0, r8),
                  pl.BlockSpec((DN_CONV_WIDTH, tc), lambda j, i: (0, j))],
        out_specs=pl.BlockSpec((tr, tc), lambda j, i: (i, j)),
        out_shape=jax.ShapeDtypeStruct((rows, DN_CONV_CH), f32),
        compiler_params=_cparams(("parallel", "parallel")),
    )(proj, proj, w)


def _dn_conv_bwd_act(proj, w, dact):
    rows = proj.shape[0]
    tr, nr, r8 = _conv_tiles(rows)
    tc = TCONV_C
    cb0 = OFF_DN // tc

    def body(x_ref, p_ref, w_ref, d_ref, o_ref):
        prev = jnp.where(pl.program_id(1) == 0, 0.0, p_ref[...])
        o_ref[...] = d_ref[...] * _dsilu(_conv(x_ref[...], prev, w_ref[...]))

    return pl.pallas_call(
        body, name="dn_conv_bwd_act", grid=(DN_CONV_CH // tc, nr),
        in_specs=[pl.BlockSpec((tr, tc), lambda j, i: (i, cb0 + j)), _prev_spec(tc, cb0, r8),
                  pl.BlockSpec((DN_CONV_WIDTH, tc), lambda j, i: (0, j)),
                  pl.BlockSpec((tr, tc), lambda j, i: (i, j))],
        out_specs=pl.BlockSpec((tr, tc), lambda j, i: (i, j)),
        out_shape=jax.ShapeDtypeStruct((rows, DN_CONV_CH), f32),
        compiler_params=_cparams(("parallel", "parallel")),
    )(proj, proj, w, dact)


def _ffn_conv_fwd(u_pre, w, b):
    rows = u_pre.shape[0]
    tr, nr, r8 = _conv_tiles(rows, TCONV_R // 2)
    tc = TCONV_FF
    nct = D_FF // tc

    def body(xg_ref, pg_ref, xu_ref, pu_ref, wg_ref, wu_ref, bg_ref, bu_ref, o_ref):
        first = pl.program_id(1) == 0
        ug = _conv(xg_ref[...], jnp.where(first, 0.0, pg_ref[...]), wg_ref[...]) + bg_ref[...]
        uu = _conv(xu_ref[...], jnp.where(first, 0.0, pu_ref[...]), wu_ref[...]) + bu_ref[...]
        o_ref[...] = (_silu(ug) * uu).astype(o_ref.dtype)

    def x_spec(off):
        return pl.BlockSpec((tr, tc), lambda j, i: (i, off + j))

    def w_spec(k, off):
        return pl.BlockSpec((k, tc), lambda j, i: (0, off + j))

    return pl.pallas_call(
        body, name="ffn_conv_fwd", grid=(nct, nr),
        in_specs=[x_spec(0), _prev_spec(tc, 0, r8), x_spec(nct), _prev_spec(tc, nct, r8),
                  w_spec(FFN_CONV_WIDTH, 0), w_spec(FFN_CONV_WIDTH, nct), w_spec(1, 0), w_spec(1, nct)],
        out_specs=pl.BlockSpec((tr, tc), lambda j, i: (i, j)),
        out_shape=jax.ShapeDtypeStruct((rows, D_FF), bf16),
        compiler_params=_cparams(("parallel", "parallel")),
    )(u_pre, u_pre, u_pre, u_pre, w, w, b, b)


def _ffn_conv_bwd_act(u_pre, w, b, dact):
    rows = u_pre.shape[0]
    tr, nr, r8 = _conv_tiles(rows, TCONV_R // 2)
    tc = TCONV_FF
    nct = D_FF // tc

    def body(xg_ref, pg_ref, xu_ref, pu_ref, wg_ref, wu_ref, bg_ref, bu_ref, d_ref,
             du_ref, dbg_ref, dbu_ref):
        i = pl.program_id(1)
        first = i == 0
        ug = _conv(xg_ref[...], jnp.where(first, 0.0, pg_ref[...]), wg_ref[...]) + bg_ref[...]
        uu = _conv(xu_ref[...], jnp.where(first, 0.0, pu_ref[...]), wu_ref[...]) + bu_ref[...]
        d = d_ref[...]
        dug = d * uu * _dsilu(ug)
        duu = d * _silu(ug)
        du_ref[0] = dug
        du_ref[1] = duu

        @pl.when(first)
        def _():
            dbg_ref[...] = jnp.zeros_like(dbg_ref)
            dbu_ref[...] = jnp.zeros_like(dbu_ref)

        dbg_ref[...] += jnp.sum(dug, axis=0, keepdims=True)
        dbu_ref[...] += jnp.sum(duu, axis=0, keepdims=True)

    def x_spec(off):
        return pl.BlockSpec((tr, tc), lambda j, i: (i, off + j))

    def w_spec(k, off):
        return pl.BlockSpec((k, tc), lambda j, i: (0, off + j))

    tile = pl.BlockSpec((tr, tc), lambda j, i: (i, j))
    vec = pl.BlockSpec((1, tc), lambda j, i: (0, j))
    return pl.pallas_call(
        body, name="ffn_conv_bwd_act", grid=(nct, nr),
        in_specs=[x_spec(0), _prev_spec(tc, 0, r8), x_spec(nct), _prev_spec(tc, nct, r8),
                  w_spec(FFN_CONV_WIDTH, 0), w_spec(FFN_CONV_WIDTH, nct), w_spec(1, 0), w_spec(1, nct), tile],
        out_specs=[pl.BlockSpec((2, tr, tc), lambda j, i: (0, i, j)), vec, vec],
        out_shape=[jax.ShapeDtypeStruct((2, rows, D_FF), f32),
                   jax.ShapeDtypeStruct((1, D_FF), f32), jax.ShapeDtypeStruct((1, D_FF), f32)],
        compiler_params=_cparams(("parallel", "arbitrary")),
    )(u_pre, u_pre, u_pre, u_pre, w, w, b, b, dact)


def _conv_bwd(dy, x, x_cb0, w, name):
    k_taps = w.shape[0]
    split = dy.ndim == 3
    rows = dy.shape[-2]
    ch = dy.shape[-1] * (2 if split else 1)
    tc = TCONV_FF if split else TCONV_C
    tr, nr, r8 = _conv_tiles(rows, TCONV_R // 2 if split else TCONV_R)
    per_half = dy.shape[-1] // tc
    last8 = rows // SUBLANES - 1

    def body(dy_ref, nx_ref, x_ref, p_ref, w_ref, dx_ref, dw_ref):
        i = pl.program_id(1)
        dyv = dy_ref[...]
        nxt = jnp.where(i == nr - 1, 0.0, nx_ref[...])
        prev = jnp.where(i == 0, 0.0, p_ref[...])
        xv = x_ref[...].astype(f32)
        wv = w_ref[...]

        @pl.when(i == 0)
        def _():
            dw_ref[...] = jnp.zeros_like(dw_ref)

        dx = dyv * wv[k_taps - 1:k_taps, :]
        dw_ref[k_taps - 1:k_taps, :] += jnp.sum(dyv * xv, axis=0, keepdims=True)
        for j in range(1, k_taps):
            dx = dx + _shift_up(dyv, nxt, j) * wv[k_taps - 1 - j:k_taps - j, :]
            dw_ref[k_taps - 1 - j:k_taps - j, :] += jnp.sum(dyv * _shift_down(xv, prev, j), axis=0, keepdims=True)
        dx_ref[...] = dx.astype(dx_ref.dtype)

    tile = pl.BlockSpec((tr, tc), lambda j, i: (i, j))
    if split:
        dy_spec = pl.BlockSpec((None, tr, tc), lambda j, i: (j // per_half, i, j % per_half))
        next_spec = pl.BlockSpec((None, SUBLANES, tc),
                                 lambda j, i: (j // per_half, jnp.minimum((i + 1) * r8, last8), j % per_half))
    else:
        dy_spec = tile
        next_spec = pl.BlockSpec((SUBLANES, tc), lambda j, i: (jnp.minimum((i + 1) * r8, last8), j))
    return pl.pallas_call(
        body, name=name, grid=(ch // tc, nr),
        in_specs=[dy_spec, next_spec,
                  pl.BlockSpec((tr, tc), lambda j, i: (i, x_cb0 + j)), _prev_spec(tc, x_cb0, r8),
                  pl.BlockSpec((k_taps, tc), lambda j, i: (0, j))],
        out_specs=[tile, pl.BlockSpec((k_taps, tc), lambda j, i: (0, j))],
        out_shape=[jax.ShapeDtypeStruct((rows, ch), bf16), jax.ShapeDtypeStruct((k_taps, ch), f32)],
        compiler_params=_cparams(("parallel", "arbitrary")),
    )(dy, dy, x, x, w)


def _hdot(a, b):
    return jnp.dot(a, b, preferred_element_type=f32, precision=lax.Precision.HIGH)


def _xdot(a, b):
    return jnp.dot(a, b, preferred_element_type=f32, precision=lax.Precision.HIGHEST)


def _bdot(a, b):
    return jnp.dot(a.astype(bf16), b.astype(bf16), preferred_element_type=f32)


def _bdot_nt(a, b):
    return lax.dot_general(a.astype(bf16), b.astype(bf16), (((1,), (1,)), ((), ())), preferred_element_type=f32)


def _bdot_tn(a, b):
    return lax.dot_general(a.astype(bf16), b.astype(bf16), (((0,), (0,)), ((), ())), preferred_element_type=f32)


GDN_GROUP = 4
GDN_NGROUPS = DN_HEADS // GDN_GROUP
GDN_ROWS = GDN_GROUP * DN_CHUNK
GDN_QK_LANES = GDN_GROUP * DN_KEY_DIM
GDN_LOGIT_LANE = DN_HEADS


def _inverse_impl(lows):
    n = lows[0].shape[0]
    r = lax.broadcasted_iota(jnp.int32, (n, n), 0)
    c = lax.broadcasted_iota(jnp.int32, (n, n), 1)
    eye = (r == c).astype(f32)
    blk = jnp.right_shift(r, 4) == jnp.right_shift(c, 4)
    d = [jnp.where(blk, low, 0.0) for low in lows]
    e = [low - x for low, x in zip(lows, d)]
    d2 = [_bdot(x, x) for x in d]
    d4 = [_bdot(x, x) for x in d2]
    d8 = [_bdot(x, x) for x in d4]
    p = [_bdot(eye - x, eye + y) for x, y in zip(d, d2)]
    p = [_bdot(x, eye + y) for x, y in zip(p, d4)]
    dinv = [_bdot(x, eye + y) for x, y in zip(p, d8)]
    nn = [_bdot(x, y) for x, y in zip(dinv, e)]
    n2 = [_bdot(x, x) for x in nn]
    ninv = [_bdot(eye - x, eye + y) for x, y in zip(nn, n2)]
    t = [_bdot(x, y) for x, y in zip(ninv, dinv)]
    for _ in range(2):
        res = [eye - x - _hdot(low, x) for low, x in zip(lows, t)]
        t = [x + _bdot(x, y) for x, y in zip(t, res)]
    return tuple(t)


@jax.custom_vjp
def _unit_lower_inverses(lows):
    return _inverse_impl(lows)


def _unit_lower_inverses_fwd(lows):
    t = _inverse_impl(lows)
    return t, t


def _unit_lower_inverses_bwd(t, ct):
    tn = (((0,), (0,)), ((), ()))
    nt = (((1,), (1,)), ((), ()))
    left = [lax.dot_general(x, g, tn, preferred_element_type=f32, precision=lax.Precision.HIGH) for x, g in zip(t, ct)]
    return (tuple(-lax.dot_general(x, y, nt, preferred_element_type=f32, precision=lax.Precision.HIGH)
                  for x, y in zip(left, t)),)


_unit_lower_inverses.defvjp(_unit_lower_inverses_fwd, _unit_lower_inverses_bwd)


def _gdn_chunk(a_log, dt_bias, norm_w, ba, *per_group):
    ng = GDN_NGROUPS
    qgs, kgs, vsts, zsts, states = [per_group[i * ng:(i + 1) * ng] for i in range(5)]
    groups = range(ng)
    n = GDN_ROWS
    r = lax.broadcasted_iota(jnp.int32, (n, n), 0)
    c = lax.broadcasted_iota(jnp.int32, (n, n), 1)
    same_head = jnp.right_shift(r, 6) == jnp.right_shift(c, 6)
    incl = jnp.logical_and(same_head, r >= c)
    strict = jnp.logical_and(same_head, r > c)
    eye = (r == c).astype(f32)
    ones = jnp.ones((n, n), f32)
    own_lanes = same_head.astype(f32)
    lane = lax.broadcasted_iota(jnp.int32, (1, LANES), 1)
    pick = lambda arr, idx: jnp.sum(jnp.where(lane == idx, arr, 0.0), axis=1, keepdims=True)
    heads = [[GDN_GROUP * g + h for h in range(GDN_GROUP)] for g in groups]
    rc = lax.broadcasted_iota(jnp.int32, (DN_CHUNK, DN_CHUNK), 0)
    cc = lax.broadcasted_iota(jnp.int32, (DN_CHUNK, DN_CHUNK), 1)

    g_all = -jnp.exp(a_log) * _softplus(ba + dt_bias)
    gc_all = _xdot((rc >= cc).astype(f32), g_all)
    gl_all = jnp.sum(g_all, axis=0, keepdims=True)
    beta = [jnp.concatenate([jax.nn.sigmoid(pick(ba, hd)) for hd in heads[g]], axis=0) for g in groups]
    gc = [jnp.concatenate([pick(gc_all, GDN_LOGIT_LANE + hd) for hd in heads[g]], axis=0) for g in groups]
    g_last = [jnp.concatenate([jnp.broadcast_to(pick(gl_all, GDN_LOGIT_LANE + hd), (DN_CHUNK, 1)) for hd in heads[g]],
                              axis=0) for g in groups]
    gr = [_hdot(ones, eye * gc[g]) for g in groups]
    decay = [jnp.where(incl, jnp.exp(jnp.where(incl, gc[g] - gr[g], 0.0)), 0.0) for g in groups]
    q = [jnp.concatenate([qgs[g]] * GDN_GROUP, axis=0) * own_lanes for g in groups]
    k = [jnp.concatenate([kgs[g]] * GDN_GROUP, axis=0) * own_lanes for g in groups]
    qn = [x * lax.rsqrt(jnp.sum(x * x, axis=1, keepdims=True) + L2_EPS) * (DN_KEY_DIM ** -0.5) for x in q]
    kn = [x * lax.rsqrt(jnp.sum(x * x, axis=1, keepdims=True) + L2_EPS) for x in k]
    kb = [kn[g] * beta[g] for g in groups]
    low = [jnp.where(strict, _bdot_nt(kb[g], kn[g]) * decay[g], 0.0) for g in groups]
    intra = [jnp.where(incl, _bdot_nt(qn[g], kn[g]) * decay[g], 0.0) for g in groups]
    t = _unit_lower_inverses(tuple(low))
    u = [_bdot(t[g], vsts[g] * beta[g]) for g in groups]
    w = [_bdot(t[g], kb[g] * jnp.exp(gc[g])) for g in groups]
    sb = [s.astype(bf16) for s in states]
    v_new = [u[g] - jnp.dot(w[g].astype(bf16), sb[g], preferred_element_type=f32) for g in groups]
    o = [jnp.dot((qn[g] * jnp.exp(gc[g])).astype(bf16), sb[g], preferred_element_type=f32) for g in groups]
    o = [o[g] + _bdot(intra[g], v_new[g]) for g in groups]
    new_state = [states[g] * jnp.exp(g_last[g]) + _bdot_tn(kn[g] * jnp.exp(g_last[g] - gc[g]), v_new[g])
                 for g in groups]
    o_n = [x * lax.rsqrt(jnp.mean(x * x, axis=1, keepdims=True) + NORM_EPS) * norm_w for x in o]
    return tuple(o_n[g] * _silu(zsts[g]) for g in groups) + tuple(new_state)


def _gdn_specs(rows, reverse):
    n = rows // DN_CHUNK
    idx = (lambda i: n - 1 - i) if reverse else (lambda i: i)
    vec = pl.BlockSpec((1, LANES), lambda i: (0, 0))
    qkv = pl.BlockSpec((DN_CHUNK, DN_CONV_CH), lambda i: (idx(i), 0))
    z = pl.BlockSpec((DN_CHUNK, DN_V_WIDTH), lambda i: (idx(i), OFF_Z // DN_V_WIDTH))
    ba = pl.BlockSpec((DN_CHUNK, LANES), lambda i: (idx(i), 0))
    wide = pl.BlockSpec((DN_CHUNK, DN_V_WIDTH), lambda i: (idx(i), 0))
    st = pl.BlockSpec((1, DN_HEADS * DN_KEY_DIM, LANES), lambda i: (idx(i), 0, 0))
    return n, vec, qkv, z, ba, wide, st


def _gdn_slices(grp):
    q = slice(grp * GDN_QK_LANES, (grp + 1) * GDN_QK_LANES)
    k = slice(DN_QK_WIDTH + grp * GDN_QK_LANES, DN_QK_WIDTH + (grp + 1) * GDN_QK_LANES)
    heads = [slice((GDN_GROUP * grp + h) * LANES, (GDN_GROUP * grp + h + 1) * LANES) for h in range(GDN_GROUP)]
    vs = [slice(2 * DN_QK_WIDTH + s.start, 2 * DN_QK_WIDTH + s.stop) for s in heads]
    return q, k, vs, heads


def _stack_cols(ref, cols):
    return jnp.concatenate([ref[:, s] for s in cols], axis=0)


def _gdn_operands(qkv_ref, z_ref, state_rows):
    sl = [_gdn_slices(grp) for grp in range(GDN_NGROUPS)]
    return ([qkv_ref[:, q] for q, _, _, _ in sl] + [qkv_ref[:, k] for _, k, _, _ in sl]
            + [_stack_cols(qkv_ref, vs) for _, _, vs, _ in sl] + [_stack_cols(z_ref, heads) for _, _, _, heads in sl]
            + [state_rows[grp * GDN_ROWS:(grp + 1) * GDN_ROWS, :] for grp in range(GDN_NGROUPS)])


def _gdn_fwd(a_log, dt_bias, norm_w, qkv_act, proj, ba):
    rows = qkv_act.shape[0]
    n, vec, qkv_s, z_s, ba_s, wide, st_s = _gdn_specs(rows, False)

    def body(al_ref, dt_ref, nw_ref, qkv_ref, z_ref, ba_ref, o_ref, st_ref, state):
        @pl.when(pl.program_id(0) == 0)
        def _():
            state[...] = jnp.zeros_like(state)

        st_ref[0] = state[...]
        out = _gdn_chunk(al_ref[...], dt_ref[...], nw_ref[...], ba_ref[...], *_gdn_operands(qkv_ref, z_ref, state))
        for grp in range(GDN_NGROUPS):
            _, _, _, heads = _gdn_slices(grp)
            for h, s in enumerate(heads):
                o_ref[:, s] = out[grp][h * DN_CHUNK:(h + 1) * DN_CHUNK].astype(o_ref.dtype)
            state[grp * GDN_ROWS:(grp + 1) * GDN_ROWS, :] = out[GDN_NGROUPS + grp]

    return pl.pallas_call(
        body, name="gdn_fwd", grid=(n,),
        in_specs=[vec, vec, vec, qkv_s, z_s, ba_s], out_specs=[wide, st_s],
        out_shape=[jax.ShapeDtypeStruct((rows, DN_V_WIDTH), bf16),
                   jax.ShapeDtypeStruct((n, DN_HEADS * DN_KEY_DIM, LANES), f32)],
        scratch_shapes=[pltpu.VMEM((DN_HEADS * DN_KEY_DIM, LANES), f32)],
        compiler_params=_cparams(("arbitrary",)),
    )(a_log, dt_bias, norm_w, qkv_act, proj, ba)


def _gdn_bwd(a_log, dt_bias, norm_w, qkv_act, proj, ba, states, do):
    rows = qkv_act.shape[0]
    n, vec, qkv_s, z_s, ba_s, wide, st_s = _gdn_specs(rows, True)

    def body(al_ref, dt_ref, nw_ref, qkv_ref, z_ref, ba_ref, st_ref, do_ref,
             dal_ref, ddt_ref, dnw_ref, dqkv_ref, dz_ref, dba_ref, dstate):
        @pl.when(pl.program_id(0) == 0)
        def _():
            dstate[...] = jnp.zeros_like(dstate)
            dal_ref[...] = jnp.zeros_like(dal_ref)
            ddt_ref[...] = jnp.zeros_like(ddt_ref)
            dnw_ref[...] = jnp.zeros_like(dnw_ref)

        ng = GDN_NGROUPS
        _, vjp = jax.vjp(_gdn_chunk, al_ref[...], dt_ref[...], nw_ref[...], ba_ref[...],
                         *_gdn_operands(qkv_ref, z_ref, st_ref[0]))
        cts = tuple(_stack_cols(do_ref, _gdn_slices(grp)[3]) for grp in range(ng))
        cts += tuple(dstate[grp * GDN_ROWS:(grp + 1) * GDN_ROWS, :] for grp in range(ng))
        grads = vjp(cts)
        dal_ref[...] += grads[0]
        ddt_ref[...] += grads[1]
        dnw_ref[...] += grads[2]
        dba_ref[...] = grads[3]
        dqs, dks, dvs, dzs, dss = [grads[4 + i * ng:4 + (i + 1) * ng] for i in range(5)]
        for grp in range(ng):
            q, k, vs, heads = _gdn_slices(grp)
            dqkv_ref[:, q] = dqs[grp]
            dqkv_ref[:, k] = dks[grp]
            for h, (sv, sh) in enumerate(zip(vs, heads)):
                rows_h = slice(h * DN_CHUNK, (h + 1) * DN_CHUNK)
                dqkv_ref[:, sv] = dvs[grp][rows_h]
                dz_ref[:, sh] = dzs[grp][rows_h].astype(dz_ref.dtype)
            dstate[grp * GDN_ROWS:(grp + 1) * GDN_ROWS, :] = dss[grp]

    return pl.pallas_call(
        body, name="gdn_bwd", grid=(n,),
        in_specs=[vec, vec, vec, qkv_s, z_s, ba_s, st_s, wide],
        out_specs=[vec, vec, vec, qkv_s, wide, ba_s],
        out_shape=[jax.ShapeDtypeStruct((1, LANES), f32)] * 3
        + [jax.ShapeDtypeStruct((rows, DN_CONV_CH), f32), jax.ShapeDtypeStruct((rows, DN_V_WIDTH), bf16),
           jax.ShapeDtypeStruct((rows, LANES), f32)],
        scratch_shapes=[pltpu.VMEM((DN_HEADS * DN_KEY_DIM, LANES), f32)],
        compiler_params=_cparams(("arbitrary",)),
    )(a_log, dt_bias, norm_w, qkv_act, proj, ba, states, do)


def _ada_fwd(c_all, w_loc, b_loc):
    def body(c_ref, w_ref, b_ref, o_ref):
        o_ref[...] = _bdot(_silu(c_ref[...]), w_ref[...]) + b_ref[...]

    return pl.pallas_call(body, name="ada_fwd", out_shape=jax.ShapeDtypeStruct((c_all.shape[0], w_loc.shape[1]), f32),
                          compiler_params=_cparams())(c_all, w_loc, b_loc)


def _ada_bwd(c_all, dmod_cols):
    def body(c_ref, d_ref, o_ref):
        o_ref[...] = _bdot_tn(_silu(c_ref[...]), d_ref[...])

    return pl.pallas_call(body, name="ada_bwd",
                          out_shape=jax.ShapeDtypeStruct((c_all.shape[1], dmod_cols.shape[1]), f32),
                          compiler_params=_cparams())(c_all, dmod_cols)


def _sum_devices(parts):
    def body(p_ref, o_ref):
        acc = p_ref[0:1, :]
        for d in range(1, N_DEV):
            acc = acc + p_ref[d:d + 1, :]
        o_ref[...] = acc

    return pl.pallas_call(body, name="sum_small", out_shape=jax.ShapeDtypeStruct((1, parts.shape[1]), f32),
                          compiler_params=_cparams())(parts)


def _adam_math(w, g, m, v):
    m2 = ADAM_B1 * m + (1.0 - ADAM_B1) * g
    v2 = ADAM_B2 * v + (1.0 - ADAM_B2) * jnp.square(g)
    m_hat = m2 / (1.0 - ADAM_B1 ** ADAM_STEP)
    v_hat = v2 / (1.0 - ADAM_B2 ** ADAM_STEP)
    delta = -ADAM_LR * (m_hat / (jnp.sqrt(v_hat) + ADAM_EPS) + ADAM_WD * w)
    return delta, m2, v2


def _row_tile(rows):
    return _pick(rows, (256, 128, 64, 32, 16, 8))


def _adamw(w, g, m, v, name):
    rows, cols = w.shape
    tr = _row_tile(rows)

    def body(w_ref, g_ref, m_ref, v_ref, d_ref, m2_ref, v2_ref):
        d_ref[...], m2_ref[...], v2_ref[...] = _adam_math(w_ref[...], g_ref[...], m_ref[...], v_ref[...])

    tile = pl.BlockSpec((tr, cols), lambda i: (i, 0))
    return pl.pallas_call(body, name=name, grid=(rows // tr,), in_specs=[tile] * 4, out_specs=[tile] * 3,
                          out_shape=[jax.ShapeDtypeStruct(w.shape, f32)] * 3,
                          compiler_params=_cparams(("parallel",)))(w, g, m, v)


def _sum_adamw(parts, w, m, v, name):
    rows, cols = w.shape
    tr = _row_tile(rows)

    def body(p_ref, w_ref, m_ref, v_ref, g_ref, d_ref, m2_ref, v2_ref):
        g = p_ref[0].astype(f32)
        for d in range(1, N_DEV):
            g = g + p_ref[d].astype(f32)
        g_ref[...] = g
        d_ref[...], m2_ref[...], v2_ref[...] = _adam_math(w_ref[...], g, m_ref[...], v_ref[...])

    tile = pl.BlockSpec((tr, cols), lambda i: (i, 0))
    return pl.pallas_call(body, name=name, grid=(rows // tr,),
                          in_specs=[pl.BlockSpec((N_DEV, tr, cols), lambda i: (0, i, 0)), tile, tile, tile],
                          out_specs=[tile] * 4, out_shape=[jax.ShapeDtypeStruct(w.shape, f32)] * 4,
                          compiler_params=_cparams(("parallel",)))(parts, w, m, v)


def _pad_lanes(a, width):
    return jnp.pad(a, ((0, 0), (0, width - a.shape[1])))


def _cols_by_device(full):
    r, c = full.shape
    return jnp.moveaxis(full.reshape(r, N_DEV, c // N_DEV), 1, 0)


def _cols_from_devices(parts):
    d, r, n = parts.shape
    return jnp.moveaxis(parts, 0, 1).reshape(r, d * n)


def kernel(x, c, w_ada, b_ada, norm1_w, w_in, dn_conv_w, dn_A_log, dn_dt_bias, dn_norm_w, w_proj_sb, w_proj_dn, w_out, norm2_w, w_ffn_in, ffn_conv_w, ffn_conv_b, w_ffn_out, final_norm_w, loss_target, m_w_ada, m_b_ada, m_norm1_w, m_w_in, m_dn_conv_w, m_dn_A_log, m_dn_dt_bias, m_dn_norm_w, m_w_proj_sb, m_w_proj_dn, m_w_out, m_norm2_w, m_w_ffn_in, m_ffn_conv_w, m_ffn_conv_b, m_w_ffn_out, m_final_norm_w, v_w_ada, v_b_ada, v_norm1_w, v_w_in, v_dn_conv_w, v_dn_A_log, v_dn_dt_bias, v_dn_norm_w, v_w_proj_sb, v_w_proj_dn, v_w_out, v_norm2_w, v_w_ffn_in, v_ffn_conv_w, v_ffn_conv_b, v_w_ffn_out, v_final_norm_w):
    d = D_MODEL
    me = 4 * lax.axis_index("x") + 2 * lax.axis_index("y") + lax.axis_index("c")
    xs = x[0]
    target = loss_target[0]
    n_ada = w_ada.shape[2]
    n_dnc = dn_conv_w.shape[2]
    n_ffc = ffn_conv_w.shape[2]

    small = jnp.concatenate([c, dn_conv_w[0].reshape(1, -1), ffn_conv_w[0].reshape(1, -1)], axis=1)
    small = _pad_lanes(small, -(-small.shape[1] // LANES) * LANES)
    small_g, w_in_g = _all_gather([small, w_in[0].astype(bf16)], "gather_w_in")
    later = [w_proj_sb[0].astype(bf16), w_proj_dn[0].astype(bf16), w_out[0].astype(bf16),
             w_ffn_in[0].astype(bf16), w_ffn_out[0].astype(bf16)]
    gather_later = _SideComm(_gather_protocol, later, _gathered_shapes(later))
    small_g = small_g[:, 0, :]
    c_all = small_g[:, :d]
    dn_cw = _cols_from_devices(small_g[:, d:d + DN_CONV_WIDTH * n_dnc].reshape(N_DEV, DN_CONV_WIDTH, n_dnc))
    o2 = d + DN_CONV_WIDTH * n_dnc
    ffn_cw = _cols_from_devices(small_g[:, o2:o2 + FFN_CONV_WIDTH * n_ffc].reshape(N_DEV, FFN_CONV_WIDTH, n_ffc))

    w_in_full = _cols_from_devices(w_in_g)
    r_sb, r_dn, r_z = 3 * SB_WIDTH, 3 * SB_WIDTH + DN_CONV_CH, 3 * SB_WIDTH + DN_CONV_CH + DN_V_WIDTH
    r_g = r_z + 2 * DN_HEADS
    w_main = jnp.concatenate([w_in_full[:, r_g:], w_in_full[:, r_sb:r_dn], w_in_full[:, r_dn:r_z],
                              w_in_full[:, :r_sb]], axis=1)
    w_ba = _pad_lanes(w_in_full[:, r_z:r_g], LANES)

    b_loc = lax.dynamic_slice(b_ada, (0, me * n_ada), (1, n_ada))
    mod_part = _ada_fwd(c_all, w_ada[0], b_loc)
    (mod_g,) = _all_gather([mod_part], "gather_mod")
    mod = lax.dynamic_index_in_dim(mod_g, me, axis=1, keepdims=False).reshape(1, N_DEV * n_ada)
    shift1, scale1, gate1, shift2, scale2, gate2 = [mod[:, i * d:(i + 1) * d] for i in range(6)]

    logit_lanes = ((0, 0), (GDN_LOGIT_LANE, LANES - GDN_LOGIT_LANE - DN_HEADS))
    a_log = jnp.pad(dn_A_log, logit_lanes)
    dt_b = jnp.pad(dn_dt_bias, logit_lanes)

    (h,) = _stage_fwd(_f_normmod, [norm1_w, shift1, scale1], [xs], [bf16], "norm1_fwd")
    proj = _mm(h, w_main, name="in_proj")
    ba = _mm(h, w_ba, name="in_proj_ba")
    k16, k0_16, k1_16, v16, v0_16, v1_16 = _sb_prepare(proj)
    o_a, sb_runs, w_psb_g, w_pdn_g, w_out_g, w_fin_g, w_fout_g = _sb_attention_fwd2(
        proj, k16, v0_16, v1_16, side=gather_later)
    w_psb = _cols_from_devices(w_psb_g)
    w_pdn = w_pdn_g.reshape(DN_V_WIDTH, d)
    w_o = w_out_g.reshape(d, d)
    w_fin = _cols_from_devices(w_fin_g)
    w_fout = w_fout_g.reshape(D_FF, d)
    qkv_act = _dn_conv_fwd(proj, dn_cw)
    o_b, states = _gdn_fwd(a_log, dt_b, dn_norm_w, qkv_act, proj, ba)
    pa = _mm(o_a, w_psb, name="proj_sb")
    pb = _mm(o_b, w_pdn, name="proj_dn")
    gates = [(proj, d, OFF_GA // d), (proj, d, OFF_GB // d)]
    (merged,) = _stage_fwd(_f_merge, [], gates + [pa, pb], [bf16], "merge_fwd")
    ao = _mm(merged, w_o, name="out_proj")
    (x1,) = _stage_fwd(_f_residual, [gate1], [xs, ao], [f32], "resid1_fwd")
    (h2,) = _stage_fwd(_f_normmod, [norm2_w, shift2, scale2], [x1], [bf16], "norm2_fwd")
    u_pre = _mm(h2, w_fin, name="ffn_in")
    act = _ffn_conv_fwd(u_pre, ffn_cw, ffn_conv_b)
    fo = _mm(act, w_fout, name="ffn_out")

    loss_p, d_gate2, d_wf, dx2, dfo = _loss_and_grads(gate2, final_norm_w.reshape(1, d), x1, fo, target)
    dact = _mm(dfo, w_fout, tb=True, name="ffn_out_dx")
    g_w_fout = _mm(act, dfo, ta=True, name="ffn_out_dw")
    du, dbg, dbu = _ffn_conv_bwd_act(u_pre, ffn_cw, ffn_conv_b, dact)
    du_pre, d_ffn_cw = _conv_bwd(du, u_pre, 0, ffn_cw, "ffn_conv_bwd")
    dh2 = _mm(du_pre, w_fin, tb=True, name="ffn_in_dx")
    g_w_fin = _mm(h2, du_pre, ta=True, name="ffn_in_dw")
    (d_n2w, d_shift2, d_scale2), (dx1,) = _stage_bwd(
        _f_normmod, [norm2_w, shift2, scale2], [x1], [dh2], [f32], "norm2_bwd", residual=(0, dx2))
    (d_gate1,), (dao,) = _stage_bwd(_f_residual, [gate1], [xs, ao], [dx1], [None, bf16], "resid1_bwd")
    dmerged = _mm(dao, w_o, tb=True, name="out_proj_dx")
    g_w_o = _mm(merged, dao, ta=True, name="out_proj_dw")
    _, (dga, dgb, dpa, dpb) = _stage_bwd(_f_merge, [], gates + [pa, pb], [dmerged], [bf16] * 4, "merge_bwd")
    do_a = _mm(dpa, w_psb, tb=True, name="proj_sb_dx")
    g_w_psb = _mm(o_a, dpa, ta=True, name="proj_sb_dw")
    do_b = _mm(dpb, w_pdn, tb=True, name="proj_dn_dx")
    g_w_pdn = _mm(o_b, dpb, ta=True, name="proj_dn_dw")
    early = [_cols_by_device(g_w_psb).astype(bf16),
             g_w_pdn.reshape(N_DEV, DN_V_WIDTH // N_DEV, d).astype(bf16),
             g_w_o.reshape(N_DEV, d // N_DEV, d).astype(bf16),
             _cols_by_device(g_w_fin).astype(bf16),
             g_w_fout.reshape(N_DEV, D_FF // N_DEV, d).astype(bf16)]
    exchange_early = _SideComm(_exchange_protocol, early, [jax.ShapeDtypeStruct(a.shape, a.dtype) for a in early])
    dq, dk, dv, *recv_early = _sb_attention_bwd2(proj, k16, k0_16, k1_16, v16, sb_runs, do_a, side=exchange_early)
    d_alog, d_dtb, d_dnw, dqkv_act, dz, dba = _gdn_bwd(a_log, dt_b, dn_norm_w, qkv_act, proj, ba, states, do_b)
    d_conv_out = _dn_conv_bwd_act(proj, dn_cw, dqkv_act)
    d_dn_pre, d_dn_cw = _conv_bwd(d_conv_out, proj, OFF_DN // TCONV_C, dn_cw, "dn_conv_bwd")
    dproj = jnp.concatenate([dga, dgb, d_dn_pre, dz, dq, dk.astype(bf16), dv.astype(bf16)], axis=1)
    dh = _mm(dproj, w_main, tb=True, name="in_proj_dx")
    dh_ba = _mm(dba, w_ba, tb=True, name="in_proj_ba_dx")
    g_w_main = _mm(h, dproj, ta=True, name="in_proj_dw")
    g_w_ba = _mm(h, dba, ta=True, name="in_proj_ba_dw")
    (d_n1w, d_shift1, d_scale1), (grad_x,) = _stage_bwd(
        _f_normmod, [norm1_w, shift1, scale1], [xs], [[dh, dh_ba]], [f32], "norm1_bwd", residual=(0, dx1))

    dmod = jnp.concatenate([d_shift1, d_scale1, d_gate1, d_shift2, d_scale2, d_gate2], axis=1)
    d_ffn_cb = jnp.concatenate([dbg, dbu], axis=1)
    small_parts = jnp.concatenate(
        [loss_p, dmod, d_n1w, d_alog, d_dtb, d_dnw, d_n2w, d_ffn_cb, d_wf,
         d_dn_cw.reshape(1, -1), d_ffn_cw.reshape(1, -1)], axis=1)
    (small_parts_g,) = _all_gather([small_parts], "gather_small_grads")
    tot = _sum_devices(small_parts_g[:, 0, :])
    offs = {}
    pos = 0
    for nm, width in (("loss", LANES), ("b_ada", 6 * d), ("norm1_w", d), ("dn_A_log", LANES), ("dn_dt_bias", LANES),
                      ("dn_norm_w", LANES), ("norm2_w", d), ("ffn_conv_b", 2 * D_FF), ("final_norm_w", d),
                      ("dn_conv_w", DN_CONV_WIDTH * DN_CONV_CH), ("ffn_conv_w", FFN_CONV_WIDTH * 2 * D_FF)):
        offs[nm] = (pos, width)
        pos += width
    seg = lambda nm: tot[:, offs[nm][0]:offs[nm][0] + offs[nm][1]]
    loss = tot[0, 0]
    g_b_ada = seg("b_ada")
    g_norm1 = seg("norm1_w")
    g_alog = seg("dn_A_log")[:, GDN_LOGIT_LANE:GDN_LOGIT_LANE + DN_HEADS]
    g_dtb = seg("dn_dt_bias")[:, GDN_LOGIT_LANE:GDN_LOGIT_LANE + DN_HEADS]
    g_dnw = seg("dn_norm_w")
    g_norm2 = seg("norm2_w")
    g_ffn_cb = seg("ffn_conv_b")
    g_fnw = seg("final_norm_w")
    g_dn_cw = lax.dynamic_slice(seg("dn_conv_w").reshape(DN_CONV_WIDTH, DN_CONV_CH), (0, me * n_dnc),
                                (DN_CONV_WIDTH, n_dnc))
    g_ffn_cw = lax.dynamic_slice(seg("ffn_conv_w").reshape(FFN_CONV_WIDTH, 2 * D_FF), (0, me * n_ffc),
                                 (FFN_CONV_WIDTH, n_ffc))

    dmod_all = small_parts_g[:, 0, offs["b_ada"][0]:offs["b_ada"][0] + 6 * d]
    g_w_ada = _ada_bwd(c_all, lax.dynamic_slice(dmod_all, (0, me * n_ada), (N_DEV, n_ada)))

    def pack(parts):
        flat = [p.reshape(1, -1) for p in parts]
        flat = [_pad_lanes(p, -(-p.shape[1] // LANES) * LANES) for p in flat]
        return jnp.concatenate(flat, axis=1), [p.shape[1] for p in flat]

    small_names_g = [g_b_ada, g_norm1, g_alog, g_dtb, g_dnw, g_norm2, g_ffn_cb, g_fnw, g_dn_cw, g_ffn_cw]
    small_w = [b_ada, norm1_w, dn_A_log, dn_dt_bias, dn_norm_w, norm2_w, ffn_conv_b, final_norm_w, dn_conv_w[0], ffn_conv_w[0]]
    small_m = [m_b_ada, m_norm1_w, m_dn_A_log, m_dn_dt_bias, m_dn_norm_w, m_norm2_w, m_ffn_conv_b, m_final_norm_w, m_dn_conv_w[0], m_ffn_conv_w[0]]
    small_v = [v_b_ada, v_norm1_w, v_dn_A_log, v_dn_dt_bias, v_dn_norm_w, v_norm2_w, v_ffn_conv_b, v_final_norm_w, v_dn_conv_w[0], v_ffn_conv_w[0]]
    pg, widths = pack(small_names_g)
    pw, _ = pack(small_w)
    pm, _ = pack(small_m)
    pv, _ = pack(small_v)
    s_delta, s_m, s_v = _adamw(pw, pg, pm, pv, "adamw_small")

    def unpack(flat):
        out, pos = [], 0
        for ref_arr, width in zip(small_w, widths):
            out.append(flat[:, pos:pos + ref_arr.size].reshape(ref_arr.shape))
            pos += width
        return out

    small_grads = [g.reshape(w_.shape) for g, w_ in zip(small_names_g, small_w)]
    small_delta, small_newm, small_newv = unpack(s_delta), unpack(s_m), unpack(s_v)

    ada_delta, ada_m, ada_v = _adamw(w_ada[0], g_w_ada, m_w_ada[0], v_w_ada[0], "adamw_ada")

    g_w_in_full = jnp.concatenate([g_w_main[:, OFF_SBQ:], g_w_main[:, OFF_DN:OFF_Z], g_w_main[:, OFF_Z:OFF_SBQ],
                                   g_w_ba[:, :2 * DN_HEADS], g_w_main[:, :OFF_DN]], axis=1)
    recv = list(_all_to_all([_cols_by_device(g_w_in_full).astype(bf16)], "exchange_w_in_grads")) + list(recv_early)
    big = {}
    for nm, parts, w_, m_, v_ in (("w_in", recv[0], w_in, m_w_in, v_w_in),
                                  ("w_proj_sb", recv[1], w_proj_sb, m_w_proj_sb, v_w_proj_sb),
                                  ("w_proj_dn", recv[2], w_proj_dn, m_w_proj_dn, v_w_proj_dn),
                                  ("w_out", recv[3], w_out, m_w_out, v_w_out),
                                  ("w_ffn_in", recv[4], w_ffn_in, m_w_ffn_in, v_w_ffn_in),
                                  ("w_ffn_out", recv[5], w_ffn_out, m_w_ffn_out, v_w_ffn_out)):
        big[nm] = [t[None] for t in _sum_adamw(parts, w_[0], m_[0], v_[0], "adamw_" + nm)]

    sg = dict(zip(["b_ada", "norm1_w", "dn_A_log", "dn_dt_bias", "dn_norm_w", "norm2_w", "ffn_conv_b", "final_norm_w",
                   "dn_conv_w", "ffn_conv_w"], range(10)))

    def small_out(table, nm):
        val = table[sg[nm]]
        return val[None] if nm in ("dn_conv_w", "ffn_conv_w") else val

    order = ["w_ada", "b_ada", "norm1_w", "w_in", "dn_conv_w", "dn_A_log", "dn_dt_bias", "dn_norm_w", "w_proj_sb",
             "w_proj_dn", "w_out", "norm2_w", "w_ffn_in", "ffn_conv_w", "ffn_conv_b", "w_ffn_out", "final_norm_w"]
    groups = []
    for k, small_table in enumerate((small_grads, small_delta, small_newm, small_newv)):
        row = []
        for nm in order:
            if nm == "w_ada":
                row.append((g_w_ada, ada_delta, ada_m, ada_v)[k][None])
            elif nm in big:
                row.append(big[nm][k])
            else:
                row.append(small_out(small_table, nm))
        groups.append(row)
    return (loss, grad_x[None], *groups[0], *groups[1], *groups[2], *groups[3])
```

```python
import functools

import jax
import jax.numpy as jnp
from jax import lax
from jax.experimental import pallas as pl
from jax.experimental.pallas import tpu as pltpu

f32 = jnp.float32
bf16 = jnp.bfloat16

D_MODEL = 1024
SB_HEADS = 8
SB_HEAD_DIM = 64
SB_WIDTH = SB_HEADS * SB_HEAD_DIM
SB_QBLOCK = 128
DN_HEADS = 8
DN_KEY_DIM = 64
DN_VAL_DIM = 128
DN_QK_WIDTH = DN_HEADS * DN_KEY_DIM
DN_V_WIDTH = DN_HEADS * DN_VAL_DIM
DN_CONV_CH = 2 * DN_QK_WIDTH + DN_V_WIDTH
DN_CONV_WIDTH = 4
DN_CHUNK = 64
D_FF = 2816
FFN_CONV_WIDTH = 3
NORM_EPS = 1e-6
L2_EPS = 1e-6
ADAM_LR = 0.001
ADAM_B1 = 0.9
ADAM_B2 = 0.999
ADAM_EPS = 1e-08
ADAM_WD = 0.01
ADAM_STEP = 10

N_DEV = 8
MESH = pl.DeviceIdType.MESH

LANES = 128
SUBLANES = 8
VMEM_LIMIT = 48 * 1024 * 1024

OFF_GA = 0
OFF_GB = D_MODEL
OFF_DN = 2 * D_MODEL
OFF_Z = OFF_DN + DN_CONV_CH
OFF_SBQ = OFF_Z + DN_V_WIDTH
OFF_SBK = OFF_SBQ + SB_WIDTH
OFF_SBV = OFF_SBK + SB_WIDTH
MAIN_WIDTH = OFF_SBV + SB_WIDTH

TM = 256
TCONV_R = 512
TCONV_C = 512
TCONV_FF = D_FF // 2
SB_PAIRS_PER_STEP = 2
SB_DEAD = -106.0
SB_NEVER = -1e30


def _cparams(sem=None):
    return pltpu.CompilerParams(dimension_semantics=sem, vmem_limit_bytes=VMEM_LIMIT)


def _pick(n, cands):
    for c in cands:
        if n % c == 0:
            return c
    return n


def _my_pos():
    return lax.axis_index("x"), lax.axis_index("y"), lax.axis_index("c")


def _flip(v, bit):
    return 1 - v if bit else v


def _comm_scratch(n):
    return [pltpu.SemaphoreType.DMA((n, 7)), pltpu.SemaphoreType.DMA((n, 7)), pltpu.SemaphoreType.DMA((n,))]


def _gather_protocol(ins, outs, send_sems, recv_sems, local_sems):
    n = len(ins)
    x, y, c = _my_pos()
    me, sibling = (x, y, c), (x, y, 1 - c)
    chips = [(1 - x, y), (x, 1 - y), (1 - x, 1 - y)]

    def slot(out, pos):
        return out.at[4 * pos[0] + 2 * pos[1] + pos[2]]

    def copy(a, k, block, to, src=None):
        return pltpu.make_async_remote_copy(
            src_ref=slot(outs[a], block) if src is None else src, dst_ref=slot(outs[a], block),
            send_sem=send_sems.at[a, k], recv_sem=recv_sems.at[a, k], device_id=to, device_id_type=MESH)

    def local(a):
        return pltpu.make_async_copy(ins[a], slot(outs[a], me), local_sems.at[a])

    def first(a):
        return [copy(a, 0, me, sibling, src=ins[a])] + [copy(a, 1 + j, me, (*chip, c), src=ins[a])
                                                         for j, chip in enumerate(chips)]

    def start():
        for a in range(n):
            local(a).start()
            for cp in first(a):
                cp.start()

    def finish():
        forwards = []
        for a in range(n):
            for j, chip in enumerate(chips):
                copy(a, 1 + j, (*chip, c), me).wait_recv()
                fwd = copy(a, 4 + j, (*chip, c), sibling)
                fwd.start()
                forwards.append(fwd)
        for a in range(n):
            copy(a, 0, sibling, me).wait_recv()
            for j, chip in enumerate(chips):
                copy(a, 4 + j, (*chip, 1 - c), me).wait_recv()
        for a in range(n):
            for cp in first(a):
                cp.wait_send()
        for cp in forwards:
            cp.wait_send()
        for a in range(n):
            local(a).wait()

    return start, finish


def _exchange_protocol(ins, outs, send_sems, recv_sems, local_sems):
    n = len(ins)
    x, y, c = _my_pos()
    me_idx = 4 * x + 2 * y + c

    def local(a):
        return pltpu.make_async_copy(ins[a].at[me_idx], outs[a].at[me_idx], local_sems.at[a])

    def copies(a, m):
        peer = (_flip(x, m & 4), _flip(y, m & 2), _flip(c, m & 1))
        peer_idx = 4 * peer[0] + 2 * peer[1] + peer[2]
        sems = dict(send_sem=send_sems.at[a, m - 1], recv_sem=recv_sems.at[a, m - 1], device_id=peer,
                    device_id_type=MESH)
        send = pltpu.make_async_remote_copy(src_ref=ins[a].at[peer_idx], dst_ref=outs[a].at[me_idx], **sems)
        recv = pltpu.make_async_remote_copy(src_ref=ins[a].at[peer_idx], dst_ref=outs[a].at[peer_idx], **sems)
        return send, recv

    def start():
        for a in range(n):
            local(a).start()
            for m in range(1, N_DEV):
                copies(a, m)[0].start()

    def finish():
        for a in range(n):
            for m in range(1, N_DEV):
                copies(a, m)[1].wait_recv()
        for a in range(n):
            for m in range(1, N_DEV):
                copies(a, m)[0].wait_send()
            local(a).wait()

    return start, finish


def _collective_call(protocol, arrs, out_shapes, name):
    n = len(arrs)

    def body(*refs):
        start, finish = protocol(refs[:n], refs[n:2 * n], *refs[2 * n:])
        start()
        finish()

    any_spec = pl.BlockSpec(memory_space=pl.ANY)
    return pl.pallas_call(body, name=name, out_shape=out_shapes, in_specs=[any_spec] * n, out_specs=[any_spec] * n,
                          scratch_shapes=_comm_scratch(n))(*arrs)


def _gathered_shapes(arrs):
    return [jax.ShapeDtypeStruct((N_DEV,) + a.shape, a.dtype) for a in arrs]


def _all_gather(arrs, name):
    return _collective_call(_gather_protocol, arrs, _gathered_shapes(arrs), name)


def _all_to_all(arrs, name):
    return _collective_call(_exchange_protocol, arrs, [jax.ShapeDtypeStruct(a.shape, a.dtype) for a in arrs], name)


MM_BLOCK_BYTES = 4 * 1024 * 1024


def _mm_tiles(m_dim, n_dim, k_dim, a_bytes, b_bytes):
    tm = _pick(m_dim, (1024, 512, 256, 128))
    tn = _pick(n_dim, (512, 256, 128))
    tk = k_dim
    if k_dim % LANES == 0:
        units = k_dim // LANES
        fits = [u for u in range(1, units + 1) if units % u == 0
                and u * LANES * max(tm * a_bytes, tn * b_bytes) <= MM_BLOCK_BYTES]
        tk = max(fits) * LANES
    return tm, tn, tk


def _mm(a, b, *, ta=False, tb=False, name):
    (k_dim, m_dim) = a.shape if ta else a.shape[::-1]
    (n_dim, kb_dim) = b.shape if tb else b.shape[::-1]
    assert k_dim == kb_dim, (a.shape, b.shape, ta, tb)
    tm, tn, tk = _mm_tiles(m_dim, n_dim, k_dim, a.dtype.itemsize, b.dtype.itemsize)
    nk = k_dim // tk
    dims = (((0 if ta else 1,), (1 if tb else 0,)), ((), ()))

    def body(a_ref, b_ref, o_ref):
        part = lax.dot_general(a_ref[...].astype(bf16), b_ref[...].astype(bf16), dims, preferred_element_type=f32)
        if nk == 1:
            o_ref[...] = part
        else:
            k = pl.program_id(2)

            @pl.when(k == 0)
            def _():
                o_ref[...] = part

            @pl.when(k > 0)
            def _():
                o_ref[...] += part

    a_spec = pl.BlockSpec((tk, tm), lambda i, j, k: (k, i)) if ta else pl.BlockSpec((tm, tk), lambda i, j, k: (i, k))
    b_spec = pl.BlockSpec((tn, tk), lambda i, j, k: (j, k)) if tb else pl.BlockSpec((tk, tn), lambda i, j, k: (k, j))
    return pl.pallas_call(
        body, name=name, grid=(m_dim // tm, n_dim // tn, nk),
        in_specs=[a_spec, b_spec], out_specs=pl.BlockSpec((tm, tn), lambda i, j, k: (i, j)),
        out_shape=jax.ShapeDtypeStruct((m_dim, n_dim), f32),
        compiler_params=_cparams(("parallel", "parallel", "arbitrary")),
    )(a, b)


def _win(t):
    return t if isinstance(t, tuple) else (t, t.shape[1], 0)


def _tile_spec(width, cb, tm):
    return pl.BlockSpec((tm, width), lambda i: (i, cb))


def _param_spec(p):
    return pl.BlockSpec(p.shape, lambda i: (0, 0))


def _stage_fwd(f, params, tiles, out_dtypes, name):
    tiles = [_win(t) for t in tiles]
    rows = tiles[0][0].shape[0]
    tm = min(TM, rows)
    avals = jax.eval_shape(f, *[jax.ShapeDtypeStruct(p.shape, f32) for p in params],
                           *[jax.ShapeDtypeStruct((tm, w), f32) for _, w, _ in tiles])
    n_p, n_t = len(params), len(tiles)

    def body(*refs):
        p = [r[...] for r in refs[:n_p]]
        t = [r[...].astype(f32) for r in refs[n_p:n_p + n_t]]
        for o_ref, val in zip(refs[n_p + n_t:], f(*p, *t)):
            o_ref[...] = val.astype(o_ref.dtype)

    return pl.pallas_call(
        body, name=name, grid=(rows // tm,),
        in_specs=[_param_spec(p) for p in params] + [_tile_spec(w, cb, tm) for _, w, cb in tiles],
        out_specs=[_tile_spec(a.shape[1], 0, tm) for a in avals],
        out_shape=[jax.ShapeDtypeStruct((rows, a.shape[1]), dt) for a, dt in zip(avals, out_dtypes)],
        compiler_params=_cparams(("parallel",)),
    )(*params, *[t[0] for t in tiles])


def _stage_bwd(f, params, tiles, cts, grad_dtypes, name, residual=None):
    tiles = [_win(t) for t in tiles]
    rows = tiles[0][0].shape[0]
    tm = min(TM, rows)
    cts = [list(g) if isinstance(g, (list, tuple)) else [g] for g in cts]
    flat_cts = [a for g in cts for a in g]
    n_p, n_t, n_c = len(params), len(tiles), len(flat_cts)
    has_res = residual is not None
    want = [j for j, dt in enumerate(grad_dtypes) if dt is not None]

    def body(*refs):
        i = pl.program_id(0)
        p = [r[...] for r in refs[:n_p]]
        t = [r[...].astype(f32) for r in refs[n_p:n_p + n_t]]
        ct_vals = [r[...].astype(f32) for r in refs[n_p + n_t:n_p + n_t + n_c]]
        ct, at = [], 0
        for g in cts:
            ct.append(functools.reduce(jnp.add, ct_vals[at:at + len(g)]))
            at += len(g)
        ct = tuple(ct)
        pos = n_p + n_t + n_c
        res_ref = refs[pos] if has_res else None
        pos += 1 if has_res else 0
        dp_refs = refs[pos:pos + n_p]
        dt_refs = refs[pos + n_p:]
        _, vjp = jax.vjp(f, *p, *t)
        grads = vjp(ct)

        @pl.when(i == 0)
        def _():
            for r in dp_refs:
                r[...] = jnp.zeros_like(r)

        for r, g in zip(dp_refs, grads[:n_p]):
            r[...] += g
        for r, j in zip(dt_refs, want):
            g = grads[n_p + j]
            if has_res and j == residual[0]:
                g = g + res_ref[...].astype(f32)
            r[...] = g.astype(r.dtype)

    in_arrays = list(params) + [t[0] for t in tiles] + flat_cts
    in_specs = ([_param_spec(p) for p in params] + [_tile_spec(w, cb, tm) for _, w, cb in tiles]
                + [_tile_spec(c.shape[1], 0, tm) for c in flat_cts])
    if has_res:
        in_arrays.append(residual[1])
        in_specs.append(_tile_spec(residual[1].shape[1], 0, tm))
    out_shape = ([jax.ShapeDtypeStruct(p.shape, f32) for p in params]
                 + [jax.ShapeDtypeStruct((rows, tiles[j][1]), grad_dtypes[j]) for j in want])
    out_specs = [_param_spec(p) for p in params] + [_tile_spec(tiles[j][1], 0, tm) for j in want]
    outs = pl.pallas_call(
        body, name=name, grid=(rows // tm,), in_specs=in_specs, out_specs=out_specs, out_shape=out_shape,
        compiler_params=_cparams(("arbitrary",)),
    )(*in_arrays)
    return outs[:n_p], outs[n_p:]


def _rms(x, w):
    return x * lax.rsqrt(jnp.mean(x * x, axis=-1, keepdims=True) + NORM_EPS) * w


def _f_normmod(w, shift, scale, x):
    return (_rms(x, w) * (1.0 + scale) + shift,)


def _f_merge(ga, gb, pa, pb):
    return (jax.nn.sigmoid(ga) * pa + jax.nn.sigmoid(gb) * pb,)


def _f_residual(gate, x, branch):
    return (x + gate * branch,)


def _f_loss(gate, wf, x1, fo, target):
    y = _rms(x1 + gate * fo, wf)
    err = jnp.square(y - target)
    return (0.5 * jnp.sum(jnp.mean(err, axis=-1, keepdims=True), axis=0, keepdims=True),)


def _loss_and_grads(gate2, wf, x1, fo, target):
    rows, d = x1.shape
    tm = min(TM, rows)

    def body(g_ref, w_ref, x_ref, fo_ref, t_ref, loss_ref, dg_ref, dw_ref, dx_ref, dfo_ref):
        i = pl.program_id(0)
        (val,), vjp = jax.vjp(_f_loss, g_ref[...], w_ref[...], x_ref[...], fo_ref[...], t_ref[...])
        dg, dw, dx, dfo, _ = vjp((jnp.ones((1, 1), f32),))

        @pl.when(i == 0)
        def _():
            loss_ref[...] = jnp.zeros_like(loss_ref)
            dg_ref[...] = jnp.zeros_like(dg_ref)
            dw_ref[...] = jnp.zeros_like(dw_ref)

        loss_ref[...] += jnp.broadcast_to(val, loss_ref.shape)
        dg_ref[...] += dg
        dw_ref[...] += dw
        dx_ref[...] = dx
        dfo_ref[...] = dfo.astype(bf16)

    vec = pl.BlockSpec((1, d), lambda i: (0, 0))
    tile = pl.BlockSpec((tm, d), lambda i: (i, 0))
    return pl.pallas_call(
        body, name="loss_fwd_bwd", grid=(rows // tm,),
        in_specs=[vec, vec, tile, tile, tile],
        out_specs=[pl.BlockSpec((1, LANES), lambda i: (0, 0)), vec, vec, tile, tile],
        out_shape=[jax.ShapeDtypeStruct((1, LANES), f32), jax.ShapeDtypeStruct((1, d), f32),
                   jax.ShapeDtypeStruct((1, d), f32), jax.ShapeDtypeStruct((rows, d), f32),
                   jax.ShapeDtypeStruct((rows, d), bf16)],
        compiler_params=_cparams(("arbitrary",)),
    )(gate2, wf, x1, fo, target)


def _softplus(z):
    return jnp.maximum(z, 0.0) + jnp.log(1.0 + jnp.exp(-jnp.abs(z)))


def _split_dot(a, m):
    hi = a.astype(bf16)
    lo = (a - hi.astype(f32)).astype(bf16)
    return jnp.dot(hi, m, preferred_element_type=f32) + jnp.dot(lo, m, preferred_element_type=f32)


def _suffix_matrix(n):
    r = lax.broadcasted_iota(jnp.int32, (n, n), 0)
    c = lax.broadcasted_iota(jnp.int32, (n, n), 1)
    return (r > c).astype(bf16)


def _head_masks():
    lane = lax.broadcasted_iota(jnp.int32, (1, LANES), 1)
    return [(lane < SB_HEAD_DIM).astype(f32), (lane >= SB_HEAD_DIM).astype(f32)]


def _sb_prepare(proj):
    def f(k, v):
        lane = lax.broadcasted_iota(jnp.int32, (1, SB_WIDTH), 1)
        m0 = (jnp.bitwise_and(lane, LANES - 1) < SB_HEAD_DIM).astype(f32)
        m1 = 1.0 - m0
        return k, k * m0, k * m1, v, v * m0, v * m1

    wins = [(proj, SB_WIDTH, OFF_SBK // SB_WIDTH), (proj, SB_WIDTH, OFF_SBV // SB_WIDTH)]
    return _stage_fwd(f, [], wins, [bf16] * 6, "sb_prepare")


def _stack_heads(x):
    m0, m1 = _head_masks()
    return jnp.concatenate([x * m0, x * m1], axis=0)


def _sb_logits(qst, k, t_pos2, kb, bq, masked):
    z = lax.dot_general(qst, k, (((1,), (1,)), ((), ())), preferred_element_type=f32)
    l = -_softplus(z)
    if masked:
        s_pos = kb * bq + lax.broadcasted_iota(jnp.int32, (1, bq), 1)
        causal = s_pos < t_pos2
        l = jnp.where(causal, l, 0.0)
    else:
        causal = None
    return z, l, causal


class _SideComm:
    def __init__(self, protocol, arrs, out_shapes):
        self.protocol, self.arrs, self.out_shapes = protocol, list(arrs), list(out_shapes)
        self.n = len(self.arrs)

    def specs(self):
        return [pl.BlockSpec(memory_space=pl.ANY)] * self.n

    def run(self, in_refs, out_refs, sems, first, last, compute):
        start, finish = self.protocol(in_refs, out_refs, *sems)
        pl.when(first)(start)
        compute()
        pl.when(last)(finish)


def _grid_ends(grid):
    i0, i1 = pl.program_id(0), pl.program_id(1)
    return (jnp.logical_and(i0 == 0, i1 == 0), jnp.logical_and(i0 == grid[0] - 1, i1 == grid[1] - 1))


def _sb_attention_fwd2(proj, k16, v0_16, v1_16, side=None):
    rows = proj.shape[0]
    bq = SB_QBLOCK
    nq = rows // bq
    assert nq <= LANES, "one lane per key block"
    npair = SB_WIDTH // LANES
    scale = SB_HEAD_DIM ** -0.5

    npp = SB_PAIRS_PER_STEP
    wq = npp * LANES
    grid = (npair // npp, nq)
    ns = side.n if side else 0

    def body(q_ref, k_ref, v0_ref, v1_ref, *rest):
        o_ref, runs_ref = rest[ns], rest[ns + 1]
        if side:
            side.run(rest[:ns], rest[ns + 2:2 * ns + 2], rest[2 * ns + 2:], *_grid_ends(grid),
                     lambda: compute(q_ref, k_ref, v0_ref, v1_ref, o_ref, runs_ref))
        else:
            compute(q_ref, k_ref, v0_ref, v1_ref, o_ref, runs_ref)

    def compute(q_ref, k_ref, v0_ref, v1_ref, o_ref, runs_ref):
        qi = pl.program_id(1)
        pairs = [slice(pp * LANES, (pp + 1) * LANES) for pp in range(npp)]
        qst = [(_stack_heads(q_ref[:, s]) * scale).astype(bf16) for s in pairs]
        r = lax.broadcasted_iota(jnp.int32, (bq, 2 * bq), 0)
        c = lax.broadcasted_iota(jnp.int32, (bq, 2 * bq), 1)
        m2 = jnp.logical_or(r > c, c >= bq).astype(bf16)
        t_pos = qi * bq + lax.broadcasted_iota(jnp.int32, (bq, 1), 0)
        t_pos2 = jnp.concatenate([t_pos, t_pos], axis=0)
        lane = lax.broadcasted_iota(jnp.int32, (1, LANES), 1)
        runs_ref[...] = jnp.full(runs_ref.shape, SB_NEVER, f32)

        def tiles(kbs, carry, masked):
            jobs = [(pp, kb) for kb in kbs for pp in range(npp)]
            rows_k = [pl.ds(pl.multiple_of(kb * bq, bq), bq) for _, kb in jobs]
            zl = [_sb_logits(qst[pp], k_ref[rk, pairs[pp]], t_pos2, kb, bq, masked) for (pp, kb), rk in zip(jobs, rows_k)]
            cs = [_split_dot(l, m2) for _, l, _ in zl]
            run = [cr[0] for cr in carry]
            acc = [cr[1] for cr in carry]
            probs = []
            for (pp, kb), (z, l, causal), cs2 in zip(jobs, zl, cs):
                a = jnp.exp(z + l + cs2[:, :bq] + run[pp])
                if masked:
                    a = jnp.where(causal, a, 0.0)
                probs.append(a.astype(bf16))
                for hh in range(2):
                    cols = slice((2 * pp + hh) * LANES, (2 * pp + hh + 1) * LANES)
                    runs_ref[:, cols] = jnp.where(lane == kb, run[pp][hh * bq:(hh + 1) * bq], runs_ref[:, cols])
                run[pp] = run[pp] + cs2[:, bq:]
            for (pp, kb), rk, ab in zip(jobs, rows_k, probs):
                acc[pp] = (acc[pp] + jnp.dot(ab[:bq], v0_ref[rk, pairs[pp]], preferred_element_type=f32)
                           + jnp.dot(ab[bq:], v1_ref[rk, pairs[pp]], preferred_element_type=f32))
            return tuple(zip(run, acc))

        zero = (jnp.zeros((2 * bq, bq), f32), jnp.zeros((bq, LANES), f32))
        carry = tiles([qi], (zero,) * npp, True)

        def alive(cr):
            return functools.reduce(jnp.maximum, [jnp.max(run) for run, _ in cr]) > SB_DEAD

        def two(state):
            i, _, cr = state
            cr = tiles([qi - 1 - 2 * i, qi - 2 - 2 * i], cr, False)
            return i + 1, alive(cr), cr

        n_two = qi // 2
        i_end, still, carry = lax.while_loop(lambda st: jnp.logical_and(st[0] < n_two, st[1]), two,
                                             (jnp.int32(0), alive(carry), carry))
        last_one = jnp.logical_and(qi % 2 == 1, jnp.logical_and(still, i_end == n_two))
        carry = lax.cond(last_one, lambda cr: tiles([0], cr, False), lambda cr: cr, carry)
        for pp in range(npp):
            o_ref[:, pairs[pp]] = carry[pp][1]

    kv = pl.BlockSpec((rows, wq), lambda p, i: (0, p))
    return pl.pallas_call(
        body, name="sb_attn_fwd", grid=grid,
        in_specs=[pl.BlockSpec((bq, wq), lambda p, i: (i, OFF_SBQ // wq + p)), kv, kv, kv] + (side.specs() if side else []),
        out_specs=[pl.BlockSpec((bq, wq), lambda p, i: (i, p)),
                   pl.BlockSpec((bq, 2 * wq), lambda p, i: (i, p))] + (side.specs() if side else []),
        out_shape=[jax.ShapeDtypeStruct((rows, SB_WIDTH), f32),
                   jax.ShapeDtypeStruct((rows, SB_HEADS * LANES), f32)] + (side.out_shapes if side else []),
        scratch_shapes=_comm_scratch(ns) if side else [],
        compiler_params=_cparams(("arbitrary", "arbitrary")),
    )(proj, k16, v0_16, v1_16, *(side.arrs if side else []))


def _sb_attention_bwd2(proj, k16, k0_16, k1_16, v16, runs, do, side=None):
    rows = proj.shape[0]
    bq = SB_QBLOCK
    nq = rows // bq
    npair = SB_WIDTH // LANES
    scale = SB_HEAD_DIM ** -0.5
    tn = (((0,), (0,)), ((), ()))
    nt = (((1,), (1,)), ((), ()))

    npp = SB_PAIRS_PER_STEP
    wq = npp * LANES
    grid = (npair // npp, nq)
    ns = side.n if side else 0

    def body(q_ref, k_ref, k0_ref, k1_ref, v_ref, runs_ref, do_ref, *rest):
        outs = rest[ns:ns + 3]
        ins = (q_ref, k_ref, k0_ref, k1_ref, v_ref, runs_ref, do_ref)
        if side:
            side.run(rest[:ns], rest[ns + 3:2 * ns + 3], rest[2 * ns + 3:], *_grid_ends(grid),
                     lambda: compute(*ins, *outs))
        else:
            compute(*ins, *outs)

    def compute(q_ref, k_ref, k0_ref, k1_ref, v_ref, runs_ref, do_ref, dq_ref, dk_ref, dv_ref):
        qi = pl.program_id(1)

        @pl.when(qi == 0)
        def _():
            dk_ref[...] = jnp.zeros_like(dk_ref)
            dv_ref[...] = jnp.zeros_like(dv_ref)

        pairs = [slice(pp * LANES, (pp + 1) * LANES) for pp in range(npp)]
        qst = [(_stack_heads(q_ref[:, s]) * scale).astype(bf16) for s in pairs]
        dost = [_stack_heads(do_ref[:, s]).astype(bf16) for s in pairs]
        runs = [jnp.concatenate([runs_ref[:, 2 * pp * LANES:(2 * pp + 1) * LANES],
                                 runs_ref[:, (2 * pp + 1) * LANES:(2 * pp + 2) * LANES]], axis=0) for pp in range(npp)]
        r = lax.broadcasted_iota(jnp.int32, (bq, 2 * bq), 0)
        c = lax.broadcasted_iota(jnp.int32, (bq, 2 * bq), 1)
        suffix_m = _suffix_matrix(bq)
        m2 = jnp.logical_or(r < c, c >= bq).astype(bf16)
        t_pos = qi * bq + lax.broadcasted_iota(jnp.int32, (bq, 1), 0)
        t_pos2 = jnp.concatenate([t_pos, t_pos], axis=0)
        lane = lax.broadcasted_iota(jnp.int32, (1, LANES), 1)

        def tiles(kbs, carry, masked):
            jobs = [(pp, kb) for kb in kbs for pp in range(npp)]
            rows_k = [pl.ds(pl.multiple_of(kb * bq, bq), bq) for _, kb in jobs]
            zl = [_sb_logits(qst[pp], k_ref[rk, pairs[pp]], t_pos2, kb, bq, masked) for (pp, kb), rk in zip(jobs, rows_k)]
            das = [lax.dot_general(dost[pp], v_ref[rk, pairs[pp]], nt, preferred_element_type=f32)
                   for (pp, kb), rk in zip(jobs, rows_k)]
            sticks = [_split_dot(l, suffix_m) for _, l, _ in zl]
            probs, ps = [], []
            for (pp, kb), (z, l, causal), stick, da in zip(jobs, zl, sticks, das):
                run = jnp.sum(jnp.where(lane == kb, runs[pp], 0.0), axis=1, keepdims=True)
                a = jnp.exp(z + l + stick + run)
                if masked:
                    a = jnp.where(causal, a, 0.0)
                probs.append(a.astype(bf16))
                ps.append(da * a)
            pcs = [_split_dot(p, m2) for p in ps]
            pref = [cr[0] for cr in carry]
            dq_acc = [cr[1] for cr in carry]
            dzs = []
            for (pp, kb), (z, l, causal), p, pc2 in zip(jobs, zl, ps, pcs):
                dz = p * jnp.exp(l) - jnp.exp(z + l) * (pc2[:, :bq] + pref[pp])
                if masked:
                    dz = jnp.where(causal, dz, 0.0)
                dzs.append(dz.astype(bf16))
                pref[pp] = pref[pp] + pc2[:, bq:]
            for (pp, kb), rk, dzb, ab in zip(jobs, rows_k, dzs, probs):
                cols = pairs[pp]
                dq_acc[pp] = (dq_acc[pp] + jnp.dot(dzb[:bq], k0_ref[rk, cols], preferred_element_type=f32)
                              + jnp.dot(dzb[bq:], k1_ref[rk, cols], preferred_element_type=f32))
                dk_ref[rk, cols] += lax.dot_general(dzb, qst[pp], tn, preferred_element_type=f32)
                dv_ref[rk, cols] += lax.dot_general(ab, dost[pp], tn, preferred_element_type=f32)
            return tuple(zip(pref, dq_acc))

        zero = (jnp.zeros((2 * bq, bq), f32), jnp.zeros((bq, LANES), f32))
        colmax = functools.reduce(jnp.maximum, [jnp.max(x, axis=0, keepdims=True) for x in runs])
        live = jnp.logical_and(colmax > SB_DEAD, lane < qi)
        kb0 = jnp.minimum(jnp.min(jnp.where(live, lane, LANES)), qi)
        n_blocks = qi - kb0
        carry = lax.fori_loop(0, n_blocks // 2, lambda i, cr: tiles([kb0 + 2 * i, kb0 + 2 * i + 1], cr, False),
                              (zero,) * npp)
        carry = lax.cond(n_blocks % 2 == 1, lambda cr: tiles([qi - 1], cr, False), lambda cr: cr, carry)
        carry = tiles([qi], carry, True)
        for pp in range(npp):
            dq_ref[:, pairs[pp]] = (carry[pp][1] * scale).astype(dq_ref.dtype)

    blk = pl.BlockSpec((bq, wq), lambda p, i: (i, p))
    full = pl.BlockSpec((rows, wq), lambda p, i: (0, p))
    return pl.pallas_call(
        body, name="sb_attn_bwd", grid=grid,
        in_specs=[pl.BlockSpec((bq, wq), lambda p, i: (i, OFF_SBQ // wq + p)), full, full, full, full,
                  pl.BlockSpec((bq, 2 * wq), lambda p, i: (i, p)), blk] + (side.specs() if side else []),
        out_specs=[blk, full, full] + (side.specs() if side else []),
        out_shape=[jax.ShapeDtypeStruct((rows, SB_WIDTH), bf16), jax.ShapeDtypeStruct((rows, SB_WIDTH), f32),
                   jax.ShapeDtypeStruct((rows, SB_WIDTH), f32)] + (side.out_shapes if side else []),
        scratch_shapes=_comm_scratch(ns) if side else [],
        compiler_params=_cparams(("arbitrary", "arbitrary")),
    )(proj, k16, k0_16, k1_16, v16, runs, do, *(side.arrs if side else []))


def _shift_down(x, prev8, j):
    if j == 0:
        return x
    r = pltpu.roll(x, j, axis=0)
    row8 = lax.broadcasted_iota(jnp.int32, prev8.shape, 0)
    head = jnp.where(row8 < j, pltpu.roll(prev8, j, axis=0), r[0:SUBLANES])
    return jnp.concatenate([head, r[SUBLANES:]], axis=0)


def _shift_up(x, next8, j):
    if j == 0:
        return x
    n = x.shape[0]
    r = pltpu.roll(x, n - j, axis=0)
    row8 = lax.broadcasted_iota(jnp.int32, next8.shape, 0)
    tail = jnp.where(row8 >= SUBLANES - j, pltpu.roll(next8, SUBLANES - j, axis=0), r[n - SUBLANES:n])
    return jnp.concatenate([r[:n - SUBLANES], tail], axis=0)


def _conv(x, prev8, w):
    k_taps = w.shape[0]
    out = x * w[k_taps - 1:k_taps, :]
    for j in range(1, k_taps):
        out = out + _shift_down(x, prev8, j) * w[k_taps - 1 - j:k_taps - j, :]
    return out


def _conv_tiles(rows, tr_max=TCONV_R):
    tr = min(tr_max, rows)
    return tr, rows // tr, tr // SUBLANES


def _prev_spec(tc, cb0, r8):
    return pl.BlockSpec((SUBLANES, tc), lambda j, i: (jnp.maximum(i * r8 - 1, 0), cb0 + j))


def _silu(x):
    return x * jax.nn.sigmoid(x)


def _dsilu(x):
    s = jax.nn.sigmoid(x)
    return s * (1.0 + x * (1.0 - s))


def _dn_conv_fwd(proj, w):
    rows = proj.shape[0]
    tr, nr, r8 = _conv_tiles(rows)
    tc = TCONV_C
    cb0 = OFF_DN // tc

    def body(x_ref, p_ref, w_ref, o_ref):
        prev = jnp.where(pl.program_id(1) == 0, 0.0, p_ref[...])
        o_ref[...] = _silu(_conv(x_ref[...], prev, w_ref[...]))

    return pl.pallas_call(
        body, name="dn_conv_fwd", grid=(DN_CONV_CH // tc, nr),
        in_specs=[pl.BlockSpec((tr, tc), lambda j, i: (i, cb0 + j)), _prev_spec(tc, cb0, r8),
                  pl.BlockSpec((DN_CONV_WIDTH, tc), lambda j, i: (0, j))],
        out_specs=pl.BlockSpec((tr, tc), lambda j, i: (i, j)),
        out_shape=jax.ShapeDtypeStruct((rows, DN_CONV_CH), f32),
        compiler_params=_cparams(("parallel", "parallel")),
    )(proj, proj, w)


def _dn_conv_bwd_act(proj, w, dact):
    rows = proj.shape[0]
    tr, nr, r8 = _conv_tiles(rows)
    tc = TCONV_C
    cb0 = OFF_DN // tc

    def body(x_ref, p_ref, w_ref, d_ref, o_ref):
        prev = jnp.where(pl.program_id(1) == 0, 0.0, p_ref[...])
        o_ref[...] = d_ref[...] * _dsilu(_conv(x_ref[...], prev, w_ref[...]))

    return pl.pallas_call(
        body, name="dn_conv_bwd_act", grid=(DN_CONV_CH // tc, nr),
        in_specs=[pl.BlockSpec((tr, tc), lambda j, i: (i, cb0 + j)), _prev_spec(tc, cb0, r8),
                  pl.BlockSpec((DN_CONV_WIDTH, tc), lambda j, i: (0, j)),
                  pl.BlockSpec((tr, tc), lambda j, i: (i, j))],
        out_specs=pl.BlockSpec((tr, tc), lambda j, i: (i, j)),
        out_shape=jax.ShapeDtypeStruct((rows, DN_CONV_CH), f32),
        compiler_params=_cparams(("parallel", "parallel")),
    )(proj, proj, w, dact)


def _ffn_conv_fwd(u_pre, w, b):
    rows = u_pre.shape[0]
    tr, nr, r8 = _conv_tiles(rows, TCONV_R // 2)
    tc = TCONV_FF
    nct = D_FF // tc

    def body(xg_ref, pg_ref, xu_ref, pu_ref, wg_ref, wu_ref, bg_ref, bu_ref, o_ref):
        first = pl.program_id(1) == 0
        ug = _conv(xg_ref[...], jnp.where(first, 0.0, pg_ref[...]), wg_ref[...]) + bg_ref[...]
        uu = _conv(xu_ref[...], jnp.where(first, 0.0, pu_ref[...]), wu_ref[...]) + bu_ref[...]
        o_ref[...] = (_silu(ug) * uu).astype(o_ref.dtype)

    def x_spec(off):
        return pl.BlockSpec((tr, tc), lambda j, i: (i, off + j))

    def w_spec(k, off):
        return pl.BlockSpec((k, tc), lambda j, i: (0, off + j))

    return pl.pallas_call(
        body, name="ffn_conv_fwd", grid=(nct, nr),
        in_specs=[x_spec(0), _prev_spec(tc, 0, r8), x_spec(nct), _prev_spec(tc, nct, r8),
                  w_spec(FFN_CONV_WIDTH, 0), w_spec(FFN_CONV_WIDTH, nct), w_spec(1, 0), w_spec(1, nct)],
        out_specs=pl.BlockSpec((tr, tc), lambda j, i: (i, j)),
        out_shape=jax.ShapeDtypeStruct((rows, D_FF), bf16),
        compiler_params=_cparams(("parallel", "parallel")),
    )(u_pre, u_pre, u_pre, u_pre, w, w, b, b)


def _ffn_conv_bwd_act(u_pre, w, b, dact):
    rows = u_pre.shape[0]
    tr, nr, r8 = _conv_tiles(rows, TCONV_R // 2)
    tc = TCONV_FF
    nct = D_FF // tc

    def body(xg_ref, pg_ref, xu_ref, pu_ref, wg_ref, wu_ref, bg_ref, bu_ref, d_ref,
             du_ref, dbg_ref, dbu_ref):
        i = pl.program_id(1)
        first = i == 0
        ug = _conv(xg_ref[...], jnp.where(first, 0.0, pg_ref[...]), wg_ref[...]) + bg_ref[...]
        uu = _conv(xu_ref[...], jnp.where(first, 0.0, pu_ref[...]), wu_ref[...]) + bu_ref[...]
        d = d_ref[...]
        dug = d * uu * _dsilu(ug)
        duu = d * _silu(ug)
        du_ref[0] = dug
        du_ref[1] = duu

        @pl.when(first)
        def _():
            dbg_ref[...] = jnp.zeros_like(dbg_ref)
            dbu_ref[...] = jnp.zeros_like(dbu_ref)

        dbg_ref[...] += jnp.sum(dug, axis=0, keepdims=True)
        dbu_ref[...] += jnp.sum(duu, axis=0, keepdims=True)

    def x_spec(off):
        return pl.BlockSpec((tr, tc), lambda j, i: (i, off + j))

    def w_spec(k, off):
        return pl.BlockSpec((k, tc), lambda j, i: (0, off + j))

    tile = pl.BlockSpec((tr, tc), lambda j, i: (i, j))
    vec = pl.BlockSpec((1, tc), lambda j, i: (0, j))
    return pl.pallas_call(
        body, name="ffn_conv_bwd_act", grid=(nct, nr),
        in_specs=[x_spec(0), _prev_spec(tc, 0, r8), x_spec(nct), _prev_spec(tc, nct, r8),
                  w_spec(FFN_CONV_WIDTH, 0), w_spec(FFN_CONV_WIDTH, nct), w_spec(1, 0), w_spec(1, nct), tile],
        out_specs=[pl.BlockSpec((2, tr, tc), lambda j, i: (0, i, j)), vec, vec],
        out_shape=[jax.ShapeDtypeStruct((2, rows, D_FF), f32),
                   jax.ShapeDtypeStruct((1, D_FF), f32), jax.ShapeDtypeStruct((1, D_FF), f32)],
        compiler_params=_cparams(("parallel", "arbitrary")),
    )(u_pre, u_pre, u_pre, u_pre, w, w, b, b, dact)


def _conv_bwd(dy, x, x_cb0, w, name):
    k_taps = w.shape[0]
    split = dy.ndim == 3
    rows = dy.shape[-2]
    ch = dy.shape[-1] * (2 if split else 1)
    tc = TCONV_FF if split else TCONV_C
    tr, nr, r8 = _conv_tiles(rows, TCONV_R // 2 if split else TCONV_R)
    per_half = dy.shape[-1] // tc
    last8 = rows // SUBLANES - 1

    def body(dy_ref, nx_ref, x_ref, p_ref, w_ref, dx_ref, dw_ref):
        i = pl.program_id(1)
        dyv = dy_ref[...]
        nxt = jnp.where(i == nr - 1, 0.0, nx_ref[...])
        prev = jnp.where(i == 0, 0.0, p_ref[...])
        xv = x_ref[...].astype(f32)
        wv = w_ref[...]

        @pl.when(i == 0)
        def _():
            dw_ref[...] = jnp.zeros_like(dw_ref)

        dx = dyv * wv[k_taps - 1:k_taps, :]
        dw_ref[k_taps - 1:k_taps, :] += jnp.sum(dyv * xv, axis=0, keepdims=True)
        for j in range(1, k_taps):
            dx = dx + _shift_up(dyv, nxt, j) * wv[k_taps - 1 - j:k_taps - j, :]
            dw_ref[k_taps - 1 - j:k_taps - j, :] += jnp.sum(dyv * _shift_down(xv, prev, j), axis=0, keepdims=True)
        dx_ref[...] = dx.astype(dx_ref.dtype)

    tile = pl.BlockSpec((tr, tc), lambda j, i: (i, j))
    if split:
        dy_spec = pl.BlockSpec((None, tr, tc), lambda j, i: (j // per_half, i, j % per_half))
        next_spec = pl.BlockSpec((None, SUBLANES, tc),
                                 lambda j, i: (j // per_half, jnp.minimum((i + 1) * r8, last8), j % per_half))
    else:
        dy_spec = tile
        next_spec = pl.BlockSpec((SUBLANES, tc), lambda j, i: (jnp.minimum((i + 1) * r8, last8), j))
    return pl.pallas_call(
        body, name=name, grid=(ch // tc, nr),
        in_specs=[dy_spec, next_spec,
                  pl.BlockSpec((tr, tc), lambda j, i: (i, x_cb0 + j)), _prev_spec(tc, x_cb0, r8),
                  pl.BlockSpec((k_taps, tc), lambda j, i: (0, j))],
        out_specs=[tile, pl.BlockSpec((k_taps, tc), lambda j, i: (0, j))],
        out_shape=[jax.ShapeDtypeStruct((rows, ch), bf16), jax.ShapeDtypeStruct((k_taps, ch), f32)],
        compiler_params=_cparams(("parallel", "arbitrary")),
    )(dy, dy, x, x, w)


def _hdot(a, b):
    return jnp.dot(a, b, preferred_element_type=f32, precision=lax.Precision.HIGH)


def _xdot(a, b):
    return jnp.dot(a, b, preferred_element_type=f32, precision=lax.Precision.HIGHEST)


def _bdot(a, b):
    return jnp.dot(a.astype(bf16), b.astype(bf16), preferred_element_type=f32)


def _bdot_nt(a, b):
    return lax.dot_general(a.astype(bf16), b.astype(bf16), (((1,), (1,)), ((), ())), preferred_element_type=f32)


def _bdot_tn(a, b):
    return lax.dot_general(a.astype(bf16), b.astype(bf16), (((0,), (0,)), ((), ())), preferred_element_type=f32)


GDN_GROUP = 4
GDN_NGROUPS = DN_HEADS // GDN_GROUP
GDN_ROWS = GDN_GROUP * DN_CHUNK
GDN_QK_LANES = GDN_GROUP * DN_KEY_DIM
GDN_LOGIT_LANE = DN_HEADS


def _inverse_impl(lows):
    n = lows[0].shape[0]
    r = lax.broadcasted_iota(jnp.int32, (n, n), 0)
    c = lax.broadcasted_iota(jnp.int32, (n, n), 1)
    eye = (r == c).astype(f32)
    blk = jnp.right_shift(r, 4) == jnp.right_shift(c, 4)
    d = [jnp.where(blk, low, 0.0) for low in lows]
    e = [low - x for low, x in zip(lows, d)]
    d2 = [_bdot(x, x) for x in d]
    d4 = [_bdot(x, x) for x in d2]
    d8 = [_bdot(x, x) for x in d4]
    p = [_bdot(eye - x, eye + y) for x, y in zip(d, d2)]
    p = [_bdot(x, eye + y) for x, y in zip(p, d4)]
    dinv = [_bdot(x, eye + y) for x, y in zip(p, d8)]
    nn = [_bdot(x, y) for x, y in zip(dinv, e)]
    n2 = [_bdot(x, x) for x in nn]
    ninv = [_bdot(eye - x, eye + y) for x, y in zip(nn, n2)]
    t = [_bdot(x, y) for x, y in zip(ninv, dinv)]
    for _ in range(2):
        res = [eye - x - _hdot(low, x) for low, x in zip(lows, t)]
        t = [x + _bdot(x, y) for x, y in zip(t, res)]
    return tuple(t)


@jax.custom_vjp
def _unit_lower_inverses(lows):
    return _inverse_impl(lows)


def _unit_lower_inverses_fwd(lows):
    t = _inverse_impl(lows)
    return t, t


def _unit_lower_inverses_bwd(t, ct):
    tn = (((0,), (0,)), ((), ()))
    nt = (((1,), (1,)), ((), ()))
    left = [lax.dot_general(x, g, tn, preferred_element_type=f32, precision=lax.Precision.HIGH) for x, g in zip(t, ct)]
    return (tuple(-lax.dot_general(x, y, nt, preferred_element_type=f32, precision=lax.Precision.HIGH)
                  for x, y in zip(left, t)),)


_unit_lower_inverses.defvjp(_unit_lower_inverses_fwd, _unit_lower_inverses_bwd)


def _gdn_chunk(a_log, dt_bias, norm_w, ba, *per_group):
    ng = GDN_NGROUPS
    qgs, kgs, vsts, zsts, states = [per_group[i * ng:(i + 1) * ng] for i in range(5)]
    groups = range(ng)
    n = GDN_ROWS
    r = lax.broadcasted_iota(jnp.int32, (n, n), 0)
    c = lax.broadcasted_iota(jnp.int32, (n, n), 1)
    same_head = jnp.right_shift(r, 6) == jnp.right_shift(c, 6)
    incl = jnp.logical_and(same_head, r >= c)
    strict = jnp.logical_and(same_head, r > c)
    eye = (r == c).astype(f32)
    ones = jnp.ones((n, n), f32)
    own_lanes = same_head.astype(f32)
    lane = lax.broadcasted_iota(jnp.int32, (1, LANES), 1)
    pick = lambda arr, idx: jnp.sum(jnp.where(lane == idx, arr, 0.0), axis=1, keepdims=True)
    heads = [[GDN_GROUP * g + h for h in range(GDN_GROUP)] for g in groups]
    rc = lax.broadcasted_iota(jnp.int32, (DN_CHUNK, DN_CHUNK), 0)
    cc = lax.broadcasted_iota(jnp.int32, (DN_CHUNK, DN_CHUNK), 1)

    g_all = -jnp.exp(a_log) * _softplus(ba + dt_bias)
    gc_all = _xdot((rc >= cc).astype(f32), g_all)
    gl_all = jnp.sum(g_all, axis=0, keepdims=True)
    beta = [jnp.concatenate([jax.nn.sigmoid(pick(ba, hd)) for hd in heads[g]], axis=0) for g in groups]
    gc = [jnp.concatenate([pick(gc_all, GDN_LOGIT_LANE + hd) for hd in heads[g]], axis=0) for g in groups]
    g_last = [jnp.concatenate([jnp.broadcast_to(pick(gl_all, GDN_LOGIT_LANE + hd), (DN_CHUNK, 1)) for hd in heads[g]],
                              axis=0) for g in groups]
    gr = [_xdot(ones, eye * gc[g]) for g in groups]
    decay = [jnp.where(incl, jnp.exp(jnp.where(incl, gc[g] - gr[g], 0.0)), 0.0) for g in groups]
    q = [jnp.concatenate([qgs[g]] * GDN_GROUP, axis=0) * own_lanes for g in groups]
    k = [jnp.concatenate([kgs[g]] * GDN_GROUP, axis=0) * own_lanes for g in groups]
    qn = [x * lax.rsqrt(jnp.sum(x * x, axis=1, keepdims=True) + L2_EPS) * (DN_KEY_DIM ** -0.5) for x in q]
    kn = [x * lax.rsqrt(jnp.sum(x * x, axis=1, keepdims=True) + L2_EPS) for x in k]
    kb = [kn[g] * beta[g] for g in groups]
    low = [jnp.where(strict, _bdot_nt(kb[g], kn[g]) * decay[g], 0.0) for g in groups]
    intra = [jnp.where(incl, _bdot_nt(qn[g], kn[g]) * decay[g], 0.0) for g in groups]
    t = _unit_lower_inverses(tuple(low))
    u = [_bdot(t[g], vsts[g] * beta[g]) for g in groups]
    w = [_bdot(t[g], kb[g] * jnp.exp(gc[g])) for g in groups]
    sb = [s.astype(bf16) for s in states]
    v_new = [u[g] - jnp.dot(w[g].astype(bf16), sb[g], preferred_element_type=f32) for g in groups]
    o = [jnp.dot((qn[g] * jnp.exp(gc[g])).astype(bf16), sb[g], preferred_element_type=f32) for g in groups]
    o = [o[g] + _bdot(intra[g], v_new[g]) for g in groups]
    new_state = [states[g] * jnp.exp(g_last[g]) + _bdot_tn(kn[g] * jnp.exp(g_last[g] - gc[g]), v_new[g])
                 for g in groups]
    o_n = [x * lax.rsqrt(jnp.mean(x * x, axis=1, keepdims=True) + NORM_EPS) * norm_w for x in o]
    return tuple(o_n[g] * _silu(zsts[g]) for g in groups) + tuple(new_state)


def _gdn_specs(rows, reverse):
    n = rows // DN_CHUNK
    idx = (lambda i: n - 1 - i) if reverse else (lambda i: i)
    vec = pl.BlockSpec((1, LANES), lambda i: (0, 0))
    qkv = pl.BlockSpec((DN_CHUNK, DN_CONV_CH), lambda i: (idx(i), 0))
    z = pl.BlockSpec((DN_CHUNK, DN_V_WIDTH), lambda i: (idx(i), OFF_Z // DN_V_WIDTH))
    ba = pl.BlockSpec((DN_CHUNK, LANES), lambda i: (idx(i), 0))
    wide = pl.BlockSpec((DN_CHUNK, DN_V_WIDTH), lambda i: (idx(i), 0))
    st = pl.BlockSpec((1, DN_HEADS * DN_KEY_DIM, LANES), lambda i: (idx(i), 0, 0))
    return n, vec, qkv, z, ba, wide, st


def _gdn_slices(grp):
    q = slice(grp * GDN_QK_LANES, (grp + 1) * GDN_QK_LANES)
    k = slice(DN_QK_WIDTH + grp * GDN_QK_LANES, DN_QK_WIDTH + (grp + 1) * GDN_QK_LANES)
    heads = [slice((GDN_GROUP * grp + h) * LANES, (GDN_GROUP * grp + h + 1) * LANES) for h in range(GDN_GROUP)]
    vs = [slice(2 * DN_QK_WIDTH + s.start, 2 * DN_QK_WIDTH + s.stop) for s in heads]
    return q, k, vs, heads


def _stack_cols(ref, cols):
    return jnp.concatenate([ref[:, s] for s in cols], axis=0)


def _gdn_operands(qkv_ref, z_ref, state_rows):
    sl = [_gdn_slices(grp) for grp in range(GDN_NGROUPS)]
    return ([qkv_ref[:, q] for q, _, _, _ in sl] + [qkv_ref[:, k] for _, k, _, _ in sl]
            + [_stack_cols(qkv_ref, vs) for _, _, vs, _ in sl] + [_stack_cols(z_ref, heads) for _, _, _, heads in sl]
            + [state_rows[grp * GDN_ROWS:(grp + 1) * GDN_ROWS, :] for grp in range(GDN_NGROUPS)])


def _gdn_fwd(a_log, dt_bias, norm_w, qkv_act, proj, ba):
    rows = qkv_act.shape[0]
    n, vec, qkv_s, z_s, ba_s, wide, st_s = _gdn_specs(rows, False)

    def body(al_ref, dt_ref, nw_ref, qkv_ref, z_ref, ba_ref, o_ref, st_ref, state):
        @pl.when(pl.program_id(0) == 0)
        def _():
            state[...] = jnp.zeros_like(state)

        st_ref[0] = state[...]
        out = _gdn_chunk(al_ref[...], dt_ref[...], nw_ref[...], ba_ref[...], *_gdn_operands(qkv_ref, z_ref, state))
        for grp in range(GDN_NGROUPS):
            _, _, _, heads = _gdn_slices(grp)
            for h, s in enumerate(heads):
                o_ref[:, s] = out[grp][h * DN_CHUNK:(h + 1) * DN_CHUNK].astype(o_ref.dtype)
            state[grp * GDN_ROWS:(grp + 1) * GDN_ROWS, :] = out[GDN_NGROUPS + grp]

    return pl.pallas_call(
        body, name="gdn_fwd", grid=(n,),
        in_specs=[vec, vec, vec, qkv_s, z_s, ba_s], out_specs=[wide, st_s],
        out_shape=[jax.ShapeDtypeStruct((rows, DN_V_WIDTH), bf16),
                   jax.ShapeDtypeStruct((n, DN_HEADS * DN_KEY_DIM, LANES), f32)],
        scratch_shapes=[pltpu.VMEM((DN_HEADS * DN_KEY_DIM, LANES), f32)],
        compiler_params=_cparams(("arbitrary",)),
    )(a_log, dt_bias, norm_w, qkv_act, proj, ba)


def _gdn_bwd(a_log, dt_bias, norm_w, qkv_act, proj, ba, states, do):
    rows = qkv_act.shape[0]
    n, vec, qkv_s, z_s, ba_s, wide, st_s = _gdn_specs(rows, True)

    def body(al_ref, dt_ref, nw_ref, qkv_ref, z_ref, ba_ref, st_ref, do_ref,
             dal_ref, ddt_ref, dnw_ref, dqkv_ref, dz_ref, dba_ref, dstate):
        @pl.when(pl.program_id(0) == 0)
        def _():
            dstate[...] = jnp.zeros_like(dstate)
            dal_ref[...] = jnp.zeros_like(dal_ref)
            ddt_ref[...] = jnp.zeros_like(ddt_ref)
            dnw_ref[...] = jnp.zeros_like(dnw_ref)

        ng = GDN_NGROUPS
        _, vjp = jax.vjp(_gdn_chunk, al_ref[...], dt_ref[...], nw_ref[...], ba_ref[...],
                         *_gdn_operands(qkv_ref, z_ref, st_ref[0]))
        cts = tuple(_stack_cols(do_ref, _gdn_slices(grp)[3]) for grp in range(ng))
        cts += tuple(dstate[grp * GDN_ROWS:(grp + 1) * GDN_ROWS, :] for grp in range(ng))
        grads = vjp(cts)
        dal_ref[...] += grads[0]
        ddt_ref[...] += grads[1]
        dnw_ref[...] += grads[2]
        dba_ref[...] = grads[3]
        dqs, dks, dvs, dzs, dss = [grads[4 + i * ng:4 + (i + 1) * ng] for i in range(5)]
        for grp in range(ng):
            q, k, vs, heads = _gdn_slices(grp)
            dqkv_ref[:, q] = dqs[grp]
            dqkv_ref[:, k] = dks[grp]
            for h, (sv, sh) in enumerate(zip(vs, heads)):
                rows_h = slice(h * DN_CHUNK, (h + 1) * DN_CHUNK)
                dqkv_ref[:, sv] = dvs[grp][rows_h]
                dz_ref[:, sh] = dzs[grp][rows_h].astype(dz_ref.dtype)
            dstate[grp * GDN_ROWS:(grp + 1) * GDN_ROWS, :] = dss[grp]

    return pl.pallas_call(
        body, name="gdn_bwd", grid=(n,),
        in_specs=[vec, vec, vec, qkv_s, z_s, ba_s, st_s, wide],
        out_specs=[vec, vec, vec, qkv_s, wide, ba_s],
        out_shape=[jax.ShapeDtypeStruct((1, LANES), f32)] * 3
        + [jax.ShapeDtypeStruct((rows, DN_CONV_CH), f32), jax.ShapeDtypeStruct((rows, DN_V_WIDTH), bf16),
           jax.ShapeDtypeStruct((rows, LANES), f32)],
        scratch_shapes=[pltpu.VMEM((DN_HEADS * DN_KEY_DIM, LANES), f32)],
        compiler_params=_cparams(("arbitrary",)),
    )(a_log, dt_bias, norm_w, qkv_act, proj, ba, states, do)


def _ada_fwd(c_all, w_loc, b_loc):
    def body(c_ref, w_ref, b_ref, o_ref):
        o_ref[...] = _bdot(_silu(c_ref[...]), w_ref[...]) + b_ref[...]

    return pl.pallas_call(body, name="ada_fwd", out_shape=jax.ShapeDtypeStruct((c_all.shape[0], w_loc.shape[1]), f32),
                          compiler_params=_cparams())(c_all, w_loc, b_loc)


def _ada_bwd(c_all, dmod_cols):
    def body(c_ref, d_ref, o_ref):
        o_ref[...] = _bdot_tn(_silu(c_ref[...]), d_ref[...])

    return pl.pallas_call(body, name="ada_bwd",
                          out_shape=jax.ShapeDtypeStruct((c_all.shape[1], dmod_cols.shape[1]), f32),
                          compiler_params=_cparams())(c_all, dmod_cols)


def _sum_devices(parts):
    def body(p_ref, o_ref):
        acc = p_ref[0:1, :]
        for d in range(1, N_DEV):
            acc = acc + p_ref[d:d + 1, :]
        o_ref[...] = acc

    return pl.pallas_call(body, name="sum_small", out_shape=jax.ShapeDtypeStruct((1, parts.shape[1]), f32),
                          compiler_params=_cparams())(parts)


def _adam_math(w, g, m, v):
    m2 = ADAM_B1 * m + (1.0 - ADAM_B1) * g
    v2 = ADAM_B2 * v + (1.0 - ADAM_B2) * jnp.square(g)
    m_hat = m2 / (1.0 - ADAM_B1 ** ADAM_STEP)
    v_hat = v2 / (1.0 - ADAM_B2 ** ADAM_STEP)
    delta = -ADAM_LR * (m_hat / (jnp.sqrt(v_hat) + ADAM_EPS) + ADAM_WD * w)
    return delta, m2, v2


def _row_tile(rows):
    return _pick(rows, (256, 128, 64, 32, 16, 8))


def _adamw(w, g, m, v, name):
    rows, cols = w.shape
    tr = _row_tile(rows)

    def body(w_ref, g_ref, m_ref, v_ref, d_ref, m2_ref, v2_ref):
        d_ref[...], m2_ref[...], v2_ref[...] = _adam_math(w_ref[...], g_ref[...], m_ref[...], v_ref[...])

    tile = pl.BlockSpec((tr, cols), lambda i: (i, 0))
    return pl.pallas_call(body, name=name, grid=(rows // tr,), in_specs=[tile] * 4, out_specs=[tile] * 3,
                          out_shape=[jax.ShapeDtypeStruct(w.shape, f32)] * 3,
                          compiler_params=_cparams(("parallel",)))(w, g, m, v)


def _sum_adamw(parts, w, m, v, name):
    rows, cols = w.shape
    tr = _row_tile(rows)

    def body(p_ref, w_ref, m_ref, v_ref, g_ref, d_ref, m2_ref, v2_ref):
        g = p_ref[0].astype(f32)
        for d in range(1, N_DEV):
            g = g + p_ref[d].astype(f32)
        g_ref[...] = g
        d_ref[...], m2_ref[...], v2_ref[...] = _adam_math(w_ref[...], g, m_ref[...], v_ref[...])

    tile = pl.BlockSpec((tr, cols), lambda i: (i, 0))
    return pl.pallas_call(body, name=name, grid=(rows // tr,),
                          in_specs=[pl.BlockSpec((N_DEV, tr, cols), lambda i: (0, i, 0)), tile, tile, tile],
                          out_specs=[tile] * 4, out_shape=[jax.ShapeDtypeStruct(w.shape, f32)] * 4,
                          compiler_params=_cparams(("parallel",)))(parts, w, m, v)


def _pad_lanes(a, width):
    return jnp.pad(a, ((0, 0), (0, width - a.shape[1])))


def _cols_by_device(full):
    r, c = full.shape
    return jnp.moveaxis(full.reshape(r, N_DEV, c // N_DEV), 1, 0)


def _cols_from_devices(parts):
    d, r, n = parts.shape
    return jnp.moveaxis(parts, 0, 1).reshape(r, d * n)


def kernel(x, c, w_ada, b_ada, norm1_w, w_in, dn_conv_w, dn_A_log, dn_dt_bias, dn_norm_w, w_proj_sb, w_proj_dn, w_out, norm2_w, w_ffn_in, ffn_conv_w, ffn_conv_b, w_ffn_out, final_norm_w, loss_target, m_w_ada, m_b_ada, m_norm1_w, m_w_in, m_dn_conv_w, m_dn_A_log, m_dn_dt_bias, m_dn_norm_w, m_w_proj_sb, m_w_proj_dn, m_w_out, m_norm2_w, m_w_ffn_in, m_ffn_conv_w, m_ffn_conv_b, m_w_ffn_out, m_final_norm_w, v_w_ada, v_b_ada, v_norm1_w, v_w_in, v_dn_conv_w, v_dn_A_log, v_dn_dt_bias, v_dn_norm_w, v_w_proj_sb, v_w_proj_dn, v_w_out, v_norm2_w, v_w_ffn_in, v_ffn_conv_w, v_ffn_conv_b, v_w_ffn_out, v_final_norm_w):
    d = D_MODEL
    me = 4 * lax.axis_index("x") + 2 * lax.axis_index("y") + lax.axis_index("c")
    xs = x[0]
    target = loss_target[0]
    n_ada = w_ada.shape[2]
    n_dnc = dn_conv_w.shape[2]
    n_ffc = ffn_conv_w.shape[2]

    small = jnp.concatenate([c, dn_conv_w[0].reshape(1, -1), ffn_conv_w[0].reshape(1, -1)], axis=1)
    small = _pad_lanes(small, -(-small.shape[1] // LANES) * LANES)
    small_g, w_in_g = _all_gather([small, w_in[0].astype(bf16)], "gather_w_in")
    later = [w_proj_sb[0].astype(bf16), w_proj_dn[0].astype(bf16), w_out[0].astype(bf16),
             w_ffn_in[0].astype(bf16), w_ffn_out[0].astype(bf16)]
    gather_later = _SideComm(_gather_protocol, later, _gathered_shapes(later))
    small_g = small_g[:, 0, :]
    c_all = small_g[:, :d]
    dn_cw = _cols_from_devices(small_g[:, d:d + DN_CONV_WIDTH * n_dnc].reshape(N_DEV, DN_CONV_WIDTH, n_dnc))
    o2 = d + DN_CONV_WIDTH * n_dnc
    ffn_cw = _cols_from_devices(small_g[:, o2:o2 + FFN_CONV_WIDTH * n_ffc].reshape(N_DEV, FFN_CONV_WIDTH, n_ffc))

    w_in_full = _cols_from_devices(w_in_g)
    r_sb, r_dn, r_z = 3 * SB_WIDTH, 3 * SB_WIDTH + DN_CONV_CH, 3 * SB_WIDTH + DN_CONV_CH + DN_V_WIDTH
    r_g = r_z + 2 * DN_HEADS
    w_main = jnp.concatenate([w_in_full[:, r_g:], w_in_full[:, r_sb:r_dn], w_in_full[:, r_dn:r_z],
                              w_in_full[:, :r_sb]], axis=1)
    w_ba = _pad_lanes(w_in_full[:, r_z:r_g], LANES)

    b_loc = lax.dynamic_slice(b_ada, (0, me * n_ada), (1, n_ada))
    mod_part = _ada_fwd(c_all, w_ada[0], b_loc)
    (mod_g,) = _all_gather([mod_part], "gather_mod")
    mod = lax.dynamic_index_in_dim(mod_g, me, axis=1, keepdims=False).reshape(1, N_DEV * n_ada)
    shift1, scale1, gate1, shift2, scale2, gate2 = [mod[:, i * d:(i + 1) * d] for i in range(6)]

    logit_lanes = ((0, 0), (GDN_LOGIT_LANE, LANES - GDN_LOGIT_LANE - DN_HEADS))
    a_log = jnp.pad(dn_A_log, logit_lanes)
    dt_b = jnp.pad(dn_dt_bias, logit_lanes)

    (h,) = _stage_fwd(_f_normmod, [norm1_w, shift1, scale1], [xs], [bf16], "norm1_fwd")
    proj = _mm(h, w_main, name="in_proj")
    ba = _mm(h, w_ba, name="in_proj_ba")
    k16, k0_16, k1_16, v16, v0_16, v1_16 = _sb_prepare(proj)
    o_a, sb_runs, w_psb_g, w_pdn_g, w_out_g, w_fin_g, w_fout_g = _sb_attention_fwd2(
        proj, k16, v0_16, v1_16, side=gather_later)
    w_psb = _cols_from_devices(w_psb_g)
    w_pdn = w_pdn_g.reshape(DN_V_WIDTH, d)
    w_o = w_out_g.reshape(d, d)
    w_fin = _cols_from_devices(w_fin_g)
    w_fout = w_fout_g.reshape(D_FF, d)
    qkv_act = _dn_conv_fwd(proj, dn_cw)
    o_b, states = _gdn_fwd(a_log, dt_b, dn_norm_w, qkv_act, proj, ba)
    pa = _mm(o_a, w_psb, name="proj_sb")
    pb = _mm(o_b, w_pdn, name="proj_dn")
    gates = [(proj, d, OFF_GA // d), (proj, d, OFF_GB // d)]
    (merged,) = _stage_fwd(_f_merge, [], gates + [pa, pb], [bf16], "merge_fwd")
    ao = _mm(merged, w_o, name="out_proj")
    (x1,) = _stage_fwd(_f_residual, [gate1], [xs, ao], [f32], "resid1_fwd")
    (h2,) = _stage_fwd(_f_normmod, [norm2_w, shift2, scale2], [x1], [bf16], "norm2_fwd")
    u_pre = _mm(h2, w_fin, name="ffn_in")
    act = _ffn_conv_fwd(u_pre, ffn_cw, ffn_conv_b)
    fo = _mm(act, w_fout, name="ffn_out")

    loss_p, d_gate2, d_wf, dx2, dfo = _loss_and_grads(gate2, final_norm_w.reshape(1, d), x1, fo, target)
    dact = _mm(dfo, w_fout, tb=True, name="ffn_out_dx")
    g_w_fout = _mm(act, dfo, ta=True, name="ffn_out_dw")
    du, dbg, dbu = _ffn_conv_bwd_act(u_pre, ffn_cw, ffn_conv_b, dact)
    du_pre, d_ffn_cw = _conv_bwd(du, u_pre, 0, ffn_cw, "ffn_conv_bwd")
    dh2 = _mm(du_pre, w_fin, tb=True, name="ffn_in_dx")
    g_w_fin = _mm(h2, du_pre, ta=True, name="ffn_in_dw")
    (d_n2w, d_shift2, d_scale2), (dx1,) = _stage_bwd(
        _f_normmod, [norm2_w, shift2, scale2], [x1], [dh2], [f32], "norm2_bwd", residual=(0, dx2))
    (d_gate1,), (dao,) = _stage_bwd(_f_residual, [gate1], [xs, ao], [dx1], [None, bf16], "resid1_bwd")
    dmerged = _mm(dao, w_o, tb=True, name="out_proj_dx")
    g_w_o = _mm(merged, dao, ta=True, name="out_proj_dw")
    _, (dga, dgb, dpa, dpb) = _stage_bwd(_f_merge, [], gates + [pa, pb], [dmerged], [bf16] * 4, "merge_bwd")
    do_a = _mm(dpa, w_psb, tb=True, name="proj_sb_dx")
    g_w_psb = _mm(o_a, dpa, ta=True, name="proj_sb_dw")
    do_b = _mm(dpb, w_pdn, tb=True, name="proj_dn_dx")
    g_w_pdn = _mm(o_b, dpb, ta=True, name="proj_dn_dw")
    early = [_cols_by_device(g_w_psb).astype(bf16),
             g_w_pdn.reshape(N_DEV, DN_V_WIDTH // N_DEV, d).astype(bf16),
             g_w_o.reshape(N_DEV, d // N_DEV, d).astype(bf16),
             _cols_by_device(g_w_fin).astype(bf16),
             g_w_fout.reshape(N_DEV, D_FF // N_DEV, d).astype(bf16)]
    exchange_early = _SideComm(_exchange_protocol, early, [jax.ShapeDtypeStruct(a.shape, a.dtype) for a in early])
    dq, dk, dv, *recv_early = _sb_attention_bwd2(proj, k16, k0_16, k1_16, v16, sb_runs, do_a, side=exchange_early)
    d_alog, d_dtb, d_dnw, dqkv_act, dz, dba = _gdn_bwd(a_log, dt_b, dn_norm_w, qkv_act, proj, ba, states, do_b)
    d_conv_out = _dn_conv_bwd_act(proj, dn_cw, dqkv_act)
    d_dn_pre, d_dn_cw = _conv_bwd(d_conv_out, proj, OFF_DN // TCONV_C, dn_cw, "dn_conv_bwd")
    dproj = jnp.concatenate([dga, dgb, d_dn_pre, dz, dq, dk.astype(bf16), dv.astype(bf16)], axis=1)
    dh = _mm(dproj, w_main, tb=True, name="in_proj_dx")
    dh_ba = _mm(dba, w_ba, tb=True, name="in_proj_ba_dx")
    g_w_main = _mm(h, dproj, ta=True, name="in_proj_dw")
    g_w_ba = _mm(h, dba, ta=True, name="in_proj_ba_dw")
    (d_n1w, d_shift1, d_scale1), (grad_x,) = _stage_bwd(
        _f_normmod, [norm1_w, shift1, scale1], [xs], [[dh, dh_ba]], [f32], "norm1_bwd", residual=(0, dx1))

    dmod = jnp.concatenate([d_shift1, d_scale1, d_gate1, d_shift2, d_scale2, d_gate2], axis=1)
    d_ffn_cb = jnp.concatenate([dbg, dbu], axis=1)
    small_parts = jnp.concatenate(
        [loss_p, dmod, d_n1w, d_alog, d_dtb, d_dnw, d_n2w, d_ffn_cb, d_wf,
         d_dn_cw.reshape(1, -1), d_ffn_cw.reshape(1, -1)], axis=1)
    (small_parts_g,) = _all_gather([small_parts], "gather_small_grads")
    tot = _sum_devices(small_parts_g[:, 0, :])
    offs = {}
    pos = 0
    for nm, width in (("loss", LANES), ("b_ada", 6 * d), ("norm1_w", d), ("dn_A_log", LANES), ("dn_dt_bias", LANES),
                      ("dn_norm_w", LANES), ("norm2_w", d), ("ffn_conv_b", 2 * D_FF), ("final_norm_w", d),
                      ("dn_conv_w", DN_CONV_WIDTH * DN_CONV_CH), ("ffn_conv_w", FFN_CONV_WIDTH * 2 * D_FF)):
        offs[nm] = (pos, width)
        pos += width
    seg = lambda nm: tot[:, offs[nm][0]:offs[nm][0] + offs[nm][1]]
    loss = tot[0, 0]
    g_b_ada = seg("b_ada")
    g_norm1 = seg("norm1_w")
    g_alog = seg("dn_A_log")[:, GDN_LOGIT_LANE:GDN_LOGIT_LANE + DN_HEADS]
    g_dtb = seg("dn_dt_bias")[:, GDN_LOGIT_LANE:GDN_LOGIT_LANE + DN_HEADS]
    g_dnw = seg("dn_norm_w")
    g_norm2 = seg("norm2_w")
    g_ffn_cb = seg("ffn_conv_b")
    g_fnw = seg("final_norm_w")
    g_dn_cw = lax.dynamic_slice(seg("dn_conv_w").reshape(DN_CONV_WIDTH, DN_CONV_CH), (0, me * n_dnc),
                                (DN_CONV_WIDTH, n_dnc))
    g_ffn_cw = lax.dynamic_slice(seg("ffn_conv_w").reshape(FFN_CONV_WIDTH, 2 * D_FF), (0, me * n_ffc),
                                 (FFN_CONV_WIDTH, n_ffc))

    dmod_all = small_parts_g[:, 0, offs["b_ada"][0]:offs["b_ada"][0] + 6 * d]
    g_w_ada = _ada_bwd(c_all, lax.dynamic_slice(dmod_all, (0, me * n_ada), (N_DEV, n_ada)))

    def pack(parts):
        flat = [p.reshape(1, -1) for p in parts]
        flat = [_pad_lanes(p, -(-p.shape[1] // LANES) * LANES) for p in flat]
        return jnp.concatenate(flat, axis=1), [p.shape[1] for p in flat]

    small_names_g = [g_b_ada, g_norm1, g_alog, g_dtb, g_dnw, g_norm2, g_ffn_cb, g_fnw, g_dn_cw, g_ffn_cw]
    small_w = [b_ada, norm1_w, dn_A_log, dn_dt_bias, dn_norm_w, norm2_w, ffn_conv_b, final_norm_w, dn_conv_w[0], ffn_conv_w[0]]
    small_m = [m_b_ada, m_norm1_w, m_dn_A_log, m_dn_dt_bias, m_dn_norm_w, m_norm2_w, m_ffn_conv_b, m_final_norm_w, m_dn_conv_w[0], m_ffn_conv_w[0]]
    small_v = [v_b_ada, v_norm1_w, v_dn_A_log, v_dn_dt_bias, v_dn_norm_w, v_norm2_w, v_ffn_conv_b, v_final_norm_w, v_dn_conv_w[0], v_ffn_conv_w[0]]
    pg, widths = pack(small_names_g)
    pw, _ = pack(small_w)
    pm, _ = pack(small_m)
    pv, _ = pack(small_v)
    s_delta, s_m, s_v = _adamw(pw, pg, pm, pv, "adamw_small")

    def unpack(flat):
        out, pos = [], 0
        for ref_arr, width in zip(small_w, widths):
            out.append(flat[:, pos:pos + ref_arr.size].reshape(ref_arr.shape))
            pos += width
        return out

    small_grads = [g.reshape(w_.shape) for g, w_ in zip(small_names_g, small_w)]
    small_delta, small_newm, small_newv = unpack(s_delta), unpack(s_m), unpack(s_v)

    ada_delta, ada_m, ada_v = _adamw(w_ada[0], g_w_ada, m_w_ada[0], v_w_ada[0], "adamw_ada")

    g_w_in_full = jnp.concatenate([g_w_main[:, OFF_SBQ:], g_w_main[:, OFF_DN:OFF_Z], g_w_main[:, OFF_Z:OFF_SBQ],
                                   g_w_ba[:, :2 * DN_HEADS], g_w_main[:, :OFF_DN]], axis=1)
    recv = list(_all_to_all([_cols_by_device(g_w_in_full).astype(bf16)], "exchange_w_in_grads")) + list(recv_early)
    big = {}
    for nm, parts, w_, m_, v_ in (("w_in", recv[0], w_in, m_w_in, v_w_in),
                                  ("w_proj_sb", recv[1], w_proj_sb, m_w_proj_sb, v_w_proj_sb),
                                  ("w_proj_dn", recv[2], w_proj_dn, m_w_proj_dn, v_w_proj_dn),
                                  ("w_out", recv[3], w_out, m_w_out, v_w_out),
                                  ("w_ffn_in", recv[4], w_ffn_in, m_w_ffn_in, v_w_ffn_in),
                                  ("w_ffn_out", recv[5], w_ffn_out, m_w_ffn_out, v_w_ffn_out)):
        big[nm] = [t[None] for t in _sum_adamw(parts, w_[0], m_[0], v_[0], "adamw_" + nm)]

    sg = dict(zip(["b_ada", "norm1_w", "dn_A_log", "dn_dt_bias", "dn_norm_w", "norm2_w", "ffn_conv_b", "final_norm_w",
                   "dn_conv_w", "ffn_conv_w"], range(10)))

    def small_out(table, nm):
        val = table[sg[nm]]
        return val[None] if nm in ("dn_conv_w", "ffn_conv_w") else val

    order = ["w_ada", "b_ada", "norm1_w", "w_in", "dn_conv_w", "dn_A_log", "dn_dt_bias", "dn_norm_w", "w_proj_sb",
             "w_proj_dn", "w_out", "norm2_w", "w_ffn_in", "ffn_conv_w", "ffn_conv_b", "w_ffn_out", "final_norm_w"]
    groups = []
    for k, small_table in enumerate((small_grads, small_delta, small_newm, small_newv)):
        row = []
        for nm in order:
            if nm == "w_ada":
                row.append((g_w_ada, ada_delta, ada_m, ada_v)[k][None])
            elif nm in big:
                row.append(big[nm][k])
            else:
                row.append(small_out(small_table, nm))
        groups.append(row)
    return (loss, grad_x[None], *groups[0], *groups[1], *groups[2], *groups[3])
```

```python
import functools

import jax
import jax.numpy as jnp
from jax import lax
from jax.experimental import pallas as pl
from jax.experimental.pallas import tpu as pltpu

f32 = jnp.float32
bf16 = jnp.bfloat16

D_MODEL = 1024
SB_HEADS = 8
SB_HEAD_DIM = 64
SB_WIDTH = SB_HEADS * SB_HEAD_DIM
SB_QBLOCK = 128
DN_HEADS = 8
DN_KEY_DIM = 64
DN_VAL_DIM = 128
DN_QK_WIDTH = DN_HEADS * DN_KEY_DIM
DN_V_WIDTH = DN_HEADS * DN_VAL_DIM
DN_CONV_CH = 2 * DN_QK_WIDTH + DN_V_WIDTH
DN_CONV_WIDTH = 4
DN_CHUNK = 64
D_FF = 2816
FFN_CONV_WIDTH = 3
NORM_EPS = 1e-6
L2_EPS = 1e-6
ADAM_LR = 0.001
ADAM_B1 = 0.9
ADAM_B2 = 0.999
ADAM_EPS = 1e-08
ADAM_WD = 0.01
ADAM_STEP = 10

N_DEV = 8
MESH = pl.DeviceIdType.MESH

LANES = 128
SUBLANES = 8
VMEM_LIMIT = 48 * 1024 * 1024

OFF_GA = 0
OFF_GB = D_MODEL
OFF_DN = 2 * D_MODEL
OFF_Z = OFF_DN + DN_CONV_CH
OFF_SBQ = OFF_Z + DN_V_WIDTH
OFF_SBK = OFF_SBQ + SB_WIDTH
OFF_SBV = OFF_SBK + SB_WIDTH
MAIN_WIDTH = OFF_SBV + SB_WIDTH

TM = 256
TCONV_R = 512
TCONV_C = 512
TCONV_FF = D_FF // 2
SB_PAIRS_PER_STEP = 2
SB_DEAD = -106.0
SB_NEVER = -1e30


def _cparams(sem=None):
    return pltpu.CompilerParams(dimension_semantics=sem, vmem_limit_bytes=VMEM_LIMIT)


def _pick(n, cands):
    for c in cands:
        if n % c == 0:
            return c
    return n


def _my_pos():
    return lax.axis_index("x"), lax.axis_index("y"), lax.axis_index("c")


def _flip(v, bit):
    return 1 - v if bit else v


def _comm_scratch(n):
    return [pltpu.SemaphoreType.DMA((n, 7)), pltpu.SemaphoreType.DMA((n, 7)), pltpu.SemaphoreType.DMA((n,))]


def _gather_protocol(ins, outs, send_sems, recv_sems, local_sems):
    n = len(ins)
    x, y, c = _my_pos()
    me, sibling = (x, y, c), (x, y, 1 - c)
    chips = [(1 - x, y), (x, 1 - y), (1 - x, 1 - y)]

    def slot(out, pos):
        return out.at[4 * pos[0] + 2 * pos[1] + pos[2]]

    def copy(a, k, block, to, src=None):
        return pltpu.make_async_remote_copy(
            src_ref=slot(outs[a], block) if src is None else src, dst_ref=slot(outs[a], block),
            send_sem=send_sems.at[a, k], recv_sem=recv_sems.at[a, k], device_id=to, device_id_type=MESH)

    def local(a):
        return pltpu.make_async_copy(ins[a], slot(outs[a], me), local_sems.at[a])

    def first(a):
        return [copy(a, 0, me, sibling, src=ins[a])] + [copy(a, 1 + j, me, (*chip, c), src=ins[a])
                                                         for j, chip in enumerate(chips)]

    def start():
        for a in range(n):
            local(a).start()
            for cp in first(a):
                cp.start()

    def finish():
        forwards = []
        for a in range(n):
            for j, chip in enumerate(chips):
                copy(a, 1 + j, (*chip, c), me).wait_recv()
                fwd = copy(a, 4 + j, (*chip, c), sibling)
                fwd.start()
                forwards.append(fwd)
        for a in range(n):
            copy(a, 0, sibling, me).wait_recv()
            for j, chip in enumerate(chips):
                copy(a, 4 + j, (*chip, 1 - c), me).wait_recv()
        for a in range(n):
            for cp in first(a):
                cp.wait_send()
        for cp in forwards:
            cp.wait_send()
        for a in range(n):
            local(a).wait()

    return start, finish


def _exchange_protocol(ins, outs, send_sems, recv_sems, local_sems):
    n = len(ins)
    x, y, c = _my_pos()
    me_idx = 4 * x + 2 * y + c

    def local(a):
        return pltpu.make_async_copy(ins[a].at[me_idx], outs[a].at[me_idx], local_sems.at[a])

    def copies(a, m):
        peer = (_flip(x, m & 4), _flip(y, m & 2), _flip(c, m & 1))
        peer_idx = 4 * peer[0] + 2 * peer[1] + peer[2]
        sems = dict(send_sem=send_sems.at[a, m - 1], recv_sem=recv_sems.at[a, m - 1], device_id=peer,
                    device_id_type=MESH)
        send = pltpu.make_async_remote_copy(src_ref=ins[a].at[peer_idx], dst_ref=outs[a].at[me_idx], **sems)
        recv = pltpu.make_async_remote_copy(src_ref=ins[a].at[peer_idx], dst_ref=outs[a].at[peer_idx], **sems)
        return send, recv

    def start():
        for a in range(n):
            local(a).start()
            for m in range(1, N_DEV):
                copies(a, m)[0].start()

    def finish():
        for a in range(n):
            for m in range(1, N_DEV):
                copies(a, m)[1].wait_recv()
        for a in range(n):
            for m in range(1, N_DEV):
                copies(a, m)[0].wait_send()
            local(a).wait()

    return start, finish


def _collective_call(protocol, arrs, out_shapes, name):
    n = len(arrs)

    def body(*refs):
        start, finish = protocol(refs[:n], refs[n:2 * n], *refs[2 * n:])
        start()
        finish()

    any_spec = pl.BlockSpec(memory_space=pl.ANY)
    return pl.pallas_call(body, name=name, out_shape=out_shapes, in_specs=[any_spec] * n, out_specs=[any_spec] * n,
                          scratch_shapes=_comm_scratch(n))(*arrs)


def _gathered_shapes(arrs):
    return [jax.ShapeDtypeStruct((N_DEV,) + a.shape, a.dtype) for a in arrs]


def _all_gather(arrs, name):
    return _collective_call(_gather_protocol, arrs, _gathered_shapes(arrs), name)


MM_BLOCK_BYTES = 4 * 1024 * 1024


def _mm_tiles(m_dim, n_dim, k_dim, a_bytes, b_bytes):
    tm = _pick(m_dim, (1024, 512, 256, 128))
    tn = _pick(n_dim, (1024, 512, 256, 128))
    tk = k_dim
    if k_dim % LANES == 0:
        units = k_dim // LANES
        fits = [u for u in range(1, units + 1) if units % u == 0
                and u * LANES * max(tm * a_bytes, tn * b_bytes) <= MM_BLOCK_BYTES]
        tk = max(fits) * LANES
    return tm, tn, tk


def _mm(a, b, *, ta=False, tb=False, name, side=None):
    (k_dim, m_dim) = a.shape if ta else a.shape[::-1]
    (n_dim, kb_dim) = b.shape if tb else b.shape[::-1]
    assert k_dim == kb_dim, (a.shape, b.shape, ta, tb)
    tm, tn, tk = _mm_tiles(m_dim, n_dim, k_dim, a.dtype.itemsize, b.dtype.itemsize)
    nk = k_dim // tk
    grid = (m_dim // tm, n_dim // tn, nk)
    dims = (((0 if ta else 1,), (1 if tb else 0,)), ((), ()))
    ns = side.n if side else 0

    def body(a_ref, b_ref, *rest):
        if side:
            side.run(rest[:ns], rest[ns + 1:2 * ns + 1], rest[2 * ns + 1:], *_grid_ends(grid),
                     lambda: compute(a_ref, b_ref, rest[ns]))
        else:
            compute(a_ref, b_ref, rest[0])

    def compute(a_ref, b_ref, o_ref):
        part = lax.dot_general(a_ref[...].astype(bf16), b_ref[...].astype(bf16), dims, preferred_element_type=f32)
        if nk == 1:
            o_ref[...] = part
        else:
            k = pl.program_id(2)

            @pl.when(k == 0)
            def _():
                o_ref[...] = part

            @pl.when(k > 0)
            def _():
                o_ref[...] += part

    a_spec = pl.BlockSpec((tk, tm), lambda i, j, k: (k, i)) if ta else pl.BlockSpec((tm, tk), lambda i, j, k: (i, k))
    b_spec = pl.BlockSpec((tn, tk), lambda i, j, k: (j, k)) if tb else pl.BlockSpec((tk, tn), lambda i, j, k: (k, j))
    out_spec = pl.BlockSpec((tm, tn), lambda i, j, k: (i, j))
    out_shape = jax.ShapeDtypeStruct((m_dim, n_dim), f32)
    if not side:
        return pl.pallas_call(body, name=name, grid=grid, in_specs=[a_spec, b_spec], out_specs=out_spec,
                              out_shape=out_shape,
                              compiler_params=_cparams(("parallel", "parallel", "arbitrary")))(a, b)
    return pl.pallas_call(
        body, name=name, grid=grid, in_specs=[a_spec, b_spec] + side.specs(), out_specs=[out_spec] + side.specs(),
        out_shape=[out_shape] + side.out_shapes, scratch_shapes=_comm_scratch(ns),
        compiler_params=_cparams(("arbitrary", "arbitrary", "arbitrary")))(a, b, *side.arrs)


def _win(t):
    return t if isinstance(t, tuple) else (t, t.shape[1], 0)


def _tile_spec(width, cb, tm):
    return pl.BlockSpec((tm, width), lambda i: (i, cb))


def _param_spec(p):
    return pl.BlockSpec(p.shape, lambda i: (0, 0))


def _stage_fwd(f, params, tiles, out_dtypes, name):
    tiles = [_win(t) for t in tiles]
    rows = tiles[0][0].shape[0]
    tm = min(TM, rows)
    avals = jax.eval_shape(f, *[jax.ShapeDtypeStruct(p.shape, f32) for p in params],
                           *[jax.ShapeDtypeStruct((tm, w), f32) for _, w, _ in tiles])
    n_p, n_t = len(params), len(tiles)

    def body(*refs):
        p = [r[...] for r in refs[:n_p]]
        t = [r[...].astype(f32) for r in refs[n_p:n_p + n_t]]
        for o_ref, val in zip(refs[n_p + n_t:], f(*p, *t)):
            o_ref[...] = val.astype(o_ref.dtype)

    return pl.pallas_call(
        body, name=name, grid=(rows // tm,),
        in_specs=[_param_spec(p) for p in params] + [_tile_spec(w, cb, tm) for _, w, cb in tiles],
        out_specs=[_tile_spec(a.shape[1], 0, tm) for a in avals],
        out_shape=[jax.ShapeDtypeStruct((rows, a.shape[1]), dt) for a, dt in zip(avals, out_dtypes)],
        compiler_params=_cparams(("parallel",)),
    )(*params, *[t[0] for t in tiles])


def _stage_bwd(f, params, tiles, cts, grad_dtypes, name, residual=None):
    tiles = [_win(t) for t in tiles]
    rows = tiles[0][0].shape[0]
    tm = min(TM, rows)
    cts = [list(g) if isinstance(g, (list, tuple)) else [g] for g in cts]
    flat_cts = [a for g in cts for a in g]
    n_p, n_t, n_c = len(params), len(tiles), len(flat_cts)
    has_res = residual is not None
    want = [j for j, dt in enumerate(grad_dtypes) if dt is not None]

    def body(*refs):
        i = pl.program_id(0)
        p = [r[...] for r in refs[:n_p]]
        t = [r[...].astype(f32) for r in refs[n_p:n_p + n_t]]
        ct_vals = [r[...].astype(f32) for r in refs[n_p + n_t:n_p + n_t + n_c]]
        ct, at = [], 0
        for g in cts:
            ct.append(functools.reduce(jnp.add, ct_vals[at:at + len(g)]))
            at += len(g)
        ct = tuple(ct)
        pos = n_p + n_t + n_c
        res_ref = refs[pos] if has_res else None
        pos += 1 if has_res else 0
        dp_refs = refs[pos:pos + n_p]
        dt_refs = refs[pos + n_p:]
        _, vjp = jax.vjp(f, *p, *t)
        grads = vjp(ct)

        @pl.when(i == 0)
        def _():
            for r in dp_refs:
                r[...] = jnp.zeros_like(r)

        for r, g in zip(dp_refs, grads[:n_p]):
            r[...] += g
        for r, j in zip(dt_refs, want):
            g = grads[n_p + j]
            if has_res and j == residual[0]:
                g = g + res_ref[...].astype(f32)
            r[...] = g.astype(r.dtype)

    in_arrays = list(params) + [t[0] for t in tiles] + flat_cts
    in_specs = ([_param_spec(p) for p in params] + [_tile_spec(w, cb, tm) for _, w, cb in tiles]
                + [_tile_spec(c.shape[1], 0, tm) for c in flat_cts])
    if has_res:
        in_arrays.append(residual[1])
        in_specs.append(_tile_spec(residual[1].shape[1], 0, tm))
    out_shape = ([jax.ShapeDtypeStruct(p.shape, f32) for p in params]
                 + [jax.ShapeDtypeStruct((rows, tiles[j][1]), grad_dtypes[j]) for j in want])
    out_specs = [_param_spec(p) for p in params] + [_tile_spec(tiles[j][1], 0, tm) for j in want]
    outs = pl.pallas_call(
        body, name=name, grid=(rows // tm,), in_specs=in_specs, out_specs=out_specs, out_shape=out_shape,
        compiler_params=_cparams(("arbitrary",)),
    )(*in_arrays)
    return outs[:n_p], outs[n_p:]


def _rms(x, w):
    return x * lax.rsqrt(jnp.mean(x * x, axis=-1, keepdims=True) + NORM_EPS) * w


def _f_normmod(w, shift, scale, x):
    return (_rms(x, w) * (1.0 + scale) + shift,)


def _f_merge(ga, gb, pa, pb):
    return (jax.nn.sigmoid(ga) * pa + jax.nn.sigmoid(gb) * pb,)


def _f_residual(gate, x, branch):
    return (x + gate * branch,)


def _f_loss(gate, wf, x1, fo, target):
    y = _rms(x1 + gate * fo, wf)
    err = jnp.square(y - target)
    return (0.5 * jnp.sum(jnp.mean(err, axis=-1, keepdims=True), axis=0, keepdims=True),)


def _loss_and_grads(gate2, wf, x1, fo, target):
    rows, d = x1.shape
    tm = min(TM, rows)

    def body(g_ref, w_ref, x_ref, fo_ref, t_ref, loss_ref, dg_ref, dw_ref, dx_ref, dfo_ref):
        i = pl.program_id(0)
        (val,), vjp = jax.vjp(_f_loss, g_ref[...], w_ref[...], x_ref[...], fo_ref[...], t_ref[...])
        dg, dw, dx, dfo, _ = vjp((jnp.ones((1, 1), f32),))

        @pl.when(i == 0)
        def _():
            loss_ref[...] = jnp.zeros_like(loss_ref)
            dg_ref[...] = jnp.zeros_like(dg_ref)
            dw_ref[...] = jnp.zeros_like(dw_ref)

        loss_ref[...] += jnp.broadcast_to(val, loss_ref.shape)
        dg_ref[...] += dg
        dw_ref[...] += dw
        dx_ref[...] = dx
        dfo_ref[...] = dfo.astype(bf16)

    vec = pl.BlockSpec((1, d), lambda i: (0, 0))
    tile = pl.BlockSpec((tm, d), lambda i: (i, 0))
    return pl.pallas_call(
        body, name="loss_fwd_bwd", grid=(rows // tm,),
        in_specs=[vec, vec, tile, tile, tile],
        out_specs=[pl.BlockSpec((1, LANES), lambda i: (0, 0)), vec, vec, tile, tile],
        out_shape=[jax.ShapeDtypeStruct((1, LANES), f32), jax.ShapeDtypeStruct((1, d), f32),
                   jax.ShapeDtypeStruct((1, d), f32), jax.ShapeDtypeStruct((rows, d), f32),
                   jax.ShapeDtypeStruct((rows, d), bf16)],
        compiler_params=_cparams(("arbitrary",)),
    )(gate2, wf, x1, fo, target)


def _softplus(z):
    return jnp.maximum(z, 0.0) + jnp.log(1.0 + jnp.exp(-jnp.abs(z)))


def _split_dot(a, m):
    hi = a.astype(bf16)
    lo = (a - hi.astype(f32)).astype(bf16)
    return jnp.dot(hi, m, preferred_element_type=f32) + jnp.dot(lo, m, preferred_element_type=f32)


def _suffix_matrix(n):
    r = lax.broadcasted_iota(jnp.int32, (n, n), 0)
    c = lax.broadcasted_iota(jnp.int32, (n, n), 1)
    return (r > c).astype(bf16)


def _head_masks():
    lane = lax.broadcasted_iota(jnp.int32, (1, LANES), 1)
    return [(lane < SB_HEAD_DIM).astype(f32), (lane >= SB_HEAD_DIM).astype(f32)]


def _sb_prepare(proj):
    def f(k, v):
        lane = lax.broadcasted_iota(jnp.int32, (1, SB_WIDTH), 1)
        m0 = (jnp.bitwise_and(lane, LANES - 1) < SB_HEAD_DIM).astype(f32)
        m1 = 1.0 - m0
        return k, k * m0, k * m1, v, v * m0, v * m1

    wins = [(proj, SB_WIDTH, OFF_SBK // SB_WIDTH), (proj, SB_WIDTH, OFF_SBV // SB_WIDTH)]
    return _stage_fwd(f, [], wins, [bf16] * 6, "sb_prepare")


def _stack_heads(x):
    m0, m1 = _head_masks()
    return jnp.concatenate([x * m0, x * m1], axis=0)


def _sb_logits(qst, k, t_pos2, kb, bq, masked):
    z = lax.dot_general(qst, k, (((1,), (1,)), ((), ())), preferred_element_type=f32)
    l = -_softplus(z)
    if masked:
        s_pos = kb * bq + lax.broadcasted_iota(jnp.int32, (1, bq), 1)
        causal = s_pos < t_pos2
        l = jnp.where(causal, l, 0.0)
    else:
        causal = None
    return z, l, causal


class _SideComm:
    def __init__(self, protocol, arrs, out_shapes):
        self.protocol, self.arrs, self.out_shapes = protocol, list(arrs), list(out_shapes)
        self.n = len(self.arrs)

    def specs(self):
        return [pl.BlockSpec(memory_space=pl.ANY)] * self.n

    def run(self, in_refs, out_refs, sems, first, last, compute):
        start, finish = self.protocol(in_refs, out_refs, *sems)
        pl.when(first)(start)
        compute()
        pl.when(last)(finish)


def _grid_ends(grid):
    ids = [pl.program_id(axis) for axis in range(len(grid))]
    first = functools.reduce(jnp.logical_and, [i == 0 for i in ids])
    last = functools.reduce(jnp.logical_and, [i == g - 1 for i, g in zip(ids, grid)])
    return first, last


def _sb_attention_fwd2(proj, k16, v0_16, v1_16, side=None):
    rows = proj.shape[0]
    bq = SB_QBLOCK
    nq = rows // bq
    assert nq <= LANES, "one lane per key block"
    npair = SB_WIDTH // LANES
    scale = SB_HEAD_DIM ** -0.5

    npp = SB_PAIRS_PER_STEP
    wq = npp * LANES
    grid = (npair // npp, nq)
    ns = side.n if side else 0

    def body(q_ref, k_ref, v0_ref, v1_ref, *rest):
        o_ref, runs_ref = rest[ns], rest[ns + 1]
        if side:
            side.run(rest[:ns], rest[ns + 2:2 * ns + 2], rest[2 * ns + 2:], *_grid_ends(grid),
                     lambda: compute(q_ref, k_ref, v0_ref, v1_ref, o_ref, runs_ref))
        else:
            compute(q_ref, k_ref, v0_ref, v1_ref, o_ref, runs_ref)

    def compute(q_ref, k_ref, v0_ref, v1_ref, o_ref, runs_ref):
        qi = pl.program_id(1)
        pairs = [slice(pp * LANES, (pp + 1) * LANES) for pp in range(npp)]
        qst = [(_stack_heads(q_ref[:, s]) * scale).astype(bf16) for s in pairs]
        r = lax.broadcasted_iota(jnp.int32, (bq, 2 * bq), 0)
        c = lax.broadcasted_iota(jnp.int32, (bq, 2 * bq), 1)
        m2 = jnp.logical_or(r > c, c >= bq).astype(bf16)
        t_pos = qi * bq + lax.broadcasted_iota(jnp.int32, (bq, 1), 0)
        t_pos2 = jnp.concatenate([t_pos, t_pos], axis=0)
        lane = lax.broadcasted_iota(jnp.int32, (1, LANES), 1)
        runs_ref[...] = jnp.full(runs_ref.shape, SB_NEVER, f32)

        def tiles(kbs, carry, masked):
            jobs = [(pp, kb) for kb in kbs for pp in range(npp)]
            rows_k = [pl.ds(pl.multiple_of(kb * bq, bq), bq) for _, kb in jobs]
            zl = [_sb_logits(qst[pp], k_ref[rk, pairs[pp]], t_pos2, kb, bq, masked) for (pp, kb), rk in zip(jobs, rows_k)]
            cs = [_split_dot(l, m2) for _, l, _ in zl]
            run = [cr[0] for cr in carry]
            acc = [cr[1] for cr in carry]
            probs = []
            for (pp, kb), (z, l, causal), cs2 in zip(jobs, zl, cs):
                a = jnp.exp(z + l + cs2[:, :bq] + run[pp])
                if masked:
                    a = jnp.where(causal, a, 0.0)
                probs.append(a.astype(bf16))
                for hh in range(2):
                    cols = slice((2 * pp + hh) * LANES, (2 * pp + hh + 1) * LANES)
                    runs_ref[:, cols] = jnp.where(lane == kb, run[pp][hh * bq:(hh + 1) * bq], runs_ref[:, cols])
                run[pp] = run[pp] + cs2[:, bq:]
            for (pp, kb), rk, ab in zip(jobs, rows_k, probs):
                acc[pp] = (acc[pp] + jnp.dot(ab[:bq], v0_ref[rk, pairs[pp]], preferred_element_type=f32)
                           + jnp.dot(ab[bq:], v1_ref[rk, pairs[pp]], preferred_element_type=f32))
            return tuple(zip(run, acc))

        zero = (jnp.zeros((2 * bq, bq), f32), jnp.zeros((bq, LANES), f32))
        carry = tiles([qi], (zero,) * npp, True)

        def alive(cr):
            return functools.reduce(jnp.maximum, [jnp.max(run) for run, _ in cr]) > SB_DEAD

        def two(state):
            i, _, cr = state
            cr = tiles([qi - 1 - 2 * i, qi - 2 - 2 * i], cr, False)
            return i + 1, alive(cr), cr

        n_two = qi // 2
        i_end, still, carry = lax.while_loop(lambda st: jnp.logical_and(st[0] < n_two, st[1]), two,
                                             (jnp.int32(0), alive(carry), carry))
        last_one = jnp.logical_and(qi % 2 == 1, jnp.logical_and(still, i_end == n_two))
        carry = lax.cond(last_one, lambda cr: tiles([0], cr, False), lambda cr: cr, carry)
        for pp in range(npp):
            o_ref[:, pairs[pp]] = carry[pp][1]

    kv = pl.BlockSpec((rows, wq), lambda p, i: (0, p))
    return pl.pallas_call(
        body, name="sb_attn_fwd", grid=grid,
        in_specs=[pl.BlockSpec((bq, wq), lambda p, i: (i, OFF_SBQ // wq + p)), kv, kv, kv] + (side.specs() if side else []),
        out_specs=[pl.BlockSpec((bq, wq), lambda p, i: (i, p)),
                   pl.BlockSpec((bq, 2 * wq), lambda p, i: (i, p))] + (side.specs() if side else []),
        out_shape=[jax.ShapeDtypeStruct((rows, SB_WIDTH), f32),
                   jax.ShapeDtypeStruct((rows, SB_HEADS * LANES), f32)] + (side.out_shapes if side else []),
        scratch_shapes=_comm_scratch(ns) if side else [],
        compiler_params=_cparams(("arbitrary", "arbitrary")),
    )(proj, k16, v0_16, v1_16, *(side.arrs if side else []))


def _sb_attention_bwd2(proj, k16, k0_16, k1_16, v16, runs, do, side=None):
    rows = proj.shape[0]
    bq = SB_QBLOCK
    nq = rows // bq
    npair = SB_WIDTH // LANES
    scale = SB_HEAD_DIM ** -0.5
    tn = (((0,), (0,)), ((), ()))
    nt = (((1,), (1,)), ((), ()))

    npp = SB_PAIRS_PER_STEP
    wq = npp * LANES
    grid = (npair // npp, nq)
    ns = side.n if side else 0

    def body(q_ref, k_ref, k0_ref, k1_ref, v_ref, runs_ref, do_ref, *rest):
        outs = rest[ns:ns + 3]
        ins = (q_ref, k_ref, k0_ref, k1_ref, v_ref, runs_ref, do_ref)
        if side:
            side.run(rest[:ns], rest[ns + 3:2 * ns + 3], rest[2 * ns + 3:], *_grid_ends(grid),
                     lambda: compute(*ins, *outs))
        else:
            compute(*ins, *outs)

    def compute(q_ref, k_ref, k0_ref, k1_ref, v_ref, runs_ref, do_ref, dq_ref, dk_ref, dv_ref):
        qi = pl.program_id(1)

        @pl.when(qi == 0)
        def _():
            dk_ref[...] = jnp.zeros_like(dk_ref)
            dv_ref[...] = jnp.zeros_like(dv_ref)

        pairs = [slice(pp * LANES, (pp + 1) * LANES) for pp in range(npp)]
        qst = [(_stack_heads(q_ref[:, s]) * scale).astype(bf16) for s in pairs]
        dost = [_stack_heads(do_ref[:, s]).astype(bf16) for s in pairs]
        runs = [jnp.concatenate([runs_ref[:, 2 * pp * LANES:(2 * pp + 1) * LANES],
                                 runs_ref[:, (2 * pp + 1) * LANES:(2 * pp + 2) * LANES]], axis=0) for pp in range(npp)]
        r = lax.broadcasted_iota(jnp.int32, (bq, 2 * bq), 0)
        c = lax.broadcasted_iota(jnp.int32, (bq, 2 * bq), 1)
        suffix_m = _suffix_matrix(bq)
        m2 = jnp.logical_or(r < c, c >= bq).astype(bf16)
        t_pos = qi * bq + lax.broadcasted_iota(jnp.int32, (bq, 1), 0)
        t_pos2 = jnp.concatenate([t_pos, t_pos], axis=0)
        lane = lax.broadcasted_iota(jnp.int32, (1, LANES), 1)

        def tiles(kbs, carry, masked):
            jobs = [(pp, kb) for kb in kbs for pp in range(npp)]
            rows_k = [pl.ds(pl.multiple_of(kb * bq, bq), bq) for _, kb in jobs]
            zl = [_sb_logits(qst[pp], k_ref[rk, pairs[pp]], t_pos2, kb, bq, masked) for (pp, kb), rk in zip(jobs, rows_k)]
            das = [lax.dot_general(dost[pp], v_ref[rk, pairs[pp]], nt, preferred_element_type=f32)
                   for (pp, kb), rk in zip(jobs, rows_k)]
            sticks = [_split_dot(l, suffix_m) for _, l, _ in zl]
            probs, ps = [], []
            for (pp, kb), (z, l, causal), stick, da in zip(jobs, zl, sticks, das):
                run = jnp.sum(jnp.where(lane == kb, runs[pp], 0.0), axis=1, keepdims=True)
                a = jnp.exp(z + l + stick + run)
                if masked:
                    a = jnp.where(causal, a, 0.0)
                probs.append(a.astype(bf16))
                ps.append(da * a)
            pcs = [_split_dot(p, m2) for p in ps]
            pref = [cr[0] for cr in carry]
            dq_acc = [cr[1] for cr in carry]
            dzs = []
            for (pp, kb), (z, l, causal), p, pc2 in zip(jobs, zl, ps, pcs):
                dz = p * jnp.exp(l) - jnp.exp(z + l) * (pc2[:, :bq] + pref[pp])
                if masked:
                    dz = jnp.where(causal, dz, 0.0)
                dzs.append(dz.astype(bf16))
                pref[pp] = pref[pp] + pc2[:, bq:]
            for (pp, kb), rk, dzb, ab in zip(jobs, rows_k, dzs, probs):
                cols = pairs[pp]
                dq_acc[pp] = (dq_acc[pp] + jnp.dot(dzb[:bq], k0_ref[rk, cols], preferred_element_type=f32)
                              + jnp.dot(dzb[bq:], k1_ref[rk, cols], preferred_element_type=f32))
                dk_ref[rk, cols] += lax.dot_general(dzb, qst[pp], tn, preferred_element_type=f32)
                dv_ref[rk, cols] += lax.dot_general(ab, dost[pp], tn, preferred_element_type=f32)
            return tuple(zip(pref, dq_acc))

        zero = (jnp.zeros((2 * bq, bq), f32), jnp.zeros((bq, LANES), f32))
        colmax = functools.reduce(jnp.maximum, [jnp.max(x, axis=0, keepdims=True) for x in runs])
        live = jnp.logical_and(colmax > SB_DEAD, lane < qi)
        kb0 = jnp.minimum(jnp.min(jnp.where(live, lane, LANES)), qi)
        n_blocks = qi - kb0
        carry = lax.fori_loop(0, n_blocks // 2, lambda i, cr: tiles([kb0 + 2 * i, kb0 + 2 * i + 1], cr, False),
                              (zero,) * npp)
        carry = lax.cond(n_blocks % 2 == 1, lambda cr: tiles([qi - 1], cr, False), lambda cr: cr, carry)
        carry = tiles([qi], carry, True)
        for pp in range(npp):
            dq_ref[:, pairs[pp]] = (carry[pp][1] * scale).astype(dq_ref.dtype)

    blk = pl.BlockSpec((bq, wq), lambda p, i: (i, p))
    full = pl.BlockSpec((rows, wq), lambda p, i: (0, p))
    return pl.pallas_call(
        body, name="sb_attn_bwd", grid=grid,
        in_specs=[pl.BlockSpec((bq, wq), lambda p, i: (i, OFF_SBQ // wq + p)), full, full, full, full,
                  pl.BlockSpec((bq, 2 * wq), lambda p, i: (i, p)), blk] + (side.specs() if side else []),
        out_specs=[blk, full, full] + (side.specs() if side else []),
        out_shape=[jax.ShapeDtypeStruct((rows, SB_WIDTH), bf16), jax.ShapeDtypeStruct((rows, SB_WIDTH), f32),
                   jax.ShapeDtypeStruct((rows, SB_WIDTH), f32)] + (side.out_shapes if side else []),
        scratch_shapes=_comm_scratch(ns) if side else [],
        compiler_params=_cparams(("arbitrary", "arbitrary")),
    )(proj, k16, k0_16, k1_16, v16, runs, do, *(side.arrs if side else []))


def _shift_down(x, prev8, j):
    if j == 0:
        return x
    r = pltpu.roll(x, j, axis=0)
    row8 = lax.broadcasted_iota(jnp.int32, prev8.shape, 0)
    head = jnp.where(row8 < j, pltpu.roll(prev8, j, axis=0), r[0:SUBLANES])
    return jnp.concatenate([head, r[SUBLANES:]], axis=0)


def _shift_up(x, next8, j):
    if j == 0:
        return x
    n = x.shape[0]
    r = pltpu.roll(x, n - j, axis=0)
    row8 = lax.broadcasted_iota(jnp.int32, next8.shape, 0)
    tail = jnp.where(row8 >= SUBLANES - j, pltpu.roll(next8, SUBLANES - j, axis=0), r[n - SUBLANES:n])
    return jnp.concatenate([r[:n - SUBLANES], tail], axis=0)


def _conv(x, prev8, w):
    k_taps = w.shape[0]
    out = x * w[k_taps - 1:k_taps, :]
    for j in range(1, k_taps):
        out = out + _shift_down(x, prev8, j) * w[k_taps - 1 - j:k_taps - j, :]
    return out


def _conv_tiles(rows, tr_max=TCONV_R):
    tr = min(tr_max, rows)
    return tr, rows // tr, tr // SUBLANES


def _prev_spec(tc, cb0, r8):
    return pl.BlockSpec((SUBLANES, tc), lambda j, i: (jnp.maximum(i * r8 - 1, 0), cb0 + j))


def _silu(x):
    return x * jax.nn.sigmoid(x)


def _dsilu(x):
    s = jax.nn.sigmoid(x)
    return s * (1.0 + x * (1.0 - s))


def _dn_conv_fwd(proj, w):
    rows = proj.shape[0]
    tr, nr, r8 = _conv_tiles(rows)
    tc = TCONV_C
    cb0 = OFF_DN // tc

    def body(x_ref, p_ref, w_ref, o_ref):
        prev = jnp.where(pl.program_id(1) == 0, 0.0, p_ref[...])
        o_ref[...] = _silu(_conv(x_ref[...], prev, w_ref[...]))

    return pl.pallas_call(
        body, name="dn_conv_fwd", grid=(DN_CONV_CH // tc, nr),
        in_specs=[pl.BlockSpec((tr, tc), lambda j, i: (i, cb0 + j)), _prev_spec(tc, cb0, r8),
                  pl.BlockSpec((DN_CONV_WIDTH, tc), lambda j, i: (0, j))],
        out_specs=pl.BlockSpec((tr, tc), lambda j, i: (i, j)),
        out_shape=jax.ShapeDtypeStruct((rows, DN_CONV_CH), f32),
        compiler_params=_cparams(("parallel", "parallel")),
    )(proj, proj, w)


def _dn_conv_bwd_act(proj, w, dact):
    rows = proj.shape[0]
    tr, nr, r8 = _conv_tiles(rows)
    tc = TCONV_C
    cb0 = OFF_DN // tc

    def body(x_ref, p_ref, w_ref, d_ref, o_ref):
        prev = jnp.where(pl.program_id(1) == 0, 0.0, p_ref[...])
        o_ref[...] = d_ref[...] * _dsilu(_conv(x_ref[...], prev, w_ref[...]))

    return pl.pallas_call(
        body, name="dn_conv_bwd_act", grid=(DN_CONV_CH // tc, nr),
        in_specs=[pl.BlockSpec((tr, tc), lambda j, i: (i, cb0 + j)), _prev_spec(tc, cb0, r8),
                  pl.BlockSpec((DN_CONV_WIDTH, tc), lambda j, i: (0, j)),
                  pl.BlockSpec((tr, tc), lambda j, i: (i, j))],
        out_specs=pl.BlockSpec((tr, tc), lambda j, i: (i, j)),
        out_shape=jax.ShapeDtypeStruct((rows, DN_CONV_CH), f32),
        compiler_params=_cparams(("parallel", "parallel")),
    )(proj, proj, w, dact)


def _ffn_conv_fwd(u_pre, w, b):
    rows = u_pre.shape[0]
    tr, nr, r8 = _conv_tiles(rows, TCONV_R // 2)
    tc = TCONV_FF
    nct = D_FF // tc

    def body(xg_ref, pg_ref, xu_ref, pu_ref, wg_ref, wu_ref, bg_ref, bu_ref, o_ref):
        first = pl.program_id(1) == 0
        ug = _conv(xg_ref[...], jnp.where(first, 0.0, pg_ref[...]), wg_ref[...]) + bg_ref[...]
        uu = _conv(xu_ref[...], jnp.where(first, 0.0, pu_ref[...]), wu_ref[...]) + bu_ref[...]
        o_ref[...] = (_silu(ug) * uu).astype(o_ref.dtype)

    def x_spec(off):
        return pl.BlockSpec((tr, tc), lambda j, i: (i, off + j))

    def w_spec(k, off):
        return pl.BlockSpec((k, tc), lambda j, i: (0, off + j))

    return pl.pallas_call(
        body, name="ffn_conv_fwd", grid=(nct, nr),
        in_specs=[x_spec(0), _prev_spec(tc, 0, r8), x_spec(nct), _prev_spec(tc, nct, r8),
                  w_spec(FFN_CONV_WIDTH, 0), w_spec(FFN_CONV_WIDTH, nct), w_spec(1, 0), w_spec(1, nct)],
        out_specs=pl.BlockSpec((tr, tc), lambda j, i: (i, j)),
        out_shape=jax.ShapeDtypeStruct((rows, D_FF), bf16),
        compiler_params=_cparams(("parallel", "parallel")),
    )(u_pre, u_pre, u_pre, u_pre, w, w, b, b)


def _ffn_conv_bwd_act(u_pre, w, b, dact):
    rows = u_pre.shape[0]
    tr, nr, r8 = _conv_tiles(rows, TCONV_R // 2)
    tc = TCONV_FF
    nct = D_FF // tc

    def body(xg_ref, pg_ref, xu_ref, pu_ref, wg_ref, wu_ref, bg_ref, bu_ref, d_ref,
             du_ref, dbg_ref, dbu_ref):
        i = pl.program_id(1)
        first = i == 0
        ug = _conv(xg_ref[...], jnp.where(first, 0.0, pg_ref[...]), wg_ref[...]) + bg_ref[...]
        uu = _conv(xu_ref[...], jnp.where(first, 0.0, pu_ref[...]), wu_ref[...]) + bu_ref[...]
        d = d_ref[...]
        dug = d * uu * _dsilu(ug)
        duu = d * _silu(ug)
        du_ref[0] = dug
        du_ref[1] = duu

        @pl.when(first)
        def _():
            dbg_ref[...] = jnp.zeros_like(dbg_ref)
            dbu_ref[...] = jnp.zeros_like(dbu_ref)

        dbg_ref[...] += jnp.sum(dug, axis=0, keepdims=True)
        dbu_ref[...] += jnp.sum(duu, axis=0, keepdims=True)

    def x_spec(off):
        return pl.BlockSpec((tr, tc), lambda j, i: (i, off + j))

    def w_spec(k, off):
        return pl.BlockSpec((k, tc), lambda j, i: (0, off + j))

    tile = pl.BlockSpec((tr, tc), lambda j, i: (i, j))
    vec = pl.BlockSpec((1, tc), lambda j, i: (0, j))
    return pl.pallas_call(
        body, name="ffn_conv_bwd_act", grid=(nct, nr),
        in_specs=[x_spec(0), _prev_spec(tc, 0, r8), x_spec(nct), _prev_spec(tc, nct, r8),
                  w_spec(FFN_CONV_WIDTH, 0), w_spec(FFN_CONV_WIDTH, nct), w_spec(1, 0), w_spec(1, nct), tile],
        out_specs=[pl.BlockSpec((2, tr, tc), lambda j, i: (0, i, j)), vec, vec],
        out_shape=[jax.ShapeDtypeStruct((2, rows, D_FF), f32),
                   jax.ShapeDtypeStruct((1, D_FF), f32), jax.ShapeDtypeStruct((1, D_FF), f32)],
        compiler_params=_cparams(("parallel", "arbitrary")),
    )(u_pre, u_pre, u_pre, u_pre, w, w, b, b, dact)


def _conv_bwd(dy, x, x_cb0, w, name):
    k_taps = w.shape[0]
    split = dy.ndim == 3
    rows = dy.shape[-2]
    ch = dy.shape[-1] * (2 if split else 1)
    tc = TCONV_FF if split else TCONV_C
    tr, nr, r8 = _conv_tiles(rows, TCONV_R // 2 if split else TCONV_R)
    per_half = dy.shape[-1] // tc
    last8 = rows // SUBLANES - 1

    def body(dy_ref, nx_ref, x_ref, p_ref, w_ref, dx_ref, dw_ref):
        i = pl.program_id(1)
        dyv = dy_ref[...]
        nxt = jnp.where(i == nr - 1, 0.0, nx_ref[...])
        prev = jnp.where(i == 0, 0.0, p_ref[...])
        xv = x_ref[...].astype(f32)
        wv = w_ref[...]

        @pl.when(i == 0)
        def _():
            dw_ref[...] = jnp.zeros_like(dw_ref)

        dx = dyv * wv[k_taps - 1:k_taps, :]
        dw_ref[k_taps - 1:k_taps, :] += jnp.sum(dyv * xv, axis=0, keepdims=True)
        for j in range(1, k_taps):
            dx = dx + _shift_up(dyv, nxt, j) * wv[k_taps - 1 - j:k_taps - j, :]
            dw_ref[k_taps - 1 - j:k_taps - j, :] += jnp.sum(dyv * _shift_down(xv, prev, j), axis=0, keepdims=True)
        dx_ref[...] = dx.astype(dx_ref.dtype)

    tile = pl.BlockSpec((tr, tc), lambda j, i: (i, j))
    if split:
        dy_spec = pl.BlockSpec((None, tr, tc), lambda j, i: (j // per_half, i, j % per_half))
        next_spec = pl.BlockSpec((None, SUBLANES, tc),
                                 lambda j, i: (j // per_half, jnp.minimum((i + 1) * r8, last8), j % per_half))
    else:
        dy_spec = tile
        next_spec = pl.BlockSpec((SUBLANES, tc), lambda j, i: (jnp.minimum((i + 1) * r8, last8), j))
    return pl.pallas_call(
        body, name=name, grid=(ch // tc, nr),
        in_specs=[dy_spec, next_spec,
                  pl.BlockSpec((tr, tc), lambda j, i: (i, x_cb0 + j)), _prev_spec(tc, x_cb0, r8),
                  pl.BlockSpec((k_taps, tc), lambda j, i: (0, j))],
        out_specs=[tile, pl.BlockSpec((k_taps, tc), lambda j, i: (0, j))],
        out_shape=[jax.ShapeDtypeStruct((rows, ch), bf16), jax.ShapeDtypeStruct((k_taps, ch), f32)],
        compiler_params=_cparams(("parallel", "arbitrary")),
    )(dy, dy, x, x, w)


def _hdot(a, b):
    return jnp.dot(a, b, preferred_element_type=f32, precision=lax.Precision.HIGH)


def _xdot(a, b):
    return jnp.dot(a, b, preferred_element_type=f32, precision=lax.Precision.HIGHEST)


def _bdot(a, b):
    return jnp.dot(a.astype(bf16), b.astype(bf16), preferred_element_type=f32)


def _bdot_nt(a, b):
    return lax.dot_general(a.astype(bf16), b.astype(bf16), (((1,), (1,)), ((), ())), preferred_element_type=f32)


def _bdot_tn(a, b):
    return lax.dot_general(a.astype(bf16), b.astype(bf16), (((0,), (0,)), ((), ())), preferred_element_type=f32)


GDN_GROUP = 4
GDN_NGROUPS = DN_HEADS // GDN_GROUP
GDN_ROWS = GDN_GROUP * DN_CHUNK
GDN_QK_LANES = GDN_GROUP * DN_KEY_DIM
GDN_LOGIT_LANE = DN_HEADS


def _inverse_impl(lows):
    n = lows[0].shape[0]
    r = lax.broadcasted_iota(jnp.int32, (n, n), 0)
    c = lax.broadcasted_iota(jnp.int32, (n, n), 1)
    eye = (r == c).astype(f32)
    blk = jnp.right_shift(r, 3) == jnp.right_shift(c, 3)
    d = [jnp.where(blk, low, 0.0) for low in lows]
    e = [low - x for low, x in zip(lows, d)]

    def nilpotent8_inverse(xs):
        acc = [eye - x for x in xs]
        power = xs
        for _ in range(2):
            power = [_bdot(x, x) for x in power]
            acc = [_bdot(a, eye + x) for a, x in zip(acc, power)]
        return acc

    dinv = nilpotent8_inverse(d)
    ninv = nilpotent8_inverse([_bdot(x, y) for x, y in zip(dinv, e)])
    t = [_bdot(x, y) for x, y in zip(ninv, dinv)]
    for _ in range(2):
        res = [eye - x - _hdot(low, x) for low, x in zip(lows, t)]
        t = [x + _bdot(x, y) for x, y in zip(t, res)]
    return tuple(t)


@jax.custom_vjp
def _unit_lower_inverses(lows):
    return _inverse_impl(lows)


def _unit_lower_inverses_fwd(lows):
    t = _inverse_impl(lows)
    return t, t


def _unit_lower_inverses_bwd(t, ct):
    tn = (((0,), (0,)), ((), ()))
    nt = (((1,), (1,)), ((), ()))
    left = [lax.dot_general(x, g, tn, preferred_element_type=f32, precision=lax.Precision.HIGH) for x, g in zip(t, ct)]
    return (tuple(-lax.dot_general(x, y, nt, preferred_element_type=f32, precision=lax.Precision.HIGH)
                  for x, y in zip(left, t)),)


_unit_lower_inverses.defvjp(_unit_lower_inverses_fwd, _unit_lower_inverses_bwd)


def _gdn_chunk(a_log, dt_bias, norm_w, ba, *per_group):
    ng = GDN_NGROUPS
    qgs, kgs, vsts, zsts, states = [per_group[i * ng:(i + 1) * ng] for i in range(5)]
    groups = range(ng)
    n = GDN_ROWS
    r = lax.broadcasted_iota(jnp.int32, (n, n), 0)
    c = lax.broadcasted_iota(jnp.int32, (n, n), 1)
    same_head = jnp.right_shift(r, 6) == jnp.right_shift(c, 6)
    incl = jnp.logical_and(same_head, r >= c)
    strict = jnp.logical_and(same_head, r > c)
    eye = (r == c).astype(f32)
    ones = jnp.ones((n, n), f32)
    own_lanes = same_head.astype(f32)
    lane = lax.broadcasted_iota(jnp.int32, (1, LANES), 1)
    pick = lambda arr, idx: jnp.sum(jnp.where(lane == idx, arr, 0.0), axis=1, keepdims=True)
    heads = [[GDN_GROUP * g + h for h in range(GDN_GROUP)] for g in groups]
    rc = lax.broadcasted_iota(jnp.int32, (DN_CHUNK, DN_CHUNK), 0)
    cc = lax.broadcasted_iota(jnp.int32, (DN_CHUNK, DN_CHUNK), 1)

    g_all = -jnp.exp(a_log) * _softplus(ba + dt_bias)
    gc_all = _xdot((rc >= cc).astype(f32), g_all)
    gl_all = jnp.sum(g_all, axis=0, keepdims=True)
    beta = [jnp.concatenate([jax.nn.sigmoid(pick(ba, hd)) for hd in heads[g]], axis=0) for g in groups]
    gc = [jnp.concatenate([pick(gc_all, GDN_LOGIT_LANE + hd) for hd in heads[g]], axis=0) for g in groups]
    g_last = [jnp.concatenate([jnp.broadcast_to(pick(gl_all, GDN_LOGIT_LANE + hd), (DN_CHUNK, 1)) for hd in heads[g]],
                              axis=0) for g in groups]
    gr = [jnp.broadcast_to(gc[g], (n, n)).T for g in groups]
    decay = [jnp.where(incl, jnp.exp(jnp.where(incl, gc[g] - gr[g], 0.0)), 0.0) for g in groups]
    q = [jnp.concatenate([qgs[g]] * GDN_GROUP, axis=0) * own_lanes for g in groups]
    k = [jnp.concatenate([kgs[g]] * GDN_GROUP, axis=0) * own_lanes for g in groups]
    qn = [x * lax.rsqrt(jnp.sum(x * x, axis=1, keepdims=True) + L2_EPS) * (DN_KEY_DIM ** -0.5) for x in q]
    kn = [x * lax.rsqrt(jnp.sum(x * x, axis=1, keepdims=True) + L2_EPS) for x in k]
    kb = [kn[g] * beta[g] for g in groups]
    low = [jnp.where(strict, _bdot_nt(kb[g], kn[g]) * decay[g], 0.0) for g in groups]
    intra = [jnp.where(incl, _bdot_nt(qn[g], kn[g]) * decay[g], 0.0) for g in groups]
    t = _unit_lower_inverses(tuple(low))
    u = [_bdot(t[g], vsts[g] * beta[g]) for g in groups]
    w = [_bdot(t[g], kb[g] * jnp.exp(gc[g])) for g in groups]
    sb = [s.astype(bf16) for s in states]
    v_new = [u[g] - jnp.dot(w[g].astype(bf16), sb[g], preferred_element_type=f32) for g in groups]
    o = [jnp.dot((qn[g] * jnp.exp(gc[g])).astype(bf16), sb[g], preferred_element_type=f32) for g in groups]
    o = [o[g] + _bdot(intra[g], v_new[g]) for g in groups]
    new_state = [states[g] * jnp.exp(g_last[g]) + _bdot_tn(kn[g] * jnp.exp(g_last[g] - gc[g]), v_new[g])
                 for g in groups]
    o_n = [x * lax.rsqrt(jnp.mean(x * x, axis=1, keepdims=True) + NORM_EPS) * norm_w for x in o]
    return tuple(o_n[g] * _silu(zsts[g]) for g in groups) + tuple(new_state)


def _gdn_specs(rows, reverse):
    n = rows // DN_CHUNK
    idx = (lambda i: n - 1 - i) if reverse else (lambda i: i)
    vec = pl.BlockSpec((1, LANES), lambda i: (0, 0))
    qkv = pl.BlockSpec((DN_CHUNK, DN_CONV_CH), lambda i: (idx(i), 0))
    z = pl.BlockSpec((DN_CHUNK, DN_V_WIDTH), lambda i: (idx(i), OFF_Z // DN_V_WIDTH))
    ba = pl.BlockSpec((DN_CHUNK, LANES), lambda i: (idx(i), 0))
    wide = pl.BlockSpec((DN_CHUNK, DN_V_WIDTH), lambda i: (idx(i), 0))
    st = pl.BlockSpec((1, DN_HEADS * DN_KEY_DIM, LANES), lambda i: (idx(i), 0, 0))
    return n, vec, qkv, z, ba, wide, st


def _gdn_slices(grp):
    q = slice(grp * GDN_QK_LANES, (grp + 1) * GDN_QK_LANES)
    k = slice(DN_QK_WIDTH + grp * GDN_QK_LANES, DN_QK_WIDTH + (grp + 1) * GDN_QK_LANES)
    heads = [slice((GDN_GROUP * grp + h) * LANES, (GDN_GROUP * grp + h + 1) * LANES) for h in range(GDN_GROUP)]
    vs = [slice(2 * DN_QK_WIDTH + s.start, 2 * DN_QK_WIDTH + s.stop) for s in heads]
    return q, k, vs, heads


def _stack_cols(ref, cols):
    return jnp.concatenate([ref[:, s] for s in cols], axis=0)


def _gdn_operands(qkv_ref, z_ref, state_rows):
    sl = [_gdn_slices(grp) for grp in range(GDN_NGROUPS)]
    return ([qkv_ref[:, q] for q, _, _, _ in sl] + [qkv_ref[:, k] for _, k, _, _ in sl]
            + [_stack_cols(qkv_ref, vs) for _, _, vs, _ in sl] + [_stack_cols(z_ref, heads) for _, _, _, heads in sl]
            + [state_rows[grp * GDN_ROWS:(grp + 1) * GDN_ROWS, :] for grp in range(GDN_NGROUPS)])


def _gdn_fwd(a_log, dt_bias, norm_w, qkv_act, proj, ba):
    rows = qkv_act.shape[0]
    n, vec, qkv_s, z_s, ba_s, wide, st_s = _gdn_specs(rows, False)

    def body(al_ref, dt_ref, nw_ref, qkv_ref, z_ref, ba_ref, o_ref, st_ref, state):
        @pl.when(pl.program_id(0) == 0)
        def _():
            state[...] = jnp.zeros_like(state)

        st_ref[0] = state[...]
        out = _gdn_chunk(al_ref[...], dt_ref[...], nw_ref[...], ba_ref[...], *_gdn_operands(qkv_ref, z_ref, state))
        for grp in range(GDN_NGROUPS):
            _, _, _, heads = _gdn_slices(grp)
            for h, s in enumerate(heads):
                o_ref[:, s] = out[grp][h * DN_CHUNK:(h + 1) * DN_CHUNK].astype(o_ref.dtype)
            state[grp * GDN_ROWS:(grp + 1) * GDN_ROWS, :] = out[GDN_NGROUPS + grp]

    return pl.pallas_call(
        body, name="gdn_fwd", grid=(n,),
        in_specs=[vec, vec, vec, qkv_s, z_s, ba_s], out_specs=[wide, st_s],
        out_shape=[jax.ShapeDtypeStruct((rows, DN_V_WIDTH), bf16),
                   jax.ShapeDtypeStruct((n, DN_HEADS * DN_KEY_DIM, LANES), f32)],
        scratch_shapes=[pltpu.VMEM((DN_HEADS * DN_KEY_DIM, LANES), f32)],
        compiler_params=_cparams(("arbitrary",)),
    )(a_log, dt_bias, norm_w, qkv_act, proj, ba)


def _gdn_bwd(a_log, dt_bias, norm_w, qkv_act, proj, ba, states, do):
    rows = qkv_act.shape[0]
    n, vec, qkv_s, z_s, ba_s, wide, st_s = _gdn_specs(rows, True)

    def body(al_ref, dt_ref, nw_ref, qkv_ref, z_ref, ba_ref, st_ref, do_ref,
             dal_ref, ddt_ref, dnw_ref, dqkv_ref, dz_ref, dba_ref, dstate):
        @pl.when(pl.program_id(0) == 0)
        def _():
            dstate[...] = jnp.zeros_like(dstate)
            dal_ref[...] = jnp.zeros_like(dal_ref)
            ddt_ref[...] = jnp.zeros_like(ddt_ref)
            dnw_ref[...] = jnp.zeros_like(dnw_ref)

        ng = GDN_NGROUPS
        _, vjp = jax.vjp(_gdn_chunk, al_ref[...], dt_ref[...], nw_ref[...], ba_ref[...],
                         *_gdn_operands(qkv_ref, z_ref, st_ref[0]))
        cts = tuple(_stack_cols(do_ref, _gdn_slices(grp)[3]) for grp in range(ng))
        cts += tuple(dstate[grp * GDN_ROWS:(grp + 1) * GDN_ROWS, :] for grp in range(ng))
        grads = vjp(cts)
        dal_ref[...] += grads[0]
        ddt_ref[...] += grads[1]
        dnw_ref[...] += grads[2]
        dba_ref[...] = grads[3]
        dqs, dks, dvs, dzs, dss = [grads[4 + i * ng:4 + (i + 1) * ng] for i in range(5)]
        for grp in range(ng):
            q, k, vs, heads = _gdn_slices(grp)
            dqkv_ref[:, q] = dqs[grp]
            dqkv_ref[:, k] = dks[grp]
            for h, (sv, sh) in enumerate(zip(vs, heads)):
                rows_h = slice(h * DN_CHUNK, (h + 1) * DN_CHUNK)
                dqkv_ref[:, sv] = dvs[grp][rows_h]
                dz_ref[:, sh] = dzs[grp][rows_h].astype(dz_ref.dtype)
            dstate[grp * GDN_ROWS:(grp + 1) * GDN_ROWS, :] = dss[grp]

    return pl.pallas_call(
        body, name="gdn_bwd", grid=(n,),
        in_specs=[vec, vec, vec, qkv_s, z_s, ba_s, st_s, wide],
        out_specs=[vec, vec, vec, qkv_s, wide, ba_s],
        out_shape=[jax.ShapeDtypeStruct((1, LANES), f32)] * 3
        + [jax.ShapeDtypeStruct((rows, DN_CONV_CH), f32), jax.ShapeDtypeStruct((rows, DN_V_WIDTH), bf16),
           jax.ShapeDtypeStruct((rows, LANES), f32)],
        scratch_shapes=[pltpu.VMEM((DN_HEADS * DN_KEY_DIM, LANES), f32)],
        compiler_params=_cparams(("arbitrary",)),
    )(a_log, dt_bias, norm_w, qkv_act, proj, ba, states, do)


def _ada_fwd(c_all, w_loc, b_loc):
    def body(c_ref, w_ref, b_ref, o_ref):
        o_ref[...] = _bdot(_silu(c_ref[...]), w_ref[...]) + b_ref[...]

    return pl.pallas_call(body, name="ada_fwd", out_shape=jax.ShapeDtypeStruct((c_all.shape[0], w_loc.shape[1]), f32),
                          compiler_params=_cparams())(c_all, w_loc, b_loc)


def _ada_bwd(c_all, dmod_cols):
    def body(c_ref, d_ref, o_ref):
        o_ref[...] = _bdot_tn(_silu(c_ref[...]), d_ref[...])

    return pl.pallas_call(body, name="ada_bwd",
                          out_shape=jax.ShapeDtypeStruct((c_all.shape[1], dmod_cols.shape[1]), f32),
                          compiler_params=_cparams())(c_all, dmod_cols)


def _sum_devices(parts):
    def body(p_ref, o_ref):
        acc = p_ref[0:1, :]
        for d in range(1, N_DEV):
            acc = acc + p_ref[d:d + 1, :]
        o_ref[...] = acc

    return pl.pallas_call(body, name="sum_small", out_shape=jax.ShapeDtypeStruct((1, parts.shape[1]), f32),
                          compiler_params=_cparams())(parts)


def _adam_math(w, g, m, v):
    m2 = ADAM_B1 * m + (1.0 - ADAM_B1) * g
    v2 = ADAM_B2 * v + (1.0 - ADAM_B2) * jnp.square(g)
    m_hat = m2 / (1.0 - ADAM_B1 ** ADAM_STEP)
    v_hat = v2 / (1.0 - ADAM_B2 ** ADAM_STEP)
    delta = -ADAM_LR * (m_hat / (jnp.sqrt(v_hat) + ADAM_EPS) + ADAM_WD * w)
    return delta, m2, v2


def _row_tile(rows):
    return _pick(rows, (256, 128, 64, 32, 16, 8))


def _adamw(w, g, m, v, name):
    rows, cols = w.shape
    tr = _row_tile(rows)

    def body(w_ref, g_ref, m_ref, v_ref, d_ref, m2_ref, v2_ref):
        d_ref[...], m2_ref[...], v2_ref[...] = _adam_math(w_ref[...], g_ref[...], m_ref[...], v_ref[...])

    tile = pl.BlockSpec((tr, cols), lambda i: (i, 0))
    return pl.pallas_call(body, name=name, grid=(rows // tr,), in_specs=[tile] * 4, out_specs=[tile] * 3,
                          out_shape=[jax.ShapeDtypeStruct(w.shape, f32)] * 3,
                          compiler_params=_cparams(("parallel",)))(w, g, m, v)


def _sum_adamw(parts, w, m, v, name):
    rows, cols = w.shape
    tr = _row_tile(rows)

    def body(p_ref, w_ref, m_ref, v_ref, g_ref, d_ref, m2_ref, v2_ref):
        g = p_ref[0].astype(f32)
        for d in range(1, N_DEV):
            g = g + p_ref[d].astype(f32)
        g_ref[...] = g
        d_ref[...], m2_ref[...], v2_ref[...] = _adam_math(w_ref[...], g, m_ref[...], v_ref[...])

    tile = pl.BlockSpec((tr, cols), lambda i: (i, 0))
    return pl.pallas_call(body, name=name, grid=(rows // tr,),
                          in_specs=[pl.BlockSpec((N_DEV, tr, cols), lambda i: (0, i, 0)), tile, tile, tile],
                          out_specs=[tile] * 4, out_shape=[jax.ShapeDtypeStruct(w.shape, f32)] * 4,
                          compiler_params=_cparams(("parallel",)))(parts, w, m, v)


def _pad_lanes(a, width):
    return jnp.pad(a, ((0, 0), (0, width - a.shape[1])))


def _cols_by_device(full):
    r, c = full.shape
    return jnp.moveaxis(full.reshape(r, N_DEV, c // N_DEV), 1, 0)


def _cols_from_devices(parts):
    d, r, n = parts.shape
    return jnp.moveaxis(parts, 0, 1).reshape(r, d * n)


def kernel(x, c, w_ada, b_ada, norm1_w, w_in, dn_conv_w, dn_A_log, dn_dt_bias, dn_norm_w, w_proj_sb, w_proj_dn, w_out, norm2_w, w_ffn_in, ffn_conv_w, ffn_conv_b, w_ffn_out, final_norm_w, loss_target, m_w_ada, m_b_ada, m_norm1_w, m_w_in, m_dn_conv_w, m_dn_A_log, m_dn_dt_bias, m_dn_norm_w, m_w_proj_sb, m_w_proj_dn, m_w_out, m_norm2_w, m_w_ffn_in, m_ffn_conv_w, m_ffn_conv_b, m_w_ffn_out, m_final_norm_w, v_w_ada, v_b_ada, v_norm1_w, v_w_in, v_dn_conv_w, v_dn_A_log, v_dn_dt_bias, v_dn_norm_w, v_w_proj_sb, v_w_proj_dn, v_w_out, v_norm2_w, v_w_ffn_in, v_ffn_conv_w, v_ffn_conv_b, v_w_ffn_out, v_final_norm_w):
    d = D_MODEL
    me = 4 * lax.axis_index("x") + 2 * lax.axis_index("y") + lax.axis_index("c")
    xs = x[0]
    target = loss_target[0]
    n_ada = w_ada.shape[2]
    n_dnc = dn_conv_w.shape[2]
    n_ffc = ffn_conv_w.shape[2]

    small = jnp.concatenate([c, dn_conv_w[0].reshape(1, -1), ffn_conv_w[0].reshape(1, -1)], axis=1)
    small = _pad_lanes(small, -(-small.shape[1] // LANES) * LANES)
    small_g, w_in_g = _all_gather([small, w_in[0].astype(bf16)], "gather_w_in")
    later = [w_proj_sb[0].astype(bf16), w_proj_dn[0].astype(bf16), w_out[0].astype(bf16),
             w_ffn_in[0].astype(bf16), w_ffn_out[0].astype(bf16)]
    gather_later = _SideComm(_gather_protocol, later, _gathered_shapes(later))
    small_g = small_g[:, 0, :]
    c_all = small_g[:, :d]
    dn_cw = _cols_from_devices(small_g[:, d:d + DN_CONV_WIDTH * n_dnc].reshape(N_DEV, DN_CONV_WIDTH, n_dnc))
    o2 = d + DN_CONV_WIDTH * n_dnc
    ffn_cw = _cols_from_devices(small_g[:, o2:o2 + FFN_CONV_WIDTH * n_ffc].reshape(N_DEV, FFN_CONV_WIDTH, n_ffc))

    w_in_full = _cols_from_devices(w_in_g)
    r_sb, r_dn, r_z = 3 * SB_WIDTH, 3 * SB_WIDTH + DN_CONV_CH, 3 * SB_WIDTH + DN_CONV_CH + DN_V_WIDTH
    r_g = r_z + 2 * DN_HEADS
    w_main = jnp.concatenate([w_in_full[:, r_g:], w_in_full[:, r_sb:r_dn], w_in_full[:, r_dn:r_z],
                              w_in_full[:, :r_sb]], axis=1)
    w_ba = _pad_lanes(w_in_full[:, r_z:r_g], LANES)

    b_loc = lax.dynamic_slice(b_ada, (0, me * n_ada), (1, n_ada))
    mod_part = _ada_fwd(c_all, w_ada[0], b_loc)
    (mod_g,) = _all_gather([mod_part], "gather_mod")
    mod = lax.dynamic_index_in_dim(mod_g, me, axis=1, keepdims=False).reshape(1, N_DEV * n_ada)
    shift1, scale1, gate1, shift2, scale2, gate2 = [mod[:, i * d:(i + 1) * d] for i in range(6)]

    logit_lanes = ((0, 0), (GDN_LOGIT_LANE, LANES - GDN_LOGIT_LANE - DN_HEADS))
    a_log = jnp.pad(dn_A_log, logit_lanes)
    dt_b = jnp.pad(dn_dt_bias, logit_lanes)

    (h,) = _stage_fwd(_f_normmod, [norm1_w, shift1, scale1], [xs], [bf16], "norm1_fwd")
    proj = _mm(h, w_main, name="in_proj")
    ba = _mm(h, w_ba, name="in_proj_ba")
    k16, k0_16, k1_16, v16, v0_16, v1_16 = _sb_prepare(proj)
    o_a, sb_runs, w_psb_g, w_pdn_g, w_out_g, w_fin_g, w_fout_g = _sb_attention_fwd2(
        proj, k16, v0_16, v1_16, side=gather_later)
    w_psb = _cols_from_devices(w_psb_g)
    w_pdn = w_pdn_g.reshape(DN_V_WIDTH, d)
    w_o = w_out_g.reshape(d, d)
    w_fin = _cols_from_devices(w_fin_g)
    w_fout = w_fout_g.reshape(D_FF, d)
    qkv_act = _dn_conv_fwd(proj, dn_cw)
    o_b, states = _gdn_fwd(a_log, dt_b, dn_norm_w, qkv_act, proj, ba)
    pa = _mm(o_a, w_psb, name="proj_sb")
    pb = _mm(o_b, w_pdn, name="proj_dn")
    gates = [(proj, d, OFF_GA // d), (proj, d, OFF_GB // d)]
    (merged,) = _stage_fwd(_f_merge, [], gates + [pa, pb], [bf16], "merge_fwd")
    ao = _mm(merged, w_o, name="out_proj")
    (x1,) = _stage_fwd(_f_residual, [gate1], [xs, ao], [f32], "resid1_fwd")
    (h2,) = _stage_fwd(_f_normmod, [norm2_w, shift2, scale2], [x1], [bf16], "norm2_fwd")
    u_pre = _mm(h2, w_fin, name="ffn_in")
    act = _ffn_conv_fwd(u_pre, ffn_cw, ffn_conv_b)
    fo = _mm(act, w_fout, name="ffn_out")

    loss_p, d_gate2, d_wf, dx2, dfo = _loss_and_grads(gate2, final_norm_w.reshape(1, d), x1, fo, target)
    dact = _mm(dfo, w_fout, tb=True, name="ffn_out_dx")
    g_w_fout = _mm(act, dfo, ta=True, name="ffn_out_dw")
    du, dbg, dbu = _ffn_conv_bwd_act(u_pre, ffn_cw, ffn_conv_b, dact)
    du_pre, d_ffn_cw = _conv_bwd(du, u_pre, 0, ffn_cw, "ffn_conv_bwd")
    dh2 = _mm(du_pre, w_fin, tb=True, name="ffn_in_dx")
    g_w_fin = _mm(h2, du_pre, ta=True, name="ffn_in_dw")
    (d_n2w, d_shift2, d_scale2), (dx1,) = _stage_bwd(
        _f_normmod, [norm2_w, shift2, scale2], [x1], [dh2], [f32], "norm2_bwd", residual=(0, dx2))
    (d_gate1,), (dao,) = _stage_bwd(_f_residual, [gate1], [xs, ao], [dx1], [None, bf16], "resid1_bwd")
    dmerged = _mm(dao, w_o, tb=True, name="out_proj_dx")
    g_w_o = _mm(merged, dao, ta=True, name="out_proj_dw")
    _, (dga, dgb, dpa, dpb) = _stage_bwd(_f_merge, [], gates + [pa, pb], [dmerged], [bf16] * 4, "merge_bwd")
    do_a = _mm(dpa, w_psb, tb=True, name="proj_sb_dx")
    g_w_psb = _mm(o_a, dpa, ta=True, name="proj_sb_dw")
    do_b = _mm(dpb, w_pdn, tb=True, name="proj_dn_dx")
    g_w_pdn = _mm(o_b, dpb, ta=True, name="proj_dn_dw")
    early = [_cols_by_device(g_w_psb).astype(bf16),
             g_w_pdn.reshape(N_DEV, DN_V_WIDTH // N_DEV, d).astype(bf16),
             g_w_o.reshape(N_DEV, d // N_DEV, d).astype(bf16),
             _cols_by_device(g_w_fin).astype(bf16),
             g_w_fout.reshape(N_DEV, D_FF // N_DEV, d).astype(bf16)]
    exchange_early = _SideComm(_exchange_protocol, early, [jax.ShapeDtypeStruct(a.shape, a.dtype) for a in early])
    dq, dk, dv, *recv_early = _sb_attention_bwd2(proj, k16, k0_16, k1_16, v16, sb_runs, do_a, side=exchange_early)
    d_alog, d_dtb, d_dnw, dqkv_act, dz, dba = _gdn_bwd(a_log, dt_b, dn_norm_w, qkv_act, proj, ba, states, do_b)
    d_conv_out = _dn_conv_bwd_act(proj, dn_cw, dqkv_act)
    d_dn_pre, d_dn_cw = _conv_bwd(d_conv_out, proj, OFF_DN // TCONV_C, dn_cw, "dn_conv_bwd")
    dproj = jnp.concatenate([dga, dgb, d_dn_pre, dz, dq, dk.astype(bf16), dv.astype(bf16)], axis=1)
    g_w_main = _mm(h, dproj, ta=True, name="in_proj_dw")
    g_w_ba = _mm(h, dba, ta=True, name="in_proj_ba_dw")
    g_w_in_full = jnp.concatenate([g_w_main[:, OFF_SBQ:], g_w_main[:, OFF_DN:OFF_Z], g_w_main[:, OFF_Z:OFF_SBQ],
                                   g_w_ba[:, :2 * DN_HEADS], g_w_main[:, :OFF_DN]], axis=1)
    w_in_parts = _cols_by_device(g_w_in_full).astype(bf16)
    exchange_w_in = _SideComm(_exchange_protocol, [w_in_parts], [jax.ShapeDtypeStruct(w_in_parts.shape, bf16)])
    dh, recv_w_in = _mm(dproj, w_main, tb=True, name="in_proj_dx", side=exchange_w_in)
    dh_ba = _mm(dba, w_ba, tb=True, name="in_proj_ba_dx")
    (d_n1w, d_shift1, d_scale1), (grad_x,) = _stage_bwd(
        _f_normmod, [norm1_w, shift1, scale1], [xs], [[dh, dh_ba]], [f32], "norm1_bwd", residual=(0, dx1))

    dmod = jnp.concatenate([d_shift1, d_scale1, d_gate1, d_shift2, d_scale2, d_gate2], axis=1)
    d_ffn_cb = jnp.concatenate([dbg, dbu], axis=1)
    small_parts = jnp.concatenate(
        [loss_p, dmod, d_n1w, d_alog, d_dtb, d_dnw, d_n2w, d_ffn_cb, d_wf,
         d_dn_cw.reshape(1, -1), d_ffn_cw.reshape(1, -1)], axis=1)
    (small_parts_g,) = _all_gather([small_parts], "gather_small_grads")
    tot = _sum_devices(small_parts_g[:, 0, :])
    offs = {}
    pos = 0
    for nm, width in (("loss", LANES), ("b_ada", 6 * d), ("norm1_w", d), ("dn_A_log", LANES), ("dn_dt_bias", LANES),
                      ("dn_norm_w", LANES), ("norm2_w", d), ("ffn_conv_b", 2 * D_FF), ("final_norm_w", d),
                      ("dn_conv_w", DN_CONV_WIDTH * DN_CONV_CH), ("ffn_conv_w", FFN_CONV_WIDTH * 2 * D_FF)):
        offs[nm] = (pos, width)
        pos += width
    seg = lambda nm: tot[:, offs[nm][0]:offs[nm][0] + offs[nm][1]]
    loss = tot[0, 0]
    g_b_ada = seg("b_ada")
    g_norm1 = seg("norm1_w")
    g_alog = seg("dn_A_log")[:, GDN_LOGIT_LANE:GDN_LOGIT_LANE + DN_HEADS]
    g_dtb = seg("dn_dt_bias")[:, GDN_LOGIT_LANE:GDN_LOGIT_LANE + DN_HEADS]
    g_dnw = seg("dn_norm_w")
    g_norm2 = seg("norm2_w")
    g_ffn_cb = seg("ffn_conv_b")
    g_fnw = seg("final_norm_w")
    g_dn_cw = lax.dynamic_slice(seg("dn_conv_w").reshape(DN_CONV_WIDTH, DN_CONV_CH), (0, me * n_dnc),
                                (DN_CONV_WIDTH, n_dnc))
    g_ffn_cw = lax.dynamic_slice(seg("ffn_conv_w").reshape(FFN_CONV_WIDTH, 2 * D_FF), (0, me * n_ffc),
                                 (FFN_CONV_WIDTH, n_ffc))

    dmod_all = small_parts_g[:, 0, offs["b_ada"][0]:offs["b_ada"][0] + 6 * d]
    g_w_ada = _ada_bwd(c_all, lax.dynamic_slice(dmod_all, (0, me * n_ada), (N_DEV, n_ada)))

    def pack(parts):
        flat = [p.reshape(1, -1) for p in parts]
        flat = [_pad_lanes(p, -(-p.shape[1] // LANES) * LANES) for p in flat]
        return jnp.concatenate(flat, axis=1), [p.shape[1] for p in flat]

    small_names_g = [g_b_ada, g_norm1, g_alog, g_dtb, g_dnw, g_norm2, g_ffn_cb, g_fnw, g_dn_cw, g_ffn_cw]
    small_w = [b_ada, norm1_w, dn_A_log, dn_dt_bias, dn_norm_w, norm2_w, ffn_conv_b, final_norm_w, dn_conv_w[0], ffn_conv_w[0]]
    small_m = [m_b_ada, m_norm1_w, m_dn_A_log, m_dn_dt_bias, m_dn_norm_w, m_norm2_w, m_ffn_conv_b, m_final_norm_w, m_dn_conv_w[0], m_ffn_conv_w[0]]
    small_v = [v_b_ada, v_norm1_w, v_dn_A_log, v_dn_dt_bias, v_dn_norm_w, v_norm2_w, v_ffn_conv_b, v_final_norm_w, v_dn_conv_w[0], v_ffn_conv_w[0]]
    pg, widths = pack(small_names_g)
    pw, _ = pack(small_w)
    pm, _ = pack(small_m)
    pv, _ = pack(small_v)
    s_delta, s_m, s_v = _adamw(pw, pg, pm, pv, "adamw_small")

    def unpack(flat):
        out, pos = [], 0
        for ref_arr, width in zip(small_w, widths):
            out.append(flat[:, pos:pos + ref_arr.size].reshape(ref_arr.shape))
            pos += width
        return out

    small_grads = [g.reshape(w_.shape) for g, w_ in zip(small_names_g, small_w)]
    small_delta, small_newm, small_newv = unpack(s_delta), unpack(s_m), unpack(s_v)

    ada_delta, ada_m, ada_v = _adamw(w_ada[0], g_w_ada, m_w_ada[0], v_w_ada[0], "adamw_ada")

    recv = [recv_w_in] + list(recv_early)
    big = {}
    for nm, parts, w_, m_, v_ in (("w_in", recv[0], w_in, m_w_in, v_w_in),
                                  ("w_proj_sb", recv[1], w_proj_sb, m_w_proj_sb, v_w_proj_sb),
                                  ("w_proj_dn", recv[2], w_proj_dn, m_w_proj_dn, v_w_proj_dn),
                                  ("w_out", recv[3], w_out, m_w_out, v_w_out),
                                  ("w_ffn_in", recv[4], w_ffn_in, m_w_ffn_in, v_w_ffn_in),
                                  ("w_ffn_out", recv[5], w_ffn_out, m_w_ffn_out, v_w_ffn_out)):
        big[nm] = [t[None] for t in _sum_adamw(parts, w_[0], m_[0], v_[0], "adamw_" + nm)]

    sg = dict(zip(["b_ada", "norm1_w", "dn_A_log", "dn_dt_bias", "dn_norm_w", "norm2_w", "ffn_conv_b", "final_norm_w",
                   "dn_conv_w", "ffn_conv_w"], range(10)))

    def small_out(table, nm):
        val = table[sg[nm]]
        return val[None] if nm in ("dn_conv_w", "ffn_conv_w") else val

    order = ["w_ada", "b_ada", "norm1_w", "w_in", "dn_conv_w", "dn_A_log", "dn_dt_bias", "dn_norm_w", "w_proj_sb",
             "w_proj_dn", "w_out", "norm2_w", "w_ffn_in", "ffn_conv_w", "ffn_conv_b", "w_ffn_out", "final_norm_w"]
    groups = []
    for k, small_table in enumerate((small_grads, small_delta, small_newm, small_newv)):
        row = []
        for nm in order:
            if nm == "w_ada":
                row.append((g_w_ada, ada_delta, ada_m, ada_v)[k][None])
            elif nm in big:
                row.append(big[nm][k])
            else:
                row.append(small_out(small_table, nm))
        groups.append(row)
    return (loss, grad_x[None], *groups[0], *groups[1], *groups[2], *groups[3])
```

```python
import functools

import jax
import jax.numpy as jnp
from jax import lax
from jax.experimental import pallas as pl
from jax.experimental.pallas import tpu as pltpu

f32 = jnp.float32
bf16 = jnp.bfloat16

D_MODEL = 1024
SB_HEADS = 8
SB_HEAD_DIM = 64
SB_WIDTH = SB_HEADS * SB_HEAD_DIM
SB_QBLOCK = 128
DN_HEADS = 8
DN_KEY_DIM = 64
DN_VAL_DIM = 128
DN_QK_WIDTH = DN_HEADS * DN_KEY_DIM
DN_V_WIDTH = DN_HEADS * DN_VAL_DIM
DN_CONV_CH = 2 * DN_QK_WIDTH + DN_V_WIDTH
DN_CONV_WIDTH = 4
DN_CHUNK = 64
D_FF = 2816
FFN_CONV_WIDTH = 3
NORM_EPS = 1e-6
L2_EPS = 1e-6
ADAM_LR = 0.001
ADAM_B1 = 0.9
ADAM_B2 = 0.999
ADAM_EPS = 1e-08
ADAM_WD = 0.01
ADAM_STEP = 10

N_DEV = 8
MESH = pl.DeviceIdType.MESH

LANES = 128
SUBLANES = 8
VMEM_LIMIT = 48 * 1024 * 1024

OFF_GA = 0
OFF_GB = D_MODEL
OFF_DN = 2 * D_MODEL
OFF_Z = OFF_DN + DN_CONV_CH
OFF_SBQ = OFF_Z + DN_V_WIDTH
OFF_SBK = OFF_SBQ + SB_WIDTH
OFF_SBV = OFF_SBK + SB_WIDTH
MAIN_WIDTH = OFF_SBV + SB_WIDTH

TM = 256
TCONV_R = 512
TCONV_C = 512
TCONV_FF = D_FF // 2
SB_PAIRS_PER_STEP = 2
SB_DEAD = -106.0
SB_NEVER = -1e30


def _cparams(sem=None):
    return pltpu.CompilerParams(dimension_semantics=sem, vmem_limit_bytes=VMEM_LIMIT)


def _pick(n, cands):
    for c in cands:
        if n % c == 0:
            return c
    return n


def _my_pos():
    return lax.axis_index("x"), lax.axis_index("y"), lax.axis_index("c")


def _flip(v, bit):
    return 1 - v if bit else v


def _comm_scratch(n):
    return [pltpu.SemaphoreType.DMA((n, 7)), pltpu.SemaphoreType.DMA((n, 7)), pltpu.SemaphoreType.DMA((n,))]


def _gather_protocol(ins, outs, send_sems, recv_sems, local_sems):
    n = len(ins)
    x, y, c = _my_pos()
    me, sibling = (x, y, c), (x, y, 1 - c)
    chips = [(1 - x, y), (x, 1 - y), (1 - x, 1 - y)]

    def slot(out, pos):
        return out.at[4 * pos[0] + 2 * pos[1] + pos[2]]

    def copy(a, k, block, to, src=None):
        return pltpu.make_async_remote_copy(
            src_ref=slot(outs[a], block) if src is None else src, dst_ref=slot(outs[a], block),
            send_sem=send_sems.at[a, k], recv_sem=recv_sems.at[a, k], device_id=to, device_id_type=MESH)

    def local(a):
        return pltpu.make_async_copy(ins[a], slot(outs[a], me), local_sems.at[a])

    def first(a):
        return [copy(a, 0, me, sibling, src=ins[a])] + [copy(a, 1 + j, me, (*chip, c), src=ins[a])
                                                         for j, chip in enumerate(chips)]

    def start():
        for a in range(n):
            local(a).start()
            for cp in first(a):
                cp.start()

    def finish():
        forwards = []
        for a in range(n):
            for j, chip in enumerate(chips):
                copy(a, 1 + j, (*chip, c), me).wait_recv()
                fwd = copy(a, 4 + j, (*chip, c), sibling)
                fwd.start()
                forwards.append(fwd)
        for a in range(n):
            copy(a, 0, sibling, me).wait_recv()
            for j, chip in enumerate(chips):
                copy(a, 4 + j, (*chip, 1 - c), me).wait_recv()
        for a in range(n):
            for cp in first(a):
                cp.wait_send()
        for cp in forwards:
            cp.wait_send()
        for a in range(n):
            local(a).wait()

    return start, finish


def _exchange_protocol(ins, outs, send_sems, recv_sems, local_sems):
    n = len(ins)
    x, y, c = _my_pos()
    me_idx = 4 * x + 2 * y + c

    def local(a):
        return pltpu.make_async_copy(ins[a].at[me_idx], outs[a].at[me_idx], local_sems.at[a])

    def copies(a, m):
        peer = (_flip(x, m & 4), _flip(y, m & 2), _flip(c, m & 1))
        peer_idx = 4 * peer[0] + 2 * peer[1] + peer[2]
        sems = dict(send_sem=send_sems.at[a, m - 1], recv_sem=recv_sems.at[a, m - 1], device_id=peer,
                    device_id_type=MESH)
        send = pltpu.make_async_remote_copy(src_ref=ins[a].at[peer_idx], dst_ref=outs[a].at[me_idx], **sems)
        recv = pltpu.make_async_remote_copy(src_ref=ins[a].at[peer_idx], dst_ref=outs[a].at[peer_idx], **sems)
        return send, recv

    def start():
        for a in range(n):
            local(a).start()
            for m in range(1, N_DEV):
                copies(a, m)[0].start()

    def finish():
        for a in range(n):
            for m in range(1, N_DEV):
                copies(a, m)[1].wait_recv()
        for a in range(n):
            for m in range(1, N_DEV):
                copies(a, m)[0].wait_send()
            local(a).wait()

    return start, finish


def _collective_call(protocol, arrs, out_shapes, name):
    n = len(arrs)

    def body(*refs):
        start, finish = protocol(refs[:n], refs[n:2 * n], *refs[2 * n:])
        start()
        finish()

    any_spec = pl.BlockSpec(memory_space=pl.ANY)
    return pl.pallas_call(body, name=name, out_shape=out_shapes, in_specs=[any_spec] * n, out_specs=[any_spec] * n,
                          scratch_shapes=_comm_scratch(n))(*arrs)


def _gathered_shapes(arrs):
    return [jax.ShapeDtypeStruct((N_DEV,) + a.shape, a.dtype) for a in arrs]


def _all_gather(arrs, name):
    return _collective_call(_gather_protocol, arrs, _gathered_shapes(arrs), name)


MM_BLOCK_BYTES = 4 * 1024 * 1024


def _mm_tiles(m_dim, n_dim, k_dim, a_bytes, b_bytes):
    tm = _pick(m_dim, (1024, 512, 256, 128))
    tn = _pick(n_dim, (1024, 512, 256, 128))
    tk = k_dim
    if k_dim % LANES == 0:
        units = k_dim // LANES
        fits = [u for u in range(1, units + 1) if units % u == 0
                and u * LANES * max(tm * a_bytes, tn * b_bytes) <= MM_BLOCK_BYTES]
        tk = max(fits) * LANES
    return tm, tn, tk


def _mm(a, b, *, ta=False, tb=False, name, side=None):
    (k_dim, m_dim) = a.shape if ta else a.shape[::-1]
    (n_dim, kb_dim) = b.shape if tb else b.shape[::-1]
    assert k_dim == kb_dim, (a.shape, b.shape, ta, tb)
    tm, tn, tk = _mm_tiles(m_dim, n_dim, k_dim, a.dtype.itemsize, b.dtype.itemsize)
    nk = k_dim // tk
    grid = (m_dim // tm, n_dim // tn, nk)
    dims = (((0 if ta else 1,), (1 if tb else 0,)), ((), ()))
    ns = side.n if side else 0

    def body(a_ref, b_ref, *rest):
        if side:
            side.run(rest[:ns], rest[ns + 1:2 * ns + 1], rest[2 * ns + 1:], *_grid_ends(grid),
                     lambda: compute(a_ref, b_ref, rest[ns]))
        else:
            compute(a_ref, b_ref, rest[0])

    def compute(a_ref, b_ref, o_ref):
        part = lax.dot_general(a_ref[...].astype(bf16), b_ref[...].astype(bf16), dims, preferred_element_type=f32)
        if nk == 1:
            o_ref[...] = part
        else:
            k = pl.program_id(2)

            @pl.when(k == 0)
            def _():
                o_ref[...] = part

            @pl.when(k > 0)
            def _():
                o_ref[...] += part

    a_spec = pl.BlockSpec((tk, tm), lambda i, j, k: (k, i)) if ta else pl.BlockSpec((tm, tk), lambda i, j, k: (i, k))
    b_spec = pl.BlockSpec((tn, tk), lambda i, j, k: (j, k)) if tb else pl.BlockSpec((tk, tn), lambda i, j, k: (k, j))
    out_spec = pl.BlockSpec((tm, tn), lambda i, j, k: (i, j))
    out_shape = jax.ShapeDtypeStruct((m_dim, n_dim), f32)
    if not side:
        return pl.pallas_call(body, name=name, grid=grid, in_specs=[a_spec, b_spec], out_specs=out_spec,
                              out_shape=out_shape,
                              compiler_params=_cparams(("parallel", "parallel", "arbitrary")))(a, b)
    return pl.pallas_call(
        body, name=name, grid=grid, in_specs=[a_spec, b_spec] + side.specs(), out_specs=[out_spec] + side.specs(),
        out_shape=[out_shape] + side.out_shapes, scratch_shapes=_comm_scratch(ns),
        compiler_params=_cparams(("arbitrary", "arbitrary", "arbitrary")))(a, b, *side.arrs)


def _win(t):
    return t if isinstance(t, tuple) else (t, t.shape[1], 0)


def _tile_spec(width, cb, tm):
    return pl.BlockSpec((tm, width), lambda i: (i, cb))


def _param_spec(p):
    return pl.BlockSpec(p.shape, lambda i: (0, 0))


def _stage_fwd(f, params, tiles, out_dtypes, name):
    tiles = [_win(t) for t in tiles]
    rows = tiles[0][0].shape[0]
    tm = min(TM, rows)
    avals = jax.eval_shape(f, *[jax.ShapeDtypeStruct(p.shape, f32) for p in params],
                           *[jax.ShapeDtypeStruct((tm, w), f32) for _, w, _ in tiles])
    n_p, n_t = len(params), len(tiles)

    def body(*refs):
        p = [r[...] for r in refs[:n_p]]
        t = [r[...].astype(f32) for r in refs[n_p:n_p + n_t]]
        for o_ref, val in zip(refs[n_p + n_t:], f(*p, *t)):
            o_ref[...] = val.astype(o_ref.dtype)

    return pl.pallas_call(
        body, name=name, grid=(rows // tm,),
        in_specs=[_param_spec(p) for p in params] + [_tile_spec(w, cb, tm) for _, w, cb in tiles],
        out_specs=[_tile_spec(a.shape[1], 0, tm) for a in avals],
        out_shape=[jax.ShapeDtypeStruct((rows, a.shape[1]), dt) for a, dt in zip(avals, out_dtypes)],
        compiler_params=_cparams(("parallel",)),
    )(*params, *[t[0] for t in tiles])


def _stage_bwd(f, params, tiles, cts, grad_dtypes, name, residual=None):
    tiles = [_win(t) for t in tiles]
    rows = tiles[0][0].shape[0]
    tm = min(TM, rows)
    cts = [list(g) if isinstance(g, (list, tuple)) else [g] for g in cts]
    flat_cts = [a for g in cts for a in g]
    n_p, n_t, n_c = len(params), len(tiles), len(flat_cts)
    has_res = residual is not None
    want = [j for j, dt in enumerate(grad_dtypes) if dt is not None]

    def body(*refs):
        i = pl.program_id(0)
        p = [r[...] for r in refs[:n_p]]
        t = [r[...].astype(f32) for r in refs[n_p:n_p + n_t]]
        ct_vals = [r[...].astype(f32) for r in refs[n_p + n_t:n_p + n_t + n_c]]
        ct, at = [], 0
        for g in cts:
            ct.append(functools.reduce(jnp.add, ct_vals[at:at + len(g)]))
            at += len(g)
        ct = tuple(ct)
        pos = n_p + n_t + n_c
        res_ref = refs[pos] if has_res else None
        pos += 1 if has_res else 0
        dp_refs = refs[pos:pos + n_p]
        dt_refs = refs[pos + n_p:]
        _, vjp = jax.vjp(f, *p, *t)
        grads = vjp(ct)

        @pl.when(i == 0)
        def _():
            for r in dp_refs:
                r[...] = jnp.zeros_like(r)

        for r, g in zip(dp_refs, grads[:n_p]):
            r[...] += g
        for r, j in zip(dt_refs, want):
            g = grads[n_p + j]
            if has_res and j == residual[0]:
                g = g + res_ref[...].astype(f32)
            r[...] = g.astype(r.dtype)

    in_arrays = list(params) + [t[0] for t in tiles] + flat_cts
    in_specs = ([_param_spec(p) for p in params] + [_tile_spec(w, cb, tm) for _, w, cb in tiles]
                + [_tile_spec(c.shape[1], 0, tm) for c in flat_cts])
    if has_res:
        in_arrays.append(residual[1])
        in_specs.append(_tile_spec(residual[1].shape[1], 0, tm))
    out_shape = ([jax.ShapeDtypeStruct(p.shape, f32) for p in params]
                 + [jax.ShapeDtypeStruct((rows, tiles[j][1]), grad_dtypes[j]) for j in want])
    out_specs = [_param_spec(p) for p in params] + [_tile_spec(tiles[j][1], 0, tm) for j in want]
    outs = pl.pallas_call(
        body, name=name, grid=(rows // tm,), in_specs=in_specs, out_specs=out_specs, out_shape=out_shape,
        compiler_params=_cparams(("arbitrary",)),
    )(*in_arrays)
    return outs[:n_p], outs[n_p:]


def _rms(x, w):
    return x * lax.rsqrt(jnp.mean(x * x, axis=-1, keepdims=True) + NORM_EPS) * w


def _f_normmod(w, shift, scale, x):
    return (_rms(x, w) * (1.0 + scale) + shift,)


def _f_merge(ga, gb, pa, pb):
    return (jax.nn.sigmoid(ga) * pa + jax.nn.sigmoid(gb) * pb,)


def _f_residual(gate, x, branch):
    return (x + gate * branch,)


def _f_loss(gate, wf, x1, fo, target):
    y = _rms(x1 + gate * fo, wf)
    err = jnp.square(y - target)
    return (0.5 * jnp.sum(jnp.mean(err, axis=-1, keepdims=True), axis=0, keepdims=True),)


def _loss_and_grads(gate2, wf, x1, fo, target):
    rows, d = x1.shape
    tm = min(TM, rows)

    def body(g_ref, w_ref, x_ref, fo_ref, t_ref, loss_ref, dg_ref, dw_ref, dx_ref, dfo_ref):
        i = pl.program_id(0)
        (val,), vjp = jax.vjp(_f_loss, g_ref[...], w_ref[...], x_ref[...], fo_ref[...], t_ref[...])
        dg, dw, dx, dfo, _ = vjp((jnp.ones((1, 1), f32),))

        @pl.when(i == 0)
        def _():
            loss_ref[...] = jnp.zeros_like(loss_ref)
            dg_ref[...] = jnp.zeros_like(dg_ref)
            dw_ref[...] = jnp.zeros_like(dw_ref)

        loss_ref[...] += jnp.broadcast_to(val, loss_ref.shape)
        dg_ref[...] += dg
        dw_ref[...] += dw
        dx_ref[...] = dx
        dfo_ref[...] = dfo.astype(bf16)

    vec = pl.BlockSpec((1, d), lambda i: (0, 0))
    tile = pl.BlockSpec((tm, d), lambda i: (i, 0))
    return pl.pallas_call(
        body, name="loss_fwd_bwd", grid=(rows // tm,),
        in_specs=[vec, vec, tile, tile, tile],
        out_specs=[pl.BlockSpec((1, LANES), lambda i: (0, 0)), vec, vec, tile, tile],
        out_shape=[jax.ShapeDtypeStruct((1, LANES), f32), jax.ShapeDtypeStruct((1, d), f32),
                   jax.ShapeDtypeStruct((1, d), f32), jax.ShapeDtypeStruct((rows, d), f32),
                   jax.ShapeDtypeStruct((rows, d), bf16)],
        compiler_params=_cparams(("arbitrary",)),
    )(gate2, wf, x1, fo, target)


def _softplus(z):
    return jnp.maximum(z, 0.0) + jnp.log(1.0 + jnp.exp(-jnp.abs(z)))


def _split_dot(a, m):
    hi = a.astype(bf16)
    lo = (a - hi.astype(f32)).astype(bf16)
    return jnp.dot(hi, m, preferred_element_type=f32) + jnp.dot(lo, m, preferred_element_type=f32)


def _suffix_matrix(n):
    r = lax.broadcasted_iota(jnp.int32, (n, n), 0)
    c = lax.broadcasted_iota(jnp.int32, (n, n), 1)
    return (r > c).astype(bf16)


def _head_masks():
    lane = lax.broadcasted_iota(jnp.int32, (1, LANES), 1)
    return [(lane < SB_HEAD_DIM).astype(f32), (lane >= SB_HEAD_DIM).astype(f32)]


def _sb_prepare(proj):
    def f(k, v):
        lane = lax.broadcasted_iota(jnp.int32, (1, SB_WIDTH), 1)
        m0 = (jnp.bitwise_and(lane, LANES - 1) < SB_HEAD_DIM).astype(f32)
        m1 = 1.0 - m0
        return k, k * m0, k * m1, v, v * m0, v * m1

    wins = [(proj, SB_WIDTH, OFF_SBK // SB_WIDTH), (proj, SB_WIDTH, OFF_SBV // SB_WIDTH)]
    return _stage_fwd(f, [], wins, [bf16] * 6, "sb_prepare")


def _stack_heads(x):
    m0, m1 = _head_masks()
    return jnp.concatenate([x * m0, x * m1], axis=0)


def _sb_logits(qst, k, t_pos2, kb, bq, masked):
    z = lax.dot_general(qst, k, (((1,), (1,)), ((), ())), preferred_element_type=f32)
    l = -_softplus(z)
    if masked:
        s_pos = kb * bq + lax.broadcasted_iota(jnp.int32, (1, bq), 1)
        causal = s_pos < t_pos2
        l = jnp.where(causal, l, 0.0)
    else:
        causal = None
    return z, l, causal


class _SideComm:
    def __init__(self, protocol, arrs, out_shapes):
        self.protocol, self.arrs, self.out_shapes = protocol, list(arrs), list(out_shapes)
        self.n = len(self.arrs)

    def specs(self):
        return [pl.BlockSpec(memory_space=pl.ANY)] * self.n

    def run(self, in_refs, out_refs, sems, first, last, compute):
        start, finish = self.protocol(in_refs, out_refs, *sems)
        pl.when(first)(start)
        compute()
        pl.when(last)(finish)


def _grid_ends(grid):
    ids = [pl.program_id(axis) for axis in range(len(grid))]
    first = functools.reduce(jnp.logical_and, [i == 0 for i in ids])
    last = functools.reduce(jnp.logical_and, [i == g - 1 for i, g in zip(ids, grid)])
    return first, last


def _sb_attention_fwd2(proj, k16, v0_16, v1_16, side=None):
    rows = proj.shape[0]
    bq = SB_QBLOCK
    nq = rows // bq
    assert nq <= LANES, "one lane per key block"
    npair = SB_WIDTH // LANES
    scale = SB_HEAD_DIM ** -0.5

    npp = SB_PAIRS_PER_STEP
    wq = npp * LANES
    grid = (npair // npp, nq)
    ns = side.n if side else 0

    def body(q_ref, k_ref, v0_ref, v1_ref, *rest):
        o_ref, runs_ref = rest[ns], rest[ns + 1]
        if side:
            side.run(rest[:ns], rest[ns + 2:2 * ns + 2], rest[2 * ns + 2:], *_grid_ends(grid),
                     lambda: compute(q_ref, k_ref, v0_ref, v1_ref, o_ref, runs_ref))
        else:
            compute(q_ref, k_ref, v0_ref, v1_ref, o_ref, runs_ref)

    def compute(q_ref, k_ref, v0_ref, v1_ref, o_ref, runs_ref):
        qi = pl.program_id(1)
        pairs = [slice(pp * LANES, (pp + 1) * LANES) for pp in range(npp)]
        qst = [(_stack_heads(q_ref[:, s]) * scale).astype(bf16) for s in pairs]
        r = lax.broadcasted_iota(jnp.int32, (bq, 2 * bq), 0)
        c = lax.broadcasted_iota(jnp.int32, (bq, 2 * bq), 1)
        m2 = jnp.logical_or(r > c, c >= bq).astype(bf16)
        t_pos = qi * bq + lax.broadcasted_iota(jnp.int32, (bq, 1), 0)
        t_pos2 = jnp.concatenate([t_pos, t_pos], axis=0)
        lane = lax.broadcasted_iota(jnp.int32, (1, LANES), 1)
        runs_ref[...] = jnp.full(runs_ref.shape, SB_NEVER, f32)

        def tiles(kbs, carry, masked):
            jobs = [(pp, kb) for kb in kbs for pp in range(npp)]
            rows_k = [pl.ds(pl.multiple_of(kb * bq, bq), bq) for _, kb in jobs]
            zl = [_sb_logits(qst[pp], k_ref[rk, pairs[pp]], t_pos2, kb, bq, masked) for (pp, kb), rk in zip(jobs, rows_k)]
            cs = [_split_dot(l, m2) for _, l, _ in zl]
            run = [cr[0] for cr in carry]
            acc = [cr[1] for cr in carry]
            probs = []
            for (pp, kb), (z, l, causal), cs2 in zip(jobs, zl, cs):
                a = jnp.exp(z + l + cs2[:, :bq] + run[pp])
                if masked:
                    a = jnp.where(causal, a, 0.0)
                probs.append(a.astype(bf16))
                for hh in range(2):
                    cols = slice((2 * pp + hh) * LANES, (2 * pp + hh + 1) * LANES)
                    runs_ref[:, cols] = jnp.where(lane == kb, run[pp][hh * bq:(hh + 1) * bq], runs_ref[:, cols])
                run[pp] = run[pp] + cs2[:, bq:]
            for (pp, kb), rk, ab in zip(jobs, rows_k, probs):
                acc[pp] = (acc[pp] + jnp.dot(ab[:bq], v0_ref[rk, pairs[pp]], preferred_element_type=f32)
                           + jnp.dot(ab[bq:], v1_ref[rk, pairs[pp]], preferred_element_type=f32))
            return tuple(zip(run, acc))

        zero = (jnp.zeros((2 * bq, bq), f32), jnp.zeros((bq, LANES), f32))
        carry = tiles([qi], (zero,) * npp, True)

        def alive(cr):
            return functools.reduce(jnp.maximum, [jnp.max(run) for run, _ in cr]) > SB_DEAD

        def two(state):
            i, _, cr = state
            cr = tiles([qi - 1 - 2 * i, qi - 2 - 2 * i], cr, False)
            return i + 1, alive(cr), cr

        n_two = qi // 2
        i_end, still, carry = lax.while_loop(lambda st: jnp.logical_and(st[0] < n_two, st[1]), two,
                                             (jnp.int32(0), alive(carry), carry))
        last_one = jnp.logical_and(qi % 2 == 1, jnp.logical_and(still, i_end == n_two))
        carry = lax.cond(last_one, lambda cr: tiles([0], cr, False), lambda cr: cr, carry)
        for pp in range(npp):
            o_ref[:, pairs[pp]] = carry[pp][1]

    kv = pl.BlockSpec((rows, wq), lambda p, i: (0, p))
    return pl.pallas_call(
        body, name="sb_attn_fwd", grid=grid,
        in_specs=[pl.BlockSpec((bq, wq), lambda p, i: (i, OFF_SBQ // wq + p)), kv, kv, kv] + (side.specs() if side else []),
        out_specs=[pl.BlockSpec((bq, wq), lambda p, i: (i, p)),
                   pl.BlockSpec((bq, 2 * wq), lambda p, i: (i, p))] + (side.specs() if side else []),
        out_shape=[jax.ShapeDtypeStruct((rows, SB_WIDTH), f32),
                   jax.ShapeDtypeStruct((rows, SB_HEADS * LANES), f32)] + (side.out_shapes if side else []),
        scratch_shapes=_comm_scratch(ns) if side else [],
        compiler_params=_cparams(("arbitrary", "arbitrary")),
    )(proj, k16, v0_16, v1_16, *(side.arrs if side else []))


def _sb_attention_bwd2(proj, k16, k0_16, k1_16, v16, runs, do, side=None):
    rows = proj.shape[0]
    bq = SB_QBLOCK
    nq = rows // bq
    npair = SB_WIDTH // LANES
    scale = SB_HEAD_DIM ** -0.5
    tn = (((0,), (0,)), ((), ()))
    nt = (((1,), (1,)), ((), ()))

    npp = SB_PAIRS_PER_STEP
    wq = npp * LANES
    grid = (npair // npp, nq)
    ns = side.n if side else 0

    def body(q_ref, k_ref, k0_ref, k1_ref, v_ref, runs_ref, do_ref, *rest):
        outs = rest[ns:ns + 3]
        ins = (q_ref, k_ref, k0_ref, k1_ref, v_ref, runs_ref, do_ref)
        if side:
            side.run(rest[:ns], rest[ns + 3:2 * ns + 3], rest[2 * ns + 3:], *_grid_ends(grid),
                     lambda: compute(*ins, *outs))
        else:
            compute(*ins, *outs)

    def compute(q_ref, k_ref, k0_ref, k1_ref, v_ref, runs_ref, do_ref, dq_ref, dk_ref, dv_ref):
        qi = pl.program_id(1)

        @pl.when(qi == 0)
        def _():
            dk_ref[...] = jnp.zeros_like(dk_ref)
            dv_ref[...] = jnp.zeros_like(dv_ref)

        pairs = [slice(pp * LANES, (pp + 1) * LANES) for pp in range(npp)]
        qst = [(_stack_heads(q_ref[:, s]) * scale).astype(bf16) for s in pairs]
        dost = [_stack_heads(do_ref[:, s]).astype(bf16) for s in pairs]
        runs = [jnp.concatenate([runs_ref[:, 2 * pp * LANES:(2 * pp + 1) * LANES],
                                 runs_ref[:, (2 * pp + 1) * LANES:(2 * pp + 2) * LANES]], axis=0) for pp in range(npp)]
        r = lax.broadcasted_iota(jnp.int32, (bq, 2 * bq), 0)
        c = lax.broadcasted_iota(jnp.int32, (bq, 2 * bq), 1)
        suffix_m = _suffix_matrix(bq)
        m2 = jnp.logical_or(r < c, c >= bq).astype(bf16)
        t_pos = qi * bq + lax.broadcasted_iota(jnp.int32, (bq, 1), 0)
        t_pos2 = jnp.concatenate([t_pos, t_pos], axis=0)
        lane = lax.broadcasted_iota(jnp.int32, (1, LANES), 1)

        def tiles(kbs, carry, masked):
            jobs = [(pp, kb) for kb in kbs for pp in range(npp)]
            rows_k = [pl.ds(pl.multiple_of(kb * bq, bq), bq) for _, kb in jobs]
            zl = [_sb_logits(qst[pp], k_ref[rk, pairs[pp]], t_pos2, kb, bq, masked) for (pp, kb), rk in zip(jobs, rows_k)]
            das = [lax.dot_general(dost[pp], v_ref[rk, pairs[pp]], nt, preferred_element_type=f32)
                   for (pp, kb), rk in zip(jobs, rows_k)]
            sticks = [_split_dot(l, suffix_m) for _, l, _ in zl]
            probs, ps = [], []
            for (pp, kb), (z, l, causal), stick, da in zip(jobs, zl, sticks, das):
                run = jnp.sum(jnp.where(lane == kb, runs[pp], 0.0), axis=1, keepdims=True)
                a = jnp.exp(z + l + stick + run)
                if masked:
                    a = jnp.where(causal, a, 0.0)
                probs.append(a.astype(bf16))
                ps.append(da * a)
            pcs = [_split_dot(p, m2) for p in ps]
            pref = [cr[0] for cr in carry]
            dq_acc = [cr[1] for cr in carry]
            dzs = []
            for (pp, kb), (z, l, causal), p, pc2 in zip(jobs, zl, ps, pcs):
                dz = p * jnp.exp(l) - jnp.exp(z + l) * (pc2[:, :bq] + pref[pp])
                if masked:
                    dz = jnp.where(causal, dz, 0.0)
                dzs.append(dz.astype(bf16))
                pref[pp] = pref[pp] + pc2[:, bq:]
            for (pp, kb), rk, dzb, ab in zip(jobs, rows_k, dzs, probs):
                cols = pairs[pp]
                dq_acc[pp] = (dq_acc[pp] + jnp.dot(dzb[:bq], k0_ref[rk, cols], preferred_element_type=f32)
                              + jnp.dot(dzb[bq:], k1_ref[rk, cols], preferred_element_type=f32))
                dk_ref[rk, cols] += lax.dot_general(dzb, qst[pp], tn, preferred_element_type=f32)
                dv_ref[rk, cols] += lax.dot_general(ab, dost[pp], tn, preferred_element_type=f32)
            return tuple(zip(pref, dq_acc))

        zero = (jnp.zeros((2 * bq, bq), f32), jnp.zeros((bq, LANES), f32))
        colmax = functools.reduce(jnp.maximum, [jnp.max(x, axis=0, keepdims=True) for x in runs])
        live = jnp.logical_and(colmax > SB_DEAD, lane < qi)
        kb0 = jnp.minimum(jnp.min(jnp.where(live, lane, LANES)), qi)
        n_blocks = qi - kb0
        carry = lax.fori_loop(0, n_blocks // 2, lambda i, cr: tiles([kb0 + 2 * i, kb0 + 2 * i + 1], cr, False),
                              (zero,) * npp)
        carry = lax.cond(n_blocks % 2 == 1, lambda cr: tiles([qi - 1], cr, False), lambda cr: cr, carry)
        carry = tiles([qi], carry, True)
        for pp in range(npp):
            dq_ref[:, pairs[pp]] = (carry[pp][1] * scale).astype(dq_ref.dtype)

    blk = pl.BlockSpec((bq, wq), lambda p, i: (i, p))
    full = pl.BlockSpec((rows, wq), lambda p, i: (0, p))
    return pl.pallas_call(
        body, name="sb_attn_bwd", grid=grid,
        in_specs=[pl.BlockSpec((bq, wq), lambda p, i: (i, OFF_SBQ // wq + p)), full, full, full, full,
                  pl.BlockSpec((bq, 2 * wq), lambda p, i: (i, p)), blk] + (side.specs() if side else []),
        out_specs=[blk, full, full] + (side.specs() if side else []),
        out_shape=[jax.ShapeDtypeStruct((rows, SB_WIDTH), bf16), jax.ShapeDtypeStruct((rows, SB_WIDTH), f32),
                   jax.ShapeDtypeStruct((rows, SB_WIDTH), f32)] + (side.out_shapes if side else []),
        scratch_shapes=_comm_scratch(ns) if side else [],
        compiler_params=_cparams(("arbitrary", "arbitrary")),
    )(proj, k16, k0_16, k1_16, v16, runs, do, *(side.arrs if side else []))


def _shift_down(x, prev8, j):
    if j == 0:
        return x
    r = pltpu.roll(x, j, axis=0)
    row8 = lax.broadcasted_iota(jnp.int32, prev8.shape, 0)
    head = jnp.where(row8 < j, pltpu.roll(prev8, j, axis=0), r[0:SUBLANES])
    return jnp.concatenate([head, r[SUBLANES:]], axis=0)


def _shift_up(x, next8, j):
    if j == 0:
        return x
    n = x.shape[0]
    r = pltpu.roll(x, n - j, axis=0)
    row8 = lax.broadcasted_iota(jnp.int32, next8.shape, 0)
    tail = jnp.where(row8 >= SUBLANES - j, pltpu.roll(next8, SUBLANES - j, axis=0), r[n - SUBLANES:n])
    return jnp.concatenate([r[:n - SUBLANES], tail], axis=0)


def _conv(x, prev8, w):
    k_taps = w.shape[0]
    out = x * w[k_taps - 1:k_taps, :]
    for j in range(1, k_taps):
        out = out + _shift_down(x, prev8, j) * w[k_taps - 1 - j:k_taps - j, :]
    return out


def _conv_tiles(rows, tr_max=TCONV_R):
    tr = min(tr_max, rows)
    return tr, rows // tr, tr // SUBLANES


def _prev_spec(tc, cb0, r8):
    return pl.BlockSpec((SUBLANES, tc), lambda j, i: (jnp.maximum(i * r8 - 1, 0), cb0 + j))


def _silu(x):
    return x * jax.nn.sigmoid(x)


def _dsilu(x):
    s = jax.nn.sigmoid(x)
    return s * (1.0 + x * (1.0 - s))


def _dn_conv_fwd(proj, w):
    rows = proj.shape[0]
    tr, nr, r8 = _conv_tiles(rows)
    tc = TCONV_C
    cb0 = OFF_DN // tc

    def body(x_ref, p_ref, w_ref, o_ref):
        prev = jnp.where(pl.program_id(1) == 0, 0.0, p_ref[...])
        o_ref[...] = _silu(_conv(x_ref[...], prev, w_ref[...]))

    return pl.pallas_call(
        body, name="dn_conv_fwd", grid=(DN_CONV_CH // tc, nr),
        in_specs=[pl.BlockSpec((tr, tc), lambda j, i: (i, cb0 + j)), _prev_spec(tc, cb0, r8),
                  pl.BlockSpec((DN_CONV_WIDTH, tc), lambda j, i: (0, j))],
        out_specs=pl.BlockSpec((tr, tc), lambda j, i: (i, j)),
        out_shape=jax.ShapeDtypeStruct((rows, DN_CONV_CH), f32),
        compiler_params=_cparams(("parallel", "parallel")),
    )(proj, proj, w)


def _dn_conv_bwd_act(proj, w, dact):
    rows = proj.shape[0]
    tr, nr, r8 = _conv_tiles(rows)
    tc = TCONV_C
    cb0 = OFF_DN // tc

    def body(x_ref, p_ref, w_ref, d_ref, o_ref):
        prev = jnp.where(pl.program_id(1) == 0, 0.0, p_ref[...])
        o_ref[...] = d_ref[...] * _dsilu(_conv(x_ref[...], prev, w_ref[...]))

    return pl.pallas_call(
        body, name="dn_conv_bwd_act", grid=(DN_CONV_CH // tc, nr),
        in_specs=[pl.BlockSpec((tr, tc), lambda j, i: (i, cb0 + j)), _prev_spec(tc, cb0, r8),
                  pl.BlockSpec((DN_CONV_WIDTH, tc), lambda j, i: (0, j)),
                  pl.BlockSpec((tr, tc), lambda j, i: (i, j))],
        out_specs=pl.BlockSpec((tr, tc), lambda j, i: (i, j)),
        out_shape=jax.ShapeDtypeStruct((rows, DN_CONV_CH), f32),
        compiler_params=_cparams(("parallel", "parallel")),
    )(proj, proj, w, dact)


def _ffn_conv_fwd(u_pre, w, b):
    rows = u_pre.shape[0]
    tr, nr, r8 = _conv_tiles(rows, TCONV_R // 2)
    tc = TCONV_FF
    nct = D_FF // tc

    def body(xg_ref, pg_ref, xu_ref, pu_ref, wg_ref, wu_ref, bg_ref, bu_ref, o_ref):
        first = pl.program_id(1) == 0
        ug = _conv(xg_ref[...], jnp.where(first, 0.0, pg_ref[...]), wg_ref[...]) + bg_ref[...]
        uu = _conv(xu_ref[...], jnp.where(first, 0.0, pu_ref[...]), wu_ref[...]) + bu_ref[...]
        o_ref[...] = (_silu(ug) * uu).astype(o_ref.dtype)

    def x_spec(off):
        return pl.BlockSpec((tr, tc), lambda j, i: (i, off + j))

    def w_spec(k, off):
        return pl.BlockSpec((k, tc), lambda j, i: (0, off + j))

    return pl.pallas_call(
        body, name="ffn_conv_fwd", grid=(nct, nr),
        in_specs=[x_spec(0), _prev_spec(tc, 0, r8), x_spec(nct), _prev_spec(tc, nct, r8),
                  w_spec(FFN_CONV_WIDTH, 0), w_spec(FFN_CONV_WIDTH, nct), w_spec(1, 0), w_spec(1, nct)],
        out_specs=pl.BlockSpec((tr, tc), lambda j, i: (i, j)),
        out_shape=jax.ShapeDtypeStruct((rows, D_FF), bf16),
        compiler_params=_cparams(("parallel", "parallel")),
    )(u_pre, u_pre, u_pre, u_pre, w, w, b, b)


def _ffn_conv_bwd_act(u_pre, w, b, dact):
    rows = u_pre.shape[0]
    tr, nr, r8 = _conv_tiles(rows, TCONV_R // 2)
    tc = TCONV_FF
    nct = D_FF // tc

    def body(xg_ref, pg_ref, xu_ref, pu_ref, wg_ref, wu_ref, bg_ref, bu_ref, d_ref,
             du_ref, dbg_ref, dbu_ref):
        i = pl.program_id(1)
        first = i == 0
        ug = _conv(xg_ref[...], jnp.where(first, 0.0, pg_ref[...]), wg_ref[...]) + bg_ref[...]
        uu = _conv(xu_ref[...], jnp.where(first, 0.0, pu_ref[...]), wu_ref[...]) + bu_ref[...]
        d = d_ref[...]
        dug = d * uu * _dsilu(ug)
        duu = d * _silu(ug)
        du_ref[0] = dug
        du_ref[1] = duu

        @pl.when(first)
        def _():
            dbg_ref[...] = jnp.zeros_like(dbg_ref)
            dbu_ref[...] = jnp.zeros_like(dbu_ref)

        dbg_ref[...] += jnp.sum(dug, axis=0, keepdims=True)
        dbu_ref[...] += jnp.sum(duu, axis=0, keepdims=True)

    def x_spec(off):
        return pl.BlockSpec((tr, tc), lambda j, i: (i, off + j))

    def w_spec(k, off):
        return pl.BlockSpec((k, tc), lambda j, i: (0, off + j))

    tile = pl.BlockSpec((tr, tc), lambda j, i: (i, j))
    vec = pl.BlockSpec((1, tc), lambda j, i: (0, j))
    return pl.pallas_call(
        body, name="ffn_conv_bwd_act", grid=(nct, nr),
        in_specs=[x_spec(0), _prev_spec(tc, 0, r8), x_spec(nct), _prev_spec(tc, nct, r8),
                  w_spec(FFN_CONV_WIDTH, 0), w_spec(FFN_CONV_WIDTH, nct), w_spec(1, 0), w_spec(1, nct), tile],
        out_specs=[pl.BlockSpec((2, tr, tc), lambda j, i: (0, i, j)), vec, vec],
        out_shape=[jax.ShapeDtypeStruct((2, rows, D_FF), f32),
                   jax.ShapeDtypeStruct((1, D_FF), f32), jax.ShapeDtypeStruct((1, D_FF), f32)],
        compiler_params=_cparams(("parallel", "arbitrary")),
    )(u_pre, u_pre, u_pre, u_pre, w, w, b, b, dact)


def _conv_bwd(dy, x, x_cb0, w, name):
    k_taps = w.shape[0]
    split = dy.ndim == 3
    rows = dy.shape[-2]
    ch = dy.shape[-1] * (2 if split else 1)
    tc = TCONV_FF if split else TCONV_C
    tr, nr, r8 = _conv_tiles(rows, TCONV_R // 2 if split else TCONV_R)
    per_half = dy.shape[-1] // tc
    last8 = rows // SUBLANES - 1

    def body(dy_ref, nx_ref, x_ref, p_ref, w_ref, dx_ref, dw_ref):
        i = pl.program_id(1)
        dyv = dy_ref[...]
        nxt = jnp.where(i == nr - 1, 0.0, nx_ref[...])
        prev = jnp.where(i == 0, 0.0, p_ref[...])
        xv = x_ref[...].astype(f32)
        wv = w_ref[...]

        @pl.when(i == 0)
        def _():
            dw_ref[...] = jnp.zeros_like(dw_ref)

        dx = dyv * wv[k_taps - 1:k_taps, :]
        dw_ref[k_taps - 1:k_taps, :] += jnp.sum(dyv * xv, axis=0, keepdims=True)
        for j in range(1, k_taps):
            dx = dx + _shift_up(dyv, nxt, j) * wv[k_taps - 1 - j:k_taps - j, :]
            dw_ref[k_taps - 1 - j:k_taps - j, :] += jnp.sum(dyv * _shift_down(xv, prev, j), axis=0, keepdims=True)
        dx_ref[...] = dx.astype(dx_ref.dtype)

    tile = pl.BlockSpec((tr, tc), lambda j, i: (i, j))
    if split:
        dy_spec = pl.BlockSpec((None, tr, tc), lambda j, i: (j // per_half, i, j % per_half))
        next_spec = pl.BlockSpec((None, SUBLANES, tc),
                                 lambda j, i: (j // per_half, jnp.minimum((i + 1) * r8, last8), j % per_half))
    else:
        dy_spec = tile
        next_spec = pl.BlockSpec((SUBLANES, tc), lambda j, i: (jnp.minimum((i + 1) * r8, last8), j))
    return pl.pallas_call(
        body, name=name, grid=(ch // tc, nr),
        in_specs=[dy_spec, next_spec,
                  pl.BlockSpec((tr, tc), lambda j, i: (i, x_cb0 + j)), _prev_spec(tc, x_cb0, r8),
                  pl.BlockSpec((k_taps, tc), lambda j, i: (0, j))],
        out_specs=[tile, pl.BlockSpec((k_taps, tc), lambda j, i: (0, j))],
        out_shape=[jax.ShapeDtypeStruct((rows, ch), bf16), jax.ShapeDtypeStruct((k_taps, ch), f32)],
        compiler_params=_cparams(("parallel", "arbitrary")),
    )(dy, dy, x, x, w)


def _hdot(a, b):
    return jnp.dot(a, b, preferred_element_type=f32, precision=lax.Precision.HIGH)


def _xdot(a, b):
    return jnp.dot(a, b, preferred_element_type=f32, precision=lax.Precision.HIGHEST)


def _bdot(a, b):
    return jnp.dot(a.astype(bf16), b.astype(bf16), preferred_element_type=f32)


def _bdot_nt(a, b):
    return lax.dot_general(a.astype(bf16), b.astype(bf16), (((1,), (1,)), ((), ())), preferred_element_type=f32)


def _bdot_tn(a, b):
    return lax.dot_general(a.astype(bf16), b.astype(bf16), (((0,), (0,)), ((), ())), preferred_element_type=f32)


GDN_GROUP = 4
GDN_NGROUPS = DN_HEADS // GDN_GROUP
GDN_ROWS = GDN_GROUP * DN_CHUNK
GDN_QK_LANES = GDN_GROUP * DN_KEY_DIM
GDN_LOGIT_LANE = DN_HEADS


def _inverse_impl(lows):
    n = lows[0].shape[0]
    r = lax.broadcasted_iota(jnp.int32, (n, n), 0)
    c = lax.broadcasted_iota(jnp.int32, (n, n), 1)
    eye = (r == c).astype(f32)
    blk = jnp.right_shift(r, 3) == jnp.right_shift(c, 3)
    d = [jnp.where(blk, low, 0.0) for low in lows]
    e = [low - x for low, x in zip(lows, d)]

    def nilpotent8_inverse(xs):
        acc = [eye - x for x in xs]
        power = xs
        for _ in range(2):
            power = [_bdot(x, x) for x in power]
            acc = [_bdot(a, eye + x) for a, x in zip(acc, power)]
        return acc

    dinv = nilpotent8_inverse(d)
    ninv = nilpotent8_inverse([_bdot(x, y) for x, y in zip(dinv, e)])
    t = [_bdot(x, y) for x, y in zip(ninv, dinv)]
    for _ in range(2):
        res = [eye - x - _hdot(low, x) for low, x in zip(lows, t)]
        t = [x + _bdot(x, y) for x, y in zip(t, res)]
    return tuple(t)


@jax.custom_vjp
def _unit_lower_inverses(lows):
    return _inverse_impl(lows)


def _unit_lower_inverses_fwd(lows):
    t = _inverse_impl(lows)
    return t, t


def _unit_lower_inverses_bwd(t, ct):
    tn = (((0,), (0,)), ((), ()))
    nt = (((1,), (1,)), ((), ()))
    left = [lax.dot_general(x, g, tn, preferred_element_type=f32, precision=lax.Precision.HIGH) for x, g in zip(t, ct)]
    return (tuple(-lax.dot_general(x, y, nt, preferred_element_type=f32, precision=lax.Precision.HIGH)
                  for x, y in zip(left, t)),)


_unit_lower_inverses.defvjp(_unit_lower_inverses_fwd, _unit_lower_inverses_bwd)


@jax.custom_vjp
def _known_inverses(lows, t):
    return t


def _known_inverses_fwd(lows, t):
    return t, t


def _known_inverses_bwd(t, ct):
    return _unit_lower_inverses_bwd(t, ct) + (tuple(jnp.zeros_like(x) for x in t),)


_known_inverses.defvjp(_known_inverses_fwd, _known_inverses_bwd)


def _gdn_chunk(a_log, dt_bias, norm_w, ba, *per_group, inverses=None, keep_inverses=False):
    ng = GDN_NGROUPS
    qgs, kgs, vsts, zsts, states = [per_group[i * ng:(i + 1) * ng] for i in range(5)]
    groups = range(ng)
    n = GDN_ROWS
    r = lax.broadcasted_iota(jnp.int32, (n, n), 0)
    c = lax.broadcasted_iota(jnp.int32, (n, n), 1)
    same_head = jnp.right_shift(r, 6) == jnp.right_shift(c, 6)
    incl = jnp.logical_and(same_head, r >= c)
    strict = jnp.logical_and(same_head, r > c)
    eye = (r == c).astype(f32)
    ones = jnp.ones((n, n), f32)
    own_lanes = same_head.astype(f32)
    lane = lax.broadcasted_iota(jnp.int32, (1, LANES), 1)
    pick = lambda arr, idx: jnp.sum(jnp.where(lane == idx, arr, 0.0), axis=1, keepdims=True)
    heads = [[GDN_GROUP * g + h for h in range(GDN_GROUP)] for g in groups]
    rc = lax.broadcasted_iota(jnp.int32, (DN_CHUNK, DN_CHUNK), 0)
    cc = lax.broadcasted_iota(jnp.int32, (DN_CHUNK, DN_CHUNK), 1)

    g_all = -jnp.exp(a_log) * _softplus(ba + dt_bias)
    gc_all = _xdot((rc >= cc).astype(f32), g_all)
    gl_all = jnp.sum(g_all, axis=0, keepdims=True)
    beta = [jnp.concatenate([jax.nn.sigmoid(pick(ba, hd)) for hd in heads[g]], axis=0) for g in groups]
    gc = [jnp.concatenate([pick(gc_all, GDN_LOGIT_LANE + hd) for hd in heads[g]], axis=0) for g in groups]
    g_last = [jnp.concatenate([jnp.broadcast_to(pick(gl_all, GDN_LOGIT_LANE + hd), (DN_CHUNK, 1)) for hd in heads[g]],
                              axis=0) for g in groups]
    gr = [jnp.broadcast_to(gc[g], (n, n)).T for g in groups]
    decay = [jnp.where(incl, jnp.exp(jnp.where(incl, gc[g] - gr[g], 0.0)), 0.0) for g in groups]
    q = [jnp.concatenate([qgs[g]] * GDN_GROUP, axis=0) * own_lanes for g in groups]
    k = [jnp.concatenate([kgs[g]] * GDN_GROUP, axis=0) * own_lanes for g in groups]
    qn = [x * lax.rsqrt(jnp.sum(x * x, axis=1, keepdims=True) + L2_EPS) * (DN_KEY_DIM ** -0.5) for x in q]
    kn = [x * lax.rsqrt(jnp.sum(x * x, axis=1, keepdims=True) + L2_EPS) for x in k]
    kb = [kn[g] * beta[g] for g in groups]
    low = [jnp.where(strict, _bdot_nt(kb[g], kn[g]) * decay[g], 0.0) for g in groups]
    intra = [jnp.where(incl, _bdot_nt(qn[g], kn[g]) * decay[g], 0.0) for g in groups]
    t = _unit_lower_inverses(tuple(low)) if inverses is None else _known_inverses(tuple(low), tuple(inverses))
    u = [_bdot(t[g], vsts[g] * beta[g]) for g in groups]
    w = [_bdot(t[g], kb[g] * jnp.exp(gc[g])) for g in groups]
    sb = [s.astype(bf16) for s in states]
    v_new = [u[g] - jnp.dot(w[g].astype(bf16), sb[g], preferred_element_type=f32) for g in groups]
    o = [jnp.dot((qn[g] * jnp.exp(gc[g])).astype(bf16), sb[g], preferred_element_type=f32) for g in groups]
    o = [o[g] + _bdot(intra[g], v_new[g]) for g in groups]
    new_state = [states[g] * jnp.exp(g_last[g]) + _bdot_tn(kn[g] * jnp.exp(g_last[g] - gc[g]), v_new[g])
                 for g in groups]
    o_n = [x * lax.rsqrt(jnp.mean(x * x, axis=1, keepdims=True) + NORM_EPS) * norm_w for x in o]
    return tuple(o_n[g] * _silu(zsts[g]) for g in groups) + tuple(new_state) + (tuple(t) if keep_inverses else ())


def _gdn_specs(rows, reverse):
    n = rows // DN_CHUNK
    idx = (lambda i: n - 1 - i) if reverse else (lambda i: i)
    vec = pl.BlockSpec((1, LANES), lambda i: (0, 0))
    qkv = pl.BlockSpec((DN_CHUNK, DN_CONV_CH), lambda i: (idx(i), 0))
    z = pl.BlockSpec((DN_CHUNK, DN_V_WIDTH), lambda i: (idx(i), OFF_Z // DN_V_WIDTH))
    ba = pl.BlockSpec((DN_CHUNK, LANES), lambda i: (idx(i), 0))
    wide = pl.BlockSpec((DN_CHUNK, DN_V_WIDTH), lambda i: (idx(i), 0))
    st = pl.BlockSpec((1, DN_HEADS * DN_KEY_DIM, LANES), lambda i: (idx(i), 0, 0))
    inv = pl.BlockSpec((1, GDN_NGROUPS * GDN_ROWS, GDN_ROWS), lambda i: (idx(i), 0, 0))
    return n, vec, qkv, z, ba, wide, st, inv


def _gdn_slices(grp):
    q = slice(grp * GDN_QK_LANES, (grp + 1) * GDN_QK_LANES)
    k = slice(DN_QK_WIDTH + grp * GDN_QK_LANES, DN_QK_WIDTH + (grp + 1) * GDN_QK_LANES)
    heads = [slice((GDN_GROUP * grp + h) * LANES, (GDN_GROUP * grp + h + 1) * LANES) for h in range(GDN_GROUP)]
    vs = [slice(2 * DN_QK_WIDTH + s.start, 2 * DN_QK_WIDTH + s.stop) for s in heads]
    return q, k, vs, heads


def _stack_cols(ref, cols):
    return jnp.concatenate([ref[:, s] for s in cols], axis=0)


def _gdn_operands(qkv_ref, z_ref, state_rows):
    sl = [_gdn_slices(grp) for grp in range(GDN_NGROUPS)]
    return ([qkv_ref[:, q] for q, _, _, _ in sl] + [qkv_ref[:, k] for _, k, _, _ in sl]
            + [_stack_cols(qkv_ref, vs) for _, _, vs, _ in sl] + [_stack_cols(z_ref, heads) for _, _, _, heads in sl]
            + [state_rows[grp * GDN_ROWS:(grp + 1) * GDN_ROWS, :] for grp in range(GDN_NGROUPS)])


def _gdn_fwd(a_log, dt_bias, norm_w, qkv_act, proj, ba):
    rows = qkv_act.shape[0]
    n, vec, qkv_s, z_s, ba_s, wide, st_s, inv_s = _gdn_specs(rows, False)

    def body(al_ref, dt_ref, nw_ref, qkv_ref, z_ref, ba_ref, o_ref, st_ref, inv_ref, state):
        @pl.when(pl.program_id(0) == 0)
        def _():
            state[...] = jnp.zeros_like(state)

        st_ref[0] = state[...]
        out = _gdn_chunk(al_ref[...], dt_ref[...], nw_ref[...], ba_ref[...], *_gdn_operands(qkv_ref, z_ref, state),
                         keep_inverses=True)
        for grp in range(GDN_NGROUPS):
            _, _, _, heads = _gdn_slices(grp)
            rs = slice(grp * GDN_ROWS, (grp + 1) * GDN_ROWS)
            for h, s in enumerate(heads):
                o_ref[:, s] = out[grp][h * DN_CHUNK:(h + 1) * DN_CHUNK].astype(o_ref.dtype)
            state[rs, :] = out[GDN_NGROUPS + grp]
            inv_ref[0, rs, :] = out[2 * GDN_NGROUPS + grp]

    return pl.pallas_call(
        body, name="gdn_fwd", grid=(n,),
        in_specs=[vec, vec, vec, qkv_s, z_s, ba_s], out_specs=[wide, st_s, inv_s],
        out_shape=[jax.ShapeDtypeStruct((rows, DN_V_WIDTH), bf16),
                   jax.ShapeDtypeStruct((n, DN_HEADS * DN_KEY_DIM, LANES), f32),
                   jax.ShapeDtypeStruct((n, GDN_NGROUPS * GDN_ROWS, GDN_ROWS), f32)],
        scratch_shapes=[pltpu.VMEM((DN_HEADS * DN_KEY_DIM, LANES), f32)],
        compiler_params=_cparams(("arbitrary",)),
    )(a_log, dt_bias, norm_w, qkv_act, proj, ba)


def _gdn_bwd(a_log, dt_bias, norm_w, qkv_act, proj, ba, states, inverses, do):
    rows = qkv_act.shape[0]
    n, vec, qkv_s, z_s, ba_s, wide, st_s, inv_s = _gdn_specs(rows, True)

    def body(al_ref, dt_ref, nw_ref, qkv_ref, z_ref, ba_ref, st_ref, inv_ref, do_ref,
             dal_ref, ddt_ref, dnw_ref, dqkv_ref, dz_ref, dba_ref, dstate):
        @pl.when(pl.program_id(0) == 0)
        def _():
            dstate[...] = jnp.zeros_like(dstate)
            dal_ref[...] = jnp.zeros_like(dal_ref)
            ddt_ref[...] = jnp.zeros_like(ddt_ref)
            dnw_ref[...] = jnp.zeros_like(dnw_ref)

        ng = GDN_NGROUPS
        kept = [inv_ref[0, grp * GDN_ROWS:(grp + 1) * GDN_ROWS, :] for grp in range(ng)]
        _, vjp = jax.vjp(functools.partial(_gdn_chunk, inverses=kept), al_ref[...], dt_ref[...], nw_ref[...],
                         ba_ref[...], *_gdn_operands(qkv_ref, z_ref, st_ref[0]))
        cts = tuple(_stack_cols(do_ref, _gdn_slices(grp)[3]) for grp in range(ng))
        cts += tuple(dstate[grp * GDN_ROWS:(grp + 1) * GDN_ROWS, :] for grp in range(ng))
        grads = vjp(cts)
        dal_ref[...] += grads[0]
        ddt_ref[...] += grads[1]
        dnw_ref[...] += grads[2]
        dba_ref[...] = grads[3]
        dqs, dks, dvs, dzs, dss = [grads[4 + i * ng:4 + (i + 1) * ng] for i in range(5)]
        for grp in range(ng):
            q, k, vs, heads = _gdn_slices(grp)
            dqkv_ref[:, q] = dqs[grp]
            dqkv_ref[:, k] = dks[grp]
            for h, (sv, sh) in enumerate(zip(vs, heads)):
                rows_h = slice(h * DN_CHUNK, (h + 1) * DN_CHUNK)
                dqkv_ref[:, sv] = dvs[grp][rows_h]
                dz_ref[:, sh] = dzs[grp][rows_h].astype(dz_ref.dtype)
            dstate[grp * GDN_ROWS:(grp + 1) * GDN_ROWS, :] = dss[grp]

    return pl.pallas_call(
        body, name="gdn_bwd", grid=(n,),
        in_specs=[vec, vec, vec, qkv_s, z_s, ba_s, st_s, inv_s, wide],
        out_specs=[vec, vec, vec, qkv_s, wide, ba_s],
        out_shape=[jax.ShapeDtypeStruct((1, LANES), f32)] * 3
        + [jax.ShapeDtypeStruct((rows, DN_CONV_CH), f32), jax.ShapeDtypeStruct((rows, DN_V_WIDTH), bf16),
           jax.ShapeDtypeStruct((rows, LANES), f32)],
        scratch_shapes=[pltpu.VMEM((DN_HEADS * DN_KEY_DIM, LANES), f32)],
        compiler_params=_cparams(("arbitrary",)),
    )(a_log, dt_bias, norm_w, qkv_act, proj, ba, states, inverses, do)


def _ada_fwd(c_all, w_loc, b_loc):
    def body(c_ref, w_ref, b_ref, o_ref):
        o_ref[...] = _bdot(_silu(c_ref[...]), w_ref[...]) + b_ref[...]

    return pl.pallas_call(body, name="ada_fwd", out_shape=jax.ShapeDtypeStruct((c_all.shape[0], w_loc.shape[1]), f32),
                          compiler_params=_cparams())(c_all, w_loc, b_loc)


def _ada_bwd(c_all, dmod_cols):
    def body(c_ref, d_ref, o_ref):
        o_ref[...] = _bdot_tn(_silu(c_ref[...]), d_ref[...])

    return pl.pallas_call(body, name="ada_bwd",
                          out_shape=jax.ShapeDtypeStruct((c_all.shape[1], dmod_cols.shape[1]), f32),
                          compiler_params=_cparams())(c_all, dmod_cols)


def _sum_devices(parts):
    def body(p_ref, o_ref):
        acc = p_ref[0:1, :]
        for d in range(1, N_DEV):
            acc = acc + p_ref[d:d + 1, :]
        o_ref[...] = acc

    return pl.pallas_call(body, name="sum_small", out_shape=jax.ShapeDtypeStruct((1, parts.shape[1]), f32),
                          compiler_params=_cparams())(parts)


def _adam_math(w, g, m, v):
    m2 = ADAM_B1 * m + (1.0 - ADAM_B1) * g
    v2 = ADAM_B2 * v + (1.0 - ADAM_B2) * jnp.square(g)
    m_hat = m2 / (1.0 - ADAM_B1 ** ADAM_STEP)
    v_hat = v2 / (1.0 - ADAM_B2 ** ADAM_STEP)
    delta = -ADAM_LR * (m_hat / (jnp.sqrt(v_hat) + ADAM_EPS) + ADAM_WD * w)
    return delta, m2, v2


def _row_tile(rows):
    return _pick(rows, (256, 128, 64, 32, 16, 8))


def _adamw(w, g, m, v, name):
    rows, cols = w.shape
    tr = _row_tile(rows)

    def body(w_ref, g_ref, m_ref, v_ref, d_ref, m2_ref, v2_ref):
        d_ref[...], m2_ref[...], v2_ref[...] = _adam_math(w_ref[...], g_ref[...], m_ref[...], v_ref[...])

    tile = pl.BlockSpec((tr, cols), lambda i: (i, 0))
    return pl.pallas_call(body, name=name, grid=(rows // tr,), in_specs=[tile] * 4, out_specs=[tile] * 3,
                          out_shape=[jax.ShapeDtypeStruct(w.shape, f32)] * 3,
                          compiler_params=_cparams(("parallel",)))(w, g, m, v)


def _sum_adamw(parts, w, m, v, name):
    rows, cols = w.shape
    tr = _row_tile(rows)

    def body(p_ref, w_ref, m_ref, v_ref, g_ref, d_ref, m2_ref, v2_ref):
        g = p_ref[0].astype(f32)
        for d in range(1, N_DEV):
            g = g + p_ref[d].astype(f32)
        g_ref[...] = g
        d_ref[...], m2_ref[...], v2_ref[...] = _adam_math(w_ref[...], g, m_ref[...], v_ref[...])

    tile = pl.BlockSpec((tr, cols), lambda i: (i, 0))
    return pl.pallas_call(body, name=name, grid=(rows // tr,),
                          in_specs=[pl.BlockSpec((N_DEV, tr, cols), lambda i: (0, i, 0)), tile, tile, tile],
                          out_specs=[tile] * 4, out_shape=[jax.ShapeDtypeStruct(w.shape, f32)] * 4,
                          compiler_params=_cparams(("parallel",)))(parts, w, m, v)


def _pad_lanes(a, width):
    return jnp.pad(a, ((0, 0), (0, width - a.shape[1])))


def _cols_by_device(full):
    r, c = full.shape
    return jnp.moveaxis(full.reshape(r, N_DEV, c // N_DEV), 1, 0)


def _cols_from_devices(parts):
    d, r, n = parts.shape
    return jnp.moveaxis(parts, 0, 1).reshape(r, d * n)


def kernel(x, c, w_ada, b_ada, norm1_w, w_in, dn_conv_w, dn_A_log, dn_dt_bias, dn_norm_w, w_proj_sb, w_proj_dn, w_out, norm2_w, w_ffn_in, ffn_conv_w, ffn_conv_b, w_ffn_out, final_norm_w, loss_target, m_w_ada, m_b_ada, m_norm1_w, m_w_in, m_dn_conv_w, m_dn_A_log, m_dn_dt_bias, m_dn_norm_w, m_w_proj_sb, m_w_proj_dn, m_w_out, m_norm2_w, m_w_ffn_in, m_ffn_conv_w, m_ffn_conv_b, m_w_ffn_out, m_final_norm_w, v_w_ada, v_b_ada, v_norm1_w, v_w_in, v_dn_conv_w, v_dn_A_log, v_dn_dt_bias, v_dn_norm_w, v_w_proj_sb, v_w_proj_dn, v_w_out, v_norm2_w, v_w_ffn_in, v_ffn_conv_w, v_ffn_conv_b, v_w_ffn_out, v_final_norm_w):
    d = D_MODEL
    me = 4 * lax.axis_index("x") + 2 * lax.axis_index("y") + lax.axis_index("c")
    xs = x[0]
    target = loss_target[0]
    n_ada = w_ada.shape[2]
    n_dnc = dn_conv_w.shape[2]
    n_ffc = ffn_conv_w.shape[2]

    small = jnp.concatenate([c, dn_conv_w[0].reshape(1, -1), ffn_conv_w[0].reshape(1, -1)], axis=1)
    small = _pad_lanes(small, -(-small.shape[1] // LANES) * LANES)
    small_g, w_in_g = _all_gather([small, w_in[0].astype(bf16)], "gather_w_in")
    later = [w_proj_sb[0].astype(bf16), w_proj_dn[0].astype(bf16), w_out[0].astype(bf16),
             w_ffn_in[0].astype(bf16), w_ffn_out[0].astype(bf16)]
    gather_later = _SideComm(_gather_protocol, later, _gathered_shapes(later))
    small_g = small_g[:, 0, :]
    c_all = small_g[:, :d]
    dn_cw = _cols_from_devices(small_g[:, d:d + DN_CONV_WIDTH * n_dnc].reshape(N_DEV, DN_CONV_WIDTH, n_dnc))
    o2 = d + DN_CONV_WIDTH * n_dnc
    ffn_cw = _cols_from_devices(small_g[:, o2:o2 + FFN_CONV_WIDTH * n_ffc].reshape(N_DEV, FFN_CONV_WIDTH, n_ffc))

    w_in_full = _cols_from_devices(w_in_g)
    r_sb, r_dn, r_z = 3 * SB_WIDTH, 3 * SB_WIDTH + DN_CONV_CH, 3 * SB_WIDTH + DN_CONV_CH + DN_V_WIDTH
    r_g = r_z + 2 * DN_HEADS
    w_main = jnp.concatenate([w_in_full[:, r_g:], w_in_full[:, r_sb:r_dn], w_in_full[:, r_dn:r_z],
                              w_in_full[:, :r_sb]], axis=1)
    w_ba = _pad_lanes(w_in_full[:, r_z:r_g], LANES)

    b_loc = lax.dynamic_slice(b_ada, (0, me * n_ada), (1, n_ada))
    mod_part = _ada_fwd(c_all, w_ada[0], b_loc)
    (mod_g,) = _all_gather([mod_part], "gather_mod")
    mod = lax.dynamic_index_in_dim(mod_g, me, axis=1, keepdims=False).reshape(1, N_DEV * n_ada)
    shift1, scale1, gate1, shift2, scale2, gate2 = [mod[:, i * d:(i + 1) * d] for i in range(6)]

    logit_lanes = ((0, 0), (GDN_LOGIT_LANE, LANES - GDN_LOGIT_LANE - DN_HEADS))
    a_log = jnp.pad(dn_A_log, logit_lanes)
    dt_b = jnp.pad(dn_dt_bias, logit_lanes)

    (h,) = _stage_fwd(_f_normmod, [norm1_w, shift1, scale1], [xs], [bf16], "norm1_fwd")
    proj = _mm(h, w_main, name="in_proj")
    ba = _mm(h, w_ba, name="in_proj_ba")
    k16, k0_16, k1_16, v16, v0_16, v1_16 = _sb_prepare(proj)
    o_a, sb_runs, w_psb_g, w_pdn_g, w_out_g, w_fin_g, w_fout_g = _sb_attention_fwd2(
        proj, k16, v0_16, v1_16, side=gather_later)
    w_psb = _cols_from_devices(w_psb_g)
    w_pdn = w_pdn_g.reshape(DN_V_WIDTH, d)
    w_o = w_out_g.reshape(d, d)
    w_fin = _cols_from_devices(w_fin_g)
    w_fout = w_fout_g.reshape(D_FF, d)
    qkv_act = _dn_conv_fwd(proj, dn_cw)
    o_b, states, dn_inverses = _gdn_fwd(a_log, dt_b, dn_norm_w, qkv_act, proj, ba)
    pa = _mm(o_a, w_psb, name="proj_sb")
    pb = _mm(o_b, w_pdn, name="proj_dn")
    gates = [(proj, d, OFF_GA // d), (proj, d, OFF_GB // d)]
    (merged,) = _stage_fwd(_f_merge, [], gates + [pa, pb], [bf16], "merge_fwd")
    ao = _mm(merged, w_o, name="out_proj")
    (x1,) = _stage_fwd(_f_residual, [gate1], [xs, ao], [f32], "resid1_fwd")
    (h2,) = _stage_fwd(_f_normmod, [norm2_w, shift2, scale2], [x1], [bf16], "norm2_fwd")
    u_pre = _mm(h2, w_fin, name="ffn_in")
    act = _ffn_conv_fwd(u_pre, ffn_cw, ffn_conv_b)
    fo = _mm(act, w_fout, name="ffn_out")

    loss_p, d_gate2, d_wf, dx2, dfo = _loss_and_grads(gate2, final_norm_w.reshape(1, d), x1, fo, target)
    dact = _mm(dfo, w_fout, tb=True, name="ffn_out_dx")
    g_w_fout = _mm(act, dfo, ta=True, name="ffn_out_dw")
    du, dbg, dbu = _ffn_conv_bwd_act(u_pre, ffn_cw, ffn_conv_b, dact)
    du_pre, d_ffn_cw = _conv_bwd(du, u_pre, 0, ffn_cw, "ffn_conv_bwd")
    dh2 = _mm(du_pre, w_fin, tb=True, name="ffn_in_dx")
    g_w_fin = _mm(h2, du_pre, ta=True, name="ffn_in_dw")
    (d_n2w, d_shift2, d_scale2), (dx1,) = _stage_bwd(
        _f_normmod, [norm2_w, shift2, scale2], [x1], [dh2], [f32], "norm2_bwd", residual=(0, dx2))
    (d_gate1,), (dao,) = _stage_bwd(_f_residual, [gate1], [xs, ao], [dx1], [None, bf16], "resid1_bwd")
    dmerged = _mm(dao, w_o, tb=True, name="out_proj_dx")
    g_w_o = _mm(merged, dao, ta=True, name="out_proj_dw")
    _, (dga, dgb, dpa, dpb) = _stage_bwd(_f_merge, [], gates + [pa, pb], [dmerged], [bf16] * 4, "merge_bwd")
    do_a = _mm(dpa, w_psb, tb=True, name="proj_sb_dx")
    g_w_psb = _mm(o_a, dpa, ta=True, name="proj_sb_dw")
    do_b = _mm(dpb, w_pdn, tb=True, name="proj_dn_dx")
    g_w_pdn = _mm(o_b, dpb, ta=True, name="proj_dn_dw")
    early = [_cols_by_device(g_w_psb).astype(bf16),
             g_w_pdn.reshape(N_DEV, DN_V_WIDTH // N_DEV, d).astype(bf16),
             g_w_o.reshape(N_DEV, d // N_DEV, d).astype(bf16),
             _cols_by_device(g_w_fin).astype(bf16),
             g_w_fout.reshape(N_DEV, D_FF // N_DEV, d).astype(bf16)]
    exchange_early = _SideComm(_exchange_protocol, early, [jax.ShapeDtypeStruct(a.shape, a.dtype) for a in early])
    dq, dk, dv, *recv_early = _sb_attention_bwd2(proj, k16, k0_16, k1_16, v16, sb_runs, do_a, side=exchange_early)
    d_alog, d_dtb, d_dnw, dqkv_act, dz, dba = _gdn_bwd(a_log, dt_b, dn_norm_w, qkv_act, proj, ba, states,
                                                       dn_inverses, do_b)
    d_conv_out = _dn_conv_bwd_act(proj, dn_cw, dqkv_act)
    d_dn_pre, d_dn_cw = _conv_bwd(d_conv_out, proj, OFF_DN // TCONV_C, dn_cw, "dn_conv_bwd")
    dproj = jnp.concatenate([dga, dgb, d_dn_pre, dz, dq, dk.astype(bf16), dv.astype(bf16)], axis=1)
    g_w_main = _mm(h, dproj, ta=True, name="in_proj_dw")
    g_w_ba = _mm(h, dba, ta=True, name="in_proj_ba_dw")
    g_w_in_full = jnp.concatenate([g_w_main[:, OFF_SBQ:], g_w_main[:, OFF_DN:OFF_Z], g_w_main[:, OFF_Z:OFF_SBQ],
                                   g_w_ba[:, :2 * DN_HEADS], g_w_main[:, :OFF_DN]], axis=1)
    w_in_parts = _cols_by_device(g_w_in_full).astype(bf16)
    exchange_w_in = _SideComm(_exchange_protocol, [w_in_parts], [jax.ShapeDtypeStruct(w_in_parts.shape, bf16)])
    dh, recv_w_in = _mm(dproj, w_main, tb=True, name="in_proj_dx", side=exchange_w_in)
    dh_ba = _mm(dba, w_ba, tb=True, name="in_proj_ba_dx")
    (d_n1w, d_shift1, d_scale1), (grad_x,) = _stage_bwd(
        _f_normmod, [norm1_w, shift1, scale1], [xs], [[dh, dh_ba]], [f32], "norm1_bwd", residual=(0, dx1))

    dmod = jnp.concatenate([d_shift1, d_scale1, d_gate1, d_shift2, d_scale2, d_gate2], axis=1)
    d_ffn_cb = jnp.concatenate([dbg, dbu], axis=1)
    small_parts = jnp.concatenate(
        [loss_p, dmod, d_n1w, d_alog, d_dtb, d_dnw, d_n2w, d_ffn_cb, d_wf,
         d_dn_cw.reshape(1, -1), d_ffn_cw.reshape(1, -1)], axis=1)
    (small_parts_g,) = _all_gather([small_parts], "gather_small_grads")
    tot = _sum_devices(small_parts_g[:, 0, :])
    offs = {}
    pos = 0
    for nm, width in (("loss", LANES), ("b_ada", 6 * d), ("norm1_w", d), ("dn_A_log", LANES), ("dn_dt_bias", LANES),
                      ("dn_norm_w", LANES), ("norm2_w", d), ("ffn_conv_b", 2 * D_FF), ("final_norm_w", d),
                      ("dn_conv_w", DN_CONV_WIDTH * DN_CONV_CH), ("ffn_conv_w", FFN_CONV_WIDTH * 2 * D_FF)):
        offs[nm] = (pos, width)
        pos += width
    seg = lambda nm: tot[:, offs[nm][0]:offs[nm][0] + offs[nm][1]]
    loss = tot[0, 0]
    g_b_ada = seg("b_ada")
    g_norm1 = seg("norm1_w")
    g_alog = seg("dn_A_log")[:, GDN_LOGIT_LANE:GDN_LOGIT_LANE + DN_HEADS]
    g_dtb = seg("dn_dt_bias")[:, GDN_LOGIT_LANE:GDN_LOGIT_LANE + DN_HEADS]
    g_dnw = seg("dn_norm_w")
    g_norm2 = seg("norm2_w")
    g_ffn_cb = seg("ffn_conv_b")
    g_fnw = seg("final_norm_w")
    g_dn_cw = lax.dynamic_slice(seg("dn_conv_w").reshape(DN_CONV_WIDTH, DN_CONV_CH), (0, me * n_dnc),
                                (DN_CONV_WIDTH, n_dnc))
    g_ffn_cw = lax.dynamic_slice(seg("ffn_conv_w").reshape(FFN_CONV_WIDTH, 2 * D_FF), (0, me * n_ffc),
                                 (FFN_CONV_WIDTH, n_ffc))

    dmod_all = small_parts_g[:, 0, offs["b_ada"][0]:offs["b_ada"][0] + 6 * d]
    g_w_ada = _ada_bwd(c_all, lax.dynamic_slice(dmod_all, (0, me * n_ada), (N_DEV, n_ada)))

    def pack(parts):
        flat = [p.reshape(1, -1) for p in parts]
        flat = [_pad_lanes(p, -(-p.shape[1] // LANES) * LANES) for p in flat]
        return jnp.concatenate(flat, axis=1), [p.shape[1] for p in flat]

    small_names_g = [g_b_ada, g_norm1, g_alog, g_dtb, g_dnw, g_norm2, g_ffn_cb, g_fnw, g_dn_cw, g_ffn_cw]
    small_w = [b_ada, norm1_w, dn_A_log, dn_dt_bias, dn_norm_w, norm2_w, ffn_conv_b, final_norm_w, dn_conv_w[0], ffn_conv_w[0]]
    small_m = [m_b_ada, m_norm1_w, m_dn_A_log, m_dn_dt_bias, m_dn_norm_w, m_norm2_w, m_ffn_conv_b, m_final_norm_w, m_dn_conv_w[0], m_ffn_conv_w[0]]
    small_v = [v_b_ada, v_norm1_w, v_dn_A_log, v_dn_dt_bias, v_dn_norm_w, v_norm2_w, v_ffn_conv_b, v_final_norm_w, v_dn_conv_w[0], v_ffn_conv_w[0]]
    pg, widths = pack(small_names_g)
    pw, _ = pack(small_w)
    pm, _ = pack(small_m)
    pv, _ = pack(small_v)
    s_delta, s_m, s_v = _adamw(pw, pg, pm, pv, "adamw_small")

    def unpack(flat):
        out, pos = [], 0
        for ref_arr, width in zip(small_w, widths):
            out.append(flat[:, pos:pos + ref_arr.size].reshape(ref_arr.shape))
            pos += width
        return out

    small_grads = [g.reshape(w_.shape) for g, w_ in zip(small_names_g, small_w)]
    small_delta, small_newm, small_newv = unpack(s_delta), unpack(s_m), unpack(s_v)

    ada_delta, ada_m, ada_v = _adamw(w_ada[0], g_w_ada, m_w_ada[0], v_w_ada[0], "adamw_ada")

    recv = [recv_w_in] + list(recv_early)
    big = {}
    for nm, parts, w_, m_, v_ in (("w_in", recv[0], w_in, m_w_in, v_w_in),
                                  ("w_proj_sb", recv[1], w_proj_sb, m_w_proj_sb, v_w_proj_sb),
                                  ("w_proj_dn", recv[2], w_proj_dn, m_w_proj_dn, v_w_proj_dn),
                                  ("w_out", recv[3], w_out, m_w_out, v_w_out),
                                  ("w_ffn_in", recv[4], w_ffn_in, m_w_ffn_in, v_w_ffn_in),
                                  ("w_ffn_out", recv[5], w_ffn_out, m_w_ffn_out, v_w_ffn_out)):
        big[nm] = [t[None] for t in _sum_adamw(parts, w_[0], m_[0], v_[0], "adamw_" + nm)]

    sg = dict(zip(["b_ada", "norm1_w", "dn_A_log", "dn_dt_bias", "dn_norm_w", "norm2_w", "ffn_conv_b", "final_norm_w",
                   "dn_conv_w", "ffn_conv_w"], range(10)))

    def small_out(table, nm):
        val = table[sg[nm]]
        return val[None] if nm in ("dn_conv_w", "ffn_conv_w") else val

    order = ["w_ada", "b_ada", "norm1_w", "w_in", "dn_conv_w", "dn_A_log", "dn_dt_bias", "dn_norm_w", "w_proj_sb",
             "w_proj_dn", "w_out", "norm2_w", "w_ffn_in", "ffn_conv_w", "ffn_conv_b", "w_ffn_out", "final_norm_w"]
    groups = []
    for k, small_table in enumerate((small_grads, small_delta, small_newm, small_newv)):
        row = []
        for nm in order:
            if nm == "w_ada":
                row.append((g_w_ada, ada_delta, ada_m, ada_v)[k][None])
            elif nm in big:
                row.append(big[nm][k])
            else:
                row.append(small_out(small_table, nm))
        groups.append(row)
    return (loss, grad_x[None], *groups[0], *groups[1], *groups[2], *groups[3])
```

```python
import functools

import jax
import jax.numpy as jnp
from jax import lax
from jax.experimental import pallas as pl
from jax.experimental.pallas import tpu as pltpu

f32 = jnp.float32
bf16 = jnp.bfloat16

D_MODEL = 1024
SB_HEADS = 8
SB_HEAD_DIM = 64
SB_WIDTH = SB_HEADS * SB_HEAD_DIM
SB_QBLOCK = 128
DN_HEADS = 8
DN_KEY_DIM = 64
DN_VAL_DIM = 128
DN_QK_WIDTH = DN_HEADS * DN_KEY_DIM
DN_V_WIDTH = DN_HEADS * DN_VAL_DIM
DN_CONV_CH = 2 * DN_QK_WIDTH + DN_V_WIDTH
DN_CONV_WIDTH = 4
DN_CHUNK = 64
D_FF = 2816
FFN_CONV_WIDTH = 3
NORM_EPS = 1e-6
L2_EPS = 1e-6
ADAM_LR = 0.001
ADAM_B1 = 0.9
ADAM_B2 = 0.999
ADAM_EPS = 1e-08
ADAM_WD = 0.01
ADAM_STEP = 10

N_DEV = 8
MESH = pl.DeviceIdType.MESH

LANES = 128
SUBLANES = 8
VMEM_LIMIT = 48 * 1024 * 1024

OFF_GA = 0
OFF_GB = D_MODEL
OFF_DN = 2 * D_MODEL
OFF_Z = OFF_DN + DN_CONV_CH
OFF_SBQ = OFF_Z + DN_V_WIDTH
OFF_SBK = OFF_SBQ + SB_WIDTH
OFF_SBV = OFF_SBK + SB_WIDTH
MAIN_WIDTH = OFF_SBV + SB_WIDTH

TM = 256
TCONV_R = 512
TCONV_C = 512
TCONV_FF = D_FF // 2
SB_PAIRS_FWD = 4
SB_PAIRS_BWD = 4
SB_DEAD = -106.0
SB_NEVER = -1e30


def _cparams(sem=None):
    return pltpu.CompilerParams(dimension_semantics=sem, vmem_limit_bytes=VMEM_LIMIT)


def _pick(n, cands):
    for c in cands:
        if n % c == 0:
            return c
    return n


def _my_pos():
    return lax.axis_index("x"), lax.axis_index("y"), lax.axis_index("c")


def _flip(v, bit):
    return 1 - v if bit else v


def _comm_scratch(n):
    return [pltpu.SemaphoreType.DMA((n, 7)), pltpu.SemaphoreType.DMA((n, 7)), pltpu.SemaphoreType.DMA((n,))]


def _gather_protocol(ins, outs, send_sems, recv_sems, local_sems):
    n = len(ins)
    x, y, c = _my_pos()
    me, sibling = (x, y, c), (x, y, 1 - c)
    chips = [(1 - x, y), (x, 1 - y), (1 - x, 1 - y)]

    def slot(out, pos):
        return out.at[4 * pos[0] + 2 * pos[1] + pos[2]]

    def copy(a, k, block, to, src=None):
        return pltpu.make_async_remote_copy(
            src_ref=slot(outs[a], block) if src is None else src, dst_ref=slot(outs[a], block),
            send_sem=send_sems.at[a, k], recv_sem=recv_sems.at[a, k], device_id=to, device_id_type=MESH)

    def local(a):
        return pltpu.make_async_copy(ins[a], slot(outs[a], me), local_sems.at[a])

    def first(a):
        return [copy(a, 0, me, sibling, src=ins[a])] + [copy(a, 1 + j, me, (*chip, c), src=ins[a])
                                                         for j, chip in enumerate(chips)]

    def start():
        for a in range(n):
            local(a).start()
            for cp in first(a):
                cp.start()

    def finish():
        forwards = []
        for a in range(n):
            for j, chip in enumerate(chips):
                copy(a, 1 + j, (*chip, c), me).wait_recv()
                fwd = copy(a, 4 + j, (*chip, c), sibling)
                fwd.start()
                forwards.append(fwd)
        for a in range(n):
            copy(a, 0, sibling, me).wait_recv()
            for j, chip in enumerate(chips):
                copy(a, 4 + j, (*chip, 1 - c), me).wait_recv()
        for a in range(n):
            for cp in first(a):
                cp.wait_send()
        for cp in forwards:
            cp.wait_send()
        for a in range(n):
            local(a).wait()

    return start, finish


def _exchange_protocol(ins, outs, send_sems, recv_sems, local_sems):
    n = len(ins)
    x, y, c = _my_pos()
    me_idx = 4 * x + 2 * y + c

    def local(a):
        return pltpu.make_async_copy(ins[a].at[me_idx], outs[a].at[me_idx], local_sems.at[a])

    def copies(a, m):
        peer = (_flip(x, m & 4), _flip(y, m & 2), _flip(c, m & 1))
        peer_idx = 4 * peer[0] + 2 * peer[1] + peer[2]
        sems = dict(send_sem=send_sems.at[a, m - 1], recv_sem=recv_sems.at[a, m - 1], device_id=peer,
                    device_id_type=MESH)
        send = pltpu.make_async_remote_copy(src_ref=ins[a].at[peer_idx], dst_ref=outs[a].at[me_idx], **sems)
        recv = pltpu.make_async_remote_copy(src_ref=ins[a].at[peer_idx], dst_ref=outs[a].at[peer_idx], **sems)
        return send, recv

    def start():
        for a in range(n):
            local(a).start()
            for m in range(1, N_DEV):
                copies(a, m)[0].start()

    def finish():
        for a in range(n):
            for m in range(1, N_DEV):
                copies(a, m)[1].wait_recv()
        for a in range(n):
            for m in range(1, N_DEV):
                copies(a, m)[0].wait_send()
            local(a).wait()

    return start, finish


def _collective_call(protocol, arrs, out_shapes, name):
    n = len(arrs)

    def body(*refs):
        start, finish = protocol(refs[:n], refs[n:2 * n], *refs[2 * n:])
        start()
        finish()

    any_spec = pl.BlockSpec(memory_space=pl.ANY)
    return pl.pallas_call(body, name=name, out_shape=out_shapes, in_specs=[any_spec] * n, out_specs=[any_spec] * n,
                          scratch_shapes=_comm_scratch(n))(*arrs)


def _gathered_shapes(arrs):
    return [jax.ShapeDtypeStruct((N_DEV,) + a.shape, a.dtype) for a in arrs]


def _all_gather(arrs, name):
    return _collective_call(_gather_protocol, arrs, _gathered_shapes(arrs), name)


MM_BLOCK_BYTES = 4 * 1024 * 1024


def _mm_tiles(m_dim, n_dim, k_dim, a_bytes, b_bytes):
    tm = _pick(m_dim, (1024, 512, 256, 128))
    tn = _pick(n_dim, (1024, 512, 256, 128))
    tk = k_dim
    if k_dim % LANES == 0:
        units = k_dim // LANES
        fits = [u for u in range(1, units + 1) if units % u == 0
                and u * LANES * max(tm * a_bytes, tn * b_bytes) <= MM_BLOCK_BYTES]
        tk = max(fits) * LANES
    return tm, tn, tk


def _mm(a, b, *, ta=False, tb=False, name, side=None):
    (k_dim, m_dim) = a.shape if ta else a.shape[::-1]
    (n_dim, kb_dim) = b.shape if tb else b.shape[::-1]
    assert k_dim == kb_dim, (a.shape, b.shape, ta, tb)
    tm, tn, tk = _mm_tiles(m_dim, n_dim, k_dim, a.dtype.itemsize, b.dtype.itemsize)
    nk = k_dim // tk
    grid = (m_dim // tm, n_dim // tn, nk)
    dims = (((0 if ta else 1,), (1 if tb else 0,)), ((), ()))
    ns = side.n if side else 0

    def body(a_ref, b_ref, *rest):
        if side:
            side.run(rest[:ns], rest[ns + 1:2 * ns + 1], rest[2 * ns + 1:], *_grid_ends(grid),
                     lambda: compute(a_ref, b_ref, rest[ns]))
        else:
            compute(a_ref, b_ref, rest[0])

    def compute(a_ref, b_ref, o_ref):
        part = lax.dot_general(a_ref[...].astype(bf16), b_ref[...].astype(bf16), dims, preferred_element_type=f32)
        if nk == 1:
            o_ref[...] = part
        else:
            k = pl.program_id(2)

            @pl.when(k == 0)
            def _():
                o_ref[...] = part

            @pl.when(k > 0)
            def _():
                o_ref[...] += part

    a_spec = pl.BlockSpec((tk, tm), lambda i, j, k: (k, i)) if ta else pl.BlockSpec((tm, tk), lambda i, j, k: (i, k))
    b_spec = pl.BlockSpec((tn, tk), lambda i, j, k: (j, k)) if tb else pl.BlockSpec((tk, tn), lambda i, j, k: (k, j))
    out_spec = pl.BlockSpec((tm, tn), lambda i, j, k: (i, j))
    out_shape = jax.ShapeDtypeStruct((m_dim, n_dim), f32)
    if not side:
        return pl.pallas_call(body, name=name, grid=grid, in_specs=[a_spec, b_spec], out_specs=out_spec,
                              out_shape=out_shape,
                              compiler_params=_cparams(("parallel", "parallel", "arbitrary")))(a, b)
    return pl.pallas_call(
        body, name=name, grid=grid, in_specs=[a_spec, b_spec] + side.specs(), out_specs=[out_spec] + side.specs(),
        out_shape=[out_shape] + side.out_shapes, scratch_shapes=_comm_scratch(ns),
        compiler_params=_cparams(("arbitrary", "arbitrary", "arbitrary")))(a, b, *side.arrs)


def _win(t):
    return t if isinstance(t, tuple) else (t, t.shape[1], 0)


def _tile_spec(width, cb, tm):
    return pl.BlockSpec((tm, width), lambda i: (i, cb))


def _param_spec(p):
    return pl.BlockSpec(p.shape, lambda i: (0, 0))


def _stage_fwd(f, params, tiles, out_dtypes, name):
    tiles = [_win(t) for t in tiles]
    rows = tiles[0][0].shape[0]
    tm = min(TM, rows)
    avals = jax.eval_shape(f, *[jax.ShapeDtypeStruct(p.shape, f32) for p in params],
                           *[jax.ShapeDtypeStruct((tm, w), f32) for _, w, _ in tiles])
    n_p, n_t = len(params), len(tiles)

    def body(*refs):
        p = [r[...] for r in refs[:n_p]]
        t = [r[...].astype(f32) for r in refs[n_p:n_p + n_t]]
        for o_ref, val in zip(refs[n_p + n_t:], f(*p, *t)):
            o_ref[...] = val.astype(o_ref.dtype)

    return pl.pallas_call(
        body, name=name, grid=(rows // tm,),
        in_specs=[_param_spec(p) for p in params] + [_tile_spec(w, cb, tm) for _, w, cb in tiles],
        out_specs=[_tile_spec(a.shape[1], 0, tm) for a in avals],
        out_shape=[jax.ShapeDtypeStruct((rows, a.shape[1]), dt) for a, dt in zip(avals, out_dtypes)],
        compiler_params=_cparams(("parallel",)),
    )(*params, *[t[0] for t in tiles])


def _stage_bwd(f, params, tiles, cts, grad_dtypes, name, residual=None):
    tiles = [_win(t) for t in tiles]
    rows = tiles[0][0].shape[0]
    tm = min(TM, rows)
    cts = [list(g) if isinstance(g, (list, tuple)) else [g] for g in cts]
    flat_cts = [a for g in cts for a in g]
    n_p, n_t, n_c = len(params), len(tiles), len(flat_cts)
    has_res = residual is not None
    want = [j for j, dt in enumerate(grad_dtypes) if dt is not None]

    def body(*refs):
        i = pl.program_id(0)
        p = [r[...] for r in refs[:n_p]]
        t = [r[...].astype(f32) for r in refs[n_p:n_p + n_t]]
        ct_vals = [r[...].astype(f32) for r in refs[n_p + n_t:n_p + n_t + n_c]]
        ct, at = [], 0
        for g in cts:
            ct.append(functools.reduce(jnp.add, ct_vals[at:at + len(g)]))
            at += len(g)
        ct = tuple(ct)
        pos = n_p + n_t + n_c
        res_ref = refs[pos] if has_res else None
        pos += 1 if has_res else 0
        dp_refs = refs[pos:pos + n_p]
        dt_refs = refs[pos + n_p:]
        _, vjp = jax.vjp(f, *p, *t)
        grads = vjp(ct)

        @pl.when(i == 0)
        def _():
            for r in dp_refs:
                r[...] = jnp.zeros_like(r)

        for r, g in zip(dp_refs, grads[:n_p]):
            r[...] += g
        for r, j in zip(dt_refs, want):
            g = grads[n_p + j]
            if has_res and j == residual[0]:
                g = g + res_ref[...].astype(f32)
            r[...] = g.astype(r.dtype)

    in_arrays = list(params) + [t[0] for t in tiles] + flat_cts
    in_specs = ([_param_spec(p) for p in params] + [_tile_spec(w, cb, tm) for _, w, cb in tiles]
                + [_tile_spec(c.shape[1], 0, tm) for c in flat_cts])
    if has_res:
        in_arrays.append(residual[1])
        in_specs.append(_tile_spec(residual[1].shape[1], 0, tm))
    out_shape = ([jax.ShapeDtypeStruct(p.shape, f32) for p in params]
                 + [jax.ShapeDtypeStruct((rows, tiles[j][1]), grad_dtypes[j]) for j in want])
    out_specs = [_param_spec(p) for p in params] + [_tile_spec(tiles[j][1], 0, tm) for j in want]
    outs = pl.pallas_call(
        body, name=name, grid=(rows // tm,), in_specs=in_specs, out_specs=out_specs, out_shape=out_shape,
        compiler_params=_cparams(("arbitrary",)),
    )(*in_arrays)
    return outs[:n_p], outs[n_p:]


def _rms(x, w):
    return x * lax.rsqrt(jnp.mean(x * x, axis=-1, keepdims=True) + NORM_EPS) * w


def _f_normmod(w, shift, scale, x):
    return (_rms(x, w) * (1.0 + scale) + shift,)


def _f_merge(ga, gb, pa, pb):
    return (jax.nn.sigmoid(ga) * pa + jax.nn.sigmoid(gb) * pb,)


def _f_residual_normmod(gate, w, shift, scale, x, branch):
    x1 = x + gate * branch
    return x1, _rms(x1, w) * (1.0 + scale) + shift


def _f_loss(gate, wf, x1, fo, target):
    y = _rms(x1 + gate * fo, wf)
    err = jnp.square(y - target)
    return (0.5 * jnp.sum(jnp.mean(err, axis=-1, keepdims=True), axis=0, keepdims=True),)


def _loss_and_grads(gate2, wf, x1, fo, target):
    rows, d = x1.shape
    tm = min(TM, rows)

    def body(g_ref, w_ref, x_ref, fo_ref, t_ref, loss_ref, dg_ref, dw_ref, dx_ref, dfo_ref):
        i = pl.program_id(0)
        (val,), vjp = jax.vjp(_f_loss, g_ref[...], w_ref[...], x_ref[...], fo_ref[...], t_ref[...])
        dg, dw, dx, dfo, _ = vjp((jnp.ones((1, 1), f32),))

        @pl.when(i == 0)
        def _():
            loss_ref[...] = jnp.zeros_like(loss_ref)
            dg_ref[...] = jnp.zeros_like(dg_ref)
            dw_ref[...] = jnp.zeros_like(dw_ref)

        loss_ref[...] += jnp.broadcast_to(val, loss_ref.shape)
        dg_ref[...] += dg
        dw_ref[...] += dw
        dx_ref[...] = dx
        dfo_ref[...] = dfo.astype(bf16)

    vec = pl.BlockSpec((1, d), lambda i: (0, 0))
    tile = pl.BlockSpec((tm, d), lambda i: (i, 0))
    return pl.pallas_call(
        body, name="loss_fwd_bwd", grid=(rows // tm,),
        in_specs=[vec, vec, tile, tile, tile],
        out_specs=[pl.BlockSpec((1, LANES), lambda i: (0, 0)), vec, vec, tile, tile],
        out_shape=[jax.ShapeDtypeStruct((1, LANES), f32), jax.ShapeDtypeStruct((1, d), f32),
                   jax.ShapeDtypeStruct((1, d), f32), jax.ShapeDtypeStruct((rows, d), f32),
                   jax.ShapeDtypeStruct((rows, d), bf16)],
        compiler_params=_cparams(("arbitrary",)),
    )(gate2, wf, x1, fo, target)


def _softplus(z):
    return jnp.maximum(z, 0.0) + jnp.log(1.0 + jnp.exp(-jnp.abs(z)))


def _split_dot(a, m):
    hi = a.astype(bf16)
    lo = (a - hi.astype(f32)).astype(bf16)
    return jnp.dot(hi, m, preferred_element_type=f32) + jnp.dot(lo, m, preferred_element_type=f32)


def _suffix_matrix(n):
    r = lax.broadcasted_iota(jnp.int32, (n, n), 0)
    c = lax.broadcasted_iota(jnp.int32, (n, n), 1)
    return (r > c).astype(bf16)


def _head_masks():
    lane = lax.broadcasted_iota(jnp.int32, (1, LANES), 1)
    return [(lane < SB_HEAD_DIM).astype(f32), (lane >= SB_HEAD_DIM).astype(f32)]


def _sb_prepare(proj):
    def f(k, v):
        lane = lax.broadcasted_iota(jnp.int32, (1, SB_WIDTH), 1)
        m0 = (jnp.bitwise_and(lane, LANES - 1) < SB_HEAD_DIM).astype(f32)
        m1 = 1.0 - m0
        return k, k * m0, k * m1, v, v * m0, v * m1

    wins = [(proj, SB_WIDTH, OFF_SBK // SB_WIDTH), (proj, SB_WIDTH, OFF_SBV // SB_WIDTH)]
    return _stage_fwd(f, [], wins, [bf16] * 6, "sb_prepare")


def _stack_heads(x):
    m0, m1 = _head_masks()
    return jnp.concatenate([x * m0, x * m1], axis=0)


def _sb_logits(qst, k, t_pos2, kb, bq, masked):
    z = lax.dot_general(qst, k, (((1,), (1,)), ((), ())), preferred_element_type=f32)
    l = -_softplus(z)
    if masked:
        s_pos = kb * bq + lax.broadcasted_iota(jnp.int32, (1, bq), 1)
        causal = s_pos < t_pos2
        l = jnp.where(causal, l, 0.0)
    else:
        causal = None
    return z, l, causal


class _SideComm:
    def __init__(self, protocol, arrs, out_shapes):
        self.protocol, self.arrs, self.out_shapes = protocol, list(arrs), list(out_shapes)
        self.n = len(self.arrs)

    def specs(self):
        return [pl.BlockSpec(memory_space=pl.ANY)] * self.n

    def run(self, in_refs, out_refs, sems, first, last, compute):
        start, finish = self.protocol(in_refs, out_refs, *sems)
        pl.when(first)(start)
        compute()
        pl.when(last)(finish)


def _grid_ends(grid):
    ids = [pl.program_id(axis) for axis in range(len(grid))]
    first = functools.reduce(jnp.logical_and, [i == 0 for i in ids])
    last = functools.reduce(jnp.logical_and, [i == g - 1 for i, g in zip(ids, grid)])
    return first, last


def _sb_attention_fwd2(proj, k16, v0_16, v1_16, side=None):
    rows = proj.shape[0]
    bq = SB_QBLOCK
    nq = rows // bq
    assert nq <= LANES, "one lane per key block"
    npair = SB_WIDTH // LANES
    scale = SB_HEAD_DIM ** -0.5

    npp = SB_PAIRS_FWD
    wq = npp * LANES
    grid = (npair // npp, nq)
    ns = side.n if side else 0

    def body(q_ref, k_ref, v0_ref, v1_ref, *rest):
        o_ref, runs_ref = rest[ns], rest[ns + 1]
        if side:
            side.run(rest[:ns], rest[ns + 2:2 * ns + 2], rest[2 * ns + 2:], *_grid_ends(grid),
                     lambda: compute(q_ref, k_ref, v0_ref, v1_ref, o_ref, runs_ref))
        else:
            compute(q_ref, k_ref, v0_ref, v1_ref, o_ref, runs_ref)

    def compute(q_ref, k_ref, v0_ref, v1_ref, o_ref, runs_ref):
        qi = pl.program_id(1)
        pairs = [slice(pp * LANES, (pp + 1) * LANES) for pp in range(npp)]
        qst = [(_stack_heads(q_ref[:, s]) * scale).astype(bf16) for s in pairs]
        r = lax.broadcasted_iota(jnp.int32, (bq, 2 * bq), 0)
        c = lax.broadcasted_iota(jnp.int32, (bq, 2 * bq), 1)
        m2 = jnp.logical_or(r > c, c >= bq).astype(bf16)
        t_pos = qi * bq + lax.broadcasted_iota(jnp.int32, (bq, 1), 0)
        t_pos2 = jnp.concatenate([t_pos, t_pos], axis=0)
        lane = lax.broadcasted_iota(jnp.int32, (1, LANES), 1)
        runs_ref[...] = jnp.full(runs_ref.shape, SB_NEVER, f32)

        def tiles(kbs, carry, masked):
            jobs = [(pp, kb) for kb in kbs for pp in range(npp)]
            rows_k = [pl.ds(pl.multiple_of(kb * bq, bq), bq) for _, kb in jobs]
            zl = [_sb_logits(qst[pp], k_ref[rk, pairs[pp]], t_pos2, kb, bq, masked) for (pp, kb), rk in zip(jobs, rows_k)]
            cs = [_split_dot(l, m2) for _, l, _ in zl]
            run = [cr[0] for cr in carry]
            acc = [cr[1] for cr in carry]
            probs = []
            for (pp, kb), (z, l, causal), cs2 in zip(jobs, zl, cs):
                a = jnp.exp(z + l + cs2[:, :bq] + run[pp])
                if masked:
                    a = jnp.where(causal, a, 0.0)
                probs.append(a.astype(bf16))
                for hh in range(2):
                    cols = slice((2 * pp + hh) * LANES, (2 * pp + hh + 1) * LANES)
                    runs_ref[:, cols] = jnp.where(lane == kb, run[pp][hh * bq:(hh + 1) * bq], runs_ref[:, cols])
                run[pp] = run[pp] + cs2[:, bq:]
            for (pp, kb), rk, ab in zip(jobs, rows_k, probs):
                acc[pp] = (acc[pp] + jnp.dot(ab[:bq], v0_ref[rk, pairs[pp]], preferred_element_type=f32)
                           + jnp.dot(ab[bq:], v1_ref[rk, pairs[pp]], preferred_element_type=f32))
            return tuple(zip(run, acc))

        zero = (jnp.zeros((2 * bq, bq), f32), jnp.zeros((bq, LANES), f32))
        carry = tiles([qi], (zero,) * npp, True)

        def alive(cr):
            return functools.reduce(jnp.maximum, [jnp.max(run) for run, _ in cr]) > SB_DEAD

        def two(state):
            i, _, cr = state
            cr = tiles([qi - 1 - 2 * i, qi - 2 - 2 * i], cr, False)
            return i + 1, alive(cr), cr

        n_two = qi // 2
        i_end, still, carry = lax.while_loop(lambda st: jnp.logical_and(st[0] < n_two, st[1]), two,
                                             (jnp.int32(0), alive(carry), carry))
        last_one = jnp.logical_and(qi % 2 == 1, jnp.logical_and(still, i_end == n_two))
        carry = lax.cond(last_one, lambda cr: tiles([0], cr, False), lambda cr: cr, carry)
        for pp in range(npp):
            o_ref[:, pairs[pp]] = carry[pp][1]

    kv = pl.BlockSpec((rows, wq), lambda p, i: (0, p))
    return pl.pallas_call(
        body, name="sb_attn_fwd", grid=grid,
        in_specs=[pl.BlockSpec((bq, wq), lambda p, i: (i, OFF_SBQ // wq + p)), kv, kv, kv] + (side.specs() if side else []),
        out_specs=[pl.BlockSpec((bq, wq), lambda p, i: (i, p)),
                   pl.BlockSpec((bq, 2 * wq), lambda p, i: (i, p))] + (side.specs() if side else []),
        out_shape=[jax.ShapeDtypeStruct((rows, SB_WIDTH), f32),
                   jax.ShapeDtypeStruct((rows, SB_HEADS * LANES), f32)] + (side.out_shapes if side else []),
        scratch_shapes=_comm_scratch(ns) if side else [],
        compiler_params=_cparams(("arbitrary", "arbitrary")),
    )(proj, k16, v0_16, v1_16, *(side.arrs if side else []))


def _sb_attention_bwd2(proj, k16, k0_16, k1_16, v16, runs, do, side=None):
    rows = proj.shape[0]
    bq = SB_QBLOCK
    nq = rows // bq
    npair = SB_WIDTH // LANES
    scale = SB_HEAD_DIM ** -0.5
    tn = (((0,), (0,)), ((), ()))
    nt = (((1,), (1,)), ((), ()))

    npp = SB_PAIRS_BWD
    wq = npp * LANES
    grid = (npair // npp, nq)
    ns = side.n if side else 0

    def body(q_ref, k_ref, k0_ref, k1_ref, v_ref, runs_ref, do_ref, *rest):
        outs = rest[ns:ns + 3]
        ins = (q_ref, k_ref, k0_ref, k1_ref, v_ref, runs_ref, do_ref)
        if side:
            side.run(rest[:ns], rest[ns + 3:2 * ns + 3], rest[2 * ns + 3:], *_grid_ends(grid),
                     lambda: compute(*ins, *outs))
        else:
            compute(*ins, *outs)

    def compute(q_ref, k_ref, k0_ref, k1_ref, v_ref, runs_ref, do_ref, dq_ref, dk_ref, dv_ref):
        qi = pl.program_id(1)

        @pl.when(qi == 0)
        def _():
            dk_ref[...] = jnp.zeros_like(dk_ref)
            dv_ref[...] = jnp.zeros_like(dv_ref)

        pairs = [slice(pp * LANES, (pp + 1) * LANES) for pp in range(npp)]
        qst = [(_stack_heads(q_ref[:, s]) * scale).astype(bf16) for s in pairs]
        dost = [_stack_heads(do_ref[:, s]).astype(bf16) for s in pairs]
        runs = [jnp.concatenate([runs_ref[:, 2 * pp * LANES:(2 * pp + 1) * LANES],
                                 runs_ref[:, (2 * pp + 1) * LANES:(2 * pp + 2) * LANES]], axis=0) for pp in range(npp)]
        r = lax.broadcasted_iota(jnp.int32, (bq, 2 * bq), 0)
        c = lax.broadcasted_iota(jnp.int32, (bq, 2 * bq), 1)
        suffix_m = _suffix_matrix(bq)
        m2 = jnp.logical_or(r < c, c >= bq).astype(bf16)
        t_pos = qi * bq + lax.broadcasted_iota(jnp.int32, (bq, 1), 0)
        t_pos2 = jnp.concatenate([t_pos, t_pos], axis=0)
        lane = lax.broadcasted_iota(jnp.int32, (1, LANES), 1)

        def tiles(kbs, carry, masked):
            jobs = [(pp, kb) for kb in kbs for pp in range(npp)]
            rows_k = [pl.ds(pl.multiple_of(kb * bq, bq), bq) for _, kb in jobs]
            zl = [_sb_logits(qst[pp], k_ref[rk, pairs[pp]], t_pos2, kb, bq, masked) for (pp, kb), rk in zip(jobs, rows_k)]
            das = [lax.dot_general(dost[pp], v_ref[rk, pairs[pp]], nt, preferred_element_type=f32)
                   for (pp, kb), rk in zip(jobs, rows_k)]
            sticks = [_split_dot(l, suffix_m) for _, l, _ in zl]
            probs, ps = [], []
            for (pp, kb), (z, l, causal), stick, da in zip(jobs, zl, sticks, das):
                run = jnp.sum(jnp.where(lane == kb, runs[pp], 0.0), axis=1, keepdims=True)
                a = jnp.exp(z + l + stick + run)
                if masked:
                    a = jnp.where(causal, a, 0.0)
                probs.append(a.astype(bf16))
                ps.append(da * a)
            pcs = [_split_dot(p, m2) for p in ps]
            pref = [cr[0] for cr in carry]
            dq_acc = [cr[1] for cr in carry]
            dzs = []
            for (pp, kb), (z, l, causal), p, pc2 in zip(jobs, zl, ps, pcs):
                dz = p * jnp.exp(l) - jnp.exp(z + l) * (pc2[:, :bq] + pref[pp])
                if masked:
                    dz = jnp.where(causal, dz, 0.0)
                dzs.append(dz.astype(bf16))
                pref[pp] = pref[pp] + pc2[:, bq:]
            for (pp, kb), rk, dzb, ab in zip(jobs, rows_k, dzs, probs):
                cols = pairs[pp]
                dq_acc[pp] = (dq_acc[pp] + jnp.dot(dzb[:bq], k0_ref[rk, cols], preferred_element_type=f32)
                              + jnp.dot(dzb[bq:], k1_ref[rk, cols], preferred_element_type=f32))
                dk_ref[rk, cols] += lax.dot_general(dzb, qst[pp], tn, preferred_element_type=f32)
                dv_ref[rk, cols] += lax.dot_general(ab, dost[pp], tn, preferred_element_type=f32)
            return tuple(zip(pref, dq_acc))

        zero = (jnp.zeros((2 * bq, bq), f32), jnp.zeros((bq, LANES), f32))
        colmax = functools.reduce(jnp.maximum, [jnp.max(x, axis=0, keepdims=True) for x in runs])
        live = jnp.logical_and(colmax > SB_DEAD, lane < qi)
        kb0 = jnp.minimum(jnp.min(jnp.where(live, lane, LANES)), qi)
        n_blocks = qi - kb0
        carry = lax.fori_loop(0, n_blocks // 2, lambda i, cr: tiles([kb0 + 2 * i, kb0 + 2 * i + 1], cr, False),
                              (zero,) * npp)
        carry = lax.cond(n_blocks % 2 == 1, lambda cr: tiles([qi - 1], cr, False), lambda cr: cr, carry)
        carry = tiles([qi], carry, True)
        for pp in range(npp):
            dq_ref[:, pairs[pp]] = (carry[pp][1] * scale).astype(dq_ref.dtype)

    blk = pl.BlockSpec((bq, wq), lambda p, i: (i, p))
    full = pl.BlockSpec((rows, wq), lambda p, i: (0, p), pipeline_mode=pl.Buffered(1))
    return pl.pallas_call(
        body, name="sb_attn_bwd", grid=grid,
        in_specs=[pl.BlockSpec((bq, wq), lambda p, i: (i, OFF_SBQ // wq + p)), full, full, full, full,
                  pl.BlockSpec((bq, 2 * wq), lambda p, i: (i, p)), blk] + (side.specs() if side else []),
        out_specs=[blk, full, full] + (side.specs() if side else []),
        out_shape=[jax.ShapeDtypeStruct((rows, SB_WIDTH), bf16), jax.ShapeDtypeStruct((rows, SB_WIDTH), f32),
                   jax.ShapeDtypeStruct((rows, SB_WIDTH), f32)] + (side.out_shapes if side else []),
        scratch_shapes=_comm_scratch(ns) if side else [],
        compiler_params=_cparams(("arbitrary", "arbitrary")),
    )(proj, k16, k0_16, k1_16, v16, runs, do, *(side.arrs if side else []))


def _shift_down(x, prev8, j):
    if j == 0:
        return x
    r = pltpu.roll(x, j, axis=0)
    row8 = lax.broadcasted_iota(jnp.int32, prev8.shape, 0)
    head = jnp.where(row8 < j, pltpu.roll(prev8, j, axis=0), r[0:SUBLANES])
    return jnp.concatenate([head, r[SUBLANES:]], axis=0)


def _shift_up(x, next8, j):
    if j == 0:
        return x
    n = x.shape[0]
    r = pltpu.roll(x, n - j, axis=0)
    row8 = lax.broadcasted_iota(jnp.int32, next8.shape, 0)
    tail = jnp.where(row8 >= SUBLANES - j, pltpu.roll(next8, SUBLANES - j, axis=0), r[n - SUBLANES:n])
    return jnp.concatenate([r[:n - SUBLANES], tail], axis=0)


def _conv(x, prev8, w):
    k_taps = w.shape[0]
    out = x * w[k_taps - 1:k_taps, :]
    for j in range(1, k_taps):
        out = out + _shift_down(x, prev8, j) * w[k_taps - 1 - j:k_taps - j, :]
    return out


def _conv_tiles(rows, tr_max=TCONV_R):
    tr = min(tr_max, rows)
    return tr, rows // tr, tr // SUBLANES


def _prev_spec(tc, cb0, r8):
    return pl.BlockSpec((SUBLANES, tc), lambda j, i: (jnp.maximum(i * r8 - 1, 0), cb0 + j))


def _silu(x):
    return x * jax.nn.sigmoid(x)


def _dsilu(x):
    s = jax.nn.sigmoid(x)
    return s * (1.0 + x * (1.0 - s))


def _dn_conv_fwd(proj, w):
    rows = proj.shape[0]
    tr, nr, r8 = _conv_tiles(rows)
    tc = TCONV_C
    cb0 = OFF_DN // tc

    def body(x_ref, p_ref, w_ref, o_ref):
        prev = jnp.where(pl.program_id(1) == 0, 0.0, p_ref[...])
        o_ref[...] = _silu(_conv(x_ref[...], prev, w_ref[...]))

    return pl.pallas_call(
        body, name="dn_conv_fwd", grid=(DN_CONV_CH // tc, nr),
        in_specs=[pl.BlockSpec((tr, tc), lambda j, i: (i, cb0 + j)), _prev_spec(tc, cb0, r8),
                  pl.BlockSpec((DN_CONV_WIDTH, tc), lambda j, i: (0, j))],
        out_specs=pl.BlockSpec((tr, tc), lambda j, i: (i, j)),
        out_shape=jax.ShapeDtypeStruct((rows, DN_CONV_CH), f32),
        compiler_params=_cparams(("parallel", "parallel")),
    )(proj, proj, w)


def _dn_conv_bwd_act(proj, w, dact):
    rows = proj.shape[0]
    tr, nr, r8 = _conv_tiles(rows)
    tc = TCONV_C
    cb0 = OFF_DN // tc

    def body(x_ref, p_ref, w_ref, d_ref, o_ref):
        prev = jnp.where(pl.program_id(1) == 0, 0.0, p_ref[...])
        o_ref[...] = d_ref[...] * _dsilu(_conv(x_ref[...], prev, w_ref[...]))

    return pl.pallas_call(
        body, name="dn_conv_bwd_act", grid=(DN_CONV_CH // tc, nr),
        in_specs=[pl.BlockSpec((tr, tc), lambda j, i: (i, cb0 + j)), _prev_spec(tc, cb0, r8),
                  pl.BlockSpec((DN_CONV_WIDTH, tc), lambda j, i: (0, j)),
                  pl.BlockSpec((tr, tc), lambda j, i: (i, j))],
        out_specs=pl.BlockSpec((tr, tc), lambda j, i: (i, j)),
        out_shape=jax.ShapeDtypeStruct((rows, DN_CONV_CH), f32),
        compiler_params=_cparams(("parallel", "parallel")),
    )(proj, proj, w, dact)


def _ffn_conv_fwd(u_pre, w, b):
    rows = u_pre.shape[0]
    tr, nr, r8 = _conv_tiles(rows, TCONV_R // 2)
    tc = TCONV_FF
    nct = D_FF // tc

    def body(xg_ref, pg_ref, xu_ref, pu_ref, wg_ref, wu_ref, bg_ref, bu_ref, o_ref):
        first = pl.program_id(1) == 0
        ug = _conv(xg_ref[...], jnp.where(first, 0.0, pg_ref[...]), wg_ref[...]) + bg_ref[...]
        uu = _conv(xu_ref[...], jnp.where(first, 0.0, pu_ref[...]), wu_ref[...]) + bu_ref[...]
        o_ref[...] = (_silu(ug) * uu).astype(o_ref.dtype)

    def x_spec(off):
        return pl.BlockSpec((tr, tc), lambda j, i: (i, off + j))

    def w_spec(k, off):
        return pl.BlockSpec((k, tc), lambda j, i: (0, off + j))

    return pl.pallas_call(
        body, name="ffn_conv_fwd", grid=(nct, nr),
        in_specs=[x_spec(0), _prev_spec(tc, 0, r8), x_spec(nct), _prev_spec(tc, nct, r8),
                  w_spec(FFN_CONV_WIDTH, 0), w_spec(FFN_CONV_WIDTH, nct), w_spec(1, 0), w_spec(1, nct)],
        out_specs=pl.BlockSpec((tr, tc), lambda j, i: (i, j)),
        out_shape=jax.ShapeDtypeStruct((rows, D_FF), bf16),
        compiler_params=_cparams(("parallel", "parallel")),
    )(u_pre, u_pre, u_pre, u_pre, w, w, b, b)


def _ffn_conv_bwd_act(u_pre, w, b, dact):
    rows = u_pre.shape[0]
    tr, nr, r8 = _conv_tiles(rows, TCONV_R // 2)
    tc = TCONV_FF
    nct = D_FF // tc

    def body(xg_ref, pg_ref, xu_ref, pu_ref, wg_ref, wu_ref, bg_ref, bu_ref, d_ref,
             du_ref, dbg_ref, dbu_ref):
        i = pl.program_id(1)
        first = i == 0
        ug = _conv(xg_ref[...], jnp.where(first, 0.0, pg_ref[...]), wg_ref[...]) + bg_ref[...]
        uu = _conv(xu_ref[...], jnp.where(first, 0.0, pu_ref[...]), wu_ref[...]) + bu_ref[...]
        d = d_ref[...]
        dug = d * uu * _dsilu(ug)
        duu = d * _silu(ug)
        du_ref[0] = dug
        du_ref[1] = duu

        @pl.when(first)
        def _():
            dbg_ref[...] = jnp.zeros_like(dbg_ref)
            dbu_ref[...] = jnp.zeros_like(dbu_ref)

        dbg_ref[...] += jnp.sum(dug, axis=0, keepdims=True)
        dbu_ref[...] += jnp.sum(duu, axis=0, keepdims=True)

    def x_spec(off):
        return pl.BlockSpec((tr, tc), lambda j, i: (i, off + j))

    def w_spec(k, off):
        return pl.BlockSpec((k, tc), lambda j, i: (0, off + j))

    tile = pl.BlockSpec((tr, tc), lambda j, i: (i, j))
    vec = pl.BlockSpec((1, tc), lambda j, i: (0, j))
    return pl.pallas_call(
        body, name="ffn_conv_bwd_act", grid=(nct, nr),
        in_specs=[x_spec(0), _prev_spec(tc, 0, r8), x_spec(nct), _prev_spec(tc, nct, r8),
                  w_spec(FFN_CONV_WIDTH, 0), w_spec(FFN_CONV_WIDTH, nct), w_spec(1, 0), w_spec(1, nct), tile],
        out_specs=[pl.BlockSpec((2, tr, tc), lambda j, i: (0, i, j)), vec, vec],
        out_shape=[jax.ShapeDtypeStruct((2, rows, D_FF), f32),
                   jax.ShapeDtypeStruct((1, D_FF), f32), jax.ShapeDtypeStruct((1, D_FF), f32)],
        compiler_params=_cparams(("parallel", "arbitrary")),
    )(u_pre, u_pre, u_pre, u_pre, w, w, b, b, dact)


def _conv_bwd(dy, x, x_cb0, w, name):
    k_taps = w.shape[0]
    split = dy.ndim == 3
    rows = dy.shape[-2]
    ch = dy.shape[-1] * (2 if split else 1)
    tc = TCONV_FF if split else TCONV_C
    tr, nr, r8 = _conv_tiles(rows, TCONV_R // 2 if split else TCONV_R)
    per_half = dy.shape[-1] // tc
    last8 = rows // SUBLANES - 1

    def body(dy_ref, nx_ref, x_ref, p_ref, w_ref, dx_ref, dw_ref):
        i = pl.program_id(1)
        dyv = dy_ref[...]
        nxt = jnp.where(i == nr - 1, 0.0, nx_ref[...])
        prev = jnp.where(i == 0, 0.0, p_ref[...])
        xv = x_ref[...].astype(f32)
        wv = w_ref[...]

        @pl.when(i == 0)
        def _():
            dw_ref[...] = jnp.zeros_like(dw_ref)

        dx = dyv * wv[k_taps - 1:k_taps, :]
        dw_ref[k_taps - 1:k_taps, :] += jnp.sum(dyv * xv, axis=0, keepdims=True)
        for j in range(1, k_taps):
            dx = dx + _shift_up(dyv, nxt, j) * wv[k_taps - 1 - j:k_taps - j, :]
            dw_ref[k_taps - 1 - j:k_taps - j, :] += jnp.sum(dyv * _shift_down(xv, prev, j), axis=0, keepdims=True)
        dx_ref[...] = dx.astype(dx_ref.dtype)

    tile = pl.BlockSpec((tr, tc), lambda j, i: (i, j))
    if split:
        dy_spec = pl.BlockSpec((None, tr, tc), lambda j, i: (j // per_half, i, j % per_half))
        next_spec = pl.BlockSpec((None, SUBLANES, tc),
                                 lambda j, i: (j // per_half, jnp.minimum((i + 1) * r8, last8), j % per_half))
    else:
        dy_spec = tile
        next_spec = pl.BlockSpec((SUBLANES, tc), lambda j, i: (jnp.minimum((i + 1) * r8, last8), j))
    return pl.pallas_call(
        body, name=name, grid=(ch // tc, nr),
        in_specs=[dy_spec, next_spec,
                  pl.BlockSpec((tr, tc), lambda j, i: (i, x_cb0 + j)), _prev_spec(tc, x_cb0, r8),
                  pl.BlockSpec((k_taps, tc), lambda j, i: (0, j))],
        out_specs=[tile, pl.BlockSpec((k_taps, tc), lambda j, i: (0, j))],
        out_shape=[jax.ShapeDtypeStruct((rows, ch), bf16), jax.ShapeDtypeStruct((k_taps, ch), f32)],
        compiler_params=_cparams(("parallel", "arbitrary")),
    )(dy, dy, x, x, w)


def _hdot(a, b):
    return jnp.dot(a, b, preferred_element_type=f32, precision=lax.Precision.HIGH)


def _xdot(a, b):
    return jnp.dot(a, b, preferred_element_type=f32, precision=lax.Precision.HIGHEST)


def _bdot(a, b):
    return jnp.dot(a.astype(bf16), b.astype(bf16), preferred_element_type=f32)


def _bdot_nt(a, b):
    return lax.dot_general(a.astype(bf16), b.astype(bf16), (((1,), (1,)), ((), ())), preferred_element_type=f32)


def _bdot_tn(a, b):
    return lax.dot_general(a.astype(bf16), b.astype(bf16), (((0,), (0,)), ((), ())), preferred_element_type=f32)


GDN_GROUP = 4
GDN_NGROUPS = DN_HEADS // GDN_GROUP
GDN_ROWS = GDN_GROUP * DN_CHUNK
GDN_QK_LANES = GDN_GROUP * DN_KEY_DIM
GDN_LOGIT_LANE = DN_HEADS


def _inverse_impl(lows):
    n = lows[0].shape[0]
    r = lax.broadcasted_iota(jnp.int32, (n, n), 0)
    c = lax.broadcasted_iota(jnp.int32, (n, n), 1)
    eye = (r == c).astype(f32)
    blk = jnp.right_shift(r, 3) == jnp.right_shift(c, 3)
    d = [jnp.where(blk, low, 0.0) for low in lows]
    e = [low - x for low, x in zip(lows, d)]

    def nilpotent8_inverse(xs):
        acc = [eye - x for x in xs]
        power = xs
        for _ in range(2):
            power = [_bdot(x, x) for x in power]
            acc = [_bdot(a, eye + x) for a, x in zip(acc, power)]
        return acc

    dinv = nilpotent8_inverse(d)
    ninv = nilpotent8_inverse([_bdot(x, y) for x, y in zip(dinv, e)])
    t = [_bdot(x, y) for x, y in zip(ninv, dinv)]
    for _ in range(2):
        res = [eye - x - _hdot(low, x) for low, x in zip(lows, t)]
        t = [x + _bdot(x, y) for x, y in zip(t, res)]
    return tuple(t)


@jax.custom_vjp
def _unit_lower_inverses(lows):
    return _inverse_impl(lows)


def _unit_lower_inverses_fwd(lows):
    t = _inverse_impl(lows)
    return t, t


def _unit_lower_inverses_bwd(t, ct):
    tn = (((0,), (0,)), ((), ()))
    nt = (((1,), (1,)), ((), ()))
    left = [lax.dot_general(x, g, tn, preferred_element_type=f32, precision=lax.Precision.HIGH) for x, g in zip(t, ct)]
    return (tuple(-lax.dot_general(x, y, nt, preferred_element_type=f32, precision=lax.Precision.HIGH)
                  for x, y in zip(left, t)),)


_unit_lower_inverses.defvjp(_unit_lower_inverses_fwd, _unit_lower_inverses_bwd)


@jax.custom_vjp
def _known_inverses(lows, t):
    return t


def _known_inverses_fwd(lows, t):
    return t, t


def _known_inverses_bwd(t, ct):
    return _unit_lower_inverses_bwd(t, ct) + (tuple(jnp.zeros_like(x) for x in t),)


_known_inverses.defvjp(_known_inverses_fwd, _known_inverses_bwd)


def _gdn_chunk(a_log, dt_bias, norm_w, ba, *per_group, inverses=None, keep_inverses=False):
    ng = GDN_NGROUPS
    qgs, kgs, vsts, zsts, states = [per_group[i * ng:(i + 1) * ng] for i in range(5)]
    groups = range(ng)
    n = GDN_ROWS
    r = lax.broadcasted_iota(jnp.int32, (n, n), 0)
    c = lax.broadcasted_iota(jnp.int32, (n, n), 1)
    same_head = jnp.right_shift(r, 6) == jnp.right_shift(c, 6)
    incl = jnp.logical_and(same_head, r >= c)
    strict = jnp.logical_and(same_head, r > c)
    eye = (r == c).astype(f32)
    ones = jnp.ones((n, n), f32)
    own_lanes = same_head.astype(f32)
    lane = lax.broadcasted_iota(jnp.int32, (1, LANES), 1)
    pick = lambda arr, idx: jnp.sum(jnp.where(lane == idx, arr, 0.0), axis=1, keepdims=True)
    heads = [[GDN_GROUP * g + h for h in range(GDN_GROUP)] for g in groups]
    rc = lax.broadcasted_iota(jnp.int32, (DN_CHUNK, DN_CHUNK), 0)
    cc = lax.broadcasted_iota(jnp.int32, (DN_CHUNK, DN_CHUNK), 1)

    g_all = -jnp.exp(a_log) * _softplus(ba + dt_bias)
    gc_all = _xdot((rc >= cc).astype(f32), g_all)
    gl_all = jnp.sum(g_all, axis=0, keepdims=True)
    beta = [jnp.concatenate([jax.nn.sigmoid(pick(ba, hd)) for hd in heads[g]], axis=0) for g in groups]
    gc = [jnp.concatenate([pick(gc_all, GDN_LOGIT_LANE + hd) for hd in heads[g]], axis=0) for g in groups]
    g_last = [jnp.concatenate([jnp.broadcast_to(pick(gl_all, GDN_LOGIT_LANE + hd), (DN_CHUNK, 1)) for hd in heads[g]],
                              axis=0) for g in groups]
    gr = [jnp.broadcast_to(gc[g], (n, n)).T for g in groups]
    decay = [jnp.where(incl, jnp.exp(jnp.where(incl, gc[g] - gr[g], 0.0)), 0.0) for g in groups]
    q = [jnp.concatenate([qgs[g]] * GDN_GROUP, axis=0) * own_lanes for g in groups]
    k = [jnp.concatenate([kgs[g]] * GDN_GROUP, axis=0) * own_lanes for g in groups]
    qn = [x * lax.rsqrt(jnp.sum(x * x, axis=1, keepdims=True) + L2_EPS) * (DN_KEY_DIM ** -0.5) for x in q]
    kn = [x * lax.rsqrt(jnp.sum(x * x, axis=1, keepdims=True) + L2_EPS) for x in k]
    kb = [kn[g] * beta[g] for g in groups]
    low = [jnp.where(strict, _bdot_nt(kb[g], kn[g]) * decay[g], 0.0) for g in groups]
    intra = [jnp.where(incl, _bdot_nt(qn[g], kn[g]) * decay[g], 0.0) for g in groups]
    t = _unit_lower_inverses(tuple(low)) if inverses is None else _known_inverses(tuple(low), tuple(inverses))
    u = [_bdot(t[g], vsts[g] * beta[g]) for g in groups]
    w = [_bdot(t[g], kb[g] * jnp.exp(gc[g])) for g in groups]
    sb = [s.astype(bf16) for s in states]
    v_new = [u[g] - jnp.dot(w[g].astype(bf16), sb[g], preferred_element_type=f32) for g in groups]
    o = [jnp.dot((qn[g] * jnp.exp(gc[g])).astype(bf16), sb[g], preferred_element_type=f32) for g in groups]
    o = [o[g] + _bdot(intra[g], v_new[g]) for g in groups]
    new_state = [states[g] * jnp.exp(g_last[g]) + _bdot_tn(kn[g] * jnp.exp(g_last[g] - gc[g]), v_new[g])
                 for g in groups]
    o_n = [x * lax.rsqrt(jnp.mean(x * x, axis=1, keepdims=True) + NORM_EPS) * norm_w for x in o]
    return tuple(o_n[g] * _silu(zsts[g]) for g in groups) + tuple(new_state) + (tuple(t) if keep_inverses else ())


def _gdn_specs(rows, reverse):
    n = rows // DN_CHUNK
    idx = (lambda i: n - 1 - i) if reverse else (lambda i: i)
    vec = pl.BlockSpec((1, LANES), lambda i: (0, 0))
    qkv = pl.BlockSpec((DN_CHUNK, DN_CONV_CH), lambda i: (idx(i), 0))
    z = pl.BlockSpec((DN_CHUNK, DN_V_WIDTH), lambda i: (idx(i), OFF_Z // DN_V_WIDTH))
    ba = pl.BlockSpec((DN_CHUNK, LANES), lambda i: (idx(i), 0))
    wide = pl.BlockSpec((DN_CHUNK, DN_V_WIDTH), lambda i: (idx(i), 0))
    st = pl.BlockSpec((1, DN_HEADS * DN_KEY_DIM, LANES), lambda i: (idx(i), 0, 0))
    inv = pl.BlockSpec((1, GDN_NGROUPS * GDN_ROWS, GDN_ROWS), lambda i: (idx(i), 0, 0))
    return n, vec, qkv, z, ba, wide, st, inv


def _gdn_slices(grp):
    q = slice(grp * GDN_QK_LANES, (grp + 1) * GDN_QK_LANES)
    k = slice(DN_QK_WIDTH + grp * GDN_QK_LANES, DN_QK_WIDTH + (grp + 1) * GDN_QK_LANES)
    heads = [slice((GDN_GROUP * grp + h) * LANES, (GDN_GROUP * grp + h + 1) * LANES) for h in range(GDN_GROUP)]
    vs = [slice(2 * DN_QK_WIDTH + s.start, 2 * DN_QK_WIDTH + s.stop) for s in heads]
    return q, k, vs, heads


def _stack_cols(ref, cols):
    return jnp.concatenate([ref[:, s] for s in cols], axis=0)


def _gdn_operands(qkv_ref, z_ref, state_rows):
    sl = [_gdn_slices(grp) for grp in range(GDN_NGROUPS)]
    return ([qkv_ref[:, q] for q, _, _, _ in sl] + [qkv_ref[:, k] for _, k, _, _ in sl]
            + [_stack_cols(qkv_ref, vs) for _, _, vs, _ in sl] + [_stack_cols(z_ref, heads) for _, _, _, heads in sl]
            + [state_rows[grp * GDN_ROWS:(grp + 1) * GDN_ROWS, :] for grp in range(GDN_NGROUPS)])


def _gdn_fwd(a_log, dt_bias, norm_w, qkv_act, proj, ba):
    rows = qkv_act.shape[0]
    n, vec, qkv_s, z_s, ba_s, wide, st_s, inv_s = _gdn_specs(rows, False)

    def body(al_ref, dt_ref, nw_ref, qkv_ref, z_ref, ba_ref, o_ref, st_ref, inv_ref, state):
        @pl.when(pl.program_id(0) == 0)
        def _():
            state[...] = jnp.zeros_like(state)

        st_ref[0] = state[...]
        out = _gdn_chunk(al_ref[...], dt_ref[...], nw_ref[...], ba_ref[...], *_gdn_operands(qkv_ref, z_ref, state),
                         keep_inverses=True)
        for grp in range(GDN_NGROUPS):
            _, _, _, heads = _gdn_slices(grp)
            rs = slice(grp * GDN_ROWS, (grp + 1) * GDN_ROWS)
            for h, s in enumerate(heads):
                o_ref[:, s] = out[grp][h * DN_CHUNK:(h + 1) * DN_CHUNK].astype(o_ref.dtype)
            state[rs, :] = out[GDN_NGROUPS + grp]
            inv_ref[0, rs, :] = out[2 * GDN_NGROUPS + grp]

    return pl.pallas_call(
        body, name="gdn_fwd", grid=(n,),
        in_specs=[vec, vec, vec, qkv_s, z_s, ba_s], out_specs=[wide, st_s, inv_s],
        out_shape=[jax.ShapeDtypeStruct((rows, DN_V_WIDTH), bf16),
                   jax.ShapeDtypeStruct((n, DN_HEADS * DN_KEY_DIM, LANES), f32),
                   jax.ShapeDtypeStruct((n, GDN_NGROUPS * GDN_ROWS, GDN_ROWS), f32)],
        scratch_shapes=[pltpu.VMEM((DN_HEADS * DN_KEY_DIM, LANES), f32)],
        compiler_params=_cparams(("arbitrary",)),
    )(a_log, dt_bias, norm_w, qkv_act, proj, ba)


def _gdn_bwd(a_log, dt_bias, norm_w, qkv_act, proj, ba, states, inverses, do):
    rows = qkv_act.shape[0]
    n, vec, qkv_s, z_s, ba_s, wide, st_s, inv_s = _gdn_specs(rows, True)

    def body(al_ref, dt_ref, nw_ref, qkv_ref, z_ref, ba_ref, st_ref, inv_ref, do_ref,
             dal_ref, ddt_ref, dnw_ref, dqkv_ref, dz_ref, dba_ref, dstate):
        @pl.when(pl.program_id(0) == 0)
        def _():
            dstate[...] = jnp.zeros_like(dstate)
            dal_ref[...] = jnp.zeros_like(dal_ref)
            ddt_ref[...] = jnp.zeros_like(ddt_ref)
            dnw_ref[...] = jnp.zeros_like(dnw_ref)

        ng = GDN_NGROUPS
        kept = [inv_ref[0, grp * GDN_ROWS:(grp + 1) * GDN_ROWS, :] for grp in range(ng)]
        _, vjp = jax.vjp(functools.partial(_gdn_chunk, inverses=kept), al_ref[...], dt_ref[...], nw_ref[...],
                         ba_ref[...], *_gdn_operands(qkv_ref, z_ref, st_ref[0]))
        cts = tuple(_stack_cols(do_ref, _gdn_slices(grp)[3]) for grp in range(ng))
        cts += tuple(dstate[grp * GDN_ROWS:(grp + 1) * GDN_ROWS, :] for grp in range(ng))
        grads = vjp(cts)
        dal_ref[...] += grads[0]
        ddt_ref[...] += grads[1]
        dnw_ref[...] += grads[2]
        dba_ref[...] = grads[3]
        dqs, dks, dvs, dzs, dss = [grads[4 + i * ng:4 + (i + 1) * ng] for i in range(5)]
        for grp in range(ng):
            q, k, vs, heads = _gdn_slices(grp)
            dqkv_ref[:, q] = dqs[grp]
            dqkv_ref[:, k] = dks[grp]
            for h, (sv, sh) in enumerate(zip(vs, heads)):
                rows_h = slice(h * DN_CHUNK, (h + 1) * DN_CHUNK)
                dqkv_ref[:, sv] = dvs[grp][rows_h]
                dz_ref[:, sh] = dzs[grp][rows_h].astype(dz_ref.dtype)
            dstate[grp * GDN_ROWS:(grp + 1) * GDN_ROWS, :] = dss[grp]

    return pl.pallas_call(
        body, name="gdn_bwd", grid=(n,),
        in_specs=[vec, vec, vec, qkv_s, z_s, ba_s, st_s, inv_s, wide],
        out_specs=[vec, vec, vec, qkv_s, wide, ba_s],
        out_shape=[jax.ShapeDtypeStruct((1, LANES), f32)] * 3
        + [jax.ShapeDtypeStruct((rows, DN_CONV_CH), f32), jax.ShapeDtypeStruct((rows, DN_V_WIDTH), bf16),
           jax.ShapeDtypeStruct((rows, LANES), f32)],
        scratch_shapes=[pltpu.VMEM((DN_HEADS * DN_KEY_DIM, LANES), f32)],
        compiler_params=_cparams(("arbitrary",)),
    )(a_log, dt_bias, norm_w, qkv_act, proj, ba, states, inverses, do)


def _ada_fwd(c_all, w_loc, b_loc):
    def body(c_ref, w_ref, b_ref, o_ref):
        o_ref[...] = _bdot(_silu(c_ref[...]), w_ref[...]) + b_ref[...]

    return pl.pallas_call(body, name="ada_fwd", out_shape=jax.ShapeDtypeStruct((c_all.shape[0], w_loc.shape[1]), f32),
                          compiler_params=_cparams())(c_all, w_loc, b_loc)


def _ada_bwd(c_all, dmod_cols):
    def body(c_ref, d_ref, o_ref):
        o_ref[...] = _bdot_tn(_silu(c_ref[...]), d_ref[...])

    return pl.pallas_call(body, name="ada_bwd",
                          out_shape=jax.ShapeDtypeStruct((c_all.shape[1], dmod_cols.shape[1]), f32),
                          compiler_params=_cparams())(c_all, dmod_cols)


def _sum_devices(parts):
    def body(p_ref, o_ref):
        acc = p_ref[0:1, :]
        for d in range(1, N_DEV):
            acc = acc + p_ref[d:d + 1, :]
        o_ref[...] = acc

    return pl.pallas_call(body, name="sum_small", out_shape=jax.ShapeDtypeStruct((1, parts.shape[1]), f32),
                          compiler_params=_cparams())(parts)


def _adam_math(w, g, m, v):
    m2 = ADAM_B1 * m + (1.0 - ADAM_B1) * g
    v2 = ADAM_B2 * v + (1.0 - ADAM_B2) * jnp.square(g)
    m_hat = m2 / (1.0 - ADAM_B1 ** ADAM_STEP)
    v_hat = v2 / (1.0 - ADAM_B2 ** ADAM_STEP)
    delta = -ADAM_LR * (m_hat / (jnp.sqrt(v_hat) + ADAM_EPS) + ADAM_WD * w)
    return delta, m2, v2


def _row_tile(rows):
    return _pick(rows, (256, 128, 64, 32, 16, 8))


def _adamw(w, g, m, v, name):
    rows, cols = w.shape
    tr = _row_tile(rows)

    def body(w_ref, g_ref, m_ref, v_ref, d_ref, m2_ref, v2_ref):
        d_ref[...], m2_ref[...], v2_ref[...] = _adam_math(w_ref[...], g_ref[...], m_ref[...], v_ref[...])

    tile = pl.BlockSpec((tr, cols), lambda i: (i, 0))
    return pl.pallas_call(body, name=name, grid=(rows // tr,), in_specs=[tile] * 4, out_specs=[tile] * 3,
                          out_shape=[jax.ShapeDtypeStruct(w.shape, f32)] * 3,
                          compiler_params=_cparams(("parallel",)))(w, g, m, v)


def _sum_adamw(parts, w, m, v, name):
    rows, cols = w.shape
    tr = _row_tile(rows)

    def body(p_ref, w_ref, m_ref, v_ref, g_ref, d_ref, m2_ref, v2_ref):
        g = p_ref[0].astype(f32)
        for d in range(1, N_DEV):
            g = g + p_ref[d].astype(f32)
        g_ref[...] = g
        d_ref[...], m2_ref[...], v2_ref[...] = _adam_math(w_ref[...], g, m_ref[...], v_ref[...])

    tile = pl.BlockSpec((tr, cols), lambda i: (i, 0))
    return pl.pallas_call(body, name=name, grid=(rows // tr,),
                          in_specs=[pl.BlockSpec((N_DEV, tr, cols), lambda i: (0, i, 0)), tile, tile, tile],
                          out_specs=[tile] * 4, out_shape=[jax.ShapeDtypeStruct(w.shape, f32)] * 4,
                          compiler_params=_cparams(("parallel",)))(parts, w, m, v)


def _pad_lanes(a, width):
    return jnp.pad(a, ((0, 0), (0, width - a.shape[1])))


def _cols_by_device(full):
    r, c = full.shape
    return jnp.moveaxis(full.reshape(r, N_DEV, c // N_DEV), 1, 0)


def _cols_from_devices(parts):
    d, r, n = parts.shape
    return jnp.moveaxis(parts, 0, 1).reshape(r, d * n)


def kernel(x, c, w_ada, b_ada, norm1_w, w_in, dn_conv_w, dn_A_log, dn_dt_bias, dn_norm_w, w_proj_sb, w_proj_dn, w_out, norm2_w, w_ffn_in, ffn_conv_w, ffn_conv_b, w_ffn_out, final_norm_w, loss_target, m_w_ada, m_b_ada, m_norm1_w, m_w_in, m_dn_conv_w, m_dn_A_log, m_dn_dt_bias, m_dn_norm_w, m_w_proj_sb, m_w_proj_dn, m_w_out, m_norm2_w, m_w_ffn_in, m_ffn_conv_w, m_ffn_conv_b, m_w_ffn_out, m_final_norm_w, v_w_ada, v_b_ada, v_norm1_w, v_w_in, v_dn_conv_w, v_dn_A_log, v_dn_dt_bias, v_dn_norm_w, v_w_proj_sb, v_w_proj_dn, v_w_out, v_norm2_w, v_w_ffn_in, v_ffn_conv_w, v_ffn_conv_b, v_w_ffn_out, v_final_norm_w):
    d = D_MODEL
    me = 4 * lax.axis_index("x") + 2 * lax.axis_index("y") + lax.axis_index("c")
    xs = x[0]
    target = loss_target[0]
    n_ada = w_ada.shape[2]
    n_dnc = dn_conv_w.shape[2]
    n_ffc = ffn_conv_w.shape[2]

    small = jnp.concatenate([c, dn_conv_w[0].reshape(1, -1), ffn_conv_w[0].reshape(1, -1)], axis=1)
    small = _pad_lanes(small, -(-small.shape[1] // LANES) * LANES)
    small_g, w_in_g = _all_gather([small, w_in[0].astype(bf16)], "gather_w_in")
    later = [w_proj_sb[0].astype(bf16), w_proj_dn[0].astype(bf16), w_out[0].astype(bf16),
             w_ffn_in[0].astype(bf16), w_ffn_out[0].astype(bf16)]
    gather_later = _SideComm(_gather_protocol, later, _gathered_shapes(later))
    small_g = small_g[:, 0, :]
    c_all = small_g[:, :d]
    dn_cw = _cols_from_devices(small_g[:, d:d + DN_CONV_WIDTH * n_dnc].reshape(N_DEV, DN_CONV_WIDTH, n_dnc))
    o2 = d + DN_CONV_WIDTH * n_dnc
    ffn_cw = _cols_from_devices(small_g[:, o2:o2 + FFN_CONV_WIDTH * n_ffc].reshape(N_DEV, FFN_CONV_WIDTH, n_ffc))

    w_in_full = _cols_from_devices(w_in_g)
    r_sb, r_dn, r_z = 3 * SB_WIDTH, 3 * SB_WIDTH + DN_CONV_CH, 3 * SB_WIDTH + DN_CONV_CH + DN_V_WIDTH
    r_g = r_z + 2 * DN_HEADS
    w_main = jnp.concatenate([w_in_full[:, r_g:], w_in_full[:, r_sb:r_dn], w_in_full[:, r_dn:r_z],
                              w_in_full[:, :r_sb]], axis=1)
    w_ba = _pad_lanes(w_in_full[:, r_z:r_g], LANES)

    b_loc = lax.dynamic_slice(b_ada, (0, me * n_ada), (1, n_ada))
    mod_part = _ada_fwd(c_all, w_ada[0], b_loc)
    (mod_g,) = _all_gather([mod_part], "gather_mod")
    mod = lax.dynamic_index_in_dim(mod_g, me, axis=1, keepdims=False).reshape(1, N_DEV * n_ada)
    shift1, scale1, gate1, shift2, scale2, gate2 = [mod[:, i * d:(i + 1) * d] for i in range(6)]

    logit_lanes = ((0, 0), (GDN_LOGIT_LANE, LANES - GDN_LOGIT_LANE - DN_HEADS))
    a_log = jnp.pad(dn_A_log, logit_lanes)
    dt_b = jnp.pad(dn_dt_bias, logit_lanes)

    (h,) = _stage_fwd(_f_normmod, [norm1_w, shift1, scale1], [xs], [bf16], "norm1_fwd")
    proj = _mm(h, w_main, name="in_proj")
    ba = _mm(h, w_ba, name="in_proj_ba")
    k16, k0_16, k1_16, v16, v0_16, v1_16 = _sb_prepare(proj)
    o_a, sb_runs, w_psb_g, w_pdn_g, w_out_g, w_fin_g, w_fout_g = _sb_attention_fwd2(
        proj, k16, v0_16, v1_16, side=gather_later)
    w_psb = _cols_from_devices(w_psb_g)
    w_pdn = w_pdn_g.reshape(DN_V_WIDTH, d)
    w_o = w_out_g.reshape(d, d)
    w_fin = _cols_from_devices(w_fin_g)
    w_fout = w_fout_g.reshape(D_FF, d)
    qkv_act = _dn_conv_fwd(proj, dn_cw)
    o_b, states, dn_inverses = _gdn_fwd(a_log, dt_b, dn_norm_w, qkv_act, proj, ba)
    pa = _mm(o_a, w_psb, name="proj_sb")
    pb = _mm(o_b, w_pdn, name="proj_dn")
    gates = [(proj, d, OFF_GA // d), (proj, d, OFF_GB // d)]
    (merged,) = _stage_fwd(_f_merge, [], gates + [pa, pb], [bf16], "merge_fwd")
    ao = _mm(merged, w_o, name="out_proj")
    mid_params = [gate1, norm2_w, shift2, scale2]
    x1, h2 = _stage_fwd(_f_residual_normmod, mid_params, [xs, ao], [f32, bf16], "resid1_norm2_fwd")
    u_pre = _mm(h2, w_fin, name="ffn_in")
    act = _ffn_conv_fwd(u_pre, ffn_cw, ffn_conv_b)
    fo = _mm(act, w_fout, name="ffn_out")

    loss_p, d_gate2, d_wf, dx2, dfo = _loss_and_grads(gate2, final_norm_w.reshape(1, d), x1, fo, target)
    dact = _mm(dfo, w_fout, tb=True, name="ffn_out_dx")
    g_w_fout = _mm(act, dfo, ta=True, name="ffn_out_dw")
    du, dbg, dbu = _ffn_conv_bwd_act(u_pre, ffn_cw, ffn_conv_b, dact)
    du_pre, d_ffn_cw = _conv_bwd(du, u_pre, 0, ffn_cw, "ffn_conv_bwd")
    dh2 = _mm(du_pre, w_fin, tb=True, name="ffn_in_dx")
    g_w_fin = _mm(h2, du_pre, ta=True, name="ffn_in_dw")
    (d_gate1, d_n2w, d_shift2, d_scale2), (dx1, dao) = _stage_bwd(
        _f_residual_normmod, mid_params, [xs, ao], [dx2, dh2], [f32, bf16], "resid1_norm2_bwd")
    dmerged = _mm(dao, w_o, tb=True, name="out_proj_dx")
    g_w_o = _mm(merged, dao, ta=True, name="out_proj_dw")
    _, (dga, dgb, dpa, dpb) = _stage_bwd(_f_merge, [], gates + [pa, pb], [dmerged], [bf16] * 4, "merge_bwd")
    do_a = _mm(dpa, w_psb, tb=True, name="proj_sb_dx")
    g_w_psb = _mm(o_a, dpa, ta=True, name="proj_sb_dw")
    do_b = _mm(dpb, w_pdn, tb=True, name="proj_dn_dx")
    g_w_pdn = _mm(o_b, dpb, ta=True, name="proj_dn_dw")
    early = [_cols_by_device(g_w_psb).astype(bf16),
             g_w_pdn.reshape(N_DEV, DN_V_WIDTH // N_DEV, d).astype(bf16),
             g_w_o.reshape(N_DEV, d // N_DEV, d).astype(bf16),
             _cols_by_device(g_w_fin).astype(bf16),
             g_w_fout.reshape(N_DEV, D_FF // N_DEV, d).astype(bf16)]
    exchange_early = _SideComm(_exchange_protocol, early, [jax.ShapeDtypeStruct(a.shape, a.dtype) for a in early])
    dq, dk, dv, *recv_early = _sb_attention_bwd2(proj, k16, k0_16, k1_16, v16, sb_runs, do_a, side=exchange_early)
    d_alog, d_dtb, d_dnw, dqkv_act, dz, dba = _gdn_bwd(a_log, dt_b, dn_norm_w, qkv_act, proj, ba, states,
                                                       dn_inverses, do_b)
    d_conv_out = _dn_conv_bwd_act(proj, dn_cw, dqkv_act)
    d_dn_pre, d_dn_cw = _conv_bwd(d_conv_out, proj, OFF_DN // TCONV_C, dn_cw, "dn_conv_bwd")
    dproj = jnp.concatenate([dga, dgb, d_dn_pre, dz, dq, dk.astype(bf16), dv.astype(bf16)], axis=1)
    g_w_main = _mm(h, dproj, ta=True, name="in_proj_dw")
    g_w_ba = _mm(h, dba, ta=True, name="in_proj_ba_dw")
    g_w_in_full = jnp.concatenate([g_w_main[:, OFF_SBQ:], g_w_main[:, OFF_DN:OFF_Z], g_w_main[:, OFF_Z:OFF_SBQ],
                                   g_w_ba[:, :2 * DN_HEADS], g_w_main[:, :OFF_DN]], axis=1)
    w_in_parts = _cols_by_device(g_w_in_full).astype(bf16)
    exchange_w_in = _SideComm(_exchange_protocol, [w_in_parts], [jax.ShapeDtypeStruct(w_in_parts.shape, bf16)])
    dh, recv_w_in = _mm(dproj, w_main, tb=True, name="in_proj_dx", side=exchange_w_in)
    dh_ba = _mm(dba, w_ba, tb=True, name="in_proj_ba_dx")
    (d_n1w, d_shift1, d_scale1), (grad_x,) = _stage_bwd(
        _f_normmod, [norm1_w, shift1, scale1], [xs], [[dh, dh_ba]], [f32], "norm1_bwd", residual=(0, dx1))

    dmod = jnp.concatenate([d_shift1, d_scale1, d_gate1, d_shift2, d_scale2, d_gate2], axis=1)
    d_ffn_cb = jnp.concatenate([dbg, dbu], axis=1)
    small_parts = jnp.concatenate(
        [loss_p, dmod, d_n1w, d_alog, d_dtb, d_dnw, d_n2w, d_ffn_cb, d_wf,
         d_dn_cw.reshape(1, -1), d_ffn_cw.reshape(1, -1)], axis=1)
    (small_parts_g,) = _all_gather([small_parts], "gather_small_grads")
    tot = _sum_devices(small_parts_g[:, 0, :])
    offs = {}
    pos = 0
    for nm, width in (("loss", LANES), ("b_ada", 6 * d), ("norm1_w", d), ("dn_A_log", LANES), ("dn_dt_bias", LANES),
                      ("dn_norm_w", LANES), ("norm2_w", d), ("ffn_conv_b", 2 * D_FF), ("final_norm_w", d),
                      ("dn_conv_w", DN_CONV_WIDTH * DN_CONV_CH), ("ffn_conv_w", FFN_CONV_WIDTH * 2 * D_FF)):
        offs[nm] = (pos, width)
        pos += width
    seg = lambda nm: tot[:, offs[nm][0]:offs[nm][0] + offs[nm][1]]
    loss = tot[0, 0]
    g_b_ada = seg("b_ada")
    g_norm1 = seg("norm1_w")
    g_alog = seg("dn_A_log")[:, GDN_LOGIT_LANE:GDN_LOGIT_LANE + DN_HEADS]
    g_dtb = seg("dn_dt_bias")[:, GDN_LOGIT_LANE:GDN_LOGIT_LANE + DN_HEADS]
    g_dnw = seg("dn_norm_w")
    g_norm2 = seg("norm2_w")
    g_ffn_cb = seg("ffn_conv_b")
    g_fnw = seg("final_norm_w")
    g_dn_cw = lax.dynamic_slice(seg("dn_conv_w").reshape(DN_CONV_WIDTH, DN_CONV_CH), (0, me * n_dnc),
                                (DN_CONV_WIDTH, n_dnc))
    g_ffn_cw = lax.dynamic_slice(seg("ffn_conv_w").reshape(FFN_CONV_WIDTH, 2 * D_FF), (0, me * n_ffc),
                                 (FFN_CONV_WIDTH, n_ffc))

    dmod_all = small_parts_g[:, 0, offs["b_ada"][0]:offs["b_ada"][0] + 6 * d]
    g_w_ada = _ada_bwd(c_all, lax.dynamic_slice(dmod_all, (0, me * n_ada), (N_DEV, n_ada)))

    def pack(parts):
        flat = [p.reshape(1, -1) for p in parts]
        flat = [_pad_lanes(p, -(-p.shape[1] // LANES) * LANES) for p in flat]
        return jnp.concatenate(flat, axis=1), [p.shape[1] for p in flat]

    small_names_g = [g_b_ada, g_norm1, g_alog, g_dtb, g_dnw, g_norm2, g_ffn_cb, g_fnw, g_dn_cw, g_ffn_cw]
    small_w = [b_ada, norm1_w, dn_A_log, dn_dt_bias, dn_norm_w, norm2_w, ffn_conv_b, final_norm_w, dn_conv_w[0], ffn_conv_w[0]]
    small_m = [m_b_ada, m_norm1_w, m_dn_A_log, m_dn_dt_bias, m_dn_norm_w, m_norm2_w, m_ffn_conv_b, m_final_norm_w, m_dn_conv_w[0], m_ffn_conv_w[0]]
    small_v = [v_b_ada, v_norm1_w, v_dn_A_log, v_dn_dt_bias, v_dn_norm_w, v_norm2_w, v_ffn_conv_b, v_final_norm_w, v_dn_conv_w[0], v_ffn_conv_w[0]]
    pg, widths = pack(small_names_g)
    pw, _ = pack(small_w)
    pm, _ = pack(small_m)
    pv, _ = pack(small_v)
    s_delta, s_m, s_v = _adamw(pw, pg, pm, pv, "adamw_small")

    def unpack(flat):
        out, pos = [], 0
        for ref_arr, width in zip(small_w, widths):
            out.append(flat[:, pos:pos + ref_arr.size].reshape(ref_arr.shape))
            pos += width
        return out

    small_grads = [g.reshape(w_.shape) for g, w_ in zip(small_names_g, small_w)]
    small_delta, small_newm, small_newv = unpack(s_delta), unpack(s_m), unpack(s_v)

    ada_delta, ada_m, ada_v = _adamw(w_ada[0], g_w_ada, m_w_ada[0], v_w_ada[0], "adamw_ada")

    recv = [recv_w_in] + list(recv_early)
    big = {}
    for nm, parts, w_, m_, v_ in (("w_in", recv[0], w_in, m_w_in, v_w_in),
                                  ("w_proj_sb", recv[1], w_proj_sb, m_w_proj_sb, v_w_proj_sb),
                                  ("w_proj_dn", recv[2], w_proj_dn, m_w_proj_dn, v_w_proj_dn),
                                  ("w_out", recv[3], w_out, m_w_out, v_w_out),
                                  ("w_ffn_in", recv[4], w_ffn_in, m_w_ffn_in, v_w_ffn_in),
                                  ("w_ffn_out", recv[5], w_ffn_out, m_w_ffn_out, v_w_ffn_out)):
        big[nm] = [t[None] for t in _sum_adamw(parts, w_[0], m_[0], v_[0], "adamw_" + nm)]

    sg = dict(zip(["b_ada", "norm1_w", "dn_A_log", "dn_dt_bias", "dn_norm_w", "norm2_w", "ffn_conv_b", "final_norm_w",
                   "dn_conv_w", "ffn_conv_w"], range(10)))

    def small_out(table, nm):
        val = table[sg[nm]]
        return val[None] if nm in ("dn_conv_w", "ffn_conv_w") else val

    order = ["w_ada", "b_ada", "norm1_w", "w_in", "dn_conv_w", "dn_A_log", "dn_dt_bias", "dn_norm_w", "w_proj_sb",
             "w_proj_dn", "w_out", "norm2_w", "w_ffn_in", "ffn_conv_w", "ffn_conv_b", "w_ffn_out", "final_norm_w"]
    groups = []
    for k, small_table in enumerate((small_grads, small_delta, small_newm, small_newv)):
        row = []
        for nm in order:
            if nm == "w_ada":
                row.append((g_w_ada, ada_delta, ada_m, ada_v)[k][None])
            elif nm in big:
                row.append(big[nm][k])
            else:
                row.append(small_out(small_table, nm))
        groups.append(row)
    return (loss, grad_x[None], *groups[0], *groups[1], *groups[2], *groups[3])
```

```python
import functools

import jax
import jax.numpy as jnp
from jax import lax
from jax.experimental import pallas as pl
from jax.experimental.pallas import tpu as pltpu

f32 = jnp.float32
bf16 = jnp.bfloat16

D_MODEL = 1024
SB_HEADS = 8
SB_HEAD_DIM = 64
SB_WIDTH = SB_HEADS * SB_HEAD_DIM
SB_QBLOCK = 128
DN_HEADS = 8
DN_KEY_DIM = 64
DN_VAL_DIM = 128
DN_QK_WIDTH = DN_HEADS * DN_KEY_DIM
DN_V_WIDTH = DN_HEADS * DN_VAL_DIM
DN_CONV_CH = 2 * DN_QK_WIDTH + DN_V_WIDTH
DN_CONV_WIDTH = 4
DN_CHUNK = 64
D_FF = 2816
FFN_CONV_WIDTH = 3
NORM_EPS = 1e-6
L2_EPS = 1e-6
ADAM_LR = 0.001
ADAM_B1 = 0.9
ADAM_B2 = 0.999
ADAM_EPS = 1e-08
ADAM_WD = 0.01
ADAM_STEP = 10

N_DEV = 8
MESH = pl.DeviceIdType.MESH

LANES = 128
SUBLANES = 8
VMEM_LIMIT = 48 * 1024 * 1024

OFF_GA = 0
OFF_GB = D_MODEL
OFF_DN = 2 * D_MODEL
OFF_Z = OFF_DN + DN_CONV_CH
OFF_SBQ = OFF_Z + DN_V_WIDTH
OFF_SBK = OFF_SBQ + SB_WIDTH
OFF_SBV = OFF_SBK + SB_WIDTH
MAIN_WIDTH = OFF_SBV + SB_WIDTH

TM = 256
TCONV_R = 512
TCONV_C = 512
TCONV_FF = D_FF // 2
SB_PAIRS_FWD = 4
SB_PAIRS_BWD = 4
SB_DEAD = -106.0
SB_NEVER = -1e30


def _cparams(sem=None):
    return pltpu.CompilerParams(dimension_semantics=sem, vmem_limit_bytes=VMEM_LIMIT)


def _pick(n, cands):
    for c in cands:
        if n % c == 0:
            return c
    return n


def _my_pos():
    return lax.axis_index("x"), lax.axis_index("y"), lax.axis_index("c")


def _flip(v, bit):
    return 1 - v if bit else v


def _comm_scratch(n):
    return [pltpu.SemaphoreType.DMA((n, 7)), pltpu.SemaphoreType.DMA((n, 7)), pltpu.SemaphoreType.DMA((n,))]


def _gather_protocol(ins, outs, send_sems, recv_sems, local_sems):
    n = len(ins)
    x, y, c = _my_pos()
    me, sibling = (x, y, c), (x, y, 1 - c)
    chips = [(1 - x, y), (x, 1 - y), (1 - x, 1 - y)]

    def slot(out, pos):
        return out.at[4 * pos[0] + 2 * pos[1] + pos[2]]

    def copy(a, k, block, to, src=None):
        return pltpu.make_async_remote_copy(
            src_ref=slot(outs[a], block) if src is None else src, dst_ref=slot(outs[a], block),
            send_sem=send_sems.at[a, k], recv_sem=recv_sems.at[a, k], device_id=to, device_id_type=MESH)

    def local(a):
        return pltpu.make_async_copy(ins[a], slot(outs[a], me), local_sems.at[a])

    def first(a):
        return [copy(a, 0, me, sibling, src=ins[a])] + [copy(a, 1 + j, me, (*chip, c), src=ins[a])
                                                         for j, chip in enumerate(chips)]

    def start():
        for a in range(n):
            local(a).start()
            for cp in first(a):
                cp.start()

    def finish():
        forwards = []
        for a in range(n):
            for j, chip in enumerate(chips):
                copy(a, 1 + j, (*chip, c), me).wait_recv()
                fwd = copy(a, 4 + j, (*chip, c), sibling)
                fwd.start()
                forwards.append(fwd)
        for a in range(n):
            copy(a, 0, sibling, me).wait_recv()
            for j, chip in enumerate(chips):
                copy(a, 4 + j, (*chip, 1 - c), me).wait_recv()
        for a in range(n):
            for cp in first(a):
                cp.wait_send()
        for cp in forwards:
            cp.wait_send()
        for a in range(n):
            local(a).wait()

    return start, finish


def _exchange_protocol(ins, outs, send_sems, recv_sems, local_sems):
    n = len(ins)
    x, y, c = _my_pos()
    me_idx = 4 * x + 2 * y + c

    def local(a):
        return pltpu.make_async_copy(ins[a].at[me_idx], outs[a].at[me_idx], local_sems.at[a])

    def copies(a, m):
        peer = (_flip(x, m & 4), _flip(y, m & 2), _flip(c, m & 1))
        peer_idx = 4 * peer[0] + 2 * peer[1] + peer[2]
        sems = dict(send_sem=send_sems.at[a, m - 1], recv_sem=recv_sems.at[a, m - 1], device_id=peer,
                    device_id_type=MESH)
        send = pltpu.make_async_remote_copy(src_ref=ins[a].at[peer_idx], dst_ref=outs[a].at[me_idx], **sems)
        recv = pltpu.make_async_remote_copy(src_ref=ins[a].at[peer_idx], dst_ref=outs[a].at[peer_idx], **sems)
        return send, recv

    def start():
        for a in range(n):
            local(a).start()
            for m in range(1, N_DEV):
                copies(a, m)[0].start()

    def finish():
        for a in range(n):
            for m in range(1, N_DEV):
                copies(a, m)[1].wait_recv()
        for a in range(n):
            for m in range(1, N_DEV):
                copies(a, m)[0].wait_send()
            local(a).wait()

    return start, finish


def _collective_call(protocol, arrs, out_shapes, name):
    n = len(arrs)

    def body(*refs):
        start, finish = protocol(refs[:n], refs[n:2 * n], *refs[2 * n:])
        start()
        finish()

    any_spec = pl.BlockSpec(memory_space=pl.ANY)
    return pl.pallas_call(body, name=name, out_shape=out_shapes, in_specs=[any_spec] * n, out_specs=[any_spec] * n,
                          scratch_shapes=_comm_scratch(n))(*arrs)


def _gathered_shapes(arrs):
    return [jax.ShapeDtypeStruct((N_DEV,) + a.shape, a.dtype) for a in arrs]


def _all_gather(arrs, name):
    return _collective_call(_gather_protocol, arrs, _gathered_shapes(arrs), name)


MM_BLOCK_BYTES = 4 * 1024 * 1024


def _mm_tiles(m_dim, n_dim, k_dim, a_bytes, b_bytes):
    tm = _pick(m_dim, (1024, 512, 256, 128))
    tn = _pick(n_dim, (1024, 512, 256, 128))
    tk = k_dim
    if k_dim % LANES == 0:
        units = k_dim // LANES
        fits = [u for u in range(1, units + 1) if units % u == 0
                and u * LANES * max(tm * a_bytes, tn * b_bytes) <= MM_BLOCK_BYTES]
        tk = max(fits) * LANES
    return tm, tn, tk


def _mm(a, b, *, ta=False, tb=False, name, side=None):
    (k_dim, m_dim) = a.shape if ta else a.shape[::-1]
    (n_dim, kb_dim) = b.shape if tb else b.shape[::-1]
    assert k_dim == kb_dim, (a.shape, b.shape, ta, tb)
    tm, tn, tk = _mm_tiles(m_dim, n_dim, k_dim, a.dtype.itemsize, b.dtype.itemsize)
    nk = k_dim // tk
    grid = (m_dim // tm, n_dim // tn, nk)
    dims = (((0 if ta else 1,), (1 if tb else 0,)), ((), ()))
    ns = side.n if side else 0

    def body(a_ref, b_ref, *rest):
        if side:
            side.run(rest[:ns], rest[ns + 1:2 * ns + 1], rest[2 * ns + 1:], *_grid_ends(grid),
                     lambda: compute(a_ref, b_ref, rest[ns]))
        else:
            compute(a_ref, b_ref, rest[0])

    def compute(a_ref, b_ref, o_ref):
        part = lax.dot_general(a_ref[...].astype(bf16), b_ref[...].astype(bf16), dims, preferred_element_type=f32)
        if nk == 1:
            o_ref[...] = part
        else:
            k = pl.program_id(2)

            @pl.when(k == 0)
            def _():
                o_ref[...] = part

            @pl.when(k > 0)
            def _():
                o_ref[...] += part

    a_spec = pl.BlockSpec((tk, tm), lambda i, j, k: (k, i)) if ta else pl.BlockSpec((tm, tk), lambda i, j, k: (i, k))
    b_spec = pl.BlockSpec((tn, tk), lambda i, j, k: (j, k)) if tb else pl.BlockSpec((tk, tn), lambda i, j, k: (k, j))
    out_spec = pl.BlockSpec((tm, tn), lambda i, j, k: (i, j))
    out_shape = jax.ShapeDtypeStruct((m_dim, n_dim), f32)
    if not side:
        return pl.pallas_call(body, name=name, grid=grid, in_specs=[a_spec, b_spec], out_specs=out_spec,
                              out_shape=out_shape,
                              compiler_params=_cparams(("parallel", "parallel", "arbitrary")))(a, b)
    return pl.pallas_call(
        body, name=name, grid=grid, in_specs=[a_spec, b_spec] + side.specs(), out_specs=[out_spec] + side.specs(),
        out_shape=[out_shape] + side.out_shapes, scratch_shapes=_comm_scratch(ns),
        compiler_params=_cparams(("arbitrary", "arbitrary", "arbitrary")))(a, b, *side.arrs)


def _win(t):
    return t if isinstance(t, tuple) else (t, t.shape[1], 0)


def _tile_spec(width, cb, tm):
    return pl.BlockSpec((tm, width), lambda i: (i, cb))


def _param_spec(p):
    return pl.BlockSpec(p.shape, lambda i: (0, 0))


def _stage_fwd(f, params, tiles, out_dtypes, name):
    tiles = [_win(t) for t in tiles]
    rows = tiles[0][0].shape[0]
    tm = min(TM, rows)
    avals = jax.eval_shape(f, *[jax.ShapeDtypeStruct(p.shape, f32) for p in params],
                           *[jax.ShapeDtypeStruct((tm, w), f32) for _, w, _ in tiles])
    n_p, n_t = len(params), len(tiles)

    def body(*refs):
        p = [r[...] for r in refs[:n_p]]
        t = [r[...].astype(f32) for r in refs[n_p:n_p + n_t]]
        for o_ref, val in zip(refs[n_p + n_t:], f(*p, *t)):
            o_ref[...] = val.astype(o_ref.dtype)

    return pl.pallas_call(
        body, name=name, grid=(rows // tm,),
        in_specs=[_param_spec(p) for p in params] + [_tile_spec(w, cb, tm) for _, w, cb in tiles],
        out_specs=[_tile_spec(a.shape[1], 0, tm) for a in avals],
        out_shape=[jax.ShapeDtypeStruct((rows, a.shape[1]), dt) for a, dt in zip(avals, out_dtypes)],
        compiler_params=_cparams(("parallel",)),
    )(*params, *[t[0] for t in tiles])


def _stage_bwd(f, params, tiles, cts, grad_dtypes, name, residual=None):
    tiles = [_win(t) for t in tiles]
    rows = tiles[0][0].shape[0]
    tm = min(TM, rows)
    cts = [list(g) if isinstance(g, (list, tuple)) else [g] for g in cts]
    flat_cts = [a for g in cts for a in g]
    n_p, n_t, n_c = len(params), len(tiles), len(flat_cts)
    has_res = residual is not None
    want = [j for j, dt in enumerate(grad_dtypes) if dt is not None]

    def body(*refs):
        i = pl.program_id(0)
        p = [r[...] for r in refs[:n_p]]
        t = [r[...].astype(f32) for r in refs[n_p:n_p + n_t]]
        ct_vals = [r[...].astype(f32) for r in refs[n_p + n_t:n_p + n_t + n_c]]
        ct, at = [], 0
        for g in cts:
            ct.append(functools.reduce(jnp.add, ct_vals[at:at + len(g)]))
            at += len(g)
        ct = tuple(ct)
        pos = n_p + n_t + n_c
        res_ref = refs[pos] if has_res else None
        pos += 1 if has_res else 0
        dp_refs = refs[pos:pos + n_p]
        dt_refs = refs[pos + n_p:]
        _, vjp = jax.vjp(f, *p, *t)
        grads = vjp(ct)

        @pl.when(i == 0)
        def _():
            for r in dp_refs:
                r[...] = jnp.zeros_like(r)

        for r, g in zip(dp_refs, grads[:n_p]):
            r[...] += g
        for r, j in zip(dt_refs, want):
            g = grads[n_p + j]
            if has_res and j == residual[0]:
                g = g + res_ref[...].astype(f32)
            r[...] = g.astype(r.dtype)

    in_arrays = list(params) + [t[0] for t in tiles] + flat_cts
    in_specs = ([_param_spec(p) for p in params] + [_tile_spec(w, cb, tm) for _, w, cb in tiles]
                + [_tile_spec(c.shape[1], 0, tm) for c in flat_cts])
    if has_res:
        in_arrays.append(residual[1])
        in_specs.append(_tile_spec(residual[1].shape[1], 0, tm))
    out_shape = ([jax.ShapeDtypeStruct(p.shape, f32) for p in params]
                 + [jax.ShapeDtypeStruct((rows, tiles[j][1]), grad_dtypes[j]) for j in want])
    out_specs = [_param_spec(p) for p in params] + [_tile_spec(tiles[j][1], 0, tm) for j in want]
    outs = pl.pallas_call(
        body, name=name, grid=(rows // tm,), in_specs=in_specs, out_specs=out_specs, out_shape=out_shape,
        compiler_params=_cparams(("arbitrary",)),
    )(*in_arrays)
    return outs[:n_p], outs[n_p:]


def _rms(x, w):
    return x * lax.rsqrt(jnp.mean(x * x, axis=-1, keepdims=True) + NORM_EPS) * w


def _f_normmod(w, shift, scale, x):
    return (_rms(x, w) * (1.0 + scale) + shift,)


def _f_merge(ga, gb, pa, pb):
    return (jax.nn.sigmoid(ga) * pa + jax.nn.sigmoid(gb) * pb,)


def _f_residual_normmod(gate, w, shift, scale, x, branch):
    x1 = x + gate * branch
    return x1, _rms(x1, w) * (1.0 + scale) + shift


def _f_loss(gate, wf, x1, fo, target):
    y = _rms(x1 + gate * fo, wf)
    err = jnp.square(y - target)
    return (0.5 * jnp.sum(jnp.mean(err, axis=-1, keepdims=True), axis=0, keepdims=True),)


def _loss_and_grads(gate2, wf, x1, fo, target):
    rows, d = x1.shape
    tm = min(TM, rows)

    def body(g_ref, w_ref, x_ref, fo_ref, t_ref, loss_ref, dg_ref, dw_ref, dx_ref, dfo_ref):
        i = pl.program_id(0)
        (val,), vjp = jax.vjp(_f_loss, g_ref[...], w_ref[...], x_ref[...], fo_ref[...], t_ref[...])
        dg, dw, dx, dfo, _ = vjp((jnp.ones((1, 1), f32),))

        @pl.when(i == 0)
        def _():
            loss_ref[...] = jnp.zeros_like(loss_ref)
            dg_ref[...] = jnp.zeros_like(dg_ref)
            dw_ref[...] = jnp.zeros_like(dw_ref)

        loss_ref[...] += jnp.broadcast_to(val, loss_ref.shape)
        dg_ref[...] += dg
        dw_ref[...] += dw
        dx_ref[...] = dx
        dfo_ref[...] = dfo.astype(bf16)

    vec = pl.BlockSpec((1, d), lambda i: (0, 0))
    tile = pl.BlockSpec((tm, d), lambda i: (i, 0))
    return pl.pallas_call(
        body, name="loss_fwd_bwd", grid=(rows // tm,),
        in_specs=[vec, vec, tile, tile, tile],
        out_specs=[pl.BlockSpec((1, LANES), lambda i: (0, 0)), vec, vec, tile, tile],
        out_shape=[jax.ShapeDtypeStruct((1, LANES), f32), jax.ShapeDtypeStruct((1, d), f32),
                   jax.ShapeDtypeStruct((1, d), f32), jax.ShapeDtypeStruct((rows, d), f32),
                   jax.ShapeDtypeStruct((rows, d), bf16)],
        compiler_params=_cparams(("arbitrary",)),
    )(gate2, wf, x1, fo, target)


def _softplus(z):
    return jnp.maximum(z, 0.0) + jnp.log(1.0 + jnp.exp(-jnp.abs(z)))


def _split_dot(a, m):
    hi = a.astype(bf16)
    lo = (a - hi.astype(f32)).astype(bf16)
    return jnp.dot(hi, m, preferred_element_type=f32) + jnp.dot(lo, m, preferred_element_type=f32)


def _suffix_matrix(n):
    r = lax.broadcasted_iota(jnp.int32, (n, n), 0)
    c = lax.broadcasted_iota(jnp.int32, (n, n), 1)
    return (r > c).astype(bf16)


def _head_masks():
    lane = lax.broadcasted_iota(jnp.int32, (1, LANES), 1)
    return [(lane < SB_HEAD_DIM).astype(f32), (lane >= SB_HEAD_DIM).astype(f32)]


def _sb_prepare(proj):
    def f(k, v):
        lane = lax.broadcasted_iota(jnp.int32, (1, SB_WIDTH), 1)
        m0 = (jnp.bitwise_and(lane, LANES - 1) < SB_HEAD_DIM).astype(f32)
        m1 = 1.0 - m0
        return k, k * m0, k * m1, v, v * m0, v * m1

    wins = [(proj, SB_WIDTH, OFF_SBK // SB_WIDTH), (proj, SB_WIDTH, OFF_SBV // SB_WIDTH)]
    return _stage_fwd(f, [], wins, [bf16] * 6, "sb_prepare")


def _stack_heads(x):
    m0, m1 = _head_masks()
    return jnp.concatenate([x * m0, x * m1], axis=0)


def _sb_logits(qst, k, t_pos2, kb, bq, masked):
    z = lax.dot_general(qst, k, (((1,), (1,)), ((), ())), preferred_element_type=f32)
    l = -_softplus(z)
    if masked:
        s_pos = kb * bq + lax.broadcasted_iota(jnp.int32, (1, bq), 1)
        causal = s_pos < t_pos2
        l = jnp.where(causal, l, 0.0)
    else:
        causal = None
    return z, l, causal


class _SideComm:
    def __init__(self, protocol, arrs, out_shapes):
        self.protocol, self.arrs, self.out_shapes = protocol, list(arrs), list(out_shapes)
        self.n = len(self.arrs)

    def specs(self):
        return [pl.BlockSpec(memory_space=pl.ANY)] * self.n

    def run(self, in_refs, out_refs, sems, first, last, compute):
        start, finish = self.protocol(in_refs, out_refs, *sems)
        pl.when(first)(start)
        compute()
        pl.when(last)(finish)


def _grid_ends(grid):
    ids = [pl.program_id(axis) for axis in range(len(grid))]
    first = functools.reduce(jnp.logical_and, [i == 0 for i in ids])
    last = functools.reduce(jnp.logical_and, [i == g - 1 for i, g in zip(ids, grid)])
    return first, last


def _sb_attention_fwd2(proj, k16, v0_16, v1_16, side=None):
    rows = proj.shape[0]
    bq = SB_QBLOCK
    nq = rows // bq
    assert nq <= LANES, "one lane per key block"
    npair = SB_WIDTH // LANES
    scale = SB_HEAD_DIM ** -0.5

    npp = SB_PAIRS_FWD
    wq = npp * LANES
    grid = (npair // npp, nq)
    ns = side.n if side else 0

    def body(q_ref, k_ref, v0_ref, v1_ref, *rest):
        o_ref, runs_ref = rest[ns], rest[ns + 1]
        if side:
            side.run(rest[:ns], rest[ns + 2:2 * ns + 2], rest[2 * ns + 2:], *_grid_ends(grid),
                     lambda: compute(q_ref, k_ref, v0_ref, v1_ref, o_ref, runs_ref))
        else:
            compute(q_ref, k_ref, v0_ref, v1_ref, o_ref, runs_ref)

    def compute(q_ref, k_ref, v0_ref, v1_ref, o_ref, runs_ref):
        qi = pl.program_id(1)
        pairs = [slice(pp * LANES, (pp + 1) * LANES) for pp in range(npp)]
        qst = [(_stack_heads(q_ref[:, s]) * scale).astype(bf16) for s in pairs]
        r = lax.broadcasted_iota(jnp.int32, (bq, 2 * bq), 0)
        c = lax.broadcasted_iota(jnp.int32, (bq, 2 * bq), 1)
        m2 = jnp.logical_or(r > c, c >= bq).astype(bf16)
        t_pos = qi * bq + lax.broadcasted_iota(jnp.int32, (bq, 1), 0)
        t_pos2 = jnp.concatenate([t_pos, t_pos], axis=0)
        lane = lax.broadcasted_iota(jnp.int32, (1, LANES), 1)
        runs_ref[...] = jnp.full(runs_ref.shape, SB_NEVER, f32)

        def tiles(kbs, carry, masked):
            jobs = [(pp, kb) for kb in kbs for pp in range(npp)]
            rows_k = [pl.ds(pl.multiple_of(kb * bq, bq), bq) for _, kb in jobs]
            zl = [_sb_logits(qst[pp], k_ref[rk, pairs[pp]], t_pos2, kb, bq, masked) for (pp, kb), rk in zip(jobs, rows_k)]
            cs = [_split_dot(l, m2) for _, l, _ in zl]
            run = [cr[0] for cr in carry]
            acc = [cr[1] for cr in carry]
            probs = []
            for (pp, kb), (z, l, causal), cs2 in zip(jobs, zl, cs):
                a = jnp.exp(z + l + cs2[:, :bq] + run[pp])
                if masked:
                    a = jnp.where(causal, a, 0.0)
                probs.append(a.astype(bf16))
                for hh in range(2):
                    cols = slice((2 * pp + hh) * LANES, (2 * pp + hh + 1) * LANES)
                    runs_ref[:, cols] = jnp.where(lane == kb, run[pp][hh * bq:(hh + 1) * bq], runs_ref[:, cols])
                run[pp] = run[pp] + cs2[:, bq:]
            for (pp, kb), rk, ab in zip(jobs, rows_k, probs):
                acc[pp] = (acc[pp] + jnp.dot(ab[:bq], v0_ref[rk, pairs[pp]], preferred_element_type=f32)
                           + jnp.dot(ab[bq:], v1_ref[rk, pairs[pp]], preferred_element_type=f32))
            return tuple(zip(run, acc))

        zero = (jnp.zeros((2 * bq, bq), f32), jnp.zeros((bq, LANES), f32))
        carry = tiles([qi], (zero,) * npp, True)

        def alive(cr):
            return functools.reduce(jnp.maximum, [jnp.max(run) for run, _ in cr]) > SB_DEAD

        def two(state):
            i, _, cr = state
            cr = tiles([qi - 1 - 2 * i, qi - 2 - 2 * i], cr, False)
            return i + 1, alive(cr), cr

        n_two = qi // 2
        i_end, still, carry = lax.while_loop(lambda st: jnp.logical_and(st[0] < n_two, st[1]), two,
                                             (jnp.int32(0), alive(carry), carry))
        last_one = jnp.logical_and(qi % 2 == 1, jnp.logical_and(still, i_end == n_two))
        carry = lax.cond(last_one, lambda cr: tiles([0], cr, False), lambda cr: cr, carry)
        for pp in range(npp):
            o_ref[:, pairs[pp]] = carry[pp][1]

    kv = pl.BlockSpec((rows, wq), lambda p, i: (0, p))
    return pl.pallas_call(
        body, name="sb_attn_fwd", grid=grid,
        in_specs=[pl.BlockSpec((bq, wq), lambda p, i: (i, OFF_SBQ // wq + p)), kv, kv, kv] + (side.specs() if side else []),
        out_specs=[pl.BlockSpec((bq, wq), lambda p, i: (i, p)),
                   pl.BlockSpec((bq, 2 * wq), lambda p, i: (i, p))] + (side.specs() if side else []),
        out_shape=[jax.ShapeDtypeStruct((rows, SB_WIDTH), f32),
                   jax.ShapeDtypeStruct((rows, SB_HEADS * LANES), f32)] + (side.out_shapes if side else []),
        scratch_shapes=_comm_scratch(ns) if side else [],
        compiler_params=_cparams(("arbitrary", "arbitrary")),
    )(proj, k16, v0_16, v1_16, *(side.arrs if side else []))


def _sb_attention_bwd2(proj, k16, k0_16, k1_16, v16, runs, do, side=None):
    rows = proj.shape[0]
    bq = SB_QBLOCK
    nq = rows // bq
    npair = SB_WIDTH // LANES
    scale = SB_HEAD_DIM ** -0.5
    tn = (((0,), (0,)), ((), ()))
    nt = (((1,), (1,)), ((), ()))

    npp = SB_PAIRS_BWD
    wq = npp * LANES
    grid = (npair // npp, nq)
    ns = side.n if side else 0

    def body(q_ref, k_ref, k0_ref, k1_ref, v_ref, runs_ref, do_ref, *rest):
        outs = rest[ns:ns + 3]
        ins = (q_ref, k_ref, k0_ref, k1_ref, v_ref, runs_ref, do_ref)
        if side:
            side.run(rest[:ns], rest[ns + 3:2 * ns + 3], rest[2 * ns + 3:], *_grid_ends(grid),
                     lambda: compute(*ins, *outs))
        else:
            compute(*ins, *outs)

    def compute(q_ref, k_ref, k0_ref, k1_ref, v_ref, runs_ref, do_ref, dq_ref, dk_ref, dv_ref):
        qi = pl.program_id(1)

        @pl.when(qi == 0)
        def _():
            dk_ref[...] = jnp.zeros_like(dk_ref)
            dv_ref[...] = jnp.zeros_like(dv_ref)

        pairs = [slice(pp * LANES, (pp + 1) * LANES) for pp in range(npp)]
        qst = [(_stack_heads(q_ref[:, s]) * scale).astype(bf16) for s in pairs]
        dost = [_stack_heads(do_ref[:, s]).astype(bf16) for s in pairs]
        runs = [jnp.concatenate([runs_ref[:, 2 * pp * LANES:(2 * pp + 1) * LANES],
                                 runs_ref[:, (2 * pp + 1) * LANES:(2 * pp + 2) * LANES]], axis=0) for pp in range(npp)]
        r = lax.broadcasted_iota(jnp.int32, (bq, 2 * bq), 0)
        c = lax.broadcasted_iota(jnp.int32, (bq, 2 * bq), 1)
        suffix_m = _suffix_matrix(bq)
        m2 = jnp.logical_or(r < c, c >= bq).astype(bf16)
        t_pos = qi * bq + lax.broadcasted_iota(jnp.int32, (bq, 1), 0)
        t_pos2 = jnp.concatenate([t_pos, t_pos], axis=0)
        lane = lax.broadcasted_iota(jnp.int32, (1, LANES), 1)

        def tiles(kbs, carry, masked):
            jobs = [(pp, kb) for kb in kbs for pp in range(npp)]
            rows_k = [pl.ds(pl.multiple_of(kb * bq, bq), bq) for _, kb in jobs]
            zl = [_sb_logits(qst[pp], k_ref[rk, pairs[pp]], t_pos2, kb, bq, masked) for (pp, kb), rk in zip(jobs, rows_k)]
            das = [lax.dot_general(dost[pp], v_ref[rk, pairs[pp]], nt, preferred_element_type=f32)
                   for (pp, kb), rk in zip(jobs, rows_k)]
            sticks = [_split_dot(l, suffix_m) for _, l, _ in zl]
            probs, ps = [], []
            for (pp, kb), (z, l, causal), stick, da in zip(jobs, zl, sticks, das):
                run = jnp.sum(jnp.where(lane == kb, runs[pp], 0.0), axis=1, keepdims=True)
                a = jnp.exp(z + l + stick + run)
                if masked:
                    a = jnp.where(causal, a, 0.0)
                probs.append(a.astype(bf16))
                ps.append(da * a)
            pcs = [_split_dot(p, m2) for p in ps]
            pref = [cr[0] for cr in carry]
            dq_acc = [cr[1] for cr in carry]
            dzs = []
            for (pp, kb), (z, l, causal), p, pc2 in zip(jobs, zl, ps, pcs):
                dz = p * jnp.exp(l) - jnp.exp(z + l) * (pc2[:, :bq] + pref[pp])
                if masked:
                    dz = jnp.where(causal, dz, 0.0)
                dzs.append(dz.astype(bf16))
                pref[pp] = pref[pp] + pc2[:, bq:]
            for (pp, kb), rk, dzb, ab in zip(jobs, rows_k, dzs, probs):
                cols = pairs[pp]
                dq_acc[pp] = (dq_acc[pp] + jnp.dot(dzb[:bq], k0_ref[rk, cols], preferred_element_type=f32)
                              + jnp.dot(dzb[bq:], k1_ref[rk, cols], preferred_element_type=f32))
                dk_ref[rk, cols] += lax.dot_general(dzb, qst[pp], tn, preferred_element_type=f32)
                dv_ref[rk, cols] += lax.dot_general(ab, dost[pp], tn, preferred_element_type=f32)
            return tuple(zip(pref, dq_acc))

        zero = (jnp.zeros((2 * bq, bq), f32), jnp.zeros((bq, LANES), f32))
        colmax = functools.reduce(jnp.maximum, [jnp.max(x, axis=0, keepdims=True) for x in runs])
        live = jnp.logical_and(colmax > SB_DEAD, lane < qi)
        kb0 = jnp.minimum(jnp.min(jnp.where(live, lane, LANES)), qi)
        n_blocks = qi - kb0
        carry = lax.fori_loop(0, n_blocks // 2, lambda i, cr: tiles([kb0 + 2 * i, kb0 + 2 * i + 1], cr, False),
                              (zero,) * npp)
        carry = lax.cond(n_blocks % 2 == 1, lambda cr: tiles([qi - 1], cr, False), lambda cr: cr, carry)
        carry = tiles([qi], carry, True)
        for pp in range(npp):
            dq_ref[:, pairs[pp]] = (carry[pp][1] * scale).astype(dq_ref.dtype)

    blk = pl.BlockSpec((bq, wq), lambda p, i: (i, p))
    full = pl.BlockSpec((rows, wq), lambda p, i: (0, p), pipeline_mode=pl.Buffered(1))
    return pl.pallas_call(
        body, name="sb_attn_bwd", grid=grid,
        in_specs=[pl.BlockSpec((bq, wq), lambda p, i: (i, OFF_SBQ // wq + p)), full, full, full, full,
                  pl.BlockSpec((bq, 2 * wq), lambda p, i: (i, p)), blk] + (side.specs() if side else []),
        out_specs=[blk, full, full] + (side.specs() if side else []),
        out_shape=[jax.ShapeDtypeStruct((rows, SB_WIDTH), bf16), jax.ShapeDtypeStruct((rows, SB_WIDTH), f32),
                   jax.ShapeDtypeStruct((rows, SB_WIDTH), f32)] + (side.out_shapes if side else []),
        scratch_shapes=_comm_scratch(ns) if side else [],
        compiler_params=_cparams(("arbitrary", "arbitrary")),
    )(proj, k16, k0_16, k1_16, v16, runs, do, *(side.arrs if side else []))


def _shift_down(x, prev8, j):
    if j == 0:
        return x
    r = pltpu.roll(x, j, axis=0)
    row8 = lax.broadcasted_iota(jnp.int32, prev8.shape, 0)
    head = jnp.where(row8 < j, pltpu.roll(prev8, j, axis=0), r[0:SUBLANES])
    return jnp.concatenate([head, r[SUBLANES:]], axis=0)


def _shift_up(x, next8, j):
    if j == 0:
        return x
    n = x.shape[0]
    r = pltpu.roll(x, n - j, axis=0)
    row8 = lax.broadcasted_iota(jnp.int32, next8.shape, 0)
    tail = jnp.where(row8 >= SUBLANES - j, pltpu.roll(next8, SUBLANES - j, axis=0), r[n - SUBLANES:n])
    return jnp.concatenate([r[:n - SUBLANES], tail], axis=0)


def _conv(x, prev8, w):
    k_taps = w.shape[0]
    out = x * w[k_taps - 1:k_taps, :]
    for j in range(1, k_taps):
        out = out + _shift_down(x, prev8, j) * w[k_taps - 1 - j:k_taps - j, :]
    return out


def _conv_tiles(rows, tr_max=TCONV_R):
    tr = min(tr_max, rows)
    return tr, rows // tr, tr // SUBLANES


def _prev_spec(tc, cb0, r8):
    return pl.BlockSpec((SUBLANES, tc), lambda j, i: (jnp.maximum(i * r8 - 1, 0), cb0 + j))


def _silu(x):
    return x * jax.nn.sigmoid(x)


def _dsilu(x):
    s = jax.nn.sigmoid(x)
    return s * (1.0 + x * (1.0 - s))


def _dn_conv_fwd(proj, w):
    rows = proj.shape[0]
    tr, nr, r8 = _conv_tiles(rows)
    tc = TCONV_C
    cb0 = OFF_DN // tc

    def body(x_ref, p_ref, w_ref, o_ref):
        prev = jnp.where(pl.program_id(1) == 0, 0.0, p_ref[...])
        o_ref[...] = _silu(_conv(x_ref[...], prev, w_ref[...]))

    return pl.pallas_call(
        body, name="dn_conv_fwd", grid=(DN_CONV_CH // tc, nr),
        in_specs=[pl.BlockSpec((tr, tc), lambda j, i: (i, cb0 + j)), _prev_spec(tc, cb0, r8),
                  pl.BlockSpec((DN_CONV_WIDTH, tc), lambda j, i: (0, j))],
        out_specs=pl.BlockSpec((tr, tc), lambda j, i: (i, j)),
        out_shape=jax.ShapeDtypeStruct((rows, DN_CONV_CH), f32),
        compiler_params=_cparams(("parallel", "parallel")),
    )(proj, proj, w)


def _dn_conv_bwd_act(proj, w, dact):
    rows = proj.shape[0]
    tr, nr, r8 = _conv_tiles(rows)
    tc = TCONV_C
    cb0 = OFF_DN // tc

    def body(x_ref, p_ref, w_ref, d_ref, o_ref):
        prev = jnp.where(pl.program_id(1) == 0, 0.0, p_ref[...])
        o_ref[...] = d_ref[...] * _dsilu(_conv(x_ref[...], prev, w_ref[...]))

    return pl.pallas_call(
        body, name="dn_conv_bwd_act", grid=(DN_CONV_CH // tc, nr),
        in_specs=[pl.BlockSpec((tr, tc), lambda j, i: (i, cb0 + j)), _prev_spec(tc, cb0, r8),
                  pl.BlockSpec((DN_CONV_WIDTH, tc), lambda j, i: (0, j)),
                  pl.BlockSpec((tr, tc), lambda j, i: (i, j))],
        out_specs=pl.BlockSpec((tr, tc), lambda j, i: (i, j)),
        out_shape=jax.ShapeDtypeStruct((rows, DN_CONV_CH), f32),
        compiler_params=_cparams(("parallel", "parallel")),
    )(proj, proj, w, dact)


def _ffn_conv_fwd(u_pre, w, b):
    rows = u_pre.shape[0]
    tr, nr, r8 = _conv_tiles(rows, TCONV_R // 2)
    tc = TCONV_FF
    nct = D_FF // tc

    def body(xg_ref, pg_ref, xu_ref, pu_ref, wg_ref, wu_ref, bg_ref, bu_ref, o_ref):
        first = pl.program_id(1) == 0
        ug = _conv(xg_ref[...], jnp.where(first, 0.0, pg_ref[...]), wg_ref[...]) + bg_ref[...]
        uu = _conv(xu_ref[...], jnp.where(first, 0.0, pu_ref[...]), wu_ref[...]) + bu_ref[...]
        o_ref[...] = (_silu(ug) * uu).astype(o_ref.dtype)

    def x_spec(off):
        return pl.BlockSpec((tr, tc), lambda j, i: (i, off + j))

    def w_spec(k, off):
        return pl.BlockSpec((k, tc), lambda j, i: (0, off + j))

    return pl.pallas_call(
        body, name="ffn_conv_fwd", grid=(nct, nr),
        in_specs=[x_spec(0), _prev_spec(tc, 0, r8), x_spec(nct), _prev_spec(tc, nct, r8),
                  w_spec(FFN_CONV_WIDTH, 0), w_spec(FFN_CONV_WIDTH, nct), w_spec(1, 0), w_spec(1, nct)],
        out_specs=pl.BlockSpec((tr, tc), lambda j, i: (i, j)),
        out_shape=jax.ShapeDtypeStruct((rows, D_FF), bf16),
        compiler_params=_cparams(("parallel", "parallel")),
    )(u_pre, u_pre, u_pre, u_pre, w, w, b, b)


def _ffn_conv_bwd_act(u_pre, w, b, dact):
    rows = u_pre.shape[0]
    tr, nr, r8 = _conv_tiles(rows, TCONV_R // 2)
    tc = TCONV_FF
    nct = D_FF // tc

    def body(xg_ref, pg_ref, xu_ref, pu_ref, wg_ref, wu_ref, bg_ref, bu_ref, d_ref,
             du_ref, dbg_ref, dbu_ref):
        i = pl.program_id(1)
        first = i == 0
        ug = _conv(xg_ref[...], jnp.where(first, 0.0, pg_ref[...]), wg_ref[...]) + bg_ref[...]
        uu = _conv(xu_ref[...], jnp.where(first, 0.0, pu_ref[...]), wu_ref[...]) + bu_ref[...]
        d = d_ref[...]
        dug = d * uu * _dsilu(ug)
        duu = d * _silu(ug)
        du_ref[0] = dug
        du_ref[1] = duu

        @pl.when(first)
        def _():
            dbg_ref[...] = jnp.zeros_like(dbg_ref)
            dbu_ref[...] = jnp.zeros_like(dbu_ref)

        dbg_ref[...] += jnp.sum(dug, axis=0, keepdims=True)
        dbu_ref[...] += jnp.sum(duu, axis=0, keepdims=True)

    def x_spec(off):
        return pl.BlockSpec((tr, tc), lambda j, i: (i, off + j))

    def w_spec(k, off):
        return pl.BlockSpec((k, tc), lambda j, i: (0, off + j))

    tile = pl.BlockSpec((tr, tc), lambda j, i: (i, j))
    vec = pl.BlockSpec((1, tc), lambda j, i: (0, j))
    return pl.pallas_call(
        body, name="ffn_conv_bwd_act", grid=(nct, nr),
        in_specs=[x_spec(0), _prev_spec(tc, 0, r8), x_spec(nct), _prev_spec(tc, nct, r8),
                  w_spec(FFN_CONV_WIDTH, 0), w_spec(FFN_CONV_WIDTH, nct), w_spec(1, 0), w_spec(1, nct), tile],
        out_specs=[pl.BlockSpec((2, tr, tc), lambda j, i: (0, i, j)), vec, vec],
        out_shape=[jax.ShapeDtypeStruct((2, rows, D_FF), f32),
                   jax.ShapeDtypeStruct((1, D_FF), f32), jax.ShapeDtypeStruct((1, D_FF), f32)],
        compiler_params=_cparams(("parallel", "arbitrary")),
    )(u_pre, u_pre, u_pre, u_pre, w, w, b, b, dact)


def _conv_bwd(dy, x, x_cb0, w, name):
    k_taps = w.shape[0]
    split = dy.ndim == 3
    rows = dy.shape[-2]
    ch = dy.shape[-1] * (2 if split else 1)
    tc = TCONV_FF if split else TCONV_C
    tr, nr, r8 = _conv_tiles(rows, TCONV_R // 2 if split else TCONV_R)
    per_half = dy.shape[-1] // tc
    last8 = rows // SUBLANES - 1

    def body(dy_ref, nx_ref, x_ref, p_ref, w_ref, dx_ref, dw_ref):
        i = pl.program_id(1)
        dyv = dy_ref[...]
        nxt = jnp.where(i == nr - 1, 0.0, nx_ref[...])
        prev = jnp.where(i == 0, 0.0, p_ref[...])
        xv = x_ref[...].astype(f32)
        wv = w_ref[...]

        @pl.when(i == 0)
        def _():
            dw_ref[...] = jnp.zeros_like(dw_ref)

        dx = dyv * wv[k_taps - 1:k_taps, :]
        dw_ref[k_taps - 1:k_taps, :] += jnp.sum(dyv * xv, axis=0, keepdims=True)
        for j in range(1, k_taps):
            dx = dx + _shift_up(dyv, nxt, j) * wv[k_taps - 1 - j:k_taps - j, :]
            dw_ref[k_taps - 1 - j:k_taps - j, :] += jnp.sum(dyv * _shift_down(xv, prev, j), axis=0, keepdims=True)
        dx_ref[...] = dx.astype(dx_ref.dtype)

    tile = pl.BlockSpec((tr, tc), lambda j, i: (i, j))
    if split:
        dy_spec = pl.BlockSpec((None, tr, tc), lambda j, i: (j // per_half, i, j % per_half))
        next_spec = pl.BlockSpec((None, SUBLANES, tc),
                                 lambda j, i: (j // per_half, jnp.minimum((i + 1) * r8, last8), j % per_half))
    else:
        dy_spec = tile
        next_spec = pl.BlockSpec((SUBLANES, tc), lambda j, i: (jnp.minimum((i + 1) * r8, last8), j))
    return pl.pallas_call(
        body, name=name, grid=(ch // tc, nr),
        in_specs=[dy_spec, next_spec,
                  pl.BlockSpec((tr, tc), lambda j, i: (i, x_cb0 + j)), _prev_spec(tc, x_cb0, r8),
                  pl.BlockSpec((k_taps, tc), lambda j, i: (0, j))],
        out_specs=[tile, pl.BlockSpec((k_taps, tc), lambda j, i: (0, j))],
        out_shape=[jax.ShapeDtypeStruct((rows, ch), bf16), jax.ShapeDtypeStruct((k_taps, ch), f32)],
        compiler_params=_cparams(("parallel", "arbitrary")),
    )(dy, dy, x, x, w)


def _hdot(a, b):
    return jnp.dot(a, b, preferred_element_type=f32, precision=lax.Precision.HIGH)


def _xdot(a, b):
    return jnp.dot(a, b, preferred_element_type=f32, precision=lax.Precision.HIGHEST)


def _bdot(a, b):
    return jnp.dot(a.astype(bf16), b.astype(bf16), preferred_element_type=f32)


def _bdot_nt(a, b):
    return lax.dot_general(a.astype(bf16), b.astype(bf16), (((1,), (1,)), ((), ())), preferred_element_type=f32)


def _bdot_tn(a, b):
    return lax.dot_general(a.astype(bf16), b.astype(bf16), (((0,), (0,)), ((), ())), preferred_element_type=f32)


GDN_GROUP = 4
GDN_NGROUPS = DN_HEADS // GDN_GROUP
GDN_ROWS = GDN_GROUP * DN_CHUNK
GDN_QK_LANES = GDN_GROUP * DN_KEY_DIM
GDN_LOGIT_LANE = DN_HEADS


def _inverse_impl(lows):
    n = lows[0].shape[0]
    r = lax.broadcasted_iota(jnp.int32, (n, n), 0)
    c = lax.broadcasted_iota(jnp.int32, (n, n), 1)
    eye = (r == c).astype(f32)
    blk = jnp.right_shift(r, 3) == jnp.right_shift(c, 3)
    d = [jnp.where(blk, low, 0.0) for low in lows]
    e = [low - x for low, x in zip(lows, d)]

    def nilpotent8_inverse(xs):
        acc = [eye - x for x in xs]
        power = xs
        for _ in range(2):
            power = [_bdot(x, x) for x in power]
            acc = [_bdot(a, eye + x) for a, x in zip(acc, power)]
        return acc

    dinv = nilpotent8_inverse(d)
    ninv = nilpotent8_inverse([_bdot(x, y) for x, y in zip(dinv, e)])
    t = [_bdot(x, y) for x, y in zip(ninv, dinv)]
    for _ in range(2):
        res = [eye - x - _hdot(low, x) for low, x in zip(lows, t)]
        t = [x + _bdot(x, y) for x, y in zip(t, res)]
    return tuple(t)


@jax.custom_vjp
def _unit_lower_inverses(lows):
    return _inverse_impl(lows)


def _unit_lower_inverses_fwd(lows):
    t = _inverse_impl(lows)
    return t, t


def _unit_lower_inverses_bwd(t, ct):
    tn = (((0,), (0,)), ((), ()))
    nt = (((1,), (1,)), ((), ()))
    left = [lax.dot_general(x, g, tn, preferred_element_type=f32, precision=lax.Precision.HIGH) for x, g in zip(t, ct)]
    return (tuple(-lax.dot_general(x, y, nt, preferred_element_type=f32, precision=lax.Precision.HIGH)
                  for x, y in zip(left, t)),)


_unit_lower_inverses.defvjp(_unit_lower_inverses_fwd, _unit_lower_inverses_bwd)


@jax.custom_vjp
def _known_inverses(lows, t):
    return t


def _known_inverses_fwd(lows, t):
    return t, t


def _known_inverses_bwd(t, ct):
    return _unit_lower_inverses_bwd(t, ct) + (tuple(jnp.zeros_like(x) for x in t),)


_known_inverses.defvjp(_known_inverses_fwd, _known_inverses_bwd)


def _gdn_chunk(a_log, dt_bias, norm_w, ba, *per_group, inverses=None, keep_inverses=False):
    ng = GDN_NGROUPS
    qgs, kgs, vsts, zsts, states = [per_group[i * ng:(i + 1) * ng] for i in range(5)]
    groups = range(ng)
    n = GDN_ROWS
    r = lax.broadcasted_iota(jnp.int32, (n, n), 0)
    c = lax.broadcasted_iota(jnp.int32, (n, n), 1)
    same_head = jnp.right_shift(r, 6) == jnp.right_shift(c, 6)
    incl = jnp.logical_and(same_head, r >= c)
    strict = jnp.logical_and(same_head, r > c)
    eye = (r == c).astype(f32)
    ones = jnp.ones((n, n), f32)
    own_lanes = same_head.astype(f32)
    lane = lax.broadcasted_iota(jnp.int32, (1, LANES), 1)
    pick = lambda arr, idx: jnp.sum(jnp.where(lane == idx, arr, 0.0), axis=1, keepdims=True)
    heads = [[GDN_GROUP * g + h for h in range(GDN_GROUP)] for g in groups]
    rc = lax.broadcasted_iota(jnp.int32, (DN_CHUNK, DN_CHUNK), 0)
    cc = lax.broadcasted_iota(jnp.int32, (DN_CHUNK, DN_CHUNK), 1)

    g_all = -jnp.exp(a_log) * _softplus(ba + dt_bias)
    gc_all = _xdot((rc >= cc).astype(f32), g_all)
    gl_all = jnp.sum(g_all, axis=0, keepdims=True)
    beta = [jnp.concatenate([jax.nn.sigmoid(pick(ba, hd)) for hd in heads[g]], axis=0) for g in groups]
    gc = [jnp.concatenate([pick(gc_all, GDN_LOGIT_LANE + hd) for hd in heads[g]], axis=0) for g in groups]
    g_last = [jnp.concatenate([jnp.broadcast_to(pick(gl_all, GDN_LOGIT_LANE + hd), (DN_CHUNK, 1)) for hd in heads[g]],
                              axis=0) for g in groups]
    gr = [jnp.broadcast_to(gc[g], (n, n)).T for g in groups]
    decay = [jnp.where(incl, jnp.exp(jnp.where(incl, gc[g] - gr[g], 0.0)), 0.0) for g in groups]
    q = [jnp.concatenate([qgs[g]] * GDN_GROUP, axis=0) * own_lanes for g in groups]
    k = [jnp.concatenate([kgs[g]] * GDN_GROUP, axis=0) * own_lanes for g in groups]
    qn = [x * lax.rsqrt(jnp.sum(x * x, axis=1, keepdims=True) + L2_EPS) * (DN_KEY_DIM ** -0.5) for x in q]
    kn = [x * lax.rsqrt(jnp.sum(x * x, axis=1, keepdims=True) + L2_EPS) for x in k]
    kb = [kn[g] * beta[g] for g in groups]
    low = [jnp.where(strict, _bdot_nt(kb[g], kn[g]) * decay[g], 0.0) for g in groups]
    intra = [jnp.where(incl, _bdot_nt(qn[g], kn[g]) * decay[g], 0.0) for g in groups]
    t = _unit_lower_inverses(tuple(low)) if inverses is None else _known_inverses(tuple(low), tuple(inverses))
    u = [_bdot(t[g], vsts[g] * beta[g]) for g in groups]
    w = [_bdot(t[g], kb[g] * jnp.exp(gc[g])) for g in groups]
    sb = [s.astype(bf16) for s in states]
    v_new = [u[g] - jnp.dot(w[g].astype(bf16), sb[g], preferred_element_type=f32) for g in groups]
    o = [jnp.dot((qn[g] * jnp.exp(gc[g])).astype(bf16), sb[g], preferred_element_type=f32) for g in groups]
    o = [o[g] + _bdot(intra[g], v_new[g]) for g in groups]
    new_state = [states[g] * jnp.exp(g_last[g]) + _bdot_tn(kn[g] * jnp.exp(g_last[g] - gc[g]), v_new[g])
                 for g in groups]
    o_n = [x * lax.rsqrt(jnp.mean(x * x, axis=1, keepdims=True) + NORM_EPS) * norm_w for x in o]
    return tuple(o_n[g] * _silu(zsts[g]) for g in groups) + tuple(new_state) + (tuple(t) if keep_inverses else ())


def _gdn_specs(rows, reverse):
    n = rows // DN_CHUNK
    idx = (lambda i: n - 1 - i) if reverse else (lambda i: i)
    vec = pl.BlockSpec((1, LANES), lambda i: (0, 0))
    qkv = pl.BlockSpec((DN_CHUNK, DN_CONV_CH), lambda i: (idx(i), 0))
    z = pl.BlockSpec((DN_CHUNK, DN_V_WIDTH), lambda i: (idx(i), OFF_Z // DN_V_WIDTH))
    ba = pl.BlockSpec((DN_CHUNK, LANES), lambda i: (idx(i), 0))
    wide = pl.BlockSpec((DN_CHUNK, DN_V_WIDTH), lambda i: (idx(i), 0))
    st = pl.BlockSpec((1, DN_HEADS * DN_KEY_DIM, LANES), lambda i: (idx(i), 0, 0))
    inv = pl.BlockSpec((1, GDN_NGROUPS * GDN_ROWS, GDN_ROWS), lambda i: (idx(i), 0, 0))
    return n, vec, qkv, z, ba, wide, st, inv


def _gdn_slices(grp):
    q = slice(grp * GDN_QK_LANES, (grp + 1) * GDN_QK_LANES)
    k = slice(DN_QK_WIDTH + grp * GDN_QK_LANES, DN_QK_WIDTH + (grp + 1) * GDN_QK_LANES)
    heads = [slice((GDN_GROUP * grp + h) * LANES, (GDN_GROUP * grp + h + 1) * LANES) for h in range(GDN_GROUP)]
    vs = [slice(2 * DN_QK_WIDTH + s.start, 2 * DN_QK_WIDTH + s.stop) for s in heads]
    return q, k, vs, heads


def _stack_cols(ref, cols):
    return jnp.concatenate([ref[:, s] for s in cols], axis=0)


def _gdn_operands(qkv_ref, z_ref, state_rows):
    sl = [_gdn_slices(grp) for grp in range(GDN_NGROUPS)]
    return ([qkv_ref[:, q] for q, _, _, _ in sl] + [qkv_ref[:, k] for _, k, _, _ in sl]
            + [_stack_cols(qkv_ref, vs) for _, _, vs, _ in sl] + [_stack_cols(z_ref, heads) for _, _, _, heads in sl]
            + [state_rows[grp * GDN_ROWS:(grp + 1) * GDN_ROWS, :] for grp in range(GDN_NGROUPS)])


def _gdn_fwd(a_log, dt_bias, norm_w, qkv_act, proj, ba):
    rows = qkv_act.shape[0]
    n, vec, qkv_s, z_s, ba_s, wide, st_s, inv_s = _gdn_specs(rows, False)

    def body(al_ref, dt_ref, nw_ref, qkv_ref, z_ref, ba_ref, o_ref, st_ref, inv_ref, state):
        @pl.when(pl.program_id(0) == 0)
        def _():
            state[...] = jnp.zeros_like(state)

        st_ref[0] = state[...]
        out = _gdn_chunk(al_ref[...], dt_ref[...], nw_ref[...], ba_ref[...], *_gdn_operands(qkv_ref, z_ref, state),
                         keep_inverses=True)
        for grp in range(GDN_NGROUPS):
            _, _, _, heads = _gdn_slices(grp)
            rs = slice(grp * GDN_ROWS, (grp + 1) * GDN_ROWS)
            for h, s in enumerate(heads):
                o_ref[:, s] = out[grp][h * DN_CHUNK:(h + 1) * DN_CHUNK].astype(o_ref.dtype)
            state[rs, :] = out[GDN_NGROUPS + grp]
            inv_ref[0, rs, :] = out[2 * GDN_NGROUPS + grp]

    return pl.pallas_call(
        body, name="gdn_fwd", grid=(n,),
        in_specs=[vec, vec, vec, qkv_s, z_s, ba_s], out_specs=[wide, st_s, inv_s],
        out_shape=[jax.ShapeDtypeStruct((rows, DN_V_WIDTH), bf16),
                   jax.ShapeDtypeStruct((n, DN_HEADS * DN_KEY_DIM, LANES), f32),
                   jax.ShapeDtypeStruct((n, GDN_NGROUPS * GDN_ROWS, GDN_ROWS), f32)],
        scratch_shapes=[pltpu.VMEM((DN_HEADS * DN_KEY_DIM, LANES), f32)],
        compiler_params=_cparams(("arbitrary",)),
    )(a_log, dt_bias, norm_w, qkv_act, proj, ba)


def _gdn_bwd(a_log, dt_bias, norm_w, qkv_act, proj, ba, states, inverses, do, side=None):
    rows = qkv_act.shape[0]
    n, vec, qkv_s, z_s, ba_s, wide, st_s, inv_s = _gdn_specs(rows, True)
    ns = side.n if side else 0

    def body(*refs):
        ins, rest = refs[:9], refs[9:]
        outs, dstate = rest[ns:ns + 6], rest[-1]
        if side:
            side.run(rest[:ns], rest[ns + 6:2 * ns + 6], rest[2 * ns + 6:-1], *_grid_ends((n,)),
                     lambda: compute(*ins, *outs, dstate))
        else:
            compute(*ins, *outs, dstate)

    def compute(al_ref, dt_ref, nw_ref, qkv_ref, z_ref, ba_ref, st_ref, inv_ref, do_ref,
                dal_ref, ddt_ref, dnw_ref, dqkv_ref, dz_ref, dba_ref, dstate):
        @pl.when(pl.program_id(0) == 0)
        def _():
            dstate[...] = jnp.zeros_like(dstate)
            dal_ref[...] = jnp.zeros_like(dal_ref)
            ddt_ref[...] = jnp.zeros_like(ddt_ref)
            dnw_ref[...] = jnp.zeros_like(dnw_ref)

        ng = GDN_NGROUPS
        kept = [inv_ref[0, grp * GDN_ROWS:(grp + 1) * GDN_ROWS, :] for grp in range(ng)]
        _, vjp = jax.vjp(functools.partial(_gdn_chunk, inverses=kept), al_ref[...], dt_ref[...], nw_ref[...],
                         ba_ref[...], *_gdn_operands(qkv_ref, z_ref, st_ref[0]))
        cts = tuple(_stack_cols(do_ref, _gdn_slices(grp)[3]) for grp in range(ng))
        cts += tuple(dstate[grp * GDN_ROWS:(grp + 1) * GDN_ROWS, :] for grp in range(ng))
        grads = vjp(cts)
        dal_ref[...] += grads[0]
        ddt_ref[...] += grads[1]
        dnw_ref[...] += grads[2]
        dba_ref[...] = grads[3]
        dqs, dks, dvs, dzs, dss = [grads[4 + i * ng:4 + (i + 1) * ng] for i in range(5)]
        for grp in range(ng):
            q, k, vs, heads = _gdn_slices(grp)
            dqkv_ref[:, q] = dqs[grp]
            dqkv_ref[:, k] = dks[grp]
            for h, (sv, sh) in enumerate(zip(vs, heads)):
                rows_h = slice(h * DN_CHUNK, (h + 1) * DN_CHUNK)
                dqkv_ref[:, sv] = dvs[grp][rows_h]
                dz_ref[:, sh] = dzs[grp][rows_h].astype(dz_ref.dtype)
            dstate[grp * GDN_ROWS:(grp + 1) * GDN_ROWS, :] = dss[grp]

    return pl.pallas_call(
        body, name="gdn_bwd", grid=(n,),
        in_specs=[vec, vec, vec, qkv_s, z_s, ba_s, st_s, inv_s, wide] + (side.specs() if side else []),
        out_specs=[vec, vec, vec, qkv_s, wide, ba_s] + (side.specs() if side else []),
        out_shape=[jax.ShapeDtypeStruct((1, LANES), f32)] * 3
        + [jax.ShapeDtypeStruct((rows, DN_CONV_CH), f32), jax.ShapeDtypeStruct((rows, DN_V_WIDTH), bf16),
           jax.ShapeDtypeStruct((rows, LANES), f32)] + (side.out_shapes if side else []),
        scratch_shapes=(_comm_scratch(ns) if side else []) + [pltpu.VMEM((DN_HEADS * DN_KEY_DIM, LANES), f32)],
        compiler_params=_cparams(("arbitrary",)),
    )(a_log, dt_bias, norm_w, qkv_act, proj, ba, states, inverses, do, *(side.arrs if side else []))


def _ada_fwd(c_all, w_loc, b_loc):
    def body(c_ref, w_ref, b_ref, o_ref):
        o_ref[...] = _bdot(_silu(c_ref[...]), w_ref[...]) + b_ref[...]

    return pl.pallas_call(body, name="ada_fwd", out_shape=jax.ShapeDtypeStruct((c_all.shape[0], w_loc.shape[1]), f32),
                          compiler_params=_cparams())(c_all, w_loc, b_loc)


def _ada_bwd(c_all, dmod_cols):
    def body(c_ref, d_ref, o_ref):
        o_ref[...] = _bdot_tn(_silu(c_ref[...]), d_ref[...])

    return pl.pallas_call(body, name="ada_bwd",
                          out_shape=jax.ShapeDtypeStruct((c_all.shape[1], dmod_cols.shape[1]), f32),
                          compiler_params=_cparams())(c_all, dmod_cols)


def _sum_devices(parts):
    def body(p_ref, o_ref):
        acc = p_ref[0:1, :]
        for d in range(1, N_DEV):
            acc = acc + p_ref[d:d + 1, :]
        o_ref[...] = acc

    return pl.pallas_call(body, name="sum_small", out_shape=jax.ShapeDtypeStruct((1, parts.shape[1]), f32),
                          compiler_params=_cparams())(parts)


def _adam_math(w, g, m, v):
    m2 = ADAM_B1 * m + (1.0 - ADAM_B1) * g
    v2 = ADAM_B2 * v + (1.0 - ADAM_B2) * jnp.square(g)
    m_hat = m2 / (1.0 - ADAM_B1 ** ADAM_STEP)
    v_hat = v2 / (1.0 - ADAM_B2 ** ADAM_STEP)
    delta = -ADAM_LR * (m_hat / (jnp.sqrt(v_hat) + ADAM_EPS) + ADAM_WD * w)
    return delta, m2, v2


def _row_tile(rows):
    return _pick(rows, (256, 128, 64, 32, 16, 8))


def _adamw(w, g, m, v, name):
    rows, cols = w.shape
    tr = _row_tile(rows)

    def body(w_ref, g_ref, m_ref, v_ref, d_ref, m2_ref, v2_ref):
        d_ref[...], m2_ref[...], v2_ref[...] = _adam_math(w_ref[...], g_ref[...], m_ref[...], v_ref[...])

    tile = pl.BlockSpec((tr, cols), lambda i: (i, 0))
    return pl.pallas_call(body, name=name, grid=(rows // tr,), in_specs=[tile] * 4, out_specs=[tile] * 3,
                          out_shape=[jax.ShapeDtypeStruct(w.shape, f32)] * 3,
                          compiler_params=_cparams(("parallel",)))(w, g, m, v)


def _sum_adamw(parts, w, m, v, name):
    rows, cols = w.shape
    tr = _row_tile(rows)

    def body(p_ref, w_ref, m_ref, v_ref, g_ref, d_ref, m2_ref, v2_ref):
        g = p_ref[0].astype(f32)
        for d in range(1, N_DEV):
            g = g + p_ref[d].astype(f32)
        g_ref[...] = g
        d_ref[...], m2_ref[...], v2_ref[...] = _adam_math(w_ref[...], g, m_ref[...], v_ref[...])

    tile = pl.BlockSpec((tr, cols), lambda i: (i, 0))
    return pl.pallas_call(body, name=name, grid=(rows // tr,),
                          in_specs=[pl.BlockSpec((N_DEV, tr, cols), lambda i: (0, i, 0)), tile, tile, tile],
                          out_specs=[tile] * 4, out_shape=[jax.ShapeDtypeStruct(w.shape, f32)] * 4,
                          compiler_params=_cparams(("parallel",)))(parts, w, m, v)


def _pad_lanes(a, width):
    return jnp.pad(a, ((0, 0), (0, width - a.shape[1])))


def _cols_by_device(full):
    r, c = full.shape
    return jnp.moveaxis(full.reshape(r, N_DEV, c // N_DEV), 1, 0)


def _cols_from_devices(parts):
    d, r, n = parts.shape
    return jnp.moveaxis(parts, 0, 1).reshape(r, d * n)


def kernel(x, c, w_ada, b_ada, norm1_w, w_in, dn_conv_w, dn_A_log, dn_dt_bias, dn_norm_w, w_proj_sb, w_proj_dn, w_out, norm2_w, w_ffn_in, ffn_conv_w, ffn_conv_b, w_ffn_out, final_norm_w, loss_target, m_w_ada, m_b_ada, m_norm1_w, m_w_in, m_dn_conv_w, m_dn_A_log, m_dn_dt_bias, m_dn_norm_w, m_w_proj_sb, m_w_proj_dn, m_w_out, m_norm2_w, m_w_ffn_in, m_ffn_conv_w, m_ffn_conv_b, m_w_ffn_out, m_final_norm_w, v_w_ada, v_b_ada, v_norm1_w, v_w_in, v_dn_conv_w, v_dn_A_log, v_dn_dt_bias, v_dn_norm_w, v_w_proj_sb, v_w_proj_dn, v_w_out, v_norm2_w, v_w_ffn_in, v_ffn_conv_w, v_ffn_conv_b, v_w_ffn_out, v_final_norm_w):
    d = D_MODEL
    me = 4 * lax.axis_index("x") + 2 * lax.axis_index("y") + lax.axis_index("c")
    xs = x[0]
    target = loss_target[0]
    n_ada = w_ada.shape[2]
    n_dnc = dn_conv_w.shape[2]
    n_ffc = ffn_conv_w.shape[2]

    small = jnp.concatenate([c, dn_conv_w[0].reshape(1, -1), ffn_conv_w[0].reshape(1, -1)], axis=1)
    small = _pad_lanes(small, -(-small.shape[1] // LANES) * LANES)
    small_g, w_in_g = _all_gather([small, w_in[0].astype(bf16)], "gather_w_in")
    later = [w_proj_sb[0].astype(bf16), w_proj_dn[0].astype(bf16), w_out[0].astype(bf16),
             w_ffn_in[0].astype(bf16), w_ffn_out[0].astype(bf16)]
    gather_later = _SideComm(_gather_protocol, later, _gathered_shapes(later))
    small_g = small_g[:, 0, :]
    c_all = small_g[:, :d]
    dn_cw = _cols_from_devices(small_g[:, d:d + DN_CONV_WIDTH * n_dnc].reshape(N_DEV, DN_CONV_WIDTH, n_dnc))
    o2 = d + DN_CONV_WIDTH * n_dnc
    ffn_cw = _cols_from_devices(small_g[:, o2:o2 + FFN_CONV_WIDTH * n_ffc].reshape(N_DEV, FFN_CONV_WIDTH, n_ffc))

    w_in_full = _cols_from_devices(w_in_g)
    r_sb, r_dn, r_z = 3 * SB_WIDTH, 3 * SB_WIDTH + DN_CONV_CH, 3 * SB_WIDTH + DN_CONV_CH + DN_V_WIDTH
    r_g = r_z + 2 * DN_HEADS
    w_main = jnp.concatenate([w_in_full[:, r_g:], w_in_full[:, r_sb:r_dn], w_in_full[:, r_dn:r_z],
                              w_in_full[:, :r_sb]], axis=1)
    w_ba = _pad_lanes(w_in_full[:, r_z:r_g], LANES)

    b_loc = lax.dynamic_slice(b_ada, (0, me * n_ada), (1, n_ada))
    mod_part = _ada_fwd(c_all, w_ada[0], b_loc)
    (mod_g,) = _all_gather([mod_part], "gather_mod")
    mod = lax.dynamic_index_in_dim(mod_g, me, axis=1, keepdims=False).reshape(1, N_DEV * n_ada)
    shift1, scale1, gate1, shift2, scale2, gate2 = [mod[:, i * d:(i + 1) * d] for i in range(6)]

    logit_lanes = ((0, 0), (GDN_LOGIT_LANE, LANES - GDN_LOGIT_LANE - DN_HEADS))
    a_log = jnp.pad(dn_A_log, logit_lanes)
    dt_b = jnp.pad(dn_dt_bias, logit_lanes)

    (h,) = _stage_fwd(_f_normmod, [norm1_w, shift1, scale1], [xs], [bf16], "norm1_fwd")
    proj = _mm(h, w_main, name="in_proj")
    ba = _mm(h, w_ba, name="in_proj_ba")
    k16, k0_16, k1_16, v16, v0_16, v1_16 = _sb_prepare(proj)
    o_a, sb_runs, w_psb_g, w_pdn_g, w_out_g, w_fin_g, w_fout_g = _sb_attention_fwd2(
        proj, k16, v0_16, v1_16, side=gather_later)
    w_psb = _cols_from_devices(w_psb_g)
    w_pdn = w_pdn_g.reshape(DN_V_WIDTH, d)
    w_o = w_out_g.reshape(d, d)
    w_fin = _cols_from_devices(w_fin_g)
    w_fout = w_fout_g.reshape(D_FF, d)
    qkv_act = _dn_conv_fwd(proj, dn_cw)
    o_b, states, dn_inverses = _gdn_fwd(a_log, dt_b, dn_norm_w, qkv_act, proj, ba)
    pa = _mm(o_a, w_psb, name="proj_sb")
    pb = _mm(o_b, w_pdn, name="proj_dn")
    gates = [(proj, d, OFF_GA // d), (proj, d, OFF_GB // d)]
    (merged,) = _stage_fwd(_f_merge, [], gates + [pa, pb], [bf16], "merge_fwd")
    ao = _mm(merged, w_o, name="out_proj")
    mid_params = [gate1, norm2_w, shift2, scale2]
    x1, h2 = _stage_fwd(_f_residual_normmod, mid_params, [xs, ao], [f32, bf16], "resid1_norm2_fwd")
    u_pre = _mm(h2, w_fin, name="ffn_in")
    act = _ffn_conv_fwd(u_pre, ffn_cw, ffn_conv_b)
    fo = _mm(act, w_fout, name="ffn_out")

    loss_p, d_gate2, d_wf, dx2, dfo = _loss_and_grads(gate2, final_norm_w.reshape(1, d), x1, fo, target)
    dact = _mm(dfo, w_fout, tb=True, name="ffn_out_dx")
    g_w_fout = _mm(act, dfo, ta=True, name="ffn_out_dw")
    du, dbg, dbu = _ffn_conv_bwd_act(u_pre, ffn_cw, ffn_conv_b, dact)
    du_pre, d_ffn_cw = _conv_bwd(du, u_pre, 0, ffn_cw, "ffn_conv_bwd")
    dh2 = _mm(du_pre, w_fin, tb=True, name="ffn_in_dx")
    g_w_fin = _mm(h2, du_pre, ta=True, name="ffn_in_dw")
    (d_gate1, d_n2w, d_shift2, d_scale2), (dx1, dao) = _stage_bwd(
        _f_residual_normmod, mid_params, [xs, ao], [dx2, dh2], [f32, bf16], "resid1_norm2_bwd")
    dmerged = _mm(dao, w_o, tb=True, name="out_proj_dx")
    g_w_o = _mm(merged, dao, ta=True, name="out_proj_dw")
    _, (dga, dgb, dpa, dpb) = _stage_bwd(_f_merge, [], gates + [pa, pb], [dmerged], [bf16] * 4, "merge_bwd")
    do_a = _mm(dpa, w_psb, tb=True, name="proj_sb_dx")
    g_w_psb = _mm(o_a, dpa, ta=True, name="proj_sb_dw")
    do_b = _mm(dpb, w_pdn, tb=True, name="proj_dn_dx")
    g_w_pdn = _mm(o_b, dpb, ta=True, name="proj_dn_dw")
    def exchange_of(parts):
        return _SideComm(_exchange_protocol, parts, [jax.ShapeDtypeStruct(a.shape, a.dtype) for a in parts])

    under_attn = [_cols_by_device(g_w_psb).astype(bf16),
                  g_w_pdn.reshape(N_DEV, DN_V_WIDTH // N_DEV, d).astype(bf16),
                  g_w_o.reshape(N_DEV, d // N_DEV, d).astype(bf16),
                  g_w_fout.reshape(N_DEV, D_FF // N_DEV, d).astype(bf16)]
    under_gdn = [_cols_by_device(g_w_fin).astype(bf16)]
    dq, dk, dv, r_psb, r_pdn, r_o, r_fout = _sb_attention_bwd2(proj, k16, k0_16, k1_16, v16, sb_runs, do_a,
                                                               side=exchange_of(under_attn))
    d_alog, d_dtb, d_dnw, dqkv_act, dz, dba, r_fin = _gdn_bwd(a_log, dt_b, dn_norm_w, qkv_act, proj, ba, states,
                                                              dn_inverses, do_b, side=exchange_of(under_gdn))
    recv_early = [r_psb, r_pdn, r_o, r_fin, r_fout]
    d_conv_out = _dn_conv_bwd_act(proj, dn_cw, dqkv_act)
    d_dn_pre, d_dn_cw = _conv_bwd(d_conv_out, proj, OFF_DN // TCONV_C, dn_cw, "dn_conv_bwd")
    dproj = jnp.concatenate([dga, dgb, d_dn_pre, dz, dq, dk.astype(bf16), dv.astype(bf16)], axis=1)
    g_w_main = _mm(h, dproj, ta=True, name="in_proj_dw")
    g_w_ba = _mm(h, dba, ta=True, name="in_proj_ba_dw")
    g_w_in_full = jnp.concatenate([g_w_main[:, OFF_SBQ:], g_w_main[:, OFF_DN:OFF_Z], g_w_main[:, OFF_Z:OFF_SBQ],
                                   g_w_ba[:, :2 * DN_HEADS], g_w_main[:, :OFF_DN]], axis=1)
    w_in_parts = _cols_by_device(g_w_in_full).astype(bf16)
    exchange_w_in = _SideComm(_exchange_protocol, [w_in_parts], [jax.ShapeDtypeStruct(w_in_parts.shape, bf16)])
    dh, recv_w_in = _mm(dproj, w_main, tb=True, name="in_proj_dx", side=exchange_w_in)
    dh_ba = _mm(dba, w_ba, tb=True, name="in_proj_ba_dx")
    (d_n1w, d_shift1, d_scale1), (grad_x,) = _stage_bwd(
        _f_normmod, [norm1_w, shift1, scale1], [xs], [[dh, dh_ba]], [f32], "norm1_bwd", residual=(0, dx1))

    dmod = jnp.concatenate([d_shift1, d_scale1, d_gate1, d_shift2, d_scale2, d_gate2], axis=1)
    d_ffn_cb = jnp.concatenate([dbg, dbu], axis=1)
    small_parts = jnp.concatenate(
        [loss_p, dmod, d_n1w, d_alog, d_dtb, d_dnw, d_n2w, d_ffn_cb, d_wf,
         d_dn_cw.reshape(1, -1), d_ffn_cw.reshape(1, -1)], axis=1)
    (small_parts_g,) = _all_gather([small_parts], "gather_small_grads")
    tot = _sum_devices(small_parts_g[:, 0, :])
    offs = {}
    pos = 0
    for nm, width in (("loss", LANES), ("b_ada", 6 * d), ("norm1_w", d), ("dn_A_log", LANES), ("dn_dt_bias", LANES),
                      ("dn_norm_w", LANES), ("norm2_w", d), ("ffn_conv_b", 2 * D_FF), ("final_norm_w", d),
                      ("dn_conv_w", DN_CONV_WIDTH * DN_CONV_CH), ("ffn_conv_w", FFN_CONV_WIDTH * 2 * D_FF)):
        offs[nm] = (pos, width)
        pos += width
    seg = lambda nm: tot[:, offs[nm][0]:offs[nm][0] + offs[nm][1]]
    loss = tot[0, 0]
    g_b_ada = seg("b_ada")
    g_norm1 = seg("norm1_w")
    g_alog = seg("dn_A_log")[:, GDN_LOGIT_LANE:GDN_LOGIT_LANE + DN_HEADS]
    g_dtb = seg("dn_dt_bias")[:, GDN_LOGIT_LANE:GDN_LOGIT_LANE + DN_HEADS]
    g_dnw = seg("dn_norm_w")
    g_norm2 = seg("norm2_w")
    g_ffn_cb = seg("ffn_conv_b")
    g_fnw = seg("final_norm_w")
    g_dn_cw = lax.dynamic_slice(seg("dn_conv_w").reshape(DN_CONV_WIDTH, DN_CONV_CH), (0, me * n_dnc),
                                (DN_CONV_WIDTH, n_dnc))
    g_ffn_cw = lax.dynamic_slice(seg("ffn_conv_w").reshape(FFN_CONV_WIDTH, 2 * D_FF), (0, me * n_ffc),
                                 (FFN_CONV_WIDTH, n_ffc))

    dmod_all = small_parts_g[:, 0, offs["b_ada"][0]:offs["b_ada"][0] + 6 * d]
    g_w_ada = _ada_bwd(c_all, lax.dynamic_slice(dmod_all, (0, me * n_ada), (N_DEV, n_ada)))

    def pack(parts):
        flat = [p.reshape(1, -1) for p in parts]
        flat = [_pad_lanes(p, -(-p.shape[1] // LANES) * LANES) for p in flat]
        return jnp.concatenate(flat, axis=1), [p.shape[1] for p in flat]

    small_names_g = [g_b_ada, g_norm1, g_alog, g_dtb, g_dnw, g_norm2, g_ffn_cb, g_fnw, g_dn_cw, g_ffn_cw]
    small_w = [b_ada, norm1_w, dn_A_log, dn_dt_bias, dn_norm_w, norm2_w, ffn_conv_b, final_norm_w, dn_conv_w[0], ffn_conv_w[0]]
    small_m = [m_b_ada, m_norm1_w, m_dn_A_log, m_dn_dt_bias, m_dn_norm_w, m_norm2_w, m_ffn_conv_b, m_final_norm_w, m_dn_conv_w[0], m_ffn_conv_w[0]]
    small_v = [v_b_ada, v_norm1_w, v_dn_A_log, v_dn_dt_bias, v_dn_norm_w, v_norm2_w, v_ffn_conv_b, v_final_norm_w, v_dn_conv_w[0], v_ffn_conv_w[0]]
    pg, widths = pack(small_names_g)
    pw, _ = pack(small_w)
    pm, _ = pack(small_m)
    pv, _ = pack(small_v)
    s_delta, s_m, s_v = _adamw(pw, pg, pm, pv, "adamw_small")

    def unpack(flat):
        out, pos = [], 0
        for ref_arr, width in zip(small_w, widths):
            out.append(flat[:, pos:pos + ref_arr.size].reshape(ref_arr.shape))
            pos += width
        return out

    small_grads = [g.reshape(w_.shape) for g, w_ in zip(small_names_g, small_w)]
    small_delta, small_newm, small_newv = unpack(s_delta), unpack(s_m), unpack(s_v)

    ada_delta, ada_m, ada_v = _adamw(w_ada[0], g_w_ada, m_w_ada[0], v_w_ada[0], "adamw_ada")

    recv = [recv_w_in] + list(recv_early)
    big = {}
    for nm, parts, w_, m_, v_ in (("w_in", recv[0], w_in, m_w_in, v_w_in),
                                  ("w_proj_sb", recv[1], w_proj_sb, m_w_proj_sb, v_w_proj_sb),
                                  ("w_proj_dn", recv[2], w_proj_dn, m_w_proj_dn, v_w_proj_dn),
                                  ("w_out", recv[3], w_out, m_w_out, v_w_out),
                                  ("w_ffn_in", recv[4], w_ffn_in, m_w_ffn_in, v_w_ffn_in),
                                  ("w_ffn_out", recv[5], w_ffn_out, m_w_ffn_out, v_w_ffn_out)):
        big[nm] = [t[None] for t in _sum_adamw(parts, w_[0], m_[0], v_[0], "adamw_" + nm)]

    sg = dict(zip(["b_ada", "norm1_w", "dn_A_log", "dn_dt_bias", "dn_norm_w", "norm2_w", "ffn_conv_b", "final_norm_w",
                   "dn_conv_w", "ffn_conv_w"], range(10)))

    def small_out(table, nm):
        val = table[sg[nm]]
        return val[None] if nm in ("dn_conv_w", "ffn_conv_w") else val

    order = ["w_ada", "b_ada", "norm1_w", "w_in", "dn_conv_w", "dn_A_log", "dn_dt_bias", "dn_norm_w", "w_proj_sb",
             "w_proj_dn", "w_out", "norm2_w", "w_ffn_in", "ffn_conv_w", "ffn_conv_b", "w_ffn_out", "final_norm_w"]
    groups = []
    for k, small_table in enumerate((small_grads, small_delta, small_newm, small_newv)):
        row = []
        for nm in order:
            if nm == "w_ada":
                row.append((g_w_ada, ada_delta, ada_m, ada_v)[k][None])
            elif nm in big:
                row.append(big[nm][k])
            else:
                row.append(small_out(small_table, nm))
        groups.append(row)
    return (loss, grad_x[None], *groups[0], *groups[1], *groups[2], *groups[3])
```

```python
import functools

import jax
import jax.numpy as jnp
from jax import lax
from jax.experimental import pallas as pl
from jax.experimental.pallas import tpu as pltpu

f32 = jnp.float32
bf16 = jnp.bfloat16

D_MODEL = 1024
SB_HEADS = 8
SB_HEAD_DIM = 64
SB_WIDTH = SB_HEADS * SB_HEAD_DIM
SB_QBLOCK = 128
DN_HEADS = 8
DN_KEY_DIM = 64
DN_VAL_DIM = 128
DN_QK_WIDTH = DN_HEADS * DN_KEY_DIM
DN_V_WIDTH = DN_HEADS * DN_VAL_DIM
DN_CONV_CH = 2 * DN_QK_WIDTH + DN_V_WIDTH
DN_CONV_WIDTH = 4
DN_CHUNK = 64
D_FF = 2816
FFN_CONV_WIDTH = 3
NORM_EPS = 1e-6
L2_EPS = 1e-6
ADAM_LR = 0.001
ADAM_B1 = 0.9
ADAM_B2 = 0.999
ADAM_EPS = 1e-08
ADAM_WD = 0.01
ADAM_STEP = 10

N_DEV = 8
MESH = pl.DeviceIdType.MESH

LANES = 128
SUBLANES = 8
VMEM_LIMIT = 48 * 1024 * 1024

OFF_GA = 0
OFF_GB = D_MODEL
OFF_DN = 2 * D_MODEL
OFF_Z = OFF_DN + DN_CONV_CH
OFF_SBQ = OFF_Z + DN_V_WIDTH
OFF_SBK = OFF_SBQ + SB_WIDTH
OFF_SBV = OFF_SBK + SB_WIDTH
MAIN_WIDTH = OFF_SBV + SB_WIDTH

TM = 256
TCONV_R = 512
TCONV_C = 512
TCONV_FF = D_FF // 2
SB_PAIRS_FWD = 4
SB_PAIRS_BWD = 4
SB_DEAD = -106.0
SB_NEVER = -1e30


def _cparams(sem=None):
    return pltpu.CompilerParams(dimension_semantics=sem, vmem_limit_bytes=VMEM_LIMIT)


def _pick(n, cands):
    for c in cands:
        if n % c == 0:
            return c
    return n


def _my_pos():
    return lax.axis_index("x"), lax.axis_index("y"), lax.axis_index("c")


def _flip(v, bit):
    return 1 - v if bit else v


def _comm_scratch(n):
    return [pltpu.SemaphoreType.DMA((n, 7)), pltpu.SemaphoreType.DMA((n, 7)), pltpu.SemaphoreType.DMA((n,))]


def _gather_protocol(ins, outs, send_sems, recv_sems, local_sems):
    n = len(ins)
    x, y, c = _my_pos()
    me, sibling = (x, y, c), (x, y, 1 - c)
    chips = [(1 - x, y), (x, 1 - y), (1 - x, 1 - y)]

    def slot(out, pos):
        return out.at[4 * pos[0] + 2 * pos[1] + pos[2]]

    def copy(a, k, block, to, src=None):
        return pltpu.make_async_remote_copy(
            src_ref=slot(outs[a], block) if src is None else src, dst_ref=slot(outs[a], block),
            send_sem=send_sems.at[a, k], recv_sem=recv_sems.at[a, k], device_id=to, device_id_type=MESH)

    def local(a):
        return pltpu.make_async_copy(ins[a], slot(outs[a], me), local_sems.at[a])

    def first(a):
        return [copy(a, 0, me, sibling, src=ins[a])] + [copy(a, 1 + j, me, (*chip, c), src=ins[a])
                                                         for j, chip in enumerate(chips)]

    def start():
        for a in range(n):
            local(a).start()
            for cp in first(a):
                cp.start()

    def finish():
        forwards = []
        for a in range(n):
            for j, chip in enumerate(chips):
                copy(a, 1 + j, (*chip, c), me).wait_recv()
                fwd = copy(a, 4 + j, (*chip, c), sibling)
                fwd.start()
                forwards.append(fwd)
        for a in range(n):
            copy(a, 0, sibling, me).wait_recv()
            for j, chip in enumerate(chips):
                copy(a, 4 + j, (*chip, 1 - c), me).wait_recv()
        for a in range(n):
            for cp in first(a):
                cp.wait_send()
        for cp in forwards:
            cp.wait_send()
        for a in range(n):
            local(a).wait()

    return start, finish


def _exchange_protocol(ins, outs, send_sems, recv_sems, local_sems):
    n = len(ins)
    x, y, c = _my_pos()
    me_idx = 4 * x + 2 * y + c

    def local(a):
        return pltpu.make_async_copy(ins[a].at[me_idx], outs[a].at[me_idx], local_sems.at[a])

    def copies(a, m):
        peer = (_flip(x, m & 4), _flip(y, m & 2), _flip(c, m & 1))
        peer_idx = 4 * peer[0] + 2 * peer[1] + peer[2]
        sems = dict(send_sem=send_sems.at[a, m - 1], recv_sem=recv_sems.at[a, m - 1], device_id=peer,
                    device_id_type=MESH)
        send = pltpu.make_async_remote_copy(src_ref=ins[a].at[peer_idx], dst_ref=outs[a].at[me_idx], **sems)
        recv = pltpu.make_async_remote_copy(src_ref=ins[a].at[peer_idx], dst_ref=outs[a].at[peer_idx], **sems)
        return send, recv

    def start():
        for a in range(n):
            local(a).start()
            for m in range(1, N_DEV):
                copies(a, m)[0].start()

    def finish():
        for a in range(n):
            for m in range(1, N_DEV):
                copies(a, m)[1].wait_recv()
        for a in range(n):
            for m in range(1, N_DEV):
                copies(a, m)[0].wait_send()
            local(a).wait()

    return start, finish


def _collective_call(protocol, arrs, out_shapes, name):
    n = len(arrs)

    def body(*refs):
        start, finish = protocol(refs[:n], refs[n:2 * n], *refs[2 * n:])
        start()
        finish()

    any_spec = pl.BlockSpec(memory_space=pl.ANY)
    return pl.pallas_call(body, name=name, out_shape=out_shapes, in_specs=[any_spec] * n, out_specs=[any_spec] * n,
                          scratch_shapes=_comm_scratch(n))(*arrs)


def _gathered_shapes(arrs):
    return [jax.ShapeDtypeStruct((N_DEV,) + a.shape, a.dtype) for a in arrs]


def _all_gather(arrs, name):
    return _collective_call(_gather_protocol, arrs, _gathered_shapes(arrs), name)


MM_BLOCK_BYTES = 4 * 1024 * 1024


def _mm_tiles(m_dim, n_dim, k_dim, a_bytes, b_bytes):
    tm = _pick(m_dim, (1024, 512, 256, 128))
    tn = _pick(n_dim, (1024, 512, 256, 128))
    tk = k_dim
    if k_dim % LANES == 0:
        units = k_dim // LANES
        fits = [u for u in range(1, units + 1) if units % u == 0
                and u * LANES * max(tm * a_bytes, tn * b_bytes) <= MM_BLOCK_BYTES]
        tk = max(fits) * LANES
    return tm, tn, tk


def _mm(a, b, *, ta=False, tb=False, name, side=None):
    (k_dim, m_dim) = a.shape if ta else a.shape[::-1]
    (n_dim, kb_dim) = b.shape if tb else b.shape[::-1]
    assert k_dim == kb_dim, (a.shape, b.shape, ta, tb)
    tm, tn, tk = _mm_tiles(m_dim, n_dim, k_dim, a.dtype.itemsize, b.dtype.itemsize)
    nk = k_dim // tk
    grid = (m_dim // tm, n_dim // tn, nk)
    dims = (((0 if ta else 1,), (1 if tb else 0,)), ((), ()))
    ns = side.n if side else 0

    def body(a_ref, b_ref, *rest):
        if side:
            side.run(rest[:ns], rest[ns + 1:2 * ns + 1], rest[2 * ns + 1:], *_grid_ends(grid),
                     lambda: compute(a_ref, b_ref, rest[ns]))
        else:
            compute(a_ref, b_ref, rest[0])

    def compute(a_ref, b_ref, o_ref):
        part = lax.dot_general(a_ref[...].astype(bf16), b_ref[...].astype(bf16), dims, preferred_element_type=f32)
        if nk == 1:
            o_ref[...] = part
        else:
            k = pl.program_id(2)

            @pl.when(k == 0)
            def _():
                o_ref[...] = part

            @pl.when(k > 0)
            def _():
                o_ref[...] += part

    a_spec = pl.BlockSpec((tk, tm), lambda i, j, k: (k, i)) if ta else pl.BlockSpec((tm, tk), lambda i, j, k: (i, k))
    b_spec = pl.BlockSpec((tn, tk), lambda i, j, k: (j, k)) if tb else pl.BlockSpec((tk, tn), lambda i, j, k: (k, j))
    out_spec = pl.BlockSpec((tm, tn), lambda i, j, k: (i, j))
    out_shape = jax.ShapeDtypeStruct((m_dim, n_dim), f32)
    if not side:
        return pl.pallas_call(body, name=name, grid=grid, in_specs=[a_spec, b_spec], out_specs=out_spec,
                              out_shape=out_shape,
                              compiler_params=_cparams(("parallel", "parallel", "arbitrary")))(a, b)
    return pl.pallas_call(
        body, name=name, grid=grid, in_specs=[a_spec, b_spec] + side.specs(), out_specs=[out_spec] + side.specs(),
        out_shape=[out_shape] + side.out_shapes, scratch_shapes=_comm_scratch(ns),
        compiler_params=_cparams(("arbitrary", "arbitrary", "arbitrary")))(a, b, *side.arrs)


def _win(t):
    return t if isinstance(t, tuple) else (t, t.shape[1], 0)


def _tile_spec(width, cb, tm):
    return pl.BlockSpec((tm, width), lambda i: (i, cb))


def _param_spec(p):
    return pl.BlockSpec(p.shape, lambda i: (0, 0))


def _stage_fwd(f, params, tiles, out_dtypes, name):
    tiles = [_win(t) for t in tiles]
    rows = tiles[0][0].shape[0]
    tm = min(TM, rows)
    avals = jax.eval_shape(f, *[jax.ShapeDtypeStruct(p.shape, f32) for p in params],
                           *[jax.ShapeDtypeStruct((tm, w), f32) for _, w, _ in tiles])
    n_p, n_t = len(params), len(tiles)

    def body(*refs):
        p = [r[...] for r in refs[:n_p]]
        t = [r[...].astype(f32) for r in refs[n_p:n_p + n_t]]
        for o_ref, val in zip(refs[n_p + n_t:], f(*p, *t)):
            o_ref[...] = val.astype(o_ref.dtype)

    return pl.pallas_call(
        body, name=name, grid=(rows // tm,),
        in_specs=[_param_spec(p) for p in params] + [_tile_spec(w, cb, tm) for _, w, cb in tiles],
        out_specs=[_tile_spec(a.shape[1], 0, tm) for a in avals],
        out_shape=[jax.ShapeDtypeStruct((rows, a.shape[1]), dt) for a, dt in zip(avals, out_dtypes)],
        compiler_params=_cparams(("parallel",)),
    )(*params, *[t[0] for t in tiles])


def _stage_bwd(f, params, tiles, cts, grad_dtypes, name, residual=None):
    tiles = [_win(t) for t in tiles]
    rows = tiles[0][0].shape[0]
    tm = min(TM, rows)
    cts = [list(g) if isinstance(g, (list, tuple)) else [g] for g in cts]
    flat_cts = [a for g in cts for a in g]
    n_p, n_t, n_c = len(params), len(tiles), len(flat_cts)
    has_res = residual is not None
    want = [j for j, dt in enumerate(grad_dtypes) if dt is not None]

    def body(*refs):
        i = pl.program_id(0)
        p = [r[...] for r in refs[:n_p]]
        t = [r[...].astype(f32) for r in refs[n_p:n_p + n_t]]
        ct_vals = [r[...].astype(f32) for r in refs[n_p + n_t:n_p + n_t + n_c]]
        ct, at = [], 0
        for g in cts:
            ct.append(functools.reduce(jnp.add, ct_vals[at:at + len(g)]))
            at += len(g)
        ct = tuple(ct)
        pos = n_p + n_t + n_c
        res_ref = refs[pos] if has_res else None
        pos += 1 if has_res else 0
        dp_refs = refs[pos:pos + n_p]
        dt_refs = refs[pos + n_p:]
        _, vjp = jax.vjp(f, *p, *t)
        grads = vjp(ct)

        @pl.when(i == 0)
        def _():
            for r in dp_refs:
                r[...] = jnp.zeros_like(r)

        for r, g in zip(dp_refs, grads[:n_p]):
            r[...] += g
        for r, j in zip(dt_refs, want):
            g = grads[n_p + j]
            if has_res and j == residual[0]:
                g = g + res_ref[...].astype(f32)
            r[...] = g.astype(r.dtype)

    in_arrays = list(params) + [t[0] for t in tiles] + flat_cts
    in_specs = ([_param_spec(p) for p in params] + [_tile_spec(w, cb, tm) for _, w, cb in tiles]
                + [_tile_spec(c.shape[1], 0, tm) for c in flat_cts])
    if has_res:
        in_arrays.append(residual[1])
        in_specs.append(_tile_spec(residual[1].shape[1], 0, tm))
    out_shape = ([jax.ShapeDtypeStruct(p.shape, f32) for p in params]
                 + [jax.ShapeDtypeStruct((rows, tiles[j][1]), grad_dtypes[j]) for j in want])
    out_specs = [_param_spec(p) for p in params] + [_tile_spec(tiles[j][1], 0, tm) for j in want]
    outs = pl.pallas_call(
        body, name=name, grid=(rows // tm,), in_specs=in_specs, out_specs=out_specs, out_shape=out_shape,
        compiler_params=_cparams(("arbitrary",)),
    )(*in_arrays)
    return outs[:n_p], outs[n_p:]


def _rms(x, w):
    return x * lax.rsqrt(jnp.mean(x * x, axis=-1, keepdims=True) + NORM_EPS) * w


def _f_normmod(w, shift, scale, x):
    return (_rms(x, w) * (1.0 + scale) + shift,)


def _f_merge(ga, gb, pa, pb):
    return (jax.nn.sigmoid(ga) * pa + jax.nn.sigmoid(gb) * pb,)


def _f_residual_normmod(gate, w, shift, scale, x, branch):
    x1 = x + gate * branch
    return x1, _rms(x1, w) * (1.0 + scale) + shift


def _f_loss(gate, wf, x1, fo, target):
    y = _rms(x1 + gate * fo, wf)
    err = jnp.square(y - target)
    return (0.5 * jnp.sum(jnp.mean(err, axis=-1, keepdims=True), axis=0, keepdims=True),)


def _loss_and_grads(gate2, wf, x1, fo, target):
    rows, d = x1.shape
    tm = min(TM, rows)

    def body(g_ref, w_ref, x_ref, fo_ref, t_ref, loss_ref, dg_ref, dw_ref, dx_ref, dfo_ref):
        i = pl.program_id(0)
        (val,), vjp = jax.vjp(_f_loss, g_ref[...], w_ref[...], x_ref[...], fo_ref[...], t_ref[...])
        dg, dw, dx, dfo, _ = vjp((jnp.ones((1, 1), f32),))

        @pl.when(i == 0)
        def _():
            loss_ref[...] = jnp.zeros_like(loss_ref)
            dg_ref[...] = jnp.zeros_like(dg_ref)
            dw_ref[...] = jnp.zeros_like(dw_ref)

        loss_ref[...] += jnp.broadcast_to(val, loss_ref.shape)
        dg_ref[...] += dg
        dw_ref[...] += dw
        dx_ref[...] = dx
        dfo_ref[...] = dfo.astype(bf16)

    vec = pl.BlockSpec((1, d), lambda i: (0, 0))
    tile = pl.BlockSpec((tm, d), lambda i: (i, 0))
    return pl.pallas_call(
        body, name="loss_fwd_bwd", grid=(rows // tm,),
        in_specs=[vec, vec, tile, tile, tile],
        out_specs=[pl.BlockSpec((1, LANES), lambda i: (0, 0)), vec, vec, tile, tile],
        out_shape=[jax.ShapeDtypeStruct((1, LANES), f32), jax.ShapeDtypeStruct((1, d), f32),
                   jax.ShapeDtypeStruct((1, d), f32), jax.ShapeDtypeStruct((rows, d), f32),
                   jax.ShapeDtypeStruct((rows, d), bf16)],
        compiler_params=_cparams(("arbitrary",)),
    )(gate2, wf, x1, fo, target)


def _softplus(z):
    return jnp.maximum(z, 0.0) + jnp.log(1.0 + jnp.exp(-jnp.abs(z)))


def _split_dot(a, m):
    hi = a.astype(bf16)
    lo = (a - hi.astype(f32)).astype(bf16)
    return jnp.dot(hi, m, preferred_element_type=f32) + jnp.dot(lo, m, preferred_element_type=f32)


def _suffix_matrix(n):
    r = lax.broadcasted_iota(jnp.int32, (n, n), 0)
    c = lax.broadcasted_iota(jnp.int32, (n, n), 1)
    return (r > c).astype(bf16)


def _head_masks():
    lane = lax.broadcasted_iota(jnp.int32, (1, LANES), 1)
    return [(lane < SB_HEAD_DIM).astype(f32), (lane >= SB_HEAD_DIM).astype(f32)]


def _sb_prepare(proj):
    def f(k, v):
        lane = lax.broadcasted_iota(jnp.int32, (1, SB_WIDTH), 1)
        m0 = (jnp.bitwise_and(lane, LANES - 1) < SB_HEAD_DIM).astype(f32)
        m1 = 1.0 - m0
        return k, k * m0, k * m1, v, v * m0, v * m1

    wins = [(proj, SB_WIDTH, OFF_SBK // SB_WIDTH), (proj, SB_WIDTH, OFF_SBV // SB_WIDTH)]
    return _stage_fwd(f, [], wins, [bf16] * 6, "sb_prepare")


def _stack_heads(x):
    m0, m1 = _head_masks()
    return jnp.concatenate([x * m0, x * m1], axis=0)


def _sb_logits(qst, k, t_pos2, kb, bq, masked):
    z = lax.dot_general(qst, k, (((1,), (1,)), ((), ())), preferred_element_type=f32)
    l = -_softplus(z)
    if masked:
        s_pos = kb * bq + lax.broadcasted_iota(jnp.int32, (1, bq), 1)
        causal = s_pos < t_pos2
        l = jnp.where(causal, l, 0.0)
    else:
        causal = None
    return z, l, causal


class _SideComm:
    def __init__(self, protocol, arrs, out_shapes):
        self.protocol, self.arrs, self.out_shapes = protocol, list(arrs), list(out_shapes)
        self.n = len(self.arrs)

    def specs(self):
        return [pl.BlockSpec(memory_space=pl.ANY)] * self.n

    def run(self, in_refs, out_refs, sems, first, last, compute):
        start, finish = self.protocol(in_refs, out_refs, *sems)
        pl.when(first)(start)
        compute()
        pl.when(last)(finish)


def _grid_ends(grid):
    ids = [pl.program_id(axis) for axis in range(len(grid))]
    first = functools.reduce(jnp.logical_and, [i == 0 for i in ids])
    last = functools.reduce(jnp.logical_and, [i == g - 1 for i, g in zip(ids, grid)])
    return first, last


def _sb_attention_fwd2(proj, k16, v0_16, v1_16, side=None):
    rows = proj.shape[0]
    bq = SB_QBLOCK
    nq = rows // bq
    assert nq <= LANES, "one lane per key block"
    npair = SB_WIDTH // LANES
    scale = SB_HEAD_DIM ** -0.5

    npp = SB_PAIRS_FWD
    wq = npp * LANES
    grid = (npair // npp, nq)
    ns = side.n if side else 0

    def body(q_ref, k_ref, v0_ref, v1_ref, *rest):
        o_ref, runs_ref = rest[ns], rest[ns + 1]
        if side:
            side.run(rest[:ns], rest[ns + 2:2 * ns + 2], rest[2 * ns + 2:], *_grid_ends(grid),
                     lambda: compute(q_ref, k_ref, v0_ref, v1_ref, o_ref, runs_ref))
        else:
            compute(q_ref, k_ref, v0_ref, v1_ref, o_ref, runs_ref)

    def compute(q_ref, k_ref, v0_ref, v1_ref, o_ref, runs_ref):
        qi = pl.program_id(1)
        pairs = [slice(pp * LANES, (pp + 1) * LANES) for pp in range(npp)]
        qst = [(_stack_heads(q_ref[:, s]) * scale).astype(bf16) for s in pairs]
        r = lax.broadcasted_iota(jnp.int32, (bq, 2 * bq), 0)
        c = lax.broadcasted_iota(jnp.int32, (bq, 2 * bq), 1)
        m2 = jnp.logical_or(r > c, c >= bq).astype(bf16)
        t_pos = qi * bq + lax.broadcasted_iota(jnp.int32, (bq, 1), 0)
        t_pos2 = jnp.concatenate([t_pos, t_pos], axis=0)
        lane = lax.broadcasted_iota(jnp.int32, (1, LANES), 1)
        runs_ref[...] = jnp.full(runs_ref.shape, SB_NEVER, f32)

        def tiles(kbs, carry, masked):
            jobs = [(pp, kb) for kb in kbs for pp in range(npp)]
            rows_k = [pl.ds(pl.multiple_of(kb * bq, bq), bq) for _, kb in jobs]
            zl = [_sb_logits(qst[pp], k_ref[rk, pairs[pp]], t_pos2, kb, bq, masked) for (pp, kb), rk in zip(jobs, rows_k)]
            cs = [_split_dot(l, m2) for _, l, _ in zl]
            run = [cr[0] for cr in carry]
            acc = [cr[1] for cr in carry]
            probs = []
            for (pp, kb), (z, l, causal), cs2 in zip(jobs, zl, cs):
                a = jnp.exp(z + l + cs2[:, :bq] + run[pp])
                if masked:
                    a = jnp.where(causal, a, 0.0)
                probs.append(a.astype(bf16))
                for hh in range(2):
                    cols = slice((2 * pp + hh) * LANES, (2 * pp + hh + 1) * LANES)
                    runs_ref[:, cols] = jnp.where(lane == kb, run[pp][hh * bq:(hh + 1) * bq], runs_ref[:, cols])
                run[pp] = run[pp] + cs2[:, bq:]
            for (pp, kb), rk, ab in zip(jobs, rows_k, probs):
                acc[pp] = (acc[pp] + jnp.dot(ab[:bq], v0_ref[rk, pairs[pp]], preferred_element_type=f32)
                           + jnp.dot(ab[bq:], v1_ref[rk, pairs[pp]], preferred_element_type=f32))
            return tuple(zip(run, acc))

        zero = (jnp.zeros((2 * bq, bq), f32), jnp.zeros((bq, LANES), f32))
        carry = tiles([qi], (zero,) * npp, True)

        def alive(cr):
            return functools.reduce(jnp.maximum, [jnp.max(run) for run, _ in cr]) > SB_DEAD

        def one(state):
            i, _, cr = state
            cr = tiles([qi - 1 - i], cr, False)
            return i + 1, alive(cr), cr

        _, _, carry = lax.while_loop(lambda st: jnp.logical_and(st[0] < qi, st[1]), one,
                                     (jnp.int32(0), alive(carry), carry))
        for pp in range(npp):
            o_ref[:, pairs[pp]] = carry[pp][1]

    kv = pl.BlockSpec((rows, wq), lambda p, i: (0, p))
    return pl.pallas_call(
        body, name="sb_attn_fwd", grid=grid,
        in_specs=[pl.BlockSpec((bq, wq), lambda p, i: (i, OFF_SBQ // wq + p)), kv, kv, kv] + (side.specs() if side else []),
        out_specs=[pl.BlockSpec((bq, wq), lambda p, i: (i, p)),
                   pl.BlockSpec((bq, 2 * wq), lambda p, i: (i, p))] + (side.specs() if side else []),
        out_shape=[jax.ShapeDtypeStruct((rows, SB_WIDTH), f32),
                   jax.ShapeDtypeStruct((rows, SB_HEADS * LANES), f32)] + (side.out_shapes if side else []),
        scratch_shapes=_comm_scratch(ns) if side else [],
        compiler_params=_cparams(("arbitrary", "arbitrary")),
    )(proj, k16, v0_16, v1_16, *(side.arrs if side else []))


def _sb_attention_bwd2(proj, k16, k0_16, k1_16, v16, runs, do, side=None):
    rows = proj.shape[0]
    bq = SB_QBLOCK
    nq = rows // bq
    npair = SB_WIDTH // LANES
    scale = SB_HEAD_DIM ** -0.5
    tn = (((0,), (0,)), ((), ()))
    nt = (((1,), (1,)), ((), ()))

    npp = SB_PAIRS_BWD
    wq = npp * LANES
    grid = (npair // npp, nq)
    ns = side.n if side else 0

    def body(q_ref, k_ref, k0_ref, k1_ref, v_ref, runs_ref, do_ref, *rest):
        outs = rest[ns:ns + 3]
        ins = (q_ref, k_ref, k0_ref, k1_ref, v_ref, runs_ref, do_ref)
        if side:
            side.run(rest[:ns], rest[ns + 3:2 * ns + 3], rest[2 * ns + 3:], *_grid_ends(grid),
                     lambda: compute(*ins, *outs))
        else:
            compute(*ins, *outs)

    def compute(q_ref, k_ref, k0_ref, k1_ref, v_ref, runs_ref, do_ref, dq_ref, dk_ref, dv_ref):
        qi = pl.program_id(1)

        @pl.when(qi == 0)
        def _():
            dk_ref[...] = jnp.zeros_like(dk_ref)
            dv_ref[...] = jnp.zeros_like(dv_ref)

        pairs = [slice(pp * LANES, (pp + 1) * LANES) for pp in range(npp)]
        qst = [(_stack_heads(q_ref[:, s]) * scale).astype(bf16) for s in pairs]
        dost = [_stack_heads(do_ref[:, s]).astype(bf16) for s in pairs]
        runs = [jnp.concatenate([runs_ref[:, 2 * pp * LANES:(2 * pp + 1) * LANES],
                                 runs_ref[:, (2 * pp + 1) * LANES:(2 * pp + 2) * LANES]], axis=0) for pp in range(npp)]
        r = lax.broadcasted_iota(jnp.int32, (bq, 2 * bq), 0)
        c = lax.broadcasted_iota(jnp.int32, (bq, 2 * bq), 1)
        suffix_m = _suffix_matrix(bq)
        m2 = jnp.logical_or(r < c, c >= bq).astype(bf16)
        t_pos = qi * bq + lax.broadcasted_iota(jnp.int32, (bq, 1), 0)
        t_pos2 = jnp.concatenate([t_pos, t_pos], axis=0)
        lane = lax.broadcasted_iota(jnp.int32, (1, LANES), 1)

        def tiles(kbs, carry, masked):
            jobs = [(pp, kb) for kb in kbs for pp in range(npp)]
            rows_k = [pl.ds(pl.multiple_of(kb * bq, bq), bq) for _, kb in jobs]
            zl = [_sb_logits(qst[pp], k_ref[rk, pairs[pp]], t_pos2, kb, bq, masked) for (pp, kb), rk in zip(jobs, rows_k)]
            das = [lax.dot_general(dost[pp], v_ref[rk, pairs[pp]], nt, preferred_element_type=f32)
                   for (pp, kb), rk in zip(jobs, rows_k)]
            sticks = [_split_dot(l, suffix_m) for _, l, _ in zl]
            probs, ps = [], []
            for (pp, kb), (z, l, causal), stick, da in zip(jobs, zl, sticks, das):
                run = jnp.sum(jnp.where(lane == kb, runs[pp], 0.0), axis=1, keepdims=True)
                a = jnp.exp(z + l + stick + run)
                if masked:
                    a = jnp.where(causal, a, 0.0)
                probs.append(a.astype(bf16))
                ps.append(da * a)
            pcs = [_split_dot(p, m2) for p in ps]
            pref = [cr[0] for cr in carry]
            dq_acc = [cr[1] for cr in carry]
            dzs = []
            for (pp, kb), (z, l, causal), p, pc2 in zip(jobs, zl, ps, pcs):
                dz = p * jnp.exp(l) - jnp.exp(z + l) * (pc2[:, :bq] + pref[pp])
                if masked:
                    dz = jnp.where(causal, dz, 0.0)
                dzs.append(dz.astype(bf16))
                pref[pp] = pref[pp] + pc2[:, bq:]
            for (pp, kb), rk, dzb, ab in zip(jobs, rows_k, dzs, probs):
                cols = pairs[pp]
                dq_acc[pp] = (dq_acc[pp] + jnp.dot(dzb[:bq], k0_ref[rk, cols], preferred_element_type=f32)
                              + jnp.dot(dzb[bq:], k1_ref[rk, cols], preferred_element_type=f32))
                dk_ref[rk, cols] += lax.dot_general(dzb, qst[pp], tn, preferred_element_type=f32)
                dv_ref[rk, cols] += lax.dot_general(ab, dost[pp], tn, preferred_element_type=f32)
            return tuple(zip(pref, dq_acc))

        zero = (jnp.zeros((2 * bq, bq), f32), jnp.zeros((bq, LANES), f32))
        colmax = functools.reduce(jnp.maximum, [jnp.max(x, axis=0, keepdims=True) for x in runs])
        live = jnp.logical_and(colmax > SB_DEAD, lane < qi)
        kb0 = jnp.minimum(jnp.min(jnp.where(live, lane, LANES)), qi)
        carry = lax.fori_loop(kb0, qi, lambda kb, cr: tiles([kb], cr, False), (zero,) * npp)
        carry = tiles([qi], carry, True)
        for pp in range(npp):
            dq_ref[:, pairs[pp]] = (carry[pp][1] * scale).astype(dq_ref.dtype)

    blk = pl.BlockSpec((bq, wq), lambda p, i: (i, p))
    full = pl.BlockSpec((rows, wq), lambda p, i: (0, p), pipeline_mode=pl.Buffered(1))
    return pl.pallas_call(
        body, name="sb_attn_bwd", grid=grid,
        in_specs=[pl.BlockSpec((bq, wq), lambda p, i: (i, OFF_SBQ // wq + p)), full, full, full, full,
                  pl.BlockSpec((bq, 2 * wq), lambda p, i: (i, p)), blk] + (side.specs() if side else []),
        out_specs=[blk, full, full] + (side.specs() if side else []),
        out_shape=[jax.ShapeDtypeStruct((rows, SB_WIDTH), bf16), jax.ShapeDtypeStruct((rows, SB_WIDTH), f32),
                   jax.ShapeDtypeStruct((rows, SB_WIDTH), f32)] + (side.out_shapes if side else []),
        scratch_shapes=_comm_scratch(ns) if side else [],
        compiler_params=_cparams(("arbitrary", "arbitrary")),
    )(proj, k16, k0_16, k1_16, v16, runs, do, *(side.arrs if side else []))


def _shift_down(x, prev8, j):
    if j == 0:
        return x
    r = pltpu.roll(x, j, axis=0)
    row8 = lax.broadcasted_iota(jnp.int32, prev8.shape, 0)
    head = jnp.where(row8 < j, pltpu.roll(prev8, j, axis=0), r[0:SUBLANES])
    return jnp.concatenate([head, r[SUBLANES:]], axis=0)


def _shift_up(x, next8, j):
    if j == 0:
        return x
    n = x.shape[0]
    r = pltpu.roll(x, n - j, axis=0)
    row8 = lax.broadcasted_iota(jnp.int32, next8.shape, 0)
    tail = jnp.where(row8 >= SUBLANES - j, pltpu.roll(next8, SUBLANES - j, axis=0), r[n - SUBLANES:n])
    return jnp.concatenate([r[:n - SUBLANES], tail], axis=0)


def _conv(x, prev8, w):
    k_taps = w.shape[0]
    out = x * w[k_taps - 1:k_taps, :]
    for j in range(1, k_taps):
        out = out + _shift_down(x, prev8, j) * w[k_taps - 1 - j:k_taps - j, :]
    return out


def _conv_tiles(rows, tr_max=TCONV_R):
    tr = min(tr_max, rows)
    return tr, rows // tr, tr // SUBLANES


def _prev_spec(tc, cb0, r8):
    return pl.BlockSpec((SUBLANES, tc), lambda j, i: (jnp.maximum(i * r8 - 1, 0), cb0 + j))


def _silu(x):
    return x * jax.nn.sigmoid(x)


def _dsilu(x):
    s = jax.nn.sigmoid(x)
    return s * (1.0 + x * (1.0 - s))


def _dn_conv_fwd(proj, w):
    rows = proj.shape[0]
    tr, nr, r8 = _conv_tiles(rows)
    tc = TCONV_C
    cb0 = OFF_DN // tc

    def body(x_ref, p_ref, w_ref, o_ref):
        prev = jnp.where(pl.program_id(1) == 0, 0.0, p_ref[...])
        o_ref[...] = _silu(_conv(x_ref[...], prev, w_ref[...]))

    return pl.pallas_call(
        body, name="dn_conv_fwd", grid=(DN_CONV_CH // tc, nr),
        in_specs=[pl.BlockSpec((tr, tc), lambda j, i: (i, cb0 + j)), _prev_spec(tc, cb0, r8),
                  pl.BlockSpec((DN_CONV_WIDTH, tc), lambda j, i: (0, j))],
        out_specs=pl.BlockSpec((tr, tc), lambda j, i: (i, j)),
        out_shape=jax.ShapeDtypeStruct((rows, DN_CONV_CH), f32),
        compiler_params=_cparams(("parallel", "parallel")),
    )(proj, proj, w)


def _dn_conv_bwd_act(proj, w, dact):
    rows = proj.shape[0]
    tr, nr, r8 = _conv_tiles(rows)
    tc = TCONV_C
    cb0 = OFF_DN // tc

    def body(x_ref, p_ref, w_ref, d_ref, o_ref):
        prev = jnp.where(pl.program_id(1) == 0, 0.0, p_ref[...])
        o_ref[...] = d_ref[...] * _dsilu(_conv(x_ref[...], prev, w_ref[...]))

    return pl.pallas_call(
        body, name="dn_conv_bwd_act", grid=(DN_CONV_CH // tc, nr),
        in_specs=[pl.BlockSpec((tr, tc), lambda j, i: (i, cb0 + j)), _prev_spec(tc, cb0, r8),
                  pl.BlockSpec((DN_CONV_WIDTH, tc), lambda j, i: (0, j)),
                  pl.BlockSpec((tr, tc), lambda j, i: (i, j))],
        out_specs=pl.BlockSpec((tr, tc), lambda j, i: (i, j)),
        out_shape=jax.ShapeDtypeStruct((rows, DN_CONV_CH), f32),
        compiler_params=_cparams(("parallel", "parallel")),
    )(proj, proj, w, dact)


def _ffn_conv_fwd(u_pre, w, b):
    rows = u_pre.shape[0]
    tr, nr, r8 = _conv_tiles(rows, TCONV_R // 2)
    tc = TCONV_FF
    nct = D_FF // tc

    def body(xg_ref, pg_ref, xu_ref, pu_ref, wg_ref, wu_ref, bg_ref, bu_ref, o_ref):
        first = pl.program_id(1) == 0
        ug = _conv(xg_ref[...], jnp.where(first, 0.0, pg_ref[...]), wg_ref[...]) + bg_ref[...]
        uu = _conv(xu_ref[...], jnp.where(first, 0.0, pu_ref[...]), wu_ref[...]) + bu_ref[...]
        o_ref[...] = (_silu(ug) * uu).astype(o_ref.dtype)

    def x_spec(off):
        return pl.BlockSpec((tr, tc), lambda j, i: (i, off + j))

    def w_spec(k, off):
        return pl.BlockSpec((k, tc), lambda j, i: (0, off + j))

    return pl.pallas_call(
        body, name="ffn_conv_fwd", grid=(nct, nr),
        in_specs=[x_spec(0), _prev_spec(tc, 0, r8), x_spec(nct), _prev_spec(tc, nct, r8),
                  w_spec(FFN_CONV_WIDTH, 0), w_spec(FFN_CONV_WIDTH, nct), w_spec(1, 0), w_spec(1, nct)],
        out_specs=pl.BlockSpec((tr, tc), lambda j, i: (i, j)),
        out_shape=jax.ShapeDtypeStruct((rows, D_FF), bf16),
        compiler_params=_cparams(("parallel", "parallel")),
    )(u_pre, u_pre, u_pre, u_pre, w, w, b, b)


def _ffn_conv_bwd_act(u_pre, w, b, dact):
    rows = u_pre.shape[0]
    tr, nr, r8 = _conv_tiles(rows, TCONV_R // 2)
    tc = TCONV_FF
    nct = D_FF // tc

    def body(xg_ref, pg_ref, xu_ref, pu_ref, wg_ref, wu_ref, bg_ref, bu_ref, d_ref,
             du_ref, dbg_ref, dbu_ref):
        i = pl.program_id(1)
        first = i == 0
        ug = _conv(xg_ref[...], jnp.where(first, 0.0, pg_ref[...]), wg_ref[...]) + bg_ref[...]
        uu = _conv(xu_ref[...], jnp.where(first, 0.0, pu_ref[...]), wu_ref[...]) + bu_ref[...]
        d = d_ref[...]
        dug = d * uu * _dsilu(ug)
        duu = d * _silu(ug)
        du_ref[0] = dug
        du_ref[1] = duu

        @pl.when(first)
        def _():
            dbg_ref[...] = jnp.zeros_like(dbg_ref)
            dbu_ref[...] = jnp.zeros_like(dbu_ref)

        dbg_ref[...] += jnp.sum(dug, axis=0, keepdims=True)
        dbu_ref[...] += jnp.sum(duu, axis=0, keepdims=True)

    def x_spec(off):
        return pl.BlockSpec((tr, tc), lambda j, i: (i, off + j))

    def w_spec(k, off):
        return pl.BlockSpec((k, tc), lambda j, i: (0, off + j))

    tile = pl.BlockSpec((tr, tc), lambda j, i: (i, j))
    vec = pl.BlockSpec((1, tc), lambda j, i: (0, j))
    return pl.pallas_call(
        body, name="ffn_conv_bwd_act", grid=(nct, nr),
        in_specs=[x_spec(0), _prev_spec(tc, 0, r8), x_spec(nct), _prev_spec(tc, nct, r8),
                  w_spec(FFN_CONV_WIDTH, 0), w_spec(FFN_CONV_WIDTH, nct), w_spec(1, 0), w_spec(1, nct), tile],
        out_specs=[pl.BlockSpec((2, tr, tc), lambda j, i: (0, i, j)), vec, vec],
        out_shape=[jax.ShapeDtypeStruct((2, rows, D_FF), f32),
                   jax.ShapeDtypeStruct((1, D_FF), f32), jax.ShapeDtypeStruct((1, D_FF), f32)],
        compiler_params=_cparams(("parallel", "arbitrary")),
    )(u_pre, u_pre, u_pre, u_pre, w, w, b, b, dact)


def _conv_bwd(dy, x, x_cb0, w, name):
    k_taps = w.shape[0]
    split = dy.ndim == 3
    rows = dy.shape[-2]
    ch = dy.shape[-1] * (2 if split else 1)
    tc = TCONV_FF if split else TCONV_C
    tr, nr, r8 = _conv_tiles(rows, TCONV_R // 2 if split else TCONV_R)
    per_half = dy.shape[-1] // tc
    last8 = rows // SUBLANES - 1

    def body(dy_ref, nx_ref, x_ref, p_ref, w_ref, dx_ref, dw_ref):
        i = pl.program_id(1)
        dyv = dy_ref[...]
        nxt = jnp.where(i == nr - 1, 0.0, nx_ref[...])
        prev = jnp.where(i == 0, 0.0, p_ref[...])
        xv = x_ref[...].astype(f32)
        wv = w_ref[...]

        @pl.when(i == 0)
        def _():
            dw_ref[...] = jnp.zeros_like(dw_ref)

        dx = dyv * wv[k_taps - 1:k_taps, :]
        dw_ref[k_taps - 1:k_taps, :] += jnp.sum(dyv * xv, axis=0, keepdims=True)
        for j in range(1, k_taps):
            dx = dx + _shift_up(dyv, nxt, j) * wv[k_taps - 1 - j:k_taps - j, :]
            dw_ref[k_taps - 1 - j:k_taps - j, :] += jnp.sum(dyv * _shift_down(xv, prev, j), axis=0, keepdims=True)
        dx_ref[...] = dx.astype(dx_ref.dtype)

    tile = pl.BlockSpec((tr, tc), lambda j, i: (i, j))
    if split:
        dy_spec = pl.BlockSpec((None, tr, tc), lambda j, i: (j // per_half, i, j % per_half))
        next_spec = pl.BlockSpec((None, SUBLANES, tc),
                                 lambda j, i: (j // per_half, jnp.minimum((i + 1) * r8, last8), j % per_half))
    else:
        dy_spec = tile
        next_spec = pl.BlockSpec((SUBLANES, tc), lambda j, i: (jnp.minimum((i + 1) * r8, last8), j))
    return pl.pallas_call(
        body, name=name, grid=(ch // tc, nr),
        in_specs=[dy_spec, next_spec,
                  pl.BlockSpec((tr, tc), lambda j, i: (i, x_cb0 + j)), _prev_spec(tc, x_cb0, r8),
                  pl.BlockSpec((k_taps, tc), lambda j, i: (0, j))],
        out_specs=[tile, pl.BlockSpec((k_taps, tc), lambda j, i: (0, j))],
        out_shape=[jax.ShapeDtypeStruct((rows, ch), bf16), jax.ShapeDtypeStruct((k_taps, ch), f32)],
        compiler_params=_cparams(("parallel", "arbitrary")),
    )(dy, dy, x, x, w)


def _hdot(a, b):
    return jnp.dot(a, b, preferred_element_type=f32, precision=lax.Precision.HIGH)


def _xdot(a, b):
    return jnp.dot(a, b, preferred_element_type=f32, precision=lax.Precision.HIGHEST)


def _bdot(a, b):
    return jnp.dot(a.astype(bf16), b.astype(bf16), preferred_element_type=f32)


def _bdot_nt(a, b):
    return lax.dot_general(a.astype(bf16), b.astype(bf16), (((1,), (1,)), ((), ())), preferred_element_type=f32)


def _bdot_tn(a, b):
    return lax.dot_general(a.astype(bf16), b.astype(bf16), (((0,), (0,)), ((), ())), preferred_element_type=f32)


GDN_GROUP = 4
GDN_NGROUPS = DN_HEADS // GDN_GROUP
GDN_ROWS = GDN_GROUP * DN_CHUNK
GDN_QK_LANES = GDN_GROUP * DN_KEY_DIM
GDN_LOGIT_LANE = DN_HEADS


def _inverse_impl(lows):
    n = lows[0].shape[0]
    r = lax.broadcasted_iota(jnp.int32, (n, n), 0)
    c = lax.broadcasted_iota(jnp.int32, (n, n), 1)
    eye = (r == c).astype(f32)
    blk = jnp.right_shift(r, 3) == jnp.right_shift(c, 3)
    d = [jnp.where(blk, low, 0.0) for low in lows]
    e = [low - x for low, x in zip(lows, d)]

    def nilpotent8_inverse(xs):
        acc = [eye - x for x in xs]
        power = xs
        for _ in range(2):
            power = [_bdot(x, x) for x in power]
            acc = [_bdot(a, eye + x) for a, x in zip(acc, power)]
        return acc

    dinv = nilpotent8_inverse(d)
    ninv = nilpotent8_inverse([_bdot(x, y) for x, y in zip(dinv, e)])
    t = [_bdot(x, y) for x, y in zip(ninv, dinv)]
    for _ in range(2):
        res = [eye - x - _hdot(low, x) for low, x in zip(lows, t)]
        t = [x + _bdot(x, y) for x, y in zip(t, res)]
    return tuple(t)


@jax.custom_vjp
def _unit_lower_inverses(lows):
    return _inverse_impl(lows)


def _unit_lower_inverses_fwd(lows):
    t = _inverse_impl(lows)
    return t, t


def _unit_lower_inverses_bwd(t, ct):
    tn = (((0,), (0,)), ((), ()))
    nt = (((1,), (1,)), ((), ()))
    left = [lax.dot_general(x, g, tn, preferred_element_type=f32, precision=lax.Precision.HIGH) for x, g in zip(t, ct)]
    return (tuple(-lax.dot_general(x, y, nt, preferred_element_type=f32, precision=lax.Precision.HIGH)
                  for x, y in zip(left, t)),)


_unit_lower_inverses.defvjp(_unit_lower_inverses_fwd, _unit_lower_inverses_bwd)


@jax.custom_vjp
def _known_inverses(lows, t):
    return t


def _known_inverses_fwd(lows, t):
    return t, t


def _known_inverses_bwd(t, ct):
    return _unit_lower_inverses_bwd(t, ct) + (tuple(jnp.zeros_like(x) for x in t),)


_known_inverses.defvjp(_known_inverses_fwd, _known_inverses_bwd)


def _gdn_chunk(a_log, dt_bias, norm_w, ba, *per_group, inverses=None, keep_inverses=False):
    ng = GDN_NGROUPS
    qgs, kgs, vsts, zsts, states = [per_group[i * ng:(i + 1) * ng] for i in range(5)]
    groups = range(ng)
    n = GDN_ROWS
    r = lax.broadcasted_iota(jnp.int32, (n, n), 0)
    c = lax.broadcasted_iota(jnp.int32, (n, n), 1)
    same_head = jnp.right_shift(r, 6) == jnp.right_shift(c, 6)
    incl = jnp.logical_and(same_head, r >= c)
    strict = jnp.logical_and(same_head, r > c)
    eye = (r == c).astype(f32)
    ones = jnp.ones((n, n), f32)
    own_lanes = same_head.astype(f32)
    lane = lax.broadcasted_iota(jnp.int32, (1, LANES), 1)
    pick = lambda arr, idx: jnp.sum(jnp.where(lane == idx, arr, 0.0), axis=1, keepdims=True)
    heads = [[GDN_GROUP * g + h for h in range(GDN_GROUP)] for g in groups]
    rc = lax.broadcasted_iota(jnp.int32, (DN_CHUNK, DN_CHUNK), 0)
    cc = lax.broadcasted_iota(jnp.int32, (DN_CHUNK, DN_CHUNK), 1)

    g_all = -jnp.exp(a_log) * _softplus(ba + dt_bias)
    gc_all = _xdot((rc >= cc).astype(f32), g_all)
    gl_all = jnp.sum(g_all, axis=0, keepdims=True)
    beta = [jnp.concatenate([jax.nn.sigmoid(pick(ba, hd)) for hd in heads[g]], axis=0) for g in groups]
    gc = [jnp.concatenate([pick(gc_all, GDN_LOGIT_LANE + hd) for hd in heads[g]], axis=0) for g in groups]
    g_last = [jnp.concatenate([jnp.broadcast_to(pick(gl_all, GDN_LOGIT_LANE + hd), (DN_CHUNK, 1)) for hd in heads[g]],
                              axis=0) for g in groups]
    gr = [jnp.broadcast_to(gc[g], (n, n)).T for g in groups]
    decay = [jnp.where(incl, jnp.exp(jnp.where(incl, gc[g] - gr[g], 0.0)), 0.0) for g in groups]
    q = [jnp.concatenate([qgs[g]] * GDN_GROUP, axis=0) * own_lanes for g in groups]
    k = [jnp.concatenate([kgs[g]] * GDN_GROUP, axis=0) * own_lanes for g in groups]
    qn = [x * lax.rsqrt(jnp.sum(x * x, axis=1, keepdims=True) + L2_EPS) * (DN_KEY_DIM ** -0.5) for x in q]
    kn = [x * lax.rsqrt(jnp.sum(x * x, axis=1, keepdims=True) + L2_EPS) for x in k]
    kb = [kn[g] * beta[g] for g in groups]
    low = [jnp.where(strict, _bdot_nt(kb[g], kn[g]) * decay[g], 0.0) for g in groups]
    intra = [jnp.where(incl, _bdot_nt(qn[g], kn[g]) * decay[g], 0.0) for g in groups]
    t = _unit_lower_inverses(tuple(low)) if inverses is None else _known_inverses(tuple(low), tuple(inverses))
    u = [_bdot(t[g], vsts[g] * beta[g]) for g in groups]
    w = [_bdot(t[g], kb[g] * jnp.exp(gc[g])) for g in groups]
    sb = [s.astype(bf16) for s in states]
    v_new = [u[g] - jnp.dot(w[g].astype(bf16), sb[g], preferred_element_type=f32) for g in groups]
    o = [jnp.dot((qn[g] * jnp.exp(gc[g])).astype(bf16), sb[g], preferred_element_type=f32) for g in groups]
    o = [o[g] + _bdot(intra[g], v_new[g]) for g in groups]
    new_state = [states[g] * jnp.exp(g_last[g]) + _bdot_tn(kn[g] * jnp.exp(g_last[g] - gc[g]), v_new[g])
                 for g in groups]
    o_n = [x * lax.rsqrt(jnp.mean(x * x, axis=1, keepdims=True) + NORM_EPS) * norm_w for x in o]
    return tuple(o_n[g] * _silu(zsts[g]) for g in groups) + tuple(new_state) + (tuple(t) if keep_inverses else ())


def _gdn_specs(rows, reverse):
    n = rows // DN_CHUNK
    idx = (lambda i: n - 1 - i) if reverse else (lambda i: i)
    vec = pl.BlockSpec((1, LANES), lambda i: (0, 0))
    qkv = pl.BlockSpec((DN_CHUNK, DN_CONV_CH), lambda i: (idx(i), 0))
    z = pl.BlockSpec((DN_CHUNK, DN_V_WIDTH), lambda i: (idx(i), OFF_Z // DN_V_WIDTH))
    ba = pl.BlockSpec((DN_CHUNK, LANES), lambda i: (idx(i), 0))
    wide = pl.BlockSpec((DN_CHUNK, DN_V_WIDTH), lambda i: (idx(i), 0))
    st = pl.BlockSpec((1, DN_HEADS * DN_KEY_DIM, LANES), lambda i: (idx(i), 0, 0))
    inv = pl.BlockSpec((1, GDN_NGROUPS * GDN_ROWS, GDN_ROWS), lambda i: (idx(i), 0, 0))
    return n, vec, qkv, z, ba, wide, st, inv


def _gdn_slices(grp):
    q = slice(grp * GDN_QK_LANES, (grp + 1) * GDN_QK_LANES)
    k = slice(DN_QK_WIDTH + grp * GDN_QK_LANES, DN_QK_WIDTH + (grp + 1) * GDN_QK_LANES)
    heads = [slice((GDN_GROUP * grp + h) * LANES, (GDN_GROUP * grp + h + 1) * LANES) for h in range(GDN_GROUP)]
    vs = [slice(2 * DN_QK_WIDTH + s.start, 2 * DN_QK_WIDTH + s.stop) for s in heads]
    return q, k, vs, heads


def _stack_cols(ref, cols):
    return jnp.concatenate([ref[:, s] for s in cols], axis=0)


def _gdn_operands(qkv_ref, z_ref, state_rows):
    sl = [_gdn_slices(grp) for grp in range(GDN_NGROUPS)]
    return ([qkv_ref[:, q] for q, _, _, _ in sl] + [qkv_ref[:, k] for _, k, _, _ in sl]
            + [_stack_cols(qkv_ref, vs) for _, _, vs, _ in sl] + [_stack_cols(z_ref, heads) for _, _, _, heads in sl]
            + [state_rows[grp * GDN_ROWS:(grp + 1) * GDN_ROWS, :] for grp in range(GDN_NGROUPS)])


def _gdn_fwd(a_log, dt_bias, norm_w, qkv_act, proj, ba):
    rows = qkv_act.shape[0]
    n, vec, qkv_s, z_s, ba_s, wide, st_s, inv_s = _gdn_specs(rows, False)

    def body(al_ref, dt_ref, nw_ref, qkv_ref, z_ref, ba_ref, o_ref, st_ref, inv_ref, state):
        @pl.when(pl.program_id(0) == 0)
        def _():
            state[...] = jnp.zeros_like(state)

        st_ref[0] = state[...]
        out = _gdn_chunk(al_ref[...], dt_ref[...], nw_ref[...], ba_ref[...], *_gdn_operands(qkv_ref, z_ref, state),
                         keep_inverses=True)
        for grp in range(GDN_NGROUPS):
            _, _, _, heads = _gdn_slices(grp)
            rs = slice(grp * GDN_ROWS, (grp + 1) * GDN_ROWS)
            for h, s in enumerate(heads):
                o_ref[:, s] = out[grp][h * DN_CHUNK:(h + 1) * DN_CHUNK].astype(o_ref.dtype)
            state[rs, :] = out[GDN_NGROUPS + grp]
            inv_ref[0, rs, :] = out[2 * GDN_NGROUPS + grp]

    return pl.pallas_call(
        body, name="gdn_fwd", grid=(n,),
        in_specs=[vec, vec, vec, qkv_s, z_s, ba_s], out_specs=[wide, st_s, inv_s],
        out_shape=[jax.ShapeDtypeStruct((rows, DN_V_WIDTH), bf16),
                   jax.ShapeDtypeStruct((n, DN_HEADS * DN_KEY_DIM, LANES), f32),
                   jax.ShapeDtypeStruct((n, GDN_NGROUPS * GDN_ROWS, GDN_ROWS), f32)],
        scratch_shapes=[pltpu.VMEM((DN_HEADS * DN_KEY_DIM, LANES), f32)],
        compiler_params=_cparams(("arbitrary",)),
    )(a_log, dt_bias, norm_w, qkv_act, proj, ba)


def _gdn_bwd(a_log, dt_bias, norm_w, qkv_act, proj, ba, states, inverses, do):
    rows = qkv_act.shape[0]
    n, vec, qkv_s, z_s, ba_s, wide, st_s, inv_s = _gdn_specs(rows, True)

    def body(al_ref, dt_ref, nw_ref, qkv_ref, z_ref, ba_ref, st_ref, inv_ref, do_ref,
             dal_ref, ddt_ref, dnw_ref, dqkv_ref, dz_ref, dba_ref, dstate):
        @pl.when(pl.program_id(0) == 0)
        def _():
            dstate[...] = jnp.zeros_like(dstate)
            dal_ref[...] = jnp.zeros_like(dal_ref)
            ddt_ref[...] = jnp.zeros_like(ddt_ref)
            dnw_ref[...] = jnp.zeros_like(dnw_ref)

        ng = GDN_NGROUPS
        kept = [inv_ref[0, grp * GDN_ROWS:(grp + 1) * GDN_ROWS, :] for grp in range(ng)]
        _, vjp = jax.vjp(functools.partial(_gdn_chunk, inverses=kept), al_ref[...], dt_ref[...], nw_ref[...],
                         ba_ref[...], *_gdn_operands(qkv_ref, z_ref, st_ref[0]))
        cts = tuple(_stack_cols(do_ref, _gdn_slices(grp)[3]) for grp in range(ng))
        cts += tuple(dstate[grp * GDN_ROWS:(grp + 1) * GDN_ROWS, :] for grp in range(ng))
        grads = vjp(cts)
        dal_ref[...] += grads[0]
        ddt_ref[...] += grads[1]
        dnw_ref[...] += grads[2]
        dba_ref[...] = grads[3]
        dqs, dks, dvs, dzs, dss = [grads[4 + i * ng:4 + (i + 1) * ng] for i in range(5)]
        for grp in range(ng):
            q, k, vs, heads = _gdn_slices(grp)
            dqkv_ref[:, q] = dqs[grp]
            dqkv_ref[:, k] = dks[grp]
            for h, (sv, sh) in enumerate(zip(vs, heads)):
                rows_h = slice(h * DN_CHUNK, (h + 1) * DN_CHUNK)
                dqkv_ref[:, sv] = dvs[grp][rows_h]
                dz_ref[:, sh] = dzs[grp][rows_h].astype(dz_ref.dtype)
            dstate[grp * GDN_ROWS:(grp + 1) * GDN_ROWS, :] = dss[grp]

    return pl.pallas_call(
        body, name="gdn_bwd", grid=(n,),
        in_specs=[vec, vec, vec, qkv_s, z_s, ba_s, st_s, inv_s, wide],
        out_specs=[vec, vec, vec, qkv_s, wide, ba_s],
        out_shape=[jax.ShapeDtypeStruct((1, LANES), f32)] * 3
        + [jax.ShapeDtypeStruct((rows, DN_CONV_CH), f32), jax.ShapeDtypeStruct((rows, DN_V_WIDTH), bf16),
           jax.ShapeDtypeStruct((rows, LANES), f32)],
        scratch_shapes=[pltpu.VMEM((DN_HEADS * DN_KEY_DIM, LANES), f32)],
        compiler_params=_cparams(("arbitrary",)),
    )(a_log, dt_bias, norm_w, qkv_act, proj, ba, states, inverses, do)


def _ada_fwd(c_all, w_loc, b_loc):
    def body(c_ref, w_ref, b_ref, o_ref):
        o_ref[...] = _bdot(_silu(c_ref[...]), w_ref[...]) + b_ref[...]

    return pl.pallas_call(body, name="ada_fwd", out_shape=jax.ShapeDtypeStruct((c_all.shape[0], w_loc.shape[1]), f32),
                          compiler_params=_cparams())(c_all, w_loc, b_loc)


def _ada_bwd(c_all, dmod_cols):
    def body(c_ref, d_ref, o_ref):
        o_ref[...] = _bdot_tn(_silu(c_ref[...]), d_ref[...])

    return pl.pallas_call(body, name="ada_bwd",
                          out_shape=jax.ShapeDtypeStruct((c_all.shape[1], dmod_cols.shape[1]), f32),
                          compiler_params=_cparams())(c_all, dmod_cols)


def _sum_devices(parts):
    def body(p_ref, o_ref):
        acc = p_ref[0:1, :]
        for d in range(1, N_DEV):
            acc = acc + p_ref[d:d + 1, :]
        o_ref[...] = acc

    return pl.pallas_call(body, name="sum_small", out_shape=jax.ShapeDtypeStruct((1, parts.shape[1]), f32),
                          compiler_params=_cparams())(parts)


def _adam_math(w, g, m, v):
    m2 = ADAM_B1 * m + (1.0 - ADAM_B1) * g
    v2 = ADAM_B2 * v + (1.0 - ADAM_B2) * jnp.square(g)
    m_hat = m2 / (1.0 - ADAM_B1 ** ADAM_STEP)
    v_hat = v2 / (1.0 - ADAM_B2 ** ADAM_STEP)
    delta = -ADAM_LR * (m_hat / (jnp.sqrt(v_hat) + ADAM_EPS) + ADAM_WD * w)
    return delta, m2, v2


def _row_tile(rows):
    return _pick(rows, (256, 128, 64, 32, 16, 8))


def _adamw(w, g, m, v, name):
    rows, cols = w.shape
    tr = _row_tile(rows)

    def body(w_ref, g_ref, m_ref, v_ref, d_ref, m2_ref, v2_ref):
        d_ref[...], m2_ref[...], v2_ref[...] = _adam_math(w_ref[...], g_ref[...], m_ref[...], v_ref[...])

    tile = pl.BlockSpec((tr, cols), lambda i: (i, 0))
    return pl.pallas_call(body, name=name, grid=(rows // tr,), in_specs=[tile] * 4, out_specs=[tile] * 3,
                          out_shape=[jax.ShapeDtypeStruct(w.shape, f32)] * 3,
                          compiler_params=_cparams(("parallel",)))(w, g, m, v)


def _sum_adamw(parts, w, m, v, name):
    rows, cols = w.shape
    tr = _row_tile(rows)

    def body(p_ref, w_ref, m_ref, v_ref, g_ref, d_ref, m2_ref, v2_ref):
        g = p_ref[0].astype(f32)
        for d in range(1, N_DEV):
            g = g + p_ref[d].astype(f32)
        g_ref[...] = g
        d_ref[...], m2_ref[...], v2_ref[...] = _adam_math(w_ref[...], g, m_ref[...], v_ref[...])

    tile = pl.BlockSpec((tr, cols), lambda i: (i, 0))
    return pl.pallas_call(body, name=name, grid=(rows // tr,),
                          in_specs=[pl.BlockSpec((N_DEV, tr, cols), lambda i: (0, i, 0)), tile, tile, tile],
                          out_specs=[tile] * 4, out_shape=[jax.ShapeDtypeStruct(w.shape, f32)] * 4,
                          compiler_params=_cparams(("parallel",)))(parts, w, m, v)


def _pad_lanes(a, width):
    return jnp.pad(a, ((0, 0), (0, width - a.shape[1])))


def _cols_by_device(full):
    r, c = full.shape
    return jnp.moveaxis(full.reshape(r, N_DEV, c // N_DEV), 1, 0)


def _cols_from_devices(parts):
    d, r, n = parts.shape
    return jnp.moveaxis(parts, 0, 1).reshape(r, d * n)


def kernel(x, c, w_ada, b_ada, norm1_w, w_in, dn_conv_w, dn_A_log, dn_dt_bias, dn_norm_w, w_proj_sb, w_proj_dn, w_out, norm2_w, w_ffn_in, ffn_conv_w, ffn_conv_b, w_ffn_out, final_norm_w, loss_target, m_w_ada, m_b_ada, m_norm1_w, m_w_in, m_dn_conv_w, m_dn_A_log, m_dn_dt_bias, m_dn_norm_w, m_w_proj_sb, m_w_proj_dn, m_w_out, m_norm2_w, m_w_ffn_in, m_ffn_conv_w, m_ffn_conv_b, m_w_ffn_out, m_final_norm_w, v_w_ada, v_b_ada, v_norm1_w, v_w_in, v_dn_conv_w, v_dn_A_log, v_dn_dt_bias, v_dn_norm_w, v_w_proj_sb, v_w_proj_dn, v_w_out, v_norm2_w, v_w_ffn_in, v_ffn_conv_w, v_ffn_conv_b, v_w_ffn_out, v_final_norm_w):
    d = D_MODEL
    me = 4 * lax.axis_index("x") + 2 * lax.axis_index("y") + lax.axis_index("c")
    xs = x[0]
    target = loss_target[0]
    n_ada = w_ada.shape[2]
    n_dnc = dn_conv_w.shape[2]
    n_ffc = ffn_conv_w.shape[2]

    small = jnp.concatenate([c, dn_conv_w[0].reshape(1, -1), ffn_conv_w[0].reshape(1, -1)], axis=1)
    small = _pad_lanes(small, -(-small.shape[1] // LANES) * LANES)
    small_g, w_in_g = _all_gather([small, w_in[0].astype(bf16)], "gather_w_in")
    later = [w_proj_sb[0].astype(bf16), w_proj_dn[0].astype(bf16), w_out[0].astype(bf16),
             w_ffn_in[0].astype(bf16), w_ffn_out[0].astype(bf16)]
    gather_later = _SideComm(_gather_protocol, later, _gathered_shapes(later))
    small_g = small_g[:, 0, :]
    c_all = small_g[:, :d]
    dn_cw = _cols_from_devices(small_g[:, d:d + DN_CONV_WIDTH * n_dnc].reshape(N_DEV, DN_CONV_WIDTH, n_dnc))
    o2 = d + DN_CONV_WIDTH * n_dnc
    ffn_cw = _cols_from_devices(small_g[:, o2:o2 + FFN_CONV_WIDTH * n_ffc].reshape(N_DEV, FFN_CONV_WIDTH, n_ffc))

    w_in_full = _cols_from_devices(w_in_g)
    r_sb, r_dn, r_z = 3 * SB_WIDTH, 3 * SB_WIDTH + DN_CONV_CH, 3 * SB_WIDTH + DN_CONV_CH + DN_V_WIDTH
    r_g = r_z + 2 * DN_HEADS
    w_main = jnp.concatenate([w_in_full[:, r_g:], w_in_full[:, r_sb:r_dn], w_in_full[:, r_dn:r_z],
                              w_in_full[:, :r_sb]], axis=1)
    w_ba = _pad_lanes(w_in_full[:, r_z:r_g], LANES)

    b_loc = lax.dynamic_slice(b_ada, (0, me * n_ada), (1, n_ada))
    mod_part = _ada_fwd(c_all, w_ada[0], b_loc)
    (mod_g,) = _all_gather([mod_part], "gather_mod")
    mod = lax.dynamic_index_in_dim(mod_g, me, axis=1, keepdims=False).reshape(1, N_DEV * n_ada)
    shift1, scale1, gate1, shift2, scale2, gate2 = [mod[:, i * d:(i + 1) * d] for i in range(6)]

    logit_lanes = ((0, 0), (GDN_LOGIT_LANE, LANES - GDN_LOGIT_LANE - DN_HEADS))
    a_log = jnp.pad(dn_A_log, logit_lanes)
    dt_b = jnp.pad(dn_dt_bias, logit_lanes)

    (h,) = _stage_fwd(_f_normmod, [norm1_w, shift1, scale1], [xs], [bf16], "norm1_fwd")
    proj = _mm(h, w_main, name="in_proj")
    ba = _mm(h, w_ba, name="in_proj_ba")
    k16, k0_16, k1_16, v16, v0_16, v1_16 = _sb_prepare(proj)
    o_a, sb_runs, w_psb_g, w_pdn_g, w_out_g, w_fin_g, w_fout_g = _sb_attention_fwd2(
        proj, k16, v0_16, v1_16, side=gather_later)
    w_psb = _cols_from_devices(w_psb_g)
    w_pdn = w_pdn_g.reshape(DN_V_WIDTH, d)
    w_o = w_out_g.reshape(d, d)
    w_fin = _cols_from_devices(w_fin_g)
    w_fout = w_fout_g.reshape(D_FF, d)
    qkv_act = _dn_conv_fwd(proj, dn_cw)
    o_b, states, dn_inverses = _gdn_fwd(a_log, dt_b, dn_norm_w, qkv_act, proj, ba)
    pa = _mm(o_a, w_psb, name="proj_sb")
    pb = _mm(o_b, w_pdn, name="proj_dn")
    gates = [(proj, d, OFF_GA // d), (proj, d, OFF_GB // d)]
    (merged,) = _stage_fwd(_f_merge, [], gates + [pa, pb], [bf16], "merge_fwd")
    ao = _mm(merged, w_o, name="out_proj")
    mid_params = [gate1, norm2_w, shift2, scale2]
    x1, h2 = _stage_fwd(_f_residual_normmod, mid_params, [xs, ao], [f32, bf16], "resid1_norm2_fwd")
    u_pre = _mm(h2, w_fin, name="ffn_in")
    act = _ffn_conv_fwd(u_pre, ffn_cw, ffn_conv_b)
    fo = _mm(act, w_fout, name="ffn_out")

    loss_p, d_gate2, d_wf, dx2, dfo = _loss_and_grads(gate2, final_norm_w.reshape(1, d), x1, fo, target)
    dact = _mm(dfo, w_fout, tb=True, name="ffn_out_dx")
    g_w_fout = _mm(act, dfo, ta=True, name="ffn_out_dw")
    du, dbg, dbu = _ffn_conv_bwd_act(u_pre, ffn_cw, ffn_conv_b, dact)
    du_pre, d_ffn_cw = _conv_bwd(du, u_pre, 0, ffn_cw, "ffn_conv_bwd")
    dh2 = _mm(du_pre, w_fin, tb=True, name="ffn_in_dx")
    g_w_fin = _mm(h2, du_pre, ta=True, name="ffn_in_dw")
    (d_gate1, d_n2w, d_shift2, d_scale2), (dx1, dao) = _stage_bwd(
        _f_residual_normmod, mid_params, [xs, ao], [dx2, dh2], [f32, bf16], "resid1_norm2_bwd")
    dmerged = _mm(dao, w_o, tb=True, name="out_proj_dx")
    g_w_o = _mm(merged, dao, ta=True, name="out_proj_dw")
    _, (dga, dgb, dpa, dpb) = _stage_bwd(_f_merge, [], gates + [pa, pb], [dmerged], [bf16] * 4, "merge_bwd")
    do_a = _mm(dpa, w_psb, tb=True, name="proj_sb_dx")
    g_w_psb = _mm(o_a, dpa, ta=True, name="proj_sb_dw")
    do_b = _mm(dpb, w_pdn, tb=True, name="proj_dn_dx")
    g_w_pdn = _mm(o_b, dpb, ta=True, name="proj_dn_dw")
    early = [_cols_by_device(g_w_psb).astype(bf16),
             g_w_pdn.reshape(N_DEV, DN_V_WIDTH // N_DEV, d).astype(bf16),
             g_w_o.reshape(N_DEV, d // N_DEV, d).astype(bf16),
             _cols_by_device(g_w_fin).astype(bf16),
             g_w_fout.reshape(N_DEV, D_FF // N_DEV, d).astype(bf16)]
    exchange_early = _SideComm(_exchange_protocol, early, [jax.ShapeDtypeStruct(a.shape, a.dtype) for a in early])
    dq, dk, dv, *recv_early = _sb_attention_bwd2(proj, k16, k0_16, k1_16, v16, sb_runs, do_a, side=exchange_early)
    d_alog, d_dtb, d_dnw, dqkv_act, dz, dba = _gdn_bwd(a_log, dt_b, dn_norm_w, qkv_act, proj, ba, states,
                                                       dn_inverses, do_b)
    d_conv_out = _dn_conv_bwd_act(proj, dn_cw, dqkv_act)
    d_dn_pre, d_dn_cw = _conv_bwd(d_conv_out, proj, OFF_DN // TCONV_C, dn_cw, "dn_conv_bwd")
    dproj = jnp.concatenate([dga, dgb, d_dn_pre, dz, dq, dk.astype(bf16), dv.astype(bf16)], axis=1)
    g_w_main = _mm(h, dproj, ta=True, name="in_proj_dw")
    g_w_ba = _mm(h, dba, ta=True, name="in_proj_ba_dw")
    g_w_in_full = jnp.concatenate([g_w_main[:, OFF_SBQ:], g_w_main[:, OFF_DN:OFF_Z], g_w_main[:, OFF_Z:OFF_SBQ],
                                   g_w_ba[:, :2 * DN_HEADS], g_w_main[:, :OFF_DN]], axis=1)
    w_in_parts = _cols_by_device(g_w_in_full).astype(bf16)
    exchange_w_in = _SideComm(_exchange_protocol, [w_in_parts], [jax.ShapeDtypeStruct(w_in_parts.shape, bf16)])
    dh, recv_w_in = _mm(dproj, w_main, tb=True, name="in_proj_dx", side=exchange_w_in)
    dh_ba = _mm(dba, w_ba, tb=True, name="in_proj_ba_dx")
    (d_n1w, d_shift1, d_scale1), (grad_x,) = _stage_bwd(
        _f_normmod, [norm1_w, shift1, scale1], [xs], [[dh, dh_ba]], [f32], "norm1_bwd", residual=(0, dx1))

    dmod = jnp.concatenate([d_shift1, d_scale1, d_gate1, d_shift2, d_scale2, d_gate2], axis=1)
    d_ffn_cb = jnp.concatenate([dbg, dbu], axis=1)
    small_parts = jnp.concatenate(
        [loss_p, dmod, d_n1w, d_alog, d_dtb, d_dnw, d_n2w, d_ffn_cb, d_wf,
         d_dn_cw.reshape(1, -1), d_ffn_cw.reshape(1, -1)], axis=1)
    (small_parts_g,) = _all_gather([small_parts], "gather_small_grads")
    tot = _sum_devices(small_parts_g[:, 0, :])
    offs = {}
    pos = 0
    for nm, width in (("loss", LANES), ("b_ada", 6 * d), ("norm1_w", d), ("dn_A_log", LANES), ("dn_dt_bias", LANES),
                      ("dn_norm_w", LANES), ("norm2_w", d), ("ffn_conv_b", 2 * D_FF), ("final_norm_w", d),
                      ("dn_conv_w", DN_CONV_WIDTH * DN_CONV_CH), ("ffn_conv_w", FFN_CONV_WIDTH * 2 * D_FF)):
        offs[nm] = (pos, width)
        pos += width
    seg = lambda nm: tot[:, offs[nm][0]:offs[nm][0] + offs[nm][1]]
    loss = tot[0, 0]
    g_b_ada = seg("b_ada")
    g_norm1 = seg("norm1_w")
    g_alog = seg("dn_A_log")[:, GDN_LOGIT_LANE:GDN_LOGIT_LANE + DN_HEADS]
    g_dtb = seg("dn_dt_bias")[:, GDN_LOGIT_LANE:GDN_LOGIT_LANE + DN_HEADS]
    g_dnw = seg("dn_norm_w")
    g_norm2 = seg("norm2_w")
    g_ffn_cb = seg("ffn_conv_b")
    g_fnw = seg("final_norm_w")
    g_dn_cw = lax.dynamic_slice(seg("dn_conv_w").reshape(DN_CONV_WIDTH, DN_CONV_CH), (0, me * n_dnc),
                                (DN_CONV_WIDTH, n_dnc))
    g_ffn_cw = lax.dynamic_slice(seg("ffn_conv_w").reshape(FFN_CONV_WIDTH, 2 * D_FF), (0, me * n_ffc),
                                 (FFN_CONV_WIDTH, n_ffc))

    dmod_all = small_parts_g[:, 0, offs["b_ada"][0]:offs["b_ada"][0] + 6 * d]
    g_w_ada = _ada_bwd(c_all, lax.dynamic_slice(dmod_all, (0, me * n_ada), (N_DEV, n_ada)))

    def pack(parts):
        flat = [p.reshape(1, -1) for p in parts]
        flat = [_pad_lanes(p, -(-p.shape[1] // LANES) * LANES) for p in flat]
        return jnp.concatenate(flat, axis=1), [p.shape[1] for p in flat]

    small_names_g = [g_b_ada, g_norm1, g_alog, g_dtb, g_dnw, g_norm2, g_ffn_cb, g_fnw, g_dn_cw, g_ffn_cw]
    small_w = [b_ada, norm1_w, dn_A_log, dn_dt_bias, dn_norm_w, norm2_w, ffn_conv_b, final_norm_w, dn_conv_w[0], ffn_conv_w[0]]
    small_m = [m_b_ada, m_norm1_w, m_dn_A_log, m_dn_dt_bias, m_dn_norm_w, m_norm2_w, m_ffn_conv_b, m_final_norm_w, m_dn_conv_w[0], m_ffn_conv_w[0]]
    small_v = [v_b_ada, v_norm1_w, v_dn_A_log, v_dn_dt_bias, v_dn_norm_w, v_norm2_w, v_ffn_conv_b, v_final_norm_w, v_dn_conv_w[0], v_ffn_conv_w[0]]
    pg, widths = pack(small_names_g)
    pw, _ = pack(small_w)
    pm, _ = pack(small_m)
    pv, _ = pack(small_v)
    s_delta, s_m, s_v = _adamw(pw, pg, pm, pv, "adamw_small")

    def unpack(flat):
        out, pos = [], 0
        for ref_arr, width in zip(small_w, widths):
            out.append(flat[:, pos:pos + ref_arr.size].reshape(ref_arr.shape))
            pos += width
        return out

    small_grads = [g.reshape(w_.shape) for g, w_ in zip(small_names_g, small_w)]
    small_delta, small_newm, small_newv = unpack(s_delta), unpack(s_m), unpack(s_v)

    ada_delta, ada_m, ada_v = _adamw(w_ada[0], g_w_ada, m_w_ada[0], v_w_ada[0], "adamw_ada")

    recv = [recv_w_in] + list(recv_early)
    big = {}
    for nm, parts, w_, m_, v_ in (("w_in", recv[0], w_in, m_w_in, v_w_in),
                                  ("w_proj_sb", recv[1], w_proj_sb, m_w_proj_sb, v_w_proj_sb),
                                  ("w_proj_dn", recv[2], w_proj_dn, m_w_proj_dn, v_w_proj_dn),
                                  ("w_out", recv[3], w_out, m_w_out, v_w_out),
                                  ("w_ffn_in", recv[4], w_ffn_in, m_w_ffn_in, v_w_ffn_in),
                                  ("w_ffn_out", recv[5], w_ffn_out, m_w_ffn_out, v_w_ffn_out)):
        big[nm] = [t[None] for t in _sum_adamw(parts, w_[0], m_[0], v_[0], "adamw_" + nm)]

    sg = dict(zip(["b_ada", "norm1_w", "dn_A_log", "dn_dt_bias", "dn_norm_w", "norm2_w", "ffn_conv_b", "final_norm_w",
                   "dn_conv_w", "ffn_conv_w"], range(10)))

    def small_out(table, nm):
        val = table[sg[nm]]
        return val[None] if nm in ("dn_conv_w", "ffn_conv_w") else val

    order = ["w_ada", "b_ada", "norm1_w", "w_in", "dn_conv_w", "dn_A_log", "dn_dt_bias", "dn_norm_w", "w_proj_sb",
             "w_proj_dn", "w_out", "norm2_w", "w_ffn_in", "ffn_conv_w", "ffn_conv_b", "w_ffn_out", "final_norm_w"]
    groups = []
    for k, small_table in enumerate((small_grads, small_delta, small_newm, small_newv)):
        row = []
        for nm in order:
            if nm == "w_ada":
                row.append((g_w_ada, ada_delta, ada_m, ada_v)[k][None])
            elif nm in big:
                row.append(big[nm][k])
            else:
                row.append(small_out(small_table, nm))
        groups.append(row)
    return (loss, grad_x[None], *groups[0], *groups[1], *groups[2], *groups[3])
```

```python
import functools

import jax
import jax.numpy as jnp
from jax import lax
from jax.experimental import pallas as pl
from jax.experimental.pallas import tpu as pltpu

f32 = jnp.float32
bf16 = jnp.bfloat16

D_MODEL = 1024
SB_HEADS = 8
SB_HEAD_DIM = 64
SB_WIDTH = SB_HEADS * SB_HEAD_DIM
SB_QBLOCK = 128
DN_HEADS = 8
DN_KEY_DIM = 64
DN_VAL_DIM = 128
DN_QK_WIDTH = DN_HEADS * DN_KEY_DIM
DN_V_WIDTH = DN_HEADS * DN_VAL_DIM
DN_CONV_CH = 2 * DN_QK_WIDTH + DN_V_WIDTH
DN_CONV_WIDTH = 4
DN_CHUNK = 64
D_FF = 2816
FFN_CONV_WIDTH = 3
NORM_EPS = 1e-6
L2_EPS = 1e-6
ADAM_LR = 0.001
ADAM_B1 = 0.9
ADAM_B2 = 0.999
ADAM_EPS = 1e-08
ADAM_WD = 0.01
ADAM_STEP = 10

N_DEV = 8
MESH = pl.DeviceIdType.MESH

LANES = 128
SUBLANES = 8
VMEM_LIMIT = 48 * 1024 * 1024

OFF_GA = 0
OFF_GB = D_MODEL
OFF_DN = 2 * D_MODEL
OFF_Z = OFF_DN + DN_CONV_CH
OFF_SBQ = OFF_Z + DN_V_WIDTH
OFF_SBK = OFF_SBQ + SB_WIDTH
OFF_SBV = OFF_SBK + SB_WIDTH
MAIN_WIDTH = OFF_SBV + SB_WIDTH

TM = 256
TCONV_R = 512
TCONV_C = 512
TCONV_FF = D_FF // 2
SB_PAIRS_FWD = 4
SB_PAIRS_BWD = 4
SB_DEAD = -106.0
SB_NEVER = -1e30


def _cparams(sem=None):
    return pltpu.CompilerParams(dimension_semantics=sem, vmem_limit_bytes=VMEM_LIMIT)


def _pick(n, cands):
    for c in cands:
        if n % c == 0:
            return c
    return n


def _my_pos():
    return lax.axis_index("x"), lax.axis_index("y"), lax.axis_index("c")


def _flip(v, bit):
    return 1 - v if bit else v


def _comm_scratch(n):
    return [pltpu.SemaphoreType.DMA((n, 7)), pltpu.SemaphoreType.DMA((n, 7)), pltpu.SemaphoreType.DMA((n,))]


def _gather_protocol(ins, outs, send_sems, recv_sems, local_sems):
    n = len(ins)
    x, y, c = _my_pos()
    me, sibling = (x, y, c), (x, y, 1 - c)
    chips = [(1 - x, y), (x, 1 - y), (1 - x, 1 - y)]

    def slot(out, pos):
        return out.at[4 * pos[0] + 2 * pos[1] + pos[2]]

    def copy(a, k, block, to, src=None):
        return pltpu.make_async_remote_copy(
            src_ref=slot(outs[a], block) if src is None else src, dst_ref=slot(outs[a], block),
            send_sem=send_sems.at[a, k], recv_sem=recv_sems.at[a, k], device_id=to, device_id_type=MESH)

    def local(a):
        return pltpu.make_async_copy(ins[a], slot(outs[a], me), local_sems.at[a])

    def first(a):
        return [copy(a, 0, me, sibling, src=ins[a])] + [copy(a, 1 + j, me, (*chip, c), src=ins[a])
                                                         for j, chip in enumerate(chips)]

    def start():
        for a in range(n):
            local(a).start()
            for cp in first(a):
                cp.start()

    def finish():
        forwards = []
        for a in range(n):
            for j, chip in enumerate(chips):
                copy(a, 1 + j, (*chip, c), me).wait_recv()
                fwd = copy(a, 4 + j, (*chip, c), sibling)
                fwd.start()
                forwards.append(fwd)
        for a in range(n):
            copy(a, 0, sibling, me).wait_recv()
            for j, chip in enumerate(chips):
                copy(a, 4 + j, (*chip, 1 - c), me).wait_recv()
        for a in range(n):
            for cp in first(a):
                cp.wait_send()
        for cp in forwards:
            cp.wait_send()
        for a in range(n):
            local(a).wait()

    return start, finish


def _exchange_protocol(ins, outs, send_sems, recv_sems, local_sems):
    n = len(ins)
    x, y, c = _my_pos()
    me_idx = 4 * x + 2 * y + c

    def local(a):
        return pltpu.make_async_copy(ins[a].at[me_idx], outs[a].at[me_idx], local_sems.at[a])

    def copies(a, m):
        peer = (_flip(x, m & 4), _flip(y, m & 2), _flip(c, m & 1))
        peer_idx = 4 * peer[0] + 2 * peer[1] + peer[2]
        sems = dict(send_sem=send_sems.at[a, m - 1], recv_sem=recv_sems.at[a, m - 1], device_id=peer,
                    device_id_type=MESH)
        send = pltpu.make_async_remote_copy(src_ref=ins[a].at[peer_idx], dst_ref=outs[a].at[me_idx], **sems)
        recv = pltpu.make_async_remote_copy(src_ref=ins[a].at[peer_idx], dst_ref=outs[a].at[peer_idx], **sems)
        return send, recv

    def start():
        for a in range(n):
            local(a).start()
            for m in range(1, N_DEV):
                copies(a, m)[0].start()

    def finish():
        for a in range(n):
            for m in range(1, N_DEV):
                copies(a, m)[1].wait_recv()
        for a in range(n):
            for m in range(1, N_DEV):
                copies(a, m)[0].wait_send()
            local(a).wait()

    return start, finish


def _collective_call(protocol, arrs, out_shapes, name):
    n = len(arrs)

    def body(*refs):
        start, finish = protocol(refs[:n], refs[n:2 * n], *refs[2 * n:])
        start()
        finish()

    any_spec = pl.BlockSpec(memory_space=pl.ANY)
    return pl.pallas_call(body, name=name, out_shape=out_shapes, in_specs=[any_spec] * n, out_specs=[any_spec] * n,
                          scratch_shapes=_comm_scratch(n))(*arrs)


def _gathered_shapes(arrs):
    return [jax.ShapeDtypeStruct((N_DEV,) + a.shape, a.dtype) for a in arrs]


def _all_gather(arrs, name):
    return _collective_call(_gather_protocol, arrs, _gathered_shapes(arrs), name)


MM_BLOCK_BYTES = 4 * 1024 * 1024


def _mm_tiles(m_dim, n_dim, k_dim, a_bytes, b_bytes):
    tm = _pick(m_dim, (1024, 512, 256, 128))
    tn = _pick(n_dim, (1024, 512, 256, 128))
    tk = k_dim
    if k_dim % LANES == 0:
        units = k_dim // LANES
        fits = [u for u in range(1, units + 1) if units % u == 0
                and u * LANES * max(tm * a_bytes, tn * b_bytes) <= MM_BLOCK_BYTES]
        tk = max(fits) * LANES
    return tm, tn, tk


def _mm(a, b, *, ta=False, tb=False, name, side=None):
    (k_dim, m_dim) = a.shape if ta else a.shape[::-1]
    (n_dim, kb_dim) = b.shape if tb else b.shape[::-1]
    assert k_dim == kb_dim, (a.shape, b.shape, ta, tb)
    tm, tn, tk = _mm_tiles(m_dim, n_dim, k_dim, a.dtype.itemsize, b.dtype.itemsize)
    nk = k_dim // tk
    grid = (m_dim // tm, n_dim // tn, nk)
    dims = (((0 if ta else 1,), (1 if tb else 0,)), ((), ()))
    ns = side.n if side else 0

    def body(a_ref, b_ref, *rest):
        if side:
            side.run(rest[:ns], rest[ns + 1:2 * ns + 1], rest[2 * ns + 1:], *_grid_ends(grid),
                     lambda: compute(a_ref, b_ref, rest[ns]))
        else:
            compute(a_ref, b_ref, rest[0])

    def compute(a_ref, b_ref, o_ref):
        part = lax.dot_general(a_ref[...].astype(bf16), b_ref[...].astype(bf16), dims, preferred_element_type=f32)
        if nk == 1:
            o_ref[...] = part
        else:
            k = pl.program_id(2)

            @pl.when(k == 0)
            def _():
                o_ref[...] = part

            @pl.when(k > 0)
            def _():
                o_ref[...] += part

    a_spec = pl.BlockSpec((tk, tm), lambda i, j, k: (k, i)) if ta else pl.BlockSpec((tm, tk), lambda i, j, k: (i, k))
    b_spec = pl.BlockSpec((tn, tk), lambda i, j, k: (j, k)) if tb else pl.BlockSpec((tk, tn), lambda i, j, k: (k, j))
    out_spec = pl.BlockSpec((tm, tn), lambda i, j, k: (i, j))
    out_shape = jax.ShapeDtypeStruct((m_dim, n_dim), f32)
    if not side:
        return pl.pallas_call(body, name=name, grid=grid, in_specs=[a_spec, b_spec], out_specs=out_spec,
                              out_shape=out_shape,
                              compiler_params=_cparams(("parallel", "parallel", "arbitrary")))(a, b)
    return pl.pallas_call(
        body, name=name, grid=grid, in_specs=[a_spec, b_spec] + side.specs(), out_specs=[out_spec] + side.specs(),
        out_shape=[out_shape] + side.out_shapes, scratch_shapes=_comm_scratch(ns),
        compiler_params=_cparams(("arbitrary", "arbitrary", "arbitrary")))(a, b, *side.arrs)


def _win(t):
    return t if isinstance(t, tuple) else (t, t.shape[1], 0)


def _tile_spec(width, cb, tm):
    return pl.BlockSpec((tm, width), lambda i: (i, cb))


def _param_spec(p):
    return pl.BlockSpec(p.shape, lambda i: (0, 0))


def _stage_fwd(f, params, tiles, out_dtypes, name):
    tiles = [_win(t) for t in tiles]
    rows = tiles[0][0].shape[0]
    tm = min(TM, rows)
    avals = jax.eval_shape(f, *[jax.ShapeDtypeStruct(p.shape, f32) for p in params],
                           *[jax.ShapeDtypeStruct((tm, w), f32) for _, w, _ in tiles])
    n_p, n_t = len(params), len(tiles)

    def body(*refs):
        p = [r[...] for r in refs[:n_p]]
        t = [r[...].astype(f32) for r in refs[n_p:n_p + n_t]]
        for o_ref, val in zip(refs[n_p + n_t:], f(*p, *t)):
            o_ref[...] = val.astype(o_ref.dtype)

    return pl.pallas_call(
        body, name=name, grid=(rows // tm,),
        in_specs=[_param_spec(p) for p in params] + [_tile_spec(w, cb, tm) for _, w, cb in tiles],
        out_specs=[_tile_spec(a.shape[1], 0, tm) for a in avals],
        out_shape=[jax.ShapeDtypeStruct((rows, a.shape[1]), dt) for a, dt in zip(avals, out_dtypes)],
        compiler_params=_cparams(("parallel",)),
    )(*params, *[t[0] for t in tiles])


def _stage_bwd(f, params, tiles, cts, grad_dtypes, name, residual=None):
    tiles = [_win(t) for t in tiles]
    rows = tiles[0][0].shape[0]
    tm = min(TM, rows)
    cts = [list(g) if isinstance(g, (list, tuple)) else [g] for g in cts]
    flat_cts = [a for g in cts for a in g]
    n_p, n_t, n_c = len(params), len(tiles), len(flat_cts)
    has_res = residual is not None
    want = [j for j, dt in enumerate(grad_dtypes) if dt is not None]

    def body(*refs):
        i = pl.program_id(0)
        p = [r[...] for r in refs[:n_p]]
        t = [r[...].astype(f32) for r in refs[n_p:n_p + n_t]]
        ct_vals = [r[...].astype(f32) for r in refs[n_p + n_t:n_p + n_t + n_c]]
        ct, at = [], 0
        for g in cts:
            ct.append(functools.reduce(jnp.add, ct_vals[at:at + len(g)]))
            at += len(g)
        ct = tuple(ct)
        pos = n_p + n_t + n_c
        res_ref = refs[pos] if has_res else None
        pos += 1 if has_res else 0
        dp_refs = refs[pos:pos + n_p]
        dt_refs = refs[pos + n_p:]
        _, vjp = jax.vjp(f, *p, *t)
        grads = vjp(ct)

        @pl.when(i == 0)
        def _():
            for r in dp_refs:
                r[...] = jnp.zeros_like(r)

        for r, g in zip(dp_refs, grads[:n_p]):
            r[...] += g
        for r, j in zip(dt_refs, want):
            g = grads[n_p + j]
            if has_res and j == residual[0]:
                g = g + res_ref[...].astype(f32)
            r[...] = g.astype(r.dtype)

    in_arrays = list(params) + [t[0] for t in tiles] + flat_cts
    in_specs = ([_param_spec(p) for p in params] + [_tile_spec(w, cb, tm) for _, w, cb in tiles]
                + [_tile_spec(c.shape[1], 0, tm) for c in flat_cts])
    if has_res:
        in_arrays.append(residual[1])
        in_specs.append(_tile_spec(residual[1].shape[1], 0, tm))
    out_shape = ([jax.ShapeDtypeStruct(p.shape, f32) for p in params]
                 + [jax.ShapeDtypeStruct((rows, tiles[j][1]), grad_dtypes[j]) for j in want])
    out_specs = [_param_spec(p) for p in params] + [_tile_spec(tiles[j][1], 0, tm) for j in want]
    outs = pl.pallas_call(
        body, name=name, grid=(rows // tm,), in_specs=in_specs, out_specs=out_specs, out_shape=out_shape,
        compiler_params=_cparams(("arbitrary",)),
    )(*in_arrays)
    return outs[:n_p], outs[n_p:]


def _rms(x, w):
    return x * lax.rsqrt(jnp.mean(x * x, axis=-1, keepdims=True) + NORM_EPS) * w


def _f_normmod(w, shift, scale, x):
    return (_rms(x, w) * (1.0 + scale) + shift,)


def _f_merge(ga, gb, pa, pb):
    return (jax.nn.sigmoid(ga) * pa + jax.nn.sigmoid(gb) * pb,)


def _f_residual_normmod(gate, w, shift, scale, x, branch):
    x1 = x + gate * branch
    return x1, _rms(x1, w) * (1.0 + scale) + shift


def _f_loss(gate, wf, x1, fo, target):
    y = _rms(x1 + gate * fo, wf)
    err = jnp.square(y - target)
    return (0.5 * jnp.sum(jnp.mean(err, axis=-1, keepdims=True), axis=0, keepdims=True),)


def _loss_and_grads(gate2, wf, x1, fo, target):
    rows, d = x1.shape
    tm = min(TM, rows)

    def body(g_ref, w_ref, x_ref, fo_ref, t_ref, loss_ref, dg_ref, dw_ref, dx_ref, dfo_ref):
        i = pl.program_id(0)
        (val,), vjp = jax.vjp(_f_loss, g_ref[...], w_ref[...], x_ref[...], fo_ref[...], t_ref[...])
        dg, dw, dx, dfo, _ = vjp((jnp.ones((1, 1), f32),))

        @pl.when(i == 0)
        def _():
            loss_ref[...] = jnp.zeros_like(loss_ref)
            dg_ref[...] = jnp.zeros_like(dg_ref)
            dw_ref[...] = jnp.zeros_like(dw_ref)

        loss_ref[...] += jnp.broadcast_to(val, loss_ref.shape)
        dg_ref[...] += dg
        dw_ref[...] += dw
        dx_ref[...] = dx
        dfo_ref[...] = dfo.astype(bf16)

    vec = pl.BlockSpec((1, d), lambda i: (0, 0))
    tile = pl.BlockSpec((tm, d), lambda i: (i, 0))
    return pl.pallas_call(
        body, name="loss_fwd_bwd", grid=(rows // tm,),
        in_specs=[vec, vec, tile, tile, tile],
        out_specs=[pl.BlockSpec((1, LANES), lambda i: (0, 0)), vec, vec, tile, tile],
        out_shape=[jax.ShapeDtypeStruct((1, LANES), f32), jax.ShapeDtypeStruct((1, d), f32),
                   jax.ShapeDtypeStruct((1, d), f32), jax.ShapeDtypeStruct((rows, d), f32),
                   jax.ShapeDtypeStruct((rows, d), bf16)],
        compiler_params=_cparams(("arbitrary",)),
    )(gate2, wf, x1, fo, target)


def _softplus(z):
    return jnp.maximum(z, 0.0) + jnp.log(1.0 + jnp.exp(-jnp.abs(z)))


def _split_dot(a, m):
    hi = a.astype(bf16)
    lo = (a - hi.astype(f32)).astype(bf16)
    return jnp.dot(hi, m, preferred_element_type=f32) + jnp.dot(lo, m, preferred_element_type=f32)


def _suffix_matrix(n):
    r = lax.broadcasted_iota(jnp.int32, (n, n), 0)
    c = lax.broadcasted_iota(jnp.int32, (n, n), 1)
    return (r > c).astype(bf16)


def _head_masks():
    lane = lax.broadcasted_iota(jnp.int32, (1, LANES), 1)
    return [(lane < SB_HEAD_DIM).astype(f32), (lane >= SB_HEAD_DIM).astype(f32)]


def _sb_prepare(proj):
    def f(k, v):
        lane = lax.broadcasted_iota(jnp.int32, (1, SB_WIDTH), 1)
        m0 = (jnp.bitwise_and(lane, LANES - 1) < SB_HEAD_DIM).astype(f32)
        m1 = 1.0 - m0
        return k, k * m0, k * m1, v, v * m0, v * m1

    wins = [(proj, SB_WIDTH, OFF_SBK // SB_WIDTH), (proj, SB_WIDTH, OFF_SBV // SB_WIDTH)]
    return _stage_fwd(f, [], wins, [bf16] * 6, "sb_prepare")


def _stack_heads(x):
    m0, m1 = _head_masks()
    return jnp.concatenate([x * m0, x * m1], axis=0)


def _sb_logits(qst, k, t_pos2, kb, bq, masked):
    z = lax.dot_general(qst, k, (((1,), (1,)), ((), ())), preferred_element_type=f32)
    l = -_softplus(z)
    if masked:
        s_pos = kb * bq + lax.broadcasted_iota(jnp.int32, (1, bq), 1)
        causal = s_pos < t_pos2
        l = jnp.where(causal, l, 0.0)
    else:
        causal = None
    return z, l, causal


class _SideComm:
    def __init__(self, protocol, arrs, out_shapes):
        self.protocol, self.arrs, self.out_shapes = protocol, list(arrs), list(out_shapes)
        self.n = len(self.arrs)

    def specs(self):
        return [pl.BlockSpec(memory_space=pl.ANY)] * self.n

    def run(self, in_refs, out_refs, sems, first, last, compute):
        start, finish = self.protocol(in_refs, out_refs, *sems)
        pl.when(first)(start)
        compute()
        pl.when(last)(finish)


def _grid_ends(grid):
    ids = [pl.program_id(axis) for axis in range(len(grid))]
    first = functools.reduce(jnp.logical_and, [i == 0 for i in ids])
    last = functools.reduce(jnp.logical_and, [i == g - 1 for i, g in zip(ids, grid)])
    return first, last


def _sb_attention_fwd2(proj, k16, v0_16, v1_16, side=None):
    rows = proj.shape[0]
    bq = SB_QBLOCK
    nq = rows // bq
    assert nq <= LANES, "one lane per key block"
    npair = SB_WIDTH // LANES
    scale = SB_HEAD_DIM ** -0.5

    npp = SB_PAIRS_FWD
    wq = npp * LANES
    grid = (npair // npp, nq)
    ns = side.n if side else 0

    def body(q_ref, k_ref, v0_ref, v1_ref, *rest):
        o_ref, runs_ref = rest[ns], rest[ns + 1]
        if side:
            side.run(rest[:ns], rest[ns + 2:2 * ns + 2], rest[2 * ns + 2:], *_grid_ends(grid),
                     lambda: compute(q_ref, k_ref, v0_ref, v1_ref, o_ref, runs_ref))
        else:
            compute(q_ref, k_ref, v0_ref, v1_ref, o_ref, runs_ref)

    def compute(q_ref, k_ref, v0_ref, v1_ref, o_ref, runs_ref):
        qi = pl.program_id(1)
        pairs = [slice(pp * LANES, (pp + 1) * LANES) for pp in range(npp)]
        qst = [(_stack_heads(q_ref[:, s]) * scale).astype(bf16) for s in pairs]
        r = lax.broadcasted_iota(jnp.int32, (bq, 2 * bq), 0)
        c = lax.broadcasted_iota(jnp.int32, (bq, 2 * bq), 1)
        m2 = jnp.logical_or(r > c, c >= bq).astype(bf16)
        t_pos = qi * bq + lax.broadcasted_iota(jnp.int32, (bq, 1), 0)
        t_pos2 = jnp.concatenate([t_pos, t_pos], axis=0)
        lane = lax.broadcasted_iota(jnp.int32, (1, LANES), 1)
        runs_ref[...] = jnp.full(runs_ref.shape, SB_NEVER, f32)

        def tiles(kbs, carry, masked):
            jobs = [(pp, kb) for kb in kbs for pp in range(npp)]
            rows_k = [pl.ds(pl.multiple_of(kb * bq, bq), bq) for _, kb in jobs]
            zl = [_sb_logits(qst[pp], k_ref[rk, pairs[pp]], t_pos2, kb, bq, masked) for (pp, kb), rk in zip(jobs, rows_k)]
            cs = [_split_dot(l, m2) for _, l, _ in zl]
            run = [cr[0] for cr in carry]
            acc = [cr[1] for cr in carry]
            probs = []
            for (pp, kb), (z, l, causal), cs2 in zip(jobs, zl, cs):
                a = jnp.exp(z + l + cs2[:, :bq] + run[pp])
                if masked:
                    a = jnp.where(causal, a, 0.0)
                probs.append(a.astype(bf16))
                for hh in range(2):
                    cols = slice((2 * pp + hh) * LANES, (2 * pp + hh + 1) * LANES)
                    runs_ref[:, cols] = jnp.where(lane == kb, run[pp][hh * bq:(hh + 1) * bq], runs_ref[:, cols])
                run[pp] = run[pp] + cs2[:, bq:]
            for (pp, kb), rk, ab in zip(jobs, rows_k, probs):
                acc[pp] = (acc[pp] + jnp.dot(ab[:bq], v0_ref[rk, pairs[pp]], preferred_element_type=f32)
                           + jnp.dot(ab[bq:], v1_ref[rk, pairs[pp]], preferred_element_type=f32))
            return tuple(zip(run, acc))

        zero = (jnp.zeros((2 * bq, bq), f32), jnp.zeros((bq, LANES), f32))
        carry = tiles([qi], (zero,) * npp, True)

        def alive(cr):
            return functools.reduce(jnp.maximum, [jnp.max(run) for run, _ in cr]) > SB_DEAD

        def two(state):
            i, _, cr = state
            cr = tiles([qi - 1 - 2 * i, qi - 2 - 2 * i], cr, False)
            return i + 1, alive(cr), cr

        n_two = qi // 2
        i_end, still, carry = lax.while_loop(lambda st: jnp.logical_and(st[0] < n_two, st[1]), two,
                                             (jnp.int32(0), alive(carry), carry))
        last_one = jnp.logical_and(qi % 2 == 1, jnp.logical_and(still, i_end == n_two))
        carry = lax.cond(last_one, lambda cr: tiles([0], cr, False), lambda cr: cr, carry)
        for pp in range(npp):
            o_ref[:, pairs[pp]] = carry[pp][1]

    kv = pl.BlockSpec((rows, wq), lambda p, i: (0, p))
    return pl.pallas_call(
        body, name="sb_attn_fwd", grid=grid,
        in_specs=[pl.BlockSpec((bq, wq), lambda p, i: (i, OFF_SBQ // wq + p)), kv, kv, kv] + (side.specs() if side else []),
        out_specs=[pl.BlockSpec((bq, wq), lambda p, i: (i, p)),
                   pl.BlockSpec((bq, 2 * wq), lambda p, i: (i, p))] + (side.specs() if side else []),
        out_shape=[jax.ShapeDtypeStruct((rows, SB_WIDTH), f32),
                   jax.ShapeDtypeStruct((rows, SB_HEADS * LANES), f32)] + (side.out_shapes if side else []),
        scratch_shapes=_comm_scratch(ns) if side else [],
        compiler_params=_cparams(("arbitrary", "arbitrary")),
    )(proj, k16, v0_16, v1_16, *(side.arrs if side else []))


def _sb_attention_bwd2(proj, k16, k0_16, k1_16, v16, runs, do, side=None):
    rows = proj.shape[0]
    bq = SB_QBLOCK
    nq = rows // bq
    npair = SB_WIDTH // LANES
    scale = SB_HEAD_DIM ** -0.5
    tn = (((0,), (0,)), ((), ()))
    nt = (((1,), (1,)), ((), ()))

    npp = SB_PAIRS_BWD
    wq = npp * LANES
    grid = (npair // npp, nq)
    ns = side.n if side else 0

    def body(q_ref, k_ref, k0_ref, k1_ref, v_ref, runs_ref, do_ref, *rest):
        outs = rest[ns:ns + 3]
        ins = (q_ref, k_ref, k0_ref, k1_ref, v_ref, runs_ref, do_ref)
        if side:
            side.run(rest[:ns], rest[ns + 3:2 * ns + 3], rest[2 * ns + 3:], *_grid_ends(grid),
                     lambda: compute(*ins, *outs))
        else:
            compute(*ins, *outs)

    def compute(q_ref, k_ref, k0_ref, k1_ref, v_ref, runs_ref, do_ref, dq_ref, dk_ref, dv_ref):
        qi = pl.program_id(1)

        @pl.when(qi == 0)
        def _():
            dk_ref[...] = jnp.zeros_like(dk_ref)
            dv_ref[...] = jnp.zeros_like(dv_ref)

        pairs = [slice(pp * LANES, (pp + 1) * LANES) for pp in range(npp)]
        qst = [(_stack_heads(q_ref[:, s]) * scale).astype(bf16) for s in pairs]
        dost = [_stack_heads(do_ref[:, s]).astype(bf16) for s in pairs]
        runs = [jnp.concatenate([runs_ref[:, 2 * pp * LANES:(2 * pp + 1) * LANES],
                                 runs_ref[:, (2 * pp + 1) * LANES:(2 * pp + 2) * LANES]], axis=0) for pp in range(npp)]
        r = lax.broadcasted_iota(jnp.int32, (bq, 2 * bq), 0)
        c = lax.broadcasted_iota(jnp.int32, (bq, 2 * bq), 1)
        suffix_m = _suffix_matrix(bq)
        m2 = jnp.logical_or(r < c, c >= bq).astype(bf16)
        t_pos = qi * bq + lax.broadcasted_iota(jnp.int32, (bq, 1), 0)
        t_pos2 = jnp.concatenate([t_pos, t_pos], axis=0)
        lane = lax.broadcasted_iota(jnp.int32, (1, LANES), 1)

        def tiles(kbs, carry, masked):
            jobs = [(pp, kb) for kb in kbs for pp in range(npp)]
            rows_k = [pl.ds(pl.multiple_of(kb * bq, bq), bq) for _, kb in jobs]
            zl = [_sb_logits(qst[pp], k_ref[rk, pairs[pp]], t_pos2, kb, bq, masked) for (pp, kb), rk in zip(jobs, rows_k)]
            das = [lax.dot_general(dost[pp], v_ref[rk, pairs[pp]], nt, preferred_element_type=f32)
                   for (pp, kb), rk in zip(jobs, rows_k)]
            sticks = [_split_dot(l, suffix_m) for _, l, _ in zl]
            probs, ps = [], []
            for (pp, kb), (z, l, causal), stick, da in zip(jobs, zl, sticks, das):
                run = jnp.sum(jnp.where(lane == kb, runs[pp], 0.0), axis=1, keepdims=True)
                a = jnp.exp(z + l + stick + run)
                if masked:
                    a = jnp.where(causal, a, 0.0)
                probs.append(a.astype(bf16))
                ps.append(da * a)
            pcs = [_split_dot(p, m2) for p in ps]
            pref = [cr[0] for cr in carry]
            dq_acc = [cr[1] for cr in carry]
            dzs = []
            for (pp, kb), (z, l, causal), p, pc2 in zip(jobs, zl, ps, pcs):
                dz = p * jnp.exp(l) - jnp.exp(z + l) * (pc2[:, :bq] + pref[pp])
                if masked:
                    dz = jnp.where(causal, dz, 0.0)
                dzs.append(dz.astype(bf16))
                pref[pp] = pref[pp] + pc2[:, bq:]
            for (pp, kb), rk, dzb, ab in zip(jobs, rows_k, dzs, probs):
                cols = pairs[pp]
                dq_acc[pp] = (dq_acc[pp] + jnp.dot(dzb[:bq], k0_ref[rk, cols], preferred_element_type=f32)
                              + jnp.dot(dzb[bq:], k1_ref[rk, cols], preferred_element_type=f32))
                dk_ref[rk, cols] += lax.dot_general(dzb, qst[pp], tn, preferred_element_type=f32)
                dv_ref[rk, cols] += lax.dot_general(ab, dost[pp], tn, preferred_element_type=f32)
            return tuple(zip(pref, dq_acc))

        zero = (jnp.zeros((2 * bq, bq), f32), jnp.zeros((bq, LANES), f32))
        colmax = functools.reduce(jnp.maximum, [jnp.max(x, axis=0, keepdims=True) for x in runs])
        live = jnp.logical_and(colmax > SB_DEAD, lane < qi)
        kb0 = jnp.minimum(jnp.min(jnp.where(live, lane, LANES)), qi)
        n_blocks = qi - kb0
        carry = lax.fori_loop(0, n_blocks // 2, lambda i, cr: tiles([kb0 + 2 * i, kb0 + 2 * i + 1], cr, False),
                              (zero,) * npp)
        carry = lax.cond(n_blocks % 2 == 1, lambda cr: tiles([qi - 1], cr, False), lambda cr: cr, carry)
        carry = tiles([qi], carry, True)
        for pp in range(npp):
            dq_ref[:, pairs[pp]] = (carry[pp][1] * scale).astype(dq_ref.dtype)

    blk = pl.BlockSpec((bq, wq), lambda p, i: (i, p))
    full = pl.BlockSpec((rows, wq), lambda p, i: (0, p), pipeline_mode=pl.Buffered(1))
    return pl.pallas_call(
        body, name="sb_attn_bwd", grid=grid,
        in_specs=[pl.BlockSpec((bq, wq), lambda p, i: (i, OFF_SBQ // wq + p)), full, full, full, full,
                  pl.BlockSpec((bq, 2 * wq), lambda p, i: (i, p)), blk] + (side.specs() if side else []),
        out_specs=[blk, full, full] + (side.specs() if side else []),
        out_shape=[jax.ShapeDtypeStruct((rows, SB_WIDTH), bf16), jax.ShapeDtypeStruct((rows, SB_WIDTH), f32),
                   jax.ShapeDtypeStruct((rows, SB_WIDTH), f32)] + (side.out_shapes if side else []),
        scratch_shapes=_comm_scratch(ns) if side else [],
        compiler_params=_cparams(("arbitrary", "arbitrary")),
    )(proj, k16, k0_16, k1_16, v16, runs, do, *(side.arrs if side else []))


def _shift_down(x, prev8, j):
    if j == 0:
        return x
    r = pltpu.roll(x, j, axis=0)
    row8 = lax.broadcasted_iota(jnp.int32, prev8.shape, 0)
    head = jnp.where(row8 < j, pltpu.roll(prev8, j, axis=0), r[0:SUBLANES])
    return jnp.concatenate([head, r[SUBLANES:]], axis=0)


def _shift_up(x, next8, j):
    if j == 0:
        return x
    n = x.shape[0]
    r = pltpu.roll(x, n - j, axis=0)
    row8 = lax.broadcasted_iota(jnp.int32, next8.shape, 0)
    tail = jnp.where(row8 >= SUBLANES - j, pltpu.roll(next8, SUBLANES - j, axis=0), r[n - SUBLANES:n])
    return jnp.concatenate([r[:n - SUBLANES], tail], axis=0)


def _conv(x, prev8, w):
    k_taps = w.shape[0]
    out = x * w[k_taps - 1:k_taps, :]
    for j in range(1, k_taps):
        out = out + _shift_down(x, prev8, j) * w[k_taps - 1 - j:k_taps - j, :]
    return out


def _conv_tiles(rows, tr_max=TCONV_R):
    tr = min(tr_max, rows)
    return tr, rows // tr, tr // SUBLANES


def _prev_spec(tc, cb0, r8):
    return pl.BlockSpec((SUBLANES, tc), lambda j, i: (jnp.maximum(i * r8 - 1, 0), cb0 + j))


def _silu(x):
    return x * jax.nn.sigmoid(x)


def _dsilu(x):
    s = jax.nn.sigmoid(x)
    return s * (1.0 + x * (1.0 - s))


def _dn_conv_fwd(proj, w):
    rows = proj.shape[0]
    tr, nr, r8 = _conv_tiles(rows)
    tc = TCONV_C
    cb0 = OFF_DN // tc

    def body(x_ref, p_ref, w_ref, o_ref):
        prev = jnp.where(pl.program_id(1) == 0, 0.0, p_ref[...])
        o_ref[...] = _silu(_conv(x_ref[...], prev, w_ref[...]))

    return pl.pallas_call(
        body, name="dn_conv_fwd", grid=(DN_CONV_CH // tc, nr),
        in_specs=[pl.BlockSpec((tr, tc), lambda j, i: (i, cb0 + j)), _prev_spec(tc, cb0, r8),
                  pl.BlockSpec((DN_CONV_WIDTH, tc), lambda j, i: (0, j))],
        out_specs=pl.BlockSpec((tr, tc), lambda j, i: (i, j)),
        out_shape=jax.ShapeDtypeStruct((rows, DN_CONV_CH), f32),
        compiler_params=_cparams(("parallel", "parallel")),
    )(proj, proj, w)


def _dn_conv_bwd_act(proj, w, dact):
    rows = proj.shape[0]
    tr, nr, r8 = _conv_tiles(rows)
    tc = TCONV_C
    cb0 = OFF_DN // tc

    def body(x_ref, p_ref, w_ref, d_ref, o_ref):
        prev = jnp.where(pl.program_id(1) == 0, 0.0, p_ref[...])
        o_ref[...] = d_ref[...] * _dsilu(_conv(x_ref[...], prev, w_ref[...]))

    return pl.pallas_call(
        body, name="dn_conv_bwd_act", grid=(DN_CONV_CH // tc, nr),
        in_specs=[pl.BlockSpec((tr, tc), lambda j, i: (i, cb0 + j)), _prev_spec(tc, cb0, r8),
                  pl.BlockSpec((DN_CONV_WIDTH, tc), lambda j, i: (0, j)),
                  pl.BlockSpec((tr, tc), lambda j, i: (i, j))],
        out_specs=pl.BlockSpec((tr, tc), lambda j, i: (i, j)),
        out_shape=jax.ShapeDtypeStruct((rows, DN_CONV_CH), f32),
        compiler_params=_cparams(("parallel", "parallel")),
    )(proj, proj, w, dact)


def _ffn_conv_fwd(u_pre, w, b):
    rows = u_pre.shape[0]
    tr, nr, r8 = _conv_tiles(rows, TCONV_R // 2)
    tc = TCONV_FF
    nct = D_FF // tc

    def body(xg_ref, pg_ref, xu_ref, pu_ref, wg_ref, wu_ref, bg_ref, bu_ref, o_ref):
        first = pl.program_id(1) == 0
        ug = _conv(xg_ref[...], jnp.where(first, 0.0, pg_ref[...]), wg_ref[...]) + bg_ref[...]
        uu = _conv(xu_ref[...], jnp.where(first, 0.0, pu_ref[...]), wu_ref[...]) + bu_ref[...]
        o_ref[...] = (_silu(ug) * uu).astype(o_ref.dtype)

    def x_spec(off):
        return pl.BlockSpec((tr, tc), lambda j, i: (i, off + j))

    def w_spec(k, off):
        return pl.BlockSpec((k, tc), lambda j, i: (0, off + j))

    return pl.pallas_call(
        body, name="ffn_conv_fwd", grid=(nct, nr),
        in_specs=[x_spec(0), _prev_spec(tc, 0, r8), x_spec(nct), _prev_spec(tc, nct, r8),
                  w_spec(FFN_CONV_WIDTH, 0), w_spec(FFN_CONV_WIDTH, nct), w_spec(1, 0), w_spec(1, nct)],
        out_specs=pl.BlockSpec((tr, tc), lambda j, i: (i, j)),
        out_shape=jax.ShapeDtypeStruct((rows, D_FF), bf16),
        compiler_params=_cparams(("parallel", "parallel")),
    )(u_pre, u_pre, u_pre, u_pre, w, w, b, b)


def _ffn_conv_bwd_act(u_pre, w, b, dact):
    rows = u_pre.shape[0]
    tr, nr, r8 = _conv_tiles(rows, TCONV_R // 2)
    tc = TCONV_FF
    nct = D_FF // tc

    def body(xg_ref, pg_ref, xu_ref, pu_ref, wg_ref, wu_ref, bg_ref, bu_ref, d_ref,
             du_ref, dbg_ref, dbu_ref):
        i = pl.program_id(1)
        first = i == 0
        ug = _conv(xg_ref[...], jnp.where(first, 0.0, pg_ref[...]), wg_ref[...]) + bg_ref[...]
        uu = _conv(xu_ref[...], jnp.where(first, 0.0, pu_ref[...]), wu_ref[...]) + bu_ref[...]
        d = d_ref[...]
        sig = jax.nn.sigmoid(ug)
        dug = d * uu * (sig * (1.0 + ug * (1.0 - sig)))
        duu = d * (ug * sig)
        du_ref[0] = dug
        du_ref[1] = duu

        @pl.when(first)
        def _():
            dbg_ref[...] = jnp.zeros_like(dbg_ref)
            dbu_ref[...] = jnp.zeros_like(dbu_ref)

        dbg_ref[...] += jnp.sum(dug, axis=0, keepdims=True)
        dbu_ref[...] += jnp.sum(duu, axis=0, keepdims=True)

    def x_spec(off):
        return pl.BlockSpec((tr, tc), lambda j, i: (i, off + j))

    def w_spec(k, off):
        return pl.BlockSpec((k, tc), lambda j, i: (0, off + j))

    tile = pl.BlockSpec((tr, tc), lambda j, i: (i, j))
    vec = pl.BlockSpec((1, tc), lambda j, i: (0, j))
    return pl.pallas_call(
        body, name="ffn_conv_bwd_act", grid=(nct, nr),
        in_specs=[x_spec(0), _prev_spec(tc, 0, r8), x_spec(nct), _prev_spec(tc, nct, r8),
                  w_spec(FFN_CONV_WIDTH, 0), w_spec(FFN_CONV_WIDTH, nct), w_spec(1, 0), w_spec(1, nct), tile],
        out_specs=[pl.BlockSpec((2, tr, tc), lambda j, i: (0, i, j)), vec, vec],
        out_shape=[jax.ShapeDtypeStruct((2, rows, D_FF), f32),
                   jax.ShapeDtypeStruct((1, D_FF), f32), jax.ShapeDtypeStruct((1, D_FF), f32)],
        compiler_params=_cparams(("parallel", "arbitrary")),
    )(u_pre, u_pre, u_pre, u_pre, w, w, b, b, dact)


def _conv_bwd(dy, x, x_cb0, w, name):
    k_taps = w.shape[0]
    split = dy.ndim == 3
    rows = dy.shape[-2]
    ch = dy.shape[-1] * (2 if split else 1)
    tc = TCONV_FF if split else TCONV_C
    tr, nr, r8 = _conv_tiles(rows, TCONV_R // 2 if split else TCONV_R)
    per_half = dy.shape[-1] // tc
    last8 = rows // SUBLANES - 1

    def body(dy_ref, nx_ref, x_ref, w_ref, dx_ref, dw_ref):
        i = pl.program_id(1)
        dyv = dy_ref[...]
        nxt = jnp.where(i == nr - 1, 0.0, nx_ref[...])
        xv = x_ref[...].astype(f32)
        wv = w_ref[...]

        @pl.when(i == 0)
        def _():
            dw_ref[...] = jnp.zeros_like(dw_ref)

        dx = dyv * wv[k_taps - 1:k_taps, :]
        dw_ref[k_taps - 1:k_taps, :] += jnp.sum(dyv * xv, axis=0, keepdims=True)
        for j in range(1, k_taps):
            dy_j = _shift_up(dyv, nxt, j)
            dx = dx + dy_j * wv[k_taps - 1 - j:k_taps - j, :]
            dw_ref[k_taps - 1 - j:k_taps - j, :] += jnp.sum(dy_j * xv, axis=0, keepdims=True)
        dx_ref[...] = dx.astype(dx_ref.dtype)

    tile = pl.BlockSpec((tr, tc), lambda j, i: (i, j))
    if split:
        dy_spec = pl.BlockSpec((None, tr, tc), lambda j, i: (j // per_half, i, j % per_half))
        next_spec = pl.BlockSpec((None, SUBLANES, tc),
                                 lambda j, i: (j // per_half, jnp.minimum((i + 1) * r8, last8), j % per_half))
    else:
        dy_spec = tile
        next_spec = pl.BlockSpec((SUBLANES, tc), lambda j, i: (jnp.minimum((i + 1) * r8, last8), j))
    return pl.pallas_call(
        body, name=name, grid=(ch // tc, nr),
        in_specs=[dy_spec, next_spec, pl.BlockSpec((tr, tc), lambda j, i: (i, x_cb0 + j)),
                  pl.BlockSpec((k_taps, tc), lambda j, i: (0, j))],
        out_specs=[tile, pl.BlockSpec((k_taps, tc), lambda j, i: (0, j))],
        out_shape=[jax.ShapeDtypeStruct((rows, ch), bf16), jax.ShapeDtypeStruct((k_taps, ch), f32)],
        compiler_params=_cparams(("parallel", "arbitrary")),
    )(dy, dy, x, w)


def _hdot(a, b):
    return jnp.dot(a, b, preferred_element_type=f32, precision=lax.Precision.HIGH)


def _xdot(a, b):
    return jnp.dot(a, b, preferred_element_type=f32, precision=lax.Precision.HIGHEST)


def _bdot(a, b):
    return jnp.dot(a.astype(bf16), b.astype(bf16), preferred_element_type=f32)


def _bdot_nt(a, b):
    return lax.dot_general(a.astype(bf16), b.astype(bf16), (((1,), (1,)), ((), ())), preferred_element_type=f32)


def _bdot_tn(a, b):
    return lax.dot_general(a.astype(bf16), b.astype(bf16), (((0,), (0,)), ((), ())), preferred_element_type=f32)


GDN_GROUP = 4
GDN_NGROUPS = DN_HEADS // GDN_GROUP
GDN_ROWS = GDN_GROUP * DN_CHUNK
GDN_QK_LANES = GDN_GROUP * DN_KEY_DIM
GDN_LOGIT_LANE = DN_HEADS


def _inverse_impl(lows):
    n = lows[0].shape[0]
    r = lax.broadcasted_iota(jnp.int32, (n, n), 0)
    c = lax.broadcasted_iota(jnp.int32, (n, n), 1)
    eye = (r == c).astype(f32)
    blk = jnp.right_shift(r, 3) == jnp.right_shift(c, 3)
    d = [jnp.where(blk, low, 0.0) for low in lows]
    e = [low - x for low, x in zip(lows, d)]

    def nilpotent8_inverse(xs):
        acc = [eye - x for x in xs]
        power = xs
        for _ in range(2):
            power = [_bdot(x, x) for x in power]
            acc = [_bdot(a, eye + x) for a, x in zip(acc, power)]
        return acc

    dinv = nilpotent8_inverse(d)
    ninv = nilpotent8_inverse([_bdot(x, y) for x, y in zip(dinv, e)])
    t = [_bdot(x, y) for x, y in zip(ninv, dinv)]
    for _ in range(2):
        res = [eye - x - _hdot(low, x) for low, x in zip(lows, t)]
        t = [x + _bdot(x, y) for x, y in zip(t, res)]
    return tuple(t)


@jax.custom_vjp
def _unit_lower_inverses(lows):
    return _inverse_impl(lows)


def _unit_lower_inverses_fwd(lows):
    t = _inverse_impl(lows)
    return t, t


def _unit_lower_inverses_bwd(t, ct):
    tn = (((0,), (0,)), ((), ()))
    nt = (((1,), (1,)), ((), ()))
    left = [lax.dot_general(x, g, tn, preferred_element_type=f32, precision=lax.Precision.HIGH) for x, g in zip(t, ct)]
    return (tuple(-lax.dot_general(x, y, nt, preferred_element_type=f32, precision=lax.Precision.HIGH)
                  for x, y in zip(left, t)),)


_unit_lower_inverses.defvjp(_unit_lower_inverses_fwd, _unit_lower_inverses_bwd)


@jax.custom_vjp
def _known_inverses(lows, t):
    return t


def _known_inverses_fwd(lows, t):
    return t, t


def _known_inverses_bwd(t, ct):
    return _unit_lower_inverses_bwd(t, ct) + (tuple(jnp.zeros_like(x) for x in t),)


_known_inverses.defvjp(_known_inverses_fwd, _known_inverses_bwd)


def _gdn_chunk(a_log, dt_bias, norm_w, ba, *per_group, inverses=None, keep_inverses=False):
    ng = GDN_NGROUPS
    qgs, kgs, vsts, zsts, states = [per_group[i * ng:(i + 1) * ng] for i in range(5)]
    groups = range(ng)
    n = GDN_ROWS
    r = lax.broadcasted_iota(jnp.int32, (n, n), 0)
    c = lax.broadcasted_iota(jnp.int32, (n, n), 1)
    same_head = jnp.right_shift(r, 6) == jnp.right_shift(c, 6)
    incl = jnp.logical_and(same_head, r >= c)
    strict = jnp.logical_and(same_head, r > c)
    eye = (r == c).astype(f32)
    ones = jnp.ones((n, n), f32)
    own_lanes = same_head.astype(f32)
    lane = lax.broadcasted_iota(jnp.int32, (1, LANES), 1)
    pick = lambda arr, idx: jnp.sum(jnp.where(lane == idx, arr, 0.0), axis=1, keepdims=True)
    heads = [[GDN_GROUP * g + h for h in range(GDN_GROUP)] for g in groups]
    rc = lax.broadcasted_iota(jnp.int32, (DN_CHUNK, DN_CHUNK), 0)
    cc = lax.broadcasted_iota(jnp.int32, (DN_CHUNK, DN_CHUNK), 1)

    g_all = -jnp.exp(a_log) * _softplus(ba + dt_bias)
    gc_all = _xdot((rc >= cc).astype(f32), g_all)
    gl_all = jnp.sum(g_all, axis=0, keepdims=True)
    beta = [jnp.concatenate([jax.nn.sigmoid(pick(ba, hd)) for hd in heads[g]], axis=0) for g in groups]
    gc = [jnp.concatenate([pick(gc_all, GDN_LOGIT_LANE + hd) for hd in heads[g]], axis=0) for g in groups]
    g_last = [jnp.concatenate([jnp.broadcast_to(pick(gl_all, GDN_LOGIT_LANE + hd), (DN_CHUNK, 1)) for hd in heads[g]],
                              axis=0) for g in groups]
    gr = [jnp.broadcast_to(gc[g], (n, n)).T for g in groups]
    decay = [jnp.where(incl, jnp.exp(jnp.where(incl, gc[g] - gr[g], 0.0)), 0.0) for g in groups]
    q = [jnp.concatenate([qgs[g]] * GDN_GROUP, axis=0) * own_lanes for g in groups]
    k = [jnp.concatenate([kgs[g]] * GDN_GROUP, axis=0) * own_lanes for g in groups]
    qn = [x * lax.rsqrt(jnp.sum(x * x, axis=1, keepdims=True) + L2_EPS) * (DN_KEY_DIM ** -0.5) for x in q]
    kn = [x * lax.rsqrt(jnp.sum(x * x, axis=1, keepdims=True) + L2_EPS) for x in k]
    kb = [kn[g] * beta[g] for g in groups]
    low = [jnp.where(strict, _bdot_nt(kb[g], kn[g]) * decay[g], 0.0) for g in groups]
    intra = [jnp.where(incl, _bdot_nt(qn[g], kn[g]) * decay[g], 0.0) for g in groups]
    t = _unit_lower_inverses(tuple(low)) if inverses is None else _known_inverses(tuple(low), tuple(inverses))
    u = [_bdot(t[g], vsts[g] * beta[g]) for g in groups]
    w = [_bdot(t[g], kb[g] * jnp.exp(gc[g])) for g in groups]
    sb = [s.astype(bf16) for s in states]
    v_new = [u[g] - jnp.dot(w[g].astype(bf16), sb[g], preferred_element_type=f32) for g in groups]
    o = [jnp.dot((qn[g] * jnp.exp(gc[g])).astype(bf16), sb[g], preferred_element_type=f32) for g in groups]
    o = [o[g] + _bdot(intra[g], v_new[g]) for g in groups]
    new_state = [states[g] * jnp.exp(g_last[g]) + _bdot_tn(kn[g] * jnp.exp(g_last[g] - gc[g]), v_new[g])
                 for g in groups]
    o_n = [x * lax.rsqrt(jnp.mean(x * x, axis=1, keepdims=True) + NORM_EPS) * norm_w for x in o]
    return tuple(o_n[g] * _silu(zsts[g]) for g in groups) + tuple(new_state) + (tuple(t) if keep_inverses else ())


def _gdn_specs(rows, reverse):
    n = rows // DN_CHUNK
    idx = (lambda i: n - 1 - i) if reverse else (lambda i: i)
    vec = pl.BlockSpec((1, LANES), lambda i: (0, 0))
    qkv = pl.BlockSpec((DN_CHUNK, DN_CONV_CH), lambda i: (idx(i), 0))
    z = pl.BlockSpec((DN_CHUNK, DN_V_WIDTH), lambda i: (idx(i), OFF_Z // DN_V_WIDTH))
    ba = pl.BlockSpec((DN_CHUNK, LANES), lambda i: (idx(i), 0))
    wide = pl.BlockSpec((DN_CHUNK, DN_V_WIDTH), lambda i: (idx(i), 0))
    st = pl.BlockSpec((1, DN_HEADS * DN_KEY_DIM, LANES), lambda i: (idx(i), 0, 0))
    inv = pl.BlockSpec((1, GDN_NGROUPS * GDN_ROWS, GDN_ROWS), lambda i: (idx(i), 0, 0))
    return n, vec, qkv, z, ba, wide, st, inv


def _gdn_slices(grp):
    q = slice(grp * GDN_QK_LANES, (grp + 1) * GDN_QK_LANES)
    k = slice(DN_QK_WIDTH + grp * GDN_QK_LANES, DN_QK_WIDTH + (grp + 1) * GDN_QK_LANES)
    heads = [slice((GDN_GROUP * grp + h) * LANES, (GDN_GROUP * grp + h + 1) * LANES) for h in range(GDN_GROUP)]
    vs = [slice(2 * DN_QK_WIDTH + s.start, 2 * DN_QK_WIDTH + s.stop) for s in heads]
    return q, k, vs, heads


def _stack_cols(ref, cols):
    return jnp.concatenate([ref[:, s] for s in cols], axis=0)


def _gdn_operands(qkv_ref, z_ref, state_rows):
    sl = [_gdn_slices(grp) for grp in range(GDN_NGROUPS)]
    return ([qkv_ref[:, q] for q, _, _, _ in sl] + [qkv_ref[:, k] for _, k, _, _ in sl]
            + [_stack_cols(qkv_ref, vs) for _, _, vs, _ in sl] + [_stack_cols(z_ref, heads) for _, _, _, heads in sl]
            + [state_rows[grp * GDN_ROWS:(grp + 1) * GDN_ROWS, :] for grp in range(GDN_NGROUPS)])


def _gdn_fwd(a_log, dt_bias, norm_w, qkv_act, proj, ba):
    rows = qkv_act.shape[0]
    n, vec, qkv_s, z_s, ba_s, wide, st_s, inv_s = _gdn_specs(rows, False)

    def body(al_ref, dt_ref, nw_ref, qkv_ref, z_ref, ba_ref, o_ref, st_ref, inv_ref, state):
        @pl.when(pl.program_id(0) == 0)
        def _():
            state[...] = jnp.zeros_like(state)

        st_ref[0] = state[...]
        out = _gdn_chunk(al_ref[...], dt_ref[...], nw_ref[...], ba_ref[...], *_gdn_operands(qkv_ref, z_ref, state),
                         keep_inverses=True)
        for grp in range(GDN_NGROUPS):
            _, _, _, heads = _gdn_slices(grp)
            rs = slice(grp * GDN_ROWS, (grp + 1) * GDN_ROWS)
            for h, s in enumerate(heads):
                o_ref[:, s] = out[grp][h * DN_CHUNK:(h + 1) * DN_CHUNK].astype(o_ref.dtype)
            state[rs, :] = out[GDN_NGROUPS + grp]
            inv_ref[0, rs, :] = out[2 * GDN_NGROUPS + grp]

    return pl.pallas_call(
        body, name="gdn_fwd", grid=(n,),
        in_specs=[vec, vec, vec, qkv_s, z_s, ba_s], out_specs=[wide, st_s, inv_s],
        out_shape=[jax.ShapeDtypeStruct((rows, DN_V_WIDTH), bf16),
                   jax.ShapeDtypeStruct((n, DN_HEADS * DN_KEY_DIM, LANES), f32),
                   jax.ShapeDtypeStruct((n, GDN_NGROUPS * GDN_ROWS, GDN_ROWS), f32)],
        scratch_shapes=[pltpu.VMEM((DN_HEADS * DN_KEY_DIM, LANES), f32)],
        compiler_params=_cparams(("arbitrary",)),
    )(a_log, dt_bias, norm_w, qkv_act, proj, ba)


def _gdn_bwd(a_log, dt_bias, norm_w, qkv_act, proj, ba, states, inverses, do):
    rows = qkv_act.shape[0]
    n, vec, qkv_s, z_s, ba_s, wide, st_s, inv_s = _gdn_specs(rows, True)

    def body(al_ref, dt_ref, nw_ref, qkv_ref, z_ref, ba_ref, st_ref, inv_ref, do_ref,
             dal_ref, ddt_ref, dnw_ref, dqkv_ref, dz_ref, dba_ref, dstate):
        @pl.when(pl.program_id(0) == 0)
        def _():
            dstate[...] = jnp.zeros_like(dstate)
            dal_ref[...] = jnp.zeros_like(dal_ref)
            ddt_ref[...] = jnp.zeros_like(ddt_ref)
            dnw_ref[...] = jnp.zeros_like(dnw_ref)

        ng = GDN_NGROUPS
        kept = [inv_ref[0, grp * GDN_ROWS:(grp + 1) * GDN_ROWS, :] for grp in range(ng)]
        _, vjp = jax.vjp(functools.partial(_gdn_chunk, inverses=kept), al_ref[...], dt_ref[...], nw_ref[...],
                         ba_ref[...], *_gdn_operands(qkv_ref, z_ref, st_ref[0]))
        cts = tuple(_stack_cols(do_ref, _gdn_slices(grp)[3]) for grp in range(ng))
        cts += tuple(dstate[grp * GDN_ROWS:(grp + 1) * GDN_ROWS, :] for grp in range(ng))
        grads = vjp(cts)
        dal_ref[...] += grads[0]
        ddt_ref[...] += grads[1]
        dnw_ref[...] += grads[2]
        dba_ref[...] = grads[3]
        dqs, dks, dvs, dzs, dss = [grads[4 + i * ng:4 + (i + 1) * ng] for i in range(5)]
        for grp in range(ng):
            q, k, vs, heads = _gdn_slices(grp)
            dqkv_ref[:, q] = dqs[grp]
            dqkv_ref[:, k] = dks[grp]
            for h, (sv, sh) in enumerate(zip(vs, heads)):
                rows_h = slice(h * DN_CHUNK, (h + 1) * DN_CHUNK)
                dqkv_ref[:, sv] = dvs[grp][rows_h]
                dz_ref[:, sh] = dzs[grp][rows_h].astype(dz_ref.dtype)
            dstate[grp * GDN_ROWS:(grp + 1) * GDN_ROWS, :] = dss[grp]

    return pl.pallas_call(
        body, name="gdn_bwd", grid=(n,),
        in_specs=[vec, vec, vec, qkv_s, z_s, ba_s, st_s, inv_s, wide],
        out_specs=[vec, vec, vec, qkv_s, wide, ba_s],
        out_shape=[jax.ShapeDtypeStruct((1, LANES), f32)] * 3
        + [jax.ShapeDtypeStruct((rows, DN_CONV_CH), f32), jax.ShapeDtypeStruct((rows, DN_V_WIDTH), bf16),
           jax.ShapeDtypeStruct((rows, LANES), f32)],
        scratch_shapes=[pltpu.VMEM((DN_HEADS * DN_KEY_DIM, LANES), f32)],
        compiler_params=_cparams(("arbitrary",)),
    )(a_log, dt_bias, norm_w, qkv_act, proj, ba, states, inverses, do)


def _ada_fwd(c_all, w_loc, b_loc):
    def body(c_ref, w_ref, b_ref, o_ref):
        o_ref[...] = _bdot(_silu(c_ref[...]), w_ref[...]) + b_ref[...]

    return pl.pallas_call(body, name="ada_fwd", out_shape=jax.ShapeDtypeStruct((c_all.shape[0], w_loc.shape[1]), f32),
                          compiler_params=_cparams())(c_all, w_loc, b_loc)


def _ada_bwd(c_all, dmod_cols):
    def body(c_ref, d_ref, o_ref):
        o_ref[...] = _bdot_tn(_silu(c_ref[...]), d_ref[...])

    return pl.pallas_call(body, name="ada_bwd",
                          out_shape=jax.ShapeDtypeStruct((c_all.shape[1], dmod_cols.shape[1]), f32),
                          compiler_params=_cparams())(c_all, dmod_cols)


def _sum_devices(parts):
    def body(p_ref, o_ref):
        acc = p_ref[0:1, :]
        for d in range(1, N_DEV):
            acc = acc + p_ref[d:d + 1, :]
        o_ref[...] = acc

    return pl.pallas_call(body, name="sum_small", out_shape=jax.ShapeDtypeStruct((1, parts.shape[1]), f32),
                          compiler_params=_cparams())(parts)


def _adam_math(w, g, m, v):
    m2 = ADAM_B1 * m + (1.0 - ADAM_B1) * g
    v2 = ADAM_B2 * v + (1.0 - ADAM_B2) * jnp.square(g)
    m_hat = m2 / (1.0 - ADAM_B1 ** ADAM_STEP)
    v_hat = v2 / (1.0 - ADAM_B2 ** ADAM_STEP)
    delta = -ADAM_LR * (m_hat / (jnp.sqrt(v_hat) + ADAM_EPS) + ADAM_WD * w)
    return delta, m2, v2


def _row_tile(rows):
    return _pick(rows, (256, 128, 64, 32, 16, 8))


def _adamw(w, g, m, v, name):
    rows, cols = w.shape
    tr = _row_tile(rows)

    def body(w_ref, g_ref, m_ref, v_ref, d_ref, m2_ref, v2_ref):
        d_ref[...], m2_ref[...], v2_ref[...] = _adam_math(w_ref[...], g_ref[...], m_ref[...], v_ref[...])

    tile = pl.BlockSpec((tr, cols), lambda i: (i, 0))
    return pl.pallas_call(body, name=name, grid=(rows // tr,), in_specs=[tile] * 4, out_specs=[tile] * 3,
                          out_shape=[jax.ShapeDtypeStruct(w.shape, f32)] * 3,
                          compiler_params=_cparams(("parallel",)))(w, g, m, v)


def _sum_adamw(parts, w, m, v, name):
    rows, cols = w.shape
    tr = _row_tile(rows)

    def body(p_ref, w_ref, m_ref, v_ref, g_ref, d_ref, m2_ref, v2_ref):
        g = p_ref[0].astype(f32)
        for d in range(1, N_DEV):
            g = g + p_ref[d].astype(f32)
        g_ref[...] = g
        d_ref[...], m2_ref[...], v2_ref[...] = _adam_math(w_ref[...], g, m_ref[...], v_ref[...])

    tile = pl.BlockSpec((tr, cols), lambda i: (i, 0))
    return pl.pallas_call(body, name=name, grid=(rows // tr,),
                          in_specs=[pl.BlockSpec((N_DEV, tr, cols), lambda i: (0, i, 0)), tile, tile, tile],
                          out_specs=[tile] * 4, out_shape=[jax.ShapeDtypeStruct(w.shape, f32)] * 4,
                          compiler_params=_cparams(("parallel",)))(parts, w, m, v)


def _pad_lanes(a, width):
    return jnp.pad(a, ((0, 0), (0, width - a.shape[1])))


def _cols_by_device(full):
    r, c = full.shape
    return jnp.moveaxis(full.reshape(r, N_DEV, c // N_DEV), 1, 0)


def _cols_from_devices(parts):
    d, r, n = parts.shape
    return jnp.moveaxis(parts, 0, 1).reshape(r, d * n)


def kernel(x, c, w_ada, b_ada, norm1_w, w_in, dn_conv_w, dn_A_log, dn_dt_bias, dn_norm_w, w_proj_sb, w_proj_dn, w_out, norm2_w, w_ffn_in, ffn_conv_w, ffn_conv_b, w_ffn_out, final_norm_w, loss_target, m_w_ada, m_b_ada, m_norm1_w, m_w_in, m_dn_conv_w, m_dn_A_log, m_dn_dt_bias, m_dn_norm_w, m_w_proj_sb, m_w_proj_dn, m_w_out, m_norm2_w, m_w_ffn_in, m_ffn_conv_w, m_ffn_conv_b, m_w_ffn_out, m_final_norm_w, v_w_ada, v_b_ada, v_norm1_w, v_w_in, v_dn_conv_w, v_dn_A_log, v_dn_dt_bias, v_dn_norm_w, v_w_proj_sb, v_w_proj_dn, v_w_out, v_norm2_w, v_w_ffn_in, v_ffn_conv_w, v_ffn_conv_b, v_w_ffn_out, v_final_norm_w):
    d = D_MODEL
    me = 4 * lax.axis_index("x") + 2 * lax.axis_index("y") + lax.axis_index("c")
    xs = x[0]
    target = loss_target[0]
    n_ada = w_ada.shape[2]
    n_dnc = dn_conv_w.shape[2]
    n_ffc = ffn_conv_w.shape[2]

    small = jnp.concatenate([c, dn_conv_w[0].reshape(1, -1), ffn_conv_w[0].reshape(1, -1)], axis=1)
    small = _pad_lanes(small, -(-small.shape[1] // LANES) * LANES)
    small_g, w_in_g = _all_gather([small, w_in[0].astype(bf16)], "gather_w_in")
    later = [w_proj_sb[0].astype(bf16), w_proj_dn[0].astype(bf16), w_out[0].astype(bf16),
             w_ffn_in[0].astype(bf16), w_ffn_out[0].astype(bf16)]
    gather_later = _SideComm(_gather_protocol, later, _gathered_shapes(later))
    small_g = small_g[:, 0, :]
    c_all = small_g[:, :d]
    dn_cw = _cols_from_devices(small_g[:, d:d + DN_CONV_WIDTH * n_dnc].reshape(N_DEV, DN_CONV_WIDTH, n_dnc))
    o2 = d + DN_CONV_WIDTH * n_dnc
    ffn_cw = _cols_from_devices(small_g[:, o2:o2 + FFN_CONV_WIDTH * n_ffc].reshape(N_DEV, FFN_CONV_WIDTH, n_ffc))

    w_in_full = _cols_from_devices(w_in_g)
    r_sb, r_dn, r_z = 3 * SB_WIDTH, 3 * SB_WIDTH + DN_CONV_CH, 3 * SB_WIDTH + DN_CONV_CH + DN_V_WIDTH
    r_g = r_z + 2 * DN_HEADS
    w_main = jnp.concatenate([w_in_full[:, r_g:], w_in_full[:, r_sb:r_dn], w_in_full[:, r_dn:r_z],
                              w_in_full[:, :r_sb]], axis=1)
    w_ba = _pad_lanes(w_in_full[:, r_z:r_g], LANES)

    b_loc = lax.dynamic_slice(b_ada, (0, me * n_ada), (1, n_ada))
    mod_part = _ada_fwd(c_all, w_ada[0], b_loc)
    (mod_g,) = _all_gather([mod_part], "gather_mod")
    mod = lax.dynamic_index_in_dim(mod_g, me, axis=1, keepdims=False).reshape(1, N_DEV * n_ada)
    shift1, scale1, gate1, shift2, scale2, gate2 = [mod[:, i * d:(i + 1) * d] for i in range(6)]

    logit_lanes = ((0, 0), (GDN_LOGIT_LANE, LANES - GDN_LOGIT_LANE - DN_HEADS))
    a_log = jnp.pad(dn_A_log, logit_lanes)
    dt_b = jnp.pad(dn_dt_bias, logit_lanes)

    (h,) = _stage_fwd(_f_normmod, [norm1_w, shift1, scale1], [xs], [bf16], "norm1_fwd")
    proj = _mm(h, w_main, name="in_proj")
    ba = _mm(h, w_ba, name="in_proj_ba")
    k16, k0_16, k1_16, v16, v0_16, v1_16 = _sb_prepare(proj)
    o_a, sb_runs, w_psb_g, w_pdn_g, w_out_g, w_fin_g, w_fout_g = _sb_attention_fwd2(
        proj, k16, v0_16, v1_16, side=gather_later)
    w_psb = _cols_from_devices(w_psb_g)
    w_pdn = w_pdn_g.reshape(DN_V_WIDTH, d)
    w_o = w_out_g.reshape(d, d)
    w_fin = _cols_from_devices(w_fin_g)
    w_fout = w_fout_g.reshape(D_FF, d)
    qkv_act = _dn_conv_fwd(proj, dn_cw)
    o_b, states, dn_inverses = _gdn_fwd(a_log, dt_b, dn_norm_w, qkv_act, proj, ba)
    pa = _mm(o_a, w_psb, name="proj_sb")
    pb = _mm(o_b, w_pdn, name="proj_dn")
    gates = [(proj, d, OFF_GA // d), (proj, d, OFF_GB // d)]
    (merged,) = _stage_fwd(_f_merge, [], gates + [pa, pb], [bf16], "merge_fwd")
    ao = _mm(merged, w_o, name="out_proj")
    mid_params = [gate1, norm2_w, shift2, scale2]
    x1, h2 = _stage_fwd(_f_residual_normmod, mid_params, [xs, ao], [f32, bf16], "resid1_norm2_fwd")
    u_pre = _mm(h2, w_fin, name="ffn_in")
    act = _ffn_conv_fwd(u_pre, ffn_cw, ffn_conv_b)
    fo = _mm(act, w_fout, name="ffn_out")

    loss_p, d_gate2, d_wf, dx2, dfo = _loss_and_grads(gate2, final_norm_w.reshape(1, d), x1, fo, target)
    dact = _mm(dfo, w_fout, tb=True, name="ffn_out_dx")
    g_w_fout = _mm(act, dfo, ta=True, name="ffn_out_dw")
    du, dbg, dbu = _ffn_conv_bwd_act(u_pre, ffn_cw, ffn_conv_b, dact)
    du_pre, d_ffn_cw = _conv_bwd(du, u_pre, 0, ffn_cw, "ffn_conv_bwd")
    dh2 = _mm(du_pre, w_fin, tb=True, name="ffn_in_dx")
    g_w_fin = _mm(h2, du_pre, ta=True, name="ffn_in_dw")
    (d_gate1, d_n2w, d_shift2, d_scale2), (dx1, dao) = _stage_bwd(
        _f_residual_normmod, mid_params, [xs, ao], [dx2, dh2], [f32, bf16], "resid1_norm2_bwd")
    dmerged = _mm(dao, w_o, tb=True, name="out_proj_dx")
    g_w_o = _mm(merged, dao, ta=True, name="out_proj_dw")
    _, (dga, dgb, dpa, dpb) = _stage_bwd(_f_merge, [], gates + [pa, pb], [dmerged], [bf16] * 4, "merge_bwd")
    do_a = _mm(dpa, w_psb, tb=True, name="proj_sb_dx")
    g_w_psb = _mm(o_a, dpa, ta=True, name="proj_sb_dw")
    do_b = _mm(dpb, w_pdn, tb=True, name="proj_dn_dx")
    g_w_pdn = _mm(o_b, dpb, ta=True, name="proj_dn_dw")
    early = [_cols_by_device(g_w_psb).astype(bf16),
             g_w_pdn.reshape(N_DEV, DN_V_WIDTH // N_DEV, d).astype(bf16),
             g_w_o.reshape(N_DEV, d // N_DEV, d).astype(bf16),
             _cols_by_device(g_w_fin).astype(bf16),
             g_w_fout.reshape(N_DEV, D_FF // N_DEV, d).astype(bf16)]
    exchange_early = _SideComm(_exchange_protocol, early, [jax.ShapeDtypeStruct(a.shape, a.dtype) for a in early])
    dq, dk, dv, *recv_early = _sb_attention_bwd2(proj, k16, k0_16, k1_16, v16, sb_runs, do_a, side=exchange_early)
    d_alog, d_dtb, d_dnw, dqkv_act, dz, dba = _gdn_bwd(a_log, dt_b, dn_norm_w, qkv_act, proj, ba, states,
                                                       dn_inverses, do_b)
    d_conv_out = _dn_conv_bwd_act(proj, dn_cw, dqkv_act)
    d_dn_pre, d_dn_cw = _conv_bwd(d_conv_out, proj, OFF_DN // TCONV_C, dn_cw, "dn_conv_bwd")
    dproj = jnp.concatenate([dga, dgb, d_dn_pre, dz, dq, dk.astype(bf16), dv.astype(bf16)], axis=1)
    g_w_main = _mm(h, dproj, ta=True, name="in_proj_dw")
    g_w_ba = _mm(h, dba, ta=True, name="in_proj_ba_dw")
    g_w_in_full = jnp.concatenate([g_w_main[:, OFF_SBQ:], g_w_main[:, OFF_DN:OFF_Z], g_w_main[:, OFF_Z:OFF_SBQ],
                                   g_w_ba[:, :2 * DN_HEADS], g_w_main[:, :OFF_DN]], axis=1)
    w_in_parts = _cols_by_device(g_w_in_full).astype(bf16)
    exchange_w_in = _SideComm(_exchange_protocol, [w_in_parts], [jax.ShapeDtypeStruct(w_in_parts.shape, bf16)])
    dh, recv_w_in = _mm(dproj, w_main, tb=True, name="in_proj_dx", side=exchange_w_in)
    dh_ba = _mm(dba, w_ba, tb=True, name="in_proj_ba_dx")
    (d_n1w, d_shift1, d_scale1), (grad_x,) = _stage_bwd(
        _f_normmod, [norm1_w, shift1, scale1], [xs], [[dh, dh_ba]], [f32], "norm1_bwd", residual=(0, dx1))

    dmod = jnp.concatenate([d_shift1, d_scale1, d_gate1, d_shift2, d_scale2, d_gate2], axis=1)
    d_ffn_cb = jnp.concatenate([dbg, dbu], axis=1)
    small_parts = jnp.concatenate(
        [loss_p, dmod, d_n1w, d_alog, d_dtb, d_dnw, d_n2w, d_ffn_cb, d_wf,
         d_dn_cw.reshape(1, -1), d_ffn_cw.reshape(1, -1)], axis=1)
    (small_parts_g,) = _all_gather([small_parts], "gather_small_grads")
    tot = _sum_devices(small_parts_g[:, 0, :])
    offs = {}
    pos = 0
    for nm, width in (("loss", LANES), ("b_ada", 6 * d), ("norm1_w", d), ("dn_A_log", LANES), ("dn_dt_bias", LANES),
                      ("dn_norm_w", LANES), ("norm2_w", d), ("ffn_conv_b", 2 * D_FF), ("final_norm_w", d),
                      ("dn_conv_w", DN_CONV_WIDTH * DN_CONV_CH), ("ffn_conv_w", FFN_CONV_WIDTH * 2 * D_FF)):
        offs[nm] = (pos, width)
        pos += width
    seg = lambda nm: tot[:, offs[nm][0]:offs[nm][0] + offs[nm][1]]
    loss = tot[0, 0]
    g_b_ada = seg("b_ada")
    g_norm1 = seg("norm1_w")
    g_alog = seg("dn_A_log")[:, GDN_LOGIT_LANE:GDN_LOGIT_LANE + DN_HEADS]
    g_dtb = seg("dn_dt_bias")[:, GDN_LOGIT_LANE:GDN_LOGIT_LANE + DN_HEADS]
    g_dnw = seg("dn_norm_w")
    g_norm2 = seg("norm2_w")
    g_ffn_cb = seg("ffn_conv_b")
    g_fnw = seg("final_norm_w")
    g_dn_cw = lax.dynamic_slice(seg("dn_conv_w").reshape(DN_CONV_WIDTH, DN_CONV_CH), (0, me * n_dnc),
                                (DN_CONV_WIDTH, n_dnc))
    g_ffn_cw = lax.dynamic_slice(seg("ffn_conv_w").reshape(FFN_CONV_WIDTH, 2 * D_FF), (0, me * n_ffc),
                                 (FFN_CONV_WIDTH, n_ffc))

    dmod_all = small_parts_g[:, 0, offs["b_ada"][0]:offs["b_ada"][0] + 6 * d]
    g_w_ada = _ada_bwd(c_all, lax.dynamic_slice(dmod_all, (0, me * n_ada), (N_DEV, n_ada)))

    def pack(parts):
        flat = [p.reshape(1, -1) for p in parts]
        flat = [_pad_lanes(p, -(-p.shape[1] // LANES) * LANES) for p in flat]
        return jnp.concatenate(flat, axis=1), [p.shape[1] for p in flat]

    small_names_g = [g_b_ada, g_norm1, g_alog, g_dtb, g_dnw, g_norm2, g_ffn_cb, g_fnw, g_dn_cw, g_ffn_cw]
    small_w = [b_ada, norm1_w, dn_A_log, dn_dt_bias, dn_norm_w, norm2_w, ffn_conv_b, final_norm_w, dn_conv_w[0], ffn_conv_w[0]]
    small_m = [m_b_ada, m_norm1_w, m_dn_A_log, m_dn_dt_bias, m_dn_norm_w, m_norm2_w, m_ffn_conv_b, m_final_norm_w, m_dn_conv_w[0], m_ffn_conv_w[0]]
    small_v = [v_b_ada, v_norm1_w, v_dn_A_log, v_dn_dt_bias, v_dn_norm_w, v_norm2_w, v_ffn_conv_b, v_final_norm_w, v_dn_conv_w[0], v_ffn_conv_w[0]]
    pg, widths = pack(small_names_g)
    pw, _ = pack(small_w)
    pm, _ = pack(small_m)
    pv, _ = pack(small_v)
    s_delta, s_m, s_v = _adamw(pw, pg, pm, pv, "adamw_small")

    def unpack(flat):
        out, pos = [], 0
        for ref_arr, width in zip(small_w, widths):
            out.append(flat[:, pos:pos + ref_arr.size].reshape(ref_arr.shape))
            pos += width
        return out

    small_grads = [g.reshape(w_.shape) for g, w_ in zip(small_names_g, small_w)]
    small_delta, small_newm, small_newv = unpack(s_delta), unpack(s_m), unpack(s_v)

    ada_delta, ada_m, ada_v = _adamw(w_ada[0], g_w_ada, m_w_ada[0], v_w_ada[0], "adamw_ada")

    recv = [recv_w_in] + list(recv_early)
    big = {}
    for nm, parts, w_, m_, v_ in (("w_in", recv[0], w_in, m_w_in, v_w_in),
                                  ("w_proj_sb", recv[1], w_proj_sb, m_w_proj_sb, v_w_proj_sb),
                                  ("w_proj_dn", recv[2], w_proj_dn, m_w_proj_dn, v_w_proj_dn),
                                  ("w_out", recv[3], w_out, m_w_out, v_w_out),
                                  ("w_ffn_in", recv[4], w_ffn_in, m_w_ffn_in, v_w_ffn_in),
                                  ("w_ffn_out", recv[5], w_ffn_out, m_w_ffn_out, v_w_ffn_out)):
        big[nm] = [t[None] for t in _sum_adamw(parts, w_[0], m_[0], v_[0], "adamw_" + nm)]

    sg = dict(zip(["b_ada", "norm1_w", "dn_A_log", "dn_dt_bias", "dn_norm_w", "norm2_w", "ffn_conv_b", "final_norm_w",
                   "dn_conv_w", "ffn_conv_w"], range(10)))

    def small_out(table, nm):
        val = table[sg[nm]]
        return val[None] if nm in ("dn_conv_w", "ffn_conv_w") else val

    order = ["w_ada", "b_ada", "norm1_w", "w_in", "dn_conv_w", "dn_A_log", "dn_dt_bias", "dn_norm_w", "w_proj_sb",
             "w_proj_dn", "w_out", "norm2_w", "w_ffn_in", "ffn_conv_w", "ffn_conv_b", "w_ffn_out", "final_norm_w"]
    groups = []
    for k, small_table in enumerate((small_grads, small_delta, small_newm, small_newv)):
        row = []
        for nm in order:
            if nm == "w_ada":
                row.append((g_w_ada, ada_delta, ada_m, ada_v)[k][None])
            elif nm in big:
                row.append(big[nm][k])
            else:
                row.append(small_out(small_table, nm))
        groups.append(row)
    return (loss, grad_x[None], *groups[0], *groups[1], *groups[2], *groups[3])
```

```python
import functools

import jax
import jax.numpy as jnp
from jax import lax
from jax.experimental import pallas as pl
from jax.experimental.pallas import tpu as pltpu

f32 = jnp.float32
bf16 = jnp.bfloat16

D_MODEL = 1024
SB_HEADS = 8
SB_HEAD_DIM = 64
SB_WIDTH = SB_HEADS * SB_HEAD_DIM
SB_QBLOCK = 128
DN_HEADS = 8
DN_KEY_DIM = 64
DN_VAL_DIM = 128
DN_QK_WIDTH = DN_HEADS * DN_KEY_DIM
DN_V_WIDTH = DN_HEADS * DN_VAL_DIM
DN_CONV_CH = 2 * DN_QK_WIDTH + DN_V_WIDTH
DN_CONV_WIDTH = 4
DN_CHUNK = 64
D_FF = 2816
FFN_CONV_WIDTH = 3
NORM_EPS = 1e-6
L2_EPS = 1e-6
ADAM_LR = 0.001
ADAM_B1 = 0.9
ADAM_B2 = 0.999
ADAM_EPS = 1e-08
ADAM_WD = 0.01
ADAM_STEP = 10

N_DEV = 8
MESH = pl.DeviceIdType.MESH

LANES = 128
SUBLANES = 8
VMEM_LIMIT = 48 * 1024 * 1024

OFF_GA = 0
OFF_GB = D_MODEL
OFF_DN = 2 * D_MODEL
OFF_Z = OFF_DN + DN_CONV_CH
OFF_SBQ = OFF_Z + DN_V_WIDTH
OFF_SBK = OFF_SBQ + SB_WIDTH
OFF_SBV = OFF_SBK + SB_WIDTH
MAIN_WIDTH = OFF_SBV + SB_WIDTH

TM = 256
TCONV_R = 512
TCONV_C = 512
TCONV_FF = D_FF // 2
SB_PAIRS_FWD = 4
SB_PAIRS_BWD = 4
SB_DEAD = -106.0
SB_NEVER = -1e30


def _cparams(sem=None):
    return pltpu.CompilerParams(dimension_semantics=sem, vmem_limit_bytes=VMEM_LIMIT)


def _pick(n, cands):
    for c in cands:
        if n % c == 0:
            return c
    return n


def _my_pos():
    return lax.axis_index("x"), lax.axis_index("y"), lax.axis_index("c")


def _flip(v, bit):
    return 1 - v if bit else v


def _comm_scratch(n):
    return [pltpu.SemaphoreType.DMA((n, 7)), pltpu.SemaphoreType.DMA((n, 7)), pltpu.SemaphoreType.DMA((n,))]


def _gather_protocol(ins, outs, send_sems, recv_sems, local_sems):
    n = len(ins)
    x, y, c = _my_pos()
    me, sibling = (x, y, c), (x, y, 1 - c)
    chips = [(1 - x, y), (x, 1 - y), (1 - x, 1 - y)]

    def slot(out, pos):
        return out.at[4 * pos[0] + 2 * pos[1] + pos[2]]

    def copy(a, k, block, to, src=None):
        return pltpu.make_async_remote_copy(
            src_ref=slot(outs[a], block) if src is None else src, dst_ref=slot(outs[a], block),
            send_sem=send_sems.at[a, k], recv_sem=recv_sems.at[a, k], device_id=to, device_id_type=MESH)

    def local(a):
        return pltpu.make_async_copy(ins[a], slot(outs[a], me), local_sems.at[a])

    def first(a):
        return [copy(a, 0, me, sibling, src=ins[a])] + [copy(a, 1 + j, me, (*chip, c), src=ins[a])
                                                         for j, chip in enumerate(chips)]

    def start():
        for a in range(n):
            local(a).start()
            for cp in first(a):
                cp.start()

    def finish():
        forwards = []
        for a in range(n):
            for j, chip in enumerate(chips):
                copy(a, 1 + j, (*chip, c), me).wait_recv()
                fwd = copy(a, 4 + j, (*chip, c), sibling)
                fwd.start()
                forwards.append(fwd)
        for a in range(n):
            copy(a, 0, sibling, me).wait_recv()
            for j, chip in enumerate(chips):
                copy(a, 4 + j, (*chip, 1 - c), me).wait_recv()
        for a in range(n):
            for cp in first(a):
                cp.wait_send()
        for cp in forwards:
            cp.wait_send()
        for a in range(n):
            local(a).wait()

    return start, finish


def _exchange_protocol(ins, outs, send_sems, recv_sems, local_sems):
    n = len(ins)
    x, y, c = _my_pos()
    me_idx = 4 * x + 2 * y + c

    def local(a):
        return pltpu.make_async_copy(ins[a].at[me_idx], outs[a].at[me_idx], local_sems.at[a])

    def copies(a, m):
        peer = (_flip(x, m & 4), _flip(y, m & 2), _flip(c, m & 1))
        peer_idx = 4 * peer[0] + 2 * peer[1] + peer[2]
        sems = dict(send_sem=send_sems.at[a, m - 1], recv_sem=recv_sems.at[a, m - 1], device_id=peer,
                    device_id_type=MESH)
        send = pltpu.make_async_remote_copy(src_ref=ins[a].at[peer_idx], dst_ref=outs[a].at[me_idx], **sems)
        recv = pltpu.make_async_remote_copy(src_ref=ins[a].at[peer_idx], dst_ref=outs[a].at[peer_idx], **sems)
        return send, recv

    def start():
        for a in range(n):
            local(a).start()
            for m in range(1, N_DEV):
                copies(a, m)[0].start()

    def finish():
        for a in range(n):
            for m in range(1, N_DEV):
                copies(a, m)[1].wait_recv()
        for a in range(n):
            for m in range(1, N_DEV):
                copies(a, m)[0].wait_send()
            local(a).wait()

    return start, finish


def _collective_call(protocol, arrs, out_shapes, name):
    n = len(arrs)

    def body(*refs):
        start, finish = protocol(refs[:n], refs[n:2 * n], *refs[2 * n:])
        start()
        finish()

    any_spec = pl.BlockSpec(memory_space=pl.ANY)
    return pl.pallas_call(body, name=name, out_shape=out_shapes, in_specs=[any_spec] * n, out_specs=[any_spec] * n,
                          scratch_shapes=_comm_scratch(n))(*arrs)


def _gathered_shapes(arrs):
    return [jax.ShapeDtypeStruct((N_DEV,) + a.shape, a.dtype) for a in arrs]


def _all_gather(arrs, name):
    return _collective_call(_gather_protocol, arrs, _gathered_shapes(arrs), name)


MM_BLOCK_BYTES = 4 * 1024 * 1024


def _mm_tiles(m_dim, n_dim, k_dim, a_bytes, b_bytes):
    tm = _pick(m_dim, (1024, 512, 256, 128))
    tn = _pick(n_dim, (1024, 512, 256, 128))
    if (m_dim % (2 * tm) == 0 and 2 * tm * k_dim * a_bytes <= MM_BLOCK_BYTES
            and 2 * tm * tn * 4 <= MM_BLOCK_BYTES):
        tm *= 2
    tk = k_dim
    if k_dim % LANES == 0:
        units = k_dim // LANES
        fits = [u for u in range(1, units + 1) if units % u == 0
                and u * LANES * max(tm * a_bytes, tn * b_bytes) <= MM_BLOCK_BYTES]
        tk = max(fits) * LANES
    return tm, tn, tk


def _mm(a, b, *, ta=False, tb=False, name, side=None):
    (k_dim, m_dim) = a.shape if ta else a.shape[::-1]
    (n_dim, kb_dim) = b.shape if tb else b.shape[::-1]
    assert k_dim == kb_dim, (a.shape, b.shape, ta, tb)
    tm, tn, tk = _mm_tiles(m_dim, n_dim, k_dim, a.dtype.itemsize, b.dtype.itemsize)
    nk = k_dim // tk
    grid = (m_dim // tm, n_dim // tn, nk)
    dims = (((0 if ta else 1,), (1 if tb else 0,)), ((), ()))
    ns = side.n if side else 0

    def body(a_ref, b_ref, *rest):
        if side:
            side.run(rest[:ns], rest[ns + 1:2 * ns + 1], rest[2 * ns + 1:], *_grid_ends(grid),
                     lambda: compute(a_ref, b_ref, rest[ns]))
        else:
            compute(a_ref, b_ref, rest[0])

    def compute(a_ref, b_ref, o_ref):
        part = lax.dot_general(a_ref[...].astype(bf16), b_ref[...].astype(bf16), dims, preferred_element_type=f32)
        if nk == 1:
            o_ref[...] = part
        else:
            k = pl.program_id(2)

            @pl.when(k == 0)
            def _():
                o_ref[...] = part

            @pl.when(k > 0)
            def _():
                o_ref[...] += part

    a_spec = pl.BlockSpec((tk, tm), lambda i, j, k: (k, i)) if ta else pl.BlockSpec((tm, tk), lambda i, j, k: (i, k))
    b_spec = pl.BlockSpec((tn, tk), lambda i, j, k: (j, k)) if tb else pl.BlockSpec((tk, tn), lambda i, j, k: (k, j))
    out_spec = pl.BlockSpec((tm, tn), lambda i, j, k: (i, j))
    out_shape = jax.ShapeDtypeStruct((m_dim, n_dim), f32)
    if not side:
        return pl.pallas_call(body, name=name, grid=grid, in_specs=[a_spec, b_spec], out_specs=out_spec,
                              out_shape=out_shape,
                              compiler_params=_cparams(("parallel", "parallel", "arbitrary")))(a, b)
    return pl.pallas_call(
        body, name=name, grid=grid, in_specs=[a_spec, b_spec] + side.specs(), out_specs=[out_spec] + side.specs(),
        out_shape=[out_shape] + side.out_shapes, scratch_shapes=_comm_scratch(ns),
        compiler_params=_cparams(("arbitrary", "arbitrary", "arbitrary")))(a, b, *side.arrs)


def _win(t):
    return t if isinstance(t, tuple) else (t, t.shape[1], 0)


def _tile_spec(width, cb, tm):
    return pl.BlockSpec((tm, width), lambda i: (i, cb))


def _param_spec(p):
    return pl.BlockSpec(p.shape, lambda i: (0, 0))


def _stage_fwd(f, params, tiles, out_dtypes, name):
    tiles = [_win(t) for t in tiles]
    rows = tiles[0][0].shape[0]
    tm = min(TM, rows)
    avals = jax.eval_shape(f, *[jax.ShapeDtypeStruct(p.shape, f32) for p in params],
                           *[jax.ShapeDtypeStruct((tm, w), f32) for _, w, _ in tiles])
    n_p, n_t = len(params), len(tiles)

    def body(*refs):
        p = [r[...] for r in refs[:n_p]]
        t = [r[...].astype(f32) for r in refs[n_p:n_p + n_t]]
        for o_ref, val in zip(refs[n_p + n_t:], f(*p, *t)):
            o_ref[...] = val.astype(o_ref.dtype)

    return pl.pallas_call(
        body, name=name, grid=(rows // tm,),
        in_specs=[_param_spec(p) for p in params] + [_tile_spec(w, cb, tm) for _, w, cb in tiles],
        out_specs=[_tile_spec(a.shape[1], 0, tm) for a in avals],
        out_shape=[jax.ShapeDtypeStruct((rows, a.shape[1]), dt) for a, dt in zip(avals, out_dtypes)],
        compiler_params=_cparams(("parallel",)),
    )(*params, *[t[0] for t in tiles])


def _stage_bwd(f, params, tiles, cts, grad_dtypes, name, residual=None):
    tiles = [_win(t) for t in tiles]
    rows = tiles[0][0].shape[0]
    tm = min(TM, rows)
    cts = [list(g) if isinstance(g, (list, tuple)) else [g] for g in cts]
    flat_cts = [a for g in cts for a in g]
    n_p, n_t, n_c = len(params), len(tiles), len(flat_cts)
    has_res = residual is not None
    want = [j for j, dt in enumerate(grad_dtypes) if dt is not None]

    def body(*refs):
        i = pl.program_id(0)
        p = [r[...] for r in refs[:n_p]]
        t = [r[...].astype(f32) for r in refs[n_p:n_p + n_t]]
        ct_vals = [r[...].astype(f32) for r in refs[n_p + n_t:n_p + n_t + n_c]]
        ct, at = [], 0
        for g in cts:
            ct.append(functools.reduce(jnp.add, ct_vals[at:at + len(g)]))
            at += len(g)
        ct = tuple(ct)
        pos = n_p + n_t + n_c
        res_ref = refs[pos] if has_res else None
        pos += 1 if has_res else 0
        dp_refs = refs[pos:pos + n_p]
        dt_refs = refs[pos + n_p:]
        _, vjp = jax.vjp(f, *p, *t)
        grads = vjp(ct)

        @pl.when(i == 0)
        def _():
            for r in dp_refs:
                r[...] = jnp.zeros_like(r)

        for r, g in zip(dp_refs, grads[:n_p]):
            r[...] += g
        for r, j in zip(dt_refs, want):
            g = grads[n_p + j]
            if has_res and j == residual[0]:
                g = g + res_ref[...].astype(f32)
            r[...] = g.astype(r.dtype)

    in_arrays = list(params) + [t[0] for t in tiles] + flat_cts
    in_specs = ([_param_spec(p) for p in params] + [_tile_spec(w, cb, tm) for _, w, cb in tiles]
                + [_tile_spec(c.shape[1], 0, tm) for c in flat_cts])
    if has_res:
        in_arrays.append(residual[1])
        in_specs.append(_tile_spec(residual[1].shape[1], 0, tm))
    out_shape = ([jax.ShapeDtypeStruct(p.shape, f32) for p in params]
                 + [jax.ShapeDtypeStruct((rows, tiles[j][1]), grad_dtypes[j]) for j in want])
    out_specs = [_param_spec(p) for p in params] + [_tile_spec(tiles[j][1], 0, tm) for j in want]
    outs = pl.pallas_call(
        body, name=name, grid=(rows // tm,), in_specs=in_specs, out_specs=out_specs, out_shape=out_shape,
        compiler_params=_cparams(("arbitrary",)),
    )(*in_arrays)
    return outs[:n_p], outs[n_p:]


def _rms(x, w):
    return x * lax.rsqrt(jnp.mean(x * x, axis=-1, keepdims=True) + NORM_EPS) * w


def _f_normmod(w, shift, scale, x):
    return (_rms(x, w) * (1.0 + scale) + shift,)


def _f_merge(ga, gb, pa, pb):
    return (jax.nn.sigmoid(ga) * pa + jax.nn.sigmoid(gb) * pb,)


def _f_residual_normmod(gate, w, shift, scale, x, branch):
    x1 = x + gate * branch
    return x1, _rms(x1, w) * (1.0 + scale) + shift


def _f_loss(gate, wf, x1, fo, target):
    y = _rms(x1 + gate * fo, wf)
    err = jnp.square(y - target)
    return (0.5 * jnp.sum(jnp.mean(err, axis=-1, keepdims=True), axis=0, keepdims=True),)


def _loss_and_grads(gate2, wf, x1, fo, target):
    rows, d = x1.shape
    tm = min(TM, rows)

    def body(g_ref, w_ref, x_ref, fo_ref, t_ref, loss_ref, dg_ref, dw_ref, dx_ref, dfo_ref):
        i = pl.program_id(0)
        (val,), vjp = jax.vjp(_f_loss, g_ref[...], w_ref[...], x_ref[...], fo_ref[...], t_ref[...])
        dg, dw, dx, dfo, _ = vjp((jnp.ones((1, 1), f32),))

        @pl.when(i == 0)
        def _():
            loss_ref[...] = jnp.zeros_like(loss_ref)
            dg_ref[...] = jnp.zeros_like(dg_ref)
            dw_ref[...] = jnp.zeros_like(dw_ref)

        loss_ref[...] += jnp.broadcast_to(val, loss_ref.shape)
        dg_ref[...] += dg
        dw_ref[...] += dw
        dx_ref[...] = dx
        dfo_ref[...] = dfo.astype(bf16)

    vec = pl.BlockSpec((1, d), lambda i: (0, 0))
    tile = pl.BlockSpec((tm, d), lambda i: (i, 0))
    return pl.pallas_call(
        body, name="loss_fwd_bwd", grid=(rows // tm,),
        in_specs=[vec, vec, tile, tile, tile],
        out_specs=[pl.BlockSpec((1, LANES), lambda i: (0, 0)), vec, vec, tile, tile],
        out_shape=[jax.ShapeDtypeStruct((1, LANES), f32), jax.ShapeDtypeStruct((1, d), f32),
                   jax.ShapeDtypeStruct((1, d), f32), jax.ShapeDtypeStruct((rows, d), f32),
                   jax.ShapeDtypeStruct((rows, d), bf16)],
        compiler_params=_cparams(("arbitrary",)),
    )(gate2, wf, x1, fo, target)


def _softplus(z):
    return jnp.maximum(z, 0.0) + jnp.log(1.0 + jnp.exp(-jnp.abs(z)))


def _split_dot(a, m):
    hi = a.astype(bf16)
    lo = (a - hi.astype(f32)).astype(bf16)
    return jnp.dot(hi, m, preferred_element_type=f32) + jnp.dot(lo, m, preferred_element_type=f32)


def _suffix_matrix(n):
    r = lax.broadcasted_iota(jnp.int32, (n, n), 0)
    c = lax.broadcasted_iota(jnp.int32, (n, n), 1)
    return (r > c).astype(bf16)


def _head_masks():
    lane = lax.broadcasted_iota(jnp.int32, (1, LANES), 1)
    return [(lane < SB_HEAD_DIM).astype(f32), (lane >= SB_HEAD_DIM).astype(f32)]


def _sb_prepare(proj):
    def f(k, v):
        lane = lax.broadcasted_iota(jnp.int32, (1, SB_WIDTH), 1)
        m0 = (jnp.bitwise_and(lane, LANES - 1) < SB_HEAD_DIM).astype(f32)
        m1 = 1.0 - m0
        return k, k * m0, k * m1, v, v * m0, v * m1

    wins = [(proj, SB_WIDTH, OFF_SBK // SB_WIDTH), (proj, SB_WIDTH, OFF_SBV // SB_WIDTH)]
    return _stage_fwd(f, [], wins, [bf16] * 6, "sb_prepare")


def _stack_heads(x):
    m0, m1 = _head_masks()
    return jnp.concatenate([x * m0, x * m1], axis=0)


def _sb_logits(qst, k, t_pos2, kb, bq, masked):
    z = lax.dot_general(qst, k, (((1,), (1,)), ((), ())), preferred_element_type=f32)
    l = -_softplus(z)
    if masked:
        s_pos = kb * bq + lax.broadcasted_iota(jnp.int32, (1, bq), 1)
        causal = s_pos < t_pos2
        l = jnp.where(causal, l, 0.0)
    else:
        causal = None
    return z, l, causal


class _SideComm:
    def __init__(self, protocol, arrs, out_shapes):
        self.protocol, self.arrs, self.out_shapes = protocol, list(arrs), list(out_shapes)
        self.n = len(self.arrs)

    def specs(self):
        return [pl.BlockSpec(memory_space=pl.ANY)] * self.n

    def run(self, in_refs, out_refs, sems, first, last, compute):
        start, finish = self.protocol(in_refs, out_refs, *sems)
        pl.when(first)(start)
        compute()
        pl.when(last)(finish)


def _grid_ends(grid):
    ids = [pl.program_id(axis) for axis in range(len(grid))]
    first = functools.reduce(jnp.logical_and, [i == 0 for i in ids])
    last = functools.reduce(jnp.logical_and, [i == g - 1 for i, g in zip(ids, grid)])
    return first, last


def _sb_attention_fwd2(proj, k16, v0_16, v1_16, side=None):
    rows = proj.shape[0]
    bq = SB_QBLOCK
    nq = rows // bq
    assert nq <= LANES, "one lane per key block"
    npair = SB_WIDTH // LANES
    scale = SB_HEAD_DIM ** -0.5

    npp = SB_PAIRS_FWD
    wq = npp * LANES
    grid = (npair // npp, nq)
    ns = side.n if side else 0

    def body(q_ref, k_ref, v0_ref, v1_ref, *rest):
        o_ref, runs_ref = rest[ns], rest[ns + 1]
        if side:
            side.run(rest[:ns], rest[ns + 2:2 * ns + 2], rest[2 * ns + 2:], *_grid_ends(grid),
                     lambda: compute(q_ref, k_ref, v0_ref, v1_ref, o_ref, runs_ref))
        else:
            compute(q_ref, k_ref, v0_ref, v1_ref, o_ref, runs_ref)

    def compute(q_ref, k_ref, v0_ref, v1_ref, o_ref, runs_ref):
        qi = pl.program_id(1)
        pairs = [slice(pp * LANES, (pp + 1) * LANES) for pp in range(npp)]
        qst = [(_stack_heads(q_ref[:, s]) * scale).astype(bf16) for s in pairs]
        r = lax.broadcasted_iota(jnp.int32, (bq, 2 * bq), 0)
        c = lax.broadcasted_iota(jnp.int32, (bq, 2 * bq), 1)
        m2 = jnp.logical_or(r > c, c >= bq).astype(bf16)
        t_pos = qi * bq + lax.broadcasted_iota(jnp.int32, (bq, 1), 0)
        t_pos2 = jnp.concatenate([t_pos, t_pos], axis=0)
        lane = lax.broadcasted_iota(jnp.int32, (1, LANES), 1)
        runs_ref[...] = jnp.full(runs_ref.shape, SB_NEVER, f32)

        def tiles(kbs, carry, masked):
            jobs = [(pp, kb) for kb in kbs for pp in range(npp)]
            rows_k = [pl.ds(pl.multiple_of(kb * bq, bq), bq) for _, kb in jobs]
            zl = [_sb_logits(qst[pp], k_ref[rk, pairs[pp]], t_pos2, kb, bq, masked) for (pp, kb), rk in zip(jobs, rows_k)]
            cs = [_split_dot(l, m2) for _, l, _ in zl]
            run = [cr[0] for cr in carry]
            acc = [cr[1] for cr in carry]
            probs = []
            for (pp, kb), (z, l, causal), cs2 in zip(jobs, zl, cs):
                a = jnp.exp(z + l + cs2[:, :bq] + run[pp])
                if masked:
                    a = jnp.where(causal, a, 0.0)
                probs.append(a.astype(bf16))
                for hh in range(2):
                    cols = slice((2 * pp + hh) * LANES, (2 * pp + hh + 1) * LANES)
                    runs_ref[:, cols] = jnp.where(lane == kb, run[pp][hh * bq:(hh + 1) * bq], runs_ref[:, cols])
                run[pp] = run[pp] + cs2[:, bq:]
            for (pp, kb), rk, ab in zip(jobs, rows_k, probs):
                acc[pp] = (acc[pp] + jnp.dot(ab[:bq], v0_ref[rk, pairs[pp]], preferred_element_type=f32)
                           + jnp.dot(ab[bq:], v1_ref[rk, pairs[pp]], preferred_element_type=f32))
            return tuple(zip(run, acc))

        zero = (jnp.zeros((2 * bq, bq), f32), jnp.zeros((bq, LANES), f32))
        carry = tiles([qi], (zero,) * npp, True)

        def alive(cr):
            return functools.reduce(jnp.maximum, [jnp.max(run) for run, _ in cr]) > SB_DEAD

        def two(state):
            i, _, cr = state
            cr = tiles([qi - 1 - 2 * i, qi - 2 - 2 * i], cr, False)
            return i + 1, alive(cr), cr

        n_two = qi // 2
        i_end, still, carry = lax.while_loop(lambda st: jnp.logical_and(st[0] < n_two, st[1]), two,
                                             (jnp.int32(0), alive(carry), carry))
        last_one = jnp.logical_and(qi % 2 == 1, jnp.logical_and(still, i_end == n_two))
        carry = lax.cond(last_one, lambda cr: tiles([0], cr, False), lambda cr: cr, carry)
        for pp in range(npp):
            o_ref[:, pairs[pp]] = carry[pp][1]

    kv = pl.BlockSpec((rows, wq), lambda p, i: (0, p))
    return pl.pallas_call(
        body, name="sb_attn_fwd", grid=grid,
        in_specs=[pl.BlockSpec((bq, wq), lambda p, i: (i, OFF_SBQ // wq + p)), kv, kv, kv] + (side.specs() if side else []),
        out_specs=[pl.BlockSpec((bq, wq), lambda p, i: (i, p)),
                   pl.BlockSpec((bq, 2 * wq), lambda p, i: (i, p))] + (side.specs() if side else []),
        out_shape=[jax.ShapeDtypeStruct((rows, SB_WIDTH), f32),
                   jax.ShapeDtypeStruct((rows, SB_HEADS * LANES), f32)] + (side.out_shapes if side else []),
        scratch_shapes=_comm_scratch(ns) if side else [],
        compiler_params=_cparams(("arbitrary", "arbitrary")),
    )(proj, k16, v0_16, v1_16, *(side.arrs if side else []))


def _sb_attention_bwd2(proj, k16, k0_16, k1_16, v16, runs, do, side=None):
    rows = proj.shape[0]
    bq = SB_QBLOCK
    nq = rows // bq
    npair = SB_WIDTH // LANES
    scale = SB_HEAD_DIM ** -0.5
    tn = (((0,), (0,)), ((), ()))
    nt = (((1,), (1,)), ((), ()))

    npp = SB_PAIRS_BWD
    wq = npp * LANES
    grid = (npair // npp, nq)
    ns = side.n if side else 0

    def body(q_ref, k_ref, k0_ref, k1_ref, v_ref, runs_ref, do_ref, *rest):
        outs = rest[ns:ns + 3]
        ins = (q_ref, k_ref, k0_ref, k1_ref, v_ref, runs_ref, do_ref)
        if side:
            side.run(rest[:ns], rest[ns + 3:2 * ns + 3], rest[2 * ns + 3:], *_grid_ends(grid),
                     lambda: compute(*ins, *outs))
        else:
            compute(*ins, *outs)

    def compute(q_ref, k_ref, k0_ref, k1_ref, v_ref, runs_ref, do_ref, dq_ref, dk_ref, dv_ref):
        qi = pl.program_id(1)

        @pl.when(qi == 0)
        def _():
            dk_ref[...] = jnp.zeros_like(dk_ref)
            dv_ref[...] = jnp.zeros_like(dv_ref)

        pairs = [slice(pp * LANES, (pp + 1) * LANES) for pp in range(npp)]
        qst = [(_stack_heads(q_ref[:, s]) * scale).astype(bf16) for s in pairs]
        dost = [_stack_heads(do_ref[:, s]).astype(bf16) for s in pairs]
        runs = [jnp.concatenate([runs_ref[:, 2 * pp * LANES:(2 * pp + 1) * LANES],
                                 runs_ref[:, (2 * pp + 1) * LANES:(2 * pp + 2) * LANES]], axis=0) for pp in range(npp)]
        r = lax.broadcasted_iota(jnp.int32, (bq, 2 * bq), 0)
        c = lax.broadcasted_iota(jnp.int32, (bq, 2 * bq), 1)
        suffix_m = _suffix_matrix(bq)
        m2 = jnp.logical_or(r < c, c >= bq).astype(bf16)
        t_pos = qi * bq + lax.broadcasted_iota(jnp.int32, (bq, 1), 0)
        t_pos2 = jnp.concatenate([t_pos, t_pos], axis=0)
        lane = lax.broadcasted_iota(jnp.int32, (1, LANES), 1)

        def tiles(kbs, carry, masked):
            jobs = [(pp, kb) for kb in kbs for pp in range(npp)]
            rows_k = [pl.ds(pl.multiple_of(kb * bq, bq), bq) for _, kb in jobs]
            zl = [_sb_logits(qst[pp], k_ref[rk, pairs[pp]], t_pos2, kb, bq, masked) for (pp, kb), rk in zip(jobs, rows_k)]
            das = [lax.dot_general(dost[pp], v_ref[rk, pairs[pp]], nt, preferred_element_type=f32)
                   for (pp, kb), rk in zip(jobs, rows_k)]
            sticks = [_split_dot(l, suffix_m) for _, l, _ in zl]
            probs, ps = [], []
            for (pp, kb), (z, l, causal), stick, da in zip(jobs, zl, sticks, das):
                run = jnp.sum(jnp.where(lane == kb, runs[pp], 0.0), axis=1, keepdims=True)
                a = jnp.exp(z + l + stick + run)
                if masked:
                    a = jnp.where(causal, a, 0.0)
                probs.append(a.astype(bf16))
                ps.append(da * a)
            pcs = [_split_dot(p, m2) for p in ps]
            pref = [cr[0] for cr in carry]
            dq_acc = [cr[1] for cr in carry]
            dzs = []
            for (pp, kb), (z, l, causal), p, pc2 in zip(jobs, zl, ps, pcs):
                dz = p * jnp.exp(l) - jnp.exp(z + l) * (pc2[:, :bq] + pref[pp])
                if masked:
                    dz = jnp.where(causal, dz, 0.0)
                dzs.append(dz.astype(bf16))
                pref[pp] = pref[pp] + pc2[:, bq:]
            for (pp, kb), rk, dzb, ab in zip(jobs, rows_k, dzs, probs):
                cols = pairs[pp]
                dq_acc[pp] = (dq_acc[pp] + jnp.dot(dzb[:bq], k0_ref[rk, cols], preferred_element_type=f32)
                              + jnp.dot(dzb[bq:], k1_ref[rk, cols], preferred_element_type=f32))
                dk_ref[rk, cols] += lax.dot_general(dzb, qst[pp], tn, preferred_element_type=f32)
                dv_ref[rk, cols] += lax.dot_general(ab, dost[pp], tn, preferred_element_type=f32)
            return tuple(zip(pref, dq_acc))

        zero = (jnp.zeros((2 * bq, bq), f32), jnp.zeros((bq, LANES), f32))
        colmax = functools.reduce(jnp.maximum, [jnp.max(x, axis=0, keepdims=True) for x in runs])
        live = jnp.logical_and(colmax > SB_DEAD, lane < qi)
        kb0 = jnp.minimum(jnp.min(jnp.where(live, lane, LANES)), qi)
        n_blocks = qi - kb0
        carry = lax.fori_loop(0, n_blocks // 2, lambda i, cr: tiles([kb0 + 2 * i, kb0 + 2 * i + 1], cr, False),
                              (zero,) * npp)
        carry = lax.cond(n_blocks % 2 == 1, lambda cr: tiles([qi - 1], cr, False), lambda cr: cr, carry)
        carry = tiles([qi], carry, True)
        for pp in range(npp):
            dq_ref[:, pairs[pp]] = (carry[pp][1] * scale).astype(dq_ref.dtype)

    blk = pl.BlockSpec((bq, wq), lambda p, i: (i, p))
    full = pl.BlockSpec((rows, wq), lambda p, i: (0, p), pipeline_mode=pl.Buffered(1))
    return pl.pallas_call(
        body, name="sb_attn_bwd", grid=grid,
        in_specs=[pl.BlockSpec((bq, wq), lambda p, i: (i, OFF_SBQ // wq + p)), full, full, full, full,
                  pl.BlockSpec((bq, 2 * wq), lambda p, i: (i, p)), blk] + (side.specs() if side else []),
        out_specs=[blk, full, full] + (side.specs() if side else []),
        out_shape=[jax.ShapeDtypeStruct((rows, SB_WIDTH), bf16), jax.ShapeDtypeStruct((rows, SB_WIDTH), f32),
                   jax.ShapeDtypeStruct((rows, SB_WIDTH), f32)] + (side.out_shapes if side else []),
        scratch_shapes=_comm_scratch(ns) if side else [],
        compiler_params=_cparams(("arbitrary", "arbitrary")),
    )(proj, k16, k0_16, k1_16, v16, runs, do, *(side.arrs if side else []))


def _shift_down(x, prev8, j):
    if j == 0:
        return x
    r = pltpu.roll(x, j, axis=0)
    row8 = lax.broadcasted_iota(jnp.int32, prev8.shape, 0)
    head = jnp.where(row8 < j, pltpu.roll(prev8, j, axis=0), r[0:SUBLANES])
    return jnp.concatenate([head, r[SUBLANES:]], axis=0)


def _shift_up(x, next8, j):
    if j == 0:
        return x
    n = x.shape[0]
    r = pltpu.roll(x, n - j, axis=0)
    row8 = lax.broadcasted_iota(jnp.int32, next8.shape, 0)
    tail = jnp.where(row8 >= SUBLANES - j, pltpu.roll(next8, SUBLANES - j, axis=0), r[n - SUBLANES:n])
    return jnp.concatenate([r[:n - SUBLANES], tail], axis=0)


def _conv(x, prev8, w):
    k_taps = w.shape[0]
    out = x * w[k_taps - 1:k_taps, :]
    for j in range(1, k_taps):
        out = out + _shift_down(x, prev8, j) * w[k_taps - 1 - j:k_taps - j, :]
    return out


def _conv_tiles(rows, tr_max=TCONV_R):
    tr = min(tr_max, rows)
    return tr, rows // tr, tr // SUBLANES


def _prev_spec(tc, cb0, r8):
    return pl.BlockSpec((SUBLANES, tc), lambda j, i: (jnp.maximum(i * r8 - 1, 0), cb0 + j))


def _silu(x):
    return x * jax.nn.sigmoid(x)


def _dsilu(x):
    s = jax.nn.sigmoid(x)
    return s * (1.0 + x * (1.0 - s))


def _dn_conv_fwd(proj, w):
    rows = proj.shape[0]
    tr, nr, r8 = _conv_tiles(rows)
    tc = TCONV_C
    cb0 = OFF_DN // tc

    def body(x_ref, p_ref, w_ref, o_ref):
        prev = jnp.where(pl.program_id(1) == 0, 0.0, p_ref[...])
        o_ref[...] = _silu(_conv(x_ref[...], prev, w_ref[...]))

    return pl.pallas_call(
        body, name="dn_conv_fwd", grid=(DN_CONV_CH // tc, nr),
        in_specs=[pl.BlockSpec((tr, tc), lambda j, i: (i, cb0 + j)), _prev_spec(tc, cb0, r8),
                  pl.BlockSpec((DN_CONV_WIDTH, tc), lambda j, i: (0, j))],
        out_specs=pl.BlockSpec((tr, tc), lambda j, i: (i, j)),
        out_shape=jax.ShapeDtypeStruct((rows, DN_CONV_CH), f32),
        compiler_params=_cparams(("parallel", "parallel")),
    )(proj, proj, w)


def _dn_conv_bwd_act(proj, w, dact):
    rows = proj.shape[0]
    tr, nr, r8 = _conv_tiles(rows)
    tc = TCONV_C
    cb0 = OFF_DN // tc

    def body(x_ref, p_ref, w_ref, d_ref, o_ref):
        prev = jnp.where(pl.program_id(1) == 0, 0.0, p_ref[...])
        o_ref[...] = d_ref[...] * _dsilu(_conv(x_ref[...], prev, w_ref[...]))

    return pl.pallas_call(
        body, name="dn_conv_bwd_act", grid=(DN_CONV_CH // tc, nr),
        in_specs=[pl.BlockSpec((tr, tc), lambda j, i: (i, cb0 + j)), _prev_spec(tc, cb0, r8),
                  pl.BlockSpec((DN_CONV_WIDTH, tc), lambda j, i: (0, j)),
                  pl.BlockSpec((tr, tc), lambda j, i: (i, j))],
        out_specs=pl.BlockSpec((tr, tc), lambda j, i: (i, j)),
        out_shape=jax.ShapeDtypeStruct((rows, DN_CONV_CH), f32),
        compiler_params=_cparams(("parallel", "parallel")),
    )(proj, proj, w, dact)


def _ffn_conv_fwd(u_pre, w, b):
    rows = u_pre.shape[0]
    tr, nr, r8 = _conv_tiles(rows, TCONV_R // 2)
    tc = TCONV_FF
    nct = D_FF // tc

    def body(xg_ref, pg_ref, xu_ref, pu_ref, wg_ref, wu_ref, bg_ref, bu_ref, o_ref, u_ref):
        first = pl.program_id(1) == 0
        ug = _conv(xg_ref[...], jnp.where(first, 0.0, pg_ref[...]), wg_ref[...]) + bg_ref[...]
        uu = _conv(xu_ref[...], jnp.where(first, 0.0, pu_ref[...]), wu_ref[...]) + bu_ref[...]
        o_ref[...] = (_silu(ug) * uu).astype(o_ref.dtype)
        u_ref[0] = ug
        u_ref[1] = uu

    def x_spec(off):
        return pl.BlockSpec((tr, tc), lambda j, i: (i, off + j))

    def w_spec(k, off):
        return pl.BlockSpec((k, tc), lambda j, i: (0, off + j))

    return pl.pallas_call(
        body, name="ffn_conv_fwd", grid=(nct, nr),
        in_specs=[x_spec(0), _prev_spec(tc, 0, r8), x_spec(nct), _prev_spec(tc, nct, r8),
                  w_spec(FFN_CONV_WIDTH, 0), w_spec(FFN_CONV_WIDTH, nct), w_spec(1, 0), w_spec(1, nct)],
        out_specs=[pl.BlockSpec((tr, tc), lambda j, i: (i, j)), pl.BlockSpec((2, tr, tc), lambda j, i: (0, i, j))],
        out_shape=[jax.ShapeDtypeStruct((rows, D_FF), bf16), jax.ShapeDtypeStruct((2, rows, D_FF), f32)],
        compiler_params=_cparams(("parallel", "parallel")),
    )(u_pre, u_pre, u_pre, u_pre, w, w, b, b)


def _ffn_conv_bwd_act(u, dact):
    rows = dact.shape[0]
    tr, nr, _ = _conv_tiles(rows, TCONV_R // 2)
    tc = TCONV_FF
    nct = D_FF // tc

    def body(u_ref, d_ref, du_ref, dbg_ref, dbu_ref):
        first = pl.program_id(1) == 0
        ug = u_ref[0]
        uu = u_ref[1]
        d = d_ref[...]
        sig = jax.nn.sigmoid(ug)
        dug = d * uu * (sig * (1.0 + ug * (1.0 - sig)))
        duu = d * (ug * sig)
        du_ref[0] = dug
        du_ref[1] = duu

        @pl.when(first)
        def _():
            dbg_ref[...] = jnp.zeros_like(dbg_ref)
            dbu_ref[...] = jnp.zeros_like(dbu_ref)

        dbg_ref[...] += jnp.sum(dug, axis=0, keepdims=True)
        dbu_ref[...] += jnp.sum(duu, axis=0, keepdims=True)

    pair = pl.BlockSpec((2, tr, tc), lambda j, i: (0, i, j))
    vec = pl.BlockSpec((1, tc), lambda j, i: (0, j))
    return pl.pallas_call(
        body, name="ffn_conv_bwd_act", grid=(nct, nr),
        in_specs=[pair, pl.BlockSpec((tr, tc), lambda j, i: (i, j))],
        out_specs=[pair, vec, vec],
        out_shape=[jax.ShapeDtypeStruct((2, rows, D_FF), f32),
                   jax.ShapeDtypeStruct((1, D_FF), f32), jax.ShapeDtypeStruct((1, D_FF), f32)],
        compiler_params=_cparams(("parallel", "arbitrary")),
    )(u, dact)


def _conv_bwd(dy, x, x_cb0, w, name):
    k_taps = w.shape[0]
    split = dy.ndim == 3
    rows = dy.shape[-2]
    ch = dy.shape[-1] * (2 if split else 1)
    tc = TCONV_FF if split else TCONV_C
    tr, nr, r8 = _conv_tiles(rows, TCONV_R // 2 if split else TCONV_R)
    per_half = dy.shape[-1] // tc
    last8 = rows // SUBLANES - 1

    def body(dy_ref, nx_ref, x_ref, w_ref, dx_ref, dw_ref):
        i = pl.program_id(1)
        dyv = dy_ref[...]
        nxt = jnp.where(i == nr - 1, 0.0, nx_ref[...])
        xv = x_ref[...].astype(f32)
        wv = w_ref[...]

        @pl.when(i == 0)
        def _():
            dw_ref[...] = jnp.zeros_like(dw_ref)

        dx = dyv * wv[k_taps - 1:k_taps, :]
        dw_ref[k_taps - 1:k_taps, :] += jnp.sum(dyv * xv, axis=0, keepdims=True)
        for j in range(1, k_taps):
            dy_j = _shift_up(dyv, nxt, j)
            dx = dx + dy_j * wv[k_taps - 1 - j:k_taps - j, :]
            dw_ref[k_taps - 1 - j:k_taps - j, :] += jnp.sum(dy_j * xv, axis=0, keepdims=True)
        dx_ref[...] = dx.astype(dx_ref.dtype)

    tile = pl.BlockSpec((tr, tc), lambda j, i: (i, j))
    if split:
        dy_spec = pl.BlockSpec((None, tr, tc), lambda j, i: (j // per_half, i, j % per_half))
        next_spec = pl.BlockSpec((None, SUBLANES, tc),
                                 lambda j, i: (j // per_half, jnp.minimum((i + 1) * r8, last8), j % per_half))
    else:
        dy_spec = tile
        next_spec = pl.BlockSpec((SUBLANES, tc), lambda j, i: (jnp.minimum((i + 1) * r8, last8), j))
    return pl.pallas_call(
        body, name=name, grid=(ch // tc, nr),
        in_specs=[dy_spec, next_spec, pl.BlockSpec((tr, tc), lambda j, i: (i, x_cb0 + j)),
                  pl.BlockSpec((k_taps, tc), lambda j, i: (0, j))],
        out_specs=[tile, pl.BlockSpec((k_taps, tc), lambda j, i: (0, j))],
        out_shape=[jax.ShapeDtypeStruct((rows, ch), bf16), jax.ShapeDtypeStruct((k_taps, ch), f32)],
        compiler_params=_cparams(("parallel", "arbitrary")),
    )(dy, dy, x, w)


def _hdot(a, b):
    return jnp.dot(a, b, preferred_element_type=f32, precision=lax.Precision.HIGH)


def _xdot(a, b):
    return jnp.dot(a, b, preferred_element_type=f32, precision=lax.Precision.HIGHEST)


def _bdot(a, b):
    return jnp.dot(a.astype(bf16), b.astype(bf16), preferred_element_type=f32)


def _bdot_nt(a, b):
    return lax.dot_general(a.astype(bf16), b.astype(bf16), (((1,), (1,)), ((), ())), preferred_element_type=f32)


def _bdot_tn(a, b):
    return lax.dot_general(a.astype(bf16), b.astype(bf16), (((0,), (0,)), ((), ())), preferred_element_type=f32)


GDN_GROUP = 4
GDN_NGROUPS = DN_HEADS // GDN_GROUP
GDN_ROWS = GDN_GROUP * DN_CHUNK
GDN_QK_LANES = GDN_GROUP * DN_KEY_DIM
GDN_LOGIT_LANE = DN_HEADS


def _inverse_impl(lows):
    n = lows[0].shape[0]
    r = lax.broadcasted_iota(jnp.int32, (n, n), 0)
    c = lax.broadcasted_iota(jnp.int32, (n, n), 1)
    eye = (r == c).astype(f32)
    blk = jnp.right_shift(r, 3) == jnp.right_shift(c, 3)
    d = [jnp.where(blk, low, 0.0) for low in lows]
    e = [low - x for low, x in zip(lows, d)]

    def nilpotent8_inverse(xs):
        acc = [eye - x for x in xs]
        power = xs
        for _ in range(2):
            power = [_bdot(x, x) for x in power]
            acc = [_bdot(a, eye + x) for a, x in zip(acc, power)]
        return acc

    dinv = nilpotent8_inverse(d)
    ninv = nilpotent8_inverse([_bdot(x, y) for x, y in zip(dinv, e)])
    t = [_bdot(x, y) for x, y in zip(ninv, dinv)]
    for _ in range(2):
        res = [eye - x - _hdot(low, x) for low, x in zip(lows, t)]
        t = [x + _bdot(x, y) for x, y in zip(t, res)]
    return tuple(t)


@jax.custom_vjp
def _unit_lower_inverses(lows):
    return _inverse_impl(lows)


def _unit_lower_inverses_fwd(lows):
    t = _inverse_impl(lows)
    return t, t


def _unit_lower_inverses_bwd(t, ct):
    tn = (((0,), (0,)), ((), ()))
    nt = (((1,), (1,)), ((), ()))
    left = [lax.dot_general(x, g, tn, preferred_element_type=f32, precision=lax.Precision.HIGH) for x, g in zip(t, ct)]
    return (tuple(-lax.dot_general(x, y, nt, preferred_element_type=f32, precision=lax.Precision.HIGH)
                  for x, y in zip(left, t)),)


_unit_lower_inverses.defvjp(_unit_lower_inverses_fwd, _unit_lower_inverses_bwd)


@jax.custom_vjp
def _known_inverses(lows, t):
    return t


def _known_inverses_fwd(lows, t):
    return t, t


def _known_inverses_bwd(t, ct):
    return _unit_lower_inverses_bwd(t, ct) + (tuple(jnp.zeros_like(x) for x in t),)


_known_inverses.defvjp(_known_inverses_fwd, _known_inverses_bwd)


def _gdn_chunk(a_log, dt_bias, norm_w, ba, *per_group, inverses=None, keep_inverses=False):
    ng = GDN_NGROUPS
    qgs, kgs, vsts, zsts, states = [per_group[i * ng:(i + 1) * ng] for i in range(5)]
    groups = range(ng)
    n = GDN_ROWS
    r = lax.broadcasted_iota(jnp.int32, (n, n), 0)
    c = lax.broadcasted_iota(jnp.int32, (n, n), 1)
    same_head = jnp.right_shift(r, 6) == jnp.right_shift(c, 6)
    incl = jnp.logical_and(same_head, r >= c)
    strict = jnp.logical_and(same_head, r > c)
    eye = (r == c).astype(f32)
    ones = jnp.ones((n, n), f32)
    own_lanes = same_head.astype(f32)
    lane = lax.broadcasted_iota(jnp.int32, (1, LANES), 1)
    pick = lambda arr, idx: jnp.sum(jnp.where(lane == idx, arr, 0.0), axis=1, keepdims=True)
    heads = [[GDN_GROUP * g + h for h in range(GDN_GROUP)] for g in groups]
    rc = lax.broadcasted_iota(jnp.int32, (DN_CHUNK, DN_CHUNK), 0)
    cc = lax.broadcasted_iota(jnp.int32, (DN_CHUNK, DN_CHUNK), 1)

    g_all = -jnp.exp(a_log) * _softplus(ba + dt_bias)
    gc_all = _xdot((rc >= cc).astype(f32), g_all)
    gl_all = jnp.sum(g_all, axis=0, keepdims=True)
    beta = [jnp.concatenate([jax.nn.sigmoid(pick(ba, hd)) for hd in heads[g]], axis=0) for g in groups]
    gc = [jnp.concatenate([pick(gc_all, GDN_LOGIT_LANE + hd) for hd in heads[g]], axis=0) for g in groups]
    g_last = [jnp.concatenate([jnp.broadcast_to(pick(gl_all, GDN_LOGIT_LANE + hd), (DN_CHUNK, 1)) for hd in heads[g]],
                              axis=0) for g in groups]
    gr = [jnp.broadcast_to(gc[g], (n, n)).T for g in groups]
    decay = [jnp.where(incl, jnp.exp(jnp.where(incl, gc[g] - gr[g], 0.0)), 0.0) for g in groups]
    q = [jnp.concatenate([qgs[g]] * GDN_GROUP, axis=0) * own_lanes for g in groups]
    k = [jnp.concatenate([kgs[g]] * GDN_GROUP, axis=0) * own_lanes for g in groups]
    qn = [x * lax.rsqrt(jnp.sum(x * x, axis=1, keepdims=True) + L2_EPS) * (DN_KEY_DIM ** -0.5) for x in q]
    kn = [x * lax.rsqrt(jnp.sum(x * x, axis=1, keepdims=True) + L2_EPS) for x in k]
    kb = [kn[g] * beta[g] for g in groups]
    low = [jnp.where(strict, _bdot_nt(kb[g], kn[g]) * decay[g], 0.0) for g in groups]
    intra = [jnp.where(incl, _bdot_nt(qn[g], kn[g]) * decay[g], 0.0) for g in groups]
    t = _unit_lower_inverses(tuple(low)) if inverses is None else _known_inverses(tuple(low), tuple(inverses))
    u = [_bdot(t[g], vsts[g] * beta[g]) for g in groups]
    w = [_bdot(t[g], kb[g] * jnp.exp(gc[g])) for g in groups]
    sb = [s.astype(bf16) for s in states]
    v_new = [u[g] - jnp.dot(w[g].astype(bf16), sb[g], preferred_element_type=f32) for g in groups]
    o = [jnp.dot((qn[g] * jnp.exp(gc[g])).astype(bf16), sb[g], preferred_element_type=f32) for g in groups]
    o = [o[g] + _bdot(intra[g], v_new[g]) for g in groups]
    new_state = [states[g] * jnp.exp(g_last[g]) + _bdot_tn(kn[g] * jnp.exp(g_last[g] - gc[g]), v_new[g])
                 for g in groups]
    o_n = [x * lax.rsqrt(jnp.mean(x * x, axis=1, keepdims=True) + NORM_EPS) * norm_w for x in o]
    return tuple(o_n[g] * _silu(zsts[g]) for g in groups) + tuple(new_state) + (tuple(t) if keep_inverses else ())


def _gdn_specs(rows, reverse):
    n = rows // DN_CHUNK
    idx = (lambda i: n - 1 - i) if reverse else (lambda i: i)
    vec = pl.BlockSpec((1, LANES), lambda i: (0, 0))
    qkv = pl.BlockSpec((DN_CHUNK, DN_CONV_CH), lambda i: (idx(i), 0))
    z = pl.BlockSpec((DN_CHUNK, DN_V_WIDTH), lambda i: (idx(i), OFF_Z // DN_V_WIDTH))
    ba = pl.BlockSpec((DN_CHUNK, LANES), lambda i: (idx(i), 0))
    wide = pl.BlockSpec((DN_CHUNK, DN_V_WIDTH), lambda i: (idx(i), 0))
    st = pl.BlockSpec((1, DN_HEADS * DN_KEY_DIM, LANES), lambda i: (idx(i), 0, 0))
    inv = pl.BlockSpec((1, GDN_NGROUPS * GDN_ROWS, GDN_ROWS), lambda i: (idx(i), 0, 0))
    return n, vec, qkv, z, ba, wide, st, inv


def _gdn_slices(grp):
    q = slice(grp * GDN_QK_LANES, (grp + 1) * GDN_QK_LANES)
    k = slice(DN_QK_WIDTH + grp * GDN_QK_LANES, DN_QK_WIDTH + (grp + 1) * GDN_QK_LANES)
    heads = [slice((GDN_GROUP * grp + h) * LANES, (GDN_GROUP * grp + h + 1) * LANES) for h in range(GDN_GROUP)]
    vs = [slice(2 * DN_QK_WIDTH + s.start, 2 * DN_QK_WIDTH + s.stop) for s in heads]
    return q, k, vs, heads


def _stack_cols(ref, cols):
    return jnp.concatenate([ref[:, s] for s in cols], axis=0)


def _gdn_operands(qkv_ref, z_ref, state_rows):
    sl = [_gdn_slices(grp) for grp in range(GDN_NGROUPS)]
    return ([qkv_ref[:, q] for q, _, _, _ in sl] + [qkv_ref[:, k] for _, k, _, _ in sl]
            + [_stack_cols(qkv_ref, vs) for _, _, vs, _ in sl] + [_stack_cols(z_ref, heads) for _, _, _, heads in sl]
            + [state_rows[grp * GDN_ROWS:(grp + 1) * GDN_ROWS, :] for grp in range(GDN_NGROUPS)])


def _gdn_fwd(a_log, dt_bias, norm_w, qkv_act, proj, ba):
    rows = qkv_act.shape[0]
    n, vec, qkv_s, z_s, ba_s, wide, st_s, inv_s = _gdn_specs(rows, False)

    def body(al_ref, dt_ref, nw_ref, qkv_ref, z_ref, ba_ref, o_ref, st_ref, inv_ref, state):
        @pl.when(pl.program_id(0) == 0)
        def _():
            state[...] = jnp.zeros_like(state)

        st_ref[0] = state[...]
        out = _gdn_chunk(al_ref[...], dt_ref[...], nw_ref[...], ba_ref[...], *_gdn_operands(qkv_ref, z_ref, state),
                         keep_inverses=True)
        for grp in range(GDN_NGROUPS):
            _, _, _, heads = _gdn_slices(grp)
            rs = slice(grp * GDN_ROWS, (grp + 1) * GDN_ROWS)
            for h, s in enumerate(heads):
                o_ref[:, s] = out[grp][h * DN_CHUNK:(h + 1) * DN_CHUNK].astype(o_ref.dtype)
            state[rs, :] = out[GDN_NGROUPS + grp]
            inv_ref[0, rs, :] = out[2 * GDN_NGROUPS + grp]

    return pl.pallas_call(
        body, name="gdn_fwd", grid=(n,),
        in_specs=[vec, vec, vec, qkv_s, z_s, ba_s], out_specs=[wide, st_s, inv_s],
        out_shape=[jax.ShapeDtypeStruct((rows, DN_V_WIDTH), bf16),
                   jax.ShapeDtypeStruct((n, DN_HEADS * DN_KEY_DIM, LANES), f32),
                   jax.ShapeDtypeStruct((n, GDN_NGROUPS * GDN_ROWS, GDN_ROWS), f32)],
        scratch_shapes=[pltpu.VMEM((DN_HEADS * DN_KEY_DIM, LANES), f32)],
        compiler_params=_cparams(("arbitrary",)),
    )(a_log, dt_bias, norm_w, qkv_act, proj, ba)


def _gdn_bwd(a_log, dt_bias, norm_w, qkv_act, proj, ba, states, inverses, do):
    rows = qkv_act.shape[0]
    n, vec, qkv_s, z_s, ba_s, wide, st_s, inv_s = _gdn_specs(rows, True)

    def body(al_ref, dt_ref, nw_ref, qkv_ref, z_ref, ba_ref, st_ref, inv_ref, do_ref,
             dal_ref, ddt_ref, dnw_ref, dqkv_ref, dz_ref, dba_ref, dstate):
        @pl.when(pl.program_id(0) == 0)
        def _():
            dstate[...] = jnp.zeros_like(dstate)
            dal_ref[...] = jnp.zeros_like(dal_ref)
            ddt_ref[...] = jnp.zeros_like(ddt_ref)
            dnw_ref[...] = jnp.zeros_like(dnw_ref)

        ng = GDN_NGROUPS
        kept = [inv_ref[0, grp * GDN_ROWS:(grp + 1) * GDN_ROWS, :] for grp in range(ng)]
        _, vjp = jax.vjp(functools.partial(_gdn_chunk, inverses=kept), al_ref[...], dt_ref[...], nw_ref[...],
                         ba_ref[...], *_gdn_operands(qkv_ref, z_ref, st_ref[0]))
        cts = tuple(_stack_cols(do_ref, _gdn_slices(grp)[3]) for grp in range(ng))
        cts += tuple(dstate[grp * GDN_ROWS:(grp + 1) * GDN_ROWS, :] for grp in range(ng))
        grads = vjp(cts)
        dal_ref[...] += grads[0]
        ddt_ref[...] += grads[1]
        dnw_ref[...] += grads[2]
        dba_ref[...] = grads[3]
        dqs, dks, dvs, dzs, dss = [grads[4 + i * ng:4 + (i + 1) * ng] for i in range(5)]
        for grp in range(ng):
            q, k, vs, heads = _gdn_slices(grp)
            dqkv_ref[:, q] = dqs[grp]
            dqkv_ref[:, k] = dks[grp]
            for h, (sv, sh) in enumerate(zip(vs, heads)):
                rows_h = slice(h * DN_CHUNK, (h + 1) * DN_CHUNK)
                dqkv_ref[:, sv] = dvs[grp][rows_h]
                dz_ref[:, sh] = dzs[grp][rows_h].astype(dz_ref.dtype)
            dstate[grp * GDN_ROWS:(grp + 1) * GDN_ROWS, :] = dss[grp]

    return pl.pallas_call(
        body, name="gdn_bwd", grid=(n,),
        in_specs=[vec, vec, vec, qkv_s, z_s, ba_s, st_s, inv_s, wide],
        out_specs=[vec, vec, vec, qkv_s, wide, ba_s],
        out_shape=[jax.ShapeDtypeStruct((1, LANES), f32)] * 3
        + [jax.ShapeDtypeStruct((rows, DN_CONV_CH), f32), jax.ShapeDtypeStruct((rows, DN_V_WIDTH), bf16),
           jax.ShapeDtypeStruct((rows, LANES), f32)],
        scratch_shapes=[pltpu.VMEM((DN_HEADS * DN_KEY_DIM, LANES), f32)],
        compiler_params=_cparams(("arbitrary",)),
    )(a_log, dt_bias, norm_w, qkv_act, proj, ba, states, inverses, do)


def _ada_fwd(c_all, w_loc, b_loc):
    def body(c_ref, w_ref, b_ref, o_ref):
        o_ref[...] = _bdot(_silu(c_ref[...]), w_ref[...]) + b_ref[...]

    return pl.pallas_call(body, name="ada_fwd", out_shape=jax.ShapeDtypeStruct((c_all.shape[0], w_loc.shape[1]), f32),
                          compiler_params=_cparams())(c_all, w_loc, b_loc)


def _ada_bwd(c_all, dmod_cols):
    def body(c_ref, d_ref, o_ref):
        o_ref[...] = _bdot_tn(_silu(c_ref[...]), d_ref[...])

    return pl.pallas_call(body, name="ada_bwd",
                          out_shape=jax.ShapeDtypeStruct((c_all.shape[1], dmod_cols.shape[1]), f32),
                          compiler_params=_cparams())(c_all, dmod_cols)


def _sum_devices(parts):
    def body(p_ref, o_ref):
        acc = p_ref[0:1, :]
        for d in range(1, N_DEV):
            acc = acc + p_ref[d:d + 1, :]
        o_ref[...] = acc

    return pl.pallas_call(body, name="sum_small", out_shape=jax.ShapeDtypeStruct((1, parts.shape[1]), f32),
                          compiler_params=_cparams())(parts)


def _adam_math(w, g, m, v):
    m2 = ADAM_B1 * m + (1.0 - ADAM_B1) * g
    v2 = ADAM_B2 * v + (1.0 - ADAM_B2) * jnp.square(g)
    m_hat = m2 / (1.0 - ADAM_B1 ** ADAM_STEP)
    v_hat = v2 / (1.0 - ADAM_B2 ** ADAM_STEP)
    delta = -ADAM_LR * (m_hat / (jnp.sqrt(v_hat) + ADAM_EPS) + ADAM_WD * w)
    return delta, m2, v2


def _row_tile(rows):
    return _pick(rows, (256, 128, 64, 32, 16, 8))


def _adamw(w, g, m, v, name):
    rows, cols = w.shape
    tr = _row_tile(rows)

    def body(w_ref, g_ref, m_ref, v_ref, d_ref, m2_ref, v2_ref):
        d_ref[...], m2_ref[...], v2_ref[...] = _adam_math(w_ref[...], g_ref[...], m_ref[...], v_ref[...])

    tile = pl.BlockSpec((tr, cols), lambda i: (i, 0))
    return pl.pallas_call(body, name=name, grid=(rows // tr,), in_specs=[tile] * 4, out_specs=[tile] * 3,
                          out_shape=[jax.ShapeDtypeStruct(w.shape, f32)] * 3,
                          compiler_params=_cparams(("parallel",)))(w, g, m, v)


def _sum_adamw(parts, w, m, v, name):
    rows, cols = w.shape
    tr = _row_tile(rows)

    def body(p_ref, w_ref, m_ref, v_ref, g_ref, d_ref, m2_ref, v2_ref):
        g = p_ref[0].astype(f32)
        for d in range(1, N_DEV):
            g = g + p_ref[d].astype(f32)
        g_ref[...] = g
        d_ref[...], m2_ref[...], v2_ref[...] = _adam_math(w_ref[...], g, m_ref[...], v_ref[...])

    tile = pl.BlockSpec((tr, cols), lambda i: (i, 0))
    return pl.pallas_call(body, name=name, grid=(rows // tr,),
                          in_specs=[pl.BlockSpec((N_DEV, tr, cols), lambda i: (0, i, 0)), tile, tile, tile],
                          out_specs=[tile] * 4, out_shape=[jax.ShapeDtypeStruct(w.shape, f32)] * 4,
                          compiler_params=_cparams(("parallel",)))(parts, w, m, v)


def _pad_lanes(a, width):
    return jnp.pad(a, ((0, 0), (0, width - a.shape[1])))


def _cols_by_device(full):
    r, c = full.shape
    return jnp.moveaxis(full.reshape(r, N_DEV, c // N_DEV), 1, 0)


def _cols_from_devices(parts):
    d, r, n = parts.shape
    return jnp.moveaxis(parts, 0, 1).reshape(r, d * n)


def kernel(x, c, w_ada, b_ada, norm1_w, w_in, dn_conv_w, dn_A_log, dn_dt_bias, dn_norm_w, w_proj_sb, w_proj_dn, w_out, norm2_w, w_ffn_in, ffn_conv_w, ffn_conv_b, w_ffn_out, final_norm_w, loss_target, m_w_ada, m_b_ada, m_norm1_w, m_w_in, m_dn_conv_w, m_dn_A_log, m_dn_dt_bias, m_dn_norm_w, m_w_proj_sb, m_w_proj_dn, m_w_out, m_norm2_w, m_w_ffn_in, m_ffn_conv_w, m_ffn_conv_b, m_w_ffn_out, m_final_norm_w, v_w_ada, v_b_ada, v_norm1_w, v_w_in, v_dn_conv_w, v_dn_A_log, v_dn_dt_bias, v_dn_norm_w, v_w_proj_sb, v_w_proj_dn, v_w_out, v_norm2_w, v_w_ffn_in, v_ffn_conv_w, v_ffn_conv_b, v_w_ffn_out, v_final_norm_w):
    d = D_MODEL
    me = 4 * lax.axis_index("x") + 2 * lax.axis_index("y") + lax.axis_index("c")
    xs = x[0]
    target = loss_target[0]
    n_ada = w_ada.shape[2]
    n_dnc = dn_conv_w.shape[2]
    n_ffc = ffn_conv_w.shape[2]

    small = jnp.concatenate([c, dn_conv_w[0].reshape(1, -1), ffn_conv_w[0].reshape(1, -1)], axis=1)
    small = _pad_lanes(small, -(-small.shape[1] // LANES) * LANES)
    small_g, w_in_g = _all_gather([small, w_in[0].astype(bf16)], "gather_w_in")
    later = [w_proj_sb[0].astype(bf16), w_proj_dn[0].astype(bf16), w_out[0].astype(bf16),
             w_ffn_in[0].astype(bf16), w_ffn_out[0].astype(bf16)]
    gather_later = _SideComm(_gather_protocol, later, _gathered_shapes(later))
    small_g = small_g[:, 0, :]
    c_all = small_g[:, :d]
    dn_cw = _cols_from_devices(small_g[:, d:d + DN_CONV_WIDTH * n_dnc].reshape(N_DEV, DN_CONV_WIDTH, n_dnc))
    o2 = d + DN_CONV_WIDTH * n_dnc
    ffn_cw = _cols_from_devices(small_g[:, o2:o2 + FFN_CONV_WIDTH * n_ffc].reshape(N_DEV, FFN_CONV_WIDTH, n_ffc))

    w_in_full = _cols_from_devices(w_in_g)
    r_sb, r_dn, r_z = 3 * SB_WIDTH, 3 * SB_WIDTH + DN_CONV_CH, 3 * SB_WIDTH + DN_CONV_CH + DN_V_WIDTH
    r_g = r_z + 2 * DN_HEADS
    w_main = jnp.concatenate([w_in_full[:, r_g:], w_in_full[:, r_sb:r_dn], w_in_full[:, r_dn:r_z],
                              w_in_full[:, :r_sb]], axis=1)
    w_ba = _pad_lanes(w_in_full[:, r_z:r_g], LANES)

    b_loc = lax.dynamic_slice(b_ada, (0, me * n_ada), (1, n_ada))
    mod_part = _ada_fwd(c_all, w_ada[0], b_loc)
    (mod_g,) = _all_gather([mod_part], "gather_mod")
    mod = lax.dynamic_index_in_dim(mod_g, me, axis=1, keepdims=False).reshape(1, N_DEV * n_ada)
    shift1, scale1, gate1, shift2, scale2, gate2 = [mod[:, i * d:(i + 1) * d] for i in range(6)]

    logit_lanes = ((0, 0), (GDN_LOGIT_LANE, LANES - GDN_LOGIT_LANE - DN_HEADS))
    a_log = jnp.pad(dn_A_log, logit_lanes)
    dt_b = jnp.pad(dn_dt_bias, logit_lanes)

    (h,) = _stage_fwd(_f_normmod, [norm1_w, shift1, scale1], [xs], [bf16], "norm1_fwd")
    proj = _mm(h, w_main, name="in_proj")
    ba = _mm(h, w_ba, name="in_proj_ba")
    k16, k0_16, k1_16, v16, v0_16, v1_16 = _sb_prepare(proj)
    o_a, sb_runs, w_psb_g, w_pdn_g, w_out_g, w_fin_g, w_fout_g = _sb_attention_fwd2(
        proj, k16, v0_16, v1_16, side=gather_later)
    w_psb = _cols_from_devices(w_psb_g)
    w_pdn = w_pdn_g.reshape(DN_V_WIDTH, d)
    w_o = w_out_g.reshape(d, d)
    w_fin = _cols_from_devices(w_fin_g)
    w_fout = w_fout_g.reshape(D_FF, d)
    qkv_act = _dn_conv_fwd(proj, dn_cw)
    o_b, states, dn_inverses = _gdn_fwd(a_log, dt_b, dn_norm_w, qkv_act, proj, ba)
    pa = _mm(o_a, w_psb, name="proj_sb")
    pb = _mm(o_b, w_pdn, name="proj_dn")
    gates = [(proj, d, OFF_GA // d), (proj, d, OFF_GB // d)]
    (merged,) = _stage_fwd(_f_merge, [], gates + [pa, pb], [bf16], "merge_fwd")
    ao = _mm(merged, w_o, name="out_proj")
    mid_params = [gate1, norm2_w, shift2, scale2]
    x1, h2 = _stage_fwd(_f_residual_normmod, mid_params, [xs, ao], [f32, bf16], "resid1_norm2_fwd")
    u_pre = _mm(h2, w_fin, name="ffn_in")
    act, u_conv = _ffn_conv_fwd(u_pre, ffn_cw, ffn_conv_b)
    fo = _mm(act, w_fout, name="ffn_out")

    loss_p, d_gate2, d_wf, dx2, dfo = _loss_and_grads(gate2, final_norm_w.reshape(1, d), x1, fo, target)
    dact = _mm(dfo, w_fout, tb=True, name="ffn_out_dx")
    g_w_fout = _mm(act, dfo, ta=True, name="ffn_out_dw")
    du, dbg, dbu = _ffn_conv_bwd_act(u_conv, dact)
    du_pre, d_ffn_cw = _conv_bwd(du, u_pre, 0, ffn_cw, "ffn_conv_bwd")
    dh2 = _mm(du_pre, w_fin, tb=True, name="ffn_in_dx")
    g_w_fin = _mm(h2, du_pre, ta=True, name="ffn_in_dw")
    (d_gate1, d_n2w, d_shift2, d_scale2), (dx1, dao) = _stage_bwd(
        _f_residual_normmod, mid_params, [xs, ao], [dx2, dh2], [f32, bf16], "resid1_norm2_bwd")
    dmerged = _mm(dao, w_o, tb=True, name="out_proj_dx")
    g_w_o = _mm(merged, dao, ta=True, name="out_proj_dw")
    _, (dga, dgb, dpa, dpb) = _stage_bwd(_f_merge, [], gates + [pa, pb], [dmerged], [bf16] * 4, "merge_bwd")
    do_a = _mm(dpa, w_psb, tb=True, name="proj_sb_dx")
    g_w_psb = _mm(o_a, dpa, ta=True, name="proj_sb_dw")
    do_b = _mm(dpb, w_pdn, tb=True, name="proj_dn_dx")
    g_w_pdn = _mm(o_b, dpb, ta=True, name="proj_dn_dw")
    early = [_cols_by_device(g_w_psb).astype(bf16),
             g_w_pdn.reshape(N_DEV, DN_V_WIDTH // N_DEV, d).astype(bf16),
             g_w_o.reshape(N_DEV, d // N_DEV, d).astype(bf16),
             _cols_by_device(g_w_fin).astype(bf16),
             g_w_fout.reshape(N_DEV, D_FF // N_DEV, d).astype(bf16)]
    exchange_early = _SideComm(_exchange_protocol, early, [jax.ShapeDtypeStruct(a.shape, a.dtype) for a in early])
    dq, dk, dv, *recv_early = _sb_attention_bwd2(proj, k16, k0_16, k1_16, v16, sb_runs, do_a, side=exchange_early)
    d_alog, d_dtb, d_dnw, dqkv_act, dz, dba = _gdn_bwd(a_log, dt_b, dn_norm_w, qkv_act, proj, ba, states,
                                                       dn_inverses, do_b)
    d_conv_out = _dn_conv_bwd_act(proj, dn_cw, dqkv_act)
    d_dn_pre, d_dn_cw = _conv_bwd(d_conv_out, proj, OFF_DN // TCONV_C, dn_cw, "dn_conv_bwd")
    dproj = jnp.concatenate([dga, dgb, d_dn_pre, dz, dq, dk.astype(bf16), dv.astype(bf16)], axis=1)
    g_w_main = _mm(h, dproj, ta=True, name="in_proj_dw")
    g_w_ba = _mm(h, dba, ta=True, name="in_proj_ba_dw")
    g_w_in_full = jnp.concatenate([g_w_main[:, OFF_SBQ:], g_w_main[:, OFF_DN:OFF_Z], g_w_main[:, OFF_Z:OFF_SBQ],
                                   g_w_ba[:, :2 * DN_HEADS], g_w_main[:, :OFF_DN]], axis=1)
    w_in_parts = _cols_by_device(g_w_in_full).astype(bf16)
    exchange_w_in = _SideComm(_exchange_protocol, [w_in_parts], [jax.ShapeDtypeStruct(w_in_parts.shape, bf16)])
    dh, recv_w_in = _mm(dproj, w_main, tb=True, name="in_proj_dx", side=exchange_w_in)
    dh_ba = _mm(dba, w_ba, tb=True, name="in_proj_ba_dx")
    (d_n1w, d_shift1, d_scale1), (grad_x,) = _stage_bwd(
        _f_normmod, [norm1_w, shift1, scale1], [xs], [[dh, dh_ba]], [f32], "norm1_bwd", residual=(0, dx1))

    dmod = jnp.concatenate([d_shift1, d_scale1, d_gate1, d_shift2, d_scale2, d_gate2], axis=1)
    d_ffn_cb = jnp.concatenate([dbg, dbu], axis=1)
    small_parts = jnp.concatenate(
        [loss_p, dmod, d_n1w, d_alog, d_dtb, d_dnw, d_n2w, d_ffn_cb, d_wf,
         d_dn_cw.reshape(1, -1), d_ffn_cw.reshape(1, -1)], axis=1)
    (small_parts_g,) = _all_gather([small_parts], "gather_small_grads")
    tot = _sum_devices(small_parts_g[:, 0, :])
    offs = {}
    pos = 0
    for nm, width in (("loss", LANES), ("b_ada", 6 * d), ("norm1_w", d), ("dn_A_log", LANES), ("dn_dt_bias", LANES),
                      ("dn_norm_w", LANES), ("norm2_w", d), ("ffn_conv_b", 2 * D_FF), ("final_norm_w", d),
                      ("dn_conv_w", DN_CONV_WIDTH * DN_CONV_CH), ("ffn_conv_w", FFN_CONV_WIDTH * 2 * D_FF)):
        offs[nm] = (pos, width)
        pos += width
    seg = lambda nm: tot[:, offs[nm][0]:offs[nm][0] + offs[nm][1]]
    loss = tot[0, 0]
    g_b_ada = seg("b_ada")
    g_norm1 = seg("norm1_w")
    g_alog = seg("dn_A_log")[:, GDN_LOGIT_LANE:GDN_LOGIT_LANE + DN_HEADS]
    g_dtb = seg("dn_dt_bias")[:, GDN_LOGIT_LANE:GDN_LOGIT_LANE + DN_HEADS]
    g_dnw = seg("dn_norm_w")
    g_norm2 = seg("norm2_w")
    g_ffn_cb = seg("ffn_conv_b")
    g_fnw = seg("final_norm_w")
    g_dn_cw = lax.dynamic_slice(seg("dn_conv_w").reshape(DN_CONV_WIDTH, DN_CONV_CH), (0, me * n_dnc),
                                (DN_CONV_WIDTH, n_dnc))
    g_ffn_cw = lax.dynamic_slice(seg("ffn_conv_w").reshape(FFN_CONV_WIDTH, 2 * D_FF), (0, me * n_ffc),
                                 (FFN_CONV_WIDTH, n_ffc))

    dmod_all = small_parts_g[:, 0, offs["b_ada"][0]:offs["b_ada"][0] + 6 * d]
    g_w_ada = _ada_bwd(c_all, lax.dynamic_slice(dmod_all, (0, me * n_ada), (N_DEV, n_ada)))

    def pack(parts):
        flat = [p.reshape(1, -1) for p in parts]
        flat = [_pad_lanes(p, -(-p.shape[1] // LANES) * LANES) for p in flat]
        return jnp.concatenate(flat, axis=1), [p.shape[1] for p in flat]

    small_names_g = [g_b_ada, g_norm1, g_alog, g_dtb, g_dnw, g_norm2, g_ffn_cb, g_fnw, g_dn_cw, g_ffn_cw]
    small_w = [b_ada, norm1_w, dn_A_log, dn_dt_bias, dn_norm_w, norm2_w, ffn_conv_b, final_norm_w, dn_conv_w[0], ffn_conv_w[0]]
    small_m = [m_b_ada, m_norm1_w, m_dn_A_log, m_dn_dt_bias, m_dn_norm_w, m_norm2_w, m_ffn_conv_b, m_final_norm_w, m_dn_conv_w[0], m_ffn_conv_w[0]]
    small_v = [v_b_ada, v_norm1_w, v_dn_A_log, v_dn_dt_bias, v_dn_norm_w, v_norm2_w, v_ffn_conv_b, v_final_norm_w, v_dn_conv_w[0], v_ffn_conv_w[0]]
    pg, widths = pack(small_names_g)
    pw, _ = pack(small_w)
    pm, _ = pack(small_m)
    pv, _ = pack(small_v)
    s_delta, s_m, s_v = _adamw(pw, pg, pm, pv, "adamw_small")

    def unpack(flat):
        out, pos = [], 0
        for ref_arr, width in zip(small_w, widths):
            out.append(flat[:, pos:pos + ref_arr.size].reshape(ref_arr.shape))
            pos += width
        return out

    small_grads = [g.reshape(w_.shape) for g, w_ in zip(small_names_g, small_w)]
    small_delta, small_newm, small_newv = unpack(s_delta), unpack(s_m), unpack(s_v)

    ada_delta, ada_m, ada_v = _adamw(w_ada[0], g_w_ada, m_w_ada[0], v_w_ada[0], "adamw_ada")

    recv = [recv_w_in] + list(recv_early)
    big = {}
    for nm, parts, w_, m_, v_ in (("w_in", recv[0], w_in, m_w_in, v_w_in),
                                  ("w_proj_sb", recv[1], w_proj_sb, m_w_proj_sb, v_w_proj_sb),
                                  ("w_proj_dn", recv[2], w_proj_dn, m_w_proj_dn, v_w_proj_dn),
                                  ("w_out", recv[3], w_out, m_w_out, v_w_out),
                                  ("w_ffn_in", recv[4], w_ffn_in, m_w_ffn_in, v_w_ffn_in),
                                  ("w_ffn_out", recv[5], w_ffn_out, m_w_ffn_out, v_w_ffn_out)):
        big[nm] = [t[None] for t in _sum_adamw(parts, w_[0], m_[0], v_[0], "adamw_" + nm)]

    sg = dict(zip(["b_ada", "norm1_w", "dn_A_log", "dn_dt_bias", "dn_norm_w", "norm2_w", "ffn_conv_b", "final_norm_w",
                   "dn_conv_w", "ffn_conv_w"], range(10)))

    def small_out(table, nm):
        val = table[sg[nm]]
        return val[None] if nm in ("dn_conv_w", "ffn_conv_w") else val

    order = ["w_ada", "b_ada", "norm1_w", "w_in", "dn_conv_w", "dn_A_log", "dn_dt_bias", "dn_norm_w", "w_proj_sb",
             "w_proj_dn", "w_out", "norm2_w", "w_ffn_in", "ffn_conv_w", "ffn_conv_b", "w_ffn_out", "final_norm_w"]
    groups = []
    for k, small_table in enumerate((small_grads, small_delta, small_newm, small_newv)):
        row = []
        for nm in order:
            if nm == "w_ada":
                row.append((g_w_ada, ada_delta, ada_m, ada_v)[k][None])
            elif nm in big:
                row.append(big[nm][k])
            else:
                row.append(small_out(small_table, nm))
        groups.append(row)
    return (loss, grad_x[None], *groups[0], *groups[1], *groups[2], *groups[3])
```

```python
import functools

import jax
import jax.numpy as jnp
from jax import lax
from jax.experimental import pallas as pl
from jax.experimental.pallas import tpu as pltpu

f32 = jnp.float32
bf16 = jnp.bfloat16

D_MODEL = 1024
SB_HEADS = 8
SB_HEAD_DIM = 64
SB_WIDTH = SB_HEADS * SB_HEAD_DIM
SB_QBLOCK = 128
DN_HEADS = 8
DN_KEY_DIM = 64
DN_VAL_DIM = 128
DN_QK_WIDTH = DN_HEADS * DN_KEY_DIM
DN_V_WIDTH = DN_HEADS * DN_VAL_DIM
DN_CONV_CH = 2 * DN_QK_WIDTH + DN_V_WIDTH
DN_CONV_WIDTH = 4
DN_CHUNK = 64
D_FF = 2816
FFN_CONV_WIDTH = 3
NORM_EPS = 1e-6
L2_EPS = 1e-6
ADAM_LR = 0.001
ADAM_B1 = 0.9
ADAM_B2 = 0.999
ADAM_EPS = 1e-08
ADAM_WD = 0.01
ADAM_STEP = 10

N_DEV = 8
MESH = pl.DeviceIdType.MESH

LANES = 128
SUBLANES = 8
VMEM_LIMIT = 48 * 1024 * 1024

OFF_GA = 0
OFF_GB = D_MODEL
OFF_DN = 2 * D_MODEL
OFF_Z = OFF_DN + DN_CONV_CH
OFF_SBQ = OFF_Z + DN_V_WIDTH
OFF_SBK = OFF_SBQ + SB_WIDTH
OFF_SBV = OFF_SBK + SB_WIDTH
MAIN_WIDTH = OFF_SBV + SB_WIDTH

TM = 256
TCONV_R = 512
TCONV_C = 512
TCONV_FF = D_FF // 2
SB_PAIRS_FWD = 4
SB_PAIRS_BWD = 4
SB_DEAD = -106.0
SB_NEVER = -1e30


def _cparams(sem=None):
    return pltpu.CompilerParams(dimension_semantics=sem, vmem_limit_bytes=VMEM_LIMIT)


def _pick(n, cands):
    for c in cands:
        if n % c == 0:
            return c
    return n


def _my_pos():
    return lax.axis_index("x"), lax.axis_index("y"), lax.axis_index("c")


def _flip(v, bit):
    return 1 - v if bit else v


def _comm_scratch(n):
    return [pltpu.SemaphoreType.DMA((n, 7)), pltpu.SemaphoreType.DMA((n, 7)), pltpu.SemaphoreType.DMA((n,))]


def _gather_protocol(ins, outs, send_sems, recv_sems, local_sems):
    n = len(ins)
    x, y, c = _my_pos()
    me, sibling = (x, y, c), (x, y, 1 - c)
    chips = [(1 - x, y), (x, 1 - y), (1 - x, 1 - y)]

    def slot(out, pos):
        return out.at[4 * pos[0] + 2 * pos[1] + pos[2]]

    def copy(a, k, block, to, src=None):
        return pltpu.make_async_remote_copy(
            src_ref=slot(outs[a], block) if src is None else src, dst_ref=slot(outs[a], block),
            send_sem=send_sems.at[a, k], recv_sem=recv_sems.at[a, k], device_id=to, device_id_type=MESH)

    def local(a):
        return pltpu.make_async_copy(ins[a], slot(outs[a], me), local_sems.at[a])

    def first(a):
        return [copy(a, 0, me, sibling, src=ins[a])] + [copy(a, 1 + j, me, (*chip, c), src=ins[a])
                                                         for j, chip in enumerate(chips)]

    def start():
        for a in range(n):
            local(a).start()
            for cp in first(a):
                cp.start()

    def finish():
        forwards = []
        for a in range(n):
            for j, chip in enumerate(chips):
                copy(a, 1 + j, (*chip, c), me).wait_recv()
                fwd = copy(a, 4 + j, (*chip, c), sibling)
                fwd.start()
                forwards.append(fwd)
        for a in range(n):
            copy(a, 0, sibling, me).wait_recv()
            for j, chip in enumerate(chips):
                copy(a, 4 + j, (*chip, 1 - c), me).wait_recv()
        for a in range(n):
            for cp in first(a):
                cp.wait_send()
        for cp in forwards:
            cp.wait_send()
        for a in range(n):
            local(a).wait()

    return start, finish


def _exchange_protocol(ins, outs, send_sems, recv_sems, local_sems):
    n = len(ins)
    x, y, c = _my_pos()
    me_idx = 4 * x + 2 * y + c

    def local(a):
        return pltpu.make_async_copy(ins[a].at[me_idx], outs[a].at[me_idx], local_sems.at[a])

    def copies(a, m):
        peer = (_flip(x, m & 4), _flip(y, m & 2), _flip(c, m & 1))
        peer_idx = 4 * peer[0] + 2 * peer[1] + peer[2]
        sems = dict(send_sem=send_sems.at[a, m - 1], recv_sem=recv_sems.at[a, m - 1], device_id=peer,
                    device_id_type=MESH)
        send = pltpu.make_async_remote_copy(src_ref=ins[a].at[peer_idx], dst_ref=outs[a].at[me_idx], **sems)
        recv = pltpu.make_async_remote_copy(src_ref=ins[a].at[peer_idx], dst_ref=outs[a].at[peer_idx], **sems)
        return send, recv

    def start():
        for a in range(n):
            local(a).start()
            for m in range(1, N_DEV):
                copies(a, m)[0].start()

    def finish():
        for a in range(n):
            for m in range(1, N_DEV):
                copies(a, m)[1].wait_recv()
        for a in range(n):
            for m in range(1, N_DEV):
                copies(a, m)[0].wait_send()
            local(a).wait()

    return start, finish


def _collective_call(protocol, arrs, out_shapes, name):
    n = len(arrs)

    def body(*refs):
        start, finish = protocol(refs[:n], refs[n:2 * n], *refs[2 * n:])
        start()
        finish()

    any_spec = pl.BlockSpec(memory_space=pl.ANY)
    return pl.pallas_call(body, name=name, out_shape=out_shapes, in_specs=[any_spec] * n, out_specs=[any_spec] * n,
                          scratch_shapes=_comm_scratch(n))(*arrs)


def _gathered_shapes(arrs):
    return [jax.ShapeDtypeStruct((N_DEV,) + a.shape, a.dtype) for a in arrs]


def _all_gather(arrs, name):
    return _collective_call(_gather_protocol, arrs, _gathered_shapes(arrs), name)


MM_BLOCK_BYTES = 7 * 1024 * 1024
MM_TILE_CAP = 1664


def _lane_tile(n, cap):
    fits = [t for t in range(LANES, min(n, cap) + 1, LANES) if n % t == 0]
    return max(fits) if fits else n


def _mm_tiles(m_dim, n_dim, k_dim, a_bytes, b_bytes):
    tm = _lane_tile(m_dim, MM_TILE_CAP)
    tn = _lane_tile(n_dim, MM_TILE_CAP)
    while tm * tn * 4 > MM_BLOCK_BYTES:
        if tn >= tm and (tn // 2) % LANES == 0:
            tn //= 2
        else:
            tm //= 2
    if (m_dim % (2 * tm) == 0 and 2 * tm * k_dim * a_bytes <= MM_BLOCK_BYTES
            and 2 * tm * tn * 4 <= MM_BLOCK_BYTES):
        tm *= 2
    tk = k_dim
    if k_dim % LANES == 0:
        units = k_dim // LANES
        fits = [u for u in range(1, units + 1) if units % u == 0
                and u * LANES * max(tm * a_bytes, tn * b_bytes) <= MM_BLOCK_BYTES]
        tk = max(fits) * LANES
    return tm, tn, tk


def _mm(a, b, *, ta=False, tb=False, name, side=None):
    (k_dim, m_dim) = a.shape if ta else a.shape[::-1]
    (n_dim, kb_dim) = b.shape if tb else b.shape[::-1]
    assert k_dim == kb_dim, (a.shape, b.shape, ta, tb)
    tm, tn, tk = _mm_tiles(m_dim, n_dim, k_dim, a.dtype.itemsize, b.dtype.itemsize)
    nk = k_dim // tk
    grid = (m_dim // tm, n_dim // tn, nk)
    dims = (((0 if ta else 1,), (1 if tb else 0,)), ((), ()))
    ns = side.n if side else 0

    def body(a_ref, b_ref, *rest):
        if side:
            side.run(rest[:ns], rest[ns + 1:2 * ns + 1], rest[2 * ns + 1:], *_grid_ends(grid),
                     lambda: compute(a_ref, b_ref, rest[ns]))
        else:
            compute(a_ref, b_ref, rest[0])

    def compute(a_ref, b_ref, o_ref):
        part = lax.dot_general(a_ref[...].astype(bf16), b_ref[...].astype(bf16), dims, preferred_element_type=f32)
        if nk == 1:
            o_ref[...] = part
        else:
            k = pl.program_id(2)

            @pl.when(k == 0)
            def _():
                o_ref[...] = part

            @pl.when(k > 0)
            def _():
                o_ref[...] += part

    a_spec = pl.BlockSpec((tk, tm), lambda i, j, k: (k, i)) if ta else pl.BlockSpec((tm, tk), lambda i, j, k: (i, k))
    b_spec = pl.BlockSpec((tn, tk), lambda i, j, k: (j, k)) if tb else pl.BlockSpec((tk, tn), lambda i, j, k: (k, j))
    out_spec = pl.BlockSpec((tm, tn), lambda i, j, k: (i, j))
    out_shape = jax.ShapeDtypeStruct((m_dim, n_dim), f32)
    if not side:
        return pl.pallas_call(body, name=name, grid=grid, in_specs=[a_spec, b_spec], out_specs=out_spec,
                              out_shape=out_shape,
                              compiler_params=_cparams(("parallel", "parallel", "arbitrary")))(a, b)
    return pl.pallas_call(
        body, name=name, grid=grid, in_specs=[a_spec, b_spec] + side.specs(), out_specs=[out_spec] + side.specs(),
        out_shape=[out_shape] + side.out_shapes, scratch_shapes=_comm_scratch(ns),
        compiler_params=_cparams(("arbitrary", "arbitrary", "arbitrary")))(a, b, *side.arrs)


def _win(t):
    return t if isinstance(t, tuple) else (t, t.shape[1], 0)


def _tile_spec(width, cb, tm):
    return pl.BlockSpec((tm, width), lambda i: (i, cb))


def _param_spec(p):
    return pl.BlockSpec(p.shape, lambda i: (0, 0))


def _stage_fwd(f, params, tiles, out_dtypes, name):
    tiles = [_win(t) for t in tiles]
    rows = tiles[0][0].shape[0]
    tm = min(TM, rows)
    avals = jax.eval_shape(f, *[jax.ShapeDtypeStruct(p.shape, f32) for p in params],
                           *[jax.ShapeDtypeStruct((tm, w), f32) for _, w, _ in tiles])
    n_p, n_t = len(params), len(tiles)

    def body(*refs):
        p = [r[...] for r in refs[:n_p]]
        t = [r[...].astype(f32) for r in refs[n_p:n_p + n_t]]
        for o_ref, val in zip(refs[n_p + n_t:], f(*p, *t)):
            o_ref[...] = val.astype(o_ref.dtype)

    return pl.pallas_call(
        body, name=name, grid=(rows // tm,),
        in_specs=[_param_spec(p) for p in params] + [_tile_spec(w, cb, tm) for _, w, cb in tiles],
        out_specs=[_tile_spec(a.shape[1], 0, tm) for a in avals],
        out_shape=[jax.ShapeDtypeStruct((rows, a.shape[1]), dt) for a, dt in zip(avals, out_dtypes)],
        compiler_params=_cparams(("parallel",)),
    )(*params, *[t[0] for t in tiles])


def _stage_bwd(f, params, tiles, cts, grad_dtypes, name, residual=None):
    tiles = [_win(t) for t in tiles]
    rows = tiles[0][0].shape[0]
    tm = min(TM, rows)
    cts = [list(g) if isinstance(g, (list, tuple)) else [g] for g in cts]
    flat_cts = [a for g in cts for a in g]
    n_p, n_t, n_c = len(params), len(tiles), len(flat_cts)
    has_res = residual is not None
    want = [j for j, dt in enumerate(grad_dtypes) if dt is not None]

    def body(*refs):
        i = pl.program_id(0)
        p = [r[...] for r in refs[:n_p]]
        t = [r[...].astype(f32) for r in refs[n_p:n_p + n_t]]
        ct_vals = [r[...].astype(f32) for r in refs[n_p + n_t:n_p + n_t + n_c]]
        ct, at = [], 0
        for g in cts:
            ct.append(functools.reduce(jnp.add, ct_vals[at:at + len(g)]))
            at += len(g)
        ct = tuple(ct)
        pos = n_p + n_t + n_c
        res_ref = refs[pos] if has_res else None
        pos += 1 if has_res else 0
        dp_refs = refs[pos:pos + n_p]
        dt_refs = refs[pos + n_p:]
        _, vjp = jax.vjp(f, *p, *t)
        grads = vjp(ct)

        @pl.when(i == 0)
        def _():
            for r in dp_refs:
                r[...] = jnp.zeros_like(r)

        for r, g in zip(dp_refs, grads[:n_p]):
            r[...] += g
        for r, j in zip(dt_refs, want):
            g = grads[n_p + j]
            if has_res and j == residual[0]:
                g = g + res_ref[...].astype(f32)
            r[...] = g.astype(r.dtype)

    in_arrays = list(params) + [t[0] for t in tiles] + flat_cts
    in_specs = ([_param_spec(p) for p in params] + [_tile_spec(w, cb, tm) for _, w, cb in tiles]
                + [_tile_spec(c.shape[1], 0, tm) for c in flat_cts])
    if has_res:
        in_arrays.append(residual[1])
        in_specs.append(_tile_spec(residual[1].shape[1], 0, tm))
    out_shape = ([jax.ShapeDtypeStruct(p.shape, f32) for p in params]
                 + [jax.ShapeDtypeStruct((rows, tiles[j][1]), grad_dtypes[j]) for j in want])
    out_specs = [_param_spec(p) for p in params] + [_tile_spec(tiles[j][1], 0, tm) for j in want]
    outs = pl.pallas_call(
        body, name=name, grid=(rows // tm,), in_specs=in_specs, out_specs=out_specs, out_shape=out_shape,
        compiler_params=_cparams(("arbitrary",)),
    )(*in_arrays)
    return outs[:n_p], outs[n_p:]


def _rms(x, w):
    return x * lax.rsqrt(jnp.mean(x * x, axis=-1, keepdims=True) + NORM_EPS) * w


def _f_normmod(w, shift, scale, x):
    return (_rms(x, w) * (1.0 + scale) + shift,)


def _f_merge(ga, gb, pa, pb):
    return (jax.nn.sigmoid(ga) * pa + jax.nn.sigmoid(gb) * pb,)


def _f_residual_normmod(gate, w, shift, scale, x, branch):
    x1 = x + gate * branch
    return x1, _rms(x1, w) * (1.0 + scale) + shift


def _f_loss(gate, wf, x1, fo, target):
    y = _rms(x1 + gate * fo, wf)
    err = jnp.square(y - target)
    return (0.5 * jnp.sum(jnp.mean(err, axis=-1, keepdims=True), axis=0, keepdims=True),)


def _loss_and_grads(gate2, wf, x1, fo, target):
    rows, d = x1.shape
    tm = min(TM, rows)

    def body(g_ref, w_ref, x_ref, fo_ref, t_ref, loss_ref, dg_ref, dw_ref, dx_ref, dfo_ref):
        i = pl.program_id(0)
        (val,), vjp = jax.vjp(_f_loss, g_ref[...], w_ref[...], x_ref[...], fo_ref[...], t_ref[...])
        dg, dw, dx, dfo, _ = vjp((jnp.ones((1, 1), f32),))

        @pl.when(i == 0)
        def _():
            loss_ref[...] = jnp.zeros_like(loss_ref)
            dg_ref[...] = jnp.zeros_like(dg_ref)
            dw_ref[...] = jnp.zeros_like(dw_ref)

        loss_ref[...] += jnp.broadcast_to(val, loss_ref.shape)
        dg_ref[...] += dg
        dw_ref[...] += dw
        dx_ref[...] = dx
        dfo_ref[...] = dfo.astype(bf16)

    vec = pl.BlockSpec((1, d), lambda i: (0, 0))
    tile = pl.BlockSpec((tm, d), lambda i: (i, 0))
    return pl.pallas_call(
        body, name="loss_fwd_bwd", grid=(rows // tm,),
        in_specs=[vec, vec, tile, tile, tile],
        out_specs=[pl.BlockSpec((1, LANES), lambda i: (0, 0)), vec, vec, tile, tile],
        out_shape=[jax.ShapeDtypeStruct((1, LANES), f32), jax.ShapeDtypeStruct((1, d), f32),
                   jax.ShapeDtypeStruct((1, d), f32), jax.ShapeDtypeStruct((rows, d), f32),
                   jax.ShapeDtypeStruct((rows, d), bf16)],
        compiler_params=_cparams(("arbitrary",)),
    )(gate2, wf, x1, fo, target)


def _softplus(z):
    return jnp.maximum(z, 0.0) + jnp.log(1.0 + jnp.exp(-jnp.abs(z)))


def _split_dot(a, m):
    hi = a.astype(bf16)
    lo = (a - hi.astype(f32)).astype(bf16)
    return jnp.dot(hi, m, preferred_element_type=f32) + jnp.dot(lo, m, preferred_element_type=f32)


def _suffix_matrix(n):
    r = lax.broadcasted_iota(jnp.int32, (n, n), 0)
    c = lax.broadcasted_iota(jnp.int32, (n, n), 1)
    return (r > c).astype(bf16)


def _head_masks():
    lane = lax.broadcasted_iota(jnp.int32, (1, LANES), 1)
    return [(lane < SB_HEAD_DIM).astype(f32), (lane >= SB_HEAD_DIM).astype(f32)]


def _sb_prepare(proj):
    def f(k, v):
        lane = lax.broadcasted_iota(jnp.int32, (1, SB_WIDTH), 1)
        m0 = (jnp.bitwise_and(lane, LANES - 1) < SB_HEAD_DIM).astype(f32)
        m1 = 1.0 - m0
        return k, k * m0, k * m1, v, v * m0, v * m1

    wins = [(proj, SB_WIDTH, OFF_SBK // SB_WIDTH), (proj, SB_WIDTH, OFF_SBV // SB_WIDTH)]
    return _stage_fwd(f, [], wins, [bf16] * 6, "sb_prepare")


def _stack_heads(x):
    m0, m1 = _head_masks()
    return jnp.concatenate([x * m0, x * m1], axis=0)


def _sb_logits(qst, k, t_pos2, kb, bq, masked):
    z = lax.dot_general(qst, k, (((1,), (1,)), ((), ())), preferred_element_type=f32)
    l = -_softplus(z)
    if masked:
        s_pos = kb * bq + lax.broadcasted_iota(jnp.int32, (1, bq), 1)
        causal = s_pos < t_pos2
        l = jnp.where(causal, l, 0.0)
    else:
        causal = None
    return z, l, causal


class _SideComm:
    def __init__(self, protocol, arrs, out_shapes):
        self.protocol, self.arrs, self.out_shapes = protocol, list(arrs), list(out_shapes)
        self.n = len(self.arrs)

    def specs(self):
        return [pl.BlockSpec(memory_space=pl.ANY)] * self.n

    def run(self, in_refs, out_refs, sems, first, last, compute):
        start, finish = self.protocol(in_refs, out_refs, *sems)
        pl.when(first)(start)
        compute()
        pl.when(last)(finish)


def _grid_ends(grid):
    ids = [pl.program_id(axis) for axis in range(len(grid))]
    first = functools.reduce(jnp.logical_and, [i == 0 for i in ids])
    last = functools.reduce(jnp.logical_and, [i == g - 1 for i, g in zip(ids, grid)])
    return first, last


def _sb_attention_fwd2(proj, k16, v0_16, v1_16, side=None):
    rows = proj.shape[0]
    bq = SB_QBLOCK
    nq = rows // bq
    assert nq <= LANES, "one lane per key block"
    npair = SB_WIDTH // LANES
    scale = SB_HEAD_DIM ** -0.5

    npp = SB_PAIRS_FWD
    wq = npp * LANES
    grid = (npair // npp, nq)
    ns = side.n if side else 0

    def body(q_ref, k_ref, v0_ref, v1_ref, *rest):
        o_ref, runs_ref = rest[ns], rest[ns + 1]
        if side:
            side.run(rest[:ns], rest[ns + 2:2 * ns + 2], rest[2 * ns + 2:], *_grid_ends(grid),
                     lambda: compute(q_ref, k_ref, v0_ref, v1_ref, o_ref, runs_ref))
        else:
            compute(q_ref, k_ref, v0_ref, v1_ref, o_ref, runs_ref)

    def compute(q_ref, k_ref, v0_ref, v1_ref, o_ref, runs_ref):
        qi = pl.program_id(1)
        pairs = [slice(pp * LANES, (pp + 1) * LANES) for pp in range(npp)]
        qst = [(_stack_heads(q_ref[:, s]) * scale).astype(bf16) for s in pairs]
        r = lax.broadcasted_iota(jnp.int32, (bq, 2 * bq), 0)
        c = lax.broadcasted_iota(jnp.int32, (bq, 2 * bq), 1)
        m2 = jnp.logical_or(r > c, c >= bq).astype(bf16)
        t_pos = qi * bq + lax.broadcasted_iota(jnp.int32, (bq, 1), 0)
        t_pos2 = jnp.concatenate([t_pos, t_pos], axis=0)
        lane = lax.broadcasted_iota(jnp.int32, (1, LANES), 1)
        runs_ref[...] = jnp.full(runs_ref.shape, SB_NEVER, f32)

        def tiles(kbs, carry, masked):
            jobs = [(pp, kb) for kb in kbs for pp in range(npp)]
            rows_k = [pl.ds(pl.multiple_of(kb * bq, bq), bq) for _, kb in jobs]
            zl = [_sb_logits(qst[pp], k_ref[rk, pairs[pp]], t_pos2, kb, bq, masked) for (pp, kb), rk in zip(jobs, rows_k)]
            cs = [_split_dot(l, m2) for _, l, _ in zl]
            run = [cr[0] for cr in carry]
            acc = [cr[1] for cr in carry]
            probs = []
            for (pp, kb), (z, l, causal), cs2 in zip(jobs, zl, cs):
                a = jnp.exp(z + l + cs2[:, :bq] + run[pp])
                if masked:
                    a = jnp.where(causal, a, 0.0)
                probs.append(a.astype(bf16))
                for hh in range(2):
                    cols = slice((2 * pp + hh) * LANES, (2 * pp + hh + 1) * LANES)
                    runs_ref[:, cols] = jnp.where(lane == kb, run[pp][hh * bq:(hh + 1) * bq], runs_ref[:, cols])
                run[pp] = run[pp] + cs2[:, bq:]
            for (pp, kb), rk, ab in zip(jobs, rows_k, probs):
                acc[pp] = (acc[pp] + jnp.dot(ab[:bq], v0_ref[rk, pairs[pp]], preferred_element_type=f32)
                           + jnp.dot(ab[bq:], v1_ref[rk, pairs[pp]], preferred_element_type=f32))
            return tuple(zip(run, acc))

        zero = (jnp.zeros((2 * bq, bq), f32), jnp.zeros((bq, LANES), f32))
        carry = tiles([qi], (zero,) * npp, True)

        def alive(cr):
            return functools.reduce(jnp.maximum, [jnp.max(run) for run, _ in cr]) > SB_DEAD

        def two(state):
            i, _, cr = state
            cr = tiles([qi - 1 - 2 * i, qi - 2 - 2 * i], cr, False)
            return i + 1, alive(cr), cr

        n_two = qi // 2
        i_end, still, carry = lax.while_loop(lambda st: jnp.logical_and(st[0] < n_two, st[1]), two,
                                             (jnp.int32(0), alive(carry), carry))
        last_one = jnp.logical_and(qi % 2 == 1, jnp.logical_and(still, i_end == n_two))
        carry = lax.cond(last_one, lambda cr: tiles([0], cr, False), lambda cr: cr, carry)
        for pp in range(npp):
            o_ref[:, pairs[pp]] = carry[pp][1]

    kv = pl.BlockSpec((rows, wq), lambda p, i: (0, p))
    return pl.pallas_call(
        body, name="sb_attn_fwd", grid=grid,
        in_specs=[pl.BlockSpec((bq, wq), lambda p, i: (i, OFF_SBQ // wq + p)), kv, kv, kv] + (side.specs() if side else []),
        out_specs=[pl.BlockSpec((bq, wq), lambda p, i: (i, p)),
                   pl.BlockSpec((bq, 2 * wq), lambda p, i: (i, p))] + (side.specs() if side else []),
        out_shape=[jax.ShapeDtypeStruct((rows, SB_WIDTH), f32),
                   jax.ShapeDtypeStruct((rows, SB_HEADS * LANES), f32)] + (side.out_shapes if side else []),
        scratch_shapes=_comm_scratch(ns) if side else [],
        compiler_params=_cparams(("arbitrary", "arbitrary")),
    )(proj, k16, v0_16, v1_16, *(side.arrs if side else []))


def _sb_attention_bwd2(proj, k16, k0_16, k1_16, v16, runs, do, side=None):
    rows = proj.shape[0]
    bq = SB_QBLOCK
    nq = rows // bq
    npair = SB_WIDTH // LANES
    scale = SB_HEAD_DIM ** -0.5
    tn = (((0,), (0,)), ((), ()))
    nt = (((1,), (1,)), ((), ()))

    npp = SB_PAIRS_BWD
    wq = npp * LANES
    grid = (npair // npp, nq)
    ns = side.n if side else 0

    def body(q_ref, k_ref, k0_ref, k1_ref, v_ref, runs_ref, do_ref, *rest):
        outs = rest[ns:ns + 3]
        ins = (q_ref, k_ref, k0_ref, k1_ref, v_ref, runs_ref, do_ref)
        if side:
            side.run(rest[:ns], rest[ns + 3:2 * ns + 3], rest[2 * ns + 3:], *_grid_ends(grid),
                     lambda: compute(*ins, *outs))
        else:
            compute(*ins, *outs)

    def compute(q_ref, k_ref, k0_ref, k1_ref, v_ref, runs_ref, do_ref, dq_ref, dk_ref, dv_ref):
        qi = pl.program_id(1)

        @pl.when(qi == 0)
        def _():
            dk_ref[...] = jnp.zeros_like(dk_ref)
            dv_ref[...] = jnp.zeros_like(dv_ref)

        pairs = [slice(pp * LANES, (pp + 1) * LANES) for pp in range(npp)]
        qst = [(_stack_heads(q_ref[:, s]) * scale).astype(bf16) for s in pairs]
        dost = [_stack_heads(do_ref[:, s]).astype(bf16) for s in pairs]
        runs = [jnp.concatenate([runs_ref[:, 2 * pp * LANES:(2 * pp + 1) * LANES],
                                 runs_ref[:, (2 * pp + 1) * LANES:(2 * pp + 2) * LANES]], axis=0) for pp in range(npp)]
        r = lax.broadcasted_iota(jnp.int32, (bq, 2 * bq), 0)
        c = lax.broadcasted_iota(jnp.int32, (bq, 2 * bq), 1)
        suffix_m = _suffix_matrix(bq)
        m2 = jnp.logical_or(r < c, c >= bq).astype(bf16)
        t_pos = qi * bq + lax.broadcasted_iota(jnp.int32, (bq, 1), 0)
        t_pos2 = jnp.concatenate([t_pos, t_pos], axis=0)
        lane = lax.broadcasted_iota(jnp.int32, (1, LANES), 1)

        def tiles(kbs, carry, masked):
            jobs = [(pp, kb) for kb in kbs for pp in range(npp)]
            rows_k = [pl.ds(pl.multiple_of(kb * bq, bq), bq) for _, kb in jobs]
            zl = [_sb_logits(qst[pp], k_ref[rk, pairs[pp]], t_pos2, kb, bq, masked) for (pp, kb), rk in zip(jobs, rows_k)]
            das = [lax.dot_general(dost[pp], v_ref[rk, pairs[pp]], nt, preferred_element_type=f32)
                   for (pp, kb), rk in zip(jobs, rows_k)]
            sticks = [_split_dot(l, suffix_m) for _, l, _ in zl]
            probs, ps = [], []
            for (pp, kb), (z, l, causal), stick, da in zip(jobs, zl, sticks, das):
                run = jnp.sum(jnp.where(lane == kb, runs[pp], 0.0), axis=1, keepdims=True)
                a = jnp.exp(z + l + stick + run)
                if masked:
                    a = jnp.where(causal, a, 0.0)
                probs.append(a.astype(bf16))
                ps.append(da * a)
            pcs = [_split_dot(p, m2) for p in ps]
            pref = [cr[0] for cr in carry]
            dq_acc = [cr[1] for cr in carry]
            dzs = []
            for (pp, kb), (z, l, causal), p, pc2 in zip(jobs, zl, ps, pcs):
                dz = p * jnp.exp(l) - jnp.exp(z + l) * (pc2[:, :bq] + pref[pp])
                if masked:
                    dz = jnp.where(causal, dz, 0.0)
                dzs.append(dz.astype(bf16))
                pref[pp] = pref[pp] + pc2[:, bq:]
            for (pp, kb), rk, dzb, ab in zip(jobs, rows_k, dzs, probs):
                cols = pairs[pp]
                dq_acc[pp] = (dq_acc[pp] + jnp.dot(dzb[:bq], k0_ref[rk, cols], preferred_element_type=f32)
                              + jnp.dot(dzb[bq:], k1_ref[rk, cols], preferred_element_type=f32))
                dk_ref[rk, cols] += lax.dot_general(dzb, qst[pp], tn, preferred_element_type=f32)
                dv_ref[rk, cols] += lax.dot_general(ab, dost[pp], tn, preferred_element_type=f32)
            return tuple(zip(pref, dq_acc))

        zero = (jnp.zeros((2 * bq, bq), f32), jnp.zeros((bq, LANES), f32))
        colmax = functools.reduce(jnp.maximum, [jnp.max(x, axis=0, keepdims=True) for x in runs])
        live = jnp.logical_and(colmax > SB_DEAD, lane < qi)
        kb0 = jnp.minimum(jnp.min(jnp.where(live, lane, LANES)), qi)
        n_blocks = qi - kb0
        carry = lax.fori_loop(0, n_blocks // 2, lambda i, cr: tiles([kb0 + 2 * i, kb0 + 2 * i + 1], cr, False),
                              (zero,) * npp)
        carry = lax.cond(n_blocks % 2 == 1, lambda cr: tiles([qi - 1], cr, False), lambda cr: cr, carry)
        carry = tiles([qi], carry, True)
        for pp in range(npp):
            dq_ref[:, pairs[pp]] = (carry[pp][1] * scale).astype(dq_ref.dtype)

    blk = pl.BlockSpec((bq, wq), lambda p, i: (i, p))
    full = pl.BlockSpec((rows, wq), lambda p, i: (0, p), pipeline_mode=pl.Buffered(1))
    return pl.pallas_call(
        body, name="sb_attn_bwd", grid=grid,
        in_specs=[pl.BlockSpec((bq, wq), lambda p, i: (i, OFF_SBQ // wq + p)), full, full, full, full,
                  pl.BlockSpec((bq, 2 * wq), lambda p, i: (i, p)), blk] + (side.specs() if side else []),
        out_specs=[blk, full, full] + (side.specs() if side else []),
        out_shape=[jax.ShapeDtypeStruct((rows, SB_WIDTH), bf16), jax.ShapeDtypeStruct((rows, SB_WIDTH), f32),
                   jax.ShapeDtypeStruct((rows, SB_WIDTH), f32)] + (side.out_shapes if side else []),
        scratch_shapes=_comm_scratch(ns) if side else [],
        compiler_params=_cparams(("arbitrary", "arbitrary")),
    )(proj, k16, k0_16, k1_16, v16, runs, do, *(side.arrs if side else []))


def _shift_down(x, prev8, j):
    if j == 0:
        return x
    r = pltpu.roll(x, j, axis=0)
    row8 = lax.broadcasted_iota(jnp.int32, prev8.shape, 0)
    head = jnp.where(row8 < j, pltpu.roll(prev8, j, axis=0), r[0:SUBLANES])
    return jnp.concatenate([head, r[SUBLANES:]], axis=0)


def _shift_up(x, next8, j):
    if j == 0:
        return x
    n = x.shape[0]
    r = pltpu.roll(x, n - j, axis=0)
    row8 = lax.broadcasted_iota(jnp.int32, next8.shape, 0)
    tail = jnp.where(row8 >= SUBLANES - j, pltpu.roll(next8, SUBLANES - j, axis=0), r[n - SUBLANES:n])
    return jnp.concatenate([r[:n - SUBLANES], tail], axis=0)


def _conv(x, prev8, w):
    k_taps = w.shape[0]
    out = x * w[k_taps - 1:k_taps, :]
    for j in range(1, k_taps):
        out = out + _shift_down(x, prev8, j) * w[k_taps - 1 - j:k_taps - j, :]
    return out


def _conv_tiles(rows, tr_max=TCONV_R):
    tr = min(tr_max, rows)
    return tr, rows // tr, tr // SUBLANES


def _prev_spec(tc, cb0, r8):
    return pl.BlockSpec((SUBLANES, tc), lambda j, i: (jnp.maximum(i * r8 - 1, 0), cb0 + j))


def _silu(x):
    return x * jax.nn.sigmoid(x)


def _dsilu(x):
    s = jax.nn.sigmoid(x)
    return s * (1.0 + x * (1.0 - s))


def _dn_conv_fwd(proj, w):
    rows = proj.shape[0]
    tr, nr, r8 = _conv_tiles(rows)
    tc = TCONV_C
    cb0 = OFF_DN // tc

    def body(x_ref, p_ref, w_ref, o_ref):
        prev = jnp.where(pl.program_id(1) == 0, 0.0, p_ref[...])
        o_ref[...] = _silu(_conv(x_ref[...], prev, w_ref[...]))

    return pl.pallas_call(
        body, name="dn_conv_fwd", grid=(DN_CONV_CH // tc, nr),
        in_specs=[pl.BlockSpec((tr, tc), lambda j, i: (i, cb0 + j)), _prev_spec(tc, cb0, r8),
                  pl.BlockSpec((DN_CONV_WIDTH, tc), lambda j, i: (0, j))],
        out_specs=pl.BlockSpec((tr, tc), lambda j, i: (i, j)),
        out_shape=jax.ShapeDtypeStruct((rows, DN_CONV_CH), f32),
        compiler_params=_cparams(("parallel", "parallel")),
    )(proj, proj, w)


def _dn_conv_bwd_act(proj, w, dact):
    rows = proj.shape[0]
    tr, nr, r8 = _conv_tiles(rows)
    tc = TCONV_C
    cb0 = OFF_DN // tc

    def body(x_ref, p_ref, w_ref, d_ref, o_ref):
        prev = jnp.where(pl.program_id(1) == 0, 0.0, p_ref[...])
        o_ref[...] = d_ref[...] * _dsilu(_conv(x_ref[...], prev, w_ref[...]))

    return pl.pallas_call(
        body, name="dn_conv_bwd_act", grid=(DN_CONV_CH // tc, nr),
        in_specs=[pl.BlockSpec((tr, tc), lambda j, i: (i, cb0 + j)), _prev_spec(tc, cb0, r8),
                  pl.BlockSpec((DN_CONV_WIDTH, tc), lambda j, i: (0, j)),
                  pl.BlockSpec((tr, tc), lambda j, i: (i, j))],
        out_specs=pl.BlockSpec((tr, tc), lambda j, i: (i, j)),
        out_shape=jax.ShapeDtypeStruct((rows, DN_CONV_CH), f32),
        compiler_params=_cparams(("parallel", "parallel")),
    )(proj, proj, w, dact)


def _ffn_conv_fwd(u_pre, w, b):
    rows = u_pre.shape[0]
    tr, nr, r8 = _conv_tiles(rows, TCONV_R // 2)
    tc = TCONV_FF
    nct = D_FF // tc

    def body(xg_ref, pg_ref, xu_ref, pu_ref, wg_ref, wu_ref, bg_ref, bu_ref, o_ref, u_ref):
        first = pl.program_id(1) == 0
        ug = _conv(xg_ref[...], jnp.where(first, 0.0, pg_ref[...]), wg_ref[...]) + bg_ref[...]
        uu = _conv(xu_ref[...], jnp.where(first, 0.0, pu_ref[...]), wu_ref[...]) + bu_ref[...]
        o_ref[...] = (_silu(ug) * uu).astype(o_ref.dtype)
        u_ref[0] = ug
        u_ref[1] = uu

    def x_spec(off):
        return pl.BlockSpec((tr, tc), lambda j, i: (i, off + j))

    def w_spec(k, off):
        return pl.BlockSpec((k, tc), lambda j, i: (0, off + j))

    return pl.pallas_call(
        body, name="ffn_conv_fwd", grid=(nct, nr),
        in_specs=[x_spec(0), _prev_spec(tc, 0, r8), x_spec(nct), _prev_spec(tc, nct, r8),
                  w_spec(FFN_CONV_WIDTH, 0), w_spec(FFN_CONV_WIDTH, nct), w_spec(1, 0), w_spec(1, nct)],
        out_specs=[pl.BlockSpec((tr, tc), lambda j, i: (i, j)), pl.BlockSpec((2, tr, tc), lambda j, i: (0, i, j))],
        out_shape=[jax.ShapeDtypeStruct((rows, D_FF), bf16), jax.ShapeDtypeStruct((2, rows, D_FF), f32)],
        compiler_params=_cparams(("parallel", "parallel")),
    )(u_pre, u_pre, u_pre, u_pre, w, w, b, b)


def _ffn_conv_bwd_act(u, dact):
    rows = dact.shape[0]
    tr, nr, _ = _conv_tiles(rows, TCONV_R // 2)
    tc = TCONV_FF
    nct = D_FF // tc

    def body(u_ref, d_ref, du_ref, dbg_ref, dbu_ref):
        first = pl.program_id(1) == 0
        ug = u_ref[0]
        uu = u_ref[1]
        d = d_ref[...]
        sig = jax.nn.sigmoid(ug)
        dug = d * uu * (sig * (1.0 + ug * (1.0 - sig)))
        duu = d * (ug * sig)
        du_ref[0] = dug
        du_ref[1] = duu

        @pl.when(first)
        def _():
            dbg_ref[...] = jnp.zeros_like(dbg_ref)
            dbu_ref[...] = jnp.zeros_like(dbu_ref)

        dbg_ref[...] += jnp.sum(dug, axis=0, keepdims=True)
        dbu_ref[...] += jnp.sum(duu, axis=0, keepdims=True)

    pair = pl.BlockSpec((2, tr, tc), lambda j, i: (0, i, j))
    vec = pl.BlockSpec((1, tc), lambda j, i: (0, j))
    return pl.pallas_call(
        body, name="ffn_conv_bwd_act", grid=(nct, nr),
        in_specs=[pair, pl.BlockSpec((tr, tc), lambda j, i: (i, j))],
        out_specs=[pair, vec, vec],
        out_shape=[jax.ShapeDtypeStruct((2, rows, D_FF), f32),
                   jax.ShapeDtypeStruct((1, D_FF), f32), jax.ShapeDtypeStruct((1, D_FF), f32)],
        compiler_params=_cparams(("parallel", "arbitrary")),
    )(u, dact)


def _conv_bwd(dy, x, x_cb0, w, name):
    k_taps = w.shape[0]
    split = dy.ndim == 3
    rows = dy.shape[-2]
    ch = dy.shape[-1] * (2 if split else 1)
    tc = TCONV_FF if split else TCONV_C
    tr, nr, r8 = _conv_tiles(rows)
    per_half = dy.shape[-1] // tc
    last8 = rows // SUBLANES - 1

    def body(dy_ref, nx_ref, x_ref, w_ref, dx_ref, dw_ref):
        i = pl.program_id(1)
        dyv = dy_ref[...]
        nxt = jnp.where(i == nr - 1, 0.0, nx_ref[...])
        xv = x_ref[...].astype(f32)
        wv = w_ref[...]

        @pl.when(i == 0)
        def _():
            dw_ref[...] = jnp.zeros_like(dw_ref)

        dx = dyv * wv[k_taps - 1:k_taps, :]
        dw_ref[k_taps - 1:k_taps, :] += jnp.sum(dyv * xv, axis=0, keepdims=True)
        for j in range(1, k_taps):
            dy_j = _shift_up(dyv, nxt, j)
            dx = dx + dy_j * wv[k_taps - 1 - j:k_taps - j, :]
            dw_ref[k_taps - 1 - j:k_taps - j, :] += jnp.sum(dy_j * xv, axis=0, keepdims=True)
        dx_ref[...] = dx.astype(dx_ref.dtype)

    tile = pl.BlockSpec((tr, tc), lambda j, i: (i, j))
    if split:
        dy_spec = pl.BlockSpec((None, tr, tc), lambda j, i: (j // per_half, i, j % per_half))
        next_spec = pl.BlockSpec((None, SUBLANES, tc),
                                 lambda j, i: (j // per_half, jnp.minimum((i + 1) * r8, last8), j % per_half))
    else:
        dy_spec = tile
        next_spec = pl.BlockSpec((SUBLANES, tc), lambda j, i: (jnp.minimum((i + 1) * r8, last8), j))
    return pl.pallas_call(
        body, name=name, grid=(ch // tc, nr),
        in_specs=[dy_spec, next_spec, pl.BlockSpec((tr, tc), lambda j, i: (i, x_cb0 + j)),
                  pl.BlockSpec((k_taps, tc), lambda j, i: (0, j))],
        out_specs=[tile, pl.BlockSpec((k_taps, tc), lambda j, i: (0, j))],
        out_shape=[jax.ShapeDtypeStruct((rows, ch), bf16), jax.ShapeDtypeStruct((k_taps, ch), f32)],
        compiler_params=_cparams(("parallel", "arbitrary")),
    )(dy, dy, x, w)


def _hdot(a, b):
    return jnp.dot(a, b, preferred_element_type=f32, precision=lax.Precision.HIGH)


def _xdot(a, b):
    return jnp.dot(a, b, preferred_element_type=f32, precision=lax.Precision.HIGHEST)


def _bdot(a, b):
    return jnp.dot(a.astype(bf16), b.astype(bf16), preferred_element_type=f32)


def _bdot_nt(a, b):
    return lax.dot_general(a.astype(bf16), b.astype(bf16), (((1,), (1,)), ((), ())), preferred_element_type=f32)


def _bdot_tn(a, b):
    return lax.dot_general(a.astype(bf16), b.astype(bf16), (((0,), (0,)), ((), ())), preferred_element_type=f32)


GDN_GROUP = 4
GDN_NGROUPS = DN_HEADS // GDN_GROUP
GDN_ROWS = GDN_GROUP * DN_CHUNK
GDN_QK_LANES = GDN_GROUP * DN_KEY_DIM
GDN_LOGIT_LANE = DN_HEADS


def _inverse_impl(lows):
    n = lows[0].shape[0]
    r = lax.broadcasted_iota(jnp.int32, (n, n), 0)
    c = lax.broadcasted_iota(jnp.int32, (n, n), 1)
    eye = (r == c).astype(f32)
    blk = jnp.right_shift(r, 3) == jnp.right_shift(c, 3)
    d = [jnp.where(blk, low, 0.0) for low in lows]
    e = [low - x for low, x in zip(lows, d)]

    def nilpotent8_inverse(xs):
        acc = [eye - x for x in xs]
        power = xs
        for _ in range(2):
            power = [_bdot(x, x) for x in power]
            acc = [_bdot(a, eye + x) for a, x in zip(acc, power)]
        return acc

    dinv = nilpotent8_inverse(d)
    ninv = nilpotent8_inverse([_bdot(x, y) for x, y in zip(dinv, e)])
    t = [_bdot(x, y) for x, y in zip(ninv, dinv)]
    for _ in range(2):
        res = [eye - x - _hdot(low, x) for low, x in zip(lows, t)]
        t = [x + _bdot(x, y) for x, y in zip(t, res)]
    return tuple(t)


@jax.custom_vjp
def _unit_lower_inverses(lows):
    return _inverse_impl(lows)


def _unit_lower_inverses_fwd(lows):
    t = _inverse_impl(lows)
    return t, t


def _unit_lower_inverses_bwd(t, ct):
    tn = (((0,), (0,)), ((), ()))
    nt = (((1,), (1,)), ((), ()))
    left = [lax.dot_general(x, g, tn, preferred_element_type=f32, precision=lax.Precision.HIGH) for x, g in zip(t, ct)]
    return (tuple(-lax.dot_general(x, y, nt, preferred_element_type=f32, precision=lax.Precision.HIGH)
                  for x, y in zip(left, t)),)


_unit_lower_inverses.defvjp(_unit_lower_inverses_fwd, _unit_lower_inverses_bwd)


@jax.custom_vjp
def _known_inverses(lows, t):
    return t


def _known_inverses_fwd(lows, t):
    return t, t


def _known_inverses_bwd(t, ct):
    return _unit_lower_inverses_bwd(t, ct) + (tuple(jnp.zeros_like(x) for x in t),)


_known_inverses.defvjp(_known_inverses_fwd, _known_inverses_bwd)


def _gdn_chunk(a_log, dt_bias, norm_w, ba, *per_group, inverses=None, keep_inverses=False):
    ng = GDN_NGROUPS
    qgs, kgs, vsts, zsts, states = [per_group[i * ng:(i + 1) * ng] for i in range(5)]
    groups = range(ng)
    n = GDN_ROWS
    r = lax.broadcasted_iota(jnp.int32, (n, n), 0)
    c = lax.broadcasted_iota(jnp.int32, (n, n), 1)
    same_head = jnp.right_shift(r, 6) == jnp.right_shift(c, 6)
    incl = jnp.logical_and(same_head, r >= c)
    strict = jnp.logical_and(same_head, r > c)
    eye = (r == c).astype(f32)
    ones = jnp.ones((n, n), f32)
    own_lanes = same_head.astype(f32)
    lane = lax.broadcasted_iota(jnp.int32, (1, LANES), 1)
    pick = lambda arr, idx: jnp.sum(jnp.where(lane == idx, arr, 0.0), axis=1, keepdims=True)
    heads = [[GDN_GROUP * g + h for h in range(GDN_GROUP)] for g in groups]
    rc = lax.broadcasted_iota(jnp.int32, (DN_CHUNK, DN_CHUNK), 0)
    cc = lax.broadcasted_iota(jnp.int32, (DN_CHUNK, DN_CHUNK), 1)

    g_all = -jnp.exp(a_log) * _softplus(ba + dt_bias)
    gc_all = _xdot((rc >= cc).astype(f32), g_all)
    gl_all = jnp.sum(g_all, axis=0, keepdims=True)
    beta = [jnp.concatenate([jax.nn.sigmoid(pick(ba, hd)) for hd in heads[g]], axis=0) for g in groups]
    gc = [jnp.concatenate([pick(gc_all, GDN_LOGIT_LANE + hd) for hd in heads[g]], axis=0) for g in groups]
    g_last = [jnp.concatenate([jnp.broadcast_to(pick(gl_all, GDN_LOGIT_LANE + hd), (DN_CHUNK, 1)) for hd in heads[g]],
                              axis=0) for g in groups]
    gr = [jnp.broadcast_to(gc[g], (n, n)).T for g in groups]
    decay = [jnp.where(incl, jnp.exp(jnp.where(incl, gc[g] - gr[g], 0.0)), 0.0) for g in groups]
    q = [jnp.concatenate([qgs[g]] * GDN_GROUP, axis=0) * own_lanes for g in groups]
    k = [jnp.concatenate([kgs[g]] * GDN_GROUP, axis=0) * own_lanes for g in groups]
    qn = [x * lax.rsqrt(jnp.sum(x * x, axis=1, keepdims=True) + L2_EPS) * (DN_KEY_DIM ** -0.5) for x in q]
    kn = [x * lax.rsqrt(jnp.sum(x * x, axis=1, keepdims=True) + L2_EPS) for x in k]
    kb = [kn[g] * beta[g] for g in groups]
    low = [jnp.where(strict, _bdot_nt(kb[g], kn[g]) * decay[g], 0.0) for g in groups]
    intra = [jnp.where(incl, _bdot_nt(qn[g], kn[g]) * decay[g], 0.0) for g in groups]
    t = _unit_lower_inverses(tuple(low)) if inverses is None else _known_inverses(tuple(low), tuple(inverses))
    u = [_bdot(t[g], vsts[g] * beta[g]) for g in groups]
    w = [_bdot(t[g], kb[g] * jnp.exp(gc[g])) for g in groups]
    sb = [s.astype(bf16) for s in states]
    v_new = [u[g] - jnp.dot(w[g].astype(bf16), sb[g], preferred_element_type=f32) for g in groups]
    o = [jnp.dot((qn[g] * jnp.exp(gc[g])).astype(bf16), sb[g], preferred_element_type=f32) for g in groups]
    o = [o[g] + _bdot(intra[g], v_new[g]) for g in groups]
    new_state = [states[g] * jnp.exp(g_last[g]) + _bdot_tn(kn[g] * jnp.exp(g_last[g] - gc[g]), v_new[g])
                 for g in groups]
    o_n = [x * lax.rsqrt(jnp.mean(x * x, axis=1, keepdims=True) + NORM_EPS) * norm_w for x in o]
    return tuple(o_n[g] * _silu(zsts[g]) for g in groups) + tuple(new_state) + (tuple(t) if keep_inverses else ())


def _gdn_specs(rows, reverse):
    n = rows // DN_CHUNK
    idx = (lambda i: n - 1 - i) if reverse else (lambda i: i)
    vec = pl.BlockSpec((1, LANES), lambda i: (0, 0))
    qkv = pl.BlockSpec((DN_CHUNK, DN_CONV_CH), lambda i: (idx(i), 0))
    z = pl.BlockSpec((DN_CHUNK, DN_V_WIDTH), lambda i: (idx(i), OFF_Z // DN_V_WIDTH))
    ba = pl.BlockSpec((DN_CHUNK, LANES), lambda i: (idx(i), 0))
    wide = pl.BlockSpec((DN_CHUNK, DN_V_WIDTH), lambda i: (idx(i), 0))
    st = pl.BlockSpec((1, DN_HEADS * DN_KEY_DIM, LANES), lambda i: (idx(i), 0, 0))
    inv = pl.BlockSpec((1, GDN_NGROUPS * GDN_ROWS, GDN_ROWS), lambda i: (idx(i), 0, 0))
    return n, vec, qkv, z, ba, wide, st, inv


def _gdn_slices(grp):
    q = slice(grp * GDN_QK_LANES, (grp + 1) * GDN_QK_LANES)
    k = slice(DN_QK_WIDTH + grp * GDN_QK_LANES, DN_QK_WIDTH + (grp + 1) * GDN_QK_LANES)
    heads = [slice((GDN_GROUP * grp + h) * LANES, (GDN_GROUP * grp + h + 1) * LANES) for h in range(GDN_GROUP)]
    vs = [slice(2 * DN_QK_WIDTH + s.start, 2 * DN_QK_WIDTH + s.stop) for s in heads]
    return q, k, vs, heads


def _stack_cols(ref, cols):
    return jnp.concatenate([ref[:, s] for s in cols], axis=0)


def _gdn_operands(qkv_ref, z_ref, state_rows):
    sl = [_gdn_slices(grp) for grp in range(GDN_NGROUPS)]
    return ([qkv_ref[:, q] for q, _, _, _ in sl] + [qkv_ref[:, k] for _, k, _, _ in sl]
            + [_stack_cols(qkv_ref, vs) for _, _, vs, _ in sl] + [_stack_cols(z_ref, heads) for _, _, _, heads in sl]
            + [state_rows[grp * GDN_ROWS:(grp + 1) * GDN_ROWS, :] for grp in range(GDN_NGROUPS)])


def _gdn_fwd(a_log, dt_bias, norm_w, qkv_act, proj, ba):
    rows = qkv_act.shape[0]
    n, vec, qkv_s, z_s, ba_s, wide, st_s, inv_s = _gdn_specs(rows, False)

    def body(al_ref, dt_ref, nw_ref, qkv_ref, z_ref, ba_ref, o_ref, st_ref, inv_ref, state):
        @pl.when(pl.program_id(0) == 0)
        def _():
            state[...] = jnp.zeros_like(state)

        st_ref[0] = state[...]
        out = _gdn_chunk(al_ref[...], dt_ref[...], nw_ref[...], ba_ref[...], *_gdn_operands(qkv_ref, z_ref, state),
                         keep_inverses=True)
        for grp in range(GDN_NGROUPS):
            _, _, _, heads = _gdn_slices(grp)
            rs = slice(grp * GDN_ROWS, (grp + 1) * GDN_ROWS)
            for h, s in enumerate(heads):
                o_ref[:, s] = out[grp][h * DN_CHUNK:(h + 1) * DN_CHUNK].astype(o_ref.dtype)
            state[rs, :] = out[GDN_NGROUPS + grp]
            inv_ref[0, rs, :] = out[2 * GDN_NGROUPS + grp]

    return pl.pallas_call(
        body, name="gdn_fwd", grid=(n,),
        in_specs=[vec, vec, vec, qkv_s, z_s, ba_s], out_specs=[wide, st_s, inv_s],
        out_shape=[jax.ShapeDtypeStruct((rows, DN_V_WIDTH), bf16),
                   jax.ShapeDtypeStruct((n, DN_HEADS * DN_KEY_DIM, LANES), f32),
                   jax.ShapeDtypeStruct((n, GDN_NGROUPS * GDN_ROWS, GDN_ROWS), f32)],
        scratch_shapes=[pltpu.VMEM((DN_HEADS * DN_KEY_DIM, LANES), f32)],
        compiler_params=_cparams(("arbitrary",)),
    )(a_log, dt_bias, norm_w, qkv_act, proj, ba)


def _gdn_bwd(a_log, dt_bias, norm_w, qkv_act, proj, ba, states, inverses, do):
    rows = qkv_act.shape[0]
    n, vec, qkv_s, z_s, ba_s, wide, st_s, inv_s = _gdn_specs(rows, True)

    def body(al_ref, dt_ref, nw_ref, qkv_ref, z_ref, ba_ref, st_ref, inv_ref, do_ref,
             dal_ref, ddt_ref, dnw_ref, dqkv_ref, dz_ref, dba_ref, dstate):
        @pl.when(pl.program_id(0) == 0)
        def _():
            dstate[...] = jnp.zeros_like(dstate)
            dal_ref[...] = jnp.zeros_like(dal_ref)
            ddt_ref[...] = jnp.zeros_like(ddt_ref)
            dnw_ref[...] = jnp.zeros_like(dnw_ref)

        ng = GDN_NGROUPS
        kept = [inv_ref[0, grp * GDN_ROWS:(grp + 1) * GDN_ROWS, :] for grp in range(ng)]
        _, vjp = jax.vjp(functools.partial(_gdn_chunk, inverses=kept), al_ref[...], dt_ref[...], nw_ref[...],
                         ba_ref[...], *_gdn_operands(qkv_ref, z_ref, st_ref[0]))
        cts = tuple(_stack_cols(do_ref, _gdn_slices(grp)[3]) for grp in range(ng))
        cts += tuple(dstate[grp * GDN_ROWS:(grp + 1) * GDN_ROWS, :] for grp in range(ng))
        grads = vjp(cts)
        dal_ref[...] += grads[0]
        ddt_ref[...] += grads[1]
        dnw_ref[...] += grads[2]
        dba_ref[...] = grads[3]
        dqs, dks, dvs, dzs, dss = [grads[4 + i * ng:4 + (i + 1) * ng] for i in range(5)]
        for grp in range(ng):
            q, k, vs, heads = _gdn_slices(grp)
            dqkv_ref[:, q] = dqs[grp]
            dqkv_ref[:, k] = dks[grp]
            for h, (sv, sh) in enumerate(zip(vs, heads)):
                rows_h = slice(h * DN_CHUNK, (h + 1) * DN_CHUNK)
                dqkv_ref[:, sv] = dvs[grp][rows_h]
                dz_ref[:, sh] = dzs[grp][rows_h].astype(dz_ref.dtype)
            dstate[grp * GDN_ROWS:(grp + 1) * GDN_ROWS, :] = dss[grp]

    return pl.pallas_call(
        body, name="gdn_bwd", grid=(n,),
        in_specs=[vec, vec, vec, qkv_s, z_s, ba_s, st_s, inv_s, wide],
        out_specs=[vec, vec, vec, qkv_s, wide, ba_s],
        out_shape=[jax.ShapeDtypeStruct((1, LANES), f32)] * 3
        + [jax.ShapeDtypeStruct((rows, DN_CONV_CH), f32), jax.ShapeDtypeStruct((rows, DN_V_WIDTH), bf16),
           jax.ShapeDtypeStruct((rows, LANES), f32)],
        scratch_shapes=[pltpu.VMEM((DN_HEADS * DN_KEY_DIM, LANES), f32)],
        compiler_params=_cparams(("arbitrary",)),
    )(a_log, dt_bias, norm_w, qkv_act, proj, ba, states, inverses, do)


def _ada_fwd(c_all, w_loc, b_loc):
    def body(c_ref, w_ref, b_ref, o_ref):
        o_ref[...] = _bdot(_silu(c_ref[...]), w_ref[...]) + b_ref[...]

    return pl.pallas_call(body, name="ada_fwd", out_shape=jax.ShapeDtypeStruct((c_all.shape[0], w_loc.shape[1]), f32),
                          compiler_params=_cparams())(c_all, w_loc, b_loc)


def _ada_bwd(c_all, dmod_cols):
    def body(c_ref, d_ref, o_ref):
        o_ref[...] = _bdot_tn(_silu(c_ref[...]), d_ref[...])

    return pl.pallas_call(body, name="ada_bwd",
                          out_shape=jax.ShapeDtypeStruct((c_all.shape[1], dmod_cols.shape[1]), f32),
                          compiler_params=_cparams())(c_all, dmod_cols)


def _sum_devices(parts):
    def body(p_ref, o_ref):
        acc = p_ref[0:1, :]
        for d in range(1, N_DEV):
            acc = acc + p_ref[d:d + 1, :]
        o_ref[...] = acc

    return pl.pallas_call(body, name="sum_small", out_shape=jax.ShapeDtypeStruct((1, parts.shape[1]), f32),
                          compiler_params=_cparams())(parts)


def _adam_math(w, g, m, v):
    m2 = ADAM_B1 * m + (1.0 - ADAM_B1) * g
    v2 = ADAM_B2 * v + (1.0 - ADAM_B2) * jnp.square(g)
    m_hat = m2 / (1.0 - ADAM_B1 ** ADAM_STEP)
    v_hat = v2 / (1.0 - ADAM_B2 ** ADAM_STEP)
    delta = -ADAM_LR * (m_hat / (jnp.sqrt(v_hat) + ADAM_EPS) + ADAM_WD * w)
    return delta, m2, v2


def _row_tile(rows):
    return _pick(rows, (256, 128, 64, 32, 16, 8))


def _adamw(w, g, m, v, name):
    rows, cols = w.shape
    tr = _row_tile(rows)

    def body(w_ref, g_ref, m_ref, v_ref, d_ref, m2_ref, v2_ref):
        d_ref[...], m2_ref[...], v2_ref[...] = _adam_math(w_ref[...], g_ref[...], m_ref[...], v_ref[...])

    tile = pl.BlockSpec((tr, cols), lambda i: (i, 0))
    return pl.pallas_call(body, name=name, grid=(rows // tr,), in_specs=[tile] * 4, out_specs=[tile] * 3,
                          out_shape=[jax.ShapeDtypeStruct(w.shape, f32)] * 3,
                          compiler_params=_cparams(("parallel",)))(w, g, m, v)


def _sum_adamw(parts, w, m, v, name):
    rows, cols = w.shape
    tr = _row_tile(rows)

    def body(p_ref, w_ref, m_ref, v_ref, g_ref, d_ref, m2_ref, v2_ref):
        g = p_ref[0].astype(f32)
        for d in range(1, N_DEV):
            g = g + p_ref[d].astype(f32)
        g_ref[...] = g
        d_ref[...], m2_ref[...], v2_ref[...] = _adam_math(w_ref[...], g, m_ref[...], v_ref[...])

    tile = pl.BlockSpec((tr, cols), lambda i: (i, 0))
    return pl.pallas_call(body, name=name, grid=(rows // tr,),
                          in_specs=[pl.BlockSpec((N_DEV, tr, cols), lambda i: (0, i, 0)), tile, tile, tile],
                          out_specs=[tile] * 4, out_shape=[jax.ShapeDtypeStruct(w.shape, f32)] * 4,
                          compiler_params=_cparams(("parallel",)))(parts, w, m, v)


def _pad_lanes(a, width):
    return jnp.pad(a, ((0, 0), (0, width - a.shape[1])))


def _cols_by_device(full):
    r, c = full.shape
    return jnp.moveaxis(full.reshape(r, N_DEV, c // N_DEV), 1, 0)


def _cols_from_devices(parts):
    d, r, n = parts.shape
    return jnp.moveaxis(parts, 0, 1).reshape(r, d * n)


def kernel(x, c, w_ada, b_ada, norm1_w, w_in, dn_conv_w, dn_A_log, dn_dt_bias, dn_norm_w, w_proj_sb, w_proj_dn, w_out, norm2_w, w_ffn_in, ffn_conv_w, ffn_conv_b, w_ffn_out, final_norm_w, loss_target, m_w_ada, m_b_ada, m_norm1_w, m_w_in, m_dn_conv_w, m_dn_A_log, m_dn_dt_bias, m_dn_norm_w, m_w_proj_sb, m_w_proj_dn, m_w_out, m_norm2_w, m_w_ffn_in, m_ffn_conv_w, m_ffn_conv_b, m_w_ffn_out, m_final_norm_w, v_w_ada, v_b_ada, v_norm1_w, v_w_in, v_dn_conv_w, v_dn_A_log, v_dn_dt_bias, v_dn_norm_w, v_w_proj_sb, v_w_proj_dn, v_w_out, v_norm2_w, v_w_ffn_in, v_ffn_conv_w, v_ffn_conv_b, v_w_ffn_out, v_final_norm_w):
    d = D_MODEL
    me = 4 * lax.axis_index("x") + 2 * lax.axis_index("y") + lax.axis_index("c")
    xs = x[0]
    target = loss_target[0]
    n_ada = w_ada.shape[2]
    n_dnc = dn_conv_w.shape[2]
    n_ffc = ffn_conv_w.shape[2]

    small = jnp.concatenate([c, dn_conv_w[0].reshape(1, -1), ffn_conv_w[0].reshape(1, -1)], axis=1)
    small = _pad_lanes(small, -(-small.shape[1] // LANES) * LANES)
    small_g, w_in_g = _all_gather([small, w_in[0].astype(bf16)], "gather_w_in")
    later = [w_proj_sb[0].astype(bf16), w_proj_dn[0].astype(bf16), w_out[0].astype(bf16),
             w_ffn_in[0].astype(bf16), w_ffn_out[0].astype(bf16)]
    gather_later = _SideComm(_gather_protocol, later, _gathered_shapes(later))
    small_g = small_g[:, 0, :]
    c_all = small_g[:, :d]
    dn_cw = _cols_from_devices(small_g[:, d:d + DN_CONV_WIDTH * n_dnc].reshape(N_DEV, DN_CONV_WIDTH, n_dnc))
    o2 = d + DN_CONV_WIDTH * n_dnc
    ffn_cw = _cols_from_devices(small_g[:, o2:o2 + FFN_CONV_WIDTH * n_ffc].reshape(N_DEV, FFN_CONV_WIDTH, n_ffc))

    w_in_full = _cols_from_devices(w_in_g)
    r_sb, r_dn, r_z = 3 * SB_WIDTH, 3 * SB_WIDTH + DN_CONV_CH, 3 * SB_WIDTH + DN_CONV_CH + DN_V_WIDTH
    r_g = r_z + 2 * DN_HEADS
    w_main = jnp.concatenate([w_in_full[:, r_g:], w_in_full[:, r_sb:r_dn], w_in_full[:, r_dn:r_z],
                              w_in_full[:, :r_sb]], axis=1)
    w_ba = _pad_lanes(w_in_full[:, r_z:r_g], LANES)

    b_loc = lax.dynamic_slice(b_ada, (0, me * n_ada), (1, n_ada))
    mod_part = _ada_fwd(c_all, w_ada[0], b_loc)
    (mod_g,) = _all_gather([mod_part], "gather_mod")
    mod = lax.dynamic_index_in_dim(mod_g, me, axis=1, keepdims=False).reshape(1, N_DEV * n_ada)
    shift1, scale1, gate1, shift2, scale2, gate2 = [mod[:, i * d:(i + 1) * d] for i in range(6)]

    logit_lanes = ((0, 0), (GDN_LOGIT_LANE, LANES - GDN_LOGIT_LANE - DN_HEADS))
    a_log = jnp.pad(dn_A_log, logit_lanes)
    dt_b = jnp.pad(dn_dt_bias, logit_lanes)

    (h,) = _stage_fwd(_f_normmod, [norm1_w, shift1, scale1], [xs], [bf16], "norm1_fwd")
    proj = _mm(h, w_main, name="in_proj")
    ba = _mm(h, w_ba, name="in_proj_ba")
    k16, k0_16, k1_16, v16, v0_16, v1_16 = _sb_prepare(proj)
    o_a, sb_runs, w_psb_g, w_pdn_g, w_out_g, w_fin_g, w_fout_g = _sb_attention_fwd2(
        proj, k16, v0_16, v1_16, side=gather_later)
    w_psb = _cols_from_devices(w_psb_g)
    w_pdn = w_pdn_g.reshape(DN_V_WIDTH, d)
    w_o = w_out_g.reshape(d, d)
    w_fin = _cols_from_devices(w_fin_g)
    w_fout = w_fout_g.reshape(D_FF, d)
    qkv_act = _dn_conv_fwd(proj, dn_cw)
    o_b, states, dn_inverses = _gdn_fwd(a_log, dt_b, dn_norm_w, qkv_act, proj, ba)
    pa = _mm(o_a, w_psb, name="proj_sb")
    pb = _mm(o_b, w_pdn, name="proj_dn")
    gates = [(proj, d, OFF_GA // d), (proj, d, OFF_GB // d)]
    (merged,) = _stage_fwd(_f_merge, [], gates + [pa, pb], [bf16], "merge_fwd")
    ao = _mm(merged, w_o, name="out_proj")
    mid_params = [gate1, norm2_w, shift2, scale2]
    x1, h2 = _stage_fwd(_f_residual_normmod, mid_params, [xs, ao], [f32, bf16], "resid1_norm2_fwd")
    u_pre = _mm(h2, w_fin, name="ffn_in")
    act, u_conv = _ffn_conv_fwd(u_pre, ffn_cw, ffn_conv_b)
    fo = _mm(act, w_fout, name="ffn_out")

    loss_p, d_gate2, d_wf, dx2, dfo = _loss_and_grads(gate2, final_norm_w.reshape(1, d), x1, fo, target)
    dact = _mm(dfo, w_fout, tb=True, name="ffn_out_dx")
    g_w_fout = _mm(act, dfo, ta=True, name="ffn_out_dw")
    du, dbg, dbu = _ffn_conv_bwd_act(u_conv, dact)
    du_pre, d_ffn_cw = _conv_bwd(du, u_pre, 0, ffn_cw, "ffn_conv_bwd")
    dh2 = _mm(du_pre, w_fin, tb=True, name="ffn_in_dx")
    g_w_fin = _mm(h2, du_pre, ta=True, name="ffn_in_dw")
    (d_gate1, d_n2w, d_shift2, d_scale2), (dx1, dao) = _stage_bwd(
        _f_residual_normmod, mid_params, [xs, ao], [dx2, dh2], [f32, bf16], "resid1_norm2_bwd")
    dmerged = _mm(dao, w_o, tb=True, name="out_proj_dx")
    g_w_o = _mm(merged, dao, ta=True, name="out_proj_dw")
    _, (dga, dgb, dpa, dpb) = _stage_bwd(_f_merge, [], gates + [pa, pb], [dmerged], [bf16] * 4, "merge_bwd")
    do_a = _mm(dpa, w_psb, tb=True, name="proj_sb_dx")
    g_w_psb = _mm(o_a, dpa, ta=True, name="proj_sb_dw")
    do_b = _mm(dpb, w_pdn, tb=True, name="proj_dn_dx")
    g_w_pdn = _mm(o_b, dpb, ta=True, name="proj_dn_dw")
    early = [_cols_by_device(g_w_psb).astype(bf16),
             g_w_pdn.reshape(N_DEV, DN_V_WIDTH // N_DEV, d).astype(bf16),
             g_w_o.reshape(N_DEV, d // N_DEV, d).astype(bf16),
             _cols_by_device(g_w_fin).astype(bf16),
             g_w_fout.reshape(N_DEV, D_FF // N_DEV, d).astype(bf16)]
    exchange_early = _SideComm(_exchange_protocol, early, [jax.ShapeDtypeStruct(a.shape, a.dtype) for a in early])
    dq, dk, dv, *recv_early = _sb_attention_bwd2(proj, k16, k0_16, k1_16, v16, sb_runs, do_a, side=exchange_early)
    d_alog, d_dtb, d_dnw, dqkv_act, dz, dba = _gdn_bwd(a_log, dt_b, dn_norm_w, qkv_act, proj, ba, states,
                                                       dn_inverses, do_b)
    d_conv_out = _dn_conv_bwd_act(proj, dn_cw, dqkv_act)
    d_dn_pre, d_dn_cw = _conv_bwd(d_conv_out, proj, OFF_DN // TCONV_C, dn_cw, "dn_conv_bwd")
    dproj = jnp.concatenate([dga, dgb, d_dn_pre, dz, dq, dk.astype(bf16), dv.astype(bf16)], axis=1)
    g_w_main = _mm(h, dproj, ta=True, name="in_proj_dw")
    g_w_ba = _mm(h, dba, ta=True, name="in_proj_ba_dw")
    g_w_in_full = jnp.concatenate([g_w_main[:, OFF_SBQ:], g_w_main[:, OFF_DN:OFF_Z], g_w_main[:, OFF_Z:OFF_SBQ],
                                   g_w_ba[:, :2 * DN_HEADS], g_w_main[:, :OFF_DN]], axis=1)
    w_in_parts = _cols_by_device(g_w_in_full).astype(bf16)
    exchange_w_in = _SideComm(_exchange_protocol, [w_in_parts], [jax.ShapeDtypeStruct(w_in_parts.shape, bf16)])
    dh, recv_w_in = _mm(dproj, w_main, tb=True, name="in_proj_dx", side=exchange_w_in)
    dh_ba = _mm(dba, w_ba, tb=True, name="in_proj_ba_dx")
    (d_n1w, d_shift1, d_scale1), (grad_x,) = _stage_bwd(
        _f_normmod, [norm1_w, shift1, scale1], [xs], [[dh, dh_ba]], [f32], "norm1_bwd", residual=(0, dx1))

    dmod = jnp.concatenate([d_shift1, d_scale1, d_gate1, d_shift2, d_scale2, d_gate2], axis=1)
    d_ffn_cb = jnp.concatenate([dbg, dbu], axis=1)
    small_parts = jnp.concatenate(
        [loss_p, dmod, d_n1w, d_alog, d_dtb, d_dnw, d_n2w, d_ffn_cb, d_wf,
         d_dn_cw.reshape(1, -1), d_ffn_cw.reshape(1, -1)], axis=1)
    (small_parts_g,) = _all_gather([small_parts], "gather_small_grads")
    tot = _sum_devices(small_parts_g[:, 0, :])
    offs = {}
    pos = 0
    for nm, width in (("loss", LANES), ("b_ada", 6 * d), ("norm1_w", d), ("dn_A_log", LANES), ("dn_dt_bias", LANES),
                      ("dn_norm_w", LANES), ("norm2_w", d), ("ffn_conv_b", 2 * D_FF), ("final_norm_w", d),
                      ("dn_conv_w", DN_CONV_WIDTH * DN_CONV_CH), ("ffn_conv_w", FFN_CONV_WIDTH * 2 * D_FF)):
        offs[nm] = (pos, width)
        pos += width
    seg = lambda nm: tot[:, offs[nm][0]:offs[nm][0] + offs[nm][1]]
    loss = tot[0, 0]
    g_b_ada = seg("b_ada")
    g_norm1 = seg("norm1_w")
    g_alog = seg("dn_A_log")[:, GDN_LOGIT_LANE:GDN_LOGIT_LANE + DN_HEADS]
    g_dtb = seg("dn_dt_bias")[:, GDN_LOGIT_LANE:GDN_LOGIT_LANE + DN_HEADS]
    g_dnw = seg("dn_norm_w")
    g_norm2 = seg("norm2_w")
    g_ffn_cb = seg("ffn_conv_b")
    g_fnw = seg("final_norm_w")
    g_dn_cw = lax.dynamic_slice(seg("dn_conv_w").reshape(DN_CONV_WIDTH, DN_CONV_CH), (0, me * n_dnc),
                                (DN_CONV_WIDTH, n_dnc))
    g_ffn_cw = lax.dynamic_slice(seg("ffn_conv_w").reshape(FFN_CONV_WIDTH, 2 * D_FF), (0, me * n_ffc),
                                 (FFN_CONV_WIDTH, n_ffc))

    dmod_all = small_parts_g[:, 0, offs["b_ada"][0]:offs["b_ada"][0] + 6 * d]
    g_w_ada = _ada_bwd(c_all, lax.dynamic_slice(dmod_all, (0, me * n_ada), (N_DEV, n_ada)))

    def pack(parts):
        flat = [p.reshape(1, -1) for p in parts]
        flat = [_pad_lanes(p, -(-p.shape[1] // LANES) * LANES) for p in flat]
        return jnp.concatenate(flat, axis=1), [p.shape[1] for p in flat]

    small_names_g = [g_b_ada, g_norm1, g_alog, g_dtb, g_dnw, g_norm2, g_ffn_cb, g_fnw, g_dn_cw, g_ffn_cw]
    small_w = [b_ada, norm1_w, dn_A_log, dn_dt_bias, dn_norm_w, norm2_w, ffn_conv_b, final_norm_w, dn_conv_w[0], ffn_conv_w[0]]
    small_m = [m_b_ada, m_norm1_w, m_dn_A_log, m_dn_dt_bias, m_dn_norm_w, m_norm2_w, m_ffn_conv_b, m_final_norm_w, m_dn_conv_w[0], m_ffn_conv_w[0]]
    small_v = [v_b_ada, v_norm1_w, v_dn_A_log, v_dn_dt_bias, v_dn_norm_w, v_norm2_w, v_ffn_conv_b, v_final_norm_w, v_dn_conv_w[0], v_ffn_conv_w[0]]
    pg, widths = pack(small_names_g)
    pw, _ = pack(small_w)
    pm, _ = pack(small_m)
    pv, _ = pack(small_v)
    s_delta, s_m, s_v = _adamw(pw, pg, pm, pv, "adamw_small")

    def unpack(flat):
        out, pos = [], 0
        for ref_arr, width in zip(small_w, widths):
            out.append(flat[:, pos:pos + ref_arr.size].reshape(ref_arr.shape))
            pos += width
        return out

    small_grads = [g.reshape(w_.shape) for g, w_ in zip(small_names_g, small_w)]
    small_delta, small_newm, small_newv = unpack(s_delta), unpack(s_m), unpack(s_v)

    ada_delta, ada_m, ada_v = _adamw(w_ada[0], g_w_ada, m_w_ada[0], v_w_ada[0], "adamw_ada")

    recv = [recv_w_in] + list(recv_early)
    big = {}
    for nm, parts, w_, m_, v_ in (("w_in", recv[0], w_in, m_w_in, v_w_in),
                                  ("w_proj_sb", recv[1], w_proj_sb, m_w_proj_sb, v_w_proj_sb),
                                  ("w_proj_dn", recv[2], w_proj_dn, m_w_proj_dn, v_w_proj_dn),
                                  ("w_out", recv[3], w_out, m_w_out, v_w_out),
                                  ("w_ffn_in", recv[4], w_ffn_in, m_w_ffn_in, v_w_ffn_in),
                                  ("w_ffn_out", recv[5], w_ffn_out, m_w_ffn_out, v_w_ffn_out)):
        big[nm] = [t[None] for t in _sum_adamw(parts, w_[0], m_[0], v_[0], "adamw_" + nm)]

    sg = dict(zip(["b_ada", "norm1_w", "dn_A_log", "dn_dt_bias", "dn_norm_w", "norm2_w", "ffn_conv_b", "final_norm_w",
                   "dn_conv_w", "ffn_conv_w"], range(10)))

    def small_out(table, nm):
        val = table[sg[nm]]
        return val[None] if nm in ("dn_conv_w", "ffn_conv_w") else val

    order = ["w_ada", "b_ada", "norm1_w", "w_in", "dn_conv_w", "dn_A_log", "dn_dt_bias", "dn_norm_w", "w_proj_sb",
             "w_proj_dn", "w_out", "norm2_w", "w_ffn_in", "ffn_conv_w", "ffn_conv_b", "w_ffn_out", "final_norm_w"]
    groups = []
    for k, small_table in enumerate((small_grads, small_delta, small_newm, small_newv)):
        row = []
        for nm in order:
            if nm == "w_ada":
                row.append((g_w_ada, ada_delta, ada_m, ada_v)[k][None])
            elif nm in big:
                row.append(big[nm][k])
            else:
                row.append(small_out(small_table, nm))
        groups.append(row)
    return (loss, grad_x[None], *groups[0], *groups[1], *groups[2], *groups[3])
```

```python
import functools

import jax
import jax.numpy as jnp
from jax import lax
from jax.experimental import pallas as pl
from jax.experimental.pallas import tpu as pltpu

f32 = jnp.float32
bf16 = jnp.bfloat16

D_MODEL = 1024
SB_HEADS = 8
SB_HEAD_DIM = 64
SB_WIDTH = SB_HEADS * SB_HEAD_DIM
SB_QBLOCK = 128
DN_HEADS = 8
DN_KEY_DIM = 64
DN_VAL_DIM = 128
DN_QK_WIDTH = DN_HEADS * DN_KEY_DIM
DN_V_WIDTH = DN_HEADS * DN_VAL_DIM
DN_CONV_CH = 2 * DN_QK_WIDTH + DN_V_WIDTH
DN_CONV_WIDTH = 4
DN_CHUNK = 64
D_FF = 2816
FFN_CONV_WIDTH = 3
NORM_EPS = 1e-6
L2_EPS = 1e-6
ADAM_LR = 0.001
ADAM_B1 = 0.9
ADAM_B2 = 0.999
ADAM_EPS = 1e-08
ADAM_WD = 0.01
ADAM_STEP = 10

N_DEV = 8
MESH = pl.DeviceIdType.MESH

LANES = 128
SUBLANES = 8
VMEM_LIMIT = 48 * 1024 * 1024

OFF_GA = 0
OFF_GB = D_MODEL
OFF_DN = 2 * D_MODEL
OFF_Z = OFF_DN + DN_CONV_CH
OFF_SBQ = OFF_Z + DN_V_WIDTH
OFF_SBK = OFF_SBQ + SB_WIDTH
OFF_SBV = OFF_SBK + SB_WIDTH
MAIN_WIDTH = OFF_SBV + SB_WIDTH

TM = 512
TCONV_R = 512
TCONV_C = 1024
TCONV_FF = D_FF // 2
SB_PAIRS_FWD = 4
SB_PAIRS_BWD = 4
SB_DEAD = -106.0
SB_NEVER = -1e30


def _cparams(sem=None):
    return pltpu.CompilerParams(dimension_semantics=sem, vmem_limit_bytes=VMEM_LIMIT)


def _pick(n, cands):
    for c in cands:
        if n % c == 0:
            return c
    return n


def _my_pos():
    return lax.axis_index("x"), lax.axis_index("y"), lax.axis_index("c")


def _flip(v, bit):
    return 1 - v if bit else v


def _comm_scratch(n):
    return [pltpu.SemaphoreType.DMA((n, 7)), pltpu.SemaphoreType.DMA((n, 7)), pltpu.SemaphoreType.DMA((n,))]


def _gather_protocol(ins, outs, send_sems, recv_sems, local_sems):
    n = len(ins)
    x, y, c = _my_pos()
    me, sibling = (x, y, c), (x, y, 1 - c)
    chips = [(1 - x, y), (x, 1 - y), (1 - x, 1 - y)]

    def slot(out, pos):
        return out.at[4 * pos[0] + 2 * pos[1] + pos[2]]

    def copy(a, k, block, to, src=None):
        return pltpu.make_async_remote_copy(
            src_ref=slot(outs[a], block) if src is None else src, dst_ref=slot(outs[a], block),
            send_sem=send_sems.at[a, k], recv_sem=recv_sems.at[a, k], device_id=to, device_id_type=MESH)

    def local(a):
        return pltpu.make_async_copy(ins[a], slot(outs[a], me), local_sems.at[a])

    def first(a):
        return [copy(a, 0, me, sibling, src=ins[a])] + [copy(a, 1 + j, me, (*chip, c), src=ins[a])
                                                         for j, chip in enumerate(chips)]

    def start():
        for a in range(n):
            local(a).start()
            for cp in first(a):
                cp.start()

    def finish():
        forwards = []
        for a in range(n):
            for j, chip in enumerate(chips):
                copy(a, 1 + j, (*chip, c), me).wait_recv()
                fwd = copy(a, 4 + j, (*chip, c), sibling)
                fwd.start()
                forwards.append(fwd)
        for a in range(n):
            copy(a, 0, sibling, me).wait_recv()
            for j, chip in enumerate(chips):
                copy(a, 4 + j, (*chip, 1 - c), me).wait_recv()
        for a in range(n):
            for cp in first(a):
                cp.wait_send()
        for cp in forwards:
            cp.wait_send()
        for a in range(n):
            local(a).wait()

    return start, finish


def _exchange_protocol(ins, outs, send_sems, recv_sems, local_sems):
    n = len(ins)
    x, y, c = _my_pos()
    me_idx = 4 * x + 2 * y + c

    def local(a):
        return pltpu.make_async_copy(ins[a].at[me_idx], outs[a].at[me_idx], local_sems.at[a])

    def copies(a, m):
        peer = (_flip(x, m & 4), _flip(y, m & 2), _flip(c, m & 1))
        peer_idx = 4 * peer[0] + 2 * peer[1] + peer[2]
        sems = dict(send_sem=send_sems.at[a, m - 1], recv_sem=recv_sems.at[a, m - 1], device_id=peer,
                    device_id_type=MESH)
        send = pltpu.make_async_remote_copy(src_ref=ins[a].at[peer_idx], dst_ref=outs[a].at[me_idx], **sems)
        recv = pltpu.make_async_remote_copy(src_ref=ins[a].at[peer_idx], dst_ref=outs[a].at[peer_idx], **sems)
        return send, recv

    def start():
        for a in range(n):
            local(a).start()
            for m in range(1, N_DEV):
                copies(a, m)[0].start()

    def finish():
        for a in range(n):
            for m in range(1, N_DEV):
                copies(a, m)[1].wait_recv()
        for a in range(n):
            for m in range(1, N_DEV):
                copies(a, m)[0].wait_send()
            local(a).wait()

    return start, finish


def _collective_call(protocol, arrs, out_shapes, name):
    n = len(arrs)

    def body(*refs):
        start, finish = protocol(refs[:n], refs[n:2 * n], *refs[2 * n:])
        start()
        finish()

    any_spec = pl.BlockSpec(memory_space=pl.ANY)
    return pl.pallas_call(body, name=name, out_shape=out_shapes, in_specs=[any_spec] * n, out_specs=[any_spec] * n,
                          scratch_shapes=_comm_scratch(n))(*arrs)


def _gathered_shapes(arrs):
    return [jax.ShapeDtypeStruct((N_DEV,) + a.shape, a.dtype) for a in arrs]


def _all_gather(arrs, name):
    return _collective_call(_gather_protocol, arrs, _gathered_shapes(arrs), name)


MM_BLOCK_BYTES = 7 * 1024 * 1024
MM_TILE_CAP = 1664


def _lane_tile(n, cap):
    fits = [t for t in range(LANES, min(n, cap) + 1, LANES) if n % t == 0]
    return max(fits) if fits else n


def _mm_tiles(m_dim, n_dim, k_dim, a_bytes, b_bytes):
    tm = _lane_tile(m_dim, MM_TILE_CAP)
    tn = _lane_tile(n_dim, MM_TILE_CAP)
    while tm * tn * 4 > MM_BLOCK_BYTES:
        if tn >= tm and (tn // 2) % LANES == 0:
            tn //= 2
        else:
            tm //= 2
    if (m_dim % (2 * tm) == 0 and 2 * tm * k_dim * a_bytes <= MM_BLOCK_BYTES
            and 2 * tm * tn * 4 <= MM_BLOCK_BYTES):
        tm *= 2
    tk = k_dim
    if k_dim % LANES == 0:
        units = k_dim // LANES
        fits = [u for u in range(1, units + 1) if units % u == 0
                and u * LANES * max(tm * a_bytes, tn * b_bytes) <= MM_BLOCK_BYTES]
        tk = max(fits) * LANES
    return tm, tn, tk


def _mm(a, b, *, ta=False, tb=False, name, side=None):
    (k_dim, m_dim) = a.shape if ta else a.shape[::-1]
    (n_dim, kb_dim) = b.shape if tb else b.shape[::-1]
    assert k_dim == kb_dim, (a.shape, b.shape, ta, tb)
    tm, tn, tk = _mm_tiles(m_dim, n_dim, k_dim, a.dtype.itemsize, b.dtype.itemsize)
    nk = k_dim // tk
    grid = (m_dim // tm, n_dim // tn, nk)
    dims = (((0 if ta else 1,), (1 if tb else 0,)), ((), ()))
    ns = side.n if side else 0

    def body(a_ref, b_ref, *rest):
        if side:
            side.run(rest[:ns], rest[ns + 1:2 * ns + 1], rest[2 * ns + 1:], *_grid_ends(grid),
                     lambda: compute(a_ref, b_ref, rest[ns]))
        else:
            compute(a_ref, b_ref, rest[0])

    def compute(a_ref, b_ref, o_ref):
        part = lax.dot_general(a_ref[...].astype(bf16), b_ref[...].astype(bf16), dims, preferred_element_type=f32)
        if nk == 1:
            o_ref[...] = part
        else:
            k = pl.program_id(2)

            @pl.when(k == 0)
            def _():
                o_ref[...] = part

            @pl.when(k > 0)
            def _():
                o_ref[...] += part

    a_spec = pl.BlockSpec((tk, tm), lambda i, j, k: (k, i)) if ta else pl.BlockSpec((tm, tk), lambda i, j, k: (i, k))
    b_spec = pl.BlockSpec((tn, tk), lambda i, j, k: (j, k)) if tb else pl.BlockSpec((tk, tn), lambda i, j, k: (k, j))
    out_spec = pl.BlockSpec((tm, tn), lambda i, j, k: (i, j))
    out_shape = jax.ShapeDtypeStruct((m_dim, n_dim), f32)
    if not side:
        return pl.pallas_call(body, name=name, grid=grid, in_specs=[a_spec, b_spec], out_specs=out_spec,
                              out_shape=out_shape,
                              compiler_params=_cparams(("parallel", "parallel", "arbitrary")))(a, b)
    return pl.pallas_call(
        body, name=name, grid=grid, in_specs=[a_spec, b_spec] + side.specs(), out_specs=[out_spec] + side.specs(),
        out_shape=[out_shape] + side.out_shapes, scratch_shapes=_comm_scratch(ns),
        compiler_params=_cparams(("arbitrary", "arbitrary", "arbitrary")))(a, b, *side.arrs)


def _win(t):
    return t if isinstance(t, tuple) else (t, t.shape[1], 0)


def _tile_spec(width, cb, tm):
    return pl.BlockSpec((tm, width), lambda i: (i, cb))


def _param_spec(p):
    return pl.BlockSpec(p.shape, lambda i: (0, 0))


def _stage_fwd(f, params, tiles, out_dtypes, name):
    tiles = [_win(t) for t in tiles]
    rows = tiles[0][0].shape[0]
    tm = min(TM, rows)
    avals = jax.eval_shape(f, *[jax.ShapeDtypeStruct(p.shape, f32) for p in params],
                           *[jax.ShapeDtypeStruct((tm, w), f32) for _, w, _ in tiles])
    n_p, n_t = len(params), len(tiles)

    def body(*refs):
        p = [r[...] for r in refs[:n_p]]
        t = [r[...].astype(f32) for r in refs[n_p:n_p + n_t]]
        for o_ref, val in zip(refs[n_p + n_t:], f(*p, *t)):
            o_ref[...] = val.astype(o_ref.dtype)

    return pl.pallas_call(
        body, name=name, grid=(rows // tm,),
        in_specs=[_param_spec(p) for p in params] + [_tile_spec(w, cb, tm) for _, w, cb in tiles],
        out_specs=[_tile_spec(a.shape[1], 0, tm) for a in avals],
        out_shape=[jax.ShapeDtypeStruct((rows, a.shape[1]), dt) for a, dt in zip(avals, out_dtypes)],
        compiler_params=_cparams(("parallel",)),
    )(*params, *[t[0] for t in tiles])


def _stage_bwd(f, params, tiles, cts, grad_dtypes, name, residual=None):
    tiles = [_win(t) for t in tiles]
    rows = tiles[0][0].shape[0]
    tm = min(TM, rows)
    cts = [list(g) if isinstance(g, (list, tuple)) else [g] for g in cts]
    flat_cts = [a for g in cts for a in g]
    n_p, n_t, n_c = len(params), len(tiles), len(flat_cts)
    has_res = residual is not None
    want = [j for j, dt in enumerate(grad_dtypes) if dt is not None]

    def body(*refs):
        i = pl.program_id(0)
        p = [r[...] for r in refs[:n_p]]
        t = [r[...].astype(f32) for r in refs[n_p:n_p + n_t]]
        ct_vals = [r[...].astype(f32) for r in refs[n_p + n_t:n_p + n_t + n_c]]
        ct, at = [], 0
        for g in cts:
            ct.append(functools.reduce(jnp.add, ct_vals[at:at + len(g)]))
            at += len(g)
        ct = tuple(ct)
        pos = n_p + n_t + n_c
        res_ref = refs[pos] if has_res else None
        pos += 1 if has_res else 0
        dp_refs = refs[pos:pos + n_p]
        dt_refs = refs[pos + n_p:]
        _, vjp = jax.vjp(f, *p, *t)
        grads = vjp(ct)

        @pl.when(i == 0)
        def _():
            for r in dp_refs:
                r[...] = jnp.zeros_like(r)

        for r, g in zip(dp_refs, grads[:n_p]):
            r[...] += g
        for r, j in zip(dt_refs, want):
            g = grads[n_p + j]
            if has_res and j == residual[0]:
                g = g + res_ref[...].astype(f32)
            r[...] = g.astype(r.dtype)

    in_arrays = list(params) + [t[0] for t in tiles] + flat_cts
    in_specs = ([_param_spec(p) for p in params] + [_tile_spec(w, cb, tm) for _, w, cb in tiles]
                + [_tile_spec(c.shape[1], 0, tm) for c in flat_cts])
    if has_res:
        in_arrays.append(residual[1])
        in_specs.append(_tile_spec(residual[1].shape[1], 0, tm))
    out_shape = ([jax.ShapeDtypeStruct(p.shape, f32) for p in params]
                 + [jax.ShapeDtypeStruct((rows, tiles[j][1]), grad_dtypes[j]) for j in want])
    out_specs = [_param_spec(p) for p in params] + [_tile_spec(tiles[j][1], 0, tm) for j in want]
    outs = pl.pallas_call(
        body, name=name, grid=(rows // tm,), in_specs=in_specs, out_specs=out_specs, out_shape=out_shape,
        compiler_params=_cparams(("arbitrary",)),
    )(*in_arrays)
    return outs[:n_p], outs[n_p:]


def _rms(x, w):
    return x * lax.rsqrt(jnp.mean(x * x, axis=-1, keepdims=True) + NORM_EPS) * w


def _f_normmod(w, shift, scale, x):
    return (_rms(x, w) * (1.0 + scale) + shift,)


def _f_merge(ga, gb, pa, pb):
    return (jax.nn.sigmoid(ga) * pa + jax.nn.sigmoid(gb) * pb,)


def _f_residual_normmod(gate, w, shift, scale, x, branch):
    x1 = x + gate * branch
    return x1, _rms(x1, w) * (1.0 + scale) + shift


def _f_loss(gate, wf, x1, fo, target):
    y = _rms(x1 + gate * fo, wf)
    err = jnp.square(y - target)
    return (0.5 * jnp.sum(jnp.mean(err, axis=-1, keepdims=True), axis=0, keepdims=True),)


def _loss_and_grads(gate2, wf, x1, fo, target):
    rows, d = x1.shape
    tm = min(TM, rows)

    def body(g_ref, w_ref, x_ref, fo_ref, t_ref, loss_ref, dg_ref, dw_ref, dx_ref, dfo_ref):
        i = pl.program_id(0)
        (val,), vjp = jax.vjp(_f_loss, g_ref[...], w_ref[...], x_ref[...], fo_ref[...], t_ref[...])
        dg, dw, dx, dfo, _ = vjp((jnp.ones((1, 1), f32),))

        @pl.when(i == 0)
        def _():
            loss_ref[...] = jnp.zeros_like(loss_ref)
            dg_ref[...] = jnp.zeros_like(dg_ref)
            dw_ref[...] = jnp.zeros_like(dw_ref)

        loss_ref[...] += jnp.broadcast_to(val, loss_ref.shape)
        dg_ref[...] += dg
        dw_ref[...] += dw
        dx_ref[...] = dx
        dfo_ref[...] = dfo.astype(bf16)

    vec = pl.BlockSpec((1, d), lambda i: (0, 0))
    tile = pl.BlockSpec((tm, d), lambda i: (i, 0))
    return pl.pallas_call(
        body, name="loss_fwd_bwd", grid=(rows // tm,),
        in_specs=[vec, vec, tile, tile, tile],
        out_specs=[pl.BlockSpec((1, LANES), lambda i: (0, 0)), vec, vec, tile, tile],
        out_shape=[jax.ShapeDtypeStruct((1, LANES), f32), jax.ShapeDtypeStruct((1, d), f32),
                   jax.ShapeDtypeStruct((1, d), f32), jax.ShapeDtypeStruct((rows, d), f32),
                   jax.ShapeDtypeStruct((rows, d), bf16)],
        compiler_params=_cparams(("arbitrary",)),
    )(gate2, wf, x1, fo, target)


def _softplus(z):
    return jnp.maximum(z, 0.0) + jnp.log(1.0 + jnp.exp(-jnp.abs(z)))


def _split_dot(a, m):
    hi = a.astype(bf16)
    lo = (a - hi.astype(f32)).astype(bf16)
    return jnp.dot(hi, m, preferred_element_type=f32) + jnp.dot(lo, m, preferred_element_type=f32)


def _suffix_matrix(n):
    r = lax.broadcasted_iota(jnp.int32, (n, n), 0)
    c = lax.broadcasted_iota(jnp.int32, (n, n), 1)
    return (r > c).astype(bf16)


def _head_masks():
    lane = lax.broadcasted_iota(jnp.int32, (1, LANES), 1)
    return [(lane < SB_HEAD_DIM).astype(f32), (lane >= SB_HEAD_DIM).astype(f32)]


def _sb_prepare(proj):
    def f(k, v):
        lane = lax.broadcasted_iota(jnp.int32, (1, SB_WIDTH), 1)
        m0 = (jnp.bitwise_and(lane, LANES - 1) < SB_HEAD_DIM).astype(f32)
        m1 = 1.0 - m0
        return k, k * m0, k * m1, v, v * m0, v * m1

    wins = [(proj, SB_WIDTH, OFF_SBK // SB_WIDTH), (proj, SB_WIDTH, OFF_SBV // SB_WIDTH)]
    return _stage_fwd(f, [], wins, [bf16] * 6, "sb_prepare")


def _stack_heads(x):
    m0, m1 = _head_masks()
    return jnp.concatenate([x * m0, x * m1], axis=0)


def _sb_logits(qst, k, t_pos2, kb, bq, masked):
    z = lax.dot_general(qst, k, (((1,), (1,)), ((), ())), preferred_element_type=f32)
    l = -_softplus(z)
    if masked:
        s_pos = kb * bq + lax.broadcasted_iota(jnp.int32, (1, bq), 1)
        causal = s_pos < t_pos2
        l = jnp.where(causal, l, 0.0)
    else:
        causal = None
    return z, l, causal


class _SideComm:
    def __init__(self, protocol, arrs, out_shapes):
        self.protocol, self.arrs, self.out_shapes = protocol, list(arrs), list(out_shapes)
        self.n = len(self.arrs)

    def specs(self):
        return [pl.BlockSpec(memory_space=pl.ANY)] * self.n

    def run(self, in_refs, out_refs, sems, first, last, compute):
        start, finish = self.protocol(in_refs, out_refs, *sems)
        pl.when(first)(start)
        compute()
        pl.when(last)(finish)


def _grid_ends(grid):
    ids = [pl.program_id(axis) for axis in range(len(grid))]
    first = functools.reduce(jnp.logical_and, [i == 0 for i in ids])
    last = functools.reduce(jnp.logical_and, [i == g - 1 for i, g in zip(ids, grid)])
    return first, last


def _sb_attention_fwd2(proj, k16, v0_16, v1_16, side=None):
    rows = proj.shape[0]
    bq = SB_QBLOCK
    nq = rows // bq
    assert nq <= LANES, "one lane per key block"
    npair = SB_WIDTH // LANES
    scale = SB_HEAD_DIM ** -0.5

    npp = SB_PAIRS_FWD
    wq = npp * LANES
    grid = (npair // npp, nq)
    ns = side.n if side else 0

    def body(q_ref, k_ref, v0_ref, v1_ref, *rest):
        o_ref, runs_ref = rest[ns], rest[ns + 1]
        if side:
            side.run(rest[:ns], rest[ns + 2:2 * ns + 2], rest[2 * ns + 2:], *_grid_ends(grid),
                     lambda: compute(q_ref, k_ref, v0_ref, v1_ref, o_ref, runs_ref))
        else:
            compute(q_ref, k_ref, v0_ref, v1_ref, o_ref, runs_ref)

    def compute(q_ref, k_ref, v0_ref, v1_ref, o_ref, runs_ref):
        qi = pl.program_id(1)
        pairs = [slice(pp * LANES, (pp + 1) * LANES) for pp in range(npp)]
        qst = [(_stack_heads(q_ref[:, s]) * scale).astype(bf16) for s in pairs]
        r = lax.broadcasted_iota(jnp.int32, (bq, 2 * bq), 0)
        c = lax.broadcasted_iota(jnp.int32, (bq, 2 * bq), 1)
        m2 = jnp.logical_or(r > c, c >= bq).astype(bf16)
        t_pos = qi * bq + lax.broadcasted_iota(jnp.int32, (bq, 1), 0)
        t_pos2 = jnp.concatenate([t_pos, t_pos], axis=0)
        lane = lax.broadcasted_iota(jnp.int32, (1, LANES), 1)
        runs_ref[...] = jnp.full(runs_ref.shape, SB_NEVER, f32)

        def tiles(kbs, carry, masked):
            jobs = [(pp, kb) for kb in kbs for pp in range(npp)]
            rows_k = [pl.ds(pl.multiple_of(kb * bq, bq), bq) for _, kb in jobs]
            zl = [_sb_logits(qst[pp], k_ref[rk, pairs[pp]], t_pos2, kb, bq, masked) for (pp, kb), rk in zip(jobs, rows_k)]
            cs = [_split_dot(l, m2) for _, l, _ in zl]
            run = [cr[0] for cr in carry]
            acc = [cr[1] for cr in carry]
            probs = []
            for (pp, kb), (z, l, causal), cs2 in zip(jobs, zl, cs):
                a = jnp.exp(z + l + cs2[:, :bq] + run[pp])
                if masked:
                    a = jnp.where(causal, a, 0.0)
                probs.append(a.astype(bf16))
                for hh in range(2):
                    cols = slice((2 * pp + hh) * LANES, (2 * pp + hh + 1) * LANES)
                    runs_ref[:, cols] = jnp.where(lane == kb, run[pp][hh * bq:(hh + 1) * bq], runs_ref[:, cols])
                run[pp] = run[pp] + cs2[:, bq:]
            for (pp, kb), rk, ab in zip(jobs, rows_k, probs):
                acc[pp] = (acc[pp] + jnp.dot(ab[:bq], v0_ref[rk, pairs[pp]], preferred_element_type=f32)
                           + jnp.dot(ab[bq:], v1_ref[rk, pairs[pp]], preferred_element_type=f32))
            return tuple(zip(run, acc))

        zero = (jnp.zeros((2 * bq, bq), f32), jnp.zeros((bq, LANES), f32))
        carry = tiles([qi], (zero,) * npp, True)

        def alive(cr):
            return functools.reduce(jnp.maximum, [jnp.max(run) for run, _ in cr]) > SB_DEAD

        def two(state):
            i, _, cr = state
            cr = tiles([qi - 1 - 2 * i, qi - 2 - 2 * i], cr, False)
            return i + 1, alive(cr), cr

        n_two = qi // 2
        i_end, still, carry = lax.while_loop(lambda st: jnp.logical_and(st[0] < n_two, st[1]), two,
                                             (jnp.int32(0), alive(carry), carry))
        last_one = jnp.logical_and(qi % 2 == 1, jnp.logical_and(still, i_end == n_two))
        carry = lax.cond(last_one, lambda cr: tiles([0], cr, False), lambda cr: cr, carry)
        for pp in range(npp):
            o_ref[:, pairs[pp]] = carry[pp][1]

    kv = pl.BlockSpec((rows, wq), lambda p, i: (0, p))
    return pl.pallas_call(
        body, name="sb_attn_fwd", grid=grid,
        in_specs=[pl.BlockSpec((bq, wq), lambda p, i: (i, OFF_SBQ // wq + p)), kv, kv, kv] + (side.specs() if side else []),
        out_specs=[pl.BlockSpec((bq, wq), lambda p, i: (i, p)),
                   pl.BlockSpec((bq, 2 * wq), lambda p, i: (i, p))] + (side.specs() if side else []),
        out_shape=[jax.ShapeDtypeStruct((rows, SB_WIDTH), f32),
                   jax.ShapeDtypeStruct((rows, SB_HEADS * LANES), f32)] + (side.out_shapes if side else []),
        scratch_shapes=_comm_scratch(ns) if side else [],
        compiler_params=_cparams(("arbitrary", "arbitrary")),
    )(proj, k16, v0_16, v1_16, *(side.arrs if side else []))


def _sb_attention_bwd2(proj, k16, k0_16, k1_16, v16, runs, do, side=None):
    rows = proj.shape[0]
    bq = SB_QBLOCK
    nq = rows // bq
    npair = SB_WIDTH // LANES
    scale = SB_HEAD_DIM ** -0.5
    tn = (((0,), (0,)), ((), ()))
    nt = (((1,), (1,)), ((), ()))

    npp = SB_PAIRS_BWD
    wq = npp * LANES
    grid = (npair // npp, nq)
    ns = side.n if side else 0

    def body(q_ref, k_ref, k0_ref, k1_ref, v_ref, runs_ref, do_ref, *rest):
        outs = rest[ns:ns + 3]
        ins = (q_ref, k_ref, k0_ref, k1_ref, v_ref, runs_ref, do_ref)
        if side:
            side.run(rest[:ns], rest[ns + 3:2 * ns + 3], rest[2 * ns + 3:], *_grid_ends(grid),
                     lambda: compute(*ins, *outs))
        else:
            compute(*ins, *outs)

    def compute(q_ref, k_ref, k0_ref, k1_ref, v_ref, runs_ref, do_ref, dq_ref, dk_ref, dv_ref):
        qi = pl.program_id(1)

        @pl.when(qi == 0)
        def _():
            dk_ref[...] = jnp.zeros_like(dk_ref)
            dv_ref[...] = jnp.zeros_like(dv_ref)

        pairs = [slice(pp * LANES, (pp + 1) * LANES) for pp in range(npp)]
        qst = [(_stack_heads(q_ref[:, s]) * scale).astype(bf16) for s in pairs]
        dost = [_stack_heads(do_ref[:, s]).astype(bf16) for s in pairs]
        runs = [jnp.concatenate([runs_ref[:, 2 * pp * LANES:(2 * pp + 1) * LANES],
                                 runs_ref[:, (2 * pp + 1) * LANES:(2 * pp + 2) * LANES]], axis=0) for pp in range(npp)]
        r = lax.broadcasted_iota(jnp.int32, (bq, 2 * bq), 0)
        c = lax.broadcasted_iota(jnp.int32, (bq, 2 * bq), 1)
        suffix_m = _suffix_matrix(bq)
        m2 = jnp.logical_or(r < c, c >= bq).astype(bf16)
        t_pos = qi * bq + lax.broadcasted_iota(jnp.int32, (bq, 1), 0)
        t_pos2 = jnp.concatenate([t_pos, t_pos], axis=0)
        lane = lax.broadcasted_iota(jnp.int32, (1, LANES), 1)

        def tiles(kbs, carry, masked):
            jobs = [(pp, kb) for kb in kbs for pp in range(npp)]
            rows_k = [pl.ds(pl.multiple_of(kb * bq, bq), bq) for _, kb in jobs]
            zl = [_sb_logits(qst[pp], k_ref[rk, pairs[pp]], t_pos2, kb, bq, masked) for (pp, kb), rk in zip(jobs, rows_k)]
            das = [lax.dot_general(dost[pp], v_ref[rk, pairs[pp]], nt, preferred_element_type=f32)
                   for (pp, kb), rk in zip(jobs, rows_k)]
            sticks = [_split_dot(l, suffix_m) for _, l, _ in zl]
            probs, ps = [], []
            for (pp, kb), (z, l, causal), stick, da in zip(jobs, zl, sticks, das):
                run = jnp.sum(jnp.where(lane == kb, runs[pp], 0.0), axis=1, keepdims=True)
                a = jnp.exp(z + l + stick + run)
                if masked:
                    a = jnp.where(causal, a, 0.0)
                probs.append(a.astype(bf16))
                ps.append(da * a)
            pcs = [_split_dot(p, m2) for p in ps]
            pref = [cr[0] for cr in carry]
            dq_acc = [cr[1] for cr in carry]
            dzs = []
            for (pp, kb), (z, l, causal), p, pc2 in zip(jobs, zl, ps, pcs):
                dz = p * jnp.exp(l) - jnp.exp(z + l) * (pc2[:, :bq] + pref[pp])
                if masked:
                    dz = jnp.where(causal, dz, 0.0)
                dzs.append(dz.astype(bf16))
                pref[pp] = pref[pp] + pc2[:, bq:]
            for (pp, kb), rk, dzb, ab in zip(jobs, rows_k, dzs, probs):
                cols = pairs[pp]
                dq_acc[pp] = (dq_acc[pp] + jnp.dot(dzb[:bq], k0_ref[rk, cols], preferred_element_type=f32)
                              + jnp.dot(dzb[bq:], k1_ref[rk, cols], preferred_element_type=f32))
                dk_ref[rk, cols] += lax.dot_general(dzb, qst[pp], tn, preferred_element_type=f32)
                dv_ref[rk, cols] += lax.dot_general(ab, dost[pp], tn, preferred_element_type=f32)
            return tuple(zip(pref, dq_acc))

        zero = (jnp.zeros((2 * bq, bq), f32), jnp.zeros((bq, LANES), f32))
        colmax = functools.reduce(jnp.maximum, [jnp.max(x, axis=0, keepdims=True) for x in runs])
        live = jnp.logical_and(colmax > SB_DEAD, lane < qi)
        kb0 = jnp.minimum(jnp.min(jnp.where(live, lane, LANES)), qi)
        n_blocks = qi - kb0
        carry = lax.fori_loop(0, n_blocks // 2, lambda i, cr: tiles([kb0 + 2 * i, kb0 + 2 * i + 1], cr, False),
                              (zero,) * npp)
        carry = lax.cond(n_blocks % 2 == 1, lambda cr: tiles([qi - 1], cr, False), lambda cr: cr, carry)
        carry = tiles([qi], carry, True)
        for pp in range(npp):
            dq_ref[:, pairs[pp]] = (carry[pp][1] * scale).astype(dq_ref.dtype)

    blk = pl.BlockSpec((bq, wq), lambda p, i: (i, p))
    full = pl.BlockSpec((rows, wq), lambda p, i: (0, p), pipeline_mode=pl.Buffered(1))
    return pl.pallas_call(
        body, name="sb_attn_bwd", grid=grid,
        in_specs=[pl.BlockSpec((bq, wq), lambda p, i: (i, OFF_SBQ // wq + p)), full, full, full, full,
                  pl.BlockSpec((bq, 2 * wq), lambda p, i: (i, p)), blk] + (side.specs() if side else []),
        out_specs=[blk, full, full] + (side.specs() if side else []),
        out_shape=[jax.ShapeDtypeStruct((rows, SB_WIDTH), bf16), jax.ShapeDtypeStruct((rows, SB_WIDTH), f32),
                   jax.ShapeDtypeStruct((rows, SB_WIDTH), f32)] + (side.out_shapes if side else []),
        scratch_shapes=_comm_scratch(ns) if side else [],
        compiler_params=_cparams(("arbitrary", "arbitrary")),
    )(proj, k16, k0_16, k1_16, v16, runs, do, *(side.arrs if side else []))


def _shift_down(x, prev8, j):
    if j == 0:
        return x
    r = pltpu.roll(x, j, axis=0)
    row8 = lax.broadcasted_iota(jnp.int32, prev8.shape, 0)
    head = jnp.where(row8 < j, pltpu.roll(prev8, j, axis=0), r[0:SUBLANES])
    return jnp.concatenate([head, r[SUBLANES:]], axis=0)


def _shift_up(x, next8, j):
    if j == 0:
        return x
    n = x.shape[0]
    r = pltpu.roll(x, n - j, axis=0)
    row8 = lax.broadcasted_iota(jnp.int32, next8.shape, 0)
    tail = jnp.where(row8 >= SUBLANES - j, pltpu.roll(next8, SUBLANES - j, axis=0), r[n - SUBLANES:n])
    return jnp.concatenate([r[:n - SUBLANES], tail], axis=0)


def _conv(x, prev8, w):
    k_taps = w.shape[0]
    out = x * w[k_taps - 1:k_taps, :]
    for j in range(1, k_taps):
        out = out + _shift_down(x, prev8, j) * w[k_taps - 1 - j:k_taps - j, :]
    return out


def _conv_tiles(rows):
    tr = min(TCONV_R, rows)
    return tr, rows // tr, tr // SUBLANES


def _prev_spec(tc, cb0, r8):
    return pl.BlockSpec((SUBLANES, tc), lambda j, i: (jnp.maximum(i * r8 - 1, 0), cb0 + j))


def _silu(x):
    return x * jax.nn.sigmoid(x)


def _dsilu(x):
    s = jax.nn.sigmoid(x)
    return s * (1.0 + x * (1.0 - s))


def _dn_conv_fwd(proj, w):
    rows = proj.shape[0]
    tr, nr, r8 = _conv_tiles(rows)
    tc = TCONV_C
    cb0 = OFF_DN // tc

    def body(x_ref, p_ref, w_ref, o_ref):
        prev = jnp.where(pl.program_id(1) == 0, 0.0, p_ref[...])
        o_ref[...] = _silu(_conv(x_ref[...], prev, w_ref[...]))

    return pl.pallas_call(
        body, name="dn_conv_fwd", grid=(DN_CONV_CH // tc, nr),
        in_specs=[pl.BlockSpec((tr, tc), lambda j, i: (i, cb0 + j)), _prev_spec(tc, cb0, r8),
                  pl.BlockSpec((DN_CONV_WIDTH, tc), lambda j, i: (0, j))],
        out_specs=pl.BlockSpec((tr, tc), lambda j, i: (i, j)),
        out_shape=jax.ShapeDtypeStruct((rows, DN_CONV_CH), f32),
        compiler_params=_cparams(("parallel", "parallel")),
    )(proj, proj, w)


def _dn_conv_bwd_act(proj, w, dact):
    rows = proj.shape[0]
    tr, nr, r8 = _conv_tiles(rows)
    tc = TCONV_C
    cb0 = OFF_DN // tc

    def body(x_ref, p_ref, w_ref, d_ref, o_ref):
        prev = jnp.where(pl.program_id(1) == 0, 0.0, p_ref[...])
        o_ref[...] = d_ref[...] * _dsilu(_conv(x_ref[...], prev, w_ref[...]))

    return pl.pallas_call(
        body, name="dn_conv_bwd_act", grid=(DN_CONV_CH // tc, nr),
        in_specs=[pl.BlockSpec((tr, tc), lambda j, i: (i, cb0 + j)), _prev_spec(tc, cb0, r8),
                  pl.BlockSpec((DN_CONV_WIDTH, tc), lambda j, i: (0, j)),
                  pl.BlockSpec((tr, tc), lambda j, i: (i, j))],
        out_specs=pl.BlockSpec((tr, tc), lambda j, i: (i, j)),
        out_shape=jax.ShapeDtypeStruct((rows, DN_CONV_CH), f32),
        compiler_params=_cparams(("parallel", "parallel")),
    )(proj, proj, w, dact)


def _ffn_conv_fwd(u_pre, w, b):
    rows = u_pre.shape[0]
    tr, nr, r8 = _conv_tiles(rows)
    tc = TCONV_FF
    nct = D_FF // tc

    def body(xg_ref, pg_ref, xu_ref, pu_ref, wg_ref, wu_ref, bg_ref, bu_ref, o_ref, u_ref):
        first = pl.program_id(1) == 0
        ug = _conv(xg_ref[...], jnp.where(first, 0.0, pg_ref[...]), wg_ref[...]) + bg_ref[...]
        uu = _conv(xu_ref[...], jnp.where(first, 0.0, pu_ref[...]), wu_ref[...]) + bu_ref[...]
        o_ref[...] = (_silu(ug) * uu).astype(o_ref.dtype)
        u_ref[0] = ug
        u_ref[1] = uu

    def x_spec(off):
        return pl.BlockSpec((tr, tc), lambda j, i: (i, off + j))

    def w_spec(k, off):
        return pl.BlockSpec((k, tc), lambda j, i: (0, off + j))

    return pl.pallas_call(
        body, name="ffn_conv_fwd", grid=(nct, nr),
        in_specs=[x_spec(0), _prev_spec(tc, 0, r8), x_spec(nct), _prev_spec(tc, nct, r8),
                  w_spec(FFN_CONV_WIDTH, 0), w_spec(FFN_CONV_WIDTH, nct), w_spec(1, 0), w_spec(1, nct)],
        out_specs=[pl.BlockSpec((tr, tc), lambda j, i: (i, j)), pl.BlockSpec((2, tr, tc), lambda j, i: (0, i, j))],
        out_shape=[jax.ShapeDtypeStruct((rows, D_FF), bf16), jax.ShapeDtypeStruct((2, rows, D_FF), f32)],
        compiler_params=_cparams(("parallel", "parallel")),
    )(u_pre, u_pre, u_pre, u_pre, w, w, b, b)


def _ffn_conv_bwd_act(u, dact):
    rows = dact.shape[0]
    tr, nr, _ = _conv_tiles(rows)
    tc = TCONV_FF
    nct = D_FF // tc

    def body(u_ref, d_ref, du_ref, dbg_ref, dbu_ref):
        first = pl.program_id(1) == 0
        ug = u_ref[0]
        uu = u_ref[1]
        d = d_ref[...]
        sig = jax.nn.sigmoid(ug)
        dug = d * uu * (sig * (1.0 + ug * (1.0 - sig)))
        duu = d * (ug * sig)
        du_ref[0] = dug
        du_ref[1] = duu

        @pl.when(first)
        def _():
            dbg_ref[...] = jnp.zeros_like(dbg_ref)
            dbu_ref[...] = jnp.zeros_like(dbu_ref)

        dbg_ref[...] += jnp.sum(dug, axis=0, keepdims=True)
        dbu_ref[...] += jnp.sum(duu, axis=0, keepdims=True)

    pair = pl.BlockSpec((2, tr, tc), lambda j, i: (0, i, j))
    vec = pl.BlockSpec((1, tc), lambda j, i: (0, j))
    return pl.pallas_call(
        body, name="ffn_conv_bwd_act", grid=(nct, nr),
        in_specs=[pair, pl.BlockSpec((tr, tc), lambda j, i: (i, j))],
        out_specs=[pair, vec, vec],
        out_shape=[jax.ShapeDtypeStruct((2, rows, D_FF), f32),
                   jax.ShapeDtypeStruct((1, D_FF), f32), jax.ShapeDtypeStruct((1, D_FF), f32)],
        compiler_params=_cparams(("parallel", "arbitrary")),
    )(u, dact)


def _conv_bwd(dy, x, x_cb0, w, name):
    k_taps = w.shape[0]
    split = dy.ndim == 3
    rows = dy.shape[-2]
    ch = dy.shape[-1] * (2 if split else 1)
    tc = TCONV_FF if split else TCONV_C
    tr, nr, r8 = _conv_tiles(rows)
    per_half = dy.shape[-1] // tc
    last8 = rows // SUBLANES - 1

    def body(dy_ref, nx_ref, x_ref, w_ref, dx_ref, dw_ref):
        i = pl.program_id(1)
        dyv = dy_ref[...]
        nxt = jnp.where(i == nr - 1, 0.0, nx_ref[...])
        xv = x_ref[...].astype(f32)
        wv = w_ref[...]

        @pl.when(i == 0)
        def _():
            dw_ref[...] = jnp.zeros_like(dw_ref)

        dx = dyv * wv[k_taps - 1:k_taps, :]
        dw_ref[k_taps - 1:k_taps, :] += jnp.sum(dyv * xv, axis=0, keepdims=True)
        for j in range(1, k_taps):
            dy_j = _shift_up(dyv, nxt, j)
            dx = dx + dy_j * wv[k_taps - 1 - j:k_taps - j, :]
            dw_ref[k_taps - 1 - j:k_taps - j, :] += jnp.sum(dy_j * xv, axis=0, keepdims=True)
        dx_ref[...] = dx.astype(dx_ref.dtype)

    tile = pl.BlockSpec((tr, tc), lambda j, i: (i, j))
    if split:
        dy_spec = pl.BlockSpec((None, tr, tc), lambda j, i: (j // per_half, i, j % per_half))
        next_spec = pl.BlockSpec((None, SUBLANES, tc),
                                 lambda j, i: (j // per_half, jnp.minimum((i + 1) * r8, last8), j % per_half))
    else:
        dy_spec = tile
        next_spec = pl.BlockSpec((SUBLANES, tc), lambda j, i: (jnp.minimum((i + 1) * r8, last8), j))
    return pl.pallas_call(
        body, name=name, grid=(ch // tc, nr),
        in_specs=[dy_spec, next_spec, pl.BlockSpec((tr, tc), lambda j, i: (i, x_cb0 + j)),
                  pl.BlockSpec((k_taps, tc), lambda j, i: (0, j))],
        out_specs=[tile, pl.BlockSpec((k_taps, tc), lambda j, i: (0, j))],
        out_shape=[jax.ShapeDtypeStruct((rows, ch), bf16), jax.ShapeDtypeStruct((k_taps, ch), f32)],
        compiler_params=_cparams(("parallel", "arbitrary")),
    )(dy, dy, x, w)


def _hdot(a, b):
    return jnp.dot(a, b, preferred_element_type=f32, precision=lax.Precision.HIGH)


def _xdot(a, b):
    return jnp.dot(a, b, preferred_element_type=f32, precision=lax.Precision.HIGHEST)


def _bdot(a, b):
    return jnp.dot(a.astype(bf16), b.astype(bf16), preferred_element_type=f32)


def _bdot_nt(a, b):
    return lax.dot_general(a.astype(bf16), b.astype(bf16), (((1,), (1,)), ((), ())), preferred_element_type=f32)


def _bdot_tn(a, b):
    return lax.dot_general(a.astype(bf16), b.astype(bf16), (((0,), (0,)), ((), ())), preferred_element_type=f32)


GDN_GROUP = 4
GDN_NGROUPS = DN_HEADS // GDN_GROUP
GDN_ROWS = GDN_GROUP * DN_CHUNK
GDN_QK_LANES = GDN_GROUP * DN_KEY_DIM
GDN_LOGIT_LANE = DN_HEADS


def _inverse_impl(lows):
    n = lows[0].shape[0]
    r = lax.broadcasted_iota(jnp.int32, (n, n), 0)
    c = lax.broadcasted_iota(jnp.int32, (n, n), 1)
    eye = (r == c).astype(f32)
    blk = jnp.right_shift(r, 3) == jnp.right_shift(c, 3)
    d = [jnp.where(blk, low, 0.0) for low in lows]
    e = [low - x for low, x in zip(lows, d)]

    def nilpotent8_inverse(xs):
        acc = [eye - x for x in xs]
        power = xs
        for _ in range(2):
            power = [_bdot(x, x) for x in power]
            acc = [_bdot(a, eye + x) for a, x in zip(acc, power)]
        return acc

    dinv = nilpotent8_inverse(d)
    ninv = nilpotent8_inverse([_bdot(x, y) for x, y in zip(dinv, e)])
    t = [_bdot(x, y) for x, y in zip(ninv, dinv)]
    for _ in range(2):
        res = [eye - x - _hdot(low, x) for low, x in zip(lows, t)]
        t = [x + _bdot(x, y) for x, y in zip(t, res)]
    return tuple(t)


@jax.custom_vjp
def _unit_lower_inverses(lows):
    return _inverse_impl(lows)


def _unit_lower_inverses_fwd(lows):
    t = _inverse_impl(lows)
    return t, t


def _unit_lower_inverses_bwd(t, ct):
    tn = (((0,), (0,)), ((), ()))
    nt = (((1,), (1,)), ((), ()))
    left = [lax.dot_general(x, g, tn, preferred_element_type=f32, precision=lax.Precision.HIGH) for x, g in zip(t, ct)]
    return (tuple(-lax.dot_general(x, y, nt, preferred_element_type=f32, precision=lax.Precision.HIGH)
                  for x, y in zip(left, t)),)


_unit_lower_inverses.defvjp(_unit_lower_inverses_fwd, _unit_lower_inverses_bwd)


@jax.custom_vjp
def _known_inverses(lows, t):
    return t


def _known_inverses_fwd(lows, t):
    return t, t


def _known_inverses_bwd(t, ct):
    return _unit_lower_inverses_bwd(t, ct) + (tuple(jnp.zeros_like(x) for x in t),)


_known_inverses.defvjp(_known_inverses_fwd, _known_inverses_bwd)


def _gdn_chunk(a_log, dt_bias, norm_w, ba, *per_group, inverses=None, keep_inverses=False):
    ng = GDN_NGROUPS
    qgs, kgs, vsts, zsts, states = [per_group[i * ng:(i + 1) * ng] for i in range(5)]
    groups = range(ng)
    n = GDN_ROWS
    r = lax.broadcasted_iota(jnp.int32, (n, n), 0)
    c = lax.broadcasted_iota(jnp.int32, (n, n), 1)
    same_head = jnp.right_shift(r, 6) == jnp.right_shift(c, 6)
    incl = jnp.logical_and(same_head, r >= c)
    strict = jnp.logical_and(same_head, r > c)
    eye = (r == c).astype(f32)
    ones = jnp.ones((n, n), f32)
    own_lanes = same_head.astype(f32)
    lane = lax.broadcasted_iota(jnp.int32, (1, LANES), 1)
    pick = lambda arr, idx: jnp.sum(jnp.where(lane == idx, arr, 0.0), axis=1, keepdims=True)
    heads = [[GDN_GROUP * g + h for h in range(GDN_GROUP)] for g in groups]
    rc = lax.broadcasted_iota(jnp.int32, (DN_CHUNK, DN_CHUNK), 0)
    cc = lax.broadcasted_iota(jnp.int32, (DN_CHUNK, DN_CHUNK), 1)

    g_all = -jnp.exp(a_log) * _softplus(ba + dt_bias)
    gc_all = _xdot((rc >= cc).astype(f32), g_all)
    gl_all = jnp.sum(g_all, axis=0, keepdims=True)
    beta = [jnp.concatenate([jax.nn.sigmoid(pick(ba, hd)) for hd in heads[g]], axis=0) for g in groups]
    gc = [jnp.concatenate([pick(gc_all, GDN_LOGIT_LANE + hd) for hd in heads[g]], axis=0) for g in groups]
    g_last = [jnp.concatenate([jnp.broadcast_to(pick(gl_all, GDN_LOGIT_LANE + hd), (DN_CHUNK, 1)) for hd in heads[g]],
                              axis=0) for g in groups]
    gr = [jnp.broadcast_to(gc[g], (n, n)).T for g in groups]
    decay = [jnp.where(incl, jnp.exp(jnp.where(incl, gc[g] - gr[g], 0.0)), 0.0) for g in groups]
    q = [jnp.concatenate([qgs[g]] * GDN_GROUP, axis=0) * own_lanes for g in groups]
    k = [jnp.concatenate([kgs[g]] * GDN_GROUP, axis=0) * own_lanes for g in groups]
    qn = [x * lax.rsqrt(jnp.sum(x * x, axis=1, keepdims=True) + L2_EPS) * (DN_KEY_DIM ** -0.5) for x in q]
    kn = [x * lax.rsqrt(jnp.sum(x * x, axis=1, keepdims=True) + L2_EPS) for x in k]
    kb = [kn[g] * beta[g] for g in groups]
    low = [jnp.where(strict, _bdot_nt(kb[g], kn[g]) * decay[g], 0.0) for g in groups]
    intra = [jnp.where(incl, _bdot_nt(qn[g], kn[g]) * decay[g], 0.0) for g in groups]
    t = _unit_lower_inverses(tuple(low)) if inverses is None else _known_inverses(tuple(low), tuple(inverses))
    u = [_bdot(t[g], vsts[g] * beta[g]) for g in groups]
    w = [_bdot(t[g], kb[g] * jnp.exp(gc[g])) for g in groups]
    sb = [s.astype(bf16) for s in states]
    v_new = [u[g] - jnp.dot(w[g].astype(bf16), sb[g], preferred_element_type=f32) for g in groups]
    o = [jnp.dot((qn[g] * jnp.exp(gc[g])).astype(bf16), sb[g], preferred_element_type=f32) for g in groups]
    o = [o[g] + _bdot(intra[g], v_new[g]) for g in groups]
    new_state = [states[g] * jnp.exp(g_last[g]) + _bdot_tn(kn[g] * jnp.exp(g_last[g] - gc[g]), v_new[g])
                 for g in groups]
    o_n = [x * lax.rsqrt(jnp.mean(x * x, axis=1, keepdims=True) + NORM_EPS) * norm_w for x in o]
    return tuple(o_n[g] * _silu(zsts[g]) for g in groups) + tuple(new_state) + (tuple(t) if keep_inverses else ())


def _gdn_specs(rows, reverse):
    n = rows // DN_CHUNK
    idx = (lambda i: n - 1 - i) if reverse else (lambda i: i)
    vec = pl.BlockSpec((1, LANES), lambda i: (0, 0))
    qkv = pl.BlockSpec((DN_CHUNK, DN_CONV_CH), lambda i: (idx(i), 0))
    z = pl.BlockSpec((DN_CHUNK, DN_V_WIDTH), lambda i: (idx(i), OFF_Z // DN_V_WIDTH))
    ba = pl.BlockSpec((DN_CHUNK, LANES), lambda i: (idx(i), 0))
    wide = pl.BlockSpec((DN_CHUNK, DN_V_WIDTH), lambda i: (idx(i), 0))
    st = pl.BlockSpec((1, DN_HEADS * DN_KEY_DIM, LANES), lambda i: (idx(i), 0, 0))
    inv = pl.BlockSpec((1, GDN_NGROUPS * GDN_ROWS, GDN_ROWS), lambda i: (idx(i), 0, 0))
    return n, vec, qkv, z, ba, wide, st, inv


def _gdn_slices(grp):
    q = slice(grp * GDN_QK_LANES, (grp + 1) * GDN_QK_LANES)
    k = slice(DN_QK_WIDTH + grp * GDN_QK_LANES, DN_QK_WIDTH + (grp + 1) * GDN_QK_LANES)
    heads = [slice((GDN_GROUP * grp + h) * LANES, (GDN_GROUP * grp + h + 1) * LANES) for h in range(GDN_GROUP)]
    vs = [slice(2 * DN_QK_WIDTH + s.start, 2 * DN_QK_WIDTH + s.stop) for s in heads]
    return q, k, vs, heads


def _stack_cols(ref, cols):
    return jnp.concatenate([ref[:, s] for s in cols], axis=0)


def _gdn_operands(qkv_ref, z_ref, state_rows):
    sl = [_gdn_slices(grp) for grp in range(GDN_NGROUPS)]
    return ([qkv_ref[:, q] for q, _, _, _ in sl] + [qkv_ref[:, k] for _, k, _, _ in sl]
            + [_stack_cols(qkv_ref, vs) for _, _, vs, _ in sl] + [_stack_cols(z_ref, heads) for _, _, _, heads in sl]
            + [state_rows[grp * GDN_ROWS:(grp + 1) * GDN_ROWS, :] for grp in range(GDN_NGROUPS)])


def _gdn_fwd(a_log, dt_bias, norm_w, qkv_act, proj, ba):
    rows = qkv_act.shape[0]
    n, vec, qkv_s, z_s, ba_s, wide, st_s, inv_s = _gdn_specs(rows, False)

    def body(al_ref, dt_ref, nw_ref, qkv_ref, z_ref, ba_ref, o_ref, st_ref, inv_ref, state):
        @pl.when(pl.program_id(0) == 0)
        def _():
            state[...] = jnp.zeros_like(state)

        st_ref[0] = state[...]
        out = _gdn_chunk(al_ref[...], dt_ref[...], nw_ref[...], ba_ref[...], *_gdn_operands(qkv_ref, z_ref, state),
                         keep_inverses=True)
        for grp in range(GDN_NGROUPS):
            _, _, _, heads = _gdn_slices(grp)
            rs = slice(grp * GDN_ROWS, (grp + 1) * GDN_ROWS)
            for h, s in enumerate(heads):
                o_ref[:, s] = out[grp][h * DN_CHUNK:(h + 1) * DN_CHUNK].astype(o_ref.dtype)
            state[rs, :] = out[GDN_NGROUPS + grp]
            inv_ref[0, rs, :] = out[2 * GDN_NGROUPS + grp]

    return pl.pallas_call(
        body, name="gdn_fwd", grid=(n,),
        in_specs=[vec, vec, vec, qkv_s, z_s, ba_s], out_specs=[wide, st_s, inv_s],
        out_shape=[jax.ShapeDtypeStruct((rows, DN_V_WIDTH), bf16),
                   jax.ShapeDtypeStruct((n, DN_HEADS * DN_KEY_DIM, LANES), f32),
                   jax.ShapeDtypeStruct((n, GDN_NGROUPS * GDN_ROWS, GDN_ROWS), f32)],
        scratch_shapes=[pltpu.VMEM((DN_HEADS * DN_KEY_DIM, LANES), f32)],
        compiler_params=_cparams(("arbitrary",)),
    )(a_log, dt_bias, norm_w, qkv_act, proj, ba)


def _gdn_bwd(a_log, dt_bias, norm_w, qkv_act, proj, ba, states, inverses, do):
    rows = qkv_act.shape[0]
    n, vec, qkv_s, z_s, ba_s, wide, st_s, inv_s = _gdn_specs(rows, True)

    def body(al_ref, dt_ref, nw_ref, qkv_ref, z_ref, ba_ref, st_ref, inv_ref, do_ref,
             dal_ref, ddt_ref, dnw_ref, dqkv_ref, dz_ref, dba_ref, dstate):
        @pl.when(pl.program_id(0) == 0)
        def _():
            dstate[...] = jnp.zeros_like(dstate)
            dal_ref[...] = jnp.zeros_like(dal_ref)
            ddt_ref[...] = jnp.zeros_like(ddt_ref)
            dnw_ref[...] = jnp.zeros_like(dnw_ref)

        ng = GDN_NGROUPS
        kept = [inv_ref[0, grp * GDN_ROWS:(grp + 1) * GDN_ROWS, :] for grp in range(ng)]
        _, vjp = jax.vjp(functools.partial(_gdn_chunk, inverses=kept), al_ref[...], dt_ref[...], nw_ref[...],
                         ba_ref[...], *_gdn_operands(qkv_ref, z_ref, st_ref[0]))
        cts = tuple(_stack_cols(do_ref, _gdn_slices(grp)[3]) for grp in range(ng))
        cts += tuple(dstate[grp * GDN_ROWS:(grp + 1) * GDN_ROWS, :] for grp in range(ng))
        grads = vjp(cts)
        dal_ref[...] += grads[0]
        ddt_ref[...] += grads[1]
        dnw_ref[...] += grads[2]
        dba_ref[...] = grads[3]
        dqs, dks, dvs, dzs, dss = [grads[4 + i * ng:4 + (i + 1) * ng] for i in range(5)]
        for grp in range(ng):
            q, k, vs, heads = _gdn_slices(grp)
            dqkv_ref[:, q] = dqs[grp]
            dqkv_ref[:, k] = dks[grp]
            for h, (sv, sh) in enumerate(zip(vs, heads)):
                rows_h = slice(h * DN_CHUNK, (h + 1) * DN_CHUNK)
                dqkv_ref[:, sv] = dvs[grp][rows_h]
                dz_ref[:, sh] = dzs[grp][rows_h].astype(dz_ref.dtype)
            dstate[grp * GDN_ROWS:(grp + 1) * GDN_ROWS, :] = dss[grp]

    return pl.pallas_call(
        body, name="gdn_bwd", grid=(n,),
        in_specs=[vec, vec, vec, qkv_s, z_s, ba_s, st_s, inv_s, wide],
        out_specs=[vec, vec, vec, qkv_s, wide, ba_s],
        out_shape=[jax.ShapeDtypeStruct((1, LANES), f32)] * 3
        + [jax.ShapeDtypeStruct((rows, DN_CONV_CH), f32), jax.ShapeDtypeStruct((rows, DN_V_WIDTH), bf16),
           jax.ShapeDtypeStruct((rows, LANES), f32)],
        scratch_shapes=[pltpu.VMEM((DN_HEADS * DN_KEY_DIM, LANES), f32)],
        compiler_params=_cparams(("arbitrary",)),
    )(a_log, dt_bias, norm_w, qkv_act, proj, ba, states, inverses, do)


def _ada_fwd(c_all, w_loc, b_loc):
    def body(c_ref, w_ref, b_ref, o_ref):
        o_ref[...] = _bdot(_silu(c_ref[...]), w_ref[...]) + b_ref[...]

    return pl.pallas_call(body, name="ada_fwd", out_shape=jax.ShapeDtypeStruct((c_all.shape[0], w_loc.shape[1]), f32),
                          compiler_params=_cparams())(c_all, w_loc, b_loc)


def _ada_bwd(c_all, dmod_cols):
    def body(c_ref, d_ref, o_ref):
        o_ref[...] = _bdot_tn(_silu(c_ref[...]), d_ref[...])

    return pl.pallas_call(body, name="ada_bwd",
                          out_shape=jax.ShapeDtypeStruct((c_all.shape[1], dmod_cols.shape[1]), f32),
                          compiler_params=_cparams())(c_all, dmod_cols)


def _sum_devices(parts):
    def body(p_ref, o_ref):
        acc = p_ref[0:1, :]
        for d in range(1, N_DEV):
            acc = acc + p_ref[d:d + 1, :]
        o_ref[...] = acc

    return pl.pallas_call(body, name="sum_small", out_shape=jax.ShapeDtypeStruct((1, parts.shape[1]), f32),
                          compiler_params=_cparams())(parts)


def _adam_math(w, g, m, v):
    m2 = ADAM_B1 * m + (1.0 - ADAM_B1) * g
    v2 = ADAM_B2 * v + (1.0 - ADAM_B2) * jnp.square(g)
    m_hat = m2 / (1.0 - ADAM_B1 ** ADAM_STEP)
    v_hat = v2 / (1.0 - ADAM_B2 ** ADAM_STEP)
    delta = -ADAM_LR * (m_hat / (jnp.sqrt(v_hat) + ADAM_EPS) + ADAM_WD * w)
    return delta, m2, v2


def _row_tile(rows):
    return _pick(rows, (256, 128, 64, 32, 16, 8))


def _adamw(w, g, m, v, name):
    rows, cols = w.shape
    tr = _row_tile(rows)

    def body(w_ref, g_ref, m_ref, v_ref, d_ref, m2_ref, v2_ref):
        d_ref[...], m2_ref[...], v2_ref[...] = _adam_math(w_ref[...], g_ref[...], m_ref[...], v_ref[...])

    tile = pl.BlockSpec((tr, cols), lambda i: (i, 0))
    return pl.pallas_call(body, name=name, grid=(rows // tr,), in_specs=[tile] * 4, out_specs=[tile] * 3,
                          out_shape=[jax.ShapeDtypeStruct(w.shape, f32)] * 3,
                          compiler_params=_cparams(("parallel",)))(w, g, m, v)


def _sum_adamw(parts, w, m, v, name):
    rows, cols = w.shape
    tr = _row_tile(rows)

    def body(p_ref, w_ref, m_ref, v_ref, g_ref, d_ref, m2_ref, v2_ref):
        g = p_ref[0].astype(f32)
        for d in range(1, N_DEV):
            g = g + p_ref[d].astype(f32)
        g_ref[...] = g
        d_ref[...], m2_ref[...], v2_ref[...] = _adam_math(w_ref[...], g, m_ref[...], v_ref[...])

    tile = pl.BlockSpec((tr, cols), lambda i: (i, 0))
    return pl.pallas_call(body, name=name, grid=(rows // tr,),
                          in_specs=[pl.BlockSpec((N_DEV, tr, cols), lambda i: (0, i, 0)), tile, tile, tile],
                          out_specs=[tile] * 4, out_shape=[jax.ShapeDtypeStruct(w.shape, f32)] * 4,
                          compiler_params=_cparams(("parallel",)))(parts, w, m, v)


def _pad_lanes(a, width):
    return jnp.pad(a, ((0, 0), (0, width - a.shape[1])))


def _cols_by_device(full):
    r, c = full.shape
    return jnp.moveaxis(full.reshape(r, N_DEV, c // N_DEV), 1, 0)


def _cols_from_devices(parts):
    d, r, n = parts.shape
    return jnp.moveaxis(parts, 0, 1).reshape(r, d * n)


def kernel(x, c, w_ada, b_ada, norm1_w, w_in, dn_conv_w, dn_A_log, dn_dt_bias, dn_norm_w, w_proj_sb, w_proj_dn, w_out, norm2_w, w_ffn_in, ffn_conv_w, ffn_conv_b, w_ffn_out, final_norm_w, loss_target, m_w_ada, m_b_ada, m_norm1_w, m_w_in, m_dn_conv_w, m_dn_A_log, m_dn_dt_bias, m_dn_norm_w, m_w_proj_sb, m_w_proj_dn, m_w_out, m_norm2_w, m_w_ffn_in, m_ffn_conv_w, m_ffn_conv_b, m_w_ffn_out, m_final_norm_w, v_w_ada, v_b_ada, v_norm1_w, v_w_in, v_dn_conv_w, v_dn_A_log, v_dn_dt_bias, v_dn_norm_w, v_w_proj_sb, v_w_proj_dn, v_w_out, v_norm2_w, v_w_ffn_in, v_ffn_conv_w, v_ffn_conv_b, v_w_ffn_out, v_final_norm_w):
    d = D_MODEL
    me = 4 * lax.axis_index("x") + 2 * lax.axis_index("y") + lax.axis_index("c")
    xs = x[0]
    target = loss_target[0]
    n_ada = w_ada.shape[2]
    n_dnc = dn_conv_w.shape[2]
    n_ffc = ffn_conv_w.shape[2]

    small = jnp.concatenate([c, dn_conv_w[0].reshape(1, -1), ffn_conv_w[0].reshape(1, -1)], axis=1)
    small = _pad_lanes(small, -(-small.shape[1] // LANES) * LANES)
    small_g, w_in_g = _all_gather([small, w_in[0].astype(bf16)], "gather_w_in")
    later = [w_proj_sb[0].astype(bf16), w_proj_dn[0].astype(bf16), w_out[0].astype(bf16),
             w_ffn_in[0].astype(bf16), w_ffn_out[0].astype(bf16)]
    gather_later = _SideComm(_gather_protocol, later, _gathered_shapes(later))
    small_g = small_g[:, 0, :]
    c_all = small_g[:, :d]
    dn_cw = _cols_from_devices(small_g[:, d:d + DN_CONV_WIDTH * n_dnc].reshape(N_DEV, DN_CONV_WIDTH, n_dnc))
    o2 = d + DN_CONV_WIDTH * n_dnc
    ffn_cw = _cols_from_devices(small_g[:, o2:o2 + FFN_CONV_WIDTH * n_ffc].reshape(N_DEV, FFN_CONV_WIDTH, n_ffc))

    w_in_full = _cols_from_devices(w_in_g)
    r_sb, r_dn, r_z = 3 * SB_WIDTH, 3 * SB_WIDTH + DN_CONV_CH, 3 * SB_WIDTH + DN_CONV_CH + DN_V_WIDTH
    r_g = r_z + 2 * DN_HEADS
    w_main = jnp.concatenate([w_in_full[:, r_g:], w_in_full[:, r_sb:r_dn], w_in_full[:, r_dn:r_z],
                              w_in_full[:, :r_sb]], axis=1)
    w_ba = _pad_lanes(w_in_full[:, r_z:r_g], LANES)

    b_loc = lax.dynamic_slice(b_ada, (0, me * n_ada), (1, n_ada))
    mod_part = _ada_fwd(c_all, w_ada[0], b_loc)
    (mod_g,) = _all_gather([mod_part], "gather_mod")
    mod = lax.dynamic_index_in_dim(mod_g, me, axis=1, keepdims=False).reshape(1, N_DEV * n_ada)
    shift1, scale1, gate1, shift2, scale2, gate2 = [mod[:, i * d:(i + 1) * d] for i in range(6)]

    logit_lanes = ((0, 0), (GDN_LOGIT_LANE, LANES - GDN_LOGIT_LANE - DN_HEADS))
    a_log = jnp.pad(dn_A_log, logit_lanes)
    dt_b = jnp.pad(dn_dt_bias, logit_lanes)

    (h,) = _stage_fwd(_f_normmod, [norm1_w, shift1, scale1], [xs], [bf16], "norm1_fwd")
    proj = _mm(h, w_main, name="in_proj")
    ba = _mm(h, w_ba, name="in_proj_ba")
    k16, k0_16, k1_16, v16, v0_16, v1_16 = _sb_prepare(proj)
    o_a, sb_runs, w_psb_g, w_pdn_g, w_out_g, w_fin_g, w_fout_g = _sb_attention_fwd2(
        proj, k16, v0_16, v1_16, side=gather_later)
    w_psb = _cols_from_devices(w_psb_g)
    w_pdn = w_pdn_g.reshape(DN_V_WIDTH, d)
    w_o = w_out_g.reshape(d, d)
    w_fin = _cols_from_devices(w_fin_g)
    w_fout = w_fout_g.reshape(D_FF, d)
    qkv_act = _dn_conv_fwd(proj, dn_cw)
    o_b, states, dn_inverses = _gdn_fwd(a_log, dt_b, dn_norm_w, qkv_act, proj, ba)
    pa = _mm(o_a, w_psb, name="proj_sb")
    pb = _mm(o_b, w_pdn, name="proj_dn")
    gates = [(proj, d, OFF_GA // d), (proj, d, OFF_GB // d)]
    (merged,) = _stage_fwd(_f_merge, [], gates + [pa, pb], [bf16], "merge_fwd")
    ao = _mm(merged, w_o, name="out_proj")
    mid_params = [gate1, norm2_w, shift2, scale2]
    x1, h2 = _stage_fwd(_f_residual_normmod, mid_params, [xs, ao], [f32, bf16], "resid1_norm2_fwd")
    u_pre = _mm(h2, w_fin, name="ffn_in")
    act, u_conv = _ffn_conv_fwd(u_pre, ffn_cw, ffn_conv_b)
    fo = _mm(act, w_fout, name="ffn_out")

    loss_p, d_gate2, d_wf, dx2, dfo = _loss_and_grads(gate2, final_norm_w.reshape(1, d), x1, fo, target)
    dact = _mm(dfo, w_fout, tb=True, name="ffn_out_dx")
    g_w_fout = _mm(act, dfo, ta=True, name="ffn_out_dw")
    du, dbg, dbu = _ffn_conv_bwd_act(u_conv, dact)
    du_pre, d_ffn_cw = _conv_bwd(du, u_pre, 0, ffn_cw, "ffn_conv_bwd")
    dh2 = _mm(du_pre, w_fin, tb=True, name="ffn_in_dx")
    g_w_fin = _mm(h2, du_pre, ta=True, name="ffn_in_dw")
    (d_gate1, d_n2w, d_shift2, d_scale2), (dx1, dao) = _stage_bwd(
        _f_residual_normmod, mid_params, [xs, ao], [dx2, dh2], [f32, bf16], "resid1_norm2_bwd")
    dmerged = _mm(dao, w_o, tb=True, name="out_proj_dx")
    g_w_o = _mm(merged, dao, ta=True, name="out_proj_dw")
    _, (dga, dgb, dpa, dpb) = _stage_bwd(_f_merge, [], gates + [pa, pb], [dmerged], [bf16] * 4, "merge_bwd")
    do_a = _mm(dpa, w_psb, tb=True, name="proj_sb_dx")
    g_w_psb = _mm(o_a, dpa, ta=True, name="proj_sb_dw")
    do_b = _mm(dpb, w_pdn, tb=True, name="proj_dn_dx")
    g_w_pdn = _mm(o_b, dpb, ta=True, name="proj_dn_dw")
    early = [_cols_by_device(g_w_psb).astype(bf16),
             g_w_pdn.reshape(N_DEV, DN_V_WIDTH // N_DEV, d).astype(bf16),
             g_w_o.reshape(N_DEV, d // N_DEV, d).astype(bf16),
             _cols_by_device(g_w_fin).astype(bf16),
             g_w_fout.reshape(N_DEV, D_FF // N_DEV, d).astype(bf16)]
    exchange_early = _SideComm(_exchange_protocol, early, [jax.ShapeDtypeStruct(a.shape, a.dtype) for a in early])
    dq, dk, dv, *recv_early = _sb_attention_bwd2(proj, k16, k0_16, k1_16, v16, sb_runs, do_a, side=exchange_early)
    d_alog, d_dtb, d_dnw, dqkv_act, dz, dba = _gdn_bwd(a_log, dt_b, dn_norm_w, qkv_act, proj, ba, states,
                                                       dn_inverses, do_b)
    d_conv_out = _dn_conv_bwd_act(proj, dn_cw, dqkv_act)
    d_dn_pre, d_dn_cw = _conv_bwd(d_conv_out, proj, OFF_DN // TCONV_C, dn_cw, "dn_conv_bwd")
    dproj = jnp.concatenate([dga, dgb, d_dn_pre, dz, dq, dk.astype(bf16), dv.astype(bf16)], axis=1)
    g_w_main = _mm(h, dproj, ta=True, name="in_proj_dw")
    g_w_ba = _mm(h, dba, ta=True, name="in_proj_ba_dw")
    g_w_in_full = jnp.concatenate([g_w_main[:, OFF_SBQ:], g_w_main[:, OFF_DN:OFF_Z], g_w_main[:, OFF_Z:OFF_SBQ],
                                   g_w_ba[:, :2 * DN_HEADS], g_w_main[:, :OFF_DN]], axis=1)
    w_in_parts = _cols_by_device(g_w_in_full).astype(bf16)
    exchange_w_in = _SideComm(_exchange_protocol, [w_in_parts], [jax.ShapeDtypeStruct(w_in_parts.shape, bf16)])
    dh, recv_w_in = _mm(dproj, w_main, tb=True, name="in_proj_dx", side=exchange_w_in)
    dh_ba = _mm(dba, w_ba, tb=True, name="in_proj_ba_dx")
    (d_n1w, d_shift1, d_scale1), (grad_x,) = _stage_bwd(
        _f_normmod, [norm1_w, shift1, scale1], [xs], [[dh, dh_ba]], [f32], "norm1_bwd", residual=(0, dx1))

    dmod = jnp.concatenate([d_shift1, d_scale1, d_gate1, d_shift2, d_scale2, d_gate2], axis=1)
    d_ffn_cb = jnp.concatenate([dbg, dbu], axis=1)
    small_parts = jnp.concatenate(
        [loss_p, dmod, d_n1w, d_alog, d_dtb, d_dnw, d_n2w, d_ffn_cb, d_wf,
         d_dn_cw.reshape(1, -1), d_ffn_cw.reshape(1, -1)], axis=1)
    (small_parts_g,) = _all_gather([small_parts], "gather_small_grads")
    tot = _sum_devices(small_parts_g[:, 0, :])
    offs = {}
    pos = 0
    for nm, width in (("loss", LANES), ("b_ada", 6 * d), ("norm1_w", d), ("dn_A_log", LANES), ("dn_dt_bias", LANES),
                      ("dn_norm_w", LANES), ("norm2_w", d), ("ffn_conv_b", 2 * D_FF), ("final_norm_w", d),
                      ("dn_conv_w", DN_CONV_WIDTH * DN_CONV_CH), ("ffn_conv_w", FFN_CONV_WIDTH * 2 * D_FF)):
        offs[nm] = (pos, width)
        pos += width
    seg = lambda nm: tot[:, offs[nm][0]:offs[nm][0] + offs[nm][1]]
    loss = tot[0, 0]
    g_b_ada = seg("b_ada")
    g_norm1 = seg("norm1_w")
    g_alog = seg("dn_A_log")[:, GDN_LOGIT_LANE:GDN_LOGIT_LANE + DN_HEADS]
    g_dtb = seg("dn_dt_bias")[:, GDN_LOGIT_LANE:GDN_LOGIT_LANE + DN_HEADS]
    g_dnw = seg("dn_norm_w")
    g_norm2 = seg("norm2_w")
    g_ffn_cb = seg("ffn_conv_b")
    g_fnw = seg("final_norm_w")
    g_dn_cw = lax.dynamic_slice(seg("dn_conv_w").reshape(DN_CONV_WIDTH, DN_CONV_CH), (0, me * n_dnc),
                                (DN_CONV_WIDTH, n_dnc))
    g_ffn_cw = lax.dynamic_slice(seg("ffn_conv_w").reshape(FFN_CONV_WIDTH, 2 * D_FF), (0, me * n_ffc),
                                 (FFN_CONV_WIDTH, n_ffc))

    dmod_all = small_parts_g[:, 0, offs["b_ada"][0]:offs["b_ada"][0] + 6 * d]
    g_w_ada = _ada_bwd(c_all, lax.dynamic_slice(dmod_all, (0, me * n_ada), (N_DEV, n_ada)))

    def pack(parts):
        flat = [p.reshape(1, -1) for p in parts]
        flat = [_pad_lanes(p, -(-p.shape[1] // LANES) * LANES) for p in flat]
        return jnp.concatenate(flat, axis=1), [p.shape[1] for p in flat]

    small_names_g = [g_b_ada, g_norm1, g_alog, g_dtb, g_dnw, g_norm2, g_ffn_cb, g_fnw, g_dn_cw, g_ffn_cw]
    small_w = [b_ada, norm1_w, dn_A_log, dn_dt_bias, dn_norm_w, norm2_w, ffn_conv_b, final_norm_w, dn_conv_w[0], ffn_conv_w[0]]
    small_m = [m_b_ada, m_norm1_w, m_dn_A_log, m_dn_dt_bias, m_dn_norm_w, m_norm2_w, m_ffn_conv_b, m_final_norm_w, m_dn_conv_w[0], m_ffn_conv_w[0]]
    small_v = [v_b_ada, v_norm1_w, v_dn_A_log, v_dn_dt_bias, v_dn_norm_w, v_norm2_w, v_ffn_conv_b, v_final_norm_w, v_dn_conv_w[0], v_ffn_conv_w[0]]
    pg, widths = pack(small_names_g)
    pw, _ = pack(small_w)
    pm, _ = pack(small_m)
    pv, _ = pack(small_v)
    s_delta, s_m, s_v = _adamw(pw, pg, pm, pv, "adamw_small")

    def unpack(flat):
        out, pos = [], 0
        for ref_arr, width in zip(small_w, widths):
            out.append(flat[:, pos:pos + ref_arr.size].reshape(ref_arr.shape))
            pos += width
        return out

    small_grads = [g.reshape(w_.shape) for g, w_ in zip(small_names_g, small_w)]
    small_delta, small_newm, small_newv = unpack(s_delta), unpack(s_m), unpack(s_v)

    ada_delta, ada_m, ada_v = _adamw(w_ada[0], g_w_ada, m_w_ada[0], v_w_ada[0], "adamw_ada")

    recv = [recv_w_in] + list(recv_early)
    big = {}
    for nm, parts, w_, m_, v_ in (("w_in", recv[0], w_in, m_w_in, v_w_in),
                                  ("w_proj_sb", recv[1], w_proj_sb, m_w_proj_sb, v_w_proj_sb),
                                  ("w_proj_dn", recv[2], w_proj_dn, m_w_proj_dn, v_w_proj_dn),
                                  ("w_out", recv[3], w_out, m_w_out, v_w_out),
                                  ("w_ffn_in", recv[4], w_ffn_in, m_w_ffn_in, v_w_ffn_in),
                                  ("w_ffn_out", recv[5], w_ffn_out, m_w_ffn_out, v_w_ffn_out)):
        big[nm] = [t[None] for t in _sum_adamw(parts, w_[0], m_[0], v_[0], "adamw_" + nm)]

    sg = dict(zip(["b_ada", "norm1_w", "dn_A_log", "dn_dt_bias", "dn_norm_w", "norm2_w", "ffn_conv_b", "final_norm_w",
                   "dn_conv_w", "ffn_conv_w"], range(10)))

    def small_out(table, nm):
        val = table[sg[nm]]
        return val[None] if nm in ("dn_conv_w", "ffn_conv_w") else val

    order = ["w_ada", "b_ada", "norm1_w", "w_in", "dn_conv_w", "dn_A_log", "dn_dt_bias", "dn_norm_w", "w_proj_sb",
             "w_proj_dn", "w_out", "norm2_w", "w_ffn_in", "ffn_conv_w", "ffn_conv_b", "w_ffn_out", "final_norm_w"]
    groups = []
    for k, small_table in enumerate((small_grads, small_delta, small_newm, small_newv)):
        row = []
        for nm in order:
            if nm == "w_ada":
                row.append((g_w_ada, ada_delta, ada_m, ada_v)[k][None])
            elif nm in big:
                row.append(big[nm][k])
            else:
                row.append(small_out(small_table, nm))
        groups.append(row)
    return (loss, grad_x[None], *groups[0], *groups[1], *groups[2], *groups[3])
```

```python
import functools

import jax
import jax.numpy as jnp
from jax import lax
from jax.experimental import pallas as pl
from jax.experimental.pallas import tpu as pltpu

f32 = jnp.float32
bf16 = jnp.bfloat16

D_MODEL = 1024
SB_HEADS = 8
SB_HEAD_DIM = 64
SB_WIDTH = SB_HEADS * SB_HEAD_DIM
SB_QBLOCK = 128
DN_HEADS = 8
DN_KEY_DIM = 64
DN_VAL_DIM = 128
DN_QK_WIDTH = DN_HEADS * DN_KEY_DIM
DN_V_WIDTH = DN_HEADS * DN_VAL_DIM
DN_CONV_CH = 2 * DN_QK_WIDTH + DN_V_WIDTH
DN_CONV_WIDTH = 4
DN_CHUNK = 64
D_FF = 2816
FFN_CONV_WIDTH = 3
NORM_EPS = 1e-6
L2_EPS = 1e-6
ADAM_LR = 0.001
ADAM_B1 = 0.9
ADAM_B2 = 0.999
ADAM_EPS = 1e-08
ADAM_WD = 0.01
ADAM_STEP = 10

N_DEV = 8
MESH = pl.DeviceIdType.MESH

LANES = 128
SUBLANES = 8
VMEM_LIMIT = 48 * 1024 * 1024

OFF_GA = 0
OFF_GB = D_MODEL
OFF_DN = 2 * D_MODEL
OFF_Z = OFF_DN + DN_CONV_CH
OFF_SBQ = OFF_Z + DN_V_WIDTH
OFF_SBK = OFF_SBQ + SB_WIDTH
OFF_SBV = OFF_SBK + SB_WIDTH
MAIN_WIDTH = OFF_SBV + SB_WIDTH

TM = 512
TCONV_R = 512
TCONV_C = 1024
TCONV_FF = D_FF // 2
SB_PAIRS_FWD = 4
SB_PAIRS_BWD = 4
SB_DEAD = -106.0
SB_NEVER = -1e30


def _cparams(sem=None):
    return pltpu.CompilerParams(dimension_semantics=sem, vmem_limit_bytes=VMEM_LIMIT)


def _pick(n, cands):
    for c in cands:
        if n % c == 0:
            return c
    return n


def _my_pos():
    return lax.axis_index("x"), lax.axis_index("y"), lax.axis_index("c")


def _flip(v, bit):
    return 1 - v if bit else v


def _comm_scratch(n):
    return [pltpu.SemaphoreType.DMA((n, 7)), pltpu.SemaphoreType.DMA((n, 7)), pltpu.SemaphoreType.DMA((n,))]


def _gather_protocol(ins, outs, send_sems, recv_sems, local_sems):
    n = len(ins)
    x, y, c = _my_pos()
    me, sibling = (x, y, c), (x, y, 1 - c)
    chips = [(1 - x, y), (x, 1 - y), (1 - x, 1 - y)]

    def slot(out, pos):
        return out.at[4 * pos[0] + 2 * pos[1] + pos[2]]

    def copy(a, k, block, to, src=None):
        return pltpu.make_async_remote_copy(
            src_ref=slot(outs[a], block) if src is None else src, dst_ref=slot(outs[a], block),
            send_sem=send_sems.at[a, k], recv_sem=recv_sems.at[a, k], device_id=to, device_id_type=MESH)

    def local(a):
        return pltpu.make_async_copy(ins[a], slot(outs[a], me), local_sems.at[a])

    def first(a):
        return [copy(a, 0, me, sibling, src=ins[a])] + [copy(a, 1 + j, me, (*chip, c), src=ins[a])
                                                         for j, chip in enumerate(chips)]

    def start():
        for a in range(n):
            local(a).start()
            for cp in first(a):
                cp.start()

    def finish():
        forwards = []
        for a in range(n):
            for j, chip in enumerate(chips):
                copy(a, 1 + j, (*chip, c), me).wait_recv()
                fwd = copy(a, 4 + j, (*chip, c), sibling)
                fwd.start()
                forwards.append(fwd)
        for a in range(n):
            copy(a, 0, sibling, me).wait_recv()
            for j, chip in enumerate(chips):
                copy(a, 4 + j, (*chip, 1 - c), me).wait_recv()
        for a in range(n):
            for cp in first(a):
                cp.wait_send()
        for cp in forwards:
            cp.wait_send()
        for a in range(n):
            local(a).wait()

    return start, finish


def _exchange_protocol(ins, outs, send_sems, recv_sems, local_sems):
    n = len(ins)
    x, y, c = _my_pos()
    me_idx = 4 * x + 2 * y + c

    def local(a):
        return pltpu.make_async_copy(ins[a].at[me_idx], outs[a].at[me_idx], local_sems.at[a])

    def copies(a, m):
        peer = (_flip(x, m & 4), _flip(y, m & 2), _flip(c, m & 1))
        peer_idx = 4 * peer[0] + 2 * peer[1] + peer[2]
        sems = dict(send_sem=send_sems.at[a, m - 1], recv_sem=recv_sems.at[a, m - 1], device_id=peer,
                    device_id_type=MESH)
        send = pltpu.make_async_remote_copy(src_ref=ins[a].at[peer_idx], dst_ref=outs[a].at[me_idx], **sems)
        recv = pltpu.make_async_remote_copy(src_ref=ins[a].at[peer_idx], dst_ref=outs[a].at[peer_idx], **sems)
        return send, recv

    def start():
        for a in range(n):
            local(a).start()
            for m in range(1, N_DEV):
                copies(a, m)[0].start()

    def finish():
        for a in range(n):
            for m in range(1, N_DEV):
                copies(a, m)[1].wait_recv()
        for a in range(n):
            for m in range(1, N_DEV):
                copies(a, m)[0].wait_send()
            local(a).wait()

    return start, finish


def _collective_call(protocol, arrs, out_shapes, name):
    n = len(arrs)

    def body(*refs):
        start, finish = protocol(refs[:n], refs[n:2 * n], *refs[2 * n:])
        start()
        finish()

    any_spec = pl.BlockSpec(memory_space=pl.ANY)
    return pl.pallas_call(body, name=name, out_shape=out_shapes, in_specs=[any_spec] * n, out_specs=[any_spec] * n,
                          scratch_shapes=_comm_scratch(n))(*arrs)


def _gathered_shapes(arrs):
    return [jax.ShapeDtypeStruct((N_DEV,) + a.shape, a.dtype) for a in arrs]


def _all_gather(arrs, name):
    return _collective_call(_gather_protocol, arrs, _gathered_shapes(arrs), name)


MM_BLOCK_BYTES = 7 * 1024 * 1024
MM_TILE_CAP = 1664


def _lane_tile(n, cap):
    fits = [t for t in range(LANES, min(n, cap) + 1, LANES) if n % t == 0]
    return max(fits) if fits else n


def _mm_tiles(m_dim, n_dim, k_dim, a_bytes, b_bytes):
    tm = _lane_tile(m_dim, MM_TILE_CAP)
    tn = _lane_tile(n_dim, MM_TILE_CAP)
    while tm * tn * 4 > MM_BLOCK_BYTES:
        if tn >= tm and (tn // 2) % LANES == 0:
            tn //= 2
        else:
            tm //= 2
    if (m_dim % (2 * tm) == 0 and 2 * tm * k_dim * a_bytes <= MM_BLOCK_BYTES
            and 2 * tm * tn * 4 <= MM_BLOCK_BYTES):
        tm *= 2
    tk = k_dim
    if k_dim % LANES == 0:
        units = k_dim // LANES
        fits = [u for u in range(1, units + 1) if units % u == 0
                and u * LANES * max(tm * a_bytes, tn * b_bytes) <= MM_BLOCK_BYTES]
        tk = max(fits) * LANES
    return tm, tn, tk


def _mm(a, b, *, ta=False, tb=False, name, side=None):
    (k_dim, m_dim) = a.shape if ta else a.shape[::-1]
    (n_dim, kb_dim) = b.shape if tb else b.shape[::-1]
    assert k_dim == kb_dim, (a.shape, b.shape, ta, tb)
    tm, tn, tk = _mm_tiles(m_dim, n_dim, k_dim, a.dtype.itemsize, b.dtype.itemsize)
    nk = k_dim // tk
    grid = (m_dim // tm, n_dim // tn, nk)
    dims = (((0 if ta else 1,), (1 if tb else 0,)), ((), ()))
    ns = side.n if side else 0

    def body(a_ref, b_ref, *rest):
        if side:
            side.run(rest[:ns], rest[ns + 1:2 * ns + 1], rest[2 * ns + 1:], *_grid_ends(grid),
                     lambda: compute(a_ref, b_ref, rest[ns]))
        else:
            compute(a_ref, b_ref, rest[0])

    def compute(a_ref, b_ref, o_ref):
        part = lax.dot_general(a_ref[...].astype(bf16), b_ref[...].astype(bf16), dims, preferred_element_type=f32)
        if nk == 1:
            o_ref[...] = part
        else:
            k = pl.program_id(2)

            @pl.when(k == 0)
            def _():
                o_ref[...] = part

            @pl.when(k > 0)
            def _():
                o_ref[...] += part

    a_spec = pl.BlockSpec((tk, tm), lambda i, j, k: (k, i)) if ta else pl.BlockSpec((tm, tk), lambda i, j, k: (i, k))
    b_spec = pl.BlockSpec((tn, tk), lambda i, j, k: (j, k)) if tb else pl.BlockSpec((tk, tn), lambda i, j, k: (k, j))
    out_spec = pl.BlockSpec((tm, tn), lambda i, j, k: (i, j))
    out_shape = jax.ShapeDtypeStruct((m_dim, n_dim), f32)
    if not side:
        return pl.pallas_call(body, name=name, grid=grid, in_specs=[a_spec, b_spec], out_specs=out_spec,
                              out_shape=out_shape,
                              compiler_params=_cparams(("parallel", "parallel", "arbitrary")))(a, b)
    return pl.pallas_call(
        body, name=name, grid=grid, in_specs=[a_spec, b_spec] + side.specs(), out_specs=[out_spec] + side.specs(),
        out_shape=[out_shape] + side.out_shapes, scratch_shapes=_comm_scratch(ns),
        compiler_params=_cparams(("arbitrary", "arbitrary", "arbitrary")))(a, b, *side.arrs)


def _win(t):
    return t if isinstance(t, tuple) else (t, t.shape[1], 0)


def _tile_spec(width, cb, tm):
    return pl.BlockSpec((tm, width), lambda i: (i, cb))


def _param_spec(p):
    return pl.BlockSpec(p.shape, lambda i: (0, 0))


def _stage_fwd(f, params, tiles, out_dtypes, name):
    tiles = [_win(t) for t in tiles]
    rows = tiles[0][0].shape[0]
    tm = min(TM, rows)
    avals = jax.eval_shape(f, *[jax.ShapeDtypeStruct(p.shape, f32) for p in params],
                           *[jax.ShapeDtypeStruct((tm, w), f32) for _, w, _ in tiles])
    n_p, n_t = len(params), len(tiles)

    def body(*refs):
        p = [r[...] for r in refs[:n_p]]
        t = [r[...].astype(f32) for r in refs[n_p:n_p + n_t]]
        for o_ref, val in zip(refs[n_p + n_t:], f(*p, *t)):
            o_ref[...] = val.astype(o_ref.dtype)

    return pl.pallas_call(
        body, name=name, grid=(rows // tm,),
        in_specs=[_param_spec(p) for p in params] + [_tile_spec(w, cb, tm) for _, w, cb in tiles],
        out_specs=[_tile_spec(a.shape[1], 0, tm) for a in avals],
        out_shape=[jax.ShapeDtypeStruct((rows, a.shape[1]), dt) for a, dt in zip(avals, out_dtypes)],
        compiler_params=_cparams(("parallel",)),
    )(*params, *[t[0] for t in tiles])


def _stage_bwd(f, params, tiles, cts, grad_dtypes, name, residual=None):
    tiles = [_win(t) for t in tiles]
    rows = tiles[0][0].shape[0]
    tm = min(TM, rows)
    cts = [list(g) if isinstance(g, (list, tuple)) else [g] for g in cts]
    flat_cts = [a for g in cts for a in g]
    n_p, n_t, n_c = len(params), len(tiles), len(flat_cts)
    has_res = residual is not None
    want = [j for j, dt in enumerate(grad_dtypes) if dt is not None]

    def body(*refs):
        i = pl.program_id(0)
        p = [r[...] for r in refs[:n_p]]
        t = [r[...].astype(f32) for r in refs[n_p:n_p + n_t]]
        ct_vals = [r[...].astype(f32) for r in refs[n_p + n_t:n_p + n_t + n_c]]
        ct, at = [], 0
        for g in cts:
            ct.append(functools.reduce(jnp.add, ct_vals[at:at + len(g)]))
            at += len(g)
        ct = tuple(ct)
        pos = n_p + n_t + n_c
        res_ref = refs[pos] if has_res else None
        pos += 1 if has_res else 0
        dp_refs = refs[pos:pos + n_p]
        dt_refs = refs[pos + n_p:]
        _, vjp = jax.vjp(f, *p, *t)
        grads = vjp(ct)

        @pl.when(i == 0)
        def _():
            for r in dp_refs:
                r[...] = jnp.zeros_like(r)

        for r, g in zip(dp_refs, grads[:n_p]):
            r[...] += g
        for r, j in zip(dt_refs, want):
            g = grads[n_p + j]
            if has_res and j == residual[0]:
                g = g + res_ref[...].astype(f32)
            r[...] = g.astype(r.dtype)

    in_arrays = list(params) + [t[0] for t in tiles] + flat_cts
    in_specs = ([_param_spec(p) for p in params] + [_tile_spec(w, cb, tm) for _, w, cb in tiles]
                + [_tile_spec(c.shape[1], 0, tm) for c in flat_cts])
    if has_res:
        in_arrays.append(residual[1])
        in_specs.append(_tile_spec(residual[1].shape[1], 0, tm))
    out_shape = ([jax.ShapeDtypeStruct(p.shape, f32) for p in params]
                 + [jax.ShapeDtypeStruct((rows, tiles[j][1]), grad_dtypes[j]) for j in want])
    out_specs = [_param_spec(p) for p in params] + [_tile_spec(tiles[j][1], 0, tm) for j in want]
    outs = pl.pallas_call(
        body, name=name, grid=(rows // tm,), in_specs=in_specs, out_specs=out_specs, out_shape=out_shape,
        compiler_params=_cparams(("arbitrary",)),
    )(*in_arrays)
    return outs[:n_p], outs[n_p:]


def _rms(x, w):
    return x * lax.rsqrt(jnp.mean(x * x, axis=-1, keepdims=True) + NORM_EPS) * w


def _f_normmod(w, shift, scale, x):
    return (_rms(x, w) * (1.0 + scale) + shift,)


def _f_merge(ga, gb, pa, pb):
    return (jax.nn.sigmoid(ga) * pa + jax.nn.sigmoid(gb) * pb,)


def _f_residual_normmod(gate, w, shift, scale, x, branch):
    x1 = x + gate * branch
    return x1, _rms(x1, w) * (1.0 + scale) + shift


def _f_loss(gate, wf, x1, fo, target):
    y = _rms(x1 + gate * fo, wf)
    err = jnp.square(y - target)
    return (0.5 * jnp.sum(jnp.mean(err, axis=-1, keepdims=True), axis=0, keepdims=True),)


def _loss_and_grads(gate2, wf, x1, fo, target):
    rows, d = x1.shape
    tm = min(TM, rows)

    def body(g_ref, w_ref, x_ref, fo_ref, t_ref, loss_ref, dg_ref, dw_ref, dx_ref, dfo_ref):
        i = pl.program_id(0)
        (val,), vjp = jax.vjp(_f_loss, g_ref[...], w_ref[...], x_ref[...], fo_ref[...], t_ref[...])
        dg, dw, dx, dfo, _ = vjp((jnp.ones((1, 1), f32),))

        @pl.when(i == 0)
        def _():
            loss_ref[...] = jnp.zeros_like(loss_ref)
            dg_ref[...] = jnp.zeros_like(dg_ref)
            dw_ref[...] = jnp.zeros_like(dw_ref)

        loss_ref[...] += jnp.broadcast_to(val, loss_ref.shape)
        dg_ref[...] += dg
        dw_ref[...] += dw
        dx_ref[...] = dx
        dfo_ref[...] = dfo.astype(bf16)

    vec = pl.BlockSpec((1, d), lambda i: (0, 0))
    tile = pl.BlockSpec((tm, d), lambda i: (i, 0))
    return pl.pallas_call(
        body, name="loss_fwd_bwd", grid=(rows // tm,),
        in_specs=[vec, vec, tile, tile, tile],
        out_specs=[pl.BlockSpec((1, LANES), lambda i: (0, 0)), vec, vec, tile, tile],
        out_shape=[jax.ShapeDtypeStruct((1, LANES), f32), jax.ShapeDtypeStruct((1, d), f32),
                   jax.ShapeDtypeStruct((1, d), f32), jax.ShapeDtypeStruct((rows, d), f32),
                   jax.ShapeDtypeStruct((rows, d), bf16)],
        compiler_params=_cparams(("arbitrary",)),
    )(gate2, wf, x1, fo, target)


def _softplus(z):
    return jnp.maximum(z, 0.0) + jnp.log(1.0 + jnp.exp(-jnp.abs(z)))


def _split_dot(a, m):
    hi = a.astype(bf16)
    lo = (a - hi.astype(f32)).astype(bf16)
    return jnp.dot(hi, m, preferred_element_type=f32) + jnp.dot(lo, m, preferred_element_type=f32)


def _suffix_matrix(n):
    r = lax.broadcasted_iota(jnp.int32, (n, n), 0)
    c = lax.broadcasted_iota(jnp.int32, (n, n), 1)
    return (r > c).astype(bf16)


def _head_masks():
    lane = lax.broadcasted_iota(jnp.int32, (1, LANES), 1)
    return [(lane < SB_HEAD_DIM).astype(f32), (lane >= SB_HEAD_DIM).astype(f32)]


def _sb_prepare(proj):
    def f(k, v):
        lane = lax.broadcasted_iota(jnp.int32, (1, SB_WIDTH), 1)
        m0 = (jnp.bitwise_and(lane, LANES - 1) < SB_HEAD_DIM).astype(f32)
        m1 = 1.0 - m0
        return k, k * m0, k * m1, v, v * m0, v * m1

    wins = [(proj, SB_WIDTH, OFF_SBK // SB_WIDTH), (proj, SB_WIDTH, OFF_SBV // SB_WIDTH)]
    return _stage_fwd(f, [], wins, [bf16] * 6, "sb_prepare")


def _stack_heads(x):
    m0, m1 = _head_masks()
    return jnp.concatenate([x * m0, x * m1], axis=0)


def _sb_logits(qst, k, t_pos2, kb, bq, masked):
    z = lax.dot_general(qst, k, (((1,), (1,)), ((), ())), preferred_element_type=f32)
    l = -_softplus(z)
    if masked:
        s_pos = kb * bq + lax.broadcasted_iota(jnp.int32, (1, bq), 1)
        causal = s_pos < t_pos2
        l = jnp.where(causal, l, 0.0)
    else:
        causal = None
    return z, l, causal


class _SideComm:
    def __init__(self, protocol, arrs, out_shapes):
        self.protocol, self.arrs, self.out_shapes = protocol, list(arrs), list(out_shapes)
        self.n = len(self.arrs)

    def specs(self):
        return [pl.BlockSpec(memory_space=pl.ANY)] * self.n

    def run(self, in_refs, out_refs, sems, first, last, compute):
        start, finish = self.protocol(in_refs, out_refs, *sems)
        pl.when(first)(start)
        compute()
        pl.when(last)(finish)


def _grid_ends(grid):
    ids = [pl.program_id(axis) for axis in range(len(grid))]
    first = functools.reduce(jnp.logical_and, [i == 0 for i in ids])
    last = functools.reduce(jnp.logical_and, [i == g - 1 for i, g in zip(ids, grid)])
    return first, last


def _sb_attention_fwd2(proj, k16, v0_16, v1_16, side=None):
    rows = proj.shape[0]
    bq = SB_QBLOCK
    nq = rows // bq
    assert nq <= LANES, "one lane per key block"
    npair = SB_WIDTH // LANES
    scale = SB_HEAD_DIM ** -0.5

    npp = SB_PAIRS_FWD
    wq = npp * LANES
    grid = (npair // npp, nq)
    ns = side.n if side else 0

    def body(q_ref, k_ref, v0_ref, v1_ref, *rest):
        o_ref, runs_ref = rest[ns], rest[ns + 1]
        if side:
            side.run(rest[:ns], rest[ns + 2:2 * ns + 2], rest[2 * ns + 2:], *_grid_ends(grid),
                     lambda: compute(q_ref, k_ref, v0_ref, v1_ref, o_ref, runs_ref))
        else:
            compute(q_ref, k_ref, v0_ref, v1_ref, o_ref, runs_ref)

    def compute(q_ref, k_ref, v0_ref, v1_ref, o_ref, runs_ref):
        qi = pl.program_id(1)
        pairs = [slice(pp * LANES, (pp + 1) * LANES) for pp in range(npp)]
        qst = [(_stack_heads(q_ref[:, s]) * scale).astype(bf16) for s in pairs]
        r = lax.broadcasted_iota(jnp.int32, (bq, 2 * bq), 0)
        c = lax.broadcasted_iota(jnp.int32, (bq, 2 * bq), 1)
        m2 = jnp.logical_or(r > c, c >= bq).astype(bf16)
        t_pos = qi * bq + lax.broadcasted_iota(jnp.int32, (bq, 1), 0)
        t_pos2 = jnp.concatenate([t_pos, t_pos], axis=0)
        lane = lax.broadcasted_iota(jnp.int32, (1, LANES), 1)
        runs_ref[...] = jnp.full(runs_ref.shape, SB_NEVER, f32)

        def tiles(kbs, carry, masked):
            jobs = [(pp, kb) for kb in kbs for pp in range(npp)]
            rows_k = [pl.ds(pl.multiple_of(kb * bq, bq), bq) for _, kb in jobs]
            zl = [_sb_logits(qst[pp], k_ref[rk, pairs[pp]], t_pos2, kb, bq, masked) for (pp, kb), rk in zip(jobs, rows_k)]
            cs = [_split_dot(l, m2) for _, l, _ in zl]
            run = [cr[0] for cr in carry]
            acc = [cr[1] for cr in carry]
            probs = []
            for (pp, kb), (z, l, causal), cs2 in zip(jobs, zl, cs):
                a = jnp.exp(z + l + cs2[:, :bq] + run[pp])
                if masked:
                    a = jnp.where(causal, a, 0.0)
                probs.append(a.astype(bf16))
                for hh in range(2):
                    cols = slice((2 * pp + hh) * LANES, (2 * pp + hh + 1) * LANES)
                    runs_ref[:, cols] = jnp.where(lane == kb, run[pp][hh * bq:(hh + 1) * bq], runs_ref[:, cols])
                run[pp] = run[pp] + cs2[:, bq:]
            for (pp, kb), rk, ab in zip(jobs, rows_k, probs):
                acc[pp] = (acc[pp] + jnp.dot(ab[:bq], v0_ref[rk, pairs[pp]], preferred_element_type=f32)
                           + jnp.dot(ab[bq:], v1_ref[rk, pairs[pp]], preferred_element_type=f32))
            return tuple(zip(run, acc))

        zero = (jnp.zeros((2 * bq, bq), f32), jnp.zeros((bq, LANES), f32))
        carry = tiles([qi], (zero,) * npp, True)

        def alive(cr):
            return functools.reduce(jnp.maximum, [jnp.max(run) for run, _ in cr]) > SB_DEAD

        def two(state):
            i, _, cr = state
            cr = tiles([qi - 1 - 2 * i, qi - 2 - 2 * i], cr, False)
            return i + 1, alive(cr), cr

        n_two = qi // 2
        i_end, still, carry = lax.while_loop(lambda st: jnp.logical_and(st[0] < n_two, st[1]), two,
                                             (jnp.int32(0), alive(carry), carry))
        last_one = jnp.logical_and(qi % 2 == 1, jnp.logical_and(still, i_end == n_two))
        carry = lax.cond(last_one, lambda cr: tiles([0], cr, False), lambda cr: cr, carry)
        for pp in range(npp):
            o_ref[:, pairs[pp]] = carry[pp][1]

    kv = pl.BlockSpec((rows, wq), lambda p, i: (0, p))
    return pl.pallas_call(
        body, name="sb_attn_fwd", grid=grid,
        in_specs=[pl.BlockSpec((bq, wq), lambda p, i: (i, OFF_SBQ // wq + p)), kv, kv, kv] + (side.specs() if side else []),
        out_specs=[pl.BlockSpec((bq, wq), lambda p, i: (i, p)),
                   pl.BlockSpec((bq, 2 * wq), lambda p, i: (i, p))] + (side.specs() if side else []),
        out_shape=[jax.ShapeDtypeStruct((rows, SB_WIDTH), f32),
                   jax.ShapeDtypeStruct((rows, SB_HEADS * LANES), f32)] + (side.out_shapes if side else []),
        scratch_shapes=_comm_scratch(ns) if side else [],
        compiler_params=_cparams(("arbitrary", "arbitrary")),
    )(proj, k16, v0_16, v1_16, *(side.arrs if side else []))


def _sb_attention_bwd2(proj, k16, k0_16, k1_16, v16, runs, do, side=None):
    rows = proj.shape[0]
    bq = SB_QBLOCK
    nq = rows // bq
    npair = SB_WIDTH // LANES
    scale = SB_HEAD_DIM ** -0.5
    tn = (((0,), (0,)), ((), ()))
    nt = (((1,), (1,)), ((), ()))

    npp = SB_PAIRS_BWD
    wq = npp * LANES
    grid = (npair // npp, nq)
    ns = side.n if side else 0

    def body(q_ref, k_ref, k0_ref, k1_ref, v_ref, runs_ref, do_ref, *rest):
        outs = rest[ns:ns + 3]
        ins = (q_ref, k_ref, k0_ref, k1_ref, v_ref, runs_ref, do_ref)
        if side:
            side.run(rest[:ns], rest[ns + 3:2 * ns + 3], rest[2 * ns + 3:], *_grid_ends(grid),
                     lambda: compute(*ins, *outs))
        else:
            compute(*ins, *outs)

    def compute(q_ref, k_ref, k0_ref, k1_ref, v_ref, runs_ref, do_ref, dq_ref, dk_ref, dv_ref):
        qi = pl.program_id(1)

        @pl.when(qi == 0)
        def _():
            dk_ref[...] = jnp.zeros_like(dk_ref)
            dv_ref[...] = jnp.zeros_like(dv_ref)

        pairs = [slice(pp * LANES, (pp + 1) * LANES) for pp in range(npp)]
        qst = [(_stack_heads(q_ref[:, s]) * scale).astype(bf16) for s in pairs]
        dost = [_stack_heads(do_ref[:, s]).astype(bf16) for s in pairs]
        runs = [jnp.concatenate([runs_ref[:, 2 * pp * LANES:(2 * pp + 1) * LANES],
                                 runs_ref[:, (2 * pp + 1) * LANES:(2 * pp + 2) * LANES]], axis=0) for pp in range(npp)]
        r = lax.broadcasted_iota(jnp.int32, (bq, 2 * bq), 0)
        c = lax.broadcasted_iota(jnp.int32, (bq, 2 * bq), 1)
        suffix_m = _suffix_matrix(bq)
        m2 = jnp.logical_or(r < c, c >= bq).astype(bf16)
        t_pos = qi * bq + lax.broadcasted_iota(jnp.int32, (bq, 1), 0)
        t_pos2 = jnp.concatenate([t_pos, t_pos], axis=0)
        lane = lax.broadcasted_iota(jnp.int32, (1, LANES), 1)

        def tiles(kbs, carry, masked):
            jobs = [(pp, kb) for kb in kbs for pp in range(npp)]
            rows_k = [pl.ds(pl.multiple_of(kb * bq, bq), bq) for _, kb in jobs]
            zl = [_sb_logits(qst[pp], k_ref[rk, pairs[pp]], t_pos2, kb, bq, masked) for (pp, kb), rk in zip(jobs, rows_k)]
            das = [lax.dot_general(dost[pp], v_ref[rk, pairs[pp]], nt, preferred_element_type=f32)
                   for (pp, kb), rk in zip(jobs, rows_k)]
            sticks = [_split_dot(l, suffix_m) for _, l, _ in zl]
            probs, ps = [], []
            for (pp, kb), (z, l, causal), stick, da in zip(jobs, zl, sticks, das):
                run = jnp.sum(jnp.where(lane == kb, runs[pp], 0.0), axis=1, keepdims=True)
                a = jnp.exp(z + l + stick + run)
                if masked:
                    a = jnp.where(causal, a, 0.0)
                probs.append(a.astype(bf16))
                ps.append(da * a)
            pcs = [_split_dot(p, m2) for p in ps]
            pref = [cr[0] for cr in carry]
            dq_acc = [cr[1] for cr in carry]
            dzs = []
            for (pp, kb), (z, l, causal), p, pc2 in zip(jobs, zl, ps, pcs):
                dz = p * jnp.exp(l) - jnp.exp(z + l) * (pc2[:, :bq] + pref[pp])
                if masked:
                    dz = jnp.where(causal, dz, 0.0)
                dzs.append(dz.astype(bf16))
                pref[pp] = pref[pp] + pc2[:, bq:]
            for (pp, kb), rk, dzb, ab in zip(jobs, rows_k, dzs, probs):
                cols = pairs[pp]
                dq_acc[pp] = (dq_acc[pp] + jnp.dot(dzb[:bq], k0_ref[rk, cols], preferred_element_type=f32)
                              + jnp.dot(dzb[bq:], k1_ref[rk, cols], preferred_element_type=f32))
                dk_ref[rk, cols] += lax.dot_general(dzb, qst[pp], tn, preferred_element_type=f32)
                dv_ref[rk, cols] += lax.dot_general(ab, dost[pp], tn, preferred_element_type=f32)
            return tuple(zip(pref, dq_acc))

        zero = (jnp.zeros((2 * bq, bq), f32), jnp.zeros((bq, LANES), f32))
        colmax = functools.reduce(jnp.maximum, [jnp.max(x, axis=0, keepdims=True) for x in runs])
        live = jnp.logical_and(colmax > SB_DEAD, lane < qi)
        kb0 = jnp.minimum(jnp.min(jnp.where(live, lane, LANES)), qi)
        n_blocks = qi - kb0
        carry = lax.fori_loop(0, n_blocks // 2, lambda i, cr: tiles([kb0 + 2 * i, kb0 + 2 * i + 1], cr, False),
                              (zero,) * npp)
        carry = lax.cond(n_blocks % 2 == 1, lambda cr: tiles([qi - 1], cr, False), lambda cr: cr, carry)
        carry = tiles([qi], carry, True)
        for pp in range(npp):
            dq_ref[:, pairs[pp]] = (carry[pp][1] * scale).astype(dq_ref.dtype)

    blk = pl.BlockSpec((bq, wq), lambda p, i: (i, p))
    full = pl.BlockSpec((rows, wq), lambda p, i: (0, p), pipeline_mode=pl.Buffered(1))
    return pl.pallas_call(
        body, name="sb_attn_bwd", grid=grid,
        in_specs=[pl.BlockSpec((bq, wq), lambda p, i: (i, OFF_SBQ // wq + p)), full, full, full, full,
                  pl.BlockSpec((bq, 2 * wq), lambda p, i: (i, p)), blk] + (side.specs() if side else []),
        out_specs=[blk, full, full] + (side.specs() if side else []),
        out_shape=[jax.ShapeDtypeStruct((rows, SB_WIDTH), bf16), jax.ShapeDtypeStruct((rows, SB_WIDTH), f32),
                   jax.ShapeDtypeStruct((rows, SB_WIDTH), f32)] + (side.out_shapes if side else []),
        scratch_shapes=_comm_scratch(ns) if side else [],
        compiler_params=_cparams(("arbitrary", "arbitrary")),
    )(proj, k16, k0_16, k1_16, v16, runs, do, *(side.arrs if side else []))


def _shift_down(x, prev8, j):
    if j == 0:
        return x
    r = pltpu.roll(x, j, axis=0)
    row8 = lax.broadcasted_iota(jnp.int32, prev8.shape, 0)
    head = jnp.where(row8 < j, pltpu.roll(prev8, j, axis=0), r[0:SUBLANES])
    return jnp.concatenate([head, r[SUBLANES:]], axis=0)


def _shift_up(x, next8, j):
    if j == 0:
        return x
    n = x.shape[0]
    r = pltpu.roll(x, n - j, axis=0)
    row8 = lax.broadcasted_iota(jnp.int32, next8.shape, 0)
    tail = jnp.where(row8 >= SUBLANES - j, pltpu.roll(next8, SUBLANES - j, axis=0), r[n - SUBLANES:n])
    return jnp.concatenate([r[:n - SUBLANES], tail], axis=0)


def _conv(x, prev8, w):
    k_taps = w.shape[0]
    out = x * w[k_taps - 1:k_taps, :]
    for j in range(1, k_taps):
        out = out + _shift_down(x, prev8, j) * w[k_taps - 1 - j:k_taps - j, :]
    return out


def _conv_tiles(rows):
    tr = min(TCONV_R, rows)
    return tr, rows // tr, tr // SUBLANES


def _prev_spec(tc, cb0, r8):
    return pl.BlockSpec((SUBLANES, tc), lambda j, i: (jnp.maximum(i * r8 - 1, 0), cb0 + j))


def _silu(x):
    return x * jax.nn.sigmoid(x)


def _dsilu(x):
    s = jax.nn.sigmoid(x)
    return s * (1.0 + x * (1.0 - s))


def _dn_conv_fwd(proj, w):
    rows = proj.shape[0]
    tr, nr, r8 = _conv_tiles(rows)
    tc = TCONV_C
    cb0 = OFF_DN // tc

    def body(x_ref, p_ref, w_ref, o_ref):
        prev = jnp.where(pl.program_id(1) == 0, 0.0, p_ref[...])
        o_ref[...] = _silu(_conv(x_ref[...], prev, w_ref[...]))

    return pl.pallas_call(
        body, name="dn_conv_fwd", grid=(DN_CONV_CH // tc, nr),
        in_specs=[pl.BlockSpec((tr, tc), lambda j, i: (i, cb0 + j)), _prev_spec(tc, cb0, r8),
                  pl.BlockSpec((DN_CONV_WIDTH, tc), lambda j, i: (0, j))],
        out_specs=pl.BlockSpec((tr, tc), lambda j, i: (i, j)),
        out_shape=jax.ShapeDtypeStruct((rows, DN_CONV_CH), f32),
        compiler_params=_cparams(("parallel", "parallel")),
    )(proj, proj, w)


def _dn_conv_bwd_act(proj, w, dact):
    rows = proj.shape[0]
    tr, nr, r8 = _conv_tiles(rows)
    tc = TCONV_C
    cb0 = OFF_DN // tc

    def body(x_ref, p_ref, w_ref, d_ref, o_ref):
        prev = jnp.where(pl.program_id(1) == 0, 0.0, p_ref[...])
        o_ref[...] = d_ref[...] * _dsilu(_conv(x_ref[...], prev, w_ref[...]))

    return pl.pallas_call(
        body, name="dn_conv_bwd_act", grid=(DN_CONV_CH // tc, nr),
        in_specs=[pl.BlockSpec((tr, tc), lambda j, i: (i, cb0 + j)), _prev_spec(tc, cb0, r8),
                  pl.BlockSpec((DN_CONV_WIDTH, tc), lambda j, i: (0, j)),
                  pl.BlockSpec((tr, tc), lambda j, i: (i, j))],
        out_specs=pl.BlockSpec((tr, tc), lambda j, i: (i, j)),
        out_shape=jax.ShapeDtypeStruct((rows, DN_CONV_CH), f32),
        compiler_params=_cparams(("parallel", "parallel")),
    )(proj, proj, w, dact)


def _ffn_conv_fwd(u_pre, w, b):
    rows = u_pre.shape[0]
    tr, nr, r8 = _conv_tiles(rows)
    tc = TCONV_FF
    nct = D_FF // tc

    def body(xg_ref, pg_ref, xu_ref, pu_ref, wg_ref, wu_ref, bg_ref, bu_ref, o_ref, u_ref):
        first = pl.program_id(1) == 0
        ug = _conv(xg_ref[...], jnp.where(first, 0.0, pg_ref[...]), wg_ref[...]) + bg_ref[...]
        uu = _conv(xu_ref[...], jnp.where(first, 0.0, pu_ref[...]), wu_ref[...]) + bu_ref[...]
        o_ref[...] = (_silu(ug) * uu).astype(o_ref.dtype)
        u_ref[0] = ug
        u_ref[1] = uu

    def x_spec(off):
        return pl.BlockSpec((tr, tc), lambda j, i: (i, off + j))

    def w_spec(k, off):
        return pl.BlockSpec((k, tc), lambda j, i: (0, off + j))

    return pl.pallas_call(
        body, name="ffn_conv_fwd", grid=(nct, nr),
        in_specs=[x_spec(0), _prev_spec(tc, 0, r8), x_spec(nct), _prev_spec(tc, nct, r8),
                  w_spec(FFN_CONV_WIDTH, 0), w_spec(FFN_CONV_WIDTH, nct), w_spec(1, 0), w_spec(1, nct)],
        out_specs=[pl.BlockSpec((tr, tc), lambda j, i: (i, j)), pl.BlockSpec((2, tr, tc), lambda j, i: (0, i, j))],
        out_shape=[jax.ShapeDtypeStruct((rows, D_FF), bf16), jax.ShapeDtypeStruct((2, rows, D_FF), f32)],
        compiler_params=_cparams(("parallel", "parallel")),
    )(u_pre, u_pre, u_pre, u_pre, w, w, b, b)


def _ffn_conv_bwd_act(u, dact):
    rows = dact.shape[0]
    tr, nr, _ = _conv_tiles(rows)
    tc = TCONV_FF
    nct = D_FF // tc

    def body(u_ref, d_ref, du_ref, dbg_ref, dbu_ref):
        first = pl.program_id(1) == 0
        ug = u_ref[0]
        uu = u_ref[1]
        d = d_ref[...]
        sig = jax.nn.sigmoid(ug)
        dug = d * uu * (sig * (1.0 + ug * (1.0 - sig)))
        duu = d * (ug * sig)
        du_ref[0] = dug
        du_ref[1] = duu

        @pl.when(first)
        def _():
            dbg_ref[...] = jnp.zeros_like(dbg_ref)
            dbu_ref[...] = jnp.zeros_like(dbu_ref)

        dbg_ref[...] += jnp.sum(dug, axis=0, keepdims=True)
        dbu_ref[...] += jnp.sum(duu, axis=0, keepdims=True)

    pair = pl.BlockSpec((2, tr, tc), lambda j, i: (0, i, j))
    vec = pl.BlockSpec((1, tc), lambda j, i: (0, j))
    return pl.pallas_call(
        body, name="ffn_conv_bwd_act", grid=(nct, nr),
        in_specs=[pair, pl.BlockSpec((tr, tc), lambda j, i: (i, j))],
        out_specs=[pair, vec, vec],
        out_shape=[jax.ShapeDtypeStruct((2, rows, D_FF), f32),
                   jax.ShapeDtypeStruct((1, D_FF), f32), jax.ShapeDtypeStruct((1, D_FF), f32)],
        compiler_params=_cparams(("parallel", "arbitrary")),
    )(u, dact)


def _conv_bwd(dy, x, x_cb0, w, name):
    k_taps = w.shape[0]
    split = dy.ndim == 3
    rows = dy.shape[-2]
    ch = dy.shape[-1] * (2 if split else 1)
    tc = TCONV_FF if split else TCONV_C
    tr, nr, r8 = _conv_tiles(rows)
    per_half = dy.shape[-1] // tc
    last8 = rows // SUBLANES - 1

    def body(dy_ref, nx_ref, x_ref, w_ref, dx_ref, dw_ref):
        i = pl.program_id(1)
        dyv = dy_ref[...]
        nxt = jnp.where(i == nr - 1, 0.0, nx_ref[...])
        xv = x_ref[...].astype(f32)
        wv = w_ref[...]

        @pl.when(i == 0)
        def _():
            dw_ref[...] = jnp.zeros_like(dw_ref)

        dx = dyv * wv[k_taps - 1:k_taps, :]
        dw_ref[k_taps - 1:k_taps, :] += jnp.sum(dyv * xv, axis=0, keepdims=True)
        for j in range(1, k_taps):
            dy_j = _shift_up(dyv, nxt, j)
            dx = dx + dy_j * wv[k_taps - 1 - j:k_taps - j, :]
            dw_ref[k_taps - 1 - j:k_taps - j, :] += jnp.sum(dy_j * xv, axis=0, keepdims=True)
        dx_ref[...] = dx.astype(dx_ref.dtype)

    tile = pl.BlockSpec((tr, tc), lambda j, i: (i, j))
    if split:
        dy_spec = pl.BlockSpec((None, tr, tc), lambda j, i: (j // per_half, i, j % per_half))
        next_spec = pl.BlockSpec((None, SUBLANES, tc),
                                 lambda j, i: (j // per_half, jnp.minimum((i + 1) * r8, last8), j % per_half))
    else:
        dy_spec = tile
        next_spec = pl.BlockSpec((SUBLANES, tc), lambda j, i: (jnp.minimum((i + 1) * r8, last8), j))
    return pl.pallas_call(
        body, name=name, grid=(ch // tc, nr),
        in_specs=[dy_spec, next_spec, pl.BlockSpec((tr, tc), lambda j, i: (i, x_cb0 + j)),
                  pl.BlockSpec((k_taps, tc), lambda j, i: (0, j))],
        out_specs=[tile, pl.BlockSpec((k_taps, tc), lambda j, i: (0, j))],
        out_shape=[jax.ShapeDtypeStruct((rows, ch), bf16), jax.ShapeDtypeStruct((k_taps, ch), f32)],
        compiler_params=_cparams(("parallel", "arbitrary")),
    )(dy, dy, x, w)


def _hdot(a, b):
    return jnp.dot(a, b, preferred_element_type=f32, precision=lax.Precision.HIGH)


def _xdot(a, b):
    return jnp.dot(a, b, preferred_element_type=f32, precision=lax.Precision.HIGHEST)


def _bdot(a, b):
    return jnp.dot(a.astype(bf16), b.astype(bf16), preferred_element_type=f32)


def _bdot_nt(a, b):
    return lax.dot_general(a.astype(bf16), b.astype(bf16), (((1,), (1,)), ((), ())), preferred_element_type=f32)


def _bdot_tn(a, b):
    return lax.dot_general(a.astype(bf16), b.astype(bf16), (((0,), (0,)), ((), ())), preferred_element_type=f32)


GDN_GROUP = 4
GDN_NGROUPS = DN_HEADS // GDN_GROUP
GDN_ROWS = GDN_GROUP * DN_CHUNK
GDN_QK_LANES = GDN_GROUP * DN_KEY_DIM
GDN_LOGIT_LANE = DN_HEADS


def _inverse_impl(lows):
    n = lows[0].shape[0]
    r = lax.broadcasted_iota(jnp.int32, (n, n), 0)
    c = lax.broadcasted_iota(jnp.int32, (n, n), 1)
    eye = (r == c).astype(f32)
    blk = jnp.right_shift(r, 3) == jnp.right_shift(c, 3)
    d = [jnp.where(blk, low, 0.0) for low in lows]
    e = [low - x for low, x in zip(lows, d)]

    def nilpotent8_inverse(xs):
        acc = [eye - x for x in xs]
        power = xs
        for _ in range(2):
            power = [_bdot(x, x) for x in power]
            acc = [_bdot(a, eye + x) for a, x in zip(acc, power)]
        return acc

    dinv = nilpotent8_inverse(d)
    ninv = nilpotent8_inverse([_bdot(x, y) for x, y in zip(dinv, e)])
    t = [_bdot(x, y) for x, y in zip(ninv, dinv)]
    for _ in range(2):
        res = [eye - x - _hdot(low, x) for low, x in zip(lows, t)]
        t = [x + _bdot(x, y) for x, y in zip(t, res)]
    return tuple(t)


@jax.custom_vjp
def _unit_lower_inverses(lows):
    return _inverse_impl(lows)


def _unit_lower_inverses_fwd(lows):
    t = _inverse_impl(lows)
    return t, t


def _unit_lower_inverses_bwd(t, ct):
    tn = (((0,), (0,)), ((), ()))
    nt = (((1,), (1,)), ((), ()))
    left = [lax.dot_general(x, g, tn, preferred_element_type=f32, precision=lax.Precision.HIGH) for x, g in zip(t, ct)]
    return (tuple(-lax.dot_general(x, y, nt, preferred_element_type=f32, precision=lax.Precision.HIGH)
                  for x, y in zip(left, t)),)


_unit_lower_inverses.defvjp(_unit_lower_inverses_fwd, _unit_lower_inverses_bwd)


@jax.custom_vjp
def _known_inverses(lows, t):
    return t


def _known_inverses_fwd(lows, t):
    return t, t


def _known_inverses_bwd(t, ct):
    return _unit_lower_inverses_bwd(t, ct) + (tuple(jnp.zeros_like(x) for x in t),)


_known_inverses.defvjp(_known_inverses_fwd, _known_inverses_bwd)


def _gdn_chunk(a_log, dt_bias, norm_w, ba, *per_group, inverses=None, keep_inverses=False):
    ng = GDN_NGROUPS
    qgs, kgs, vsts, zsts, states = [per_group[i * ng:(i + 1) * ng] for i in range(5)]
    groups = range(ng)
    n = GDN_ROWS
    r = lax.broadcasted_iota(jnp.int32, (n, n), 0)
    c = lax.broadcasted_iota(jnp.int32, (n, n), 1)
    same_head = jnp.right_shift(r, 6) == jnp.right_shift(c, 6)
    incl = jnp.logical_and(same_head, r >= c)
    strict = jnp.logical_and(same_head, r > c)
    eye = (r == c).astype(f32)
    ones = jnp.ones((n, n), f32)
    own_lanes = same_head.astype(f32)
    lane = lax.broadcasted_iota(jnp.int32, (1, LANES), 1)
    pick = lambda arr, idx: jnp.sum(jnp.where(lane == idx, arr, 0.0), axis=1, keepdims=True)
    heads = [[GDN_GROUP * g + h for h in range(GDN_GROUP)] for g in groups]
    rc = lax.broadcasted_iota(jnp.int32, (DN_CHUNK, DN_CHUNK), 0)
    cc = lax.broadcasted_iota(jnp.int32, (DN_CHUNK, DN_CHUNK), 1)

    g_all = -jnp.exp(a_log) * _softplus(ba + dt_bias)
    gc_all = _xdot((rc >= cc).astype(f32), g_all)
    gl_all = jnp.sum(g_all, axis=0, keepdims=True)
    beta = [jnp.concatenate([jax.nn.sigmoid(pick(ba, hd)) for hd in heads[g]], axis=0) for g in groups]
    gc = [jnp.concatenate([pick(gc_all, GDN_LOGIT_LANE + hd) for hd in heads[g]], axis=0) for g in groups]
    g_last = [jnp.concatenate([jnp.broadcast_to(pick(gl_all, GDN_LOGIT_LANE + hd), (DN_CHUNK, 1)) for hd in heads[g]],
                              axis=0) for g in groups]
    gr = [jnp.broadcast_to(gc[g], (n, n)).T for g in groups]
    decay = [jnp.where(incl, jnp.exp(jnp.where(incl, gc[g] - gr[g], 0.0)), 0.0) for g in groups]
    q = [jnp.concatenate([qgs[g]] * GDN_GROUP, axis=0) * own_lanes for g in groups]
    k = [jnp.concatenate([kgs[g]] * GDN_GROUP, axis=0) * own_lanes for g in groups]
    qn = [x * lax.rsqrt(jnp.sum(x * x, axis=1, keepdims=True) + L2_EPS) * (DN_KEY_DIM ** -0.5) for x in q]
    kn = [x * lax.rsqrt(jnp.sum(x * x, axis=1, keepdims=True) + L2_EPS) for x in k]
    kb = [kn[g] * beta[g] for g in groups]
    low = [jnp.where(strict, _bdot_nt(kb[g], kn[g]) * decay[g], 0.0) for g in groups]
    intra = [jnp.where(incl, _bdot_nt(qn[g], kn[g]) * decay[g], 0.0) for g in groups]
    t = _unit_lower_inverses(tuple(low)) if inverses is None else _known_inverses(tuple(low), tuple(inverses))
    u = [_bdot(t[g], vsts[g] * beta[g]) for g in groups]
    w = [_bdot(t[g], kb[g] * jnp.exp(gc[g])) for g in groups]
    sb = [s.astype(bf16) for s in states]
    v_new = [u[g] - jnp.dot(w[g].astype(bf16), sb[g], preferred_element_type=f32) for g in groups]
    o = [jnp.dot((qn[g] * jnp.exp(gc[g])).astype(bf16), sb[g], preferred_element_type=f32) for g in groups]
    o = [o[g] + _bdot(intra[g], v_new[g]) for g in groups]
    new_state = [states[g] * jnp.exp(g_last[g]) + _bdot_tn(kn[g] * jnp.exp(g_last[g] - gc[g]), v_new[g])
                 for g in groups]
    o_n = [x * lax.rsqrt(jnp.mean(x * x, axis=1, keepdims=True) + NORM_EPS) * norm_w for x in o]
    return tuple(o_n[g] * _silu(zsts[g]) for g in groups) + tuple(new_state) + (tuple(t) if keep_inverses else ())


GDN_STEP_CHUNKS = 2


def _gdn_specs(rows, reverse):
    nc = GDN_STEP_CHUNKS
    tr = nc * DN_CHUNK
    n = rows // tr
    idx = (lambda i: n - 1 - i) if reverse else (lambda i: i)
    vec = pl.BlockSpec((1, LANES), lambda i: (0, 0))
    qkv = pl.BlockSpec((tr, DN_CONV_CH), lambda i: (idx(i), 0))
    z = pl.BlockSpec((tr, DN_V_WIDTH), lambda i: (idx(i), OFF_Z // DN_V_WIDTH))
    ba = pl.BlockSpec((tr, LANES), lambda i: (idx(i), 0))
    wide = pl.BlockSpec((tr, DN_V_WIDTH), lambda i: (idx(i), 0))
    st = pl.BlockSpec((nc, DN_HEADS * DN_KEY_DIM, LANES), lambda i: (idx(i), 0, 0))
    inv = pl.BlockSpec((nc, GDN_NGROUPS * GDN_ROWS, GDN_ROWS), lambda i: (idx(i), 0, 0))
    return n, vec, qkv, z, ba, wide, st, inv


def _chunk_rows(c):
    return slice(c * DN_CHUNK, (c + 1) * DN_CHUNK)


def _gdn_slices(grp):
    q = slice(grp * GDN_QK_LANES, (grp + 1) * GDN_QK_LANES)
    k = slice(DN_QK_WIDTH + grp * GDN_QK_LANES, DN_QK_WIDTH + (grp + 1) * GDN_QK_LANES)
    heads = [slice((GDN_GROUP * grp + h) * LANES, (GDN_GROUP * grp + h + 1) * LANES) for h in range(GDN_GROUP)]
    vs = [slice(2 * DN_QK_WIDTH + s.start, 2 * DN_QK_WIDTH + s.stop) for s in heads]
    return q, k, vs, heads


def _stack_cols(ref, rows, cols):
    return jnp.concatenate([ref[rows, s] for s in cols], axis=0)


def _gdn_operands(qkv_ref, z_ref, rows, state_rows):
    sl = [_gdn_slices(grp) for grp in range(GDN_NGROUPS)]
    return ([qkv_ref[rows, q] for q, _, _, _ in sl] + [qkv_ref[rows, k] for _, k, _, _ in sl]
            + [_stack_cols(qkv_ref, rows, vs) for _, _, vs, _ in sl]
            + [_stack_cols(z_ref, rows, heads) for _, _, _, heads in sl]
            + [state_rows[grp * GDN_ROWS:(grp + 1) * GDN_ROWS, :] for grp in range(GDN_NGROUPS)])


def _gdn_fwd(a_log, dt_bias, norm_w, qkv_act, proj, ba):
    rows = qkv_act.shape[0]
    n, vec, qkv_s, z_s, ba_s, wide, st_s, inv_s = _gdn_specs(rows, False)

    def body(al_ref, dt_ref, nw_ref, qkv_ref, z_ref, ba_ref, o_ref, st_ref, inv_ref, state):
        @pl.when(pl.program_id(0) == 0)
        def _():
            state[...] = jnp.zeros_like(state)

        for c in range(GDN_STEP_CHUNKS):
            tok = _chunk_rows(c)
            st_ref[c] = state[...]
            out = _gdn_chunk(al_ref[...], dt_ref[...], nw_ref[...], ba_ref[tok, :],
                             *_gdn_operands(qkv_ref, z_ref, tok, state), keep_inverses=True)
            for grp in range(GDN_NGROUPS):
                _, _, _, heads = _gdn_slices(grp)
                rs = slice(grp * GDN_ROWS, (grp + 1) * GDN_ROWS)
                for h, s in enumerate(heads):
                    o_ref[tok, s] = out[grp][h * DN_CHUNK:(h + 1) * DN_CHUNK].astype(o_ref.dtype)
                state[rs, :] = out[GDN_NGROUPS + grp]
                inv_ref[c, rs, :] = out[2 * GDN_NGROUPS + grp]

    n_chunks = rows // DN_CHUNK
    return pl.pallas_call(
        body, name="gdn_fwd", grid=(n,),
        in_specs=[vec, vec, vec, qkv_s, z_s, ba_s], out_specs=[wide, st_s, inv_s],
        out_shape=[jax.ShapeDtypeStruct((rows, DN_V_WIDTH), bf16),
                   jax.ShapeDtypeStruct((n_chunks, DN_HEADS * DN_KEY_DIM, LANES), f32),
                   jax.ShapeDtypeStruct((n_chunks, GDN_NGROUPS * GDN_ROWS, GDN_ROWS), f32)],
        scratch_shapes=[pltpu.VMEM((DN_HEADS * DN_KEY_DIM, LANES), f32)],
        compiler_params=_cparams(("arbitrary",)),
    )(a_log, dt_bias, norm_w, qkv_act, proj, ba)


def _gdn_bwd(a_log, dt_bias, norm_w, qkv_act, proj, ba, states, inverses, do):
    rows = qkv_act.shape[0]
    n, vec, qkv_s, z_s, ba_s, wide, st_s, inv_s = _gdn_specs(rows, True)

    def body(al_ref, dt_ref, nw_ref, qkv_ref, z_ref, ba_ref, st_ref, inv_ref, do_ref,
             dal_ref, ddt_ref, dnw_ref, dqkv_ref, dz_ref, dba_ref, dstate):
        @pl.when(pl.program_id(0) == 0)
        def _():
            dstate[...] = jnp.zeros_like(dstate)
            dal_ref[...] = jnp.zeros_like(dal_ref)
            ddt_ref[...] = jnp.zeros_like(ddt_ref)
            dnw_ref[...] = jnp.zeros_like(dnw_ref)

        ng = GDN_NGROUPS
        for c in reversed(range(GDN_STEP_CHUNKS)):
            tok = _chunk_rows(c)
            kept = [inv_ref[c, grp * GDN_ROWS:(grp + 1) * GDN_ROWS, :] for grp in range(ng)]
            _, vjp = jax.vjp(functools.partial(_gdn_chunk, inverses=kept), al_ref[...], dt_ref[...], nw_ref[...],
                             ba_ref[tok, :], *_gdn_operands(qkv_ref, z_ref, tok, st_ref[c]))
            cts = tuple(_stack_cols(do_ref, tok, _gdn_slices(grp)[3]) for grp in range(ng))
            cts += tuple(dstate[grp * GDN_ROWS:(grp + 1) * GDN_ROWS, :] for grp in range(ng))
            grads = vjp(cts)
            dal_ref[...] += grads[0]
            ddt_ref[...] += grads[1]
            dnw_ref[...] += grads[2]
            dba_ref[tok, :] = grads[3]
            dqs, dks, dvs, dzs, dss = [grads[4 + i * ng:4 + (i + 1) * ng] for i in range(5)]
            for grp in range(ng):
                q, k, vs, heads = _gdn_slices(grp)
                dqkv_ref[tok, q] = dqs[grp]
                dqkv_ref[tok, k] = dks[grp]
                for h, (sv, sh) in enumerate(zip(vs, heads)):
                    rows_h = slice(h * DN_CHUNK, (h + 1) * DN_CHUNK)
                    dqkv_ref[tok, sv] = dvs[grp][rows_h]
                    dz_ref[tok, sh] = dzs[grp][rows_h].astype(dz_ref.dtype)
                dstate[grp * GDN_ROWS:(grp + 1) * GDN_ROWS, :] = dss[grp]

    return pl.pallas_call(
        body, name="gdn_bwd", grid=(n,),
        in_specs=[vec, vec, vec, qkv_s, z_s, ba_s, st_s, inv_s, wide],
        out_specs=[vec, vec, vec, qkv_s, wide, ba_s],
        out_shape=[jax.ShapeDtypeStruct((1, LANES), f32)] * 3
        + [jax.ShapeDtypeStruct((rows, DN_CONV_CH), f32), jax.ShapeDtypeStruct((rows, DN_V_WIDTH), bf16),
           jax.ShapeDtypeStruct((rows, LANES), f32)],
        scratch_shapes=[pltpu.VMEM((DN_HEADS * DN_KEY_DIM, LANES), f32)],
        compiler_params=_cparams(("arbitrary",)),
    )(a_log, dt_bias, norm_w, qkv_act, proj, ba, states, inverses, do)


def _ada_fwd(c_all, w_loc, b_loc):
    def body(c_ref, w_ref, b_ref, o_ref):
        o_ref[...] = _bdot(_silu(c_ref[...]), w_ref[...]) + b_ref[...]

    return pl.pallas_call(body, name="ada_fwd", out_shape=jax.ShapeDtypeStruct((c_all.shape[0], w_loc.shape[1]), f32),
                          compiler_params=_cparams())(c_all, w_loc, b_loc)


def _ada_bwd(c_all, dmod_cols):
    def body(c_ref, d_ref, o_ref):
        o_ref[...] = _bdot_tn(_silu(c_ref[...]), d_ref[...])

    return pl.pallas_call(body, name="ada_bwd",
                          out_shape=jax.ShapeDtypeStruct((c_all.shape[1], dmod_cols.shape[1]), f32),
                          compiler_params=_cparams())(c_all, dmod_cols)


def _sum_devices(parts):
    def body(p_ref, o_ref):
        acc = p_ref[0:1, :]
        for d in range(1, N_DEV):
            acc = acc + p_ref[d:d + 1, :]
        o_ref[...] = acc

    return pl.pallas_call(body, name="sum_small", out_shape=jax.ShapeDtypeStruct((1, parts.shape[1]), f32),
                          compiler_params=_cparams())(parts)


def _adam_math(w, g, m, v):
    m2 = ADAM_B1 * m + (1.0 - ADAM_B1) * g
    v2 = ADAM_B2 * v + (1.0 - ADAM_B2) * jnp.square(g)
    m_hat = m2 / (1.0 - ADAM_B1 ** ADAM_STEP)
    v_hat = v2 / (1.0 - ADAM_B2 ** ADAM_STEP)
    delta = -ADAM_LR * (m_hat / (jnp.sqrt(v_hat) + ADAM_EPS) + ADAM_WD * w)
    return delta, m2, v2


def _row_tile(rows):
    return _pick(rows, (256, 128, 64, 32, 16, 8))


def _adamw(w, g, m, v, name):
    rows, cols = w.shape
    tr = _row_tile(rows)

    def body(w_ref, g_ref, m_ref, v_ref, d_ref, m2_ref, v2_ref):
        d_ref[...], m2_ref[...], v2_ref[...] = _adam_math(w_ref[...], g_ref[...], m_ref[...], v_ref[...])

    tile = pl.BlockSpec((tr, cols), lambda i: (i, 0))
    return pl.pallas_call(body, name=name, grid=(rows // tr,), in_specs=[tile] * 4, out_specs=[tile] * 3,
                          out_shape=[jax.ShapeDtypeStruct(w.shape, f32)] * 3,
                          compiler_params=_cparams(("parallel",)))(w, g, m, v)


def _sum_adamw(parts, w, m, v, name):
    rows, cols = w.shape
    tr = _row_tile(rows)

    def body(p_ref, w_ref, m_ref, v_ref, g_ref, d_ref, m2_ref, v2_ref):
        g = p_ref[0].astype(f32)
        for d in range(1, N_DEV):
            g = g + p_ref[d].astype(f32)
        g_ref[...] = g
        d_ref[...], m2_ref[...], v2_ref[...] = _adam_math(w_ref[...], g, m_ref[...], v_ref[...])

    tile = pl.BlockSpec((tr, cols), lambda i: (i, 0))
    return pl.pallas_call(body, name=name, grid=(rows // tr,),
                          in_specs=[pl.BlockSpec((N_DEV, tr, cols), lambda i: (0, i, 0)), tile, tile, tile],
                          out_specs=[tile] * 4, out_shape=[jax.ShapeDtypeStruct(w.shape, f32)] * 4,
                          compiler_params=_cparams(("parallel",)))(parts, w, m, v)


def _pad_lanes(a, width):
    return jnp.pad(a, ((0, 0), (0, width - a.shape[1])))


def _cols_by_device(full):
    r, c = full.shape
    return jnp.moveaxis(full.reshape(r, N_DEV, c // N_DEV), 1, 0)


def _cols_from_devices(parts):
    d, r, n = parts.shape
    return jnp.moveaxis(parts, 0, 1).reshape(r, d * n)


def kernel(x, c, w_ada, b_ada, norm1_w, w_in, dn_conv_w, dn_A_log, dn_dt_bias, dn_norm_w, w_proj_sb, w_proj_dn, w_out, norm2_w, w_ffn_in, ffn_conv_w, ffn_conv_b, w_ffn_out, final_norm_w, loss_target, m_w_ada, m_b_ada, m_norm1_w, m_w_in, m_dn_conv_w, m_dn_A_log, m_dn_dt_bias, m_dn_norm_w, m_w_proj_sb, m_w_proj_dn, m_w_out, m_norm2_w, m_w_ffn_in, m_ffn_conv_w, m_ffn_conv_b, m_w_ffn_out, m_final_norm_w, v_w_ada, v_b_ada, v_norm1_w, v_w_in, v_dn_conv_w, v_dn_A_log, v_dn_dt_bias, v_dn_norm_w, v_w_proj_sb, v_w_proj_dn, v_w_out, v_norm2_w, v_w_ffn_in, v_ffn_conv_w, v_ffn_conv_b, v_w_ffn_out, v_final_norm_w):
    d = D_MODEL
    me = 4 * lax.axis_index("x") + 2 * lax.axis_index("y") + lax.axis_index("c")
    xs = x[0]
    target = loss_target[0]
    n_ada = w_ada.shape[2]
    n_dnc = dn_conv_w.shape[2]
    n_ffc = ffn_conv_w.shape[2]

    small = jnp.concatenate([c, dn_conv_w[0].reshape(1, -1), ffn_conv_w[0].reshape(1, -1)], axis=1)
    small = _pad_lanes(small, -(-small.shape[1] // LANES) * LANES)
    small_g, w_in_g = _all_gather([small, w_in[0].astype(bf16)], "gather_w_in")
    later = [w_proj_sb[0].astype(bf16), w_proj_dn[0].astype(bf16), w_out[0].astype(bf16),
             w_ffn_in[0].astype(bf16), w_ffn_out[0].astype(bf16)]
    gather_later = _SideComm(_gather_protocol, later, _gathered_shapes(later))
    small_g = small_g[:, 0, :]
    c_all = small_g[:, :d]
    dn_cw = _cols_from_devices(small_g[:, d:d + DN_CONV_WIDTH * n_dnc].reshape(N_DEV, DN_CONV_WIDTH, n_dnc))
    o2 = d + DN_CONV_WIDTH * n_dnc
    ffn_cw = _cols_from_devices(small_g[:, o2:o2 + FFN_CONV_WIDTH * n_ffc].reshape(N_DEV, FFN_CONV_WIDTH, n_ffc))

    w_in_full = _cols_from_devices(w_in_g)
    r_sb, r_dn, r_z = 3 * SB_WIDTH, 3 * SB_WIDTH + DN_CONV_CH, 3 * SB_WIDTH + DN_CONV_CH + DN_V_WIDTH
    r_g = r_z + 2 * DN_HEADS
    w_main = jnp.concatenate([w_in_full[:, r_g:], w_in_full[:, r_sb:r_dn], w_in_full[:, r_dn:r_z],
                              w_in_full[:, :r_sb]], axis=1)
    w_ba = _pad_lanes(w_in_full[:, r_z:r_g], LANES)

    b_loc = lax.dynamic_slice(b_ada, (0, me * n_ada), (1, n_ada))
    mod_part = _ada_fwd(c_all, w_ada[0], b_loc)
    (mod_g,) = _all_gather([mod_part], "gather_mod")
    mod = lax.dynamic_index_in_dim(mod_g, me, axis=1, keepdims=False).reshape(1, N_DEV * n_ada)
    shift1, scale1, gate1, shift2, scale2, gate2 = [mod[:, i * d:(i + 1) * d] for i in range(6)]

    logit_lanes = ((0, 0), (GDN_LOGIT_LANE, LANES - GDN_LOGIT_LANE - DN_HEADS))
    a_log = jnp.pad(dn_A_log, logit_lanes)
    dt_b = jnp.pad(dn_dt_bias, logit_lanes)

    (h,) = _stage_fwd(_f_normmod, [norm1_w, shift1, scale1], [xs], [bf16], "norm1_fwd")
    proj = _mm(h, w_main, name="in_proj")
    ba = _mm(h, w_ba, name="in_proj_ba")
    k16, k0_16, k1_16, v16, v0_16, v1_16 = _sb_prepare(proj)
    o_a, sb_runs, w_psb_g, w_pdn_g, w_out_g, w_fin_g, w_fout_g = _sb_attention_fwd2(
        proj, k16, v0_16, v1_16, side=gather_later)
    w_psb = _cols_from_devices(w_psb_g)
    w_pdn = w_pdn_g.reshape(DN_V_WIDTH, d)
    w_o = w_out_g.reshape(d, d)
    w_fin = _cols_from_devices(w_fin_g)
    w_fout = w_fout_g.reshape(D_FF, d)
    qkv_act = _dn_conv_fwd(proj, dn_cw)
    o_b, states, dn_inverses = _gdn_fwd(a_log, dt_b, dn_norm_w, qkv_act, proj, ba)
    pa = _mm(o_a, w_psb, name="proj_sb")
    pb = _mm(o_b, w_pdn, name="proj_dn")
    gates = [(proj, d, OFF_GA // d), (proj, d, OFF_GB // d)]
    (merged,) = _stage_fwd(_f_merge, [], gates + [pa, pb], [bf16], "merge_fwd")
    ao = _mm(merged, w_o, name="out_proj")
    mid_params = [gate1, norm2_w, shift2, scale2]
    x1, h2 = _stage_fwd(_f_residual_normmod, mid_params, [xs, ao], [f32, bf16], "resid1_norm2_fwd")
    u_pre = _mm(h2, w_fin, name="ffn_in")
    act, u_conv = _ffn_conv_fwd(u_pre, ffn_cw, ffn_conv_b)
    fo = _mm(act, w_fout, name="ffn_out")

    loss_p, d_gate2, d_wf, dx2, dfo = _loss_and_grads(gate2, final_norm_w.reshape(1, d), x1, fo, target)
    dact = _mm(dfo, w_fout, tb=True, name="ffn_out_dx")
    g_w_fout = _mm(act, dfo, ta=True, name="ffn_out_dw")
    du, dbg, dbu = _ffn_conv_bwd_act(u_conv, dact)
    du_pre, d_ffn_cw = _conv_bwd(du, u_pre, 0, ffn_cw, "ffn_conv_bwd")
    dh2 = _mm(du_pre, w_fin, tb=True, name="ffn_in_dx")
    g_w_fin = _mm(h2, du_pre, ta=True, name="ffn_in_dw")
    (d_gate1, d_n2w, d_shift2, d_scale2), (dx1, dao) = _stage_bwd(
        _f_residual_normmod, mid_params, [xs, ao], [dx2, dh2], [f32, bf16], "resid1_norm2_bwd")
    dmerged = _mm(dao, w_o, tb=True, name="out_proj_dx")
    g_w_o = _mm(merged, dao, ta=True, name="out_proj_dw")
    _, (dga, dgb, dpa, dpb) = _stage_bwd(_f_merge, [], gates + [pa, pb], [dmerged], [bf16] * 4, "merge_bwd")
    do_a = _mm(dpa, w_psb, tb=True, name="proj_sb_dx")
    g_w_psb = _mm(o_a, dpa, ta=True, name="proj_sb_dw")
    do_b = _mm(dpb, w_pdn, tb=True, name="proj_dn_dx")
    g_w_pdn = _mm(o_b, dpb, ta=True, name="proj_dn_dw")
    early = [_cols_by_device(g_w_psb).astype(bf16),
             g_w_pdn.reshape(N_DEV, DN_V_WIDTH // N_DEV, d).astype(bf16),
             g_w_o.reshape(N_DEV, d // N_DEV, d).astype(bf16),
             _cols_by_device(g_w_fin).astype(bf16),
             g_w_fout.reshape(N_DEV, D_FF // N_DEV, d).astype(bf16)]
    exchange_early = _SideComm(_exchange_protocol, early, [jax.ShapeDtypeStruct(a.shape, a.dtype) for a in early])
    dq, dk, dv, *recv_early = _sb_attention_bwd2(proj, k16, k0_16, k1_16, v16, sb_runs, do_a, side=exchange_early)
    d_alog, d_dtb, d_dnw, dqkv_act, dz, dba = _gdn_bwd(a_log, dt_b, dn_norm_w, qkv_act, proj, ba, states,
                                                       dn_inverses, do_b)
    d_conv_out = _dn_conv_bwd_act(proj, dn_cw, dqkv_act)
    d_dn_pre, d_dn_cw = _conv_bwd(d_conv_out, proj, OFF_DN // TCONV_C, dn_cw, "dn_conv_bwd")
    dproj = jnp.concatenate([dga, dgb, d_dn_pre, dz, dq, dk.astype(bf16), dv.astype(bf16)], axis=1)
    g_w_main = _mm(h, dproj, ta=True, name="in_proj_dw")
    g_w_ba = _mm(h, dba, ta=True, name="in_proj_ba_dw")
    g_w_in_full = jnp.concatenate([g_w_main[:, OFF_SBQ:], g_w_main[:, OFF_DN:OFF_Z], g_w_main[:, OFF_Z:OFF_SBQ],
                                   g_w_ba[:, :2 * DN_HEADS], g_w_main[:, :OFF_DN]], axis=1)
    w_in_parts = _cols_by_device(g_w_in_full).astype(bf16)
    exchange_w_in = _SideComm(_exchange_protocol, [w_in_parts], [jax.ShapeDtypeStruct(w_in_parts.shape, bf16)])
    dh, recv_w_in = _mm(dproj, w_main, tb=True, name="in_proj_dx", side=exchange_w_in)
    dh_ba = _mm(dba, w_ba, tb=True, name="in_proj_ba_dx")
    (d_n1w, d_shift1, d_scale1), (grad_x,) = _stage_bwd(
        _f_normmod, [norm1_w, shift1, scale1], [xs], [[dh, dh_ba]], [f32], "norm1_bwd", residual=(0, dx1))

    dmod = jnp.concatenate([d_shift1, d_scale1, d_gate1, d_shift2, d_scale2, d_gate2], axis=1)
    d_ffn_cb = jnp.concatenate([dbg, dbu], axis=1)
    small_parts = jnp.concatenate(
        [loss_p, dmod, d_n1w, d_alog, d_dtb, d_dnw, d_n2w, d_ffn_cb, d_wf,
         d_dn_cw.reshape(1, -1), d_ffn_cw.reshape(1, -1)], axis=1)
    (small_parts_g,) = _all_gather([small_parts], "gather_small_grads")
    tot = _sum_devices(small_parts_g[:, 0, :])
    offs = {}
    pos = 0
    for nm, width in (("loss", LANES), ("b_ada", 6 * d), ("norm1_w", d), ("dn_A_log", LANES), ("dn_dt_bias", LANES),
                      ("dn_norm_w", LANES), ("norm2_w", d), ("ffn_conv_b", 2 * D_FF), ("final_norm_w", d),
                      ("dn_conv_w", DN_CONV_WIDTH * DN_CONV_CH), ("ffn_conv_w", FFN_CONV_WIDTH * 2 * D_FF)):
        offs[nm] = (pos, width)
        pos += width
    seg = lambda nm: tot[:, offs[nm][0]:offs[nm][0] + offs[nm][1]]
    loss = tot[0, 0]
    g_b_ada = seg("b_ada")
    g_norm1 = seg("norm1_w")
    g_alog = seg("dn_A_log")[:, GDN_LOGIT_LANE:GDN_LOGIT_LANE + DN_HEADS]
    g_dtb = seg("dn_dt_bias")[:, GDN_LOGIT_LANE:GDN_LOGIT_LANE + DN_HEADS]
    g_dnw = seg("dn_norm_w")
    g_norm2 = seg("norm2_w")
    g_ffn_cb = seg("ffn_conv_b")
    g_fnw = seg("final_norm_w")
    g_dn_cw = lax.dynamic_slice(seg("dn_conv_w").reshape(DN_CONV_WIDTH, DN_CONV_CH), (0, me * n_dnc),
                                (DN_CONV_WIDTH, n_dnc))
    g_ffn_cw = lax.dynamic_slice(seg("ffn_conv_w").reshape(FFN_CONV_WIDTH, 2 * D_FF), (0, me * n_ffc),
                                 (FFN_CONV_WIDTH, n_ffc))

    dmod_all = small_parts_g[:, 0, offs["b_ada"][0]:offs["b_ada"][0] + 6 * d]
    g_w_ada = _ada_bwd(c_all, lax.dynamic_slice(dmod_all, (0, me * n_ada), (N_DEV, n_ada)))

    def pack(parts):
        flat = [p.reshape(1, -1) for p in parts]
        flat = [_pad_lanes(p, -(-p.shape[1] // LANES) * LANES) for p in flat]
        return jnp.concatenate(flat, axis=1), [p.shape[1] for p in flat]

    small_names_g = [g_b_ada, g_norm1, g_alog, g_dtb, g_dnw, g_norm2, g_ffn_cb, g_fnw, g_dn_cw, g_ffn_cw]
    small_w = [b_ada, norm1_w, dn_A_log, dn_dt_bias, dn_norm_w, norm2_w, ffn_conv_b, final_norm_w, dn_conv_w[0], ffn_conv_w[0]]
    small_m = [m_b_ada, m_norm1_w, m_dn_A_log, m_dn_dt_bias, m_dn_norm_w, m_norm2_w, m_ffn_conv_b, m_final_norm_w, m_dn_conv_w[0], m_ffn_conv_w[0]]
    small_v = [v_b_ada, v_norm1_w, v_dn_A_log, v_dn_dt_bias, v_dn_norm_w, v_norm2_w, v_ffn_conv_b, v_final_norm_w, v_dn_conv_w[0], v_ffn_conv_w[0]]
    pg, widths = pack(small_names_g)
    pw, _ = pack(small_w)
    pm, _ = pack(small_m)
    pv, _ = pack(small_v)
    s_delta, s_m, s_v = _adamw(pw, pg, pm, pv, "adamw_small")

    def unpack(flat):
        out, pos = [], 0
        for ref_arr, width in zip(small_w, widths):
            out.append(flat[:, pos:pos + ref_arr.size].reshape(ref_arr.shape))
            pos += width
        return out

    small_grads = [g.reshape(w_.shape) for g, w_ in zip(small_names_g, small_w)]
    small_delta, small_newm, small_newv = unpack(s_delta), unpack(s_m), unpack(s_v)

    ada_delta, ada_m, ada_v = _adamw(w_ada[0], g_w_ada, m_w_ada[0], v_w_ada[0], "adamw_ada")

    recv = [recv_w_in] + list(recv_early)
    big = {}
    for nm, parts, w_, m_, v_ in (("w_in", recv[0], w_in, m_w_in, v_w_in),
                                  ("w_proj_sb", recv[1], w_proj_sb, m_w_proj_sb, v_w_proj_sb),
                                  ("w_proj_dn", recv[2], w_proj_dn, m_w_proj_dn, v_w_proj_dn),
                                  ("w_out", recv[3], w_out, m_w_out, v_w_out),
                                  ("w_ffn_in", recv[4], w_ffn_in, m_w_ffn_in, v_w_ffn_in),
                                  ("w_ffn_out", recv[5], w_ffn_out, m_w_ffn_out, v_w_ffn_out)):
        big[nm] = [t[None] for t in _sum_adamw(parts, w_[0], m_[0], v_[0], "adamw_" + nm)]

    sg = dict(zip(["b_ada", "norm1_w", "dn_A_log", "dn_dt_bias", "dn_norm_w", "norm2_w", "ffn_conv_b", "final_norm_w",
                   "dn_conv_w", "ffn_conv_w"], range(10)))

    def small_out(table, nm):
        val = table[sg[nm]]
        return val[None] if nm in ("dn_conv_w", "ffn_conv_w") else val

    order = ["w_ada", "b_ada", "norm1_w", "w_in", "dn_conv_w", "dn_A_log", "dn_dt_bias", "dn_norm_w", "w_proj_sb",
             "w_proj_dn", "w_out", "norm2_w", "w_ffn_in", "ffn_conv_w", "ffn_conv_b", "w_ffn_out", "final_norm_w"]
    groups = []
    for k, small_table in enumerate((small_grads, small_delta, small_newm, small_newv)):
        row = []
        for nm in order:
            if nm == "w_ada":
                row.append((g_w_ada, ada_delta, ada_m, ada_v)[k][None])
            elif nm in big:
                row.append(big[nm][k])
            else:
                row.append(small_out(small_table, nm))
        groups.append(row)
    return (loss, grad_x[None], *groups[0], *groups[1], *groups[2], *groups[3])
```

```python
import functools

import jax
import jax.numpy as jnp
from jax import lax
from jax.experimental import pallas as pl
from jax.experimental.pallas import tpu as pltpu

f32 = jnp.float32
bf16 = jnp.bfloat16

D_MODEL = 1024
SB_HEADS = 8
SB_HEAD_DIM = 64
SB_WIDTH = SB_HEADS * SB_HEAD_DIM
SB_QBLOCK = 128
DN_HEADS = 8
DN_KEY_DIM = 64
DN_VAL_DIM = 128
DN_QK_WIDTH = DN_HEADS * DN_KEY_DIM
DN_V_WIDTH = DN_HEADS * DN_VAL_DIM
DN_CONV_CH = 2 * DN_QK_WIDTH + DN_V_WIDTH
DN_CONV_WIDTH = 4
DN_CHUNK = 64
D_FF = 2816
FFN_CONV_WIDTH = 3
NORM_EPS = 1e-6
L2_EPS = 1e-6
ADAM_LR = 0.001
ADAM_B1 = 0.9
ADAM_B2 = 0.999
ADAM_EPS = 1e-08
ADAM_WD = 0.01
ADAM_STEP = 10

N_DEV = 8
MESH = pl.DeviceIdType.MESH

LANES = 128
SUBLANES = 8
VMEM_LIMIT = 48 * 1024 * 1024

OFF_GA = 0
OFF_GB = D_MODEL
OFF_DN = 2 * D_MODEL
OFF_Z = OFF_DN + DN_CONV_CH
OFF_SBQ = OFF_Z + DN_V_WIDTH
OFF_SBK = OFF_SBQ + SB_WIDTH
OFF_SBV = OFF_SBK + SB_WIDTH
MAIN_WIDTH = OFF_SBV + SB_WIDTH

TM = 512
TCONV_R = 512
TCONV_C = 1024
TCONV_FF = D_FF // 2
SB_PAIRS_FWD = 4
SB_PAIRS_BWD = 4
SB_DEAD = -106.0
SB_NEVER = -1e30


def _cparams(sem=None):
    return pltpu.CompilerParams(dimension_semantics=sem, vmem_limit_bytes=VMEM_LIMIT)


def _pick(n, cands):
    for c in cands:
        if n % c == 0:
            return c
    return n


def _my_pos():
    return lax.axis_index("x"), lax.axis_index("y"), lax.axis_index("c")


def _flip(v, bit):
    return 1 - v if bit else v


def _comm_scratch(n):
    return [pltpu.SemaphoreType.DMA((n, 7)), pltpu.SemaphoreType.DMA((n, 7)), pltpu.SemaphoreType.DMA((n,))]


def _gather_protocol(ins, outs, send_sems, recv_sems, local_sems):
    n = len(ins)
    x, y, c = _my_pos()
    me, sibling = (x, y, c), (x, y, 1 - c)
    chips = [(1 - x, y), (x, 1 - y), (1 - x, 1 - y)]

    def slot(out, pos):
        return out.at[4 * pos[0] + 2 * pos[1] + pos[2]]

    def copy(a, k, block, to, src=None):
        return pltpu.make_async_remote_copy(
            src_ref=slot(outs[a], block) if src is None else src, dst_ref=slot(outs[a], block),
            send_sem=send_sems.at[a, k], recv_sem=recv_sems.at[a, k], device_id=to, device_id_type=MESH)

    def local(a):
        return pltpu.make_async_copy(ins[a], slot(outs[a], me), local_sems.at[a])

    def first(a):
        return [copy(a, 0, me, sibling, src=ins[a])] + [copy(a, 1 + j, me, (*chip, c), src=ins[a])
                                                         for j, chip in enumerate(chips)]

    def start():
        for a in range(n):
            local(a).start()
            for cp in first(a):
                cp.start()

    def finish():
        forwards = []
        for a in range(n):
            for j, chip in enumerate(chips):
                copy(a, 1 + j, (*chip, c), me).wait_recv()
                fwd = copy(a, 4 + j, (*chip, c), sibling)
                fwd.start()
                forwards.append(fwd)
        for a in range(n):
            copy(a, 0, sibling, me).wait_recv()
            for j, chip in enumerate(chips):
                copy(a, 4 + j, (*chip, 1 - c), me).wait_recv()
        for a in range(n):
            for cp in first(a):
                cp.wait_send()
        for cp in forwards:
            cp.wait_send()
        for a in range(n):
            local(a).wait()

    return start, finish


def _exchange_protocol(ins, outs, send_sems, recv_sems, local_sems):
    n = len(ins)
    x, y, c = _my_pos()
    me_idx = 4 * x + 2 * y + c

    def local(a):
        return pltpu.make_async_copy(ins[a].at[me_idx], outs[a].at[me_idx], local_sems.at[a])

    def copies(a, m):
        peer = (_flip(x, m & 4), _flip(y, m & 2), _flip(c, m & 1))
        peer_idx = 4 * peer[0] + 2 * peer[1] + peer[2]
        sems = dict(send_sem=send_sems.at[a, m - 1], recv_sem=recv_sems.at[a, m - 1], device_id=peer,
                    device_id_type=MESH)
        send = pltpu.make_async_remote_copy(src_ref=ins[a].at[peer_idx], dst_ref=outs[a].at[me_idx], **sems)
        recv = pltpu.make_async_remote_copy(src_ref=ins[a].at[peer_idx], dst_ref=outs[a].at[peer_idx], **sems)
        return send, recv

    def start():
        for a in range(n):
            local(a).start()
            for m in range(1, N_DEV):
                copies(a, m)[0].start()

    def finish():
        for a in range(n):
            for m in range(1, N_DEV):
                copies(a, m)[1].wait_recv()
        for a in range(n):
            for m in range(1, N_DEV):
                copies(a, m)[0].wait_send()
            local(a).wait()

    return start, finish


def _collective_call(protocol, arrs, out_shapes, name):
    n = len(arrs)

    def body(*refs):
        start, finish = protocol(refs[:n], refs[n:2 * n], *refs[2 * n:])
        start()
        finish()

    any_spec = pl.BlockSpec(memory_space=pl.ANY)
    return pl.pallas_call(body, name=name, out_shape=out_shapes, in_specs=[any_spec] * n, out_specs=[any_spec] * n,
                          scratch_shapes=_comm_scratch(n))(*arrs)


def _gathered_shapes(arrs):
    return [jax.ShapeDtypeStruct((N_DEV,) + a.shape, a.dtype) for a in arrs]


def _all_gather(arrs, name):
    return _collective_call(_gather_protocol, arrs, _gathered_shapes(arrs), name)


MM_BLOCK_BYTES = 7 * 1024 * 1024
MM_TILE_CAP = 1664


def _lane_tile(n, cap):
    fits = [t for t in range(LANES, min(n, cap) + 1, LANES) if n % t == 0]
    return max(fits) if fits else n


def _mm_tiles(m_dim, n_dim, k_dim, a_bytes, b_bytes):
    tm = _lane_tile(m_dim, MM_TILE_CAP)
    tn = _lane_tile(n_dim, MM_TILE_CAP)
    while tm * tn * 4 > MM_BLOCK_BYTES:
        if tn >= tm and (tn // 2) % LANES == 0:
            tn //= 2
        else:
            tm //= 2
    if (m_dim % (2 * tm) == 0 and 2 * tm * k_dim * a_bytes <= MM_BLOCK_BYTES
            and 2 * tm * tn * 4 <= MM_BLOCK_BYTES):
        tm *= 2
    tk = k_dim
    if k_dim % LANES == 0:
        units = k_dim // LANES
        fits = [u for u in range(1, units + 1) if units % u == 0
                and u * LANES * max(tm * a_bytes, tn * b_bytes) <= MM_BLOCK_BYTES]
        tk = max(fits) * LANES
    return tm, tn, tk


def _mm(a, b, *, ta=False, tb=False, name, side=None):
    (k_dim, m_dim) = a.shape if ta else a.shape[::-1]
    (n_dim, kb_dim) = b.shape if tb else b.shape[::-1]
    assert k_dim == kb_dim, (a.shape, b.shape, ta, tb)
    tm, tn, tk = _mm_tiles(m_dim, n_dim, k_dim, a.dtype.itemsize, b.dtype.itemsize)
    nk = k_dim // tk
    grid = (m_dim // tm, n_dim // tn, nk)
    dims = (((0 if ta else 1,), (1 if tb else 0,)), ((), ()))
    ns = side.n if side else 0

    def body(a_ref, b_ref, *rest):
        if side:
            side.run(rest[:ns], rest[ns + 1:2 * ns + 1], rest[2 * ns + 1:], *_grid_ends(grid),
                     lambda: compute(a_ref, b_ref, rest[ns]))
        else:
            compute(a_ref, b_ref, rest[0])

    def compute(a_ref, b_ref, o_ref):
        part = lax.dot_general(a_ref[...].astype(bf16), b_ref[...].astype(bf16), dims, preferred_element_type=f32)
        if nk == 1:
            o_ref[...] = part
        else:
            k = pl.program_id(2)

            @pl.when(k == 0)
            def _():
                o_ref[...] = part

            @pl.when(k > 0)
            def _():
                o_ref[...] += part

    a_spec = pl.BlockSpec((tk, tm), lambda i, j, k: (k, i)) if ta else pl.BlockSpec((tm, tk), lambda i, j, k: (i, k))
    b_spec = pl.BlockSpec((tn, tk), lambda i, j, k: (j, k)) if tb else pl.BlockSpec((tk, tn), lambda i, j, k: (k, j))
    out_spec = pl.BlockSpec((tm, tn), lambda i, j, k: (i, j))
    out_shape = jax.ShapeDtypeStruct((m_dim, n_dim), f32)
    if not side:
        return pl.pallas_call(body, name=name, grid=grid, in_specs=[a_spec, b_spec], out_specs=out_spec,
                              out_shape=out_shape,
                              compiler_params=_cparams(("parallel", "parallel", "arbitrary")))(a, b)
    return pl.pallas_call(
        body, name=name, grid=grid, in_specs=[a_spec, b_spec] + side.specs(), out_specs=[out_spec] + side.specs(),
        out_shape=[out_shape] + side.out_shapes, scratch_shapes=_comm_scratch(ns),
        compiler_params=_cparams(("arbitrary", "arbitrary", "arbitrary")))(a, b, *side.arrs)


def _win(t):
    return t if isinstance(t, tuple) else (t, t.shape[1], 0)


def _tile_spec(width, cb, tm):
    return pl.BlockSpec((tm, width), lambda i: (i, cb))


def _param_spec(p):
    return pl.BlockSpec(p.shape, lambda i: (0, 0))


def _stage_fwd(f, params, tiles, out_dtypes, name):
    tiles = [_win(t) for t in tiles]
    rows = tiles[0][0].shape[0]
    tm = min(TM, rows)
    avals = jax.eval_shape(f, *[jax.ShapeDtypeStruct(p.shape, f32) for p in params],
                           *[jax.ShapeDtypeStruct((tm, w), f32) for _, w, _ in tiles])
    n_p, n_t = len(params), len(tiles)

    def body(*refs):
        p = [r[...] for r in refs[:n_p]]
        t = [r[...].astype(f32) for r in refs[n_p:n_p + n_t]]
        for o_ref, val in zip(refs[n_p + n_t:], f(*p, *t)):
            o_ref[...] = val.astype(o_ref.dtype)

    return pl.pallas_call(
        body, name=name, grid=(rows // tm,),
        in_specs=[_param_spec(p) for p in params] + [_tile_spec(w, cb, tm) for _, w, cb in tiles],
        out_specs=[_tile_spec(a.shape[1], 0, tm) for a in avals],
        out_shape=[jax.ShapeDtypeStruct((rows, a.shape[1]), dt) for a, dt in zip(avals, out_dtypes)],
        compiler_params=_cparams(("parallel",)),
    )(*params, *[t[0] for t in tiles])


def _stage_bwd(f, params, tiles, cts, grad_dtypes, name, residual=None):
    tiles = [_win(t) for t in tiles]
    rows = tiles[0][0].shape[0]
    tm = min(TM, rows)
    cts = [list(g) if isinstance(g, (list, tuple)) else [g] for g in cts]
    flat_cts = [a for g in cts for a in g]
    n_p, n_t, n_c = len(params), len(tiles), len(flat_cts)
    has_res = residual is not None
    want = [j for j, dt in enumerate(grad_dtypes) if dt is not None]

    def body(*refs):
        i = pl.program_id(0)
        p = [r[...] for r in refs[:n_p]]
        t = [r[...].astype(f32) for r in refs[n_p:n_p + n_t]]
        ct_vals = [r[...].astype(f32) for r in refs[n_p + n_t:n_p + n_t + n_c]]
        ct, at = [], 0
        for g in cts:
            ct.append(functools.reduce(jnp.add, ct_vals[at:at + len(g)]))
            at += len(g)
        ct = tuple(ct)
        pos = n_p + n_t + n_c
        res_ref = refs[pos] if has_res else None
        pos += 1 if has_res else 0
        dp_refs = refs[pos:pos + n_p]
        dt_refs = refs[pos + n_p:]
        _, vjp = jax.vjp(f, *p, *t)
        grads = vjp(ct)

        @pl.when(i == 0)
        def _():
            for r in dp_refs:
                r[...] = jnp.zeros_like(r)

        for r, g in zip(dp_refs, grads[:n_p]):
            r[...] += g
        for r, j in zip(dt_refs, want):
            g = grads[n_p + j]
            if has_res and j == residual[0]:
                g = g + res_ref[...].astype(f32)
            r[...] = g.astype(r.dtype)

    in_arrays = list(params) + [t[0] for t in tiles] + flat_cts
    in_specs = ([_param_spec(p) for p in params] + [_tile_spec(w, cb, tm) for _, w, cb in tiles]
                + [_tile_spec(c.shape[1], 0, tm) for c in flat_cts])
    if has_res:
        in_arrays.append(residual[1])
        in_specs.append(_tile_spec(residual[1].shape[1], 0, tm))
    out_shape = ([jax.ShapeDtypeStruct(p.shape, f32) for p in params]
                 + [jax.ShapeDtypeStruct((rows, tiles[j][1]), grad_dtypes[j]) for j in want])
    out_specs = [_param_spec(p) for p in params] + [_tile_spec(tiles[j][1], 0, tm) for j in want]
    outs = pl.pallas_call(
        body, name=name, grid=(rows // tm,), in_specs=in_specs, out_specs=out_specs, out_shape=out_shape,
        compiler_params=_cparams(("arbitrary",)),
    )(*in_arrays)
    return outs[:n_p], outs[n_p:]


def _rms(x, w):
    return x * lax.rsqrt(jnp.mean(x * x, axis=-1, keepdims=True) + NORM_EPS) * w


def _f_normmod(w, shift, scale, x):
    return (_rms(x, w) * (1.0 + scale) + shift,)


def _f_merge(ga, gb, pa, pb):
    return (jax.nn.sigmoid(ga) * pa + jax.nn.sigmoid(gb) * pb,)


def _f_residual_normmod(gate, w, shift, scale, x, branch):
    x1 = x + gate * branch
    return x1, _rms(x1, w) * (1.0 + scale) + shift


def _f_loss(gate, wf, x1, fo, target):
    y = _rms(x1 + gate * fo, wf)
    err = jnp.square(y - target)
    return (0.5 * jnp.sum(jnp.mean(err, axis=-1, keepdims=True), axis=0, keepdims=True),)


def _loss_and_grads(gate2, wf, x1, fo, target):
    rows, d = x1.shape
    tm = min(TM, rows)

    def body(g_ref, w_ref, x_ref, fo_ref, t_ref, loss_ref, dg_ref, dw_ref, dx_ref, dfo_ref):
        i = pl.program_id(0)
        (val,), vjp = jax.vjp(_f_loss, g_ref[...], w_ref[...], x_ref[...], fo_ref[...], t_ref[...])
        dg, dw, dx, dfo, _ = vjp((jnp.ones((1, 1), f32),))

        @pl.when(i == 0)
        def _():
            loss_ref[...] = jnp.zeros_like(loss_ref)
            dg_ref[...] = jnp.zeros_like(dg_ref)
            dw_ref[...] = jnp.zeros_like(dw_ref)

        loss_ref[...] += jnp.broadcast_to(val, loss_ref.shape)
        dg_ref[...] += dg
        dw_ref[...] += dw
        dx_ref[...] = dx
        dfo_ref[...] = dfo.astype(bf16)

    vec = pl.BlockSpec((1, d), lambda i: (0, 0))
    tile = pl.BlockSpec((tm, d), lambda i: (i, 0))
    return pl.pallas_call(
        body, name="loss_fwd_bwd", grid=(rows // tm,),
        in_specs=[vec, vec, tile, tile, tile],
        out_specs=[pl.BlockSpec((1, LANES), lambda i: (0, 0)), vec, vec, tile, tile],
        out_shape=[jax.ShapeDtypeStruct((1, LANES), f32), jax.ShapeDtypeStruct((1, d), f32),
                   jax.ShapeDtypeStruct((1, d), f32), jax.ShapeDtypeStruct((rows, d), f32),
                   jax.ShapeDtypeStruct((rows, d), bf16)],
        compiler_params=_cparams(("arbitrary",)),
    )(gate2, wf, x1, fo, target)


def _softplus(z):
    return jnp.maximum(z, 0.0) + jnp.log(1.0 + jnp.exp(-jnp.abs(z)))


def _split_dot(a, m):
    hi = a.astype(bf16)
    lo = (a - hi.astype(f32)).astype(bf16)
    return jnp.dot(hi, m, preferred_element_type=f32) + jnp.dot(lo, m, preferred_element_type=f32)


def _suffix_matrix(n):
    r = lax.broadcasted_iota(jnp.int32, (n, n), 0)
    c = lax.broadcasted_iota(jnp.int32, (n, n), 1)
    return (r > c).astype(bf16)


def _head_masks():
    lane = lax.broadcasted_iota(jnp.int32, (1, LANES), 1)
    return [(lane < SB_HEAD_DIM).astype(f32), (lane >= SB_HEAD_DIM).astype(f32)]


def _sb_prepare(proj):
    def f(k, v):
        lane = lax.broadcasted_iota(jnp.int32, (1, SB_WIDTH), 1)
        m0 = (jnp.bitwise_and(lane, LANES - 1) < SB_HEAD_DIM).astype(f32)
        m1 = 1.0 - m0
        return k, k * m0, k * m1, v, v * m0, v * m1

    wins = [(proj, SB_WIDTH, OFF_SBK // SB_WIDTH), (proj, SB_WIDTH, OFF_SBV // SB_WIDTH)]
    return _stage_fwd(f, [], wins, [bf16] * 6, "sb_prepare")


def _stack_heads(x):
    m0, m1 = _head_masks()
    return jnp.concatenate([x * m0, x * m1], axis=0)


def _sb_logits(qst, k, t_pos2, kb, bq, masked):
    z = lax.dot_general(qst, k, (((1,), (1,)), ((), ())), preferred_element_type=f32)
    l = -_softplus(z)
    if masked:
        s_pos = kb * bq + lax.broadcasted_iota(jnp.int32, (1, bq), 1)
        causal = s_pos < t_pos2
        l = jnp.where(causal, l, 0.0)
    else:
        causal = None
    return z, l, causal


class _SideComm:
    def __init__(self, protocol, arrs, out_shapes):
        self.protocol, self.arrs, self.out_shapes = protocol, list(arrs), list(out_shapes)
        self.n = len(self.arrs)

    def specs(self):
        return [pl.BlockSpec(memory_space=pl.ANY)] * self.n

    def run(self, in_refs, out_refs, sems, first, last, compute):
        start, finish = self.protocol(in_refs, out_refs, *sems)
        pl.when(first)(start)
        compute()
        pl.when(last)(finish)


def _grid_ends(grid):
    ids = [pl.program_id(axis) for axis in range(len(grid))]
    first = functools.reduce(jnp.logical_and, [i == 0 for i in ids])
    last = functools.reduce(jnp.logical_and, [i == g - 1 for i, g in zip(ids, grid)])
    return first, last


def _sb_attention_fwd2(proj, k16, v0_16, v1_16, side=None):
    rows = proj.shape[0]
    bq = SB_QBLOCK
    nq = rows // bq
    assert nq <= LANES, "one lane per key block"
    npair = SB_WIDTH // LANES
    scale = SB_HEAD_DIM ** -0.5

    npp = SB_PAIRS_FWD
    wq = npp * LANES
    grid = (npair // npp, nq)
    ns = side.n if side else 0

    def body(q_ref, k_ref, v0_ref, v1_ref, *rest):
        o_ref, runs_ref = rest[ns], rest[ns + 1]
        if side:
            side.run(rest[:ns], rest[ns + 2:2 * ns + 2], rest[2 * ns + 2:], *_grid_ends(grid),
                     lambda: compute(q_ref, k_ref, v0_ref, v1_ref, o_ref, runs_ref))
        else:
            compute(q_ref, k_ref, v0_ref, v1_ref, o_ref, runs_ref)

    def compute(q_ref, k_ref, v0_ref, v1_ref, o_ref, runs_ref):
        qi = pl.program_id(1)
        pairs = [slice(pp * LANES, (pp + 1) * LANES) for pp in range(npp)]
        qst = [(_stack_heads(q_ref[:, s]) * scale).astype(bf16) for s in pairs]
        r = lax.broadcasted_iota(jnp.int32, (bq, 2 * bq), 0)
        c = lax.broadcasted_iota(jnp.int32, (bq, 2 * bq), 1)
        m2 = jnp.logical_or(r > c, c >= bq).astype(bf16)
        t_pos = qi * bq + lax.broadcasted_iota(jnp.int32, (bq, 1), 0)
        t_pos2 = jnp.concatenate([t_pos, t_pos], axis=0)
        lane = lax.broadcasted_iota(jnp.int32, (1, LANES), 1)
        runs_ref[...] = jnp.full(runs_ref.shape, SB_NEVER, f32)

        def tiles(kbs, carry, masked):
            jobs = [(pp, kb) for kb in kbs for pp in range(npp)]
            rows_k = [pl.ds(pl.multiple_of(kb * bq, bq), bq) for _, kb in jobs]
            zl = [_sb_logits(qst[pp], k_ref[rk, pairs[pp]], t_pos2, kb, bq, masked) for (pp, kb), rk in zip(jobs, rows_k)]
            cs = [_split_dot(l, m2) for _, l, _ in zl]
            run = [cr[0] for cr in carry]
            acc = [cr[1] for cr in carry]
            probs = []
            for (pp, kb), (z, l, causal), cs2 in zip(jobs, zl, cs):
                a = jnp.exp(z + l + cs2[:, :bq] + run[pp])
                if masked:
                    a = jnp.where(causal, a, 0.0)
                probs.append(a.astype(bf16))
                for hh in range(2):
                    cols = slice((2 * pp + hh) * LANES, (2 * pp + hh + 1) * LANES)
                    runs_ref[:, cols] = jnp.where(lane == kb, run[pp][hh * bq:(hh + 1) * bq], runs_ref[:, cols])
                run[pp] = run[pp] + cs2[:, bq:]
            for (pp, kb), rk, ab in zip(jobs, rows_k, probs):
                acc[pp] = (acc[pp] + jnp.dot(ab[:bq], v0_ref[rk, pairs[pp]], preferred_element_type=f32)
                           + jnp.dot(ab[bq:], v1_ref[rk, pairs[pp]], preferred_element_type=f32))
            return tuple(zip(run, acc))

        zero = (jnp.zeros((2 * bq, bq), f32), jnp.zeros((bq, LANES), f32))
        carry = tiles([qi], (zero,) * npp, True)

        def alive(cr):
            return functools.reduce(jnp.maximum, [jnp.max(run) for run, _ in cr]) > SB_DEAD

        def two(state):
            i, _, cr = state
            cr = tiles([qi - 1 - 2 * i, qi - 2 - 2 * i], cr, False)
            return i + 1, alive(cr), cr

        n_two = qi // 2
        i_end, still, carry = lax.while_loop(lambda st: jnp.logical_and(st[0] < n_two, st[1]), two,
                                             (jnp.int32(0), alive(carry), carry))
        last_one = jnp.logical_and(qi % 2 == 1, jnp.logical_and(still, i_end == n_two))
        carry = lax.cond(last_one, lambda cr: tiles([0], cr, False), lambda cr: cr, carry)
        for pp in range(npp):
            o_ref[:, pairs[pp]] = carry[pp][1]

    kv = pl.BlockSpec((rows, wq), lambda p, i: (0, p))
    return pl.pallas_call(
        body, name="sb_attn_fwd", grid=grid,
        in_specs=[pl.BlockSpec((bq, wq), lambda p, i: (i, OFF_SBQ // wq + p)), kv, kv, kv] + (side.specs() if side else []),
        out_specs=[pl.BlockSpec((bq, wq), lambda p, i: (i, p)),
                   pl.BlockSpec((bq, 2 * wq), lambda p, i: (i, p))] + (side.specs() if side else []),
        out_shape=[jax.ShapeDtypeStruct((rows, SB_WIDTH), f32),
                   jax.ShapeDtypeStruct((rows, SB_HEADS * LANES), f32)] + (side.out_shapes if side else []),
        scratch_shapes=_comm_scratch(ns) if side else [],
        compiler_params=_cparams(("arbitrary", "arbitrary")),
    )(proj, k16, v0_16, v1_16, *(side.arrs if side else []))


def _sb_attention_bwd2(proj, k16, k0_16, k1_16, v16, runs, do, side=None):
    rows = proj.shape[0]
    bq = SB_QBLOCK
    nq = rows // bq
    npair = SB_WIDTH // LANES
    scale = SB_HEAD_DIM ** -0.5
    tn = (((0,), (0,)), ((), ()))
    nt = (((1,), (1,)), ((), ()))

    npp = SB_PAIRS_BWD
    wq = npp * LANES
    grid = (npair // npp, nq)
    ns = side.n if side else 0

    def body(q_ref, k_ref, k0_ref, k1_ref, v_ref, runs_ref, do_ref, *rest):
        outs = rest[ns:ns + 3]
        ins = (q_ref, k_ref, k0_ref, k1_ref, v_ref, runs_ref, do_ref)
        if side:
            side.run(rest[:ns], rest[ns + 3:2 * ns + 3], rest[2 * ns + 3:], *_grid_ends(grid),
                     lambda: compute(*ins, *outs))
        else:
            compute(*ins, *outs)

    def compute(q_ref, k_ref, k0_ref, k1_ref, v_ref, runs_ref, do_ref, dq_ref, dk_ref, dv_ref):
        qi = pl.program_id(1)

        @pl.when(qi == 0)
        def _():
            dk_ref[...] = jnp.zeros_like(dk_ref)
            dv_ref[...] = jnp.zeros_like(dv_ref)

        pairs = [slice(pp * LANES, (pp + 1) * LANES) for pp in range(npp)]
        qst = [(_stack_heads(q_ref[:, s]) * scale).astype(bf16) for s in pairs]
        dost = [_stack_heads(do_ref[:, s]).astype(bf16) for s in pairs]
        runs = [jnp.concatenate([runs_ref[:, 2 * pp * LANES:(2 * pp + 1) * LANES],
                                 runs_ref[:, (2 * pp + 1) * LANES:(2 * pp + 2) * LANES]], axis=0) for pp in range(npp)]
        r = lax.broadcasted_iota(jnp.int32, (bq, 2 * bq), 0)
        c = lax.broadcasted_iota(jnp.int32, (bq, 2 * bq), 1)
        suffix_m = _suffix_matrix(bq)
        m2 = jnp.logical_or(r < c, c >= bq).astype(bf16)
        t_pos = qi * bq + lax.broadcasted_iota(jnp.int32, (bq, 1), 0)
        t_pos2 = jnp.concatenate([t_pos, t_pos], axis=0)
        lane = lax.broadcasted_iota(jnp.int32, (1, LANES), 1)

        def tiles(kbs, carry, masked):
            jobs = [(pp, kb) for kb in kbs for pp in range(npp)]
            rows_k = [pl.ds(pl.multiple_of(kb * bq, bq), bq) for _, kb in jobs]
            zl = [_sb_logits(qst[pp], k_ref[rk, pairs[pp]], t_pos2, kb, bq, masked) for (pp, kb), rk in zip(jobs, rows_k)]
            das = [lax.dot_general(dost[pp], v_ref[rk, pairs[pp]], nt, preferred_element_type=f32)
                   for (pp, kb), rk in zip(jobs, rows_k)]
            sticks = [_split_dot(l, suffix_m) for _, l, _ in zl]
            probs, ps = [], []
            for (pp, kb), (z, l, causal), stick, da in zip(jobs, zl, sticks, das):
                run = jnp.sum(jnp.where(lane == kb, runs[pp], 0.0), axis=1, keepdims=True)
                a = jnp.exp(z + l + stick + run)
                if masked:
                    a = jnp.where(causal, a, 0.0)
                probs.append(a.astype(bf16))
                ps.append(da * a)
            pcs = [_split_dot(p, m2) for p in ps]
            pref = [cr[0] for cr in carry]
            dq_acc = [cr[1] for cr in carry]
            dzs = []
            for (pp, kb), (z, l, causal), p, pc2 in zip(jobs, zl, ps, pcs):
                dz = p * jnp.exp(l) - jnp.exp(z + l) * (pc2[:, :bq] + pref[pp])
                if masked:
                    dz = jnp.where(causal, dz, 0.0)
                dzs.append(dz.astype(bf16))
                pref[pp] = pref[pp] + pc2[:, bq:]
            for (pp, kb), rk, dzb, ab in zip(jobs, rows_k, dzs, probs):
                cols = pairs[pp]
                dq_acc[pp] = (dq_acc[pp] + jnp.dot(dzb[:bq], k0_ref[rk, cols], preferred_element_type=f32)
                              + jnp.dot(dzb[bq:], k1_ref[rk, cols], preferred_element_type=f32))
                dk_ref[rk, cols] += lax.dot_general(dzb, qst[pp], tn, preferred_element_type=f32)
                dv_ref[rk, cols] += lax.dot_general(ab, dost[pp], tn, preferred_element_type=f32)
            return tuple(zip(pref, dq_acc))

        zero = (jnp.zeros((2 * bq, bq), f32), jnp.zeros((bq, LANES), f32))
        colmax = functools.reduce(jnp.maximum, [jnp.max(x, axis=0, keepdims=True) for x in runs])
        live = jnp.logical_and(colmax > SB_DEAD, lane < qi)
        kb0 = jnp.minimum(jnp.min(jnp.where(live, lane, LANES)), qi)
        n_blocks = qi - kb0
        carry = lax.fori_loop(0, n_blocks // 2, lambda i, cr: tiles([kb0 + 2 * i, kb0 + 2 * i + 1], cr, False),
                              (zero,) * npp)
        carry = lax.cond(n_blocks % 2 == 1, lambda cr: tiles([qi - 1], cr, False), lambda cr: cr, carry)
        carry = tiles([qi], carry, True)
        for pp in range(npp):
            dq_ref[:, pairs[pp]] = (carry[pp][1] * scale).astype(dq_ref.dtype)

    blk = pl.BlockSpec((bq, wq), lambda p, i: (i, p))
    full = pl.BlockSpec((rows, wq), lambda p, i: (0, p), pipeline_mode=pl.Buffered(1))
    return pl.pallas_call(
        body, name="sb_attn_bwd", grid=grid,
        in_specs=[pl.BlockSpec((bq, wq), lambda p, i: (i, OFF_SBQ // wq + p)), full, full, full, full,
                  pl.BlockSpec((bq, 2 * wq), lambda p, i: (i, p)), blk] + (side.specs() if side else []),
        out_specs=[blk, full, full] + (side.specs() if side else []),
        out_shape=[jax.ShapeDtypeStruct((rows, SB_WIDTH), bf16), jax.ShapeDtypeStruct((rows, SB_WIDTH), f32),
                   jax.ShapeDtypeStruct((rows, SB_WIDTH), f32)] + (side.out_shapes if side else []),
        scratch_shapes=_comm_scratch(ns) if side else [],
        compiler_params=_cparams(("arbitrary", "arbitrary")),
    )(proj, k16, k0_16, k1_16, v16, runs, do, *(side.arrs if side else []))


def _shift_down(x, prev8, j):
    if j == 0:
        return x
    r = pltpu.roll(x, j, axis=0)
    row8 = lax.broadcasted_iota(jnp.int32, prev8.shape, 0)
    head = jnp.where(row8 < j, pltpu.roll(prev8, j, axis=0), r[0:SUBLANES])
    return jnp.concatenate([head, r[SUBLANES:]], axis=0)


def _shift_up(x, next8, j):
    if j == 0:
        return x
    n = x.shape[0]
    r = pltpu.roll(x, n - j, axis=0)
    row8 = lax.broadcasted_iota(jnp.int32, next8.shape, 0)
    tail = jnp.where(row8 >= SUBLANES - j, pltpu.roll(next8, SUBLANES - j, axis=0), r[n - SUBLANES:n])
    return jnp.concatenate([r[:n - SUBLANES], tail], axis=0)


def _conv(x, prev8, w):
    k_taps = w.shape[0]
    out = x * w[k_taps - 1:k_taps, :]
    for j in range(1, k_taps):
        out = out + _shift_down(x, prev8, j) * w[k_taps - 1 - j:k_taps - j, :]
    return out


def _conv_tiles(rows):
    tr = min(TCONV_R, rows)
    return tr, rows // tr, tr // SUBLANES


def _prev_spec(tc, cb0, r8):
    return pl.BlockSpec((SUBLANES, tc), lambda j, i: (jnp.maximum(i * r8 - 1, 0), cb0 + j))


def _silu(x):
    return x * jax.nn.sigmoid(x)


def _dsilu(x):
    s = jax.nn.sigmoid(x)
    return s * (1.0 + x * (1.0 - s))


def _dn_conv_fwd(proj, w):
    rows = proj.shape[0]
    tr, nr, r8 = _conv_tiles(rows)
    tc = TCONV_C
    cb0 = OFF_DN // tc

    def body(x_ref, p_ref, w_ref, o_ref):
        prev = jnp.where(pl.program_id(1) == 0, 0.0, p_ref[...])
        o_ref[...] = _silu(_conv(x_ref[...], prev, w_ref[...]))

    return pl.pallas_call(
        body, name="dn_conv_fwd", grid=(DN_CONV_CH // tc, nr),
        in_specs=[pl.BlockSpec((tr, tc), lambda j, i: (i, cb0 + j)), _prev_spec(tc, cb0, r8),
                  pl.BlockSpec((DN_CONV_WIDTH, tc), lambda j, i: (0, j))],
        out_specs=pl.BlockSpec((tr, tc), lambda j, i: (i, j)),
        out_shape=jax.ShapeDtypeStruct((rows, DN_CONV_CH), f32),
        compiler_params=_cparams(("parallel", "parallel")),
    )(proj, proj, w)


def _dn_conv_bwd_act(proj, w, dact):
    rows = proj.shape[0]
    tr, nr, r8 = _conv_tiles(rows)
    tc = TCONV_C
    cb0 = OFF_DN // tc

    def body(x_ref, p_ref, w_ref, d_ref, o_ref):
        prev = jnp.where(pl.program_id(1) == 0, 0.0, p_ref[...])
        o_ref[...] = d_ref[...] * _dsilu(_conv(x_ref[...], prev, w_ref[...]))

    return pl.pallas_call(
        body, name="dn_conv_bwd_act", grid=(DN_CONV_CH // tc, nr),
        in_specs=[pl.BlockSpec((tr, tc), lambda j, i: (i, cb0 + j)), _prev_spec(tc, cb0, r8),
                  pl.BlockSpec((DN_CONV_WIDTH, tc), lambda j, i: (0, j)),
                  pl.BlockSpec((tr, tc), lambda j, i: (i, j))],
        out_specs=pl.BlockSpec((tr, tc), lambda j, i: (i, j)),
        out_shape=jax.ShapeDtypeStruct((rows, DN_CONV_CH), f32),
        compiler_params=_cparams(("parallel", "parallel")),
    )(proj, proj, w, dact)


def _ffn_conv_fwd(u_pre, w, b):
    rows = u_pre.shape[0]
    tr, nr, r8 = _conv_tiles(rows)
    tc = TCONV_FF
    nct = D_FF // tc

    def body(xg_ref, pg_ref, xu_ref, pu_ref, wg_ref, wu_ref, bg_ref, bu_ref, o_ref, u_ref):
        first = pl.program_id(1) == 0
        ug = _conv(xg_ref[...], jnp.where(first, 0.0, pg_ref[...]), wg_ref[...]) + bg_ref[...]
        uu = _conv(xu_ref[...], jnp.where(first, 0.0, pu_ref[...]), wu_ref[...]) + bu_ref[...]
        o_ref[...] = (_silu(ug) * uu).astype(o_ref.dtype)
        u_ref[0] = ug
        u_ref[1] = uu

    def x_spec(off):
        return pl.BlockSpec((tr, tc), lambda j, i: (i, off + j))

    def w_spec(k, off):
        return pl.BlockSpec((k, tc), lambda j, i: (0, off + j))

    return pl.pallas_call(
        body, name="ffn_conv_fwd", grid=(nct, nr),
        in_specs=[x_spec(0), _prev_spec(tc, 0, r8), x_spec(nct), _prev_spec(tc, nct, r8),
                  w_spec(FFN_CONV_WIDTH, 0), w_spec(FFN_CONV_WIDTH, nct), w_spec(1, 0), w_spec(1, nct)],
        out_specs=[pl.BlockSpec((tr, tc), lambda j, i: (i, j)), pl.BlockSpec((2, tr, tc), lambda j, i: (0, i, j))],
        out_shape=[jax.ShapeDtypeStruct((rows, D_FF), bf16), jax.ShapeDtypeStruct((2, rows, D_FF), f32)],
        compiler_params=_cparams(("parallel", "parallel")),
    )(u_pre, u_pre, u_pre, u_pre, w, w, b, b)


def _ffn_conv_bwd_act(u, dact):
    rows = dact.shape[0]
    tr, nr, _ = _conv_tiles(rows)
    tc = TCONV_FF
    nct = D_FF // tc

    def body(u_ref, d_ref, du_ref, dbg_ref, dbu_ref):
        first = pl.program_id(1) == 0
        ug = u_ref[0]
        uu = u_ref[1]
        d = d_ref[...]
        sig = jax.nn.sigmoid(ug)
        dug = d * uu * (sig * (1.0 + ug * (1.0 - sig)))
        duu = d * (ug * sig)
        du_ref[0] = dug
        du_ref[1] = duu

        @pl.when(first)
        def _():
            dbg_ref[...] = jnp.zeros_like(dbg_ref)
            dbu_ref[...] = jnp.zeros_like(dbu_ref)

        dbg_ref[...] += jnp.sum(dug, axis=0, keepdims=True)
        dbu_ref[...] += jnp.sum(duu, axis=0, keepdims=True)

    pair = pl.BlockSpec((2, tr, tc), lambda j, i: (0, i, j))
    vec = pl.BlockSpec((1, tc), lambda j, i: (0, j))
    return pl.pallas_call(
        body, name="ffn_conv_bwd_act", grid=(nct, nr),
        in_specs=[pair, pl.BlockSpec((tr, tc), lambda j, i: (i, j))],
        out_specs=[pair, vec, vec],
        out_shape=[jax.ShapeDtypeStruct((2, rows, D_FF), f32),
                   jax.ShapeDtypeStruct((1, D_FF), f32), jax.ShapeDtypeStruct((1, D_FF), f32)],
        compiler_params=_cparams(("parallel", "arbitrary")),
    )(u, dact)


def _conv_bwd(dy, x, x_cb0, w, name):
    k_taps = w.shape[0]
    split = dy.ndim == 3
    rows = dy.shape[-2]
    ch = dy.shape[-1] * (2 if split else 1)
    tc = TCONV_FF if split else TCONV_C
    tr, nr, r8 = _conv_tiles(rows)
    per_half = dy.shape[-1] // tc
    last8 = rows // SUBLANES - 1

    def body(dy_ref, nx_ref, x_ref, w_ref, dx_ref, dw_ref):
        i = pl.program_id(1)
        dyv = dy_ref[...]
        nxt = jnp.where(i == nr - 1, 0.0, nx_ref[...])
        xv = x_ref[...].astype(f32)
        wv = w_ref[...]

        @pl.when(i == 0)
        def _():
            dw_ref[...] = jnp.zeros_like(dw_ref)

        dx = dyv * wv[k_taps - 1:k_taps, :]
        dw_ref[k_taps - 1:k_taps, :] += jnp.sum(dyv * xv, axis=0, keepdims=True)
        for j in range(1, k_taps):
            dy_j = _shift_up(dyv, nxt, j)
            dx = dx + dy_j * wv[k_taps - 1 - j:k_taps - j, :]
            dw_ref[k_taps - 1 - j:k_taps - j, :] += jnp.sum(dy_j * xv, axis=0, keepdims=True)
        dx_ref[...] = dx.astype(dx_ref.dtype)

    tile = pl.BlockSpec((tr, tc), lambda j, i: (i, j))
    if split:
        dy_spec = pl.BlockSpec((None, tr, tc), lambda j, i: (j // per_half, i, j % per_half))
        next_spec = pl.BlockSpec((None, SUBLANES, tc),
                                 lambda j, i: (j // per_half, jnp.minimum((i + 1) * r8, last8), j % per_half))
    else:
        dy_spec = tile
        next_spec = pl.BlockSpec((SUBLANES, tc), lambda j, i: (jnp.minimum((i + 1) * r8, last8), j))
    return pl.pallas_call(
        body, name=name, grid=(ch // tc, nr),
        in_specs=[dy_spec, next_spec, pl.BlockSpec((tr, tc), lambda j, i: (i, x_cb0 + j)),
                  pl.BlockSpec((k_taps, tc), lambda j, i: (0, j))],
        out_specs=[tile, pl.BlockSpec((k_taps, tc), lambda j, i: (0, j))],
        out_shape=[jax.ShapeDtypeStruct((rows, ch), bf16), jax.ShapeDtypeStruct((k_taps, ch), f32)],
        compiler_params=_cparams(("parallel", "arbitrary")),
    )(dy, dy, x, w)


def _hdot(a, b):
    return jnp.dot(a, b, preferred_element_type=f32, precision=lax.Precision.HIGH)


def _xdot(a, b):
    return jnp.dot(a, b, preferred_element_type=f32, precision=lax.Precision.HIGHEST)


def _bdot(a, b):
    return jnp.dot(a.astype(bf16), b.astype(bf16), preferred_element_type=f32)


def _bdot_nt(a, b):
    return lax.dot_general(a.astype(bf16), b.astype(bf16), (((1,), (1,)), ((), ())), preferred_element_type=f32)


def _bdot_tn(a, b):
    return lax.dot_general(a.astype(bf16), b.astype(bf16), (((0,), (0,)), ((), ())), preferred_element_type=f32)


GDN_GROUP = 4
GDN_NGROUPS = DN_HEADS // GDN_GROUP
GDN_ROWS = GDN_GROUP * DN_CHUNK
GDN_QK_LANES = GDN_GROUP * DN_KEY_DIM
GDN_LOGIT_LANE = DN_HEADS


def _inverse_impl(lows):
    n = lows[0].shape[0]
    r = lax.broadcasted_iota(jnp.int32, (n, n), 0)
    c = lax.broadcasted_iota(jnp.int32, (n, n), 1)
    eye = (r == c).astype(f32)
    blk = jnp.right_shift(r, 3) == jnp.right_shift(c, 3)
    d = [jnp.where(blk, low, 0.0) for low in lows]
    e = [low - x for low, x in zip(lows, d)]

    def nilpotent8_inverse(xs):
        acc = [eye - x for x in xs]
        power = xs
        for _ in range(2):
            power = [_bdot(x, x) for x in power]
            acc = [_bdot(a, eye + x) for a, x in zip(acc, power)]
        return acc

    dinv = nilpotent8_inverse(d)
    ninv = nilpotent8_inverse([_bdot(x, y) for x, y in zip(dinv, e)])
    t = [_bdot(x, y) for x, y in zip(ninv, dinv)]
    for _ in range(2):
        res = [eye - x - _hdot(low, x) for low, x in zip(lows, t)]
        t = [x + _bdot(x, y) for x, y in zip(t, res)]
    return tuple(t)


@jax.custom_vjp
def _unit_lower_inverses(lows):
    return _inverse_impl(lows)


def _unit_lower_inverses_fwd(lows):
    t = _inverse_impl(lows)
    return t, t


def _unit_lower_inverses_bwd(t, ct):
    tn = (((0,), (0,)), ((), ()))
    nt = (((1,), (1,)), ((), ()))
    left = [lax.dot_general(x, g, tn, preferred_element_type=f32, precision=lax.Precision.HIGH) for x, g in zip(t, ct)]
    return (tuple(-lax.dot_general(x, y, nt, preferred_element_type=f32, precision=lax.Precision.HIGH)
                  for x, y in zip(left, t)),)


_unit_lower_inverses.defvjp(_unit_lower_inverses_fwd, _unit_lower_inverses_bwd)


@jax.custom_vjp
def _known_inverses(lows, t):
    return t


def _known_inverses_fwd(lows, t):
    return t, t


def _known_inverses_bwd(t, ct):
    return _unit_lower_inverses_bwd(t, ct) + (tuple(jnp.zeros_like(x) for x in t),)


_known_inverses.defvjp(_known_inverses_fwd, _known_inverses_bwd)


def _gdn_chunk(a_log, dt_bias, norm_w, ba, *per_group, inverses=None, keep_inverses=False):
    ng = GDN_NGROUPS
    qgs, kgs, vsts, zsts, states = [per_group[i * ng:(i + 1) * ng] for i in range(5)]
    groups = range(ng)
    n = GDN_ROWS
    r = lax.broadcasted_iota(jnp.int32, (n, n), 0)
    c = lax.broadcasted_iota(jnp.int32, (n, n), 1)
    same_head = jnp.right_shift(r, 6) == jnp.right_shift(c, 6)
    incl = jnp.logical_and(same_head, r >= c)
    strict = jnp.logical_and(same_head, r > c)
    eye = (r == c).astype(f32)
    ones = jnp.ones((n, n), f32)
    own_lanes = same_head.astype(f32)
    lane = lax.broadcasted_iota(jnp.int32, (1, LANES), 1)
    pick = lambda arr, idx: jnp.sum(jnp.where(lane == idx, arr, 0.0), axis=1, keepdims=True)
    heads = [[GDN_GROUP * g + h for h in range(GDN_GROUP)] for g in groups]
    rc = lax.broadcasted_iota(jnp.int32, (DN_CHUNK, DN_CHUNK), 0)
    cc = lax.broadcasted_iota(jnp.int32, (DN_CHUNK, DN_CHUNK), 1)

    g_all = -jnp.exp(a_log) * _softplus(ba + dt_bias)
    gc_all = _xdot((rc >= cc).astype(f32), g_all)
    gl_all = jnp.sum(g_all, axis=0, keepdims=True)
    beta = [jnp.concatenate([jax.nn.sigmoid(pick(ba, hd)) for hd in heads[g]], axis=0) for g in groups]
    gc = [jnp.concatenate([pick(gc_all, GDN_LOGIT_LANE + hd) for hd in heads[g]], axis=0) for g in groups]
    g_last = [jnp.concatenate([jnp.broadcast_to(pick(gl_all, GDN_LOGIT_LANE + hd), (DN_CHUNK, 1)) for hd in heads[g]],
                              axis=0) for g in groups]
    gr = [jnp.broadcast_to(gc[g], (n, n)).T for g in groups]
    decay = [jnp.where(incl, jnp.exp(jnp.where(incl, gc[g] - gr[g], 0.0)), 0.0) for g in groups]
    q = [jnp.concatenate([qgs[g]] * GDN_GROUP, axis=0) * own_lanes for g in groups]
    k = [jnp.concatenate([kgs[g]] * GDN_GROUP, axis=0) * own_lanes for g in groups]
    qn = [x * lax.rsqrt(jnp.sum(x * x, axis=1, keepdims=True) + L2_EPS) * (DN_KEY_DIM ** -0.5) for x in q]
    kn = [x * lax.rsqrt(jnp.sum(x * x, axis=1, keepdims=True) + L2_EPS) for x in k]
    kb = [kn[g] * beta[g] for g in groups]
    low = [jnp.where(strict, _bdot_nt(kb[g], kn[g]) * decay[g], 0.0) for g in groups]
    intra = [jnp.where(incl, _bdot_nt(qn[g], kn[g]) * decay[g], 0.0) for g in groups]
    t = _unit_lower_inverses(tuple(low)) if inverses is None else _known_inverses(tuple(low), tuple(inverses))
    u = [_bdot(t[g], vsts[g] * beta[g]) for g in groups]
    w = [_bdot(t[g], kb[g] * jnp.exp(gc[g])) for g in groups]
    sb = [s.astype(bf16) for s in states]
    v_new = [u[g] - jnp.dot(w[g].astype(bf16), sb[g], preferred_element_type=f32) for g in groups]
    o = [jnp.dot((qn[g] * jnp.exp(gc[g])).astype(bf16), sb[g], preferred_element_type=f32) for g in groups]
    o = [o[g] + _bdot(intra[g], v_new[g]) for g in groups]
    new_state = [states[g] * jnp.exp(g_last[g]) + _bdot_tn(kn[g] * jnp.exp(g_last[g] - gc[g]), v_new[g])
                 for g in groups]
    o_n = [x * lax.rsqrt(jnp.mean(x * x, axis=1, keepdims=True) + NORM_EPS) * norm_w for x in o]
    return tuple(o_n[g] * _silu(zsts[g]) for g in groups) + tuple(new_state) + (tuple(t) if keep_inverses else ())


GDN_FWD_CHUNKS = 2
GDN_BWD_CHUNKS = 1


def _gdn_specs(rows, reverse, nc):
    tr = nc * DN_CHUNK
    n = rows // tr
    idx = (lambda i: n - 1 - i) if reverse else (lambda i: i)
    vec = pl.BlockSpec((1, LANES), lambda i: (0, 0))
    qkv = pl.BlockSpec((tr, DN_CONV_CH), lambda i: (idx(i), 0))
    z = pl.BlockSpec((tr, DN_V_WIDTH), lambda i: (idx(i), OFF_Z // DN_V_WIDTH))
    ba = pl.BlockSpec((tr, LANES), lambda i: (idx(i), 0))
    wide = pl.BlockSpec((tr, DN_V_WIDTH), lambda i: (idx(i), 0))
    st = pl.BlockSpec((nc, DN_HEADS * DN_KEY_DIM, LANES), lambda i: (idx(i), 0, 0))
    inv = pl.BlockSpec((nc, GDN_NGROUPS * GDN_ROWS, GDN_ROWS), lambda i: (idx(i), 0, 0))
    return n, vec, qkv, z, ba, wide, st, inv


def _chunk_rows(c):
    return slice(c * DN_CHUNK, (c + 1) * DN_CHUNK)


def _gdn_slices(grp):
    q = slice(grp * GDN_QK_LANES, (grp + 1) * GDN_QK_LANES)
    k = slice(DN_QK_WIDTH + grp * GDN_QK_LANES, DN_QK_WIDTH + (grp + 1) * GDN_QK_LANES)
    heads = [slice((GDN_GROUP * grp + h) * LANES, (GDN_GROUP * grp + h + 1) * LANES) for h in range(GDN_GROUP)]
    vs = [slice(2 * DN_QK_WIDTH + s.start, 2 * DN_QK_WIDTH + s.stop) for s in heads]
    return q, k, vs, heads


def _stack_cols(ref, rows, cols):
    return jnp.concatenate([ref[rows, s] for s in cols], axis=0)


def _gdn_operands(qkv_ref, z_ref, rows, state_rows):
    sl = [_gdn_slices(grp) for grp in range(GDN_NGROUPS)]
    return ([qkv_ref[rows, q] for q, _, _, _ in sl] + [qkv_ref[rows, k] for _, k, _, _ in sl]
            + [_stack_cols(qkv_ref, rows, vs) for _, _, vs, _ in sl]
            + [_stack_cols(z_ref, rows, heads) for _, _, _, heads in sl]
            + [state_rows[grp * GDN_ROWS:(grp + 1) * GDN_ROWS, :] for grp in range(GDN_NGROUPS)])


def _gdn_fwd(a_log, dt_bias, norm_w, qkv_act, proj, ba):
    rows = qkv_act.shape[0]
    n, vec, qkv_s, z_s, ba_s, wide, st_s, inv_s = _gdn_specs(rows, False, GDN_FWD_CHUNKS)

    def body(al_ref, dt_ref, nw_ref, qkv_ref, z_ref, ba_ref, o_ref, st_ref, inv_ref, state):
        @pl.when(pl.program_id(0) == 0)
        def _():
            state[...] = jnp.zeros_like(state)

        for c in range(GDN_FWD_CHUNKS):
            tok = _chunk_rows(c)
            st_ref[c] = state[...]
            out = _gdn_chunk(al_ref[...], dt_ref[...], nw_ref[...], ba_ref[tok, :],
                             *_gdn_operands(qkv_ref, z_ref, tok, state), keep_inverses=True)
            for grp in range(GDN_NGROUPS):
                _, _, _, heads = _gdn_slices(grp)
                rs = slice(grp * GDN_ROWS, (grp + 1) * GDN_ROWS)
                for h, s in enumerate(heads):
                    o_ref[tok, s] = out[grp][h * DN_CHUNK:(h + 1) * DN_CHUNK].astype(o_ref.dtype)
                state[rs, :] = out[GDN_NGROUPS + grp]
                inv_ref[c, rs, :] = out[2 * GDN_NGROUPS + grp]

    n_chunks = rows // DN_CHUNK
    return pl.pallas_call(
        body, name="gdn_fwd", grid=(n,),
        in_specs=[vec, vec, vec, qkv_s, z_s, ba_s], out_specs=[wide, st_s, inv_s],
        out_shape=[jax.ShapeDtypeStruct((rows, DN_V_WIDTH), bf16),
                   jax.ShapeDtypeStruct((n_chunks, DN_HEADS * DN_KEY_DIM, LANES), f32),
                   jax.ShapeDtypeStruct((n_chunks, GDN_NGROUPS * GDN_ROWS, GDN_ROWS), f32)],
        scratch_shapes=[pltpu.VMEM((DN_HEADS * DN_KEY_DIM, LANES), f32)],
        compiler_params=_cparams(("arbitrary",)),
    )(a_log, dt_bias, norm_w, qkv_act, proj, ba)


def _gdn_bwd(a_log, dt_bias, norm_w, qkv_act, proj, ba, states, inverses, do):
    rows = qkv_act.shape[0]
    n, vec, qkv_s, z_s, ba_s, wide, st_s, inv_s = _gdn_specs(rows, True, GDN_BWD_CHUNKS)

    def body(al_ref, dt_ref, nw_ref, qkv_ref, z_ref, ba_ref, st_ref, inv_ref, do_ref,
             dal_ref, ddt_ref, dnw_ref, dqkv_ref, dz_ref, dba_ref, dstate):
        @pl.when(pl.program_id(0) == 0)
        def _():
            dstate[...] = jnp.zeros_like(dstate)
            dal_ref[...] = jnp.zeros_like(dal_ref)
            ddt_ref[...] = jnp.zeros_like(ddt_ref)
            dnw_ref[...] = jnp.zeros_like(dnw_ref)

        ng = GDN_NGROUPS
        for c in reversed(range(GDN_BWD_CHUNKS)):
            tok = _chunk_rows(c)
            kept = [inv_ref[c, grp * GDN_ROWS:(grp + 1) * GDN_ROWS, :] for grp in range(ng)]
            _, vjp = jax.vjp(functools.partial(_gdn_chunk, inverses=kept), al_ref[...], dt_ref[...], nw_ref[...],
                             ba_ref[tok, :], *_gdn_operands(qkv_ref, z_ref, tok, st_ref[c]))
            cts = tuple(_stack_cols(do_ref, tok, _gdn_slices(grp)[3]) for grp in range(ng))
            cts += tuple(dstate[grp * GDN_ROWS:(grp + 1) * GDN_ROWS, :] for grp in range(ng))
            grads = vjp(cts)
            dal_ref[...] += grads[0]
            ddt_ref[...] += grads[1]
            dnw_ref[...] += grads[2]
            dba_ref[tok, :] = grads[3]
            dqs, dks, dvs, dzs, dss = [grads[4 + i * ng:4 + (i + 1) * ng] for i in range(5)]
            for grp in range(ng):
                q, k, vs, heads = _gdn_slices(grp)
                dqkv_ref[tok, q] = dqs[grp]
                dqkv_ref[tok, k] = dks[grp]
                for h, (sv, sh) in enumerate(zip(vs, heads)):
                    rows_h = slice(h * DN_CHUNK, (h + 1) * DN_CHUNK)
                    dqkv_ref[tok, sv] = dvs[grp][rows_h]
                    dz_ref[tok, sh] = dzs[grp][rows_h].astype(dz_ref.dtype)
                dstate[grp * GDN_ROWS:(grp + 1) * GDN_ROWS, :] = dss[grp]

    return pl.pallas_call(
        body, name="gdn_bwd", grid=(n,),
        in_specs=[vec, vec, vec, qkv_s, z_s, ba_s, st_s, inv_s, wide],
        out_specs=[vec, vec, vec, qkv_s, wide, ba_s],
        out_shape=[jax.ShapeDtypeStruct((1, LANES), f32)] * 3
        + [jax.ShapeDtypeStruct((rows, DN_CONV_CH), f32), jax.ShapeDtypeStruct((rows, DN_V_WIDTH), bf16),
           jax.ShapeDtypeStruct((rows, LANES), f32)],
        scratch_shapes=[pltpu.VMEM((DN_HEADS * DN_KEY_DIM, LANES), f32)],
        compiler_params=_cparams(("arbitrary",)),
    )(a_log, dt_bias, norm_w, qkv_act, proj, ba, states, inverses, do)


def _ada_fwd(c_all, w_loc, b_loc):
    def body(c_ref, w_ref, b_ref, o_ref):
        o_ref[...] = _bdot(_silu(c_ref[...]), w_ref[...]) + b_ref[...]

    return pl.pallas_call(body, name="ada_fwd", out_shape=jax.ShapeDtypeStruct((c_all.shape[0], w_loc.shape[1]), f32),
                          compiler_params=_cparams())(c_all, w_loc, b_loc)


def _ada_bwd(c_all, dmod_cols):
    def body(c_ref, d_ref, o_ref):
        o_ref[...] = _bdot_tn(_silu(c_ref[...]), d_ref[...])

    return pl.pallas_call(body, name="ada_bwd",
                          out_shape=jax.ShapeDtypeStruct((c_all.shape[1], dmod_cols.shape[1]), f32),
                          compiler_params=_cparams())(c_all, dmod_cols)


def _sum_devices(parts):
    def body(p_ref, o_ref):
        acc = p_ref[0:1, :]
        for d in range(1, N_DEV):
            acc = acc + p_ref[d:d + 1, :]
        o_ref[...] = acc

    return pl.pallas_call(body, name="sum_small", out_shape=jax.ShapeDtypeStruct((1, parts.shape[1]), f32),
                          compiler_params=_cparams())(parts)


def _adam_math(w, g, m, v):
    m2 = ADAM_B1 * m + (1.0 - ADAM_B1) * g
    v2 = ADAM_B2 * v + (1.0 - ADAM_B2) * jnp.square(g)
    m_hat = m2 / (1.0 - ADAM_B1 ** ADAM_STEP)
    v_hat = v2 / (1.0 - ADAM_B2 ** ADAM_STEP)
    delta = -ADAM_LR * (m_hat / (jnp.sqrt(v_hat) + ADAM_EPS) + ADAM_WD * w)
    return delta, m2, v2


def _row_tile(rows):
    return _pick(rows, (256, 128, 64, 32, 16, 8))


def _adamw(w, g, m, v, name):
    rows, cols = w.shape
    tr = _row_tile(rows)

    def body(w_ref, g_ref, m_ref, v_ref, d_ref, m2_ref, v2_ref):
        d_ref[...], m2_ref[...], v2_ref[...] = _adam_math(w_ref[...], g_ref[...], m_ref[...], v_ref[...])

    tile = pl.BlockSpec((tr, cols), lambda i: (i, 0))
    return pl.pallas_call(body, name=name, grid=(rows // tr,), in_specs=[tile] * 4, out_specs=[tile] * 3,
                          out_shape=[jax.ShapeDtypeStruct(w.shape, f32)] * 3,
                          compiler_params=_cparams(("parallel",)))(w, g, m, v)


def _sum_adamw(parts, w, m, v, name):
    rows, cols = w.shape
    tr = _row_tile(rows)

    def body(p_ref, w_ref, m_ref, v_ref, g_ref, d_ref, m2_ref, v2_ref):
        g = p_ref[0].astype(f32)
        for d in range(1, N_DEV):
            g = g + p_ref[d].astype(f32)
        g_ref[...] = g
        d_ref[...], m2_ref[...], v2_ref[...] = _adam_math(w_ref[...], g, m_ref[...], v_ref[...])

    tile = pl.BlockSpec((tr, cols), lambda i: (i, 0))
    return pl.pallas_call(body, name=name, grid=(rows // tr,),
                          in_specs=[pl.BlockSpec((N_DEV, tr, cols), lambda i: (0, i, 0)), tile, tile, tile],
                          out_specs=[tile] * 4, out_shape=[jax.ShapeDtypeStruct(w.shape, f32)] * 4,
                          compiler_params=_cparams(("parallel",)))(parts, w, m, v)


def _pad_lanes(a, width):
    return jnp.pad(a, ((0, 0), (0, width - a.shape[1])))


def _cols_by_device(full):
    r, c = full.shape
    return jnp.moveaxis(full.reshape(r, N_DEV, c // N_DEV), 1, 0)


def _cols_from_devices(parts):
    d, r, n = parts.shape
    return jnp.moveaxis(parts, 0, 1).reshape(r, d * n)


def kernel(x, c, w_ada, b_ada, norm1_w, w_in, dn_conv_w, dn_A_log, dn_dt_bias, dn_norm_w, w_proj_sb, w_proj_dn, w_out, norm2_w, w_ffn_in, ffn_conv_w, ffn_conv_b, w_ffn_out, final_norm_w, loss_target, m_w_ada, m_b_ada, m_norm1_w, m_w_in, m_dn_conv_w, m_dn_A_log, m_dn_dt_bias, m_dn_norm_w, m_w_proj_sb, m_w_proj_dn, m_w_out, m_norm2_w, m_w_ffn_in, m_ffn_conv_w, m_ffn_conv_b, m_w_ffn_out, m_final_norm_w, v_w_ada, v_b_ada, v_norm1_w, v_w_in, v_dn_conv_w, v_dn_A_log, v_dn_dt_bias, v_dn_norm_w, v_w_proj_sb, v_w_proj_dn, v_w_out, v_norm2_w, v_w_ffn_in, v_ffn_conv_w, v_ffn_conv_b, v_w_ffn_out, v_final_norm_w):
    d = D_MODEL
    me = 4 * lax.axis_index("x") + 2 * lax.axis_index("y") + lax.axis_index("c")
    xs = x[0]
    target = loss_target[0]
    n_ada = w_ada.shape[2]
    n_dnc = dn_conv_w.shape[2]
    n_ffc = ffn_conv_w.shape[2]

    small = jnp.concatenate([c, dn_conv_w[0].reshape(1, -1), ffn_conv_w[0].reshape(1, -1)], axis=1)
    small = _pad_lanes(small, -(-small.shape[1] // LANES) * LANES)
    small_g, w_in_g = _all_gather([small, w_in[0].astype(bf16)], "gather_w_in")
    later = [w_proj_sb[0].astype(bf16), w_proj_dn[0].astype(bf16), w_out[0].astype(bf16),
             w_ffn_in[0].astype(bf16), w_ffn_out[0].astype(bf16)]
    gather_later = _SideComm(_gather_protocol, later, _gathered_shapes(later))
    small_g = small_g[:, 0, :]
    c_all = small_g[:, :d]
    dn_cw = _cols_from_devices(small_g[:, d:d + DN_CONV_WIDTH * n_dnc].reshape(N_DEV, DN_CONV_WIDTH, n_dnc))
    o2 = d + DN_CONV_WIDTH * n_dnc
    ffn_cw = _cols_from_devices(small_g[:, o2:o2 + FFN_CONV_WIDTH * n_ffc].reshape(N_DEV, FFN_CONV_WIDTH, n_ffc))

    w_in_full = _cols_from_devices(w_in_g)
    r_sb, r_dn, r_z = 3 * SB_WIDTH, 3 * SB_WIDTH + DN_CONV_CH, 3 * SB_WIDTH + DN_CONV_CH + DN_V_WIDTH
    r_g = r_z + 2 * DN_HEADS
    w_main = jnp.concatenate([w_in_full[:, r_g:], w_in_full[:, r_sb:r_dn], w_in_full[:, r_dn:r_z],
                              w_in_full[:, :r_sb]], axis=1)
    w_ba = _pad_lanes(w_in_full[:, r_z:r_g], LANES)

    b_loc = lax.dynamic_slice(b_ada, (0, me * n_ada), (1, n_ada))
    mod_part = _ada_fwd(c_all, w_ada[0], b_loc)
    (mod_g,) = _all_gather([mod_part], "gather_mod")
    mod = lax.dynamic_index_in_dim(mod_g, me, axis=1, keepdims=False).reshape(1, N_DEV * n_ada)
    shift1, scale1, gate1, shift2, scale2, gate2 = [mod[:, i * d:(i + 1) * d] for i in range(6)]

    logit_lanes = ((0, 0), (GDN_LOGIT_LANE, LANES - GDN_LOGIT_LANE - DN_HEADS))
    a_log = jnp.pad(dn_A_log, logit_lanes)
    dt_b = jnp.pad(dn_dt_bias, logit_lanes)

    (h,) = _stage_fwd(_f_normmod, [norm1_w, shift1, scale1], [xs], [bf16], "norm1_fwd")
    proj = _mm(h, w_main, name="in_proj")
    ba = _mm(h, w_ba, name="in_proj_ba")
    k16, k0_16, k1_16, v16, v0_16, v1_16 = _sb_prepare(proj)
    o_a, sb_runs, w_psb_g, w_pdn_g, w_out_g, w_fin_g, w_fout_g = _sb_attention_fwd2(
        proj, k16, v0_16, v1_16, side=gather_later)
    w_psb = _cols_from_devices(w_psb_g)
    w_pdn = w_pdn_g.reshape(DN_V_WIDTH, d)
    w_o = w_out_g.reshape(d, d)
    w_fin = _cols_from_devices(w_fin_g)
    w_fout = w_fout_g.reshape(D_FF, d)
    qkv_act = _dn_conv_fwd(proj, dn_cw)
    o_b, states, dn_inverses = _gdn_fwd(a_log, dt_b, dn_norm_w, qkv_act, proj, ba)
    pa = _mm(o_a, w_psb, name="proj_sb")
    pb = _mm(o_b, w_pdn, name="proj_dn")
    gates = [(proj, d, OFF_GA // d), (proj, d, OFF_GB // d)]
    (merged,) = _stage_fwd(_f_merge, [], gates + [pa, pb], [bf16], "merge_fwd")
    ao = _mm(merged, w_o, name="out_proj")
    mid_params = [gate1, norm2_w, shift2, scale2]
    x1, h2 = _stage_fwd(_f_residual_normmod, mid_params, [xs, ao], [f32, bf16], "resid1_norm2_fwd")
    u_pre = _mm(h2, w_fin, name="ffn_in")
    act, u_conv = _ffn_conv_fwd(u_pre, ffn_cw, ffn_conv_b)
    fo = _mm(act, w_fout, name="ffn_out")

    loss_p, d_gate2, d_wf, dx2, dfo = _loss_and_grads(gate2, final_norm_w.reshape(1, d), x1, fo, target)
    dact = _mm(dfo, w_fout, tb=True, name="ffn_out_dx")
    g_w_fout = _mm(act, dfo, ta=True, name="ffn_out_dw")
    du, dbg, dbu = _ffn_conv_bwd_act(u_conv, dact)
    du_pre, d_ffn_cw = _conv_bwd(du, u_pre, 0, ffn_cw, "ffn_conv_bwd")
    dh2 = _mm(du_pre, w_fin, tb=True, name="ffn_in_dx")
    g_w_fin = _mm(h2, du_pre, ta=True, name="ffn_in_dw")
    (d_gate1, d_n2w, d_shift2, d_scale2), (dx1, dao) = _stage_bwd(
        _f_residual_normmod, mid_params, [xs, ao], [dx2, dh2], [f32, bf16], "resid1_norm2_bwd")
    dmerged = _mm(dao, w_o, tb=True, name="out_proj_dx")
    g_w_o = _mm(merged, dao, ta=True, name="out_proj_dw")
    _, (dga, dgb, dpa, dpb) = _stage_bwd(_f_merge, [], gates + [pa, pb], [dmerged], [bf16] * 4, "merge_bwd")
    do_a = _mm(dpa, w_psb, tb=True, name="proj_sb_dx")
    g_w_psb = _mm(o_a, dpa, ta=True, name="proj_sb_dw")
    do_b = _mm(dpb, w_pdn, tb=True, name="proj_dn_dx")
    g_w_pdn = _mm(o_b, dpb, ta=True, name="proj_dn_dw")
    early = [_cols_by_device(g_w_psb).astype(bf16),
             g_w_pdn.reshape(N_DEV, DN_V_WIDTH // N_DEV, d).astype(bf16),
             g_w_o.reshape(N_DEV, d // N_DEV, d).astype(bf16),
             _cols_by_device(g_w_fin).astype(bf16),
             g_w_fout.reshape(N_DEV, D_FF // N_DEV, d).astype(bf16)]
    exchange_early = _SideComm(_exchange_protocol, early, [jax.ShapeDtypeStruct(a.shape, a.dtype) for a in early])
    dq, dk, dv, *recv_early = _sb_attention_bwd2(proj, k16, k0_16, k1_16, v16, sb_runs, do_a, side=exchange_early)
    d_alog, d_dtb, d_dnw, dqkv_act, dz, dba = _gdn_bwd(a_log, dt_b, dn_norm_w, qkv_act, proj, ba, states,
                                                       dn_inverses, do_b)
    d_conv_out = _dn_conv_bwd_act(proj, dn_cw, dqkv_act)
    d_dn_pre, d_dn_cw = _conv_bwd(d_conv_out, proj, OFF_DN // TCONV_C, dn_cw, "dn_conv_bwd")
    dproj = jnp.concatenate([dga, dgb, d_dn_pre, dz, dq, dk.astype(bf16), dv.astype(bf16)], axis=1)
    g_w_main = _mm(h, dproj, ta=True, name="in_proj_dw")
    g_w_ba = _mm(h, dba, ta=True, name="in_proj_ba_dw")
    g_w_in_full = jnp.concatenate([g_w_main[:, OFF_SBQ:], g_w_main[:, OFF_DN:OFF_Z], g_w_main[:, OFF_Z:OFF_SBQ],
                                   g_w_ba[:, :2 * DN_HEADS], g_w_main[:, :OFF_DN]], axis=1)
    w_in_parts = _cols_by_device(g_w_in_full).astype(bf16)
    exchange_w_in = _SideComm(_exchange_protocol, [w_in_parts], [jax.ShapeDtypeStruct(w_in_parts.shape, bf16)])
    dh, recv_w_in = _mm(dproj, w_main, tb=True, name="in_proj_dx", side=exchange_w_in)
    dh_ba = _mm(dba, w_ba, tb=True, name="in_proj_ba_dx")
    (d_n1w, d_shift1, d_scale1), (grad_x,) = _stage_bwd(
        _f_normmod, [norm1_w, shift1, scale1], [xs], [[dh, dh_ba]], [f32], "norm1_bwd", residual=(0, dx1))

    dmod = jnp.concatenate([d_shift1, d_scale1, d_gate1, d_shift2, d_scale2, d_gate2], axis=1)
    d_ffn_cb = jnp.concatenate([dbg, dbu], axis=1)
    small_parts = jnp.concatenate(
        [loss_p, dmod, d_n1w, d_alog, d_dtb, d_dnw, d_n2w, d_ffn_cb, d_wf,
         d_dn_cw.reshape(1, -1), d_ffn_cw.reshape(1, -1)], axis=1)
    (small_parts_g,) = _all_gather([small_parts], "gather_small_grads")
    tot = _sum_devices(small_parts_g[:, 0, :])
    offs = {}
    pos = 0
    for nm, width in (("loss", LANES), ("b_ada", 6 * d), ("norm1_w", d), ("dn_A_log", LANES), ("dn_dt_bias", LANES),
                      ("dn_norm_w", LANES), ("norm2_w", d), ("ffn_conv_b", 2 * D_FF), ("final_norm_w", d),
                      ("dn_conv_w", DN_CONV_WIDTH * DN_CONV_CH), ("ffn_conv_w", FFN_CONV_WIDTH * 2 * D_FF)):
        offs[nm] = (pos, width)
        pos += width
    seg = lambda nm: tot[:, offs[nm][0]:offs[nm][0] + offs[nm][1]]
    loss = tot[0, 0]
    g_b_ada = seg("b_ada")
    g_norm1 = seg("norm1_w")
    g_alog = seg("dn_A_log")[:, GDN_LOGIT_LANE:GDN_LOGIT_LANE + DN_HEADS]
    g_dtb = seg("dn_dt_bias")[:, GDN_LOGIT_LANE:GDN_LOGIT_LANE + DN_HEADS]
    g_dnw = seg("dn_norm_w")
    g_norm2 = seg("norm2_w")
    g_ffn_cb = seg("ffn_conv_b")
    g_fnw = seg("final_norm_w")
    g_dn_cw = lax.dynamic_slice(seg("dn_conv_w").reshape(DN_CONV_WIDTH, DN_CONV_CH), (0, me * n_dnc),
                                (DN_CONV_WIDTH, n_dnc))
    g_ffn_cw = lax.dynamic_slice(seg("ffn_conv_w").reshape(FFN_CONV_WIDTH, 2 * D_FF), (0, me * n_ffc),
                                 (FFN_CONV_WIDTH, n_ffc))

    dmod_all = small_parts_g[:, 0, offs["b_ada"][0]:offs["b_ada"][0] + 6 * d]
    g_w_ada = _ada_bwd(c_all, lax.dynamic_slice(dmod_all, (0, me * n_ada), (N_DEV, n_ada)))

    def pack(parts):
        flat = [p.reshape(1, -1) for p in parts]
        flat = [_pad_lanes(p, -(-p.shape[1] // LANES) * LANES) for p in flat]
        return jnp.concatenate(flat, axis=1), [p.shape[1] for p in flat]

    small_names_g = [g_b_ada, g_norm1, g_alog, g_dtb, g_dnw, g_norm2, g_ffn_cb, g_fnw, g_dn_cw, g_ffn_cw]
    small_w = [b_ada, norm1_w, dn_A_log, dn_dt_bias, dn_norm_w, norm2_w, ffn_conv_b, final_norm_w, dn_conv_w[0], ffn_conv_w[0]]
    small_m = [m_b_ada, m_norm1_w, m_dn_A_log, m_dn_dt_bias, m_dn_norm_w, m_norm2_w, m_ffn_conv_b, m_final_norm_w, m_dn_conv_w[0], m_ffn_conv_w[0]]
    small_v = [v_b_ada, v_norm1_w, v_dn_A_log, v_dn_dt_bias, v_dn_norm_w, v_norm2_w, v_ffn_conv_b, v_final_norm_w, v_dn_conv_w[0], v_ffn_conv_w[0]]
    pg, widths = pack(small_names_g)
    pw, _ = pack(small_w)
    pm, _ = pack(small_m)
    pv, _ = pack(small_v)
    s_delta, s_m, s_v = _adamw(pw, pg, pm, pv, "adamw_small")

    def unpack(flat):
        out, pos = [], 0
        for ref_arr, width in zip(small_w, widths):
            out.append(flat[:, pos:pos + ref_arr.size].reshape(ref_arr.shape))
            pos += width
        return out

    small_grads = [g.reshape(w_.shape) for g, w_ in zip(small_names_g, small_w)]
    small_delta, small_newm, small_newv = unpack(s_delta), unpack(s_m), unpack(s_v)

    ada_delta, ada_m, ada_v = _adamw(w_ada[0], g_w_ada, m_w_ada[0], v_w_ada[0], "adamw_ada")

    recv = [recv_w_in] + list(recv_early)
    big = {}
    for nm, parts, w_, m_, v_ in (("w_in", recv[0], w_in, m_w_in, v_w_in),
                                  ("w_proj_sb", recv[1], w_proj_sb, m_w_proj_sb, v_w_proj_sb),
                                  ("w_proj_dn", recv[2], w_proj_dn, m_w_proj_dn, v_w_proj_dn),
                                  ("w_out", recv[3], w_out, m_w_out, v_w_out),
                                  ("w_ffn_in", recv[4], w_ffn_in, m_w_ffn_in, v_w_ffn_in),
                                  ("w_ffn_out", recv[5], w_ffn_out, m_w_ffn_out, v_w_ffn_out)):
        big[nm] = [t[None] for t in _sum_adamw(parts, w_[0], m_[0], v_[0], "adamw_" + nm)]

    sg = dict(zip(["b_ada", "norm1_w", "dn_A_log", "dn_dt_bias", "dn_norm_w", "norm2_w", "ffn_conv_b", "final_norm_w",
                   "dn_conv_w", "ffn_conv_w"], range(10)))

    def small_out(table, nm):
        val = table[sg[nm]]
        return val[None] if nm in ("dn_conv_w", "ffn_conv_w") else val

    order = ["w_ada", "b_ada", "norm1_w", "w_in", "dn_conv_w", "dn_A_log", "dn_dt_bias", "dn_norm_w", "w_proj_sb",
             "w_proj_dn", "w_out", "norm2_w", "w_ffn_in", "ffn_conv_w", "ffn_conv_b", "w_ffn_out", "final_norm_w"]
    groups = []
    for k, small_table in enumerate((small_grads, small_delta, small_newm, small_newv)):
        row = []
        for nm in order:
            if nm == "w_ada":
                row.append((g_w_ada, ada_delta, ada_m, ada_v)[k][None])
            elif nm in big:
                row.append(big[nm][k])
            else:
                row.append(small_out(small_table, nm))
        groups.append(row)
    return (loss, grad_x[None], *groups[0], *groups[1], *groups[2], *groups[3])
```

```python
import functools

import jax
import jax.numpy as jnp
from jax import lax
from jax.experimental import pallas as pl
from jax.experimental.pallas import tpu as pltpu

f32 = jnp.float32
bf16 = jnp.bfloat16

D_MODEL = 1024
SB_HEADS = 8
SB_HEAD_DIM = 64
SB_WIDTH = SB_HEADS * SB_HEAD_DIM
SB_QBLOCK = 128
DN_HEADS = 8
DN_KEY_DIM = 64
DN_VAL_DIM = 128
DN_QK_WIDTH = DN_HEADS * DN_KEY_DIM
DN_V_WIDTH = DN_HEADS * DN_VAL_DIM
DN_CONV_CH = 2 * DN_QK_WIDTH + DN_V_WIDTH
DN_CONV_WIDTH = 4
DN_CHUNK = 64
D_FF = 2816
FFN_CONV_WIDTH = 3
NORM_EPS = 1e-6
L2_EPS = 1e-6
ADAM_LR = 0.001
ADAM_B1 = 0.9
ADAM_B2 = 0.999
ADAM_EPS = 1e-08
ADAM_WD = 0.01
ADAM_STEP = 10

N_DEV = 8
MESH = pl.DeviceIdType.MESH

LANES = 128
SUBLANES = 8
VMEM_LIMIT = 48 * 1024 * 1024

OFF_GA = 0
OFF_GB = D_MODEL
OFF_DN = 2 * D_MODEL
OFF_Z = OFF_DN + DN_CONV_CH
OFF_SBQ = OFF_Z + DN_V_WIDTH
OFF_SBK = OFF_SBQ + SB_WIDTH
OFF_SBV = OFF_SBK + SB_WIDTH
MAIN_WIDTH = OFF_SBV + SB_WIDTH

TM = 512
TCONV_R = 512
TCONV_C = 1024
TCONV_FF = D_FF // 2
SB_PAIRS_FWD = 4
SB_PAIRS_BWD = 4
SB_DEAD = -106.0
SB_NEVER = -1e30


def _cparams(sem=None):
    return pltpu.CompilerParams(dimension_semantics=sem, vmem_limit_bytes=VMEM_LIMIT)


def _pick(n, cands):
    for c in cands:
        if n % c == 0:
            return c
    return n


def _my_pos():
    return lax.axis_index("x"), lax.axis_index("y"), lax.axis_index("c")


def _flip(v, bit):
    return 1 - v if bit else v


def _comm_scratch(n):
    return [pltpu.SemaphoreType.DMA((n, 7)), pltpu.SemaphoreType.DMA((n, 7)), pltpu.SemaphoreType.DMA((n,))]


def _gather_protocol(ins, outs, send_sems, recv_sems, local_sems):
    n = len(ins)
    x, y, c = _my_pos()
    me, sibling = (x, y, c), (x, y, 1 - c)
    chips = [(1 - x, y), (x, 1 - y), (1 - x, 1 - y)]

    def slot(out, pos):
        return out.at[4 * pos[0] + 2 * pos[1] + pos[2]]

    def copy(a, k, block, to, src=None):
        return pltpu.make_async_remote_copy(
            src_ref=slot(outs[a], block) if src is None else src, dst_ref=slot(outs[a], block),
            send_sem=send_sems.at[a, k], recv_sem=recv_sems.at[a, k], device_id=to, device_id_type=MESH)

    def local(a):
        return pltpu.make_async_copy(ins[a], slot(outs[a], me), local_sems.at[a])

    def first(a):
        return [copy(a, 0, me, sibling, src=ins[a])] + [copy(a, 1 + j, me, (*chip, c), src=ins[a])
                                                         for j, chip in enumerate(chips)]

    def start():
        for a in range(n):
            local(a).start()
            for cp in first(a):
                cp.start()

    def finish():
        forwards = []
        for a in range(n):
            for j, chip in enumerate(chips):
                copy(a, 1 + j, (*chip, c), me).wait_recv()
                fwd = copy(a, 4 + j, (*chip, c), sibling)
                fwd.start()
                forwards.append(fwd)
        for a in range(n):
            copy(a, 0, sibling, me).wait_recv()
            for j, chip in enumerate(chips):
                copy(a, 4 + j, (*chip, 1 - c), me).wait_recv()
        for a in range(n):
            for cp in first(a):
                cp.wait_send()
        for cp in forwards:
            cp.wait_send()
        for a in range(n):
            local(a).wait()

    return start, finish


def _exchange_protocol(ins, outs, send_sems, recv_sems, local_sems):
    n = len(ins)
    x, y, c = _my_pos()
    me_idx = 4 * x + 2 * y + c

    def local(a):
        return pltpu.make_async_copy(ins[a].at[me_idx], outs[a].at[me_idx], local_sems.at[a])

    def copies(a, m):
        peer = (_flip(x, m & 4), _flip(y, m & 2), _flip(c, m & 1))
        peer_idx = 4 * peer[0] + 2 * peer[1] + peer[2]
        sems = dict(send_sem=send_sems.at[a, m - 1], recv_sem=recv_sems.at[a, m - 1], device_id=peer,
                    device_id_type=MESH)
        send = pltpu.make_async_remote_copy(src_ref=ins[a].at[peer_idx], dst_ref=outs[a].at[me_idx], **sems)
        recv = pltpu.make_async_remote_copy(src_ref=ins[a].at[peer_idx], dst_ref=outs[a].at[peer_idx], **sems)
        return send, recv

    def start():
        for a in range(n):
            local(a).start()
            for m in range(1, N_DEV):
                copies(a, m)[0].start()

    def finish():
        for a in range(n):
            for m in range(1, N_DEV):
                copies(a, m)[1].wait_recv()
        for a in range(n):
            for m in range(1, N_DEV):
                copies(a, m)[0].wait_send()
            local(a).wait()

    return start, finish


def _collective_call(protocol, arrs, out_shapes, name):
    n = len(arrs)

    def body(*refs):
        start, finish = protocol(refs[:n], refs[n:2 * n], *refs[2 * n:])
        start()
        finish()

    any_spec = pl.BlockSpec(memory_space=pl.ANY)
    return pl.pallas_call(body, name=name, out_shape=out_shapes, in_specs=[any_spec] * n, out_specs=[any_spec] * n,
                          scratch_shapes=_comm_scratch(n))(*arrs)


def _gathered_shapes(arrs):
    return [jax.ShapeDtypeStruct((N_DEV,) + a.shape, a.dtype) for a in arrs]


def _all_gather(arrs, name):
    return _collective_call(_gather_protocol, arrs, _gathered_shapes(arrs), name)


MM_BLOCK_BYTES = 7 * 1024 * 1024
MM_TILE_CAP = 1664


def _lane_tile(n, cap):
    fits = [t for t in range(LANES, min(n, cap) + 1, LANES) if n % t == 0]
    return max(fits) if fits else n


def _mm_tiles(m_dim, n_dim, k_dim, a_bytes, b_bytes):
    tm = _lane_tile(m_dim, MM_TILE_CAP)
    tn = _lane_tile(n_dim, MM_TILE_CAP)
    while tm * tn * 4 > MM_BLOCK_BYTES:
        if tn >= tm and (tn // 2) % LANES == 0:
            tn //= 2
        else:
            tm //= 2
    if (m_dim % (2 * tm) == 0 and 2 * tm * k_dim * a_bytes <= MM_BLOCK_BYTES
            and 2 * tm * tn * 4 <= MM_BLOCK_BYTES):
        tm *= 2
    tk = k_dim
    if k_dim % LANES == 0:
        units = k_dim // LANES
        fits = [u for u in range(1, units + 1) if units % u == 0
                and u * LANES * max(tm * a_bytes, tn * b_bytes) <= MM_BLOCK_BYTES]
        tk = max(fits) * LANES
    return tm, tn, tk


def _mm(a, b, *, ta=False, tb=False, name, side=None):
    (k_dim, m_dim) = a.shape if ta else a.shape[::-1]
    (n_dim, kb_dim) = b.shape if tb else b.shape[::-1]
    assert k_dim == kb_dim, (a.shape, b.shape, ta, tb)
    tm, tn, tk = _mm_tiles(m_dim, n_dim, k_dim, a.dtype.itemsize, b.dtype.itemsize)
    nk = k_dim // tk
    grid = (m_dim // tm, n_dim // tn, nk)
    dims = (((0 if ta else 1,), (1 if tb else 0,)), ((), ()))
    ns = side.n if side else 0

    def body(a_ref, b_ref, *rest):
        if side:
            side.run(rest[:ns], rest[ns + 1:2 * ns + 1], rest[2 * ns + 1:], *_grid_ends(grid),
                     lambda: compute(a_ref, b_ref, rest[ns]))
        else:
            compute(a_ref, b_ref, rest[0])

    def compute(a_ref, b_ref, o_ref):
        part = lax.dot_general(a_ref[...].astype(bf16), b_ref[...].astype(bf16), dims, preferred_element_type=f32)
        if nk == 1:
            o_ref[...] = part
        else:
            k = pl.program_id(2)

            @pl.when(k == 0)
            def _():
                o_ref[...] = part

            @pl.when(k > 0)
            def _():
                o_ref[...] += part

    a_spec = pl.BlockSpec((tk, tm), lambda i, j, k: (k, i)) if ta else pl.BlockSpec((tm, tk), lambda i, j, k: (i, k))
    b_spec = pl.BlockSpec((tn, tk), lambda i, j, k: (j, k)) if tb else pl.BlockSpec((tk, tn), lambda i, j, k: (k, j))
    out_spec = pl.BlockSpec((tm, tn), lambda i, j, k: (i, j))
    out_shape = jax.ShapeDtypeStruct((m_dim, n_dim), f32)
    if not side:
        return pl.pallas_call(body, name=name, grid=grid, in_specs=[a_spec, b_spec], out_specs=out_spec,
                              out_shape=out_shape,
                              compiler_params=_cparams(("parallel", "parallel", "arbitrary")))(a, b)
    return pl.pallas_call(
        body, name=name, grid=grid, in_specs=[a_spec, b_spec] + side.specs(), out_specs=[out_spec] + side.specs(),
        out_shape=[out_shape] + side.out_shapes, scratch_shapes=_comm_scratch(ns),
        compiler_params=_cparams(("arbitrary", "arbitrary", "arbitrary")))(a, b, *side.arrs)


def _win(t):
    return t if isinstance(t, tuple) else (t, t.shape[1], 0)


def _tile_spec(width, cb, tm):
    return pl.BlockSpec((tm, width), lambda i: (i, cb))


def _param_spec(p):
    return pl.BlockSpec(p.shape, lambda i: (0, 0))


def _stage_fwd(f, params, tiles, out_dtypes, name):
    tiles = [_win(t) for t in tiles]
    rows = tiles[0][0].shape[0]
    tm = min(TM, rows)
    avals = jax.eval_shape(f, *[jax.ShapeDtypeStruct(p.shape, f32) for p in params],
                           *[jax.ShapeDtypeStruct((tm, w), f32) for _, w, _ in tiles])
    n_p, n_t = len(params), len(tiles)

    def body(*refs):
        p = [r[...] for r in refs[:n_p]]
        t = [r[...].astype(f32) for r in refs[n_p:n_p + n_t]]
        for o_ref, val in zip(refs[n_p + n_t:], f(*p, *t)):
            o_ref[...] = val.astype(o_ref.dtype)

    return pl.pallas_call(
        body, name=name, grid=(rows // tm,),
        in_specs=[_param_spec(p) for p in params] + [_tile_spec(w, cb, tm) for _, w, cb in tiles],
        out_specs=[_tile_spec(a.shape[1], 0, tm) for a in avals],
        out_shape=[jax.ShapeDtypeStruct((rows, a.shape[1]), dt) for a, dt in zip(avals, out_dtypes)],
        compiler_params=_cparams(("parallel",)),
    )(*params, *[t[0] for t in tiles])


def _stage_bwd(f, params, tiles, cts, grad_dtypes, name, residual=None):
    tiles = [_win(t) for t in tiles]
    rows = tiles[0][0].shape[0]
    tm = min(TM, rows)
    cts = [list(g) if isinstance(g, (list, tuple)) else [g] for g in cts]
    flat_cts = [a for g in cts for a in g]
    n_p, n_t, n_c = len(params), len(tiles), len(flat_cts)
    has_res = residual is not None
    want = [j for j, dt in enumerate(grad_dtypes) if dt is not None]

    def body(*refs):
        i = pl.program_id(0)
        p = [r[...] for r in refs[:n_p]]
        t = [r[...].astype(f32) for r in refs[n_p:n_p + n_t]]
        ct_vals = [r[...].astype(f32) for r in refs[n_p + n_t:n_p + n_t + n_c]]
        ct, at = [], 0
        for g in cts:
            ct.append(functools.reduce(jnp.add, ct_vals[at:at + len(g)]))
            at += len(g)
        ct = tuple(ct)
        pos = n_p + n_t + n_c
        res_ref = refs[pos] if has_res else None
        pos += 1 if has_res else 0
        dp_refs = refs[pos:pos + n_p]
        dt_refs = refs[pos + n_p:]
        _, vjp = jax.vjp(f, *p, *t)
        grads = vjp(ct)

        @pl.when(i == 0)
        def _():
            for r in dp_refs:
                r[...] = jnp.zeros_like(r)

        for r, g in zip(dp_refs, grads[:n_p]):
            r[...] += g
        for r, j in zip(dt_refs, want):
            g = grads[n_p + j]
            if has_res and j == residual[0]:
                g = g + res_ref[...].astype(f32)
            r[...] = g.astype(r.dtype)

    in_arrays = list(params) + [t[0] for t in tiles] + flat_cts
    in_specs = ([_param_spec(p) for p in params] + [_tile_spec(w, cb, tm) for _, w, cb in tiles]
                + [_tile_spec(c.shape[1], 0, tm) for c in flat_cts])
    if has_res:
        in_arrays.append(residual[1])
        in_specs.append(_tile_spec(residual[1].shape[1], 0, tm))
    out_shape = ([jax.ShapeDtypeStruct(p.shape, f32) for p in params]
                 + [jax.ShapeDtypeStruct((rows, tiles[j][1]), grad_dtypes[j]) for j in want])
    out_specs = [_param_spec(p) for p in params] + [_tile_spec(tiles[j][1], 0, tm) for j in want]
    outs = pl.pallas_call(
        body, name=name, grid=(rows // tm,), in_specs=in_specs, out_specs=out_specs, out_shape=out_shape,
        compiler_params=_cparams(("arbitrary",)),
    )(*in_arrays)
    return outs[:n_p], outs[n_p:]


def _rms(x, w):
    return x * lax.rsqrt(jnp.mean(x * x, axis=-1, keepdims=True) + NORM_EPS) * w


def _f_normmod(w, shift, scale, x):
    return (_rms(x, w) * (1.0 + scale) + shift,)


def _f_merge(ga, gb, pa, pb):
    return (jax.nn.sigmoid(ga) * pa + jax.nn.sigmoid(gb) * pb,)


def _f_residual_normmod(gate, w, shift, scale, x, branch):
    x1 = x + gate * branch
    return x1, _rms(x1, w) * (1.0 + scale) + shift


def _f_loss(gate, wf, x1, fo, target):
    y = _rms(x1 + gate * fo, wf)
    err = jnp.square(y - target)
    return (0.5 * jnp.sum(jnp.mean(err, axis=-1, keepdims=True), axis=0, keepdims=True),)


def _loss_and_grads(gate2, wf, x1, fo, target):
    rows, d = x1.shape
    tm = min(TM, rows)

    def body(g_ref, w_ref, x_ref, fo_ref, t_ref, loss_ref, dg_ref, dw_ref, dx_ref, dfo_ref):
        i = pl.program_id(0)
        (val,), vjp = jax.vjp(_f_loss, g_ref[...], w_ref[...], x_ref[...], fo_ref[...], t_ref[...])
        dg, dw, dx, dfo, _ = vjp((jnp.ones((1, 1), f32),))

        @pl.when(i == 0)
        def _():
            loss_ref[...] = jnp.zeros_like(loss_ref)
            dg_ref[...] = jnp.zeros_like(dg_ref)
            dw_ref[...] = jnp.zeros_like(dw_ref)

        loss_ref[...] += jnp.broadcast_to(val, loss_ref.shape)
        dg_ref[...] += dg
        dw_ref[...] += dw
        dx_ref[...] = dx
        dfo_ref[...] = dfo.astype(bf16)

    vec = pl.BlockSpec((1, d), lambda i: (0, 0))
    tile = pl.BlockSpec((tm, d), lambda i: (i, 0))
    return pl.pallas_call(
        body, name="loss_fwd_bwd", grid=(rows // tm,),
        in_specs=[vec, vec, tile, tile, tile],
        out_specs=[pl.BlockSpec((1, LANES), lambda i: (0, 0)), vec, vec, tile, tile],
        out_shape=[jax.ShapeDtypeStruct((1, LANES), f32), jax.ShapeDtypeStruct((1, d), f32),
                   jax.ShapeDtypeStruct((1, d), f32), jax.ShapeDtypeStruct((rows, d), f32),
                   jax.ShapeDtypeStruct((rows, d), bf16)],
        compiler_params=_cparams(("arbitrary",)),
    )(gate2, wf, x1, fo, target)


def _softplus(z):
    return jnp.maximum(z, 0.0) + jnp.log(1.0 + jnp.exp(-jnp.abs(z)))


def _split_dot(a, m):
    hi = a.astype(bf16)
    lo = (a - hi.astype(f32)).astype(bf16)
    return jnp.dot(hi, m, preferred_element_type=f32) + jnp.dot(lo, m, preferred_element_type=f32)


def _suffix_matrix(n):
    r = lax.broadcasted_iota(jnp.int32, (n, n), 0)
    c = lax.broadcasted_iota(jnp.int32, (n, n), 1)
    return (r > c).astype(bf16)


def _head_masks():
    lane = lax.broadcasted_iota(jnp.int32, (1, LANES), 1)
    return [(lane < SB_HEAD_DIM).astype(f32), (lane >= SB_HEAD_DIM).astype(f32)]


def _sb_prepare(proj):
    def f(k, v):
        lane = lax.broadcasted_iota(jnp.int32, (1, SB_WIDTH), 1)
        m0 = (jnp.bitwise_and(lane, LANES - 1) < SB_HEAD_DIM).astype(f32)
        m1 = 1.0 - m0
        return k, k * m0, k * m1, v, v * m0, v * m1

    wins = [(proj, SB_WIDTH, OFF_SBK // SB_WIDTH), (proj, SB_WIDTH, OFF_SBV // SB_WIDTH)]
    return _stage_fwd(f, [], wins, [bf16] * 6, "sb_prepare")


def _stack_heads(x):
    m0, m1 = _head_masks()
    return jnp.concatenate([x * m0, x * m1], axis=0)


def _sb_logits(qst, k, t_pos2, kb, bq, masked):
    z = lax.dot_general(qst, k, (((1,), (1,)), ((), ())), preferred_element_type=f32)
    l = -_softplus(z)
    if masked:
        s_pos = kb * bq + lax.broadcasted_iota(jnp.int32, (1, bq), 1)
        causal = s_pos < t_pos2
        l = jnp.where(causal, l, 0.0)
    else:
        causal = None
    return z, l, causal


class _SideComm:
    def __init__(self, protocol, arrs, out_shapes):
        self.protocol, self.arrs, self.out_shapes = protocol, list(arrs), list(out_shapes)
        self.n = len(self.arrs)

    def specs(self):
        return [pl.BlockSpec(memory_space=pl.ANY)] * self.n

    def run(self, in_refs, out_refs, sems, first, last, compute):
        start, finish = self.protocol(in_refs, out_refs, *sems)
        pl.when(first)(start)
        compute()
        pl.when(last)(finish)


def _grid_ends(grid):
    ids = [pl.program_id(axis) for axis in range(len(grid))]
    first = functools.reduce(jnp.logical_and, [i == 0 for i in ids])
    last = functools.reduce(jnp.logical_and, [i == g - 1 for i, g in zip(ids, grid)])
    return first, last


def _sb_attention_fwd2(proj, k16, v0_16, v1_16, side=None):
    rows = proj.shape[0]
    bq = SB_QBLOCK
    nq = rows // bq
    assert nq <= LANES, "one lane per key block"
    npair = SB_WIDTH // LANES
    scale = SB_HEAD_DIM ** -0.5

    npp = SB_PAIRS_FWD
    wq = npp * LANES
    grid = (npair // npp, nq)
    ns = side.n if side else 0

    def body(q_ref, k_ref, v0_ref, v1_ref, *rest):
        o_ref, runs_ref = rest[ns], rest[ns + 1]
        if side:
            side.run(rest[:ns], rest[ns + 2:2 * ns + 2], rest[2 * ns + 2:], *_grid_ends(grid),
                     lambda: compute(q_ref, k_ref, v0_ref, v1_ref, o_ref, runs_ref))
        else:
            compute(q_ref, k_ref, v0_ref, v1_ref, o_ref, runs_ref)

    def compute(q_ref, k_ref, v0_ref, v1_ref, o_ref, runs_ref):
        qi = pl.program_id(1)
        pairs = [slice(pp * LANES, (pp + 1) * LANES) for pp in range(npp)]
        qst = [(_stack_heads(q_ref[:, s]) * scale).astype(bf16) for s in pairs]
        r = lax.broadcasted_iota(jnp.int32, (bq, 2 * bq), 0)
        c = lax.broadcasted_iota(jnp.int32, (bq, 2 * bq), 1)
        m2 = jnp.logical_or(r > c, c >= bq).astype(bf16)
        t_pos = qi * bq + lax.broadcasted_iota(jnp.int32, (bq, 1), 0)
        t_pos2 = jnp.concatenate([t_pos, t_pos], axis=0)
        lane = lax.broadcasted_iota(jnp.int32, (1, LANES), 1)
        runs_ref[...] = jnp.full(runs_ref.shape, SB_NEVER, f32)

        def tiles(kbs, carry, masked):
            jobs = [(pp, kb) for kb in kbs for pp in range(npp)]
            rows_k = [pl.ds(pl.multiple_of(kb * bq, bq), bq) for _, kb in jobs]
            zl = [_sb_logits(qst[pp], k_ref[rk, pairs[pp]], t_pos2, kb, bq, masked) for (pp, kb), rk in zip(jobs, rows_k)]
            cs = [_split_dot(l, m2) for _, l, _ in zl]
            run = [cr[0] for cr in carry]
            acc = [cr[1] for cr in carry]
            probs = []
            for (pp, kb), (z, l, causal), cs2 in zip(jobs, zl, cs):
                a = jnp.exp(z + l + cs2[:, :bq] + run[pp])
                if masked:
                    a = jnp.where(causal, a, 0.0)
                probs.append(a.astype(bf16))
                for hh in range(2):
                    cols = slice((2 * pp + hh) * LANES, (2 * pp + hh + 1) * LANES)
                    runs_ref[:, cols] = jnp.where(lane == kb, run[pp][hh * bq:(hh + 1) * bq], runs_ref[:, cols])
                run[pp] = run[pp] + cs2[:, bq:]
            for (pp, kb), rk, ab in zip(jobs, rows_k, probs):
                acc[pp] = (acc[pp] + jnp.dot(ab[:bq], v0_ref[rk, pairs[pp]], preferred_element_type=f32)
                           + jnp.dot(ab[bq:], v1_ref[rk, pairs[pp]], preferred_element_type=f32))
            return tuple(zip(run, acc))

        zero = (jnp.zeros((2 * bq, bq), f32), jnp.zeros((bq, LANES), f32))
        carry = tiles([qi], (zero,) * npp, True)

        def alive(cr):
            return functools.reduce(jnp.maximum, [jnp.max(run) for run, _ in cr]) > SB_DEAD

        def two(state):
            i, _, cr = state
            cr = tiles([qi - 1 - 2 * i, qi - 2 - 2 * i], cr, False)
            return i + 1, alive(cr), cr

        n_two = qi // 2
        i_end, still, carry = lax.while_loop(lambda st: jnp.logical_and(st[0] < n_two, st[1]), two,
                                             (jnp.int32(0), alive(carry), carry))
        last_one = jnp.logical_and(qi % 2 == 1, jnp.logical_and(still, i_end == n_two))
        carry = lax.cond(last_one, lambda cr: tiles([0], cr, False), lambda cr: cr, carry)
        for pp in range(npp):
            o_ref[:, pairs[pp]] = carry[pp][1]

    kv = pl.BlockSpec((rows, wq), lambda p, i: (0, p))
    return pl.pallas_call(
        body, name="sb_attn_fwd", grid=grid,
        in_specs=[pl.BlockSpec((bq, wq), lambda p, i: (i, OFF_SBQ // wq + p)), kv, kv, kv] + (side.specs() if side else []),
        out_specs=[pl.BlockSpec((bq, wq), lambda p, i: (i, p)),
                   pl.BlockSpec((bq, 2 * wq), lambda p, i: (i, p))] + (side.specs() if side else []),
        out_shape=[jax.ShapeDtypeStruct((rows, SB_WIDTH), f32),
                   jax.ShapeDtypeStruct((rows, SB_HEADS * LANES), f32)] + (side.out_shapes if side else []),
        scratch_shapes=_comm_scratch(ns) if side else [],
        compiler_params=_cparams(("arbitrary", "arbitrary")),
    )(proj, k16, v0_16, v1_16, *(side.arrs if side else []))


def _sb_attention_bwd2(proj, k16, k0_16, k1_16, v16, runs, do, side=None):
    rows = proj.shape[0]
    bq = SB_QBLOCK
    nq = rows // bq
    npair = SB_WIDTH // LANES
    scale = SB_HEAD_DIM ** -0.5
    tn = (((0,), (0,)), ((), ()))
    nt = (((1,), (1,)), ((), ()))

    npp = SB_PAIRS_BWD
    wq = npp * LANES
    grid = (npair // npp, nq)
    ns = side.n if side else 0

    def body(q_ref, k_ref, k0_ref, k1_ref, v_ref, runs_ref, do_ref, *rest):
        outs = rest[ns:ns + 3]
        ins = (q_ref, k_ref, k0_ref, k1_ref, v_ref, runs_ref, do_ref)
        if side:
            side.run(rest[:ns], rest[ns + 3:2 * ns + 3], rest[2 * ns + 3:], *_grid_ends(grid),
                     lambda: compute(*ins, *outs))
        else:
            compute(*ins, *outs)

    def compute(q_ref, k_ref, k0_ref, k1_ref, v_ref, runs_ref, do_ref, dq_ref, dk_ref, dv_ref):
        qi = pl.program_id(1)

        @pl.when(qi == 0)
        def _():
            dk_ref[...] = jnp.zeros_like(dk_ref)
            dv_ref[...] = jnp.zeros_like(dv_ref)

        pairs = [slice(pp * LANES, (pp + 1) * LANES) for pp in range(npp)]
        qst = [(_stack_heads(q_ref[:, s]) * scale).astype(bf16) for s in pairs]
        dost = [_stack_heads(do_ref[:, s]).astype(bf16) for s in pairs]
        runs = [jnp.concatenate([runs_ref[:, 2 * pp * LANES:(2 * pp + 1) * LANES],
                                 runs_ref[:, (2 * pp + 1) * LANES:(2 * pp + 2) * LANES]], axis=0) for pp in range(npp)]
        r = lax.broadcasted_iota(jnp.int32, (bq, 2 * bq), 0)
        c = lax.broadcasted_iota(jnp.int32, (bq, 2 * bq), 1)
        suffix_m = _suffix_matrix(bq)
        m2 = jnp.logical_or(r < c, c >= bq).astype(bf16)
        t_pos = qi * bq + lax.broadcasted_iota(jnp.int32, (bq, 1), 0)
        t_pos2 = jnp.concatenate([t_pos, t_pos], axis=0)
        lane = lax.broadcasted_iota(jnp.int32, (1, LANES), 1)

        def tiles(kbs, carry, masked):
            jobs = [(pp, kb) for kb in kbs for pp in range(npp)]
            rows_k = [pl.ds(pl.multiple_of(kb * bq, bq), bq) for _, kb in jobs]
            zl = [_sb_logits(qst[pp], k_ref[rk, pairs[pp]], t_pos2, kb, bq, masked) for (pp, kb), rk in zip(jobs, rows_k)]
            das = [lax.dot_general(dost[pp], v_ref[rk, pairs[pp]], nt, preferred_element_type=f32)
                   for (pp, kb), rk in zip(jobs, rows_k)]
            sticks = [_split_dot(l, suffix_m) for _, l, _ in zl]
            probs, ps = [], []
            for (pp, kb), (z, l, causal), stick, da in zip(jobs, zl, sticks, das):
                run = jnp.sum(jnp.where(lane == kb, runs[pp], 0.0), axis=1, keepdims=True)
                a = jnp.exp(z + l + stick + run)
                if masked:
                    a = jnp.where(causal, a, 0.0)
                probs.append(a.astype(bf16))
                ps.append(da * a)
            pcs = [_split_dot(p, m2) for p in ps]
            pref = [cr[0] for cr in carry]
            dq_acc = [cr[1] for cr in carry]
            dzs = []
            for (pp, kb), (z, l, causal), p, pc2 in zip(jobs, zl, ps, pcs):
                dz = p * jnp.exp(l) - jnp.exp(z + l) * (pc2[:, :bq] + pref[pp])
                if masked:
                    dz = jnp.where(causal, dz, 0.0)
                dzs.append(dz.astype(bf16))
                pref[pp] = pref[pp] + pc2[:, bq:]
            for (pp, kb), rk, dzb, ab in zip(jobs, rows_k, dzs, probs):
                cols = pairs[pp]
                dq_acc[pp] = (dq_acc[pp] + jnp.dot(dzb[:bq], k0_ref[rk, cols], preferred_element_type=f32)
                              + jnp.dot(dzb[bq:], k1_ref[rk, cols], preferred_element_type=f32))
                dk_ref[rk, cols] += lax.dot_general(dzb, qst[pp], tn, preferred_element_type=f32)
                dv_ref[rk, cols] += lax.dot_general(ab, dost[pp], tn, preferred_element_type=f32)
            return tuple(zip(pref, dq_acc))

        zero = (jnp.zeros((2 * bq, bq), f32), jnp.zeros((bq, LANES), f32))
        colmax = functools.reduce(jnp.maximum, [jnp.max(x, axis=0, keepdims=True) for x in runs])
        live = jnp.logical_and(colmax > SB_DEAD, lane < qi)
        kb0 = jnp.minimum(jnp.min(jnp.where(live, lane, LANES)), qi)
        n_blocks = qi - kb0
        carry = lax.fori_loop(0, n_blocks // 2, lambda i, cr: tiles([kb0 + 2 * i, kb0 + 2 * i + 1], cr, False),
                              (zero,) * npp)
        carry = lax.cond(n_blocks % 2 == 1, lambda cr: tiles([qi - 1], cr, False), lambda cr: cr, carry)
        carry = tiles([qi], carry, True)
        for pp in range(npp):
            dq_ref[:, pairs[pp]] = (carry[pp][1] * scale).astype(dq_ref.dtype)

    blk = pl.BlockSpec((bq, wq), lambda p, i: (i, p))
    full = pl.BlockSpec((rows, wq), lambda p, i: (0, p), pipeline_mode=pl.Buffered(1))
    return pl.pallas_call(
        body, name="sb_attn_bwd", grid=grid,
        in_specs=[pl.BlockSpec((bq, wq), lambda p, i: (i, OFF_SBQ // wq + p)), full, full, full, full,
                  pl.BlockSpec((bq, 2 * wq), lambda p, i: (i, p)), blk] + (side.specs() if side else []),
        out_specs=[blk, full, full] + (side.specs() if side else []),
        out_shape=[jax.ShapeDtypeStruct((rows, SB_WIDTH), bf16), jax.ShapeDtypeStruct((rows, SB_WIDTH), f32),
                   jax.ShapeDtypeStruct((rows, SB_WIDTH), f32)] + (side.out_shapes if side else []),
        scratch_shapes=_comm_scratch(ns) if side else [],
        compiler_params=_cparams(("arbitrary", "arbitrary")),
    )(proj, k16, k0_16, k1_16, v16, runs, do, *(side.arrs if side else []))


def _shift_down(x, prev8, j):
    if j == 0:
        return x
    r = pltpu.roll(x, j, axis=0)
    row8 = lax.broadcasted_iota(jnp.int32, prev8.shape, 0)
    head = jnp.where(row8 < j, pltpu.roll(prev8, j, axis=0), r[0:SUBLANES])
    return jnp.concatenate([head, r[SUBLANES:]], axis=0)


def _shift_up(x, next8, j):
    if j == 0:
        return x
    n = x.shape[0]
    r = pltpu.roll(x, n - j, axis=0)
    row8 = lax.broadcasted_iota(jnp.int32, next8.shape, 0)
    tail = jnp.where(row8 >= SUBLANES - j, pltpu.roll(next8, SUBLANES - j, axis=0), r[n - SUBLANES:n])
    return jnp.concatenate([r[:n - SUBLANES], tail], axis=0)


def _conv(x, prev8, w):
    k_taps = w.shape[0]
    out = x * w[k_taps - 1:k_taps, :]
    for j in range(1, k_taps):
        out = out + _shift_down(x, prev8, j) * w[k_taps - 1 - j:k_taps - j, :]
    return out


def _conv_tiles(rows):
    tr = min(TCONV_R, rows)
    return tr, rows // tr, tr // SUBLANES


def _prev_spec(tc, cb0, r8):
    return pl.BlockSpec((SUBLANES, tc), lambda j, i: (jnp.maximum(i * r8 - 1, 0), cb0 + j))


def _silu(x):
    return x * jax.nn.sigmoid(x)


def _dsilu(x):
    s = jax.nn.sigmoid(x)
    return s * (1.0 + x * (1.0 - s))


def _dn_conv_fwd(proj, w):
    rows = proj.shape[0]
    tr, nr, r8 = _conv_tiles(rows)
    tc = TCONV_C
    cb0 = OFF_DN // tc

    def body(x_ref, p_ref, w_ref, o_ref):
        prev = jnp.where(pl.program_id(1) == 0, 0.0, p_ref[...])
        o_ref[...] = _silu(_conv(x_ref[...], prev, w_ref[...]))

    return pl.pallas_call(
        body, name="dn_conv_fwd", grid=(DN_CONV_CH // tc, nr),
        in_specs=[pl.BlockSpec((tr, tc), lambda j, i: (i, cb0 + j)), _prev_spec(tc, cb0, r8),
                  pl.BlockSpec((DN_CONV_WIDTH, tc), lambda j, i: (0, j))],
        out_specs=pl.BlockSpec((tr, tc), lambda j, i: (i, j)),
        out_shape=jax.ShapeDtypeStruct((rows, DN_CONV_CH), f32),
        compiler_params=_cparams(("parallel", "parallel")),
    )(proj, proj, w)


def _dn_conv_bwd_act(proj, w, dact):
    rows = proj.shape[0]
    tr, nr, r8 = _conv_tiles(rows)
    tc = TCONV_C
    cb0 = OFF_DN // tc

    def body(x_ref, p_ref, w_ref, d_ref, o_ref):
        prev = jnp.where(pl.program_id(1) == 0, 0.0, p_ref[...])
        o_ref[...] = d_ref[...] * _dsilu(_conv(x_ref[...], prev, w_ref[...]))

    return pl.pallas_call(
        body, name="dn_conv_bwd_act", grid=(DN_CONV_CH // tc, nr),
        in_specs=[pl.BlockSpec((tr, tc), lambda j, i: (i, cb0 + j)), _prev_spec(tc, cb0, r8),
                  pl.BlockSpec((DN_CONV_WIDTH, tc), lambda j, i: (0, j)),
                  pl.BlockSpec((tr, tc), lambda j, i: (i, j))],
        out_specs=pl.BlockSpec((tr, tc), lambda j, i: (i, j)),
        out_shape=jax.ShapeDtypeStruct((rows, DN_CONV_CH), f32),
        compiler_params=_cparams(("parallel", "parallel")),
    )(proj, proj, w, dact)


def _ffn_conv_fwd(u_pre, w, b):
    rows = u_pre.shape[0]
    tr, nr, r8 = _conv_tiles(rows)
    tc = TCONV_FF
    nct = D_FF // tc

    def body(xg_ref, pg_ref, xu_ref, pu_ref, wg_ref, wu_ref, bg_ref, bu_ref, o_ref, u_ref):
        first = pl.program_id(1) == 0
        ug = _conv(xg_ref[...], jnp.where(first, 0.0, pg_ref[...]), wg_ref[...]) + bg_ref[...]
        uu = _conv(xu_ref[...], jnp.where(first, 0.0, pu_ref[...]), wu_ref[...]) + bu_ref[...]
        o_ref[...] = (_silu(ug) * uu).astype(o_ref.dtype)
        u_ref[0] = ug
        u_ref[1] = uu

    def x_spec(off):
        return pl.BlockSpec((tr, tc), lambda j, i: (i, off + j))

    def w_spec(k, off):
        return pl.BlockSpec((k, tc), lambda j, i: (0, off + j))

    return pl.pallas_call(
        body, name="ffn_conv_fwd", grid=(nct, nr),
        in_specs=[x_spec(0), _prev_spec(tc, 0, r8), x_spec(nct), _prev_spec(tc, nct, r8),
                  w_spec(FFN_CONV_WIDTH, 0), w_spec(FFN_CONV_WIDTH, nct), w_spec(1, 0), w_spec(1, nct)],
        out_specs=[pl.BlockSpec((tr, tc), lambda j, i: (i, j)), pl.BlockSpec((2, tr, tc), lambda j, i: (0, i, j))],
        out_shape=[jax.ShapeDtypeStruct((rows, D_FF), bf16), jax.ShapeDtypeStruct((2, rows, D_FF), f32)],
        compiler_params=_cparams(("parallel", "parallel")),
    )(u_pre, u_pre, u_pre, u_pre, w, w, b, b)


def _ffn_conv_bwd_act(u, dact):
    rows = dact.shape[0]
    tr, nr, _ = _conv_tiles(rows)
    tc = TCONV_FF
    nct = D_FF // tc

    def body(u_ref, d_ref, du_ref, dbg_ref, dbu_ref):
        first = pl.program_id(1) == 0
        ug = u_ref[0]
        uu = u_ref[1]
        d = d_ref[...]
        sig = jax.nn.sigmoid(ug)
        dug = d * uu * (sig * (1.0 + ug * (1.0 - sig)))
        duu = d * (ug * sig)
        du_ref[0] = dug
        du_ref[1] = duu

        @pl.when(first)
        def _():
            dbg_ref[...] = jnp.zeros_like(dbg_ref)
            dbu_ref[...] = jnp.zeros_like(dbu_ref)

        dbg_ref[...] += jnp.sum(dug, axis=0, keepdims=True)
        dbu_ref[...] += jnp.sum(duu, axis=0, keepdims=True)

    pair = pl.BlockSpec((2, tr, tc), lambda j, i: (0, i, j))
    vec = pl.BlockSpec((1, tc), lambda j, i: (0, j))
    return pl.pallas_call(
        body, name="ffn_conv_bwd_act", grid=(nct, nr),
        in_specs=[pair, pl.BlockSpec((tr, tc), lambda j, i: (i, j))],
        out_specs=[pair, vec, vec],
        out_shape=[jax.ShapeDtypeStruct((2, rows, D_FF), f32),
                   jax.ShapeDtypeStruct((1, D_FF), f32), jax.ShapeDtypeStruct((1, D_FF), f32)],
        compiler_params=_cparams(("parallel", "arbitrary")),
    )(u, dact)


def _conv_bwd(dy, x, x_cb0, w, name):
    k_taps = w.shape[0]
    split = dy.ndim == 3
    rows = dy.shape[-2]
    ch = dy.shape[-1] * (2 if split else 1)
    tc = TCONV_FF if split else TCONV_C
    tr, nr, r8 = _conv_tiles(rows)
    per_half = dy.shape[-1] // tc
    last8 = rows // SUBLANES - 1

    def body(dy_ref, nx_ref, x_ref, w_ref, dx_ref, dw_ref):
        i = pl.program_id(1)
        dyv = dy_ref[...]
        nxt = jnp.where(i == nr - 1, 0.0, nx_ref[...])
        xv = x_ref[...].astype(f32)
        wv = w_ref[...]

        @pl.when(i == 0)
        def _():
            dw_ref[...] = jnp.zeros_like(dw_ref)

        dx = dyv * wv[k_taps - 1:k_taps, :]
        dw_ref[k_taps - 1:k_taps, :] += jnp.sum(dyv * xv, axis=0, keepdims=True)
        for j in range(1, k_taps):
            dy_j = _shift_up(dyv, nxt, j)
            dx = dx + dy_j * wv[k_taps - 1 - j:k_taps - j, :]
            dw_ref[k_taps - 1 - j:k_taps - j, :] += jnp.sum(dy_j * xv, axis=0, keepdims=True)
        dx_ref[...] = dx.astype(dx_ref.dtype)

    tile = pl.BlockSpec((tr, tc), lambda j, i: (i, j))
    if split:
        dy_spec = pl.BlockSpec((None, tr, tc), lambda j, i: (j // per_half, i, j % per_half))
        next_spec = pl.BlockSpec((None, SUBLANES, tc),
                                 lambda j, i: (j // per_half, jnp.minimum((i + 1) * r8, last8), j % per_half))
    else:
        dy_spec = tile
        next_spec = pl.BlockSpec((SUBLANES, tc), lambda j, i: (jnp.minimum((i + 1) * r8, last8), j))
    return pl.pallas_call(
        body, name=name, grid=(ch // tc, nr),
        in_specs=[dy_spec, next_spec, pl.BlockSpec((tr, tc), lambda j, i: (i, x_cb0 + j)),
                  pl.BlockSpec((k_taps, tc), lambda j, i: (0, j))],
        out_specs=[tile, pl.BlockSpec((k_taps, tc), lambda j, i: (0, j))],
        out_shape=[jax.ShapeDtypeStruct((rows, ch), bf16), jax.ShapeDtypeStruct((k_taps, ch), f32)],
        compiler_params=_cparams(("parallel", "arbitrary")),
    )(dy, dy, x, w)


def _hdot(a, b):
    return jnp.dot(a, b, preferred_element_type=f32, precision=lax.Precision.HIGH)


def _xdot(a, b):
    return jnp.dot(a, b, preferred_element_type=f32, precision=lax.Precision.HIGHEST)


def _bdot(a, b):
    return jnp.dot(a.astype(bf16), b.astype(bf16), preferred_element_type=f32)


def _bdot_nt(a, b):
    return lax.dot_general(a.astype(bf16), b.astype(bf16), (((1,), (1,)), ((), ())), preferred_element_type=f32)


def _bdot_tn(a, b):
    return lax.dot_general(a.astype(bf16), b.astype(bf16), (((0,), (0,)), ((), ())), preferred_element_type=f32)


GDN_GROUP = 4
GDN_NGROUPS = DN_HEADS // GDN_GROUP
GDN_ROWS = GDN_GROUP * DN_CHUNK
GDN_QK_LANES = GDN_GROUP * DN_KEY_DIM
GDN_LOGIT_LANE = DN_HEADS


def _inverse_impl(lows):
    n = lows[0].shape[0]
    r = lax.broadcasted_iota(jnp.int32, (n, n), 0)
    c = lax.broadcasted_iota(jnp.int32, (n, n), 1)
    eye = (r == c).astype(f32)
    blk = jnp.right_shift(r, 3) == jnp.right_shift(c, 3)
    d = [jnp.where(blk, low, 0.0) for low in lows]
    e = [low - x for low, x in zip(lows, d)]

    def nilpotent8_inverse(xs):
        acc = [eye - x for x in xs]
        power = xs
        for _ in range(2):
            power = [_bdot(x, x) for x in power]
            acc = [_bdot(a, eye + x) for a, x in zip(acc, power)]
        return acc

    dinv = nilpotent8_inverse(d)
    ninv = nilpotent8_inverse([_bdot(x, y) for x, y in zip(dinv, e)])
    t = [_bdot(x, y) for x, y in zip(ninv, dinv)]
    for _ in range(2):
        res = [eye - x - _hdot(low, x) for low, x in zip(lows, t)]
        t = [x + _bdot(x, y) for x, y in zip(t, res)]
    return tuple(t)


@jax.custom_vjp
def _unit_lower_inverses(lows):
    return _inverse_impl(lows)


def _unit_lower_inverses_fwd(lows):
    t = _inverse_impl(lows)
    return t, t


def _unit_lower_inverses_bwd(t, ct):
    left = [_bdot_tn(x, g) for x, g in zip(t, ct)]
    return (tuple(-_bdot_nt(x, y) for x, y in zip(left, t)),)


_unit_lower_inverses.defvjp(_unit_lower_inverses_fwd, _unit_lower_inverses_bwd)


@jax.custom_vjp
def _known_inverses(lows, t):
    return t


def _known_inverses_fwd(lows, t):
    return t, t


def _known_inverses_bwd(t, ct):
    return _unit_lower_inverses_bwd(t, ct) + (tuple(jnp.zeros_like(x) for x in t),)


_known_inverses.defvjp(_known_inverses_fwd, _known_inverses_bwd)


def _gdn_chunk(a_log, dt_bias, norm_w, ba, *per_group, inverses=None, keep_inverses=False):
    ng = GDN_NGROUPS
    qgs, kgs, vsts, zsts, states = [per_group[i * ng:(i + 1) * ng] for i in range(5)]
    groups = range(ng)
    n = GDN_ROWS
    r = lax.broadcasted_iota(jnp.int32, (n, n), 0)
    c = lax.broadcasted_iota(jnp.int32, (n, n), 1)
    same_head = jnp.right_shift(r, 6) == jnp.right_shift(c, 6)
    incl = jnp.logical_and(same_head, r >= c)
    strict = jnp.logical_and(same_head, r > c)
    eye = (r == c).astype(f32)
    ones = jnp.ones((n, n), f32)
    own_lanes = same_head.astype(f32)
    lane = lax.broadcasted_iota(jnp.int32, (1, LANES), 1)
    pick = lambda arr, idx: jnp.sum(jnp.where(lane == idx, arr, 0.0), axis=1, keepdims=True)
    heads = [[GDN_GROUP * g + h for h in range(GDN_GROUP)] for g in groups]
    rc = lax.broadcasted_iota(jnp.int32, (DN_CHUNK, DN_CHUNK), 0)
    cc = lax.broadcasted_iota(jnp.int32, (DN_CHUNK, DN_CHUNK), 1)

    g_all = -jnp.exp(a_log) * _softplus(ba + dt_bias)
    gc_all = _xdot((rc >= cc).astype(f32), g_all)
    gl_all = jnp.sum(g_all, axis=0, keepdims=True)
    beta = [jnp.concatenate([jax.nn.sigmoid(pick(ba, hd)) for hd in heads[g]], axis=0) for g in groups]
    gc = [jnp.concatenate([pick(gc_all, GDN_LOGIT_LANE + hd) for hd in heads[g]], axis=0) for g in groups]
    g_last = [jnp.concatenate([jnp.broadcast_to(pick(gl_all, GDN_LOGIT_LANE + hd), (DN_CHUNK, 1)) for hd in heads[g]],
                              axis=0) for g in groups]
    gr = [jnp.broadcast_to(gc[g], (n, n)).T for g in groups]
    decay = [jnp.where(incl, jnp.exp(jnp.where(incl, gc[g] - gr[g], 0.0)), 0.0) for g in groups]
    q = [jnp.concatenate([qgs[g]] * GDN_GROUP, axis=0) * own_lanes for g in groups]
    k = [jnp.concatenate([kgs[g]] * GDN_GROUP, axis=0) * own_lanes for g in groups]
    qn = [x * lax.rsqrt(jnp.sum(x * x, axis=1, keepdims=True) + L2_EPS) * (DN_KEY_DIM ** -0.5) for x in q]
    kn = [x * lax.rsqrt(jnp.sum(x * x, axis=1, keepdims=True) + L2_EPS) for x in k]
    kb = [kn[g] * beta[g] for g in groups]
    low = [jnp.where(strict, _bdot_nt(kb[g], kn[g]) * decay[g], 0.0) for g in groups]
    intra = [jnp.where(incl, _bdot_nt(qn[g], kn[g]) * decay[g], 0.0) for g in groups]
    t = _unit_lower_inverses(tuple(low)) if inverses is None else _known_inverses(tuple(low), tuple(inverses))
    u = [_bdot(t[g], vsts[g] * beta[g]) for g in groups]
    w = [_bdot(t[g], kb[g] * jnp.exp(gc[g])) for g in groups]
    sb = [s.astype(bf16) for s in states]
    v_new = [u[g] - jnp.dot(w[g].astype(bf16), sb[g], preferred_element_type=f32) for g in groups]
    o = [jnp.dot((qn[g] * jnp.exp(gc[g])).astype(bf16), sb[g], preferred_element_type=f32) for g in groups]
    o = [o[g] + _bdot(intra[g], v_new[g]) for g in groups]
    new_state = [states[g] * jnp.exp(g_last[g]) + _bdot_tn(kn[g] * jnp.exp(g_last[g] - gc[g]), v_new[g])
                 for g in groups]
    o_n = [x * lax.rsqrt(jnp.mean(x * x, axis=1, keepdims=True) + NORM_EPS) * norm_w for x in o]
    return tuple(o_n[g] * _silu(zsts[g]) for g in groups) + tuple(new_state) + (tuple(t) if keep_inverses else ())


GDN_FWD_CHUNKS = 2
GDN_BWD_CHUNKS = 1


def _gdn_specs(rows, reverse, nc):
    tr = nc * DN_CHUNK
    n = rows // tr
    idx = (lambda i: n - 1 - i) if reverse else (lambda i: i)
    vec = pl.BlockSpec((1, LANES), lambda i: (0, 0))
    qkv = pl.BlockSpec((tr, DN_CONV_CH), lambda i: (idx(i), 0))
    z = pl.BlockSpec((tr, DN_V_WIDTH), lambda i: (idx(i), OFF_Z // DN_V_WIDTH))
    ba = pl.BlockSpec((tr, LANES), lambda i: (idx(i), 0))
    wide = pl.BlockSpec((tr, DN_V_WIDTH), lambda i: (idx(i), 0))
    st = pl.BlockSpec((nc, DN_HEADS * DN_KEY_DIM, LANES), lambda i: (idx(i), 0, 0))
    inv = pl.BlockSpec((nc, GDN_NGROUPS * GDN_ROWS, GDN_ROWS), lambda i: (idx(i), 0, 0))
    return n, vec, qkv, z, ba, wide, st, inv


def _chunk_rows(c):
    return slice(c * DN_CHUNK, (c + 1) * DN_CHUNK)


def _gdn_slices(grp):
    q = slice(grp * GDN_QK_LANES, (grp + 1) * GDN_QK_LANES)
    k = slice(DN_QK_WIDTH + grp * GDN_QK_LANES, DN_QK_WIDTH + (grp + 1) * GDN_QK_LANES)
    heads = [slice((GDN_GROUP * grp + h) * LANES, (GDN_GROUP * grp + h + 1) * LANES) for h in range(GDN_GROUP)]
    vs = [slice(2 * DN_QK_WIDTH + s.start, 2 * DN_QK_WIDTH + s.stop) for s in heads]
    return q, k, vs, heads


def _stack_cols(ref, rows, cols):
    return jnp.concatenate([ref[rows, s] for s in cols], axis=0)


def _gdn_operands(qkv_ref, z_ref, rows, state_rows):
    sl = [_gdn_slices(grp) for grp in range(GDN_NGROUPS)]
    return ([qkv_ref[rows, q] for q, _, _, _ in sl] + [qkv_ref[rows, k] for _, k, _, _ in sl]
            + [_stack_cols(qkv_ref, rows, vs) for _, _, vs, _ in sl]
            + [_stack_cols(z_ref, rows, heads) for _, _, _, heads in sl]
            + [state_rows[grp * GDN_ROWS:(grp + 1) * GDN_ROWS, :] for grp in range(GDN_NGROUPS)])


def _gdn_fwd(a_log, dt_bias, norm_w, qkv_act, proj, ba):
    rows = qkv_act.shape[0]
    n, vec, qkv_s, z_s, ba_s, wide, st_s, inv_s = _gdn_specs(rows, False, GDN_FWD_CHUNKS)

    def body(al_ref, dt_ref, nw_ref, qkv_ref, z_ref, ba_ref, o_ref, st_ref, inv_ref, state):
        @pl.when(pl.program_id(0) == 0)
        def _():
            state[...] = jnp.zeros_like(state)

        for c in range(GDN_FWD_CHUNKS):
            tok = _chunk_rows(c)
            st_ref[c] = state[...]
            out = _gdn_chunk(al_ref[...], dt_ref[...], nw_ref[...], ba_ref[tok, :],
                             *_gdn_operands(qkv_ref, z_ref, tok, state), keep_inverses=True)
            for grp in range(GDN_NGROUPS):
                _, _, _, heads = _gdn_slices(grp)
                rs = slice(grp * GDN_ROWS, (grp + 1) * GDN_ROWS)
                for h, s in enumerate(heads):
                    o_ref[tok, s] = out[grp][h * DN_CHUNK:(h + 1) * DN_CHUNK].astype(o_ref.dtype)
                state[rs, :] = out[GDN_NGROUPS + grp]
                inv_ref[c, rs, :] = out[2 * GDN_NGROUPS + grp]

    n_chunks = rows // DN_CHUNK
    return pl.pallas_call(
        body, name="gdn_fwd", grid=(n,),
        in_specs=[vec, vec, vec, qkv_s, z_s, ba_s], out_specs=[wide, st_s, inv_s],
        out_shape=[jax.ShapeDtypeStruct((rows, DN_V_WIDTH), bf16),
                   jax.ShapeDtypeStruct((n_chunks, DN_HEADS * DN_KEY_DIM, LANES), f32),
                   jax.ShapeDtypeStruct((n_chunks, GDN_NGROUPS * GDN_ROWS, GDN_ROWS), f32)],
        scratch_shapes=[pltpu.VMEM((DN_HEADS * DN_KEY_DIM, LANES), f32)],
        compiler_params=_cparams(("arbitrary",)),
    )(a_log, dt_bias, norm_w, qkv_act, proj, ba)


def _gdn_bwd(a_log, dt_bias, norm_w, qkv_act, proj, ba, states, inverses, do):
    rows = qkv_act.shape[0]
    n, vec, qkv_s, z_s, ba_s, wide, st_s, inv_s = _gdn_specs(rows, True, GDN_BWD_CHUNKS)

    def body(al_ref, dt_ref, nw_ref, qkv_ref, z_ref, ba_ref, st_ref, inv_ref, do_ref,
             dal_ref, ddt_ref, dnw_ref, dqkv_ref, dz_ref, dba_ref, dstate):
        @pl.when(pl.program_id(0) == 0)
        def _():
            dstate[...] = jnp.zeros_like(dstate)
            dal_ref[...] = jnp.zeros_like(dal_ref)
            ddt_ref[...] = jnp.zeros_like(ddt_ref)
            dnw_ref[...] = jnp.zeros_like(dnw_ref)

        ng = GDN_NGROUPS
        for c in reversed(range(GDN_BWD_CHUNKS)):
            tok = _chunk_rows(c)
            kept = [inv_ref[c, grp * GDN_ROWS:(grp + 1) * GDN_ROWS, :] for grp in range(ng)]
            _, vjp = jax.vjp(functools.partial(_gdn_chunk, inverses=kept), al_ref[...], dt_ref[...], nw_ref[...],
                             ba_ref[tok, :], *_gdn_operands(qkv_ref, z_ref, tok, st_ref[c]))
            cts = tuple(_stack_cols(do_ref, tok, _gdn_slices(grp)[3]) for grp in range(ng))
            cts += tuple(dstate[grp * GDN_ROWS:(grp + 1) * GDN_ROWS, :] for grp in range(ng))
            grads = vjp(cts)
            dal_ref[...] += grads[0]
            ddt_ref[...] += grads[1]
            dnw_ref[...] += grads[2]
            dba_ref[tok, :] = grads[3]
            dqs, dks, dvs, dzs, dss = [grads[4 + i * ng:4 + (i + 1) * ng] for i in range(5)]
            for grp in range(ng):
                q, k, vs, heads = _gdn_slices(grp)
                dqkv_ref[tok, q] = dqs[grp]
                dqkv_ref[tok, k] = dks[grp]
                for h, (sv, sh) in enumerate(zip(vs, heads)):
                    rows_h = slice(h * DN_CHUNK, (h + 1) * DN_CHUNK)
                    dqkv_ref[tok, sv] = dvs[grp][rows_h]
                    dz_ref[tok, sh] = dzs[grp][rows_h].astype(dz_ref.dtype)
                dstate[grp * GDN_ROWS:(grp + 1) * GDN_ROWS, :] = dss[grp]

    return pl.pallas_call(
        body, name="gdn_bwd", grid=(n,),
        in_specs=[vec, vec, vec, qkv_s, z_s, ba_s, st_s, inv_s, wide],
        out_specs=[vec, vec, vec, qkv_s, wide, ba_s],
        out_shape=[jax.ShapeDtypeStruct((1, LANES), f32)] * 3
        + [jax.ShapeDtypeStruct((rows, DN_CONV_CH), f32), jax.ShapeDtypeStruct((rows, DN_V_WIDTH), bf16),
           jax.ShapeDtypeStruct((rows, LANES), f32)],
        scratch_shapes=[pltpu.VMEM((DN_HEADS * DN_KEY_DIM, LANES), f32)],
        compiler_params=_cparams(("arbitrary",)),
    )(a_log, dt_bias, norm_w, qkv_act, proj, ba, states, inverses, do)


def _ada_fwd(c_all, w_loc, b_loc):
    def body(c_ref, w_ref, b_ref, o_ref):
        o_ref[...] = _bdot(_silu(c_ref[...]), w_ref[...]) + b_ref[...]

    return pl.pallas_call(body, name="ada_fwd", out_shape=jax.ShapeDtypeStruct((c_all.shape[0], w_loc.shape[1]), f32),
                          compiler_params=_cparams())(c_all, w_loc, b_loc)


def _ada_bwd(c_all, dmod_cols):
    def body(c_ref, d_ref, o_ref):
        o_ref[...] = _bdot_tn(_silu(c_ref[...]), d_ref[...])

    return pl.pallas_call(body, name="ada_bwd",
                          out_shape=jax.ShapeDtypeStruct((c_all.shape[1], dmod_cols.shape[1]), f32),
                          compiler_params=_cparams())(c_all, dmod_cols)


def _sum_devices(parts):
    def body(p_ref, o_ref):
        acc = p_ref[0:1, :]
        for d in range(1, N_DEV):
            acc = acc + p_ref[d:d + 1, :]
        o_ref[...] = acc

    return pl.pallas_call(body, name="sum_small", out_shape=jax.ShapeDtypeStruct((1, parts.shape[1]), f32),
                          compiler_params=_cparams())(parts)


def _adam_math(w, g, m, v):
    m2 = ADAM_B1 * m + (1.0 - ADAM_B1) * g
    v2 = ADAM_B2 * v + (1.0 - ADAM_B2) * jnp.square(g)
    m_hat = m2 / (1.0 - ADAM_B1 ** ADAM_STEP)
    v_hat = v2 / (1.0 - ADAM_B2 ** ADAM_STEP)
    delta = -ADAM_LR * (m_hat / (jnp.sqrt(v_hat) + ADAM_EPS) + ADAM_WD * w)
    return delta, m2, v2


def _row_tile(rows):
    return _pick(rows, (256, 128, 64, 32, 16, 8))


def _adamw(w, g, m, v, name):
    rows, cols = w.shape
    tr = _row_tile(rows)

    def body(w_ref, g_ref, m_ref, v_ref, d_ref, m2_ref, v2_ref):
        d_ref[...], m2_ref[...], v2_ref[...] = _adam_math(w_ref[...], g_ref[...], m_ref[...], v_ref[...])

    tile = pl.BlockSpec((tr, cols), lambda i: (i, 0))
    return pl.pallas_call(body, name=name, grid=(rows // tr,), in_specs=[tile] * 4, out_specs=[tile] * 3,
                          out_shape=[jax.ShapeDtypeStruct(w.shape, f32)] * 3,
                          compiler_params=_cparams(("parallel",)))(w, g, m, v)


def _sum_adamw(parts, w, m, v, name):
    rows, cols = w.shape
    tr = _row_tile(rows)

    def body(p_ref, w_ref, m_ref, v_ref, g_ref, d_ref, m2_ref, v2_ref):
        g = p_ref[0].astype(f32)
        for d in range(1, N_DEV):
            g = g + p_ref[d].astype(f32)
        g_ref[...] = g
        d_ref[...], m2_ref[...], v2_ref[...] = _adam_math(w_ref[...], g, m_ref[...], v_ref[...])

    tile = pl.BlockSpec((tr, cols), lambda i: (i, 0))
    return pl.pallas_call(body, name=name, grid=(rows // tr,),
                          in_specs=[pl.BlockSpec((N_DEV, tr, cols), lambda i: (0, i, 0)), tile, tile, tile],
                          out_specs=[tile] * 4, out_shape=[jax.ShapeDtypeStruct(w.shape, f32)] * 4,
                          compiler_params=_cparams(("parallel",)))(parts, w, m, v)


def _pad_lanes(a, width):
    return jnp.pad(a, ((0, 0), (0, width - a.shape[1])))


def _cols_by_device(full):
    r, c = full.shape
    return jnp.moveaxis(full.reshape(r, N_DEV, c // N_DEV), 1, 0)


def _cols_from_devices(parts):
    d, r, n = parts.shape
    return jnp.moveaxis(parts, 0, 1).reshape(r, d * n)


def kernel(x, c, w_ada, b_ada, norm1_w, w_in, dn_conv_w, dn_A_log, dn_dt_bias, dn_norm_w, w_proj_sb, w_proj_dn, w_out, norm2_w, w_ffn_in, ffn_conv_w, ffn_conv_b, w_ffn_out, final_norm_w, loss_target, m_w_ada, m_b_ada, m_norm1_w, m_w_in, m_dn_conv_w, m_dn_A_log, m_dn_dt_bias, m_dn_norm_w, m_w_proj_sb, m_w_proj_dn, m_w_out, m_norm2_w, m_w_ffn_in, m_ffn_conv_w, m_ffn_conv_b, m_w_ffn_out, m_final_norm_w, v_w_ada, v_b_ada, v_norm1_w, v_w_in, v_dn_conv_w, v_dn_A_log, v_dn_dt_bias, v_dn_norm_w, v_w_proj_sb, v_w_proj_dn, v_w_out, v_norm2_w, v_w_ffn_in, v_ffn_conv_w, v_ffn_conv_b, v_w_ffn_out, v_final_norm_w):
    d = D_MODEL
    me = 4 * lax.axis_index("x") + 2 * lax.axis_index("y") + lax.axis_index("c")
    xs = x[0]
    target = loss_target[0]
    n_ada = w_ada.shape[2]
    n_dnc = dn_conv_w.shape[2]
    n_ffc = ffn_conv_w.shape[2]

    small = jnp.concatenate([c, dn_conv_w[0].reshape(1, -1), ffn_conv_w[0].reshape(1, -1)], axis=1)
    small = _pad_lanes(small, -(-small.shape[1] // LANES) * LANES)
    small_g, w_in_g = _all_gather([small, w_in[0].astype(bf16)], "gather_w_in")
    later = [w_proj_sb[0].astype(bf16), w_proj_dn[0].astype(bf16), w_out[0].astype(bf16),
             w_ffn_in[0].astype(bf16), w_ffn_out[0].astype(bf16)]
    gather_later = _SideComm(_gather_protocol, later, _gathered_shapes(later))
    small_g = small_g[:, 0, :]
    c_all = small_g[:, :d]
    dn_cw = _cols_from_devices(small_g[:, d:d + DN_CONV_WIDTH * n_dnc].reshape(N_DEV, DN_CONV_WIDTH, n_dnc))
    o2 = d + DN_CONV_WIDTH * n_dnc
    ffn_cw = _cols_from_devices(small_g[:, o2:o2 + FFN_CONV_WIDTH * n_ffc].reshape(N_DEV, FFN_CONV_WIDTH, n_ffc))

    w_in_full = _cols_from_devices(w_in_g)
    r_sb, r_dn, r_z = 3 * SB_WIDTH, 3 * SB_WIDTH + DN_CONV_CH, 3 * SB_WIDTH + DN_CONV_CH + DN_V_WIDTH
    r_g = r_z + 2 * DN_HEADS
    w_main = jnp.concatenate([w_in_full[:, r_g:], w_in_full[:, r_sb:r_dn], w_in_full[:, r_dn:r_z],
                              w_in_full[:, :r_sb]], axis=1)
    w_ba = _pad_lanes(w_in_full[:, r_z:r_g], LANES)

    b_loc = lax.dynamic_slice(b_ada, (0, me * n_ada), (1, n_ada))
    mod_part = _ada_fwd(c_all, w_ada[0], b_loc)
    (mod_g,) = _all_gather([mod_part], "gather_mod")
    mod = lax.dynamic_index_in_dim(mod_g, me, axis=1, keepdims=False).reshape(1, N_DEV * n_ada)
    shift1, scale1, gate1, shift2, scale2, gate2 = [mod[:, i * d:(i + 1) * d] for i in range(6)]

    logit_lanes = ((0, 0), (GDN_LOGIT_LANE, LANES - GDN_LOGIT_LANE - DN_HEADS))
    a_log = jnp.pad(dn_A_log, logit_lanes)
    dt_b = jnp.pad(dn_dt_bias, logit_lanes)

    (h,) = _stage_fwd(_f_normmod, [norm1_w, shift1, scale1], [xs], [bf16], "norm1_fwd")
    proj = _mm(h, w_main, name="in_proj")
    ba = _mm(h, w_ba, name="in_proj_ba")
    k16, k0_16, k1_16, v16, v0_16, v1_16 = _sb_prepare(proj)
    o_a, sb_runs, w_psb_g, w_pdn_g, w_out_g, w_fin_g, w_fout_g = _sb_attention_fwd2(
        proj, k16, v0_16, v1_16, side=gather_later)
    w_psb = _cols_from_devices(w_psb_g)
    w_pdn = w_pdn_g.reshape(DN_V_WIDTH, d)
    w_o = w_out_g.reshape(d, d)
    w_fin = _cols_from_devices(w_fin_g)
    w_fout = w_fout_g.reshape(D_FF, d)
    qkv_act = _dn_conv_fwd(proj, dn_cw)
    o_b, states, dn_inverses = _gdn_fwd(a_log, dt_b, dn_norm_w, qkv_act, proj, ba)
    pa = _mm(o_a, w_psb, name="proj_sb")
    pb = _mm(o_b, w_pdn, name="proj_dn")
    gates = [(proj, d, OFF_GA // d), (proj, d, OFF_GB // d)]
    (merged,) = _stage_fwd(_f_merge, [], gates + [pa, pb], [bf16], "merge_fwd")
    ao = _mm(merged, w_o, name="out_proj")
    mid_params = [gate1, norm2_w, shift2, scale2]
    x1, h2 = _stage_fwd(_f_residual_normmod, mid_params, [xs, ao], [f32, bf16], "resid1_norm2_fwd")
    u_pre = _mm(h2, w_fin, name="ffn_in")
    act, u_conv = _ffn_conv_fwd(u_pre, ffn_cw, ffn_conv_b)
    fo = _mm(act, w_fout, name="ffn_out")

    loss_p, d_gate2, d_wf, dx2, dfo = _loss_and_grads(gate2, final_norm_w.reshape(1, d), x1, fo, target)
    dact = _mm(dfo, w_fout, tb=True, name="ffn_out_dx")
    g_w_fout = _mm(act, dfo, ta=True, name="ffn_out_dw")
    du, dbg, dbu = _ffn_conv_bwd_act(u_conv, dact)
    du_pre, d_ffn_cw = _conv_bwd(du, u_pre, 0, ffn_cw, "ffn_conv_bwd")
    dh2 = _mm(du_pre, w_fin, tb=True, name="ffn_in_dx")
    g_w_fin = _mm(h2, du_pre, ta=True, name="ffn_in_dw")
    (d_gate1, d_n2w, d_shift2, d_scale2), (dx1, dao) = _stage_bwd(
        _f_residual_normmod, mid_params, [xs, ao], [dx2, dh2], [f32, bf16], "resid1_norm2_bwd")
    dmerged = _mm(dao, w_o, tb=True, name="out_proj_dx")
    g_w_o = _mm(merged, dao, ta=True, name="out_proj_dw")
    _, (dga, dgb, dpa, dpb) = _stage_bwd(_f_merge, [], gates + [pa, pb], [dmerged], [bf16] * 4, "merge_bwd")
    do_a = _mm(dpa, w_psb, tb=True, name="proj_sb_dx")
    g_w_psb = _mm(o_a, dpa, ta=True, name="proj_sb_dw")
    do_b = _mm(dpb, w_pdn, tb=True, name="proj_dn_dx")
    g_w_pdn = _mm(o_b, dpb, ta=True, name="proj_dn_dw")
    early = [_cols_by_device(g_w_psb).astype(bf16),
             g_w_pdn.reshape(N_DEV, DN_V_WIDTH // N_DEV, d).astype(bf16),
             g_w_o.reshape(N_DEV, d // N_DEV, d).astype(bf16),
             _cols_by_device(g_w_fin).astype(bf16),
             g_w_fout.reshape(N_DEV, D_FF // N_DEV, d).astype(bf16)]
    exchange_early = _SideComm(_exchange_protocol, early, [jax.ShapeDtypeStruct(a.shape, a.dtype) for a in early])
    dq, dk, dv, *recv_early = _sb_attention_bwd2(proj, k16, k0_16, k1_16, v16, sb_runs, do_a, side=exchange_early)
    d_alog, d_dtb, d_dnw, dqkv_act, dz, dba = _gdn_bwd(a_log, dt_b, dn_norm_w, qkv_act, proj, ba, states,
                                                       dn_inverses, do_b)
    d_conv_out = _dn_conv_bwd_act(proj, dn_cw, dqkv_act)
    d_dn_pre, d_dn_cw = _conv_bwd(d_conv_out, proj, OFF_DN // TCONV_C, dn_cw, "dn_conv_bwd")
    dproj = jnp.concatenate([dga, dgb, d_dn_pre, dz, dq, dk.astype(bf16), dv.astype(bf16)], axis=1)
    g_w_main = _mm(h, dproj, ta=True, name="in_proj_dw")
    g_w_ba = _mm(h, dba, ta=True, name="in_proj_ba_dw")
    g_w_in_full = jnp.concatenate([g_w_main[:, OFF_SBQ:], g_w_main[:, OFF_DN:OFF_Z], g_w_main[:, OFF_Z:OFF_SBQ],
                                   g_w_ba[:, :2 * DN_HEADS], g_w_main[:, :OFF_DN]], axis=1)
    w_in_parts = _cols_by_device(g_w_in_full).astype(bf16)
    exchange_w_in = _SideComm(_exchange_protocol, [w_in_parts], [jax.ShapeDtypeStruct(w_in_parts.shape, bf16)])
    dh, recv_w_in = _mm(dproj, w_main, tb=True, name="in_proj_dx", side=exchange_w_in)
    dh_ba = _mm(dba, w_ba, tb=True, name="in_proj_ba_dx")
    (d_n1w, d_shift1, d_scale1), (grad_x,) = _stage_bwd(
        _f_normmod, [norm1_w, shift1, scale1], [xs], [[dh, dh_ba]], [f32], "norm1_bwd", residual=(0, dx1))

    dmod = jnp.concatenate([d_shift1, d_scale1, d_gate1, d_shift2, d_scale2, d_gate2], axis=1)
    d_ffn_cb = jnp.concatenate([dbg, dbu], axis=1)
    small_parts = jnp.concatenate(
        [loss_p, dmod, d_n1w, d_alog, d_dtb, d_dnw, d_n2w, d_ffn_cb, d_wf,
         d_dn_cw.reshape(1, -1), d_ffn_cw.reshape(1, -1)], axis=1)
    (small_parts_g,) = _all_gather([small_parts], "gather_small_grads")
    tot = _sum_devices(small_parts_g[:, 0, :])
    offs = {}
    pos = 0
    for nm, width in (("loss", LANES), ("b_ada", 6 * d), ("norm1_w", d), ("dn_A_log", LANES), ("dn_dt_bias", LANES),
                      ("dn_norm_w", LANES), ("norm2_w", d), ("ffn_conv_b", 2 * D_FF), ("final_norm_w", d),
                      ("dn_conv_w", DN_CONV_WIDTH * DN_CONV_CH), ("ffn_conv_w", FFN_CONV_WIDTH * 2 * D_FF)):
        offs[nm] = (pos, width)
        pos += width
    seg = lambda nm: tot[:, offs[nm][0]:offs[nm][0] + offs[nm][1]]
    loss = tot[0, 0]
    g_b_ada = seg("b_ada")
    g_norm1 = seg("norm1_w")
    g_alog = seg("dn_A_log")[:, GDN_LOGIT_LANE:GDN_LOGIT_LANE + DN_HEADS]
    g_dtb = seg("dn_dt_bias")[:, GDN_LOGIT_LANE:GDN_LOGIT_LANE + DN_HEADS]
    g_dnw = seg("dn_norm_w")
    g_norm2 = seg("norm2_w")
    g_ffn_cb = seg("ffn_conv_b")
    g_fnw = seg("final_norm_w")
    g_dn_cw = lax.dynamic_slice(seg("dn_conv_w").reshape(DN_CONV_WIDTH, DN_CONV_CH), (0, me * n_dnc),
                                (DN_CONV_WIDTH, n_dnc))
    g_ffn_cw = lax.dynamic_slice(seg("ffn_conv_w").reshape(FFN_CONV_WIDTH, 2 * D_FF), (0, me * n_ffc),
                                 (FFN_CONV_WIDTH, n_ffc))

    dmod_all = small_parts_g[:, 0, offs["b_ada"][0]:offs["b_ada"][0] + 6 * d]
    g_w_ada = _ada_bwd(c_all, lax.dynamic_slice(dmod_all, (0, me * n_ada), (N_DEV, n_ada)))

    def pack(parts):
        flat = [p.reshape(1, -1) for p in parts]
        flat = [_pad_lanes(p, -(-p.shape[1] // LANES) * LANES) for p in flat]
        return jnp.concatenate(flat, axis=1), [p.shape[1] for p in flat]

    small_names_g = [g_b_ada, g_norm1, g_alog, g_dtb, g_dnw, g_norm2, g_ffn_cb, g_fnw, g_dn_cw, g_ffn_cw]
    small_w = [b_ada, norm1_w, dn_A_log, dn_dt_bias, dn_norm_w, norm2_w, ffn_conv_b, final_norm_w, dn_conv_w[0], ffn_conv_w[0]]
    small_m = [m_b_ada, m_norm1_w, m_dn_A_log, m_dn_dt_bias, m_dn_norm_w, m_norm2_w, m_ffn_conv_b, m_final_norm_w, m_dn_conv_w[0], m_ffn_conv_w[0]]
    small_v = [v_b_ada, v_norm1_w, v_dn_A_log, v_dn_dt_bias, v_dn_norm_w, v_norm2_w, v_ffn_conv_b, v_final_norm_w, v_dn_conv_w[0], v_ffn_conv_w[0]]
    pg, widths = pack(small_names_g)
    pw, _ = pack(small_w)
    pm, _ = pack(small_m)
    pv, _ = pack(small_v)
    s_delta, s_m, s_v = _adamw(pw, pg, pm, pv, "adamw_small")

    def unpack(flat):
        out, pos = [], 0
        for ref_arr, width in zip(small_w, widths):
            out.append(flat[:, pos:pos + ref_arr.size].reshape(ref_arr.shape))
            pos += width
        return out

    small_grads = [g.reshape(w_.shape) for g, w_ in zip(small_names_g, small_w)]
    small_delta, small_newm, small_newv = unpack(s_delta), unpack(s_m), unpack(s_v)

    ada_delta, ada_m, ada_v = _adamw(w_ada[0], g_w_ada, m_w_ada[0], v_w_ada[0], "adamw_ada")

    recv = [recv_w_in] + list(recv_early)
    big = {}
    for nm, parts, w_, m_, v_ in (("w_in", recv[0], w_in, m_w_in, v_w_in),
                                  ("w_proj_sb", recv[1], w_proj_sb, m_w_proj_sb, v_w_proj_sb),
                                  ("w_proj_dn", recv[2], w_proj_dn, m_w_proj_dn, v_w_proj_dn),
                                  ("w_out", recv[3], w_out, m_w_out, v_w_out),
                                  ("w_ffn_in", recv[4], w_ffn_in, m_w_ffn_in, v_w_ffn_in),
                                  ("w_ffn_out", recv[5], w_ffn_out, m_w_ffn_out, v_w_ffn_out)):
        big[nm] = [t[None] for t in _sum_adamw(parts, w_[0], m_[0], v_[0], "adamw_" + nm)]

    sg = dict(zip(["b_ada", "norm1_w", "dn_A_log", "dn_dt_bias", "dn_norm_w", "norm2_w", "ffn_conv_b", "final_norm_w",
                   "dn_conv_w", "ffn_conv_w"], range(10)))

    def small_out(table, nm):
        val = table[sg[nm]]
        return val[None] if nm in ("dn_conv_w", "ffn_conv_w") else val

    order = ["w_ada", "b_ada", "norm1_w", "w_in", "dn_conv_w", "dn_A_log", "dn_dt_bias", "dn_norm_w", "w_proj_sb",
             "w_proj_dn", "w_out", "norm2_w", "w_ffn_in", "ffn_conv_w", "ffn_conv_b", "w_ffn_out", "final_norm_w"]
    groups = []
    for k, small_table in enumerate((small_grads, small_delta, small_newm, small_newv)):
        row = []
        for nm in order:
            if nm == "w_ada":
                row.append((g_w_ada, ada_delta, ada_m, ada_v)[k][None])
            elif nm in big:
                row.append(big[nm][k])
            else:
                row.append(small_out(small_table, nm))
        groups.append(row)
    return (loss, grad_x[None], *groups[0], *groups[1], *groups[2], *groups[3])
```

```python
import functools

import jax
import jax.numpy as jnp
from jax import lax
from jax.experimental import pallas as pl
from jax.experimental.pallas import tpu as pltpu

f32 = jnp.float32
bf16 = jnp.bfloat16

D_MODEL = 1024
SB_HEADS = 8
SB_HEAD_DIM = 64
SB_WIDTH = SB_HEADS * SB_HEAD_DIM
SB_QBLOCK = 128
DN_HEADS = 8
DN_KEY_DIM = 64
DN_VAL_DIM = 128
DN_QK_WIDTH = DN_HEADS * DN_KEY_DIM
DN_V_WIDTH = DN_HEADS * DN_VAL_DIM
DN_CONV_CH = 2 * DN_QK_WIDTH + DN_V_WIDTH
DN_CONV_WIDTH = 4
DN_CHUNK = 64
D_FF = 2816
FFN_CONV_WIDTH = 3
NORM_EPS = 1e-6
L2_EPS = 1e-6
ADAM_LR = 0.001
ADAM_B1 = 0.9
ADAM_B2 = 0.999
ADAM_EPS = 1e-08
ADAM_WD = 0.01
ADAM_STEP = 10

N_DEV = 8
MESH = pl.DeviceIdType.MESH

LANES = 128
SUBLANES = 8
VMEM_LIMIT = 48 * 1024 * 1024

OFF_GA = 0
OFF_GB = D_MODEL
OFF_DN = 2 * D_MODEL
OFF_Z = OFF_DN + DN_CONV_CH
OFF_SBQ = OFF_Z + DN_V_WIDTH
OFF_SBK = OFF_SBQ + SB_WIDTH
OFF_SBV = OFF_SBK + SB_WIDTH
MAIN_WIDTH = OFF_SBV + SB_WIDTH

TM = 512
TCONV_R = 512
TCONV_C = 1024
TCONV_FF = D_FF // 2
SB_PAIRS_FWD = 4
SB_PAIRS_BWD = 4
SB_DEAD = -106.0
SB_NEVER = -1e30


def _cparams(sem=None):
    return pltpu.CompilerParams(dimension_semantics=sem, vmem_limit_bytes=VMEM_LIMIT)


def _pick(n, cands):
    for c in cands:
        if n % c == 0:
            return c
    return n


def _my_pos():
    return lax.axis_index("x"), lax.axis_index("y"), lax.axis_index("c")


def _flip(v, bit):
    return 1 - v if bit else v


def _comm_scratch(n):
    return [pltpu.SemaphoreType.DMA((n, 7)), pltpu.SemaphoreType.DMA((n, 7)), pltpu.SemaphoreType.DMA((n,))]


def _gather_protocol(ins, outs, send_sems, recv_sems, local_sems):
    n = len(ins)
    x, y, c = _my_pos()
    me, sibling = (x, y, c), (x, y, 1 - c)
    chips = [(1 - x, y), (x, 1 - y), (1 - x, 1 - y)]

    def slot(out, pos):
        return out.at[4 * pos[0] + 2 * pos[1] + pos[2]]

    def copy(a, k, block, to, src=None):
        return pltpu.make_async_remote_copy(
            src_ref=slot(outs[a], block) if src is None else src, dst_ref=slot(outs[a], block),
            send_sem=send_sems.at[a, k], recv_sem=recv_sems.at[a, k], device_id=to, device_id_type=MESH)

    def local(a):
        return pltpu.make_async_copy(ins[a], slot(outs[a], me), local_sems.at[a])

    def first(a):
        return [copy(a, 0, me, sibling, src=ins[a])] + [copy(a, 1 + j, me, (*chip, c), src=ins[a])
                                                         for j, chip in enumerate(chips)]

    def start():
        for a in range(n):
            local(a).start()
            for cp in first(a):
                cp.start()

    def finish():
        forwards = []
        for a in range(n):
            for j, chip in enumerate(chips):
                copy(a, 1 + j, (*chip, c), me).wait_recv()
                fwd = copy(a, 4 + j, (*chip, c), sibling)
                fwd.start()
                forwards.append(fwd)
        for a in range(n):
            copy(a, 0, sibling, me).wait_recv()
            for j, chip in enumerate(chips):
                copy(a, 4 + j, (*chip, 1 - c), me).wait_recv()
        for a in range(n):
            for cp in first(a):
                cp.wait_send()
        for cp in forwards:
            cp.wait_send()
        for a in range(n):
            local(a).wait()

    return start, finish


def _exchange_protocol(ins, outs, send_sems, recv_sems, local_sems):
    n = len(ins)
    x, y, c = _my_pos()
    me_idx = 4 * x + 2 * y + c

    def local(a):
        return pltpu.make_async_copy(ins[a].at[me_idx], outs[a].at[me_idx], local_sems.at[a])

    def copies(a, m):
        peer = (_flip(x, m & 4), _flip(y, m & 2), _flip(c, m & 1))
        peer_idx = 4 * peer[0] + 2 * peer[1] + peer[2]
        sems = dict(send_sem=send_sems.at[a, m - 1], recv_sem=recv_sems.at[a, m - 1], device_id=peer,
                    device_id_type=MESH)
        send = pltpu.make_async_remote_copy(src_ref=ins[a].at[peer_idx], dst_ref=outs[a].at[me_idx], **sems)
        recv = pltpu.make_async_remote_copy(src_ref=ins[a].at[peer_idx], dst_ref=outs[a].at[peer_idx], **sems)
        return send, recv

    def start():
        for a in range(n):
            local(a).start()
            for m in range(1, N_DEV):
                copies(a, m)[0].start()

    def finish():
        for a in range(n):
            for m in range(1, N_DEV):
                copies(a, m)[1].wait_recv()
        for a in range(n):
            for m in range(1, N_DEV):
                copies(a, m)[0].wait_send()
            local(a).wait()

    return start, finish


def _collective_call(protocol, arrs, out_shapes, name):
    n = len(arrs)

    def body(*refs):
        start, finish = protocol(refs[:n], refs[n:2 * n], *refs[2 * n:])
        start()
        finish()

    any_spec = pl.BlockSpec(memory_space=pl.ANY)
    return pl.pallas_call(body, name=name, out_shape=out_shapes, in_specs=[any_spec] * n, out_specs=[any_spec] * n,
                          scratch_shapes=_comm_scratch(n))(*arrs)


def _gathered_shapes(arrs):
    return [jax.ShapeDtypeStruct((N_DEV,) + a.shape, a.dtype) for a in arrs]


def _all_gather(arrs, name):
    return _collective_call(_gather_protocol, arrs, _gathered_shapes(arrs), name)


MM_BLOCK_BYTES = 7 * 1024 * 1024
MM_TILE_CAP = 1664


def _lane_tile(n, cap):
    fits = [t for t in range(LANES, min(n, cap) + 1, LANES) if n % t == 0]
    return max(fits) if fits else n


def _mm_tiles(m_dim, n_dim, k_dim, a_bytes, b_bytes):
    tm = _lane_tile(m_dim, MM_TILE_CAP)
    tn = _lane_tile(n_dim, MM_TILE_CAP)
    while tm * tn * 4 > MM_BLOCK_BYTES:
        if tn >= tm and (tn // 2) % LANES == 0:
            tn //= 2
        else:
            tm //= 2
    if (m_dim % (2 * tm) == 0 and 2 * tm * k_dim * a_bytes <= MM_BLOCK_BYTES
            and 2 * tm * tn * 4 <= MM_BLOCK_BYTES):
        tm *= 2
    tk = k_dim
    if k_dim % LANES == 0:
        units = k_dim // LANES
        fits = [u for u in range(1, units + 1) if units % u == 0
                and u * LANES * max(tm * a_bytes, tn * b_bytes) <= MM_BLOCK_BYTES]
        tk = max(fits) * LANES
    return tm, tn, tk


def _mm(a, b, *, ta=False, tb=False, name, side=None):
    (k_dim, m_dim) = a.shape if ta else a.shape[::-1]
    (n_dim, kb_dim) = b.shape if tb else b.shape[::-1]
    assert k_dim == kb_dim, (a.shape, b.shape, ta, tb)
    tm, tn, tk = _mm_tiles(m_dim, n_dim, k_dim, a.dtype.itemsize, b.dtype.itemsize)
    nk = k_dim // tk
    grid = (m_dim // tm, n_dim // tn, nk)
    dims = (((0 if ta else 1,), (1 if tb else 0,)), ((), ()))
    ns = side.n if side else 0

    def body(a_ref, b_ref, *rest):
        if side:
            side.run(rest[:ns], rest[ns + 1:2 * ns + 1], rest[2 * ns + 1:], *_grid_ends(grid),
                     lambda: compute(a_ref, b_ref, rest[ns]))
        else:
            compute(a_ref, b_ref, rest[0])

    def compute(a_ref, b_ref, o_ref):
        part = lax.dot_general(a_ref[...].astype(bf16), b_ref[...].astype(bf16), dims, preferred_element_type=f32)
        if nk == 1:
            o_ref[...] = part
        else:
            k = pl.program_id(2)

            @pl.when(k == 0)
            def _():
                o_ref[...] = part

            @pl.when(k > 0)
            def _():
                o_ref[...] += part

    a_spec = pl.BlockSpec((tk, tm), lambda i, j, k: (k, i)) if ta else pl.BlockSpec((tm, tk), lambda i, j, k: (i, k))
    b_spec = pl.BlockSpec((tn, tk), lambda i, j, k: (j, k)) if tb else pl.BlockSpec((tk, tn), lambda i, j, k: (k, j))
    out_spec = pl.BlockSpec((tm, tn), lambda i, j, k: (i, j))
    out_shape = jax.ShapeDtypeStruct((m_dim, n_dim), f32)
    if not side:
        return pl.pallas_call(body, name=name, grid=grid, in_specs=[a_spec, b_spec], out_specs=out_spec,
                              out_shape=out_shape,
                              compiler_params=_cparams(("parallel", "parallel", "arbitrary")))(a, b)
    return pl.pallas_call(
        body, name=name, grid=grid, in_specs=[a_spec, b_spec] + side.specs(), out_specs=[out_spec] + side.specs(),
        out_shape=[out_shape] + side.out_shapes, scratch_shapes=_comm_scratch(ns),
        compiler_params=_cparams(("arbitrary", "arbitrary", "arbitrary")))(a, b, *side.arrs)


def _win(t):
    return t if isinstance(t, tuple) else (t, t.shape[1], 0)


def _tile_spec(width, cb, tm):
    return pl.BlockSpec((tm, width), lambda i: (i, cb))


def _param_spec(p):
    return pl.BlockSpec(p.shape, lambda i: (0, 0))


def _stage_fwd(f, params, tiles, out_dtypes, name):
    tiles = [_win(t) for t in tiles]
    rows = tiles[0][0].shape[0]
    tm = min(TM, rows)
    avals = jax.eval_shape(f, *[jax.ShapeDtypeStruct(p.shape, f32) for p in params],
                           *[jax.ShapeDtypeStruct((tm, w), f32) for _, w, _ in tiles])
    n_p, n_t = len(params), len(tiles)

    def body(*refs):
        p = [r[...] for r in refs[:n_p]]
        t = [r[...].astype(f32) for r in refs[n_p:n_p + n_t]]
        for o_ref, val in zip(refs[n_p + n_t:], f(*p, *t)):
            o_ref[...] = val.astype(o_ref.dtype)

    return pl.pallas_call(
        body, name=name, grid=(rows // tm,),
        in_specs=[_param_spec(p) for p in params] + [_tile_spec(w, cb, tm) for _, w, cb in tiles],
        out_specs=[_tile_spec(a.shape[1], 0, tm) for a in avals],
        out_shape=[jax.ShapeDtypeStruct((rows, a.shape[1]), dt) for a, dt in zip(avals, out_dtypes)],
        compiler_params=_cparams(("parallel",)),
    )(*params, *[t[0] for t in tiles])


def _stage_bwd(f, params, tiles, cts, grad_dtypes, name, residual=None):
    tiles = [_win(t) for t in tiles]
    rows = tiles[0][0].shape[0]
    tm = min(TM, rows)
    cts = [list(g) if isinstance(g, (list, tuple)) else [g] for g in cts]
    flat_cts = [a for g in cts for a in g]
    n_p, n_t, n_c = len(params), len(tiles), len(flat_cts)
    has_res = residual is not None
    want = [j for j, dt in enumerate(grad_dtypes) if dt is not None]

    def body(*refs):
        i = pl.program_id(0)
        p = [r[...] for r in refs[:n_p]]
        t = [r[...].astype(f32) for r in refs[n_p:n_p + n_t]]
        ct_vals = [r[...].astype(f32) for r in refs[n_p + n_t:n_p + n_t + n_c]]
        ct, at = [], 0
        for g in cts:
            ct.append(functools.reduce(jnp.add, ct_vals[at:at + len(g)]))
            at += len(g)
        ct = tuple(ct)
        pos = n_p + n_t + n_c
        res_ref = refs[pos] if has_res else None
        pos += 1 if has_res else 0
        dp_refs = refs[pos:pos + n_p]
        dt_refs = refs[pos + n_p:]
        _, vjp = jax.vjp(f, *p, *t)
        grads = vjp(ct)

        @pl.when(i == 0)
        def _():
            for r in dp_refs:
                r[...] = jnp.zeros_like(r)

        for r, g in zip(dp_refs, grads[:n_p]):
            r[...] += g
        for r, j in zip(dt_refs, want):
            g = grads[n_p + j]
            if has_res and j == residual[0]:
                g = g + res_ref[...].astype(f32)
            r[...] = g.astype(r.dtype)

    in_arrays = list(params) + [t[0] for t in tiles] + flat_cts
    in_specs = ([_param_spec(p) for p in params] + [_tile_spec(w, cb, tm) for _, w, cb in tiles]
                + [_tile_spec(c.shape[1], 0, tm) for c in flat_cts])
    if has_res:
        in_arrays.append(residual[1])
        in_specs.append(_tile_spec(residual[1].shape[1], 0, tm))
    out_shape = ([jax.ShapeDtypeStruct(p.shape, f32) for p in params]
                 + [jax.ShapeDtypeStruct((rows, tiles[j][1]), grad_dtypes[j]) for j in want])
    out_specs = [_param_spec(p) for p in params] + [_tile_spec(tiles[j][1], 0, tm) for j in want]
    outs = pl.pallas_call(
        body, name=name, grid=(rows // tm,), in_specs=in_specs, out_specs=out_specs, out_shape=out_shape,
        compiler_params=_cparams(("arbitrary",)),
    )(*in_arrays)
    return outs[:n_p], outs[n_p:]


def _rms(x, w):
    return x * lax.rsqrt(jnp.mean(x * x, axis=-1, keepdims=True) + NORM_EPS) * w


def _f_normmod(w, shift, scale, x):
    return (_rms(x, w) * (1.0 + scale) + shift,)


def _f_merge(ga, gb, pa, pb):
    return (jax.nn.sigmoid(ga) * pa + jax.nn.sigmoid(gb) * pb,)


def _f_residual_normmod(gate, w, shift, scale, x, branch):
    x1 = x + gate * branch
    return x1, _rms(x1, w) * (1.0 + scale) + shift


def _f_loss(gate, wf, x1, fo, target):
    y = _rms(x1 + gate * fo, wf)
    err = jnp.square(y - target)
    return (0.5 * jnp.sum(jnp.mean(err, axis=-1, keepdims=True), axis=0, keepdims=True),)


def _loss_and_grads(gate2, wf, x1, fo, target):
    rows, d = x1.shape
    tm = min(TM, rows)

    def body(g_ref, w_ref, x_ref, fo_ref, t_ref, loss_ref, dg_ref, dw_ref, dx_ref, dfo_ref):
        i = pl.program_id(0)
        (val,), vjp = jax.vjp(_f_loss, g_ref[...], w_ref[...], x_ref[...], fo_ref[...], t_ref[...])
        dg, dw, dx, dfo, _ = vjp((jnp.ones((1, 1), f32),))

        @pl.when(i == 0)
        def _():
            loss_ref[...] = jnp.zeros_like(loss_ref)
            dg_ref[...] = jnp.zeros_like(dg_ref)
            dw_ref[...] = jnp.zeros_like(dw_ref)

        loss_ref[...] += jnp.broadcast_to(val, loss_ref.shape)
        dg_ref[...] += dg
        dw_ref[...] += dw
        dx_ref[...] = dx
        dfo_ref[...] = dfo.astype(bf16)

    vec = pl.BlockSpec((1, d), lambda i: (0, 0))
    tile = pl.BlockSpec((tm, d), lambda i: (i, 0))
    return pl.pallas_call(
        body, name="loss_fwd_bwd", grid=(rows // tm,),
        in_specs=[vec, vec, tile, tile, tile],
        out_specs=[pl.BlockSpec((1, LANES), lambda i: (0, 0)), vec, vec, tile, tile],
        out_shape=[jax.ShapeDtypeStruct((1, LANES), f32), jax.ShapeDtypeStruct((1, d), f32),
                   jax.ShapeDtypeStruct((1, d), f32), jax.ShapeDtypeStruct((rows, d), f32),
                   jax.ShapeDtypeStruct((rows, d), bf16)],
        compiler_params=_cparams(("arbitrary",)),
    )(gate2, wf, x1, fo, target)


def _softplus(z):
    return jnp.maximum(z, 0.0) + jnp.log(1.0 + jnp.exp(-jnp.abs(z)))


def _split_dot(a, m):
    hi = a.astype(bf16)
    lo = (a - hi.astype(f32)).astype(bf16)
    return jnp.dot(hi, m, preferred_element_type=f32) + jnp.dot(lo, m, preferred_element_type=f32)


def _suffix_matrix(n):
    r = lax.broadcasted_iota(jnp.int32, (n, n), 0)
    c = lax.broadcasted_iota(jnp.int32, (n, n), 1)
    return (r > c).astype(bf16)


def _head_masks():
    lane = lax.broadcasted_iota(jnp.int32, (1, LANES), 1)
    return [(lane < SB_HEAD_DIM).astype(f32), (lane >= SB_HEAD_DIM).astype(f32)]


def _sb_prepare(proj):
    wins = [(proj, SB_WIDTH, OFF_SBK // SB_WIDTH), (proj, SB_WIDTH, OFF_SBV // SB_WIDTH)]
    return _stage_fwd(lambda k, v: (k, v), [], wins, [bf16] * 2, "sb_prepare")


def _by_head(rows_head0, rows_head1):
    m0, m1 = _head_masks()
    return rows_head0 * m0 + rows_head1 * m1


def _stack_heads(x):
    m0, m1 = _head_masks()
    return jnp.concatenate([x * m0, x * m1], axis=0)


def _sb_logits(qst, k, t_pos2, kb, bq, masked):
    z = lax.dot_general(qst, k, (((1,), (1,)), ((), ())), preferred_element_type=f32)
    l = -_softplus(z)
    if masked:
        s_pos = kb * bq + lax.broadcasted_iota(jnp.int32, (1, bq), 1)
        causal = s_pos < t_pos2
        l = jnp.where(causal, l, 0.0)
    else:
        causal = None
    return z, l, causal


class _SideComm:
    def __init__(self, protocol, arrs, out_shapes):
        self.protocol, self.arrs, self.out_shapes = protocol, list(arrs), list(out_shapes)
        self.n = len(self.arrs)

    def specs(self):
        return [pl.BlockSpec(memory_space=pl.ANY)] * self.n

    def run(self, in_refs, out_refs, sems, first, last, compute):
        start, finish = self.protocol(in_refs, out_refs, *sems)
        pl.when(first)(start)
        compute()
        pl.when(last)(finish)


def _grid_ends(grid):
    ids = [pl.program_id(axis) for axis in range(len(grid))]
    first = functools.reduce(jnp.logical_and, [i == 0 for i in ids])
    last = functools.reduce(jnp.logical_and, [i == g - 1 for i, g in zip(ids, grid)])
    return first, last


def _sb_attention_fwd2(proj, k16, v16, side=None):
    rows = proj.shape[0]
    bq = SB_QBLOCK
    nq = rows // bq
    assert nq <= LANES, "one lane per key block"
    npair = SB_WIDTH // LANES
    scale = SB_HEAD_DIM ** -0.5

    npp = SB_PAIRS_FWD
    wq = npp * LANES
    grid = (npair // npp, nq)
    ns = side.n if side else 0

    def body(q_ref, k_ref, v_ref, *rest):
        o_ref, runs_ref = rest[ns], rest[ns + 1]
        if side:
            side.run(rest[:ns], rest[ns + 2:2 * ns + 2], rest[2 * ns + 2:], *_grid_ends(grid),
                     lambda: compute(q_ref, k_ref, v_ref, o_ref, runs_ref))
        else:
            compute(q_ref, k_ref, v_ref, o_ref, runs_ref)

    def compute(q_ref, k_ref, v_ref, o_ref, runs_ref):
        qi = pl.program_id(1)
        pairs = [slice(pp * LANES, (pp + 1) * LANES) for pp in range(npp)]
        qst = [(_stack_heads(q_ref[:, s]) * scale).astype(bf16) for s in pairs]
        r = lax.broadcasted_iota(jnp.int32, (bq, 2 * bq), 0)
        c = lax.broadcasted_iota(jnp.int32, (bq, 2 * bq), 1)
        m2 = jnp.logical_or(r > c, c >= bq).astype(bf16)
        t_pos = qi * bq + lax.broadcasted_iota(jnp.int32, (bq, 1), 0)
        t_pos2 = jnp.concatenate([t_pos, t_pos], axis=0)
        lane = lax.broadcasted_iota(jnp.int32, (1, LANES), 1)
        runs_ref[...] = jnp.full(runs_ref.shape, SB_NEVER, f32)

        def tiles(kbs, carry, masked):
            jobs = [(pp, kb) for kb in kbs for pp in range(npp)]
            rows_k = [pl.ds(pl.multiple_of(kb * bq, bq), bq) for _, kb in jobs]
            zl = [_sb_logits(qst[pp], k_ref[rk, pairs[pp]], t_pos2, kb, bq, masked) for (pp, kb), rk in zip(jobs, rows_k)]
            cs = [_split_dot(l, m2) for _, l, _ in zl]
            run = [cr[0] for cr in carry]
            acc = [cr[1] for cr in carry]
            probs = []
            for (pp, kb), (z, l, causal), cs2 in zip(jobs, zl, cs):
                a = jnp.exp(z + l + cs2[:, :bq] + run[pp])
                if masked:
                    a = jnp.where(causal, a, 0.0)
                probs.append(a.astype(bf16))
                for hh in range(2):
                    cols = slice((2 * pp + hh) * LANES, (2 * pp + hh + 1) * LANES)
                    runs_ref[:, cols] = jnp.where(lane == kb, run[pp][hh * bq:(hh + 1) * bq], runs_ref[:, cols])
                run[pp] = run[pp] + cs2[:, bq:]
            for (pp, kb), rk, ab in zip(jobs, rows_k, probs):
                v = v_ref[rk, pairs[pp]]
                acc[pp] = acc[pp] + _by_head(jnp.dot(ab[:bq], v, preferred_element_type=f32),
                                             jnp.dot(ab[bq:], v, preferred_element_type=f32))
            return tuple(zip(run, acc))

        zero = (jnp.zeros((2 * bq, bq), f32), jnp.zeros((bq, LANES), f32))
        carry = tiles([qi], (zero,) * npp, True)

        def alive(cr):
            return functools.reduce(jnp.maximum, [jnp.max(run) for run, _ in cr]) > SB_DEAD

        def two(state):
            i, _, cr = state
            cr = tiles([qi - 1 - 2 * i, qi - 2 - 2 * i], cr, False)
            return i + 1, alive(cr), cr

        n_two = qi // 2
        i_end, still, carry = lax.while_loop(lambda st: jnp.logical_and(st[0] < n_two, st[1]), two,
                                             (jnp.int32(0), alive(carry), carry))
        last_one = jnp.logical_and(qi % 2 == 1, jnp.logical_and(still, i_end == n_two))
        carry = lax.cond(last_one, lambda cr: tiles([0], cr, False), lambda cr: cr, carry)
        for pp in range(npp):
            o_ref[:, pairs[pp]] = carry[pp][1]

    kv = pl.BlockSpec((rows, wq), lambda p, i: (0, p))
    return pl.pallas_call(
        body, name="sb_attn_fwd", grid=grid,
        in_specs=[pl.BlockSpec((bq, wq), lambda p, i: (i, OFF_SBQ // wq + p)), kv, kv] + (side.specs() if side else []),
        out_specs=[pl.BlockSpec((bq, wq), lambda p, i: (i, p)),
                   pl.BlockSpec((bq, 2 * wq), lambda p, i: (i, p))] + (side.specs() if side else []),
        out_shape=[jax.ShapeDtypeStruct((rows, SB_WIDTH), f32),
                   jax.ShapeDtypeStruct((rows, SB_HEADS * LANES), f32)] + (side.out_shapes if side else []),
        scratch_shapes=_comm_scratch(ns) if side else [],
        compiler_params=_cparams(("arbitrary", "arbitrary")),
    )(proj, k16, v16, *(side.arrs if side else []))


def _sb_attention_bwd2(proj, k16, v16, runs, do, side=None):
    rows = proj.shape[0]
    bq = SB_QBLOCK
    nq = rows // bq
    npair = SB_WIDTH // LANES
    scale = SB_HEAD_DIM ** -0.5
    tn = (((0,), (0,)), ((), ()))
    nt = (((1,), (1,)), ((), ()))

    npp = SB_PAIRS_BWD
    wq = npp * LANES
    grid = (npair // npp, nq)
    ns = side.n if side else 0

    def body(q_ref, k_ref, v_ref, runs_ref, do_ref, *rest):
        outs = rest[ns:ns + 3]
        ins = (q_ref, k_ref, v_ref, runs_ref, do_ref)
        if side:
            side.run(rest[:ns], rest[ns + 3:2 * ns + 3], rest[2 * ns + 3:], *_grid_ends(grid),
                     lambda: compute(*ins, *outs))
        else:
            compute(*ins, *outs)

    def compute(q_ref, k_ref, v_ref, runs_ref, do_ref, dq_ref, dk_ref, dv_ref):
        qi = pl.program_id(1)

        @pl.when(qi == 0)
        def _():
            dk_ref[...] = jnp.zeros_like(dk_ref)
            dv_ref[...] = jnp.zeros_like(dv_ref)

        pairs = [slice(pp * LANES, (pp + 1) * LANES) for pp in range(npp)]
        qst = [(_stack_heads(q_ref[:, s]) * scale).astype(bf16) for s in pairs]
        dost = [_stack_heads(do_ref[:, s]).astype(bf16) for s in pairs]
        runs = [jnp.concatenate([runs_ref[:, 2 * pp * LANES:(2 * pp + 1) * LANES],
                                 runs_ref[:, (2 * pp + 1) * LANES:(2 * pp + 2) * LANES]], axis=0) for pp in range(npp)]
        r = lax.broadcasted_iota(jnp.int32, (bq, 2 * bq), 0)
        c = lax.broadcasted_iota(jnp.int32, (bq, 2 * bq), 1)
        suffix_m = _suffix_matrix(bq)
        m2 = jnp.logical_or(r < c, c >= bq).astype(bf16)
        t_pos = qi * bq + lax.broadcasted_iota(jnp.int32, (bq, 1), 0)
        t_pos2 = jnp.concatenate([t_pos, t_pos], axis=0)
        lane = lax.broadcasted_iota(jnp.int32, (1, LANES), 1)

        def tiles(kbs, carry, masked):
            jobs = [(pp, kb) for kb in kbs for pp in range(npp)]
            rows_k = [pl.ds(pl.multiple_of(kb * bq, bq), bq) for _, kb in jobs]
            zl = [_sb_logits(qst[pp], k_ref[rk, pairs[pp]], t_pos2, kb, bq, masked) for (pp, kb), rk in zip(jobs, rows_k)]
            das = [lax.dot_general(dost[pp], v_ref[rk, pairs[pp]], nt, preferred_element_type=f32)
                   for (pp, kb), rk in zip(jobs, rows_k)]
            sticks = [_split_dot(l, suffix_m) for _, l, _ in zl]
            probs, ps = [], []
            for (pp, kb), (z, l, causal), stick, da in zip(jobs, zl, sticks, das):
                run = jnp.sum(jnp.where(lane == kb, runs[pp], 0.0), axis=1, keepdims=True)
                a = jnp.exp(z + l + stick + run)
                if masked:
                    a = jnp.where(causal, a, 0.0)
                probs.append(a.astype(bf16))
                ps.append(da * a)
            pcs = [_split_dot(p, m2) for p in ps]
            pref = [cr[0] for cr in carry]
            dq_acc = [cr[1] for cr in carry]
            dzs = []
            for (pp, kb), (z, l, causal), p, pc2 in zip(jobs, zl, ps, pcs):
                dz = p * jnp.exp(l) - jnp.exp(z + l) * (pc2[:, :bq] + pref[pp])
                if masked:
                    dz = jnp.where(causal, dz, 0.0)
                dzs.append(dz.astype(bf16))
                pref[pp] = pref[pp] + pc2[:, bq:]
            for (pp, kb), rk, dzb, ab in zip(jobs, rows_k, dzs, probs):
                cols = pairs[pp]
                k = k_ref[rk, cols]
                dq_acc[pp] = dq_acc[pp] + _by_head(jnp.dot(dzb[:bq], k, preferred_element_type=f32),
                                                   jnp.dot(dzb[bq:], k, preferred_element_type=f32))
                dk_ref[rk, cols] += lax.dot_general(dzb, qst[pp], tn, preferred_element_type=f32)
                dv_ref[rk, cols] += lax.dot_general(ab, dost[pp], tn, preferred_element_type=f32)
            return tuple(zip(pref, dq_acc))

        zero = (jnp.zeros((2 * bq, bq), f32), jnp.zeros((bq, LANES), f32))
        colmax = functools.reduce(jnp.maximum, [jnp.max(x, axis=0, keepdims=True) for x in runs])
        live = jnp.logical_and(colmax > SB_DEAD, lane < qi)
        kb0 = jnp.minimum(jnp.min(jnp.where(live, lane, LANES)), qi)
        n_blocks = qi - kb0
        carry = lax.fori_loop(0, n_blocks // 2, lambda i, cr: tiles([kb0 + 2 * i, kb0 + 2 * i + 1], cr, False),
                              (zero,) * npp)
        carry = lax.cond(n_blocks % 2 == 1, lambda cr: tiles([qi - 1], cr, False), lambda cr: cr, carry)
        carry = tiles([qi], carry, True)
        for pp in range(npp):
            dq_ref[:, pairs[pp]] = (carry[pp][1] * scale).astype(dq_ref.dtype)

    blk = pl.BlockSpec((bq, wq), lambda p, i: (i, p))
    full = pl.BlockSpec((rows, wq), lambda p, i: (0, p), pipeline_mode=pl.Buffered(1))
    return pl.pallas_call(
        body, name="sb_attn_bwd", grid=grid,
        in_specs=[pl.BlockSpec((bq, wq), lambda p, i: (i, OFF_SBQ // wq + p)), full, full,
                  pl.BlockSpec((bq, 2 * wq), lambda p, i: (i, p)), blk] + (side.specs() if side else []),
        out_specs=[blk, full, full] + (side.specs() if side else []),
        out_shape=[jax.ShapeDtypeStruct((rows, SB_WIDTH), bf16), jax.ShapeDtypeStruct((rows, SB_WIDTH), f32),
                   jax.ShapeDtypeStruct((rows, SB_WIDTH), f32)] + (side.out_shapes if side else []),
        scratch_shapes=_comm_scratch(ns) if side else [],
        compiler_params=_cparams(("arbitrary", "arbitrary")),
    )(proj, k16, v16, runs, do, *(side.arrs if side else []))


def _shift_down(x, prev8, j):
    if j == 0:
        return x
    r = pltpu.roll(x, j, axis=0)
    row8 = lax.broadcasted_iota(jnp.int32, prev8.shape, 0)
    head = jnp.where(row8 < j, pltpu.roll(prev8, j, axis=0), r[0:SUBLANES])
    return jnp.concatenate([head, r[SUBLANES:]], axis=0)


def _shift_up(x, next8, j):
    if j == 0:
        return x
    n = x.shape[0]
    r = pltpu.roll(x, n - j, axis=0)
    row8 = lax.broadcasted_iota(jnp.int32, next8.shape, 0)
    tail = jnp.where(row8 >= SUBLANES - j, pltpu.roll(next8, SUBLANES - j, axis=0), r[n - SUBLANES:n])
    return jnp.concatenate([r[:n - SUBLANES], tail], axis=0)


def _conv(x, prev8, w):
    k_taps = w.shape[0]
    out = x * w[k_taps - 1:k_taps, :]
    for j in range(1, k_taps):
        out = out + _shift_down(x, prev8, j) * w[k_taps - 1 - j:k_taps - j, :]
    return out


def _conv_tiles(rows):
    tr = min(TCONV_R, rows)
    return tr, rows // tr, tr // SUBLANES


def _prev_spec(tc, cb0, r8):
    return pl.BlockSpec((SUBLANES, tc), lambda j, i: (jnp.maximum(i * r8 - 1, 0), cb0 + j))


def _silu(x):
    return x * jax.nn.sigmoid(x)


def _dsilu(x):
    s = jax.nn.sigmoid(x)
    return s * (1.0 + x * (1.0 - s))


def _dn_conv_fwd(proj, w):
    rows = proj.shape[0]
    tr, nr, r8 = _conv_tiles(rows)
    tc = TCONV_C
    cb0 = OFF_DN // tc

    def body(x_ref, p_ref, w_ref, o_ref):
        prev = jnp.where(pl.program_id(1) == 0, 0.0, p_ref[...])
        o_ref[...] = _silu(_conv(x_ref[...], prev, w_ref[...]))

    return pl.pallas_call(
        body, name="dn_conv_fwd", grid=(DN_CONV_CH // tc, nr),
        in_specs=[pl.BlockSpec((tr, tc), lambda j, i: (i, cb0 + j)), _prev_spec(tc, cb0, r8),
                  pl.BlockSpec((DN_CONV_WIDTH, tc), lambda j, i: (0, j))],
        out_specs=pl.BlockSpec((tr, tc), lambda j, i: (i, j)),
        out_shape=jax.ShapeDtypeStruct((rows, DN_CONV_CH), f32),
        compiler_params=_cparams(("parallel", "parallel")),
    )(proj, proj, w)


def _dn_conv_bwd_act(proj, w, dact):
    rows = proj.shape[0]
    tr, nr, r8 = _conv_tiles(rows)
    tc = TCONV_C
    cb0 = OFF_DN // tc

    def body(x_ref, p_ref, w_ref, d_ref, o_ref):
        prev = jnp.where(pl.program_id(1) == 0, 0.0, p_ref[...])
        o_ref[...] = d_ref[...] * _dsilu(_conv(x_ref[...], prev, w_ref[...]))

    return pl.pallas_call(
        body, name="dn_conv_bwd_act", grid=(DN_CONV_CH // tc, nr),
        in_specs=[pl.BlockSpec((tr, tc), lambda j, i: (i, cb0 + j)), _prev_spec(tc, cb0, r8),
                  pl.BlockSpec((DN_CONV_WIDTH, tc), lambda j, i: (0, j)),
                  pl.BlockSpec((tr, tc), lambda j, i: (i, j))],
        out_specs=pl.BlockSpec((tr, tc), lambda j, i: (i, j)),
        out_shape=jax.ShapeDtypeStruct((rows, DN_CONV_CH), f32),
        compiler_params=_cparams(("parallel", "parallel")),
    )(proj, proj, w, dact)


def _ffn_conv_fwd(u_pre, w, b):
    rows = u_pre.shape[0]
    tr, nr, r8 = _conv_tiles(rows)
    tc = TCONV_FF
    nct = D_FF // tc

    def body(xg_ref, pg_ref, xu_ref, pu_ref, wg_ref, wu_ref, bg_ref, bu_ref, o_ref, u_ref):
        first = pl.program_id(1) == 0
        ug = _conv(xg_ref[...], jnp.where(first, 0.0, pg_ref[...]), wg_ref[...]) + bg_ref[...]
        uu = _conv(xu_ref[...], jnp.where(first, 0.0, pu_ref[...]), wu_ref[...]) + bu_ref[...]
        o_ref[...] = (_silu(ug) * uu).astype(o_ref.dtype)
        u_ref[0] = ug
        u_ref[1] = uu

    def x_spec(off):
        return pl.BlockSpec((tr, tc), lambda j, i: (i, off + j))

    def w_spec(k, off):
        return pl.BlockSpec((k, tc), lambda j, i: (0, off + j))

    return pl.pallas_call(
        body, name="ffn_conv_fwd", grid=(nct, nr),
        in_specs=[x_spec(0), _prev_spec(tc, 0, r8), x_spec(nct), _prev_spec(tc, nct, r8),
                  w_spec(FFN_CONV_WIDTH, 0), w_spec(FFN_CONV_WIDTH, nct), w_spec(1, 0), w_spec(1, nct)],
        out_specs=[pl.BlockSpec((tr, tc), lambda j, i: (i, j)), pl.BlockSpec((2, tr, tc), lambda j, i: (0, i, j))],
        out_shape=[jax.ShapeDtypeStruct((rows, D_FF), bf16), jax.ShapeDtypeStruct((2, rows, D_FF), f32)],
        compiler_params=_cparams(("parallel", "parallel")),
    )(u_pre, u_pre, u_pre, u_pre, w, w, b, b)


def _ffn_conv_bwd_act(u, dact):
    rows = dact.shape[0]
    tr, nr, _ = _conv_tiles(rows)
    tc = TCONV_FF
    nct = D_FF // tc

    def body(u_ref, d_ref, du_ref, dbg_ref, dbu_ref):
        first = pl.program_id(1) == 0
        ug = u_ref[0]
        uu = u_ref[1]
        d = d_ref[...]
        sig = jax.nn.sigmoid(ug)
        dug = d * uu * (sig * (1.0 + ug * (1.0 - sig)))
        duu = d * (ug * sig)
        du_ref[0] = dug
        du_ref[1] = duu

        @pl.when(first)
        def _():
            dbg_ref[...] = jnp.zeros_like(dbg_ref)
            dbu_ref[...] = jnp.zeros_like(dbu_ref)

        dbg_ref[...] += jnp.sum(dug, axis=0, keepdims=True)
        dbu_ref[...] += jnp.sum(duu, axis=0, keepdims=True)

    pair = pl.BlockSpec((2, tr, tc), lambda j, i: (0, i, j))
    vec = pl.BlockSpec((1, tc), lambda j, i: (0, j))
    return pl.pallas_call(
        body, name="ffn_conv_bwd_act", grid=(nct, nr),
        in_specs=[pair, pl.BlockSpec((tr, tc), lambda j, i: (i, j))],
        out_specs=[pair, vec, vec],
        out_shape=[jax.ShapeDtypeStruct((2, rows, D_FF), f32),
                   jax.ShapeDtypeStruct((1, D_FF), f32), jax.ShapeDtypeStruct((1, D_FF), f32)],
        compiler_params=_cparams(("parallel", "arbitrary")),
    )(u, dact)


def _conv_bwd(dy, x, x_cb0, w, name):
    k_taps = w.shape[0]
    split = dy.ndim == 3
    rows = dy.shape[-2]
    ch = dy.shape[-1] * (2 if split else 1)
    tc = TCONV_FF if split else TCONV_C
    tr, nr, r8 = _conv_tiles(rows)
    per_half = dy.shape[-1] // tc
    last8 = rows // SUBLANES - 1

    def body(dy_ref, nx_ref, x_ref, w_ref, dx_ref, dw_ref):
        i = pl.program_id(1)
        dyv = dy_ref[...]
        nxt = jnp.where(i == nr - 1, 0.0, nx_ref[...])
        xv = x_ref[...].astype(f32)
        wv = w_ref[...]

        @pl.when(i == 0)
        def _():
            dw_ref[...] = jnp.zeros_like(dw_ref)

        dx = dyv * wv[k_taps - 1:k_taps, :]
        dw_ref[k_taps - 1:k_taps, :] += jnp.sum(dyv * xv, axis=0, keepdims=True)
        for j in range(1, k_taps):
            dy_j = _shift_up(dyv, nxt, j)
            dx = dx + dy_j * wv[k_taps - 1 - j:k_taps - j, :]
            dw_ref[k_taps - 1 - j:k_taps - j, :] += jnp.sum(dy_j * xv, axis=0, keepdims=True)
        dx_ref[...] = dx.astype(dx_ref.dtype)

    tile = pl.BlockSpec((tr, tc), lambda j, i: (i, j))
    if split:
        dy_spec = pl.BlockSpec((None, tr, tc), lambda j, i: (j // per_half, i, j % per_half))
        next_spec = pl.BlockSpec((None, SUBLANES, tc),
                                 lambda j, i: (j // per_half, jnp.minimum((i + 1) * r8, last8), j % per_half))
    else:
        dy_spec = tile
        next_spec = pl.BlockSpec((SUBLANES, tc), lambda j, i: (jnp.minimum((i + 1) * r8, last8), j))
    return pl.pallas_call(
        body, name=name, grid=(ch // tc, nr),
        in_specs=[dy_spec, next_spec, pl.BlockSpec((tr, tc), lambda j, i: (i, x_cb0 + j)),
                  pl.BlockSpec((k_taps, tc), lambda j, i: (0, j))],
        out_specs=[tile, pl.BlockSpec((k_taps, tc), lambda j, i: (0, j))],
        out_shape=[jax.ShapeDtypeStruct((rows, ch), bf16), jax.ShapeDtypeStruct((k_taps, ch), f32)],
        compiler_params=_cparams(("parallel", "arbitrary")),
    )(dy, dy, x, w)


def _hdot(a, b):
    return jnp.dot(a, b, preferred_element_type=f32, precision=lax.Precision.HIGH)


def _xdot(a, b):
    return jnp.dot(a, b, preferred_element_type=f32, precision=lax.Precision.HIGHEST)


def _bdot(a, b):
    return jnp.dot(a.astype(bf16), b.astype(bf16), preferred_element_type=f32)


def _bdot_nt(a, b):
    return lax.dot_general(a.astype(bf16), b.astype(bf16), (((1,), (1,)), ((), ())), preferred_element_type=f32)


def _bdot_tn(a, b):
    return lax.dot_general(a.astype(bf16), b.astype(bf16), (((0,), (0,)), ((), ())), preferred_element_type=f32)


GDN_GROUP = 4
GDN_NGROUPS = DN_HEADS // GDN_GROUP
GDN_ROWS = GDN_GROUP * DN_CHUNK
GDN_QK_LANES = GDN_GROUP * DN_KEY_DIM
GDN_LOGIT_LANE = DN_HEADS


def _inverse_impl(lows):
    n = lows[0].shape[0]
    r = lax.broadcasted_iota(jnp.int32, (n, n), 0)
    c = lax.broadcasted_iota(jnp.int32, (n, n), 1)
    eye = (r == c).astype(f32)
    blk = jnp.right_shift(r, 3) == jnp.right_shift(c, 3)
    d = [jnp.where(blk, low, 0.0) for low in lows]
    e = [low - x for low, x in zip(lows, d)]

    def nilpotent8_inverse(xs):
        acc = [eye - x for x in xs]
        power = xs
        for _ in range(2):
            power = [_bdot(x, x) for x in power]
            acc = [_bdot(a, eye + x) for a, x in zip(acc, power)]
        return acc

    dinv = nilpotent8_inverse(d)
    ninv = nilpotent8_inverse([_bdot(x, y) for x, y in zip(dinv, e)])
    t = [_bdot(x, y) for x, y in zip(ninv, dinv)]
    for _ in range(2):
        res = [eye - x - _hdot(low, x) for low, x in zip(lows, t)]
        t = [x + _bdot(x, y) for x, y in zip(t, res)]
    return tuple(t)


@jax.custom_vjp
def _unit_lower_inverses(lows):
    return _inverse_impl(lows)


def _unit_lower_inverses_fwd(lows):
    t = _inverse_impl(lows)
    return t, t


def _unit_lower_inverses_bwd(t, ct):
    tn = (((0,), (0,)), ((), ()))
    nt = (((1,), (1,)), ((), ()))
    left = [lax.dot_general(x, g, tn, preferred_element_type=f32, precision=lax.Precision.HIGH) for x, g in zip(t, ct)]
    return (tuple(-lax.dot_general(x, y, nt, preferred_element_type=f32, precision=lax.Precision.HIGH)
                  for x, y in zip(left, t)),)


_unit_lower_inverses.defvjp(_unit_lower_inverses_fwd, _unit_lower_inverses_bwd)


@jax.custom_vjp
def _known_inverses(lows, t):
    return t


def _known_inverses_fwd(lows, t):
    return t, t


def _known_inverses_bwd(t, ct):
    return _unit_lower_inverses_bwd(t, ct) + (tuple(jnp.zeros_like(x) for x in t),)


_known_inverses.defvjp(_known_inverses_fwd, _known_inverses_bwd)


def _gdn_chunk(a_log, dt_bias, norm_w, ba, *per_group, inverses=None, keep_inverses=False):
    ng = GDN_NGROUPS
    qgs, kgs, vsts, zsts, states = [per_group[i * ng:(i + 1) * ng] for i in range(5)]
    groups = range(ng)
    n = GDN_ROWS
    r = lax.broadcasted_iota(jnp.int32, (n, n), 0)
    c = lax.broadcasted_iota(jnp.int32, (n, n), 1)
    same_head = jnp.right_shift(r, 6) == jnp.right_shift(c, 6)
    incl = jnp.logical_and(same_head, r >= c)
    strict = jnp.logical_and(same_head, r > c)
    eye = (r == c).astype(f32)
    ones = jnp.ones((n, n), f32)
    own_lanes = same_head.astype(f32)
    lane = lax.broadcasted_iota(jnp.int32, (1, LANES), 1)
    pick = lambda arr, idx: jnp.sum(jnp.where(lane == idx, arr, 0.0), axis=1, keepdims=True)
    heads = [[GDN_GROUP * g + h for h in range(GDN_GROUP)] for g in groups]
    rc = lax.broadcasted_iota(jnp.int32, (DN_CHUNK, DN_CHUNK), 0)
    cc = lax.broadcasted_iota(jnp.int32, (DN_CHUNK, DN_CHUNK), 1)

    g_all = -jnp.exp(a_log) * _softplus(ba + dt_bias)
    gc_all = _xdot((rc >= cc).astype(f32), g_all)
    gl_all = jnp.sum(g_all, axis=0, keepdims=True)
    beta = [jnp.concatenate([jax.nn.sigmoid(pick(ba, hd)) for hd in heads[g]], axis=0) for g in groups]
    gc = [jnp.concatenate([pick(gc_all, GDN_LOGIT_LANE + hd) for hd in heads[g]], axis=0) for g in groups]
    g_last = [jnp.concatenate([jnp.broadcast_to(pick(gl_all, GDN_LOGIT_LANE + hd), (DN_CHUNK, 1)) for hd in heads[g]],
                              axis=0) for g in groups]
    gr = [jnp.broadcast_to(gc[g], (n, n)).T for g in groups]
    decay = [jnp.where(incl, jnp.exp(jnp.where(incl, gc[g] - gr[g], 0.0)), 0.0) for g in groups]
    q = [jnp.concatenate([qgs[g]] * GDN_GROUP, axis=0) * own_lanes for g in groups]
    k = [jnp.concatenate([kgs[g]] * GDN_GROUP, axis=0) * own_lanes for g in groups]
    qn = [x * lax.rsqrt(jnp.sum(x * x, axis=1, keepdims=True) + L2_EPS) * (DN_KEY_DIM ** -0.5) for x in q]
    kn = [x * lax.rsqrt(jnp.sum(x * x, axis=1, keepdims=True) + L2_EPS) for x in k]
    kb = [kn[g] * beta[g] for g in groups]
    low = [jnp.where(strict, _bdot_nt(kb[g], kn[g]) * decay[g], 0.0) for g in groups]
    intra = [jnp.where(incl, _bdot_nt(qn[g], kn[g]) * decay[g], 0.0) for g in groups]
    t = _unit_lower_inverses(tuple(low)) if inverses is None else _known_inverses(tuple(low), tuple(inverses))
    u = [_bdot(t[g], vsts[g] * beta[g]) for g in groups]
    w = [_bdot(t[g], kb[g] * jnp.exp(gc[g])) for g in groups]
    sb = [s.astype(bf16) for s in states]
    v_new = [u[g] - jnp.dot(w[g].astype(bf16), sb[g], preferred_element_type=f32) for g in groups]
    o = [jnp.dot((qn[g] * jnp.exp(gc[g])).astype(bf16), sb[g], preferred_element_type=f32) for g in groups]
    o = [o[g] + _bdot(intra[g], v_new[g]) for g in groups]
    new_state = [states[g] * jnp.exp(g_last[g]) + _bdot_tn(kn[g] * jnp.exp(g_last[g] - gc[g]), v_new[g])
                 for g in groups]
    o_n = [x * lax.rsqrt(jnp.mean(x * x, axis=1, keepdims=True) + NORM_EPS) * norm_w for x in o]
    return tuple(o_n[g] * _silu(zsts[g]) for g in groups) + tuple(new_state) + (tuple(t) if keep_inverses else ())


GDN_FWD_CHUNKS = 2
GDN_BWD_CHUNKS = 1


def _gdn_specs(rows, reverse, nc):
    tr = nc * DN_CHUNK
    n = rows // tr
    idx = (lambda i: n - 1 - i) if reverse else (lambda i: i)
    vec = pl.BlockSpec((1, LANES), lambda i: (0, 0))
    qkv = pl.BlockSpec((tr, DN_CONV_CH), lambda i: (idx(i), 0))
    z = pl.BlockSpec((tr, DN_V_WIDTH), lambda i: (idx(i), OFF_Z // DN_V_WIDTH))
    ba = pl.BlockSpec((tr, LANES), lambda i: (idx(i), 0))
    wide = pl.BlockSpec((tr, DN_V_WIDTH), lambda i: (idx(i), 0))
    st = pl.BlockSpec((nc, DN_HEADS * DN_KEY_DIM, LANES), lambda i: (idx(i), 0, 0))
    inv = pl.BlockSpec((nc, GDN_NGROUPS * GDN_ROWS, GDN_ROWS), lambda i: (idx(i), 0, 0))
    return n, vec, qkv, z, ba, wide, st, inv


def _chunk_rows(c):
    return slice(c * DN_CHUNK, (c + 1) * DN_CHUNK)


def _gdn_slices(grp):
    q = slice(grp * GDN_QK_LANES, (grp + 1) * GDN_QK_LANES)
    k = slice(DN_QK_WIDTH + grp * GDN_QK_LANES, DN_QK_WIDTH + (grp + 1) * GDN_QK_LANES)
    heads = [slice((GDN_GROUP * grp + h) * LANES, (GDN_GROUP * grp + h + 1) * LANES) for h in range(GDN_GROUP)]
    vs = [slice(2 * DN_QK_WIDTH + s.start, 2 * DN_QK_WIDTH + s.stop) for s in heads]
    return q, k, vs, heads


def _stack_cols(ref, rows, cols):
    return jnp.concatenate([ref[rows, s] for s in cols], axis=0)


def _gdn_operands(qkv_ref, z_ref, rows, state_rows):
    sl = [_gdn_slices(grp) for grp in range(GDN_NGROUPS)]
    return ([qkv_ref[rows, q] for q, _, _, _ in sl] + [qkv_ref[rows, k] for _, k, _, _ in sl]
            + [_stack_cols(qkv_ref, rows, vs) for _, _, vs, _ in sl]
            + [_stack_cols(z_ref, rows, heads) for _, _, _, heads in sl]
            + [state_rows[grp * GDN_ROWS:(grp + 1) * GDN_ROWS, :] for grp in range(GDN_NGROUPS)])


def _gdn_fwd(a_log, dt_bias, norm_w, qkv_act, proj, ba):
    rows = qkv_act.shape[0]
    n, vec, qkv_s, z_s, ba_s, wide, st_s, inv_s = _gdn_specs(rows, False, GDN_FWD_CHUNKS)

    def body(al_ref, dt_ref, nw_ref, qkv_ref, z_ref, ba_ref, o_ref, st_ref, inv_ref, state):
        @pl.when(pl.program_id(0) == 0)
        def _():
            state[...] = jnp.zeros_like(state)

        for c in range(GDN_FWD_CHUNKS):
            tok = _chunk_rows(c)
            st_ref[c] = state[...]
            out = _gdn_chunk(al_ref[...], dt_ref[...], nw_ref[...], ba_ref[tok, :],
                             *_gdn_operands(qkv_ref, z_ref, tok, state), keep_inverses=True)
            for grp in range(GDN_NGROUPS):
                _, _, _, heads = _gdn_slices(grp)
                rs = slice(grp * GDN_ROWS, (grp + 1) * GDN_ROWS)
                for h, s in enumerate(heads):
                    o_ref[tok, s] = out[grp][h * DN_CHUNK:(h + 1) * DN_CHUNK].astype(o_ref.dtype)
                state[rs, :] = out[GDN_NGROUPS + grp]
                inv_ref[c, rs, :] = out[2 * GDN_NGROUPS + grp]

    n_chunks = rows // DN_CHUNK
    return pl.pallas_call(
        body, name="gdn_fwd", grid=(n,),
        in_specs=[vec, vec, vec, qkv_s, z_s, ba_s], out_specs=[wide, st_s, inv_s],
        out_shape=[jax.ShapeDtypeStruct((rows, DN_V_WIDTH), bf16),
                   jax.ShapeDtypeStruct((n_chunks, DN_HEADS * DN_KEY_DIM, LANES), f32),
                   jax.ShapeDtypeStruct((n_chunks, GDN_NGROUPS * GDN_ROWS, GDN_ROWS), f32)],
        scratch_shapes=[pltpu.VMEM((DN_HEADS * DN_KEY_DIM, LANES), f32)],
        compiler_params=_cparams(("arbitrary",)),
    )(a_log, dt_bias, norm_w, qkv_act, proj, ba)


def _gdn_bwd(a_log, dt_bias, norm_w, qkv_act, proj, ba, states, inverses, do):
    rows = qkv_act.shape[0]
    n, vec, qkv_s, z_s, ba_s, wide, st_s, inv_s = _gdn_specs(rows, True, GDN_BWD_CHUNKS)

    def body(al_ref, dt_ref, nw_ref, qkv_ref, z_ref, ba_ref, st_ref, inv_ref, do_ref,
             dal_ref, ddt_ref, dnw_ref, dqkv_ref, dz_ref, dba_ref, dstate):
        @pl.when(pl.program_id(0) == 0)
        def _():
            dstate[...] = jnp.zeros_like(dstate)
            dal_ref[...] = jnp.zeros_like(dal_ref)
            ddt_ref[...] = jnp.zeros_like(ddt_ref)
            dnw_ref[...] = jnp.zeros_like(dnw_ref)

        ng = GDN_NGROUPS
        for c in reversed(range(GDN_BWD_CHUNKS)):
            tok = _chunk_rows(c)
            kept = [inv_ref[c, grp * GDN_ROWS:(grp + 1) * GDN_ROWS, :] for grp in range(ng)]
            _, vjp = jax.vjp(functools.partial(_gdn_chunk, inverses=kept), al_ref[...], dt_ref[...], nw_ref[...],
                             ba_ref[tok, :], *_gdn_operands(qkv_ref, z_ref, tok, st_ref[c]))
            cts = tuple(_stack_cols(do_ref, tok, _gdn_slices(grp)[3]) for grp in range(ng))
            cts += tuple(dstate[grp * GDN_ROWS:(grp + 1) * GDN_ROWS, :] for grp in range(ng))
            grads = vjp(cts)
            dal_ref[...] += grads[0]
            ddt_ref[...] += grads[1]
            dnw_ref[...] += grads[2]
            dba_ref[tok, :] = grads[3]
            dqs, dks, dvs, dzs, dss = [grads[4 + i * ng:4 + (i + 1) * ng] for i in range(5)]
            for grp in range(ng):
                q, k, vs, heads = _gdn_slices(grp)
                dqkv_ref[tok, q] = dqs[grp]
                dqkv_ref[tok, k] = dks[grp]
                for h, (sv, sh) in enumerate(zip(vs, heads)):
                    rows_h = slice(h * DN_CHUNK, (h + 1) * DN_CHUNK)
                    dqkv_ref[tok, sv] = dvs[grp][rows_h]
                    dz_ref[tok, sh] = dzs[grp][rows_h].astype(dz_ref.dtype)
                dstate[grp * GDN_ROWS:(grp + 1) * GDN_ROWS, :] = dss[grp]

    return pl.pallas_call(
        body, name="gdn_bwd", grid=(n,),
        in_specs=[vec, vec, vec, qkv_s, z_s, ba_s, st_s, inv_s, wide],
        out_specs=[vec, vec, vec, qkv_s, wide, ba_s],
        out_shape=[jax.ShapeDtypeStruct((1, LANES), f32)] * 3
        + [jax.ShapeDtypeStruct((rows, DN_CONV_CH), f32), jax.ShapeDtypeStruct((rows, DN_V_WIDTH), bf16),
           jax.ShapeDtypeStruct((rows, LANES), f32)],
        scratch_shapes=[pltpu.VMEM((DN_HEADS * DN_KEY_DIM, LANES), f32)],
        compiler_params=_cparams(("arbitrary",)),
    )(a_log, dt_bias, norm_w, qkv_act, proj, ba, states, inverses, do)


def _ada_fwd(c_all, w_loc, b_loc):
    def body(c_ref, w_ref, b_ref, o_ref):
        o_ref[...] = _bdot(_silu(c_ref[...]), w_ref[...]) + b_ref[...]

    return pl.pallas_call(body, name="ada_fwd", out_shape=jax.ShapeDtypeStruct((c_all.shape[0], w_loc.shape[1]), f32),
                          compiler_params=_cparams())(c_all, w_loc, b_loc)


def _ada_bwd(c_all, dmod_cols):
    def body(c_ref, d_ref, o_ref):
        o_ref[...] = _bdot_tn(_silu(c_ref[...]), d_ref[...])

    return pl.pallas_call(body, name="ada_bwd",
                          out_shape=jax.ShapeDtypeStruct((c_all.shape[1], dmod_cols.shape[1]), f32),
                          compiler_params=_cparams())(c_all, dmod_cols)


def _sum_devices(parts):
    def body(p_ref, o_ref):
        acc = p_ref[0:1, :]
        for d in range(1, N_DEV):
            acc = acc + p_ref[d:d + 1, :]
        o_ref[...] = acc

    return pl.pallas_call(body, name="sum_small", out_shape=jax.ShapeDtypeStruct((1, parts.shape[1]), f32),
                          compiler_params=_cparams())(parts)


def _adam_math(w, g, m, v):
    m2 = ADAM_B1 * m + (1.0 - ADAM_B1) * g
    v2 = ADAM_B2 * v + (1.0 - ADAM_B2) * jnp.square(g)
    m_hat = m2 / (1.0 - ADAM_B1 ** ADAM_STEP)
    v_hat = v2 / (1.0 - ADAM_B2 ** ADAM_STEP)
    delta = -ADAM_LR * (m_hat / (jnp.sqrt(v_hat) + ADAM_EPS) + ADAM_WD * w)
    return delta, m2, v2


def _row_tile(rows):
    return _pick(rows, (256, 128, 64, 32, 16, 8))


def _adamw(w, g, m, v, name):
    rows, cols = w.shape
    tr = _row_tile(rows)

    def body(w_ref, g_ref, m_ref, v_ref, d_ref, m2_ref, v2_ref):
        d_ref[...], m2_ref[...], v2_ref[...] = _adam_math(w_ref[...], g_ref[...], m_ref[...], v_ref[...])

    tile = pl.BlockSpec((tr, cols), lambda i: (i, 0))
    return pl.pallas_call(body, name=name, grid=(rows // tr,), in_specs=[tile] * 4, out_specs=[tile] * 3,
                          out_shape=[jax.ShapeDtypeStruct(w.shape, f32)] * 3,
                          compiler_params=_cparams(("parallel",)))(w, g, m, v)


def _sum_adamw(parts, w, m, v, name):
    rows, cols = w.shape
    tr = _row_tile(rows)

    def body(p_ref, w_ref, m_ref, v_ref, g_ref, d_ref, m2_ref, v2_ref):
        g = p_ref[0].astype(f32)
        for d in range(1, N_DEV):
            g = g + p_ref[d].astype(f32)
        g_ref[...] = g
        d_ref[...], m2_ref[...], v2_ref[...] = _adam_math(w_ref[...], g, m_ref[...], v_ref[...])

    tile = pl.BlockSpec((tr, cols), lambda i: (i, 0))
    return pl.pallas_call(body, name=name, grid=(rows // tr,),
                          in_specs=[pl.BlockSpec((N_DEV, tr, cols), lambda i: (0, i, 0)), tile, tile, tile],
                          out_specs=[tile] * 4, out_shape=[jax.ShapeDtypeStruct(w.shape, f32)] * 4,
                          compiler_params=_cparams(("parallel",)))(parts, w, m, v)


def _pad_lanes(a, width):
    return jnp.pad(a, ((0, 0), (0, width - a.shape[1])))


def _cols_by_device(full):
    r, c = full.shape
    return jnp.moveaxis(full.reshape(r, N_DEV, c // N_DEV), 1, 0)


def _cols_from_devices(parts):
    d, r, n = parts.shape
    return jnp.moveaxis(parts, 0, 1).reshape(r, d * n)


def kernel(x, c, w_ada, b_ada, norm1_w, w_in, dn_conv_w, dn_A_log, dn_dt_bias, dn_norm_w, w_proj_sb, w_proj_dn, w_out, norm2_w, w_ffn_in, ffn_conv_w, ffn_conv_b, w_ffn_out, final_norm_w, loss_target, m_w_ada, m_b_ada, m_norm1_w, m_w_in, m_dn_conv_w, m_dn_A_log, m_dn_dt_bias, m_dn_norm_w, m_w_proj_sb, m_w_proj_dn, m_w_out, m_norm2_w, m_w_ffn_in, m_ffn_conv_w, m_ffn_conv_b, m_w_ffn_out, m_final_norm_w, v_w_ada, v_b_ada, v_norm1_w, v_w_in, v_dn_conv_w, v_dn_A_log, v_dn_dt_bias, v_dn_norm_w, v_w_proj_sb, v_w_proj_dn, v_w_out, v_norm2_w, v_w_ffn_in, v_ffn_conv_w, v_ffn_conv_b, v_w_ffn_out, v_final_norm_w):
    d = D_MODEL
    me = 4 * lax.axis_index("x") + 2 * lax.axis_index("y") + lax.axis_index("c")
    xs = x[0]
    target = loss_target[0]
    n_ada = w_ada.shape[2]
    n_dnc = dn_conv_w.shape[2]
    n_ffc = ffn_conv_w.shape[2]

    small = jnp.concatenate([c, dn_conv_w[0].reshape(1, -1), ffn_conv_w[0].reshape(1, -1)], axis=1)
    small = _pad_lanes(small, -(-small.shape[1] // LANES) * LANES)
    small_g, w_in_g = _all_gather([small, w_in[0].astype(bf16)], "gather_w_in")
    later = [w_proj_sb[0].astype(bf16), w_proj_dn[0].astype(bf16), w_out[0].astype(bf16),
             w_ffn_in[0].astype(bf16), w_ffn_out[0].astype(bf16)]
    gather_later = _SideComm(_gather_protocol, later, _gathered_shapes(later))
    small_g = small_g[:, 0, :]
    c_all = small_g[:, :d]
    dn_cw = _cols_from_devices(small_g[:, d:d + DN_CONV_WIDTH * n_dnc].reshape(N_DEV, DN_CONV_WIDTH, n_dnc))
    o2 = d + DN_CONV_WIDTH * n_dnc
    ffn_cw = _cols_from_devices(small_g[:, o2:o2 + FFN_CONV_WIDTH * n_ffc].reshape(N_DEV, FFN_CONV_WIDTH, n_ffc))

    w_in_full = _cols_from_devices(w_in_g)
    r_sb, r_dn, r_z = 3 * SB_WIDTH, 3 * SB_WIDTH + DN_CONV_CH, 3 * SB_WIDTH + DN_CONV_CH + DN_V_WIDTH
    r_g = r_z + 2 * DN_HEADS
    w_main = jnp.concatenate([w_in_full[:, r_g:], w_in_full[:, r_sb:r_dn], w_in_full[:, r_dn:r_z],
                              w_in_full[:, :r_sb]], axis=1)
    w_ba = _pad_lanes(w_in_full[:, r_z:r_g], LANES)

    b_loc = lax.dynamic_slice(b_ada, (0, me * n_ada), (1, n_ada))
    mod_part = _ada_fwd(c_all, w_ada[0], b_loc)
    (mod_g,) = _all_gather([mod_part], "gather_mod")
    mod = lax.dynamic_index_in_dim(mod_g, me, axis=1, keepdims=False).reshape(1, N_DEV * n_ada)
    shift1, scale1, gate1, shift2, scale2, gate2 = [mod[:, i * d:(i + 1) * d] for i in range(6)]

    logit_lanes = ((0, 0), (GDN_LOGIT_LANE, LANES - GDN_LOGIT_LANE - DN_HEADS))
    a_log = jnp.pad(dn_A_log, logit_lanes)
    dt_b = jnp.pad(dn_dt_bias, logit_lanes)

    (h,) = _stage_fwd(_f_normmod, [norm1_w, shift1, scale1], [xs], [bf16], "norm1_fwd")
    proj = _mm(h, w_main, name="in_proj")
    ba = _mm(h, w_ba, name="in_proj_ba")
    k16, v16 = _sb_prepare(proj)
    o_a, sb_runs, w_psb_g, w_pdn_g, w_out_g, w_fin_g, w_fout_g = _sb_attention_fwd2(
        proj, k16, v16, side=gather_later)
    w_psb = _cols_from_devices(w_psb_g)
    w_pdn = w_pdn_g.reshape(DN_V_WIDTH, d)
    w_o = w_out_g.reshape(d, d)
    w_fin = _cols_from_devices(w_fin_g)
    w_fout = w_fout_g.reshape(D_FF, d)
    qkv_act = _dn_conv_fwd(proj, dn_cw)
    o_b, states, dn_inverses = _gdn_fwd(a_log, dt_b, dn_norm_w, qkv_act, proj, ba)
    pa = _mm(o_a, w_psb, name="proj_sb")
    pb = _mm(o_b, w_pdn, name="proj_dn")
    gates = [(proj, d, OFF_GA // d), (proj, d, OFF_GB // d)]
    (merged,) = _stage_fwd(_f_merge, [], gates + [pa, pb], [bf16], "merge_fwd")
    ao = _mm(merged, w_o, name="out_proj")
    mid_params = [gate1, norm2_w, shift2, scale2]
    x1, h2 = _stage_fwd(_f_residual_normmod, mid_params, [xs, ao], [f32, bf16], "resid1_norm2_fwd")
    u_pre = _mm(h2, w_fin, name="ffn_in")
    act, u_conv = _ffn_conv_fwd(u_pre, ffn_cw, ffn_conv_b)
    fo = _mm(act, w_fout, name="ffn_out")

    loss_p, d_gate2, d_wf, dx2, dfo = _loss_and_grads(gate2, final_norm_w.reshape(1, d), x1, fo, target)
    dact = _mm(dfo, w_fout, tb=True, name="ffn_out_dx")
    g_w_fout = _mm(act, dfo, ta=True, name="ffn_out_dw")
    du, dbg, dbu = _ffn_conv_bwd_act(u_conv, dact)
    du_pre, d_ffn_cw = _conv_bwd(du, u_pre, 0, ffn_cw, "ffn_conv_bwd")
    dh2 = _mm(du_pre, w_fin, tb=True, name="ffn_in_dx")
    g_w_fin = _mm(h2, du_pre, ta=True, name="ffn_in_dw")
    (d_gate1, d_n2w, d_shift2, d_scale2), (dx1, dao) = _stage_bwd(
        _f_residual_normmod, mid_params, [xs, ao], [dx2, dh2], [f32, bf16], "resid1_norm2_bwd")
    dmerged = _mm(dao, w_o, tb=True, name="out_proj_dx")
    g_w_o = _mm(merged, dao, ta=True, name="out_proj_dw")
    _, (dga, dgb, dpa, dpb) = _stage_bwd(_f_merge, [], gates + [pa, pb], [dmerged], [bf16] * 4, "merge_bwd")
    do_a = _mm(dpa, w_psb, tb=True, name="proj_sb_dx")
    g_w_psb = _mm(o_a, dpa, ta=True, name="proj_sb_dw")
    do_b = _mm(dpb, w_pdn, tb=True, name="proj_dn_dx")
    g_w_pdn = _mm(o_b, dpb, ta=True, name="proj_dn_dw")
    early = [_cols_by_device(g_w_psb).astype(bf16),
             g_w_pdn.reshape(N_DEV, DN_V_WIDTH // N_DEV, d).astype(bf16),
             g_w_o.reshape(N_DEV, d // N_DEV, d).astype(bf16),
             _cols_by_device(g_w_fin).astype(bf16),
             g_w_fout.reshape(N_DEV, D_FF // N_DEV, d).astype(bf16)]
    exchange_early = _SideComm(_exchange_protocol, early, [jax.ShapeDtypeStruct(a.shape, a.dtype) for a in early])
    dq, dk, dv, *recv_early = _sb_attention_bwd2(proj, k16, v16, sb_runs, do_a, side=exchange_early)
    d_alog, d_dtb, d_dnw, dqkv_act, dz, dba = _gdn_bwd(a_log, dt_b, dn_norm_w, qkv_act, proj, ba, states,
                                                       dn_inverses, do_b)
    d_conv_out = _dn_conv_bwd_act(proj, dn_cw, dqkv_act)
    d_dn_pre, d_dn_cw = _conv_bwd(d_conv_out, proj, OFF_DN // TCONV_C, dn_cw, "dn_conv_bwd")
    dproj = jnp.concatenate([dga, dgb, d_dn_pre, dz, dq, dk.astype(bf16), dv.astype(bf16)], axis=1)
    g_w_main = _mm(h, dproj, ta=True, name="in_proj_dw")
    g_w_ba = _mm(h, dba, ta=True, name="in_proj_ba_dw")
    g_w_in_full = jnp.concatenate([g_w_main[:, OFF_SBQ:], g_w_main[:, OFF_DN:OFF_Z], g_w_main[:, OFF_Z:OFF_SBQ],
                                   g_w_ba[:, :2 * DN_HEADS], g_w_main[:, :OFF_DN]], axis=1)
    w_in_parts = _cols_by_device(g_w_in_full).astype(bf16)
    exchange_w_in = _SideComm(_exchange_protocol, [w_in_parts], [jax.ShapeDtypeStruct(w_in_parts.shape, bf16)])
    dh, recv_w_in = _mm(dproj, w_main, tb=True, name="in_proj_dx", side=exchange_w_in)
    dh_ba = _mm(dba, w_ba, tb=True, name="in_proj_ba_dx")
    (d_n1w, d_shift1, d_scale1), (grad_x,) = _stage_bwd(
        _f_normmod, [norm1_w, shift1, scale1], [xs], [[dh, dh_ba]], [f32], "norm1_bwd", residual=(0, dx1))

    dmod = jnp.concatenate([d_shift1, d_scale1, d_gate1, d_shift2, d_scale2, d_gate2], axis=1)
    d_ffn_cb = jnp.concatenate([dbg, dbu], axis=1)
    small_parts = jnp.concatenate(
        [loss_p, dmod, d_n1w, d_alog, d_dtb, d_dnw, d_n2w, d_ffn_cb, d_wf,
         d_dn_cw.reshape(1, -1), d_ffn_cw.reshape(1, -1)], axis=1)
    (small_parts_g,) = _all_gather([small_parts], "gather_small_grads")
    tot = _sum_devices(small_parts_g[:, 0, :])
    offs = {}
    pos = 0
    for nm, width in (("loss", LANES), ("b_ada", 6 * d), ("norm1_w", d), ("dn_A_log", LANES), ("dn_dt_bias", LANES),
                      ("dn_norm_w", LANES), ("norm2_w", d), ("ffn_conv_b", 2 * D_FF), ("final_norm_w", d),
                      ("dn_conv_w", DN_CONV_WIDTH * DN_CONV_CH), ("ffn_conv_w", FFN_CONV_WIDTH * 2 * D_FF)):
        offs[nm] = (pos, width)
        pos += width
    seg = lambda nm: tot[:, offs[nm][0]:offs[nm][0] + offs[nm][1]]
    loss = tot[0, 0]
    g_b_ada = seg("b_ada")
    g_norm1 = seg("norm1_w")
    g_alog = seg("dn_A_log")[:, GDN_LOGIT_LANE:GDN_LOGIT_LANE + DN_HEADS]
    g_dtb = seg("dn_dt_bias")[:, GDN_LOGIT_LANE:GDN_LOGIT_LANE + DN_HEADS]
    g_dnw = seg("dn_norm_w")
    g_norm2 = seg("norm2_w")
    g_ffn_cb = seg("ffn_conv_b")
    g_fnw = seg("final_norm_w")
    g_dn_cw = lax.dynamic_slice(seg("dn_conv_w").reshape(DN_CONV_WIDTH, DN_CONV_CH), (0, me * n_dnc),
                                (DN_CONV_WIDTH, n_dnc))
    g_ffn_cw = lax.dynamic_slice(seg("ffn_conv_w").reshape(FFN_CONV_WIDTH, 2 * D_FF), (0, me * n_ffc),
                                 (FFN_CONV_WIDTH, n_ffc))

    dmod_all = small_parts_g[:, 0, offs["b_ada"][0]:offs["b_ada"][0] + 6 * d]
    g_w_ada = _ada_bwd(c_all, lax.dynamic_slice(dmod_all, (0, me * n_ada), (N_DEV, n_ada)))

    def pack(parts):
        flat = [p.reshape(1, -1) for p in parts]
        flat = [_pad_lanes(p, -(-p.shape[1] // LANES) * LANES) for p in flat]
        return jnp.concatenate(flat, axis=1), [p.shape[1] for p in flat]

    small_names_g = [g_b_ada, g_norm1, g_alog, g_dtb, g_dnw, g_norm2, g_ffn_cb, g_fnw, g_dn_cw, g_ffn_cw]
    small_w = [b_ada, norm1_w, dn_A_log, dn_dt_bias, dn_norm_w, norm2_w, ffn_conv_b, final_norm_w, dn_conv_w[0], ffn_conv_w[0]]
    small_m = [m_b_ada, m_norm1_w, m_dn_A_log, m_dn_dt_bias, m_dn_norm_w, m_norm2_w, m_ffn_conv_b, m_final_norm_w, m_dn_conv_w[0], m_ffn_conv_w[0]]
    small_v = [v_b_ada, v_norm1_w, v_dn_A_log, v_dn_dt_bias, v_dn_norm_w, v_norm2_w, v_ffn_conv_b, v_final_norm_w, v_dn_conv_w[0], v_ffn_conv_w[0]]
    pg, widths = pack(small_names_g)
    pw, _ = pack(small_w)
    pm, _ = pack(small_m)
    pv, _ = pack(small_v)
    s_delta, s_m, s_v = _adamw(pw, pg, pm, pv, "adamw_small")

    def unpack(flat):
        out, pos = [], 0
        for ref_arr, width in zip(small_w, widths):
            out.append(flat[:, pos:pos + ref_arr.size].reshape(ref_arr.shape))
            pos += width
        return out

    small_grads = [g.reshape(w_.shape) for g, w_ in zip(small_names_g, small_w)]
    small_delta, small_newm, small_newv = unpack(s_delta), unpack(s_m), unpack(s_v)

    ada_delta, ada_m, ada_v = _adamw(w_ada[0], g_w_ada, m_w_ada[0], v_w_ada[0], "adamw_ada")

    recv = [recv_w_in] + list(recv_early)
    big = {}
    for nm, parts, w_, m_, v_ in (("w_in", recv[0], w_in, m_w_in, v_w_in),
                                  ("w_proj_sb", recv[1], w_proj_sb, m_w_proj_sb, v_w_proj_sb),
                                  ("w_proj_dn", recv[2], w_proj_dn, m_w_proj_dn, v_w_proj_dn),
                                  ("w_out", recv[3], w_out, m_w_out, v_w_out),
                                  ("w_ffn_in", recv[4], w_ffn_in, m_w_ffn_in, v_w_ffn_in),
                                  ("w_ffn_out", recv[5], w_ffn_out, m_w_ffn_out, v_w_ffn_out)):
        big[nm] = [t[None] for t in _sum_adamw(parts, w_[0], m_[0], v_[0], "adamw_" + nm)]

    sg = dict(zip(["b_ada", "norm1_w", "dn_A_log", "dn_dt_bias", "dn_norm_w", "norm2_w", "ffn_conv_b", "final_norm_w",
                   "dn_conv_w", "ffn_conv_w"], range(10)))

    def small_out(table, nm):
        val = table[sg[nm]]
        return val[None] if nm in ("dn_conv_w", "ffn_conv_w") else val

    order = ["w_ada", "b_ada", "norm1_w", "w_in", "dn_conv_w", "dn_A_log", "dn_dt_bias", "dn_norm_w", "w_proj_sb",
             "w_proj_dn", "w_out", "norm2_w", "w_ffn_in", "ffn_conv_w", "ffn_conv_b", "w_ffn_out", "final_norm_w"]
    groups = []
    for k, small_table in enumerate((small_grads, small_delta, small_newm, small_newv)):
        row = []
        for nm in order:
            if nm == "w_ada":
                row.append((g_w_ada, ada_delta, ada_m, ada_v)[k][None])
            elif nm in big:
                row.append(big[nm][k])
            else:
                row.append(small_out(small_table, nm))
        groups.append(row)
    return (loss, grad_x[None], *groups[0], *groups[1], *groups[2], *groups[3])
```

```python
import functools

import jax
import jax.numpy as jnp
from jax import lax
from jax.experimental import pallas as pl
from jax.experimental.pallas import tpu as pltpu

f32 = jnp.float32
bf16 = jnp.bfloat16

D_MODEL = 1024
SB_HEADS = 8
SB_HEAD_DIM = 64
SB_WIDTH = SB_HEADS * SB_HEAD_DIM
SB_QBLOCK = 128
DN_HEADS = 8
DN_KEY_DIM = 64
DN_VAL_DIM = 128
DN_QK_WIDTH = DN_HEADS * DN_KEY_DIM
DN_V_WIDTH = DN_HEADS * DN_VAL_DIM
DN_CONV_CH = 2 * DN_QK_WIDTH + DN_V_WIDTH
DN_CONV_WIDTH = 4
DN_CHUNK = 64
D_FF = 2816
FFN_CONV_WIDTH = 3
NORM_EPS = 1e-6
L2_EPS = 1e-6
ADAM_LR = 0.001
ADAM_B1 = 0.9
ADAM_B2 = 0.999
ADAM_EPS = 1e-08
ADAM_WD = 0.01
ADAM_STEP = 10

N_DEV = 8
MESH = pl.DeviceIdType.MESH

LANES = 128
SUBLANES = 8
VMEM_LIMIT = 48 * 1024 * 1024

OFF_GA = 0
OFF_GB = D_MODEL
OFF_DN = 2 * D_MODEL
OFF_Z = OFF_DN + DN_CONV_CH
OFF_SBQ = OFF_Z + DN_V_WIDTH
OFF_SBK = OFF_SBQ + SB_WIDTH
OFF_SBV = OFF_SBK + SB_WIDTH
MAIN_WIDTH = OFF_SBV + SB_WIDTH

TM = 512
TCONV_R = 512
TCONV_C = 1024
TCONV_FF = D_FF // 2
SB_PAIRS_FWD = 4
SB_PAIRS_BWD = 4
SB_DEAD = -106.0
SB_NEVER = -1e30


def _cparams(sem=None):
    return pltpu.CompilerParams(dimension_semantics=sem, vmem_limit_bytes=VMEM_LIMIT)


def _pick(n, cands):
    for c in cands:
        if n % c == 0:
            return c
    return n


def _my_pos():
    return lax.axis_index("x"), lax.axis_index("y"), lax.axis_index("c")


def _flip(v, bit):
    return 1 - v if bit else v


def _comm_scratch(n):
    return [pltpu.SemaphoreType.DMA((n, 7)), pltpu.SemaphoreType.DMA((n, 7)), pltpu.SemaphoreType.DMA((n,))]


def _gather_protocol(ins, outs, send_sems, recv_sems, local_sems):
    n = len(ins)
    x, y, c = _my_pos()
    me, sibling = (x, y, c), (x, y, 1 - c)
    chips = [(1 - x, y), (x, 1 - y), (1 - x, 1 - y)]

    def slot(out, pos):
        return out.at[4 * pos[0] + 2 * pos[1] + pos[2]]

    def copy(a, k, block, to, src=None):
        return pltpu.make_async_remote_copy(
            src_ref=slot(outs[a], block) if src is None else src, dst_ref=slot(outs[a], block),
            send_sem=send_sems.at[a, k], recv_sem=recv_sems.at[a, k], device_id=to, device_id_type=MESH)

    def local(a):
        return pltpu.make_async_copy(ins[a], slot(outs[a], me), local_sems.at[a])

    def first(a):
        return [copy(a, 0, me, sibling, src=ins[a])] + [copy(a, 1 + j, me, (*chip, c), src=ins[a])
                                                         for j, chip in enumerate(chips)]

    def start():
        for a in range(n):
            local(a).start()
            for cp in first(a):
                cp.start()

    def finish():
        forwards = []
        for a in range(n):
            for j, chip in enumerate(chips):
                copy(a, 1 + j, (*chip, c), me).wait_recv()
                fwd = copy(a, 4 + j, (*chip, c), sibling)
                fwd.start()
                forwards.append(fwd)
        for a in range(n):
            copy(a, 0, sibling, me).wait_recv()
            for j, chip in enumerate(chips):
                copy(a, 4 + j, (*chip, 1 - c), me).wait_recv()
        for a in range(n):
            for cp in first(a):
                cp.wait_send()
        for cp in forwards:
            cp.wait_send()
        for a in range(n):
            local(a).wait()

    return start, finish


def _exchange_protocol(ins, outs, send_sems, recv_sems, local_sems):
    n = len(ins)
    x, y, c = _my_pos()
    me_idx = 4 * x + 2 * y + c

    def local(a):
        return pltpu.make_async_copy(ins[a].at[me_idx], outs[a].at[me_idx], local_sems.at[a])

    def copies(a, m):
        peer = (_flip(x, m & 4), _flip(y, m & 2), _flip(c, m & 1))
        peer_idx = 4 * peer[0] + 2 * peer[1] + peer[2]
        sems = dict(send_sem=send_sems.at[a, m - 1], recv_sem=recv_sems.at[a, m - 1], device_id=peer,
                    device_id_type=MESH)
        send = pltpu.make_async_remote_copy(src_ref=ins[a].at[peer_idx], dst_ref=outs[a].at[me_idx], **sems)
        recv = pltpu.make_async_remote_copy(src_ref=ins[a].at[peer_idx], dst_ref=outs[a].at[peer_idx], **sems)
        return send, recv

    def start():
        for a in range(n):
            local(a).start()
            for m in range(1, N_DEV):
                copies(a, m)[0].start()

    def finish():
        for a in range(n):
            for m in range(1, N_DEV):
                copies(a, m)[1].wait_recv()
        for a in range(n):
            for m in range(1, N_DEV):
                copies(a, m)[0].wait_send()
            local(a).wait()

    return start, finish


def _collective_call(protocol, arrs, out_shapes, name):
    n = len(arrs)

    def body(*refs):
        start, finish = protocol(refs[:n], refs[n:2 * n], *refs[2 * n:])
        start()
        finish()

    any_spec = pl.BlockSpec(memory_space=pl.ANY)
    return pl.pallas_call(body, name=name, out_shape=out_shapes, in_specs=[any_spec] * n, out_specs=[any_spec] * n,
                          scratch_shapes=_comm_scratch(n))(*arrs)


def _gathered_shapes(arrs):
    return [jax.ShapeDtypeStruct((N_DEV,) + a.shape, a.dtype) for a in arrs]


def _all_gather(arrs, name):
    return _collective_call(_gather_protocol, arrs, _gathered_shapes(arrs), name)


MM_BLOCK_BYTES = 7 * 1024 * 1024
MM_TILE_CAP = 1664


def _lane_tile(n, cap):
    fits = [t for t in range(LANES, min(n, cap) + 1, LANES) if n % t == 0]
    return max(fits) if fits else n


def _mm_tiles(m_dim, n_dim, k_dim, a_bytes, b_bytes):
    tm = _lane_tile(m_dim, MM_TILE_CAP)
    tn = _lane_tile(n_dim, MM_TILE_CAP)
    while tm * tn * 4 > MM_BLOCK_BYTES:
        if tn >= tm and (tn // 2) % LANES == 0:
            tn //= 2
        else:
            tm //= 2
    if (m_dim % (2 * tm) == 0 and 2 * tm * k_dim * a_bytes <= MM_BLOCK_BYTES
            and 2 * tm * tn * 4 <= MM_BLOCK_BYTES):
        tm *= 2
    tk = k_dim
    if k_dim % LANES == 0:
        units = k_dim // LANES
        fits = [u for u in range(1, units + 1) if units % u == 0
                and u * LANES * max(tm * a_bytes, tn * b_bytes) <= MM_BLOCK_BYTES]
        tk = max(fits) * LANES
    return tm, tn, tk


def _mm(a, b, *, ta=False, tb=False, name, side=None):
    (k_dim, m_dim) = a.shape if ta else a.shape[::-1]
    (n_dim, kb_dim) = b.shape if tb else b.shape[::-1]
    assert k_dim == kb_dim, (a.shape, b.shape, ta, tb)
    tm, tn, tk = _mm_tiles(m_dim, n_dim, k_dim, a.dtype.itemsize, b.dtype.itemsize)
    nk = k_dim // tk
    grid = (m_dim // tm, n_dim // tn, nk)
    dims = (((0 if ta else 1,), (1 if tb else 0,)), ((), ()))
    ns = side.n if side else 0

    def body(a_ref, b_ref, *rest):
        if side:
            side.run(rest[:ns], rest[ns + 1:2 * ns + 1], rest[2 * ns + 1:], *_grid_ends(grid),
                     lambda: compute(a_ref, b_ref, rest[ns]))
        else:
            compute(a_ref, b_ref, rest[0])

    def compute(a_ref, b_ref, o_ref):
        part = lax.dot_general(a_ref[...].astype(bf16), b_ref[...].astype(bf16), dims, preferred_element_type=f32)
        if nk == 1:
            o_ref[...] = part
        else:
            k = pl.program_id(2)

            @pl.when(k == 0)
            def _():
                o_ref[...] = part

            @pl.when(k > 0)
            def _():
                o_ref[...] += part

    a_spec = pl.BlockSpec((tk, tm), lambda i, j, k: (k, i)) if ta else pl.BlockSpec((tm, tk), lambda i, j, k: (i, k))
    b_spec = pl.BlockSpec((tn, tk), lambda i, j, k: (j, k)) if tb else pl.BlockSpec((tk, tn), lambda i, j, k: (k, j))
    out_spec = pl.BlockSpec((tm, tn), lambda i, j, k: (i, j))
    out_shape = jax.ShapeDtypeStruct((m_dim, n_dim), f32)
    if not side:
        return pl.pallas_call(body, name=name, grid=grid, in_specs=[a_spec, b_spec], out_specs=out_spec,
                              out_shape=out_shape,
                              compiler_params=_cparams(("parallel", "parallel", "arbitrary")))(a, b)
    return pl.pallas_call(
        body, name=name, grid=grid, in_specs=[a_spec, b_spec] + side.specs(), out_specs=[out_spec] + side.specs(),
        out_shape=[out_shape] + side.out_shapes, scratch_shapes=_comm_scratch(ns),
        compiler_params=_cparams(("arbitrary", "arbitrary", "arbitrary")))(a, b, *side.arrs)


def _win(t):
    return t if isinstance(t, tuple) else (t, t.shape[1], 0)


def _tile_spec(width, cb, tm):
    return pl.BlockSpec((tm, width), lambda i: (i, cb))


def _param_spec(p):
    return pl.BlockSpec(p.shape, lambda i: (0, 0))


def _stage_fwd(f, params, tiles, out_dtypes, name):
    tiles = [_win(t) for t in tiles]
    rows = tiles[0][0].shape[0]
    tm = min(TM, rows)
    avals = jax.eval_shape(f, *[jax.ShapeDtypeStruct(p.shape, f32) for p in params],
                           *[jax.ShapeDtypeStruct((tm, w), f32) for _, w, _ in tiles])
    n_p, n_t = len(params), len(tiles)

    def body(*refs):
        p = [r[...] for r in refs[:n_p]]
        t = [r[...].astype(f32) for r in refs[n_p:n_p + n_t]]
        for o_ref, val in zip(refs[n_p + n_t:], f(*p, *t)):
            o_ref[...] = val.astype(o_ref.dtype)

    return pl.pallas_call(
        body, name=name, grid=(rows // tm,),
        in_specs=[_param_spec(p) for p in params] + [_tile_spec(w, cb, tm) for _, w, cb in tiles],
        out_specs=[_tile_spec(a.shape[1], 0, tm) for a in avals],
        out_shape=[jax.ShapeDtypeStruct((rows, a.shape[1]), dt) for a, dt in zip(avals, out_dtypes)],
        compiler_params=_cparams(("parallel",)),
    )(*params, *[t[0] for t in tiles])


def _stage_bwd(f, params, tiles, cts, grad_dtypes, name, residual=None):
    tiles = [_win(t) for t in tiles]
    rows = tiles[0][0].shape[0]
    tm = min(TM, rows)
    cts = [list(g) if isinstance(g, (list, tuple)) else [g] for g in cts]
    flat_cts = [a for g in cts for a in g]
    n_p, n_t, n_c = len(params), len(tiles), len(flat_cts)
    has_res = residual is not None
    want = [j for j, dt in enumerate(grad_dtypes) if dt is not None]

    def body(*refs):
        i = pl.program_id(0)
        p = [r[...] for r in refs[:n_p]]
        t = [r[...].astype(f32) for r in refs[n_p:n_p + n_t]]
        ct_vals = [r[...].astype(f32) for r in refs[n_p + n_t:n_p + n_t + n_c]]
        ct, at = [], 0
        for g in cts:
            ct.append(functools.reduce(jnp.add, ct_vals[at:at + len(g)]))
            at += len(g)
        ct = tuple(ct)
        pos = n_p + n_t + n_c
        res_ref = refs[pos] if has_res else None
        pos += 1 if has_res else 0
        dp_refs = refs[pos:pos + n_p]
        dt_refs = refs[pos + n_p:]
        _, vjp = jax.vjp(f, *p, *t)
        grads = vjp(ct)

        @pl.when(i == 0)
        def _():
            for r in dp_refs:
                r[...] = jnp.zeros_like(r)

        for r, g in zip(dp_refs, grads[:n_p]):
            r[...] += g
        for r, j in zip(dt_refs, want):
            g = grads[n_p + j]
            if has_res and j == residual[0]:
                g = g + res_ref[...].astype(f32)
            r[...] = g.astype(r.dtype)

    in_arrays = list(params) + [t[0] for t in tiles] + flat_cts
    in_specs = ([_param_spec(p) for p in params] + [_tile_spec(w, cb, tm) for _, w, cb in tiles]
                + [_tile_spec(c.shape[1], 0, tm) for c in flat_cts])
    if has_res:
        in_arrays.append(residual[1])
        in_specs.append(_tile_spec(residual[1].shape[1], 0, tm))
    out_shape = ([jax.ShapeDtypeStruct(p.shape, f32) for p in params]
                 + [jax.ShapeDtypeStruct((rows, tiles[j][1]), grad_dtypes[j]) for j in want])
    out_specs = [_param_spec(p) for p in params] + [_tile_spec(tiles[j][1], 0, tm) for j in want]
    outs = pl.pallas_call(
        body, name=name, grid=(rows // tm,), in_specs=in_specs, out_specs=out_specs, out_shape=out_shape,
        compiler_params=_cparams(("arbitrary",)),
    )(*in_arrays)
    return outs[:n_p], outs[n_p:]


def _rms(x, w):
    return x * lax.rsqrt(jnp.mean(x * x, axis=-1, keepdims=True) + NORM_EPS) * w


def _f_normmod(w, shift, scale, x):
    return (_rms(x, w) * (1.0 + scale) + shift,)


def _f_merge(ga, gb, pa, pb):
    return (jax.nn.sigmoid(ga) * pa + jax.nn.sigmoid(gb) * pb,)


def _f_residual_normmod(gate, w, shift, scale, x, branch):
    x1 = x + gate * branch
    return x1, _rms(x1, w) * (1.0 + scale) + shift


def _f_loss(gate, wf, x1, fo, target):
    y = _rms(x1 + gate * fo, wf)
    err = jnp.square(y - target)
    return (0.5 * jnp.sum(jnp.mean(err, axis=-1, keepdims=True), axis=0, keepdims=True),)


def _loss_and_grads(gate2, wf, x1, fo, target):
    rows, d = x1.shape
    tm = min(TM, rows)

    def body(g_ref, w_ref, x_ref, fo_ref, t_ref, loss_ref, dg_ref, dw_ref, dx_ref, dfo_ref):
        i = pl.program_id(0)
        (val,), vjp = jax.vjp(_f_loss, g_ref[...], w_ref[...], x_ref[...], fo_ref[...], t_ref[...])
        dg, dw, dx, dfo, _ = vjp((jnp.ones((1, 1), f32),))

        @pl.when(i == 0)
        def _():
            loss_ref[...] = jnp.zeros_like(loss_ref)
            dg_ref[...] = jnp.zeros_like(dg_ref)
            dw_ref[...] = jnp.zeros_like(dw_ref)

        loss_ref[...] += jnp.broadcast_to(val, loss_ref.shape)
        dg_ref[...] += dg
        dw_ref[...] += dw
        dx_ref[...] = dx
        dfo_ref[...] = dfo.astype(bf16)

    vec = pl.BlockSpec((1, d), lambda i: (0, 0))
    tile = pl.BlockSpec((tm, d), lambda i: (i, 0))
    return pl.pallas_call(
        body, name="loss_fwd_bwd", grid=(rows // tm,),
        in_specs=[vec, vec, tile, tile, tile],
        out_specs=[pl.BlockSpec((1, LANES), lambda i: (0, 0)), vec, vec, tile, tile],
        out_shape=[jax.ShapeDtypeStruct((1, LANES), f32), jax.ShapeDtypeStruct((1, d), f32),
                   jax.ShapeDtypeStruct((1, d), f32), jax.ShapeDtypeStruct((rows, d), f32),
                   jax.ShapeDtypeStruct((rows, d), bf16)],
        compiler_params=_cparams(("arbitrary",)),
    )(gate2, wf, x1, fo, target)


def _softplus(z):
    return jnp.maximum(z, 0.0) + jnp.log(1.0 + jnp.exp(-jnp.abs(z)))


def _split_dot(a, m):
    hi = a.astype(bf16)
    lo = (a - hi.astype(f32)).astype(bf16)
    return jnp.dot(hi, m, preferred_element_type=f32) + jnp.dot(lo, m, preferred_element_type=f32)


def _suffix_matrix(n):
    r = lax.broadcasted_iota(jnp.int32, (n, n), 0)
    c = lax.broadcasted_iota(jnp.int32, (n, n), 1)
    return (r > c).astype(bf16)


def _head_masks():
    lane = lax.broadcasted_iota(jnp.int32, (1, LANES), 1)
    return [(lane < SB_HEAD_DIM).astype(f32), (lane >= SB_HEAD_DIM).astype(f32)]


def _sb_prepare(proj):
    wins = [(proj, SB_WIDTH, OFF_SBK // SB_WIDTH), (proj, SB_WIDTH, OFF_SBV // SB_WIDTH)]
    return _stage_fwd(lambda k, v: (k, v), [], wins, [bf16] * 2, "sb_prepare")


def _by_head(rows_head0, rows_head1):
    m0, m1 = _head_masks()
    return rows_head0 * m0 + rows_head1 * m1


def _stack_heads(x):
    m0, m1 = _head_masks()
    return jnp.concatenate([x * m0, x * m1], axis=0)


def _sb_logits(qst, k, t_pos2, kb, bq, masked):
    z = lax.dot_general(qst, k, (((1,), (1,)), ((), ())), preferred_element_type=f32)
    l = -_softplus(z)
    if masked:
        s_pos = kb * bq + lax.broadcasted_iota(jnp.int32, (1, bq), 1)
        causal = s_pos < t_pos2
        l = jnp.where(causal, l, 0.0)
    else:
        causal = None
    return z, l, causal


class _SideComm:
    def __init__(self, protocol, arrs, out_shapes):
        self.protocol, self.arrs, self.out_shapes = protocol, list(arrs), list(out_shapes)
        self.n = len(self.arrs)

    def specs(self):
        return [pl.BlockSpec(memory_space=pl.ANY)] * self.n

    def run(self, in_refs, out_refs, sems, first, last, compute):
        start, finish = self.protocol(in_refs, out_refs, *sems)
        pl.when(first)(start)
        compute()
        pl.when(last)(finish)


def _grid_ends(grid):
    ids = [pl.program_id(axis) for axis in range(len(grid))]
    first = functools.reduce(jnp.logical_and, [i == 0 for i in ids])
    last = functools.reduce(jnp.logical_and, [i == g - 1 for i, g in zip(ids, grid)])
    return first, last


def _sb_attention_fwd2(proj, k16, v16, side=None):
    rows = proj.shape[0]
    bq = SB_QBLOCK
    nq = rows // bq
    assert nq <= LANES, "one lane per key block"
    npair = SB_WIDTH // LANES
    scale = SB_HEAD_DIM ** -0.5

    npp = SB_PAIRS_FWD
    wq = npp * LANES
    grid = (npair // npp, nq)
    ns = side.n if side else 0

    def body(q_ref, k_ref, v_ref, *rest):
        o_ref, runs_ref = rest[ns], rest[ns + 1]
        if side:
            side.run(rest[:ns], rest[ns + 2:2 * ns + 2], rest[2 * ns + 2:], *_grid_ends(grid),
                     lambda: compute(q_ref, k_ref, v_ref, o_ref, runs_ref))
        else:
            compute(q_ref, k_ref, v_ref, o_ref, runs_ref)

    def compute(q_ref, k_ref, v_ref, o_ref, runs_ref):
        qi = pl.program_id(1)
        pairs = [slice(pp * LANES, (pp + 1) * LANES) for pp in range(npp)]
        qst = [(_stack_heads(q_ref[:, s]) * scale).astype(bf16) for s in pairs]
        r = lax.broadcasted_iota(jnp.int32, (bq, 2 * bq), 0)
        c = lax.broadcasted_iota(jnp.int32, (bq, 2 * bq), 1)
        m2 = jnp.logical_or(r > c, c >= bq).astype(bf16)
        t_pos = qi * bq + lax.broadcasted_iota(jnp.int32, (bq, 1), 0)
        t_pos2 = jnp.concatenate([t_pos, t_pos], axis=0)
        lane = lax.broadcasted_iota(jnp.int32, (1, LANES), 1)
        runs_ref[...] = jnp.full(runs_ref.shape, SB_NEVER, f32)

        def tiles(kbs, carry, masked):
            jobs = [(pp, kb) for kb in kbs for pp in range(npp)]
            rows_k = [pl.ds(pl.multiple_of(kb * bq, bq), bq) for _, kb in jobs]
            zl = [_sb_logits(qst[pp], k_ref[rk, pairs[pp]], t_pos2, kb, bq, masked) for (pp, kb), rk in zip(jobs, rows_k)]
            cs = [_split_dot(l, m2) for _, l, _ in zl]
            run = [cr[0] for cr in carry]
            acc = [cr[1] for cr in carry]
            probs = []
            for (pp, kb), (z, l, causal), cs2 in zip(jobs, zl, cs):
                a = jnp.exp(z + l + cs2[:, :bq] + run[pp])
                if masked:
                    a = jnp.where(causal, a, 0.0)
                probs.append(a.astype(bf16))
                for hh in range(2):
                    cols = slice((2 * pp + hh) * LANES, (2 * pp + hh + 1) * LANES)
                    runs_ref[:, cols] = jnp.where(lane == kb, run[pp][hh * bq:(hh + 1) * bq], runs_ref[:, cols])
                run[pp] = run[pp] + cs2[:, bq:]
            for (pp, kb), rk, ab in zip(jobs, rows_k, probs):
                v = v_ref[rk, pairs[pp]]
                acc[pp] = acc[pp] + _by_head(jnp.dot(ab[:bq], v, preferred_element_type=f32),
                                             jnp.dot(ab[bq:], v, preferred_element_type=f32))
            return tuple(zip(run, acc))

        zero = (jnp.zeros((2 * bq, bq), f32), jnp.zeros((bq, LANES), f32))
        carry = tiles([qi], (zero,) * npp, True)

        def alive(cr):
            return functools.reduce(jnp.maximum, [jnp.max(run) for run, _ in cr]) > SB_DEAD

        def two(state):
            i, _, cr = state
            cr = tiles([qi - 1 - 2 * i, qi - 2 - 2 * i], cr, False)
            return i + 1, alive(cr), cr

        n_two = qi // 2
        i_end, still, carry = lax.while_loop(lambda st: jnp.logical_and(st[0] < n_two, st[1]), two,
                                             (jnp.int32(0), alive(carry), carry))
        last_one = jnp.logical_and(qi % 2 == 1, jnp.logical_and(still, i_end == n_two))
        carry = lax.cond(last_one, lambda cr: tiles([0], cr, False), lambda cr: cr, carry)
        for pp in range(npp):
            o_ref[:, pairs[pp]] = carry[pp][1]

    kv = pl.BlockSpec((rows, wq), lambda p, i: (0, p))
    return pl.pallas_call(
        body, name="sb_attn_fwd", grid=grid,
        in_specs=[pl.BlockSpec((bq, wq), lambda p, i: (i, OFF_SBQ // wq + p)), kv, kv] + (side.specs() if side else []),
        out_specs=[pl.BlockSpec((bq, wq), lambda p, i: (i, p)),
                   pl.BlockSpec((bq, 2 * wq), lambda p, i: (i, p))] + (side.specs() if side else []),
        out_shape=[jax.ShapeDtypeStruct((rows, SB_WIDTH), f32),
                   jax.ShapeDtypeStruct((rows, SB_HEADS * LANES), f32)] + (side.out_shapes if side else []),
        scratch_shapes=_comm_scratch(ns) if side else [],
        compiler_params=_cparams(("arbitrary", "arbitrary")),
    )(proj, k16, v16, *(side.arrs if side else []))


def _sb_attention_bwd2(proj, k16, v16, runs, do, side=None):
    rows = proj.shape[0]
    bq = SB_QBLOCK
    nq = rows // bq
    npair = SB_WIDTH // LANES
    scale = SB_HEAD_DIM ** -0.5
    tn = (((0,), (0,)), ((), ()))
    nt = (((1,), (1,)), ((), ()))

    npp = SB_PAIRS_BWD
    wq = npp * LANES
    grid = (npair // npp, nq)
    ns = side.n if side else 0

    def body(q_ref, k_ref, v_ref, runs_ref, do_ref, *rest):
        outs, accs = rest[ns:ns + 3], rest[-2:]
        ins = (q_ref, k_ref, v_ref, runs_ref, do_ref)
        if side:
            side.run(rest[:ns], rest[ns + 3:2 * ns + 3], rest[2 * ns + 3:-2], *_grid_ends(grid),
                     lambda: compute(*ins, *outs, *accs))
        else:
            compute(*ins, *outs, *accs)

    def compute(q_ref, k_ref, v_ref, runs_ref, do_ref, dq_ref, dk_out, dv_out, dk_ref, dv_ref):
        qi = pl.program_id(1)

        @pl.when(qi == 0)
        def _():
            dk_ref[...] = jnp.zeros_like(dk_ref)
            dv_ref[...] = jnp.zeros_like(dv_ref)

        pairs = [slice(pp * LANES, (pp + 1) * LANES) for pp in range(npp)]
        qst = [(_stack_heads(q_ref[:, s]) * scale).astype(bf16) for s in pairs]
        dost = [_stack_heads(do_ref[:, s]).astype(bf16) for s in pairs]
        runs = [jnp.concatenate([runs_ref[:, 2 * pp * LANES:(2 * pp + 1) * LANES],
                                 runs_ref[:, (2 * pp + 1) * LANES:(2 * pp + 2) * LANES]], axis=0) for pp in range(npp)]
        r = lax.broadcasted_iota(jnp.int32, (bq, 2 * bq), 0)
        c = lax.broadcasted_iota(jnp.int32, (bq, 2 * bq), 1)
        suffix_m = _suffix_matrix(bq)
        m2 = jnp.logical_or(r < c, c >= bq).astype(bf16)
        t_pos = qi * bq + lax.broadcasted_iota(jnp.int32, (bq, 1), 0)
        t_pos2 = jnp.concatenate([t_pos, t_pos], axis=0)
        lane = lax.broadcasted_iota(jnp.int32, (1, LANES), 1)

        def tiles(kbs, carry, masked):
            jobs = [(pp, kb) for kb in kbs for pp in range(npp)]
            rows_k = [pl.ds(pl.multiple_of(kb * bq, bq), bq) for _, kb in jobs]
            zl = [_sb_logits(qst[pp], k_ref[rk, pairs[pp]], t_pos2, kb, bq, masked) for (pp, kb), rk in zip(jobs, rows_k)]
            das = [lax.dot_general(dost[pp], v_ref[rk, pairs[pp]], nt, preferred_element_type=f32)
                   for (pp, kb), rk in zip(jobs, rows_k)]
            sticks = [_split_dot(l, suffix_m) for _, l, _ in zl]
            probs, ps = [], []
            for (pp, kb), (z, l, causal), stick, da in zip(jobs, zl, sticks, das):
                run = jnp.sum(jnp.where(lane == kb, runs[pp], 0.0), axis=1, keepdims=True)
                a = jnp.exp(z + l + stick + run)
                if masked:
                    a = jnp.where(causal, a, 0.0)
                probs.append(a.astype(bf16))
                ps.append(da * a)
            pcs = [_split_dot(p, m2) for p in ps]
            pref = [cr[0] for cr in carry]
            dq_acc = [cr[1] for cr in carry]
            dzs = []
            for (pp, kb), (z, l, causal), p, pc2 in zip(jobs, zl, ps, pcs):
                dz = p * jnp.exp(l) - jnp.exp(z + l) * (pc2[:, :bq] + pref[pp])
                if masked:
                    dz = jnp.where(causal, dz, 0.0)
                dzs.append(dz.astype(bf16))
                pref[pp] = pref[pp] + pc2[:, bq:]
            for (pp, kb), rk, dzb, ab in zip(jobs, rows_k, dzs, probs):
                cols = pairs[pp]
                k = k_ref[rk, cols]
                dq_acc[pp] = dq_acc[pp] + _by_head(jnp.dot(dzb[:bq], k, preferred_element_type=f32),
                                                   jnp.dot(dzb[bq:], k, preferred_element_type=f32))
                dk_ref[rk, cols] += lax.dot_general(dzb, qst[pp], tn, preferred_element_type=f32)
                dv_ref[rk, cols] += lax.dot_general(ab, dost[pp], tn, preferred_element_type=f32)
            return tuple(zip(pref, dq_acc))

        zero = (jnp.zeros((2 * bq, bq), f32), jnp.zeros((bq, LANES), f32))
        colmax = functools.reduce(jnp.maximum, [jnp.max(x, axis=0, keepdims=True) for x in runs])
        live = jnp.logical_and(colmax > SB_DEAD, lane < qi)
        kb0 = jnp.minimum(jnp.min(jnp.where(live, lane, LANES)), qi)
        n_blocks = qi - kb0
        carry = lax.fori_loop(0, n_blocks // 2, lambda i, cr: tiles([kb0 + 2 * i, kb0 + 2 * i + 1], cr, False),
                              (zero,) * npp)
        carry = lax.cond(n_blocks % 2 == 1, lambda cr: tiles([qi - 1], cr, False), lambda cr: cr, carry)
        carry = tiles([qi], carry, True)
        for pp in range(npp):
            dq_ref[:, pairs[pp]] = (carry[pp][1] * scale).astype(dq_ref.dtype)

        @pl.when(qi == nq - 1)
        def _():
            dk_out[...] = dk_ref[...].astype(dk_out.dtype)
            dv_out[...] = dv_ref[...].astype(dv_out.dtype)

    blk = pl.BlockSpec((bq, wq), lambda p, i: (i, p))
    full = pl.BlockSpec((rows, wq), lambda p, i: (0, p), pipeline_mode=pl.Buffered(1))
    return pl.pallas_call(
        body, name="sb_attn_bwd", grid=grid,
        in_specs=[pl.BlockSpec((bq, wq), lambda p, i: (i, OFF_SBQ // wq + p)), full, full,
                  pl.BlockSpec((bq, 2 * wq), lambda p, i: (i, p)), blk] + (side.specs() if side else []),
        out_specs=[blk, full, full] + (side.specs() if side else []),
        out_shape=[jax.ShapeDtypeStruct((rows, SB_WIDTH), bf16)] * 3 + (side.out_shapes if side else []),
        scratch_shapes=(_comm_scratch(ns) if side else []) + [pltpu.VMEM((rows, wq), f32)] * 2,
        compiler_params=_cparams(("arbitrary", "arbitrary")),
    )(proj, k16, v16, runs, do, *(side.arrs if side else []))


def _shift_down(x, prev8, j):
    if j == 0:
        return x
    r = pltpu.roll(x, j, axis=0)
    row8 = lax.broadcasted_iota(jnp.int32, prev8.shape, 0)
    head = jnp.where(row8 < j, pltpu.roll(prev8, j, axis=0), r[0:SUBLANES])
    return jnp.concatenate([head, r[SUBLANES:]], axis=0)


def _shift_up(x, next8, j):
    if j == 0:
        return x
    n = x.shape[0]
    r = pltpu.roll(x, n - j, axis=0)
    row8 = lax.broadcasted_iota(jnp.int32, next8.shape, 0)
    tail = jnp.where(row8 >= SUBLANES - j, pltpu.roll(next8, SUBLANES - j, axis=0), r[n - SUBLANES:n])
    return jnp.concatenate([r[:n - SUBLANES], tail], axis=0)


def _conv(x, prev8, w):
    k_taps = w.shape[0]
    out = x * w[k_taps - 1:k_taps, :]
    for j in range(1, k_taps):
        out = out + _shift_down(x, prev8, j) * w[k_taps - 1 - j:k_taps - j, :]
    return out


def _conv_tiles(rows):
    tr = min(TCONV_R, rows)
    return tr, rows // tr, tr // SUBLANES


def _prev_spec(tc, cb0, r8):
    return pl.BlockSpec((SUBLANES, tc), lambda j, i: (jnp.maximum(i * r8 - 1, 0), cb0 + j))


def _silu(x):
    return x * jax.nn.sigmoid(x)


def _dsilu(x):
    s = jax.nn.sigmoid(x)
    return s * (1.0 + x * (1.0 - s))


def _dn_conv_fwd(proj, w):
    rows = proj.shape[0]
    tr, nr, r8 = _conv_tiles(rows)
    tc = TCONV_C
    cb0 = OFF_DN // tc

    def body(x_ref, p_ref, w_ref, o_ref):
        prev = jnp.where(pl.program_id(1) == 0, 0.0, p_ref[...])
        o_ref[...] = _silu(_conv(x_ref[...], prev, w_ref[...]))

    return pl.pallas_call(
        body, name="dn_conv_fwd", grid=(DN_CONV_CH // tc, nr),
        in_specs=[pl.BlockSpec((tr, tc), lambda j, i: (i, cb0 + j)), _prev_spec(tc, cb0, r8),
                  pl.BlockSpec((DN_CONV_WIDTH, tc), lambda j, i: (0, j))],
        out_specs=pl.BlockSpec((tr, tc), lambda j, i: (i, j)),
        out_shape=jax.ShapeDtypeStruct((rows, DN_CONV_CH), f32),
        compiler_params=_cparams(("parallel", "parallel")),
    )(proj, proj, w)


def _dn_conv_bwd_act(proj, w, dact):
    rows = proj.shape[0]
    tr, nr, r8 = _conv_tiles(rows)
    tc = TCONV_C
    cb0 = OFF_DN // tc

    def body(x_ref, p_ref, w_ref, d_ref, o_ref):
        prev = jnp.where(pl.program_id(1) == 0, 0.0, p_ref[...])
        o_ref[...] = d_ref[...] * _dsilu(_conv(x_ref[...], prev, w_ref[...]))

    return pl.pallas_call(
        body, name="dn_conv_bwd_act", grid=(DN_CONV_CH // tc, nr),
        in_specs=[pl.BlockSpec((tr, tc), lambda j, i: (i, cb0 + j)), _prev_spec(tc, cb0, r8),
                  pl.BlockSpec((DN_CONV_WIDTH, tc), lambda j, i: (0, j)),
                  pl.BlockSpec((tr, tc), lambda j, i: (i, j))],
        out_specs=pl.BlockSpec((tr, tc), lambda j, i: (i, j)),
        out_shape=jax.ShapeDtypeStruct((rows, DN_CONV_CH), f32),
        compiler_params=_cparams(("parallel", "parallel")),
    )(proj, proj, w, dact)


def _ffn_conv_fwd(u_pre, w, b):
    rows = u_pre.shape[0]
    tr, nr, r8 = _conv_tiles(rows)
    tc = TCONV_FF
    nct = D_FF // tc

    def body(xg_ref, pg_ref, xu_ref, pu_ref, wg_ref, wu_ref, bg_ref, bu_ref, o_ref, u_ref):
        first = pl.program_id(1) == 0
        ug = _conv(xg_ref[...], jnp.where(first, 0.0, pg_ref[...]), wg_ref[...]) + bg_ref[...]
        uu = _conv(xu_ref[...], jnp.where(first, 0.0, pu_ref[...]), wu_ref[...]) + bu_ref[...]
        o_ref[...] = (_silu(ug) * uu).astype(o_ref.dtype)
        u_ref[0] = ug
        u_ref[1] = uu

    def x_spec(off):
        return pl.BlockSpec((tr, tc), lambda j, i: (i, off + j))

    def w_spec(k, off):
        return pl.BlockSpec((k, tc), lambda j, i: (0, off + j))

    return pl.pallas_call(
        body, name="ffn_conv_fwd", grid=(nct, nr),
        in_specs=[x_spec(0), _prev_spec(tc, 0, r8), x_spec(nct), _prev_spec(tc, nct, r8),
                  w_spec(FFN_CONV_WIDTH, 0), w_spec(FFN_CONV_WIDTH, nct), w_spec(1, 0), w_spec(1, nct)],
        out_specs=[pl.BlockSpec((tr, tc), lambda j, i: (i, j)), pl.BlockSpec((2, tr, tc), lambda j, i: (0, i, j))],
        out_shape=[jax.ShapeDtypeStruct((rows, D_FF), bf16), jax.ShapeDtypeStruct((2, rows, D_FF), f32)],
        compiler_params=_cparams(("parallel", "parallel")),
    )(u_pre, u_pre, u_pre, u_pre, w, w, b, b)


def _ffn_conv_bwd_act(u, dact):
    rows = dact.shape[0]
    tr, nr, _ = _conv_tiles(rows)
    tc = TCONV_FF
    nct = D_FF // tc

    def body(u_ref, d_ref, du_ref, dbg_ref, dbu_ref):
        first = pl.program_id(1) == 0
        ug = u_ref[0]
        uu = u_ref[1]
        d = d_ref[...]
        sig = jax.nn.sigmoid(ug)
        dug = d * uu * (sig * (1.0 + ug * (1.0 - sig)))
        duu = d * (ug * sig)
        du_ref[0] = dug
        du_ref[1] = duu

        @pl.when(first)
        def _():
            dbg_ref[...] = jnp.zeros_like(dbg_ref)
            dbu_ref[...] = jnp.zeros_like(dbu_ref)

        dbg_ref[...] += jnp.sum(dug, axis=0, keepdims=True)
        dbu_ref[...] += jnp.sum(duu, axis=0, keepdims=True)

    pair = pl.BlockSpec((2, tr, tc), lambda j, i: (0, i, j))
    vec = pl.BlockSpec((1, tc), lambda j, i: (0, j))
    return pl.pallas_call(
        body, name="ffn_conv_bwd_act", grid=(nct, nr),
        in_specs=[pair, pl.BlockSpec((tr, tc), lambda j, i: (i, j))],
        out_specs=[pair, vec, vec],
        out_shape=[jax.ShapeDtypeStruct((2, rows, D_FF), f32),
                   jax.ShapeDtypeStruct((1, D_FF), f32), jax.ShapeDtypeStruct((1, D_FF), f32)],
        compiler_params=_cparams(("parallel", "arbitrary")),
    )(u, dact)


def _conv_bwd(dy, x, x_cb0, w, name):
    k_taps = w.shape[0]
    split = dy.ndim == 3
    rows = dy.shape[-2]
    ch = dy.shape[-1] * (2 if split else 1)
    tc = TCONV_FF if split else TCONV_C
    tr, nr, r8 = _conv_tiles(rows)
    per_half = dy.shape[-1] // tc
    last8 = rows // SUBLANES - 1

    def body(dy_ref, nx_ref, x_ref, w_ref, dx_ref, dw_ref):
        i = pl.program_id(1)
        dyv = dy_ref[...]
        nxt = jnp.where(i == nr - 1, 0.0, nx_ref[...])
        xv = x_ref[...].astype(f32)
        wv = w_ref[...]

        @pl.when(i == 0)
        def _():
            dw_ref[...] = jnp.zeros_like(dw_ref)

        dx = dyv * wv[k_taps - 1:k_taps, :]
        dw_ref[k_taps - 1:k_taps, :] += jnp.sum(dyv * xv, axis=0, keepdims=True)
        for j in range(1, k_taps):
            dy_j = _shift_up(dyv, nxt, j)
            dx = dx + dy_j * wv[k_taps - 1 - j:k_taps - j, :]
            dw_ref[k_taps - 1 - j:k_taps - j, :] += jnp.sum(dy_j * xv, axis=0, keepdims=True)
        dx_ref[...] = dx.astype(dx_ref.dtype)

    tile = pl.BlockSpec((tr, tc), lambda j, i: (i, j))
    if split:
        dy_spec = pl.BlockSpec((None, tr, tc), lambda j, i: (j // per_half, i, j % per_half))
        next_spec = pl.BlockSpec((None, SUBLANES, tc),
                                 lambda j, i: (j // per_half, jnp.minimum((i + 1) * r8, last8), j % per_half))
    else:
        dy_spec = tile
        next_spec = pl.BlockSpec((SUBLANES, tc), lambda j, i: (jnp.minimum((i + 1) * r8, last8), j))
    return pl.pallas_call(
        body, name=name, grid=(ch // tc, nr),
        in_specs=[dy_spec, next_spec, pl.BlockSpec((tr, tc), lambda j, i: (i, x_cb0 + j)),
                  pl.BlockSpec((k_taps, tc), lambda j, i: (0, j))],
        out_specs=[tile, pl.BlockSpec((k_taps, tc), lambda j, i: (0, j))],
        out_shape=[jax.ShapeDtypeStruct((rows, ch), bf16), jax.ShapeDtypeStruct((k_taps, ch), f32)],
        compiler_params=_cparams(("parallel", "arbitrary")),
    )(dy, dy, x, w)


def _hdot(a, b):
    return jnp.dot(a, b, preferred_element_type=f32, precision=lax.Precision.HIGH)


def _xdot(a, b):
    return jnp.dot(a, b, preferred_element_type=f32, precision=lax.Precision.HIGHEST)


def _bdot(a, b):
    return jnp.dot(a.astype(bf16), b.astype(bf16), preferred_element_type=f32)


def _bdot_nt(a, b):
    return lax.dot_general(a.astype(bf16), b.astype(bf16), (((1,), (1,)), ((), ())), preferred_element_type=f32)


def _bdot_tn(a, b):
    return lax.dot_general(a.astype(bf16), b.astype(bf16), (((0,), (0,)), ((), ())), preferred_element_type=f32)


GDN_GROUP = 4
GDN_NGROUPS = DN_HEADS // GDN_GROUP
GDN_ROWS = GDN_GROUP * DN_CHUNK
GDN_QK_LANES = GDN_GROUP * DN_KEY_DIM
GDN_LOGIT_LANE = DN_HEADS


def _inverse_impl(lows):
    n = lows[0].shape[0]
    r = lax.broadcasted_iota(jnp.int32, (n, n), 0)
    c = lax.broadcasted_iota(jnp.int32, (n, n), 1)
    eye = (r == c).astype(f32)
    blk = jnp.right_shift(r, 3) == jnp.right_shift(c, 3)
    d = [jnp.where(blk, low, 0.0) for low in lows]
    e = [low - x for low, x in zip(lows, d)]

    def nilpotent8_inverse(xs):
        acc = [eye - x for x in xs]
        power = xs
        for _ in range(2):
            power = [_bdot(x, x) for x in power]
            acc = [_bdot(a, eye + x) for a, x in zip(acc, power)]
        return acc

    dinv = nilpotent8_inverse(d)
    ninv = nilpotent8_inverse([_bdot(x, y) for x, y in zip(dinv, e)])
    t = [_bdot(x, y) for x, y in zip(ninv, dinv)]
    for _ in range(2):
        res = [eye - x - _hdot(low, x) for low, x in zip(lows, t)]
        t = [x + _bdot(x, y) for x, y in zip(t, res)]
    return tuple(t)


@jax.custom_vjp
def _unit_lower_inverses(lows):
    return _inverse_impl(lows)


def _unit_lower_inverses_fwd(lows):
    t = _inverse_impl(lows)
    return t, t


def _unit_lower_inverses_bwd(t, ct):
    tn = (((0,), (0,)), ((), ()))
    nt = (((1,), (1,)), ((), ()))
    left = [lax.dot_general(x, g, tn, preferred_element_type=f32, precision=lax.Precision.HIGH) for x, g in zip(t, ct)]
    return (tuple(-lax.dot_general(x, y, nt, preferred_element_type=f32, precision=lax.Precision.HIGH)
                  for x, y in zip(left, t)),)


_unit_lower_inverses.defvjp(_unit_lower_inverses_fwd, _unit_lower_inverses_bwd)


@jax.custom_vjp
def _known_inverses(lows, t):
    return t


def _known_inverses_fwd(lows, t):
    return t, t


def _known_inverses_bwd(t, ct):
    return _unit_lower_inverses_bwd(t, ct) + (tuple(jnp.zeros_like(x) for x in t),)


_known_inverses.defvjp(_known_inverses_fwd, _known_inverses_bwd)


def _gdn_chunk(a_log, dt_bias, norm_w, ba, *per_group, inverses=None, keep_inverses=False):
    ng = GDN_NGROUPS
    qgs, kgs, vsts, zsts, states = [per_group[i * ng:(i + 1) * ng] for i in range(5)]
    groups = range(ng)
    n = GDN_ROWS
    r = lax.broadcasted_iota(jnp.int32, (n, n), 0)
    c = lax.broadcasted_iota(jnp.int32, (n, n), 1)
    same_head = jnp.right_shift(r, 6) == jnp.right_shift(c, 6)
    incl = jnp.logical_and(same_head, r >= c)
    strict = jnp.logical_and(same_head, r > c)
    eye = (r == c).astype(f32)
    ones = jnp.ones((n, n), f32)
    own_lanes = same_head.astype(f32)
    lane = lax.broadcasted_iota(jnp.int32, (1, LANES), 1)
    pick = lambda arr, idx: jnp.sum(jnp.where(lane == idx, arr, 0.0), axis=1, keepdims=True)
    heads = [[GDN_GROUP * g + h for h in range(GDN_GROUP)] for g in groups]
    rc = lax.broadcasted_iota(jnp.int32, (DN_CHUNK, DN_CHUNK), 0)
    cc = lax.broadcasted_iota(jnp.int32, (DN_CHUNK, DN_CHUNK), 1)

    g_all = -jnp.exp(a_log) * _softplus(ba + dt_bias)
    gc_all = _xdot((rc >= cc).astype(f32), g_all)
    gl_all = jnp.sum(g_all, axis=0, keepdims=True)
    beta = [jnp.concatenate([jax.nn.sigmoid(pick(ba, hd)) for hd in heads[g]], axis=0) for g in groups]
    gc = [jnp.concatenate([pick(gc_all, GDN_LOGIT_LANE + hd) for hd in heads[g]], axis=0) for g in groups]
    g_last = [jnp.concatenate([jnp.broadcast_to(pick(gl_all, GDN_LOGIT_LANE + hd), (DN_CHUNK, 1)) for hd in heads[g]],
                              axis=0) for g in groups]
    gr = [jnp.broadcast_to(gc[g], (n, n)).T for g in groups]
    decay = [jnp.where(incl, jnp.exp(jnp.where(incl, gc[g] - gr[g], 0.0)), 0.0) for g in groups]
    q = [jnp.concatenate([qgs[g]] * GDN_GROUP, axis=0) * own_lanes for g in groups]
    k = [jnp.concatenate([kgs[g]] * GDN_GROUP, axis=0) * own_lanes for g in groups]
    qn = [x * lax.rsqrt(jnp.sum(x * x, axis=1, keepdims=True) + L2_EPS) * (DN_KEY_DIM ** -0.5) for x in q]
    kn = [x * lax.rsqrt(jnp.sum(x * x, axis=1, keepdims=True) + L2_EPS) for x in k]
    kb = [kn[g] * beta[g] for g in groups]
    low = [jnp.where(strict, _bdot_nt(kb[g], kn[g]) * decay[g], 0.0) for g in groups]
    intra = [jnp.where(incl, _bdot_nt(qn[g], kn[g]) * decay[g], 0.0) for g in groups]
    t = _unit_lower_inverses(tuple(low)) if inverses is None else _known_inverses(tuple(low), tuple(inverses))
    u = [_bdot(t[g], vsts[g] * beta[g]) for g in groups]
    w = [_bdot(t[g], kb[g] * jnp.exp(gc[g])) for g in groups]
    sb = [s.astype(bf16) for s in states]
    v_new = [u[g] - jnp.dot(w[g].astype(bf16), sb[g], preferred_element_type=f32) for g in groups]
    o = [jnp.dot((qn[g] * jnp.exp(gc[g])).astype(bf16), sb[g], preferred_element_type=f32) for g in groups]
    o = [o[g] + _bdot(intra[g], v_new[g]) for g in groups]
    new_state = [states[g] * jnp.exp(g_last[g]) + _bdot_tn(kn[g] * jnp.exp(g_last[g] - gc[g]), v_new[g])
                 for g in groups]
    o_n = [x * lax.rsqrt(jnp.mean(x * x, axis=1, keepdims=True) + NORM_EPS) * norm_w for x in o]
    return tuple(o_n[g] * _silu(zsts[g]) for g in groups) + tuple(new_state) + (tuple(t) if keep_inverses else ())


GDN_FWD_CHUNKS = 2
GDN_BWD_CHUNKS = 1


def _gdn_specs(rows, reverse, nc):
    tr = nc * DN_CHUNK
    n = rows // tr
    idx = (lambda i: n - 1 - i) if reverse else (lambda i: i)
    vec = pl.BlockSpec((1, LANES), lambda i: (0, 0))
    qkv = pl.BlockSpec((tr, DN_CONV_CH), lambda i: (idx(i), 0))
    z = pl.BlockSpec((tr, DN_V_WIDTH), lambda i: (idx(i), OFF_Z // DN_V_WIDTH))
    ba = pl.BlockSpec((tr, LANES), lambda i: (idx(i), 0))
    wide = pl.BlockSpec((tr, DN_V_WIDTH), lambda i: (idx(i), 0))
    st = pl.BlockSpec((nc, DN_HEADS * DN_KEY_DIM, LANES), lambda i: (idx(i), 0, 0))
    inv = pl.BlockSpec((nc, GDN_NGROUPS * GDN_ROWS, GDN_ROWS), lambda i: (idx(i), 0, 0))
    return n, vec, qkv, z, ba, wide, st, inv


def _chunk_rows(c):
    return slice(c * DN_CHUNK, (c + 1) * DN_CHUNK)


def _gdn_slices(grp):
    q = slice(grp * GDN_QK_LANES, (grp + 1) * GDN_QK_LANES)
    k = slice(DN_QK_WIDTH + grp * GDN_QK_LANES, DN_QK_WIDTH + (grp + 1) * GDN_QK_LANES)
    heads = [slice((GDN_GROUP * grp + h) * LANES, (GDN_GROUP * grp + h + 1) * LANES) for h in range(GDN_GROUP)]
    vs = [slice(2 * DN_QK_WIDTH + s.start, 2 * DN_QK_WIDTH + s.stop) for s in heads]
    return q, k, vs, heads


def _stack_cols(ref, rows, cols):
    return jnp.concatenate([ref[rows, s] for s in cols], axis=0)


def _gdn_operands(qkv_ref, z_ref, rows, state_rows):
    sl = [_gdn_slices(grp) for grp in range(GDN_NGROUPS)]
    return ([qkv_ref[rows, q] for q, _, _, _ in sl] + [qkv_ref[rows, k] for _, k, _, _ in sl]
            + [_stack_cols(qkv_ref, rows, vs) for _, _, vs, _ in sl]
            + [_stack_cols(z_ref, rows, heads) for _, _, _, heads in sl]
            + [state_rows[grp * GDN_ROWS:(grp + 1) * GDN_ROWS, :] for grp in range(GDN_NGROUPS)])


def _gdn_fwd(a_log, dt_bias, norm_w, qkv_act, proj, ba):
    rows = qkv_act.shape[0]
    n, vec, qkv_s, z_s, ba_s, wide, st_s, inv_s = _gdn_specs(rows, False, GDN_FWD_CHUNKS)

    def body(al_ref, dt_ref, nw_ref, qkv_ref, z_ref, ba_ref, o_ref, st_ref, inv_ref, state):
        @pl.when(pl.program_id(0) == 0)
        def _():
            state[...] = jnp.zeros_like(state)

        for c in range(GDN_FWD_CHUNKS):
            tok = _chunk_rows(c)
            st_ref[c] = state[...]
            out = _gdn_chunk(al_ref[...], dt_ref[...], nw_ref[...], ba_ref[tok, :],
                             *_gdn_operands(qkv_ref, z_ref, tok, state), keep_inverses=True)
            for grp in range(GDN_NGROUPS):
                _, _, _, heads = _gdn_slices(grp)
                rs = slice(grp * GDN_ROWS, (grp + 1) * GDN_ROWS)
                for h, s in enumerate(heads):
                    o_ref[tok, s] = out[grp][h * DN_CHUNK:(h + 1) * DN_CHUNK].astype(o_ref.dtype)
                state[rs, :] = out[GDN_NGROUPS + grp]
                inv_ref[c, rs, :] = out[2 * GDN_NGROUPS + grp]

    n_chunks = rows // DN_CHUNK
    return pl.pallas_call(
        body, name="gdn_fwd", grid=(n,),
        in_specs=[vec, vec, vec, qkv_s, z_s, ba_s], out_specs=[wide, st_s, inv_s],
        out_shape=[jax.ShapeDtypeStruct((rows, DN_V_WIDTH), bf16),
                   jax.ShapeDtypeStruct((n_chunks, DN_HEADS * DN_KEY_DIM, LANES), f32),
                   jax.ShapeDtypeStruct((n_chunks, GDN_NGROUPS * GDN_ROWS, GDN_ROWS), f32)],
        scratch_shapes=[pltpu.VMEM((DN_HEADS * DN_KEY_DIM, LANES), f32)],
        compiler_params=_cparams(("arbitrary",)),
    )(a_log, dt_bias, norm_w, qkv_act, proj, ba)


def _gdn_bwd(a_log, dt_bias, norm_w, qkv_act, proj, ba, states, inverses, do):
    rows = qkv_act.shape[0]
    n, vec, qkv_s, z_s, ba_s, wide, st_s, inv_s = _gdn_specs(rows, True, GDN_BWD_CHUNKS)

    def body(al_ref, dt_ref, nw_ref, qkv_ref, z_ref, ba_ref, st_ref, inv_ref, do_ref,
             dal_ref, ddt_ref, dnw_ref, dqkv_ref, dz_ref, dba_ref, dstate):
        @pl.when(pl.program_id(0) == 0)
        def _():
            dstate[...] = jnp.zeros_like(dstate)
            dal_ref[...] = jnp.zeros_like(dal_ref)
            ddt_ref[...] = jnp.zeros_like(ddt_ref)
            dnw_ref[...] = jnp.zeros_like(dnw_ref)

        ng = GDN_NGROUPS
        for c in reversed(range(GDN_BWD_CHUNKS)):
            tok = _chunk_rows(c)
            kept = [inv_ref[c, grp * GDN_ROWS:(grp + 1) * GDN_ROWS, :] for grp in range(ng)]
            _, vjp = jax.vjp(functools.partial(_gdn_chunk, inverses=kept), al_ref[...], dt_ref[...], nw_ref[...],
                             ba_ref[tok, :], *_gdn_operands(qkv_ref, z_ref, tok, st_ref[c]))
            cts = tuple(_stack_cols(do_ref, tok, _gdn_slices(grp)[3]) for grp in range(ng))
            cts += tuple(dstate[grp * GDN_ROWS:(grp + 1) * GDN_ROWS, :] for grp in range(ng))
            grads = vjp(cts)
            dal_ref[...] += grads[0]
            ddt_ref[...] += grads[1]
            dnw_ref[...] += grads[2]
            dba_ref[tok, :] = grads[3]
            dqs, dks, dvs, dzs, dss = [grads[4 + i * ng:4 + (i + 1) * ng] for i in range(5)]
            for grp in range(ng):
                q, k, vs, heads = _gdn_slices(grp)
                dqkv_ref[tok, q] = dqs[grp]
                dqkv_ref[tok, k] = dks[grp]
                for h, (sv, sh) in enumerate(zip(vs, heads)):
                    rows_h = slice(h * DN_CHUNK, (h + 1) * DN_CHUNK)
                    dqkv_ref[tok, sv] = dvs[grp][rows_h]
                    dz_ref[tok, sh] = dzs[grp][rows_h].astype(dz_ref.dtype)
                dstate[grp * GDN_ROWS:(grp + 1) * GDN_ROWS, :] = dss[grp]

    return pl.pallas_call(
        body, name="gdn_bwd", grid=(n,),
        in_specs=[vec, vec, vec, qkv_s, z_s, ba_s, st_s, inv_s, wide],
        out_specs=[vec, vec, vec, qkv_s, wide, ba_s],
        out_shape=[jax.ShapeDtypeStruct((1, LANES), f32)] * 3
        + [jax.ShapeDtypeStruct((rows, DN_CONV_CH), f32), jax.ShapeDtypeStruct((rows, DN_V_WIDTH), bf16),
           jax.ShapeDtypeStruct((rows, LANES), f32)],
        scratch_shapes=[pltpu.VMEM((DN_HEADS * DN_KEY_DIM, LANES), f32)],
        compiler_params=_cparams(("arbitrary",)),
    )(a_log, dt_bias, norm_w, qkv_act, proj, ba, states, inverses, do)


def _ada_fwd(c_all, w_loc, b_loc):
    def body(c_ref, w_ref, b_ref, o_ref):
        o_ref[...] = _bdot(_silu(c_ref[...]), w_ref[...]) + b_ref[...]

    return pl.pallas_call(body, name="ada_fwd", out_shape=jax.ShapeDtypeStruct((c_all.shape[0], w_loc.shape[1]), f32),
                          compiler_params=_cparams())(c_all, w_loc, b_loc)


def _ada_bwd(c_all, dmod_cols):
    def body(c_ref, d_ref, o_ref):
        o_ref[...] = _bdot_tn(_silu(c_ref[...]), d_ref[...])

    return pl.pallas_call(body, name="ada_bwd",
                          out_shape=jax.ShapeDtypeStruct((c_all.shape[1], dmod_cols.shape[1]), f32),
                          compiler_params=_cparams())(c_all, dmod_cols)


def _sum_devices(parts):
    def body(p_ref, o_ref):
        acc = p_ref[0:1, :]
        for d in range(1, N_DEV):
            acc = acc + p_ref[d:d + 1, :]
        o_ref[...] = acc

    return pl.pallas_call(body, name="sum_small", out_shape=jax.ShapeDtypeStruct((1, parts.shape[1]), f32),
                          compiler_params=_cparams())(parts)


def _adam_math(w, g, m, v):
    m2 = ADAM_B1 * m + (1.0 - ADAM_B1) * g
    v2 = ADAM_B2 * v + (1.0 - ADAM_B2) * jnp.square(g)
    m_hat = m2 / (1.0 - ADAM_B1 ** ADAM_STEP)
    v_hat = v2 / (1.0 - ADAM_B2 ** ADAM_STEP)
    delta = -ADAM_LR * (m_hat / (jnp.sqrt(v_hat) + ADAM_EPS) + ADAM_WD * w)
    return delta, m2, v2


def _row_tile(rows):
    return _pick(rows, (256, 128, 64, 32, 16, 8))


def _adamw(w, g, m, v, name):
    rows, cols = w.shape
    tr = _row_tile(rows)

    def body(w_ref, g_ref, m_ref, v_ref, d_ref, m2_ref, v2_ref):
        d_ref[...], m2_ref[...], v2_ref[...] = _adam_math(w_ref[...], g_ref[...], m_ref[...], v_ref[...])

    tile = pl.BlockSpec((tr, cols), lambda i: (i, 0))
    return pl.pallas_call(body, name=name, grid=(rows // tr,), in_specs=[tile] * 4, out_specs=[tile] * 3,
                          out_shape=[jax.ShapeDtypeStruct(w.shape, f32)] * 3,
                          compiler_params=_cparams(("parallel",)))(w, g, m, v)


def _sum_adamw(parts, w, m, v, name):
    rows, cols = w.shape
    tr = _row_tile(rows)

    def body(p_ref, w_ref, m_ref, v_ref, g_ref, d_ref, m2_ref, v2_ref):
        g = p_ref[0].astype(f32)
        for d in range(1, N_DEV):
            g = g + p_ref[d].astype(f32)
        g_ref[...] = g
        d_ref[...], m2_ref[...], v2_ref[...] = _adam_math(w_ref[...], g, m_ref[...], v_ref[...])

    tile = pl.BlockSpec((tr, cols), lambda i: (i, 0))
    return pl.pallas_call(body, name=name, grid=(rows // tr,),
                          in_specs=[pl.BlockSpec((N_DEV, tr, cols), lambda i: (0, i, 0)), tile, tile, tile],
                          out_specs=[tile] * 4, out_shape=[jax.ShapeDtypeStruct(w.shape, f32)] * 4,
                          compiler_params=_cparams(("parallel",)))(parts, w, m, v)


def _pad_lanes(a, width):
    return jnp.pad(a, ((0, 0), (0, width - a.shape[1])))


def _cols_by_device(full):
    r, c = full.shape
    return jnp.moveaxis(full.reshape(r, N_DEV, c // N_DEV), 1, 0)


def _cols_from_devices(parts):
    d, r, n = parts.shape
    return jnp.moveaxis(parts, 0, 1).reshape(r, d * n)


def kernel(x, c, w_ada, b_ada, norm1_w, w_in, dn_conv_w, dn_A_log, dn_dt_bias, dn_norm_w, w_proj_sb, w_proj_dn, w_out, norm2_w, w_ffn_in, ffn_conv_w, ffn_conv_b, w_ffn_out, final_norm_w, loss_target, m_w_ada, m_b_ada, m_norm1_w, m_w_in, m_dn_conv_w, m_dn_A_log, m_dn_dt_bias, m_dn_norm_w, m_w_proj_sb, m_w_proj_dn, m_w_out, m_norm2_w, m_w_ffn_in, m_ffn_conv_w, m_ffn_conv_b, m_w_ffn_out, m_final_norm_w, v_w_ada, v_b_ada, v_norm1_w, v_w_in, v_dn_conv_w, v_dn_A_log, v_dn_dt_bias, v_dn_norm_w, v_w_proj_sb, v_w_proj_dn, v_w_out, v_norm2_w, v_w_ffn_in, v_ffn_conv_w, v_ffn_conv_b, v_w_ffn_out, v_final_norm_w):
    d = D_MODEL
    me = 4 * lax.axis_index("x") + 2 * lax.axis_index("y") + lax.axis_index("c")
    xs = x[0]
    target = loss_target[0]
    n_ada = w_ada.shape[2]
    n_dnc = dn_conv_w.shape[2]
    n_ffc = ffn_conv_w.shape[2]

    small = jnp.concatenate([c, dn_conv_w[0].reshape(1, -1), ffn_conv_w[0].reshape(1, -1)], axis=1)
    small = _pad_lanes(small, -(-small.shape[1] // LANES) * LANES)
    small_g, w_in_g = _all_gather([small, w_in[0].astype(bf16)], "gather_w_in")
    later = [w_proj_sb[0].astype(bf16), w_proj_dn[0].astype(bf16), w_out[0].astype(bf16),
             w_ffn_in[0].astype(bf16), w_ffn_out[0].astype(bf16)]
    gather_later = _SideComm(_gather_protocol, later, _gathered_shapes(later))
    small_g = small_g[:, 0, :]
    c_all = small_g[:, :d]
    dn_cw = _cols_from_devices(small_g[:, d:d + DN_CONV_WIDTH * n_dnc].reshape(N_DEV, DN_CONV_WIDTH, n_dnc))
    o2 = d + DN_CONV_WIDTH * n_dnc
    ffn_cw = _cols_from_devices(small_g[:, o2:o2 + FFN_CONV_WIDTH * n_ffc].reshape(N_DEV, FFN_CONV_WIDTH, n_ffc))

    w_in_full = _cols_from_devices(w_in_g)
    r_sb, r_dn, r_z = 3 * SB_WIDTH, 3 * SB_WIDTH + DN_CONV_CH, 3 * SB_WIDTH + DN_CONV_CH + DN_V_WIDTH
    r_g = r_z + 2 * DN_HEADS
    w_main = jnp.concatenate([w_in_full[:, r_g:], w_in_full[:, r_sb:r_dn], w_in_full[:, r_dn:r_z],
                              w_in_full[:, :r_sb]], axis=1)
    w_ba = _pad_lanes(w_in_full[:, r_z:r_g], LANES)

    b_loc = lax.dynamic_slice(b_ada, (0, me * n_ada), (1, n_ada))
    mod_part = _ada_fwd(c_all, w_ada[0], b_loc)
    (mod_g,) = _all_gather([mod_part], "gather_mod")
    mod = lax.dynamic_index_in_dim(mod_g, me, axis=1, keepdims=False).reshape(1, N_DEV * n_ada)
    shift1, scale1, gate1, shift2, scale2, gate2 = [mod[:, i * d:(i + 1) * d] for i in range(6)]

    logit_lanes = ((0, 0), (GDN_LOGIT_LANE, LANES - GDN_LOGIT_LANE - DN_HEADS))
    a_log = jnp.pad(dn_A_log, logit_lanes)
    dt_b = jnp.pad(dn_dt_bias, logit_lanes)

    (h,) = _stage_fwd(_f_normmod, [norm1_w, shift1, scale1], [xs], [bf16], "norm1_fwd")
    proj = _mm(h, w_main, name="in_proj")
    ba = _mm(h, w_ba, name="in_proj_ba")
    k16, v16 = _sb_prepare(proj)
    o_a, sb_runs, w_psb_g, w_pdn_g, w_out_g, w_fin_g, w_fout_g = _sb_attention_fwd2(
        proj, k16, v16, side=gather_later)
    w_psb = _cols_from_devices(w_psb_g)
    w_pdn = w_pdn_g.reshape(DN_V_WIDTH, d)
    w_o = w_out_g.reshape(d, d)
    w_fin = _cols_from_devices(w_fin_g)
    w_fout = w_fout_g.reshape(D_FF, d)
    qkv_act = _dn_conv_fwd(proj, dn_cw)
    o_b, states, dn_inverses = _gdn_fwd(a_log, dt_b, dn_norm_w, qkv_act, proj, ba)
    pa = _mm(o_a, w_psb, name="proj_sb")
    pb = _mm(o_b, w_pdn, name="proj_dn")
    gates = [(proj, d, OFF_GA // d), (proj, d, OFF_GB // d)]
    (merged,) = _stage_fwd(_f_merge, [], gates + [pa, pb], [bf16], "merge_fwd")
    ao = _mm(merged, w_o, name="out_proj")
    mid_params = [gate1, norm2_w, shift2, scale2]
    x1, h2 = _stage_fwd(_f_residual_normmod, mid_params, [xs, ao], [f32, bf16], "resid1_norm2_fwd")
    u_pre = _mm(h2, w_fin, name="ffn_in")
    act, u_conv = _ffn_conv_fwd(u_pre, ffn_cw, ffn_conv_b)
    fo = _mm(act, w_fout, name="ffn_out")

    loss_p, d_gate2, d_wf, dx2, dfo = _loss_and_grads(gate2, final_norm_w.reshape(1, d), x1, fo, target)
    dact = _mm(dfo, w_fout, tb=True, name="ffn_out_dx")
    g_w_fout = _mm(act, dfo, ta=True, name="ffn_out_dw")
    du, dbg, dbu = _ffn_conv_bwd_act(u_conv, dact)
    du_pre, d_ffn_cw = _conv_bwd(du, u_pre, 0, ffn_cw, "ffn_conv_bwd")
    dh2 = _mm(du_pre, w_fin, tb=True, name="ffn_in_dx")
    g_w_fin = _mm(h2, du_pre, ta=True, name="ffn_in_dw")
    (d_gate1, d_n2w, d_shift2, d_scale2), (dx1, dao) = _stage_bwd(
        _f_residual_normmod, mid_params, [xs, ao], [dx2, dh2], [f32, bf16], "resid1_norm2_bwd")
    dmerged = _mm(dao, w_o, tb=True, name="out_proj_dx")
    g_w_o = _mm(merged, dao, ta=True, name="out_proj_dw")
    _, (dga, dgb, dpa, dpb) = _stage_bwd(_f_merge, [], gates + [pa, pb], [dmerged], [bf16] * 4, "merge_bwd")
    do_a = _mm(dpa, w_psb, tb=True, name="proj_sb_dx")
    g_w_psb = _mm(o_a, dpa, ta=True, name="proj_sb_dw")
    do_b = _mm(dpb, w_pdn, tb=True, name="proj_dn_dx")
    g_w_pdn = _mm(o_b, dpb, ta=True, name="proj_dn_dw")
    early = [_cols_by_device(g_w_psb).astype(bf16),
             g_w_pdn.reshape(N_DEV, DN_V_WIDTH // N_DEV, d).astype(bf16),
             g_w_o.reshape(N_DEV, d // N_DEV, d).astype(bf16),
             _cols_by_device(g_w_fin).astype(bf16),
             g_w_fout.reshape(N_DEV, D_FF // N_DEV, d).astype(bf16)]
    exchange_early = _SideComm(_exchange_protocol, early, [jax.ShapeDtypeStruct(a.shape, a.dtype) for a in early])
    dq, dk, dv, *recv_early = _sb_attention_bwd2(proj, k16, v16, sb_runs, do_a, side=exchange_early)
    d_alog, d_dtb, d_dnw, dqkv_act, dz, dba = _gdn_bwd(a_log, dt_b, dn_norm_w, qkv_act, proj, ba, states,
                                                       dn_inverses, do_b)
    d_conv_out = _dn_conv_bwd_act(proj, dn_cw, dqkv_act)
    d_dn_pre, d_dn_cw = _conv_bwd(d_conv_out, proj, OFF_DN // TCONV_C, dn_cw, "dn_conv_bwd")
    dproj = jnp.concatenate([dga, dgb, d_dn_pre, dz, dq, dk, dv], axis=1)
    g_w_main = _mm(h, dproj, ta=True, name="in_proj_dw")
    g_w_ba = _mm(h, dba, ta=True, name="in_proj_ba_dw")
    g_w_in_full = jnp.concatenate([g_w_main[:, OFF_SBQ:], g_w_main[:, OFF_DN:OFF_Z], g_w_main[:, OFF_Z:OFF_SBQ],
                                   g_w_ba[:, :2 * DN_HEADS], g_w_main[:, :OFF_DN]], axis=1)
    w_in_parts = _cols_by_device(g_w_in_full).astype(bf16)
    exchange_w_in = _SideComm(_exchange_protocol, [w_in_parts], [jax.ShapeDtypeStruct(w_in_parts.shape, bf16)])
    dh, recv_w_in = _mm(dproj, w_main, tb=True, name="in_proj_dx", side=exchange_w_in)
    dh_ba = _mm(dba, w_ba, tb=True, name="in_proj_ba_dx")
    (d_n1w, d_shift1, d_scale1), (grad_x,) = _stage_bwd(
        _f_normmod, [norm1_w, shift1, scale1], [xs], [[dh, dh_ba]], [f32], "norm1_bwd", residual=(0, dx1))

    dmod = jnp.concatenate([d_shift1, d_scale1, d_gate1, d_shift2, d_scale2, d_gate2], axis=1)
    d_ffn_cb = jnp.concatenate([dbg, dbu], axis=1)
    small_parts = jnp.concatenate(
        [loss_p, dmod, d_n1w, d_alog, d_dtb, d_dnw, d_n2w, d_ffn_cb, d_wf,
         d_dn_cw.reshape(1, -1), d_ffn_cw.reshape(1, -1)], axis=1)
    (small_parts_g,) = _all_gather([small_parts], "gather_small_grads")
    tot = _sum_devices(small_parts_g[:, 0, :])
    offs = {}
    pos = 0
    for nm, width in (("loss", LANES), ("b_ada", 6 * d), ("norm1_w", d), ("dn_A_log", LANES), ("dn_dt_bias", LANES),
                      ("dn_norm_w", LANES), ("norm2_w", d), ("ffn_conv_b", 2 * D_FF), ("final_norm_w", d),
                      ("dn_conv_w", DN_CONV_WIDTH * DN_CONV_CH), ("ffn_conv_w", FFN_CONV_WIDTH * 2 * D_FF)):
        offs[nm] = (pos, width)
        pos += width
    seg = lambda nm: tot[:, offs[nm][0]:offs[nm][0] + offs[nm][1]]
    loss = tot[0, 0]
    g_b_ada = seg("b_ada")
    g_norm1 = seg("norm1_w")
    g_alog = seg("dn_A_log")[:, GDN_LOGIT_LANE:GDN_LOGIT_LANE + DN_HEADS]
    g_dtb = seg("dn_dt_bias")[:, GDN_LOGIT_LANE:GDN_LOGIT_LANE + DN_HEADS]
    g_dnw = seg("dn_norm_w")
    g_norm2 = seg("norm2_w")
    g_ffn_cb = seg("ffn_conv_b")
    g_fnw = seg("final_norm_w")
    g_dn_cw = lax.dynamic_slice(seg("dn_conv_w").reshape(DN_CONV_WIDTH, DN_CONV_CH), (0, me * n_dnc),
                                (DN_CONV_WIDTH, n_dnc))
    g_ffn_cw = lax.dynamic_slice(seg("ffn_conv_w").reshape(FFN_CONV_WIDTH, 2 * D_FF), (0, me * n_ffc),
                                 (FFN_CONV_WIDTH, n_ffc))

    dmod_all = small_parts_g[:, 0, offs["b_ada"][0]:offs["b_ada"][0] + 6 * d]
    g_w_ada = _ada_bwd(c_all, lax.dynamic_slice(dmod_all, (0, me * n_ada), (N_DEV, n_ada)))

    def pack(parts):
        flat = [p.reshape(1, -1) for p in parts]
        flat = [_pad_lanes(p, -(-p.shape[1] // LANES) * LANES) for p in flat]
        return jnp.concatenate(flat, axis=1), [p.shape[1] for p in flat]

    small_names_g = [g_b_ada, g_norm1, g_alog, g_dtb, g_dnw, g_norm2, g_ffn_cb, g_fnw, g_dn_cw, g_ffn_cw]
    small_w = [b_ada, norm1_w, dn_A_log, dn_dt_bias, dn_norm_w, norm2_w, ffn_conv_b, final_norm_w, dn_conv_w[0], ffn_conv_w[0]]
    small_m = [m_b_ada, m_norm1_w, m_dn_A_log, m_dn_dt_bias, m_dn_norm_w, m_norm2_w, m_ffn_conv_b, m_final_norm_w, m_dn_conv_w[0], m_ffn_conv_w[0]]
    small_v = [v_b_ada, v_norm1_w, v_dn_A_log, v_dn_dt_bias, v_dn_norm_w, v_norm2_w, v_ffn_conv_b, v_final_norm_w, v_dn_conv_w[0], v_ffn_conv_w[0]]
    pg, widths = pack(small_names_g)
    pw, _ = pack(small_w)
    pm, _ = pack(small_m)
    pv, _ = pack(small_v)
    s_delta, s_m, s_v = _adamw(pw, pg, pm, pv, "adamw_small")

    def unpack(flat):
        out, pos = [], 0
        for ref_arr, width in zip(small_w, widths):
            out.append(flat[:, pos:pos + ref_arr.size].reshape(ref_arr.shape))
            pos += width
        return out

    small_grads = [g.reshape(w_.shape) for g, w_ in zip(small_names_g, small_w)]
    small_delta, small_newm, small_newv = unpack(s_delta), unpack(s_m), unpack(s_v)

    ada_delta, ada_m, ada_v = _adamw(w_ada[0], g_w_ada, m_w_ada[0], v_w_ada[0], "adamw_ada")

    recv = [recv_w_in] + list(recv_early)
    big = {}
    for nm, parts, w_, m_, v_ in (("w_in", recv[0], w_in, m_w_in, v_w_in),
                                  ("w_proj_sb", recv[1], w_proj_sb, m_w_proj_sb, v_w_proj_sb),
                                  ("w_proj_dn", recv[2], w_proj_dn, m_w_proj_dn, v_w_proj_dn),
                                  ("w_out", recv[3], w_out, m_w_out, v_w_out),
                                  ("w_ffn_in", recv[4], w_ffn_in, m_w_ffn_in, v_w_ffn_in),
                                  ("w_ffn_out", recv[5], w_ffn_out, m_w_ffn_out, v_w_ffn_out)):
        big[nm] = [t[None] for t in _sum_adamw(parts, w_[0], m_[0], v_[0], "adamw_" + nm)]

    sg = dict(zip(["b_ada", "norm1_w", "dn_A_log", "dn_dt_bias", "dn_norm_w", "norm2_w", "ffn_conv_b", "final_norm_w",
                   "dn_conv_w", "ffn_conv_w"], range(10)))

    def small_out(table, nm):
        val = table[sg[nm]]
        return val[None] if nm in ("dn_conv_w", "ffn_conv_w") else val

    order = ["w_ada", "b_ada", "norm1_w", "w_in", "dn_conv_w", "dn_A_log", "dn_dt_bias", "dn_norm_w", "w_proj_sb",
             "w_proj_dn", "w_out", "norm2_w", "w_ffn_in", "ffn_conv_w", "ffn_conv_b", "w_ffn_out", "final_norm_w"]
    groups = []
    for k, small_table in enumerate((small_grads, small_delta, small_newm, small_newv)):
        row = []
        for nm in order:
            if nm == "w_ada":
                row.append((g_w_ada, ada_delta, ada_m, ada_v)[k][None])
            elif nm in big:
                row.append(big[nm][k])
            else:
                row.append(small_out(small_table, nm))
        groups.append(row)
    return (loss, grad_x[None], *groups[0], *groups[1], *groups[2], *groups[3])
```

```python
import functools

import jax
import jax.numpy as jnp
from jax import lax
from jax.experimental import pallas as pl
from jax.experimental.pallas import tpu as pltpu

f32 = jnp.float32
bf16 = jnp.bfloat16

D_MODEL = 1024
SB_HEADS = 8
SB_HEAD_DIM = 64
SB_WIDTH = SB_HEADS * SB_HEAD_DIM
SB_QBLOCK = 128
DN_HEADS = 8
DN_KEY_DIM = 64
DN_VAL_DIM = 128
DN_QK_WIDTH = DN_HEADS * DN_KEY_DIM
DN_V_WIDTH = DN_HEADS * DN_VAL_DIM
DN_CONV_CH = 2 * DN_QK_WIDTH + DN_V_WIDTH
DN_CONV_WIDTH = 4
DN_CHUNK = 64
D_FF = 2816
FFN_CONV_WIDTH = 3
NORM_EPS = 1e-6
L2_EPS = 1e-6
ADAM_LR = 0.001
ADAM_B1 = 0.9
ADAM_B2 = 0.999
ADAM_EPS = 1e-08
ADAM_WD = 0.01
ADAM_STEP = 10

N_DEV = 8
MESH = pl.DeviceIdType.MESH

LANES = 128
SUBLANES = 8
VMEM_LIMIT = 48 * 1024 * 1024

OFF_GA = 0
OFF_GB = D_MODEL
OFF_DN = 2 * D_MODEL
OFF_Z = OFF_DN + DN_CONV_CH
OFF_SBQ = OFF_Z + DN_V_WIDTH
OFF_SBK = OFF_SBQ + SB_WIDTH
OFF_SBV = OFF_SBK + SB_WIDTH
MAIN_WIDTH = OFF_SBV + SB_WIDTH

TM = 512
TCONV_R = 512
TCONV_C = 1024
TCONV_FF = D_FF // 2
SB_PAIRS_FWD = 4
SB_PAIRS_BWD = 4
SB_DEAD = -106.0
SB_NEVER = -1e30


def _cparams(sem=None):
    return pltpu.CompilerParams(dimension_semantics=sem, vmem_limit_bytes=VMEM_LIMIT)


def _pick(n, cands):
    for c in cands:
        if n % c == 0:
            return c
    return n


def _my_pos():
    return lax.axis_index("x"), lax.axis_index("y"), lax.axis_index("c")


def _flip(v, bit):
    return 1 - v if bit else v


def _comm_scratch(n):
    return [pltpu.SemaphoreType.DMA((n, 7)), pltpu.SemaphoreType.DMA((n, 7)), pltpu.SemaphoreType.DMA((n,))]


def _gather_protocol(ins, outs, send_sems, recv_sems, local_sems):
    n = len(ins)
    x, y, c = _my_pos()
    me, sibling = (x, y, c), (x, y, 1 - c)
    chips = [(1 - x, y), (x, 1 - y), (1 - x, 1 - y)]

    def slot(out, pos):
        return out.at[4 * pos[0] + 2 * pos[1] + pos[2]]

    def copy(a, k, block, to, src=None):
        return pltpu.make_async_remote_copy(
            src_ref=slot(outs[a], block) if src is None else src, dst_ref=slot(outs[a], block),
            send_sem=send_sems.at[a, k], recv_sem=recv_sems.at[a, k], device_id=to, device_id_type=MESH)

    def local(a):
        return pltpu.make_async_copy(ins[a], slot(outs[a], me), local_sems.at[a])

    def first(a):
        return [copy(a, 0, me, sibling, src=ins[a])] + [copy(a, 1 + j, me, (*chip, c), src=ins[a])
                                                         for j, chip in enumerate(chips)]

    def start():
        for a in range(n):
            local(a).start()
            for cp in first(a):
                cp.start()

    def finish():
        forwards = []
        for a in range(n):
            for j, chip in enumerate(chips):
                copy(a, 1 + j, (*chip, c), me).wait_recv()
                fwd = copy(a, 4 + j, (*chip, c), sibling)
                fwd.start()
                forwards.append(fwd)
        for a in range(n):
            copy(a, 0, sibling, me).wait_recv()
            for j, chip in enumerate(chips):
                copy(a, 4 + j, (*chip, 1 - c), me).wait_recv()
        for a in range(n):
            for cp in first(a):
                cp.wait_send()
        for cp in forwards:
            cp.wait_send()
        for a in range(n):
            local(a).wait()

    return start, finish


def _exchange_protocol(ins, outs, send_sems, recv_sems, local_sems):
    n = len(ins)
    x, y, c = _my_pos()
    me_idx = 4 * x + 2 * y + c

    def local(a):
        return pltpu.make_async_copy(ins[a].at[me_idx], outs[a].at[me_idx], local_sems.at[a])

    def copies(a, m):
        peer = (_flip(x, m & 4), _flip(y, m & 2), _flip(c, m & 1))
        peer_idx = 4 * peer[0] + 2 * peer[1] + peer[2]
        sems = dict(send_sem=send_sems.at[a, m - 1], recv_sem=recv_sems.at[a, m - 1], device_id=peer,
                    device_id_type=MESH)
        send = pltpu.make_async_remote_copy(src_ref=ins[a].at[peer_idx], dst_ref=outs[a].at[me_idx], **sems)
        recv = pltpu.make_async_remote_copy(src_ref=ins[a].at[peer_idx], dst_ref=outs[a].at[peer_idx], **sems)
        return send, recv

    def start():
        for a in range(n):
            for m in reversed(range(1, N_DEV)):
                copies(a, m)[0].start()
            local(a).start()

    def finish():
        for a in range(n):
            for m in range(1, N_DEV):
                copies(a, m)[1].wait_recv()
        for a in range(n):
            for m in range(1, N_DEV):
                copies(a, m)[0].wait_send()
            local(a).wait()

    return start, finish


def _collective_call(protocol, arrs, out_shapes, name):
    n = len(arrs)

    def body(*refs):
        start, finish = protocol(refs[:n], refs[n:2 * n], *refs[2 * n:])
        start()
        finish()

    any_spec = pl.BlockSpec(memory_space=pl.ANY)
    return pl.pallas_call(body, name=name, out_shape=out_shapes, in_specs=[any_spec] * n, out_specs=[any_spec] * n,
                          scratch_shapes=_comm_scratch(n))(*arrs)


def _gathered_shapes(arrs):
    return [jax.ShapeDtypeStruct((N_DEV,) + a.shape, a.dtype) for a in arrs]


def _all_gather(arrs, name):
    return _collective_call(_gather_protocol, arrs, _gathered_shapes(arrs), name)


MM_BLOCK_BYTES = 7 * 1024 * 1024
MM_TILE_CAP = 1664


def _lane_tile(n, cap):
    fits = [t for t in range(LANES, min(n, cap) + 1, LANES) if n % t == 0]
    return max(fits) if fits else n


def _mm_tiles(m_dim, n_dim, k_dim, a_bytes, b_bytes):
    tm = _lane_tile(m_dim, MM_TILE_CAP)
    tn = _lane_tile(n_dim, MM_TILE_CAP)
    while tm * tn * 4 > MM_BLOCK_BYTES:
        if tn >= tm and (tn // 2) % LANES == 0:
            tn //= 2
        else:
            tm //= 2
    if (m_dim % (2 * tm) == 0 and 2 * tm * k_dim * a_bytes <= MM_BLOCK_BYTES
            and 2 * tm * tn * 4 <= MM_BLOCK_BYTES):
        tm *= 2
    tk = k_dim
    if k_dim % LANES == 0:
        units = k_dim // LANES
        fits = [u for u in range(1, units + 1) if units % u == 0
                and u * LANES * max(tm * a_bytes, tn * b_bytes) <= MM_BLOCK_BYTES]
        tk = max(fits) * LANES
    return tm, tn, tk


def _mm(a, b, *, ta=False, tb=False, name, side=None):
    (k_dim, m_dim) = a.shape if ta else a.shape[::-1]
    (n_dim, kb_dim) = b.shape if tb else b.shape[::-1]
    assert k_dim == kb_dim, (a.shape, b.shape, ta, tb)
    tm, tn, tk = _mm_tiles(m_dim, n_dim, k_dim, a.dtype.itemsize, b.dtype.itemsize)
    nk = k_dim // tk
    grid = (m_dim // tm, n_dim // tn, nk)
    dims = (((0 if ta else 1,), (1 if tb else 0,)), ((), ()))
    ns = side.n if side else 0

    def body(a_ref, b_ref, *rest):
        if side:
            side.run(rest[:ns], rest[ns + 1:2 * ns + 1], rest[2 * ns + 1:], *_grid_ends(grid),
                     lambda: compute(a_ref, b_ref, rest[ns]))
        else:
            compute(a_ref, b_ref, rest[0])

    def compute(a_ref, b_ref, o_ref):
        part = lax.dot_general(a_ref[...].astype(bf16), b_ref[...].astype(bf16), dims, preferred_element_type=f32)
        if nk == 1:
            o_ref[...] = part
        else:
            k = pl.program_id(2)

            @pl.when(k == 0)
            def _():
                o_ref[...] = part

            @pl.when(k > 0)
            def _():
                o_ref[...] += part

    a_spec = pl.BlockSpec((tk, tm), lambda i, j, k: (k, i)) if ta else pl.BlockSpec((tm, tk), lambda i, j, k: (i, k))
    b_spec = pl.BlockSpec((tn, tk), lambda i, j, k: (j, k)) if tb else pl.BlockSpec((tk, tn), lambda i, j, k: (k, j))
    out_spec = pl.BlockSpec((tm, tn), lambda i, j, k: (i, j))
    out_shape = jax.ShapeDtypeStruct((m_dim, n_dim), f32)
    if not side:
        return pl.pallas_call(body, name=name, grid=grid, in_specs=[a_spec, b_spec], out_specs=out_spec,
                              out_shape=out_shape,
                              compiler_params=_cparams(("parallel", "parallel", "arbitrary")))(a, b)
    return pl.pallas_call(
        body, name=name, grid=grid, in_specs=[a_spec, b_spec] + side.specs(), out_specs=[out_spec] + side.specs(),
        out_shape=[out_shape] + side.out_shapes, scratch_shapes=_comm_scratch(ns),
        compiler_params=_cparams(("arbitrary", "arbitrary", "arbitrary")))(a, b, *side.arrs)


def _win(t):
    return t if isinstance(t, tuple) else (t, t.shape[1], 0)


def _tile_spec(width, cb, tm):
    return pl.BlockSpec((tm, width), lambda i: (i, cb))


def _param_spec(p):
    return pl.BlockSpec(p.shape, lambda i: (0, 0))


def _stage_fwd(f, params, tiles, out_dtypes, name):
    tiles = [_win(t) for t in tiles]
    rows = tiles[0][0].shape[0]
    tm = min(TM, rows)
    avals = jax.eval_shape(f, *[jax.ShapeDtypeStruct(p.shape, f32) for p in params],
                           *[jax.ShapeDtypeStruct((tm, w), f32) for _, w, _ in tiles])
    n_p, n_t = len(params), len(tiles)

    def body(*refs):
        p = [r[...] for r in refs[:n_p]]
        t = [r[...].astype(f32) for r in refs[n_p:n_p + n_t]]
        for o_ref, val in zip(refs[n_p + n_t:], f(*p, *t)):
            o_ref[...] = val.astype(o_ref.dtype)

    return pl.pallas_call(
        body, name=name, grid=(rows // tm,),
        in_specs=[_param_spec(p) for p in params] + [_tile_spec(w, cb, tm) for _, w, cb in tiles],
        out_specs=[_tile_spec(a.shape[1], 0, tm) for a in avals],
        out_shape=[jax.ShapeDtypeStruct((rows, a.shape[1]), dt) for a, dt in zip(avals, out_dtypes)],
        compiler_params=_cparams(("parallel",)),
    )(*params, *[t[0] for t in tiles])


def _stage_bwd(f, params, tiles, cts, grad_dtypes, name, residual=None):
    tiles = [_win(t) for t in tiles]
    rows = tiles[0][0].shape[0]
    tm = min(TM, rows)
    cts = [list(g) if isinstance(g, (list, tuple)) else [g] for g in cts]
    flat_cts = [a for g in cts for a in g]
    n_p, n_t, n_c = len(params), len(tiles), len(flat_cts)
    has_res = residual is not None
    want = [j for j, dt in enumerate(grad_dtypes) if dt is not None]

    def body(*refs):
        i = pl.program_id(0)
        p = [r[...] for r in refs[:n_p]]
        t = [r[...].astype(f32) for r in refs[n_p:n_p + n_t]]
        ct_vals = [r[...].astype(f32) for r in refs[n_p + n_t:n_p + n_t + n_c]]
        ct, at = [], 0
        for g in cts:
            ct.append(functools.reduce(jnp.add, ct_vals[at:at + len(g)]))
            at += len(g)
        ct = tuple(ct)
        pos = n_p + n_t + n_c
        res_ref = refs[pos] if has_res else None
        pos += 1 if has_res else 0
        dp_refs = refs[pos:pos + n_p]
        dt_refs = refs[pos + n_p:]
        _, vjp = jax.vjp(f, *p, *t)
        grads = vjp(ct)

        @pl.when(i == 0)
        def _():
            for r in dp_refs:
                r[...] = jnp.zeros_like(r)

        for r, g in zip(dp_refs, grads[:n_p]):
            r[...] += g
        for r, j in zip(dt_refs, want):
            g = grads[n_p + j]
            if has_res and j == residual[0]:
                g = g + res_ref[...].astype(f32)
            r[...] = g.astype(r.dtype)

    in_arrays = list(params) + [t[0] for t in tiles] + flat_cts
    in_specs = ([_param_spec(p) for p in params] + [_tile_spec(w, cb, tm) for _, w, cb in tiles]
                + [_tile_spec(c.shape[1], 0, tm) for c in flat_cts])
    if has_res:
        in_arrays.append(residual[1])
        in_specs.append(_tile_spec(residual[1].shape[1], 0, tm))
    out_shape = ([jax.ShapeDtypeStruct(p.shape, f32) for p in params]
                 + [jax.ShapeDtypeStruct((rows, tiles[j][1]), grad_dtypes[j]) for j in want])
    out_specs = [_param_spec(p) for p in params] + [_tile_spec(tiles[j][1], 0, tm) for j in want]
    outs = pl.pallas_call(
        body, name=name, grid=(rows // tm,), in_specs=in_specs, out_specs=out_specs, out_shape=out_shape,
        compiler_params=_cparams(("arbitrary",)),
    )(*in_arrays)
    return outs[:n_p], outs[n_p:]


def _rms(x, w):
    return x * lax.rsqrt(jnp.mean(x * x, axis=-1, keepdims=True) + NORM_EPS) * w


def _f_normmod(w, shift, scale, x):
    return (_rms(x, w) * (1.0 + scale) + shift,)


def _f_merge(ga, gb, pa, pb):
    return (jax.nn.sigmoid(ga) * pa + jax.nn.sigmoid(gb) * pb,)


def _f_residual_normmod(gate, w, shift, scale, x, branch):
    x1 = x + gate * branch
    return x1, _rms(x1, w) * (1.0 + scale) + shift


def _f_loss(gate, wf, x1, fo, target):
    y = _rms(x1 + gate * fo, wf)
    err = jnp.square(y - target)
    return (0.5 * jnp.sum(jnp.mean(err, axis=-1, keepdims=True), axis=0, keepdims=True),)


def _loss_and_grads(gate2, wf, x1, fo, target):
    rows, d = x1.shape
    tm = min(TM, rows)

    def body(g_ref, w_ref, x_ref, fo_ref, t_ref, loss_ref, dg_ref, dw_ref, dx_ref, dfo_ref):
        i = pl.program_id(0)
        (val,), vjp = jax.vjp(_f_loss, g_ref[...], w_ref[...], x_ref[...], fo_ref[...], t_ref[...])
        dg, dw, dx, dfo, _ = vjp((jnp.ones((1, 1), f32),))

        @pl.when(i == 0)
        def _():
            loss_ref[...] = jnp.zeros_like(loss_ref)
            dg_ref[...] = jnp.zeros_like(dg_ref)
            dw_ref[...] = jnp.zeros_like(dw_ref)

        loss_ref[...] += jnp.broadcast_to(val, loss_ref.shape)
        dg_ref[...] += dg
        dw_ref[...] += dw
        dx_ref[...] = dx
        dfo_ref[...] = dfo.astype(bf16)

    vec = pl.BlockSpec((1, d), lambda i: (0, 0))
    tile = pl.BlockSpec((tm, d), lambda i: (i, 0))
    return pl.pallas_call(
        body, name="loss_fwd_bwd", grid=(rows // tm,),
        in_specs=[vec, vec, tile, tile, tile],
        out_specs=[pl.BlockSpec((1, LANES), lambda i: (0, 0)), vec, vec, tile, tile],
        out_shape=[jax.ShapeDtypeStruct((1, LANES), f32), jax.ShapeDtypeStruct((1, d), f32),
                   jax.ShapeDtypeStruct((1, d), f32), jax.ShapeDtypeStruct((rows, d), f32),
                   jax.ShapeDtypeStruct((rows, d), bf16)],
        compiler_params=_cparams(("arbitrary",)),
    )(gate2, wf, x1, fo, target)


def _softplus(z):
    return jnp.maximum(z, 0.0) + jnp.log(1.0 + jnp.exp(-jnp.abs(z)))


def _split_dot(a, m):
    hi = a.astype(bf16)
    lo = (a - hi.astype(f32)).astype(bf16)
    return jnp.dot(hi, m, preferred_element_type=f32) + jnp.dot(lo, m, preferred_element_type=f32)


def _suffix_matrix(n):
    r = lax.broadcasted_iota(jnp.int32, (n, n), 0)
    c = lax.broadcasted_iota(jnp.int32, (n, n), 1)
    return (r > c).astype(bf16)


def _head_masks():
    lane = lax.broadcasted_iota(jnp.int32, (1, LANES), 1)
    return [(lane < SB_HEAD_DIM).astype(f32), (lane >= SB_HEAD_DIM).astype(f32)]


def _sb_prepare(proj):
    wins = [(proj, SB_WIDTH, OFF_SBK // SB_WIDTH), (proj, SB_WIDTH, OFF_SBV // SB_WIDTH)]
    return _stage_fwd(lambda k, v: (k, v), [], wins, [bf16] * 2, "sb_prepare")


def _by_head(rows_head0, rows_head1):
    m0, m1 = _head_masks()
    return rows_head0 * m0 + rows_head1 * m1


def _stack_heads(x):
    m0, m1 = _head_masks()
    return jnp.concatenate([x * m0, x * m1], axis=0)


def _sb_logits(qst, k, t_pos2, kb, bq, masked):
    z = lax.dot_general(qst, k, (((1,), (1,)), ((), ())), preferred_element_type=f32)
    l = -_softplus(z)
    if masked:
        s_pos = kb * bq + lax.broadcasted_iota(jnp.int32, (1, bq), 1)
        causal = s_pos < t_pos2
        l = jnp.where(causal, l, 0.0)
    else:
        causal = None
    return z, l, causal


class _SideComm:
    def __init__(self, protocol, arrs, out_shapes):
        self.protocol, self.arrs, self.out_shapes = protocol, list(arrs), list(out_shapes)
        self.n = len(self.arrs)

    def specs(self):
        return [pl.BlockSpec(memory_space=pl.ANY)] * self.n

    def run(self, in_refs, out_refs, sems, first, last, compute):
        start, finish = self.protocol(in_refs, out_refs, *sems)
        pl.when(first)(start)
        compute()
        pl.when(last)(finish)


def _grid_ends(grid):
    ids = [pl.program_id(axis) for axis in range(len(grid))]
    first = functools.reduce(jnp.logical_and, [i == 0 for i in ids])
    last = functools.reduce(jnp.logical_and, [i == g - 1 for i, g in zip(ids, grid)])
    return first, last


def _sb_attention_fwd2(proj, k16, v16, side=None):
    rows = proj.shape[0]
    bq = SB_QBLOCK
    nq = rows // bq
    assert nq <= LANES, "one lane per key block"
    npair = SB_WIDTH // LANES
    scale = SB_HEAD_DIM ** -0.5

    npp = SB_PAIRS_FWD
    wq = npp * LANES
    grid = (npair // npp, nq)
    ns = side.n if side else 0

    def body(q_ref, k_ref, v_ref, *rest):
        o_ref, runs_ref = rest[ns], rest[ns + 1]
        if side:
            side.run(rest[:ns], rest[ns + 2:2 * ns + 2], rest[2 * ns + 2:], *_grid_ends(grid),
                     lambda: compute(q_ref, k_ref, v_ref, o_ref, runs_ref))
        else:
            compute(q_ref, k_ref, v_ref, o_ref, runs_ref)

    def compute(q_ref, k_ref, v_ref, o_ref, runs_ref):
        qi = pl.program_id(1)
        pairs = [slice(pp * LANES, (pp + 1) * LANES) for pp in range(npp)]
        qst = [(_stack_heads(q_ref[:, s]) * scale).astype(bf16) for s in pairs]
        r = lax.broadcasted_iota(jnp.int32, (bq, 2 * bq), 0)
        c = lax.broadcasted_iota(jnp.int32, (bq, 2 * bq), 1)
        m2 = jnp.logical_or(r > c, c >= bq).astype(bf16)
        t_pos = qi * bq + lax.broadcasted_iota(jnp.int32, (bq, 1), 0)
        t_pos2 = jnp.concatenate([t_pos, t_pos], axis=0)
        lane = lax.broadcasted_iota(jnp.int32, (1, LANES), 1)
        runs_ref[...] = jnp.full(runs_ref.shape, SB_NEVER, f32)

        def tiles(kbs, carry, masked):
            jobs = [(pp, kb) for kb in kbs for pp in range(npp)]
            rows_k = [pl.ds(pl.multiple_of(kb * bq, bq), bq) for _, kb in jobs]
            zl = [_sb_logits(qst[pp], k_ref[rk, pairs[pp]], t_pos2, kb, bq, masked) for (pp, kb), rk in zip(jobs, rows_k)]
            cs = [_split_dot(l, m2) for _, l, _ in zl]
            run = [cr[0] for cr in carry]
            acc = [cr[1] for cr in carry]
            probs = []
            for (pp, kb), (z, l, causal), cs2 in zip(jobs, zl, cs):
                a = jnp.exp(z + l + cs2[:, :bq] + run[pp])
                if masked:
                    a = jnp.where(causal, a, 0.0)
                probs.append(a.astype(bf16))
                for hh in range(2):
                    cols = slice((2 * pp + hh) * LANES, (2 * pp + hh + 1) * LANES)
                    runs_ref[:, cols] = jnp.where(lane == kb, run[pp][hh * bq:(hh + 1) * bq], runs_ref[:, cols])
                run[pp] = run[pp] + cs2[:, bq:]
            for (pp, kb), rk, ab in zip(jobs, rows_k, probs):
                v = v_ref[rk, pairs[pp]]
                acc[pp] = acc[pp] + _by_head(jnp.dot(ab[:bq], v, preferred_element_type=f32),
                                             jnp.dot(ab[bq:], v, preferred_element_type=f32))
            return tuple(zip(run, acc))

        zero = (jnp.zeros((2 * bq, bq), f32), jnp.zeros((bq, LANES), f32))
        carry = tiles([qi], (zero,) * npp, True)

        def alive(cr):
            return functools.reduce(jnp.maximum, [jnp.max(run) for run, _ in cr]) > SB_DEAD

        def two(state):
            i, _, cr = state
            cr = tiles([qi - 1 - 2 * i, qi - 2 - 2 * i], cr, False)
            return i + 1, alive(cr), cr

        n_two = qi // 2
        i_end, still, carry = lax.while_loop(lambda st: jnp.logical_and(st[0] < n_two, st[1]), two,
                                             (jnp.int32(0), alive(carry), carry))
        last_one = jnp.logical_and(qi % 2 == 1, jnp.logical_and(still, i_end == n_two))
        carry = lax.cond(last_one, lambda cr: tiles([0], cr, False), lambda cr: cr, carry)
        for pp in range(npp):
            o_ref[:, pairs[pp]] = carry[pp][1]

    kv = pl.BlockSpec((rows, wq), lambda p, i: (0, p))
    return pl.pallas_call(
        body, name="sb_attn_fwd", grid=grid,
        in_specs=[pl.BlockSpec((bq, wq), lambda p, i: (i, OFF_SBQ // wq + p)), kv, kv] + (side.specs() if side else []),
        out_specs=[pl.BlockSpec((bq, wq), lambda p, i: (i, p)),
                   pl.BlockSpec((bq, 2 * wq), lambda p, i: (i, p))] + (side.specs() if side else []),
        out_shape=[jax.ShapeDtypeStruct((rows, SB_WIDTH), f32),
                   jax.ShapeDtypeStruct((rows, SB_HEADS * LANES), f32)] + (side.out_shapes if side else []),
        scratch_shapes=_comm_scratch(ns) if side else [],
        compiler_params=_cparams(("arbitrary", "arbitrary")),
    )(proj, k16, v16, *(side.arrs if side else []))


def _sb_attention_bwd2(proj, k16, v16, runs, do, side=None):
    rows = proj.shape[0]
    bq = SB_QBLOCK
    nq = rows // bq
    npair = SB_WIDTH // LANES
    scale = SB_HEAD_DIM ** -0.5
    tn = (((0,), (0,)), ((), ()))
    nt = (((1,), (1,)), ((), ()))

    npp = SB_PAIRS_BWD
    wq = npp * LANES
    grid = (npair // npp, nq)
    ns = side.n if side else 0

    def body(q_ref, k_ref, v_ref, runs_ref, do_ref, *rest):
        outs, accs = rest[ns:ns + 3], rest[-2:]
        ins = (q_ref, k_ref, v_ref, runs_ref, do_ref)
        if side:
            side.run(rest[:ns], rest[ns + 3:2 * ns + 3], rest[2 * ns + 3:-2], *_grid_ends(grid),
                     lambda: compute(*ins, *outs, *accs))
        else:
            compute(*ins, *outs, *accs)

    def compute(q_ref, k_ref, v_ref, runs_ref, do_ref, dq_ref, dk_out, dv_out, dk_ref, dv_ref):
        qi = pl.program_id(1)

        @pl.when(qi == 0)
        def _():
            dk_ref[...] = jnp.zeros_like(dk_ref)
            dv_ref[...] = jnp.zeros_like(dv_ref)

        pairs = [slice(pp * LANES, (pp + 1) * LANES) for pp in range(npp)]
        qst = [(_stack_heads(q_ref[:, s]) * scale).astype(bf16) for s in pairs]
        dost = [_stack_heads(do_ref[:, s]).astype(bf16) for s in pairs]
        runs = [jnp.concatenate([runs_ref[:, 2 * pp * LANES:(2 * pp + 1) * LANES],
                                 runs_ref[:, (2 * pp + 1) * LANES:(2 * pp + 2) * LANES]], axis=0) for pp in range(npp)]
        r = lax.broadcasted_iota(jnp.int32, (bq, 2 * bq), 0)
        c = lax.broadcasted_iota(jnp.int32, (bq, 2 * bq), 1)
        suffix_m = _suffix_matrix(bq)
        m2 = jnp.logical_or(r < c, c >= bq).astype(bf16)
        t_pos = qi * bq + lax.broadcasted_iota(jnp.int32, (bq, 1), 0)
        t_pos2 = jnp.concatenate([t_pos, t_pos], axis=0)
        lane = lax.broadcasted_iota(jnp.int32, (1, LANES), 1)

        def tiles(kbs, carry, masked):
            jobs = [(pp, kb) for kb in kbs for pp in range(npp)]
            rows_k = [pl.ds(pl.multiple_of(kb * bq, bq), bq) for _, kb in jobs]
            zl = [_sb_logits(qst[pp], k_ref[rk, pairs[pp]], t_pos2, kb, bq, masked) for (pp, kb), rk in zip(jobs, rows_k)]
            das = [lax.dot_general(dost[pp], v_ref[rk, pairs[pp]], nt, preferred_element_type=f32)
                   for (pp, kb), rk in zip(jobs, rows_k)]
            sticks = [_split_dot(l, suffix_m) for _, l, _ in zl]
            probs, ps = [], []
            for (pp, kb), (z, l, causal), stick, da in zip(jobs, zl, sticks, das):
                run = jnp.sum(jnp.where(lane == kb, runs[pp], 0.0), axis=1, keepdims=True)
                a = jnp.exp(z + l + stick + run)
                if masked:
                    a = jnp.where(causal, a, 0.0)
                probs.append(a.astype(bf16))
                ps.append(da * a)
            pcs = [_split_dot(p, m2) for p in ps]
            pref = [cr[0] for cr in carry]
            dq_acc = [cr[1] for cr in carry]
            dzs = []
            for (pp, kb), (z, l, causal), p, pc2 in zip(jobs, zl, ps, pcs):
                dz = p * jnp.exp(l) - jnp.exp(z + l) * (pc2[:, :bq] + pref[pp])
                if masked:
                    dz = jnp.where(causal, dz, 0.0)
                dzs.append(dz.astype(bf16))
                pref[pp] = pref[pp] + pc2[:, bq:]
            for (pp, kb), rk, dzb, ab in zip(jobs, rows_k, dzs, probs):
                cols = pairs[pp]
                k = k_ref[rk, cols]
                dq_acc[pp] = dq_acc[pp] + _by_head(jnp.dot(dzb[:bq], k, preferred_element_type=f32),
                                                   jnp.dot(dzb[bq:], k, preferred_element_type=f32))
                dk_ref[rk, cols] += lax.dot_general(dzb, qst[pp], tn, preferred_element_type=f32)
                dv_ref[rk, cols] += lax.dot_general(ab, dost[pp], tn, preferred_element_type=f32)
            return tuple(zip(pref, dq_acc))

        zero = (jnp.zeros((2 * bq, bq), f32), jnp.zeros((bq, LANES), f32))
        colmax = functools.reduce(jnp.maximum, [jnp.max(x, axis=0, keepdims=True) for x in runs])
        live = jnp.logical_and(colmax > SB_DEAD, lane < qi)
        kb0 = jnp.minimum(jnp.min(jnp.where(live, lane, LANES)), qi)
        n_blocks = qi - kb0
        carry = lax.fori_loop(0, n_blocks // 2, lambda i, cr: tiles([kb0 + 2 * i, kb0 + 2 * i + 1], cr, False),
                              (zero,) * npp)
        carry = lax.cond(n_blocks % 2 == 1, lambda cr: tiles([qi - 1], cr, False), lambda cr: cr, carry)
        carry = tiles([qi], carry, True)
        for pp in range(npp):
            dq_ref[:, pairs[pp]] = (carry[pp][1] * scale).astype(dq_ref.dtype)

        @pl.when(qi == nq - 1)
        def _():
            dk_out[...] = dk_ref[...].astype(dk_out.dtype)
            dv_out[...] = dv_ref[...].astype(dv_out.dtype)

    blk = pl.BlockSpec((bq, wq), lambda p, i: (i, p))
    full = pl.BlockSpec((rows, wq), lambda p, i: (0, p), pipeline_mode=pl.Buffered(1))
    return pl.pallas_call(
        body, name="sb_attn_bwd", grid=grid,
        in_specs=[pl.BlockSpec((bq, wq), lambda p, i: (i, OFF_SBQ // wq + p)), full, full,
                  pl.BlockSpec((bq, 2 * wq), lambda p, i: (i, p)), blk] + (side.specs() if side else []),
        out_specs=[blk, full, full] + (side.specs() if side else []),
        out_shape=[jax.ShapeDtypeStruct((rows, SB_WIDTH), bf16)] * 3 + (side.out_shapes if side else []),
        scratch_shapes=(_comm_scratch(ns) if side else []) + [pltpu.VMEM((rows, wq), f32)] * 2,
        compiler_params=_cparams(("arbitrary", "arbitrary")),
    )(proj, k16, v16, runs, do, *(side.arrs if side else []))


def _shift_down(x, prev8, j):
    if j == 0:
        return x
    r = pltpu.roll(x, j, axis=0)
    row8 = lax.broadcasted_iota(jnp.int32, prev8.shape, 0)
    head = jnp.where(row8 < j, pltpu.roll(prev8, j, axis=0), r[0:SUBLANES])
    return jnp.concatenate([head, r[SUBLANES:]], axis=0)


def _shift_up(x, next8, j):
    if j == 0:
        return x
    n = x.shape[0]
    r = pltpu.roll(x, n - j, axis=0)
    row8 = lax.broadcasted_iota(jnp.int32, next8.shape, 0)
    tail = jnp.where(row8 >= SUBLANES - j, pltpu.roll(next8, SUBLANES - j, axis=0), r[n - SUBLANES:n])
    return jnp.concatenate([r[:n - SUBLANES], tail], axis=0)


def _conv(x, prev8, w):
    k_taps = w.shape[0]
    out = x * w[k_taps - 1:k_taps, :]
    for j in range(1, k_taps):
        out = out + _shift_down(x, prev8, j) * w[k_taps - 1 - j:k_taps - j, :]
    return out


def _conv_tiles(rows):
    tr = min(TCONV_R, rows)
    return tr, rows // tr, tr // SUBLANES


def _prev_spec(tc, cb0, r8):
    return pl.BlockSpec((SUBLANES, tc), lambda j, i: (jnp.maximum(i * r8 - 1, 0), cb0 + j))


def _silu(x):
    return x * jax.nn.sigmoid(x)


def _dsilu(x):
    s = jax.nn.sigmoid(x)
    return s * (1.0 + x * (1.0 - s))


def _dn_conv_fwd(proj, w):
    rows = proj.shape[0]
    tr, nr, r8 = _conv_tiles(rows)
    tc = TCONV_C
    cb0 = OFF_DN // tc

    def body(x_ref, p_ref, w_ref, o_ref):
        prev = jnp.where(pl.program_id(1) == 0, 0.0, p_ref[...])
        o_ref[...] = _silu(_conv(x_ref[...], prev, w_ref[...]))

    return pl.pallas_call(
        body, name="dn_conv_fwd", grid=(DN_CONV_CH // tc, nr),
        in_specs=[pl.BlockSpec((tr, tc), lambda j, i: (i, cb0 + j)), _prev_spec(tc, cb0, r8),
                  pl.BlockSpec((DN_CONV_WIDTH, tc), lambda j, i: (0, j))],
        out_specs=pl.BlockSpec((tr, tc), lambda j, i: (i, j)),
        out_shape=jax.ShapeDtypeStruct((rows, DN_CONV_CH), f32),
        compiler_params=_cparams(("parallel", "parallel")),
    )(proj, proj, w)


def _dn_conv_bwd_act(proj, w, dact):
    rows = proj.shape[0]
    tr, nr, r8 = _conv_tiles(rows)
    tc = TCONV_C
    cb0 = OFF_DN // tc

    def body(x_ref, p_ref, w_ref, d_ref, o_ref):
        prev = jnp.where(pl.program_id(1) == 0, 0.0, p_ref[...])
        o_ref[...] = d_ref[...] * _dsilu(_conv(x_ref[...], prev, w_ref[...]))

    return pl.pallas_call(
        body, name="dn_conv_bwd_act", grid=(DN_CONV_CH // tc, nr),
        in_specs=[pl.BlockSpec((tr, tc), lambda j, i: (i, cb0 + j)), _prev_spec(tc, cb0, r8),
                  pl.BlockSpec((DN_CONV_WIDTH, tc), lambda j, i: (0, j)),
                  pl.BlockSpec((tr, tc), lambda j, i: (i, j))],
        out_specs=pl.BlockSpec((tr, tc), lambda j, i: (i, j)),
        out_shape=jax.ShapeDtypeStruct((rows, DN_CONV_CH), f32),
        compiler_params=_cparams(("parallel", "parallel")),
    )(proj, proj, w, dact)


def _ffn_conv_fwd(u_pre, w, b):
    rows = u_pre.shape[0]
    tr, nr, r8 = _conv_tiles(rows)
    tc = TCONV_FF
    nct = D_FF // tc

    def body(xg_ref, pg_ref, xu_ref, pu_ref, wg_ref, wu_ref, bg_ref, bu_ref, o_ref, u_ref):
        first = pl.program_id(1) == 0
        ug = _conv(xg_ref[...], jnp.where(first, 0.0, pg_ref[...]), wg_ref[...]) + bg_ref[...]
        uu = _conv(xu_ref[...], jnp.where(first, 0.0, pu_ref[...]), wu_ref[...]) + bu_ref[...]
        o_ref[...] = (_silu(ug) * uu).astype(o_ref.dtype)
        u_ref[0] = ug
        u_ref[1] = uu

    def x_spec(off):
        return pl.BlockSpec((tr, tc), lambda j, i: (i, off + j))

    def w_spec(k, off):
        return pl.BlockSpec((k, tc), lambda j, i: (0, off + j))

    return pl.pallas_call(
        body, name="ffn_conv_fwd", grid=(nct, nr),
        in_specs=[x_spec(0), _prev_spec(tc, 0, r8), x_spec(nct), _prev_spec(tc, nct, r8),
                  w_spec(FFN_CONV_WIDTH, 0), w_spec(FFN_CONV_WIDTH, nct), w_spec(1, 0), w_spec(1, nct)],
        out_specs=[pl.BlockSpec((tr, tc), lambda j, i: (i, j)), pl.BlockSpec((2, tr, tc), lambda j, i: (0, i, j))],
        out_shape=[jax.ShapeDtypeStruct((rows, D_FF), bf16), jax.ShapeDtypeStruct((2, rows, D_FF), f32)],
        compiler_params=_cparams(("parallel", "parallel")),
    )(u_pre, u_pre, u_pre, u_pre, w, w, b, b)


def _ffn_conv_bwd_act(u, dact):
    rows = dact.shape[0]
    tr, nr, _ = _conv_tiles(rows)
    tc = TCONV_FF
    nct = D_FF // tc

    def body(u_ref, d_ref, du_ref, dbg_ref, dbu_ref):
        first = pl.program_id(1) == 0
        ug = u_ref[0]
        uu = u_ref[1]
        d = d_ref[...]
        sig = jax.nn.sigmoid(ug)
        dug = d * uu * (sig * (1.0 + ug * (1.0 - sig)))
        duu = d * (ug * sig)
        du_ref[0] = dug
        du_ref[1] = duu

        @pl.when(first)
        def _():
            dbg_ref[...] = jnp.zeros_like(dbg_ref)
            dbu_ref[...] = jnp.zeros_like(dbu_ref)

        dbg_ref[...] += jnp.sum(dug, axis=0, keepdims=True)
        dbu_ref[...] += jnp.sum(duu, axis=0, keepdims=True)

    pair = pl.BlockSpec((2, tr, tc), lambda j, i: (0, i, j))
    vec = pl.BlockSpec((1, tc), lambda j, i: (0, j))
    return pl.pallas_call(
        body, name="ffn_conv_bwd_act", grid=(nct, nr),
        in_specs=[pair, pl.BlockSpec((tr, tc), lambda j, i: (i, j))],
        out_specs=[pair, vec, vec],
        out_shape=[jax.ShapeDtypeStruct((2, rows, D_FF), f32),
                   jax.ShapeDtypeStruct((1, D_FF), f32), jax.ShapeDtypeStruct((1, D_FF), f32)],
        compiler_params=_cparams(("parallel", "arbitrary")),
    )(u, dact)


def _conv_bwd(dy, x, x_cb0, w, name):
    k_taps = w.shape[0]
    split = dy.ndim == 3
    rows = dy.shape[-2]
    ch = dy.shape[-1] * (2 if split else 1)
    tc = TCONV_FF if split else TCONV_C
    tr, nr, r8 = _conv_tiles(rows)
    per_half = dy.shape[-1] // tc
    last8 = rows // SUBLANES - 1

    def body(dy_ref, nx_ref, x_ref, w_ref, dx_ref, dw_ref):
        i = pl.program_id(1)
        dyv = dy_ref[...]
        nxt = jnp.where(i == nr - 1, 0.0, nx_ref[...])
        xv = x_ref[...].astype(f32)
        wv = w_ref[...]

        @pl.when(i == 0)
        def _():
            dw_ref[...] = jnp.zeros_like(dw_ref)

        dx = dyv * wv[k_taps - 1:k_taps, :]
        dw_ref[k_taps - 1:k_taps, :] += jnp.sum(dyv * xv, axis=0, keepdims=True)
        for j in range(1, k_taps):
            dy_j = _shift_up(dyv, nxt, j)
            dx = dx + dy_j * wv[k_taps - 1 - j:k_taps - j, :]
            dw_ref[k_taps - 1 - j:k_taps - j, :] += jnp.sum(dy_j * xv, axis=0, keepdims=True)
        dx_ref[...] = dx.astype(dx_ref.dtype)

    tile = pl.BlockSpec((tr, tc), lambda j, i: (i, j))
    if split:
        dy_spec = pl.BlockSpec((None, tr, tc), lambda j, i: (j // per_half, i, j % per_half))
        next_spec = pl.BlockSpec((None, SUBLANES, tc),
                                 lambda j, i: (j // per_half, jnp.minimum((i + 1) * r8, last8), j % per_half))
    else:
        dy_spec = tile
        next_spec = pl.BlockSpec((SUBLANES, tc), lambda j, i: (jnp.minimum((i + 1) * r8, last8), j))
    return pl.pallas_call(
        body, name=name, grid=(ch // tc, nr),
        in_specs=[dy_spec, next_spec, pl.BlockSpec((tr, tc), lambda j, i: (i, x_cb0 + j)),
                  pl.BlockSpec((k_taps, tc), lambda j, i: (0, j))],
        out_specs=[tile, pl.BlockSpec((k_taps, tc), lambda j, i: (0, j))],
        out_shape=[jax.ShapeDtypeStruct((rows, ch), bf16), jax.ShapeDtypeStruct((k_taps, ch), f32)],
        compiler_params=_cparams(("parallel", "arbitrary")),
    )(dy, dy, x, w)


def _hdot(a, b):
    return jnp.dot(a, b, preferred_element_type=f32, precision=lax.Precision.HIGH)


def _xdot(a, b):
    return jnp.dot(a, b, preferred_element_type=f32, precision=lax.Precision.HIGHEST)


def _bdot(a, b):
    return jnp.dot(a.astype(bf16), b.astype(bf16), preferred_element_type=f32)


def _bdot_nt(a, b):
    return lax.dot_general(a.astype(bf16), b.astype(bf16), (((1,), (1,)), ((), ())), preferred_element_type=f32)


def _bdot_tn(a, b):
    return lax.dot_general(a.astype(bf16), b.astype(bf16), (((0,), (0,)), ((), ())), preferred_element_type=f32)


GDN_GROUP = 4
GDN_NGROUPS = DN_HEADS // GDN_GROUP
GDN_ROWS = GDN_GROUP * DN_CHUNK
GDN_QK_LANES = GDN_GROUP * DN_KEY_DIM
GDN_LOGIT_LANE = DN_HEADS


def _inverse_impl(lows):
    n = lows[0].shape[0]
    r = lax.broadcasted_iota(jnp.int32, (n, n), 0)
    c = lax.broadcasted_iota(jnp.int32, (n, n), 1)
    eye = (r == c).astype(f32)
    blk = jnp.right_shift(r, 3) == jnp.right_shift(c, 3)
    d = [jnp.where(blk, low, 0.0) for low in lows]
    e = [low - x for low, x in zip(lows, d)]

    def nilpotent8_inverse(xs):
        acc = [eye - x for x in xs]
        power = xs
        for _ in range(2):
            power = [_bdot(x, x) for x in power]
            acc = [_bdot(a, eye + x) for a, x in zip(acc, power)]
        return acc

    dinv = nilpotent8_inverse(d)
    ninv = nilpotent8_inverse([_bdot(x, y) for x, y in zip(dinv, e)])
    t = [_bdot(x, y) for x, y in zip(ninv, dinv)]
    for _ in range(2):
        res = [eye - x - _hdot(low, x) for low, x in zip(lows, t)]
        t = [x + _bdot(x, y) for x, y in zip(t, res)]
    return tuple(t)


@jax.custom_vjp
def _unit_lower_inverses(lows):
    return _inverse_impl(lows)


def _unit_lower_inverses_fwd(lows):
    t = _inverse_impl(lows)
    return t, t


def _unit_lower_inverses_bwd(t, ct):
    tn = (((0,), (0,)), ((), ()))
    nt = (((1,), (1,)), ((), ()))
    left = [lax.dot_general(x, g, tn, preferred_element_type=f32, precision=lax.Precision.HIGH) for x, g in zip(t, ct)]
    return (tuple(-lax.dot_general(x, y, nt, preferred_element_type=f32, precision=lax.Precision.HIGH)
                  for x, y in zip(left, t)),)


_unit_lower_inverses.defvjp(_unit_lower_inverses_fwd, _unit_lower_inverses_bwd)


@jax.custom_vjp
def _known_inverses(lows, t):
    return t


def _known_inverses_fwd(lows, t):
    return t, t


def _known_inverses_bwd(t, ct):
    return _unit_lower_inverses_bwd(t, ct) + (tuple(jnp.zeros_like(x) for x in t),)


_known_inverses.defvjp(_known_inverses_fwd, _known_inverses_bwd)


def _gdn_chunk(a_log, dt_bias, norm_w, ba, *per_group, inverses=None, keep_inverses=False):
    ng = GDN_NGROUPS
    qgs, kgs, vsts, zsts, states = [per_group[i * ng:(i + 1) * ng] for i in range(5)]
    groups = range(ng)
    n = GDN_ROWS
    r = lax.broadcasted_iota(jnp.int32, (n, n), 0)
    c = lax.broadcasted_iota(jnp.int32, (n, n), 1)
    same_head = jnp.right_shift(r, 6) == jnp.right_shift(c, 6)
    incl = jnp.logical_and(same_head, r >= c)
    strict = jnp.logical_and(same_head, r > c)
    eye = (r == c).astype(f32)
    ones = jnp.ones((n, n), f32)
    own_lanes = same_head.astype(f32)
    lane = lax.broadcasted_iota(jnp.int32, (1, LANES), 1)
    pick = lambda arr, idx: jnp.sum(jnp.where(lane == idx, arr, 0.0), axis=1, keepdims=True)
    heads = [[GDN_GROUP * g + h for h in range(GDN_GROUP)] for g in groups]
    rc = lax.broadcasted_iota(jnp.int32, (DN_CHUNK, DN_CHUNK), 0)
    cc = lax.broadcasted_iota(jnp.int32, (DN_CHUNK, DN_CHUNK), 1)

    g_all = -jnp.exp(a_log) * _softplus(ba + dt_bias)
    gc_all = _xdot((rc >= cc).astype(f32), g_all)
    gl_all = jnp.sum(g_all, axis=0, keepdims=True)
    beta = [jnp.concatenate([jax.nn.sigmoid(pick(ba, hd)) for hd in heads[g]], axis=0) for g in groups]
    gc = [jnp.concatenate([pick(gc_all, GDN_LOGIT_LANE + hd) for hd in heads[g]], axis=0) for g in groups]
    g_last = [jnp.concatenate([jnp.broadcast_to(pick(gl_all, GDN_LOGIT_LANE + hd), (DN_CHUNK, 1)) for hd in heads[g]],
                              axis=0) for g in groups]
    gr = [jnp.broadcast_to(gc[g], (n, n)).T for g in groups]
    decay = [jnp.where(incl, jnp.exp(jnp.where(incl, gc[g] - gr[g], 0.0)), 0.0) for g in groups]
    q = [jnp.concatenate([qgs[g]] * GDN_GROUP, axis=0) * own_lanes for g in groups]
    k = [jnp.concatenate([kgs[g]] * GDN_GROUP, axis=0) * own_lanes for g in groups]
    qn = [x * lax.rsqrt(jnp.sum(x * x, axis=1, keepdims=True) + L2_EPS) * (DN_KEY_DIM ** -0.5) for x in q]
    kn = [x * lax.rsqrt(jnp.sum(x * x, axis=1, keepdims=True) + L2_EPS) for x in k]
    kb = [kn[g] * beta[g] for g in groups]
    low = [jnp.where(strict, _bdot_nt(kb[g], kn[g]) * decay[g], 0.0) for g in groups]
    intra = [jnp.where(incl, _bdot_nt(qn[g], kn[g]) * decay[g], 0.0) for g in groups]
    t = _unit_lower_inverses(tuple(low)) if inverses is None else _known_inverses(tuple(low), tuple(inverses))
    u = [_bdot(t[g], vsts[g] * beta[g]) for g in groups]
    w = [_bdot(t[g], kb[g] * jnp.exp(gc[g])) for g in groups]
    sb = [s.astype(bf16) for s in states]
    v_new = [u[g] - jnp.dot(w[g].astype(bf16), sb[g], preferred_element_type=f32) for g in groups]
    o = [jnp.dot((qn[g] * jnp.exp(gc[g])).astype(bf16), sb[g], preferred_element_type=f32) for g in groups]
    o = [o[g] + _bdot(intra[g], v_new[g]) for g in groups]
    new_state = [states[g] * jnp.exp(g_last[g]) + _bdot_tn(kn[g] * jnp.exp(g_last[g] - gc[g]), v_new[g])
                 for g in groups]
    o_n = [x * lax.rsqrt(jnp.mean(x * x, axis=1, keepdims=True) + NORM_EPS) * norm_w for x in o]
    return tuple(o_n[g] * _silu(zsts[g]) for g in groups) + tuple(new_state) + (tuple(t) if keep_inverses else ())


GDN_FWD_CHUNKS = 2
GDN_BWD_CHUNKS = 1


def _gdn_specs(rows, reverse, nc):
    tr = nc * DN_CHUNK
    n = rows // tr
    idx = (lambda i: n - 1 - i) if reverse else (lambda i: i)
    vec = pl.BlockSpec((1, LANES), lambda i: (0, 0))
    qkv = pl.BlockSpec((tr, DN_CONV_CH), lambda i: (idx(i), 0))
    z = pl.BlockSpec((tr, DN_V_WIDTH), lambda i: (idx(i), OFF_Z // DN_V_WIDTH))
    ba = pl.BlockSpec((tr, LANES), lambda i: (idx(i), 0))
    wide = pl.BlockSpec((tr, DN_V_WIDTH), lambda i: (idx(i), 0))
    st = pl.BlockSpec((nc, DN_HEADS * DN_KEY_DIM, LANES), lambda i: (idx(i), 0, 0))
    inv = pl.BlockSpec((nc, GDN_NGROUPS * GDN_ROWS, GDN_ROWS), lambda i: (idx(i), 0, 0))
    return n, vec, qkv, z, ba, wide, st, inv


def _chunk_rows(c):
    return slice(c * DN_CHUNK, (c + 1) * DN_CHUNK)


def _gdn_slices(grp):
    q = slice(grp * GDN_QK_LANES, (grp + 1) * GDN_QK_LANES)
    k = slice(DN_QK_WIDTH + grp * GDN_QK_LANES, DN_QK_WIDTH + (grp + 1) * GDN_QK_LANES)
    heads = [slice((GDN_GROUP * grp + h) * LANES, (GDN_GROUP * grp + h + 1) * LANES) for h in range(GDN_GROUP)]
    vs = [slice(2 * DN_QK_WIDTH + s.start, 2 * DN_QK_WIDTH + s.stop) for s in heads]
    return q, k, vs, heads


def _stack_cols(ref, rows, cols):
    return jnp.concatenate([ref[rows, s] for s in cols], axis=0)


def _gdn_operands(qkv_ref, z_ref, rows, state_rows):
    sl = [_gdn_slices(grp) for grp in range(GDN_NGROUPS)]
    return ([qkv_ref[rows, q] for q, _, _, _ in sl] + [qkv_ref[rows, k] for _, k, _, _ in sl]
            + [_stack_cols(qkv_ref, rows, vs) for _, _, vs, _ in sl]
            + [_stack_cols(z_ref, rows, heads) for _, _, _, heads in sl]
            + [state_rows[grp * GDN_ROWS:(grp + 1) * GDN_ROWS, :] for grp in range(GDN_NGROUPS)])


def _gdn_fwd(a_log, dt_bias, norm_w, qkv_act, proj, ba):
    rows = qkv_act.shape[0]
    n, vec, qkv_s, z_s, ba_s, wide, st_s, inv_s = _gdn_specs(rows, False, GDN_FWD_CHUNKS)

    def body(al_ref, dt_ref, nw_ref, qkv_ref, z_ref, ba_ref, o_ref, st_ref, inv_ref, state):
        @pl.when(pl.program_id(0) == 0)
        def _():
            state[...] = jnp.zeros_like(state)

        for c in range(GDN_FWD_CHUNKS):
            tok = _chunk_rows(c)
            st_ref[c] = state[...]
            out = _gdn_chunk(al_ref[...], dt_ref[...], nw_ref[...], ba_ref[tok, :],
                             *_gdn_operands(qkv_ref, z_ref, tok, state), keep_inverses=True)
            for grp in range(GDN_NGROUPS):
                _, _, _, heads = _gdn_slices(grp)
                rs = slice(grp * GDN_ROWS, (grp + 1) * GDN_ROWS)
                for h, s in enumerate(heads):
                    o_ref[tok, s] = out[grp][h * DN_CHUNK:(h + 1) * DN_CHUNK].astype(o_ref.dtype)
                state[rs, :] = out[GDN_NGROUPS + grp]
                inv_ref[c, rs, :] = out[2 * GDN_NGROUPS + grp]

    n_chunks = rows // DN_CHUNK
    return pl.pallas_call(
        body, name="gdn_fwd", grid=(n,),
        in_specs=[vec, vec, vec, qkv_s, z_s, ba_s], out_specs=[wide, st_s, inv_s],
        out_shape=[jax.ShapeDtypeStruct((rows, DN_V_WIDTH), bf16),
                   jax.ShapeDtypeStruct((n_chunks, DN_HEADS * DN_KEY_DIM, LANES), f32),
                   jax.ShapeDtypeStruct((n_chunks, GDN_NGROUPS * GDN_ROWS, GDN_ROWS), f32)],
        scratch_shapes=[pltpu.VMEM((DN_HEADS * DN_KEY_DIM, LANES), f32)],
        compiler_params=_cparams(("arbitrary",)),
    )(a_log, dt_bias, norm_w, qkv_act, proj, ba)


def _gdn_bwd(a_log, dt_bias, norm_w, qkv_act, proj, ba, states, inverses, do):
    rows = qkv_act.shape[0]
    n, vec, qkv_s, z_s, ba_s, wide, st_s, inv_s = _gdn_specs(rows, True, GDN_BWD_CHUNKS)

    def body(al_ref, dt_ref, nw_ref, qkv_ref, z_ref, ba_ref, st_ref, inv_ref, do_ref,
             dal_ref, ddt_ref, dnw_ref, dqkv_ref, dz_ref, dba_ref, dstate):
        @pl.when(pl.program_id(0) == 0)
        def _():
            dstate[...] = jnp.zeros_like(dstate)
            dal_ref[...] = jnp.zeros_like(dal_ref)
            ddt_ref[...] = jnp.zeros_like(ddt_ref)
            dnw_ref[...] = jnp.zeros_like(dnw_ref)

        ng = GDN_NGROUPS
        for c in reversed(range(GDN_BWD_CHUNKS)):
            tok = _chunk_rows(c)
            kept = [inv_ref[c, grp * GDN_ROWS:(grp + 1) * GDN_ROWS, :] for grp in range(ng)]
            _, vjp = jax.vjp(functools.partial(_gdn_chunk, inverses=kept), al_ref[...], dt_ref[...], nw_ref[...],
                             ba_ref[tok, :], *_gdn_operands(qkv_ref, z_ref, tok, st_ref[c]))
            cts = tuple(_stack_cols(do_ref, tok, _gdn_slices(grp)[3]) for grp in range(ng))
            cts += tuple(dstate[grp * GDN_ROWS:(grp + 1) * GDN_ROWS, :] for grp in range(ng))
            grads = vjp(cts)
            dal_ref[...] += grads[0]
            ddt_ref[...] += grads[1]
            dnw_ref[...] += grads[2]
            dba_ref[tok, :] = grads[3]
            dqs, dks, dvs, dzs, dss = [grads[4 + i * ng:4 + (i + 1) * ng] for i in range(5)]
            for grp in range(ng):
                q, k, vs, heads = _gdn_slices(grp)
                dqkv_ref[tok, q] = dqs[grp]
                dqkv_ref[tok, k] = dks[grp]
                for h, (sv, sh) in enumerate(zip(vs, heads)):
                    rows_h = slice(h * DN_CHUNK, (h + 1) * DN_CHUNK)
                    dqkv_ref[tok, sv] = dvs[grp][rows_h]
                    dz_ref[tok, sh] = dzs[grp][rows_h].astype(dz_ref.dtype)
                dstate[grp * GDN_ROWS:(grp + 1) * GDN_ROWS, :] = dss[grp]

    return pl.pallas_call(
        body, name="gdn_bwd", grid=(n,),
        in_specs=[vec, vec, vec, qkv_s, z_s, ba_s, st_s, inv_s, wide],
        out_specs=[vec, vec, vec, qkv_s, wide, ba_s],
        out_shape=[jax.ShapeDtypeStruct((1, LANES), f32)] * 3
        + [jax.ShapeDtypeStruct((rows, DN_CONV_CH), f32), jax.ShapeDtypeStruct((rows, DN_V_WIDTH), bf16),
           jax.ShapeDtypeStruct((rows, LANES), f32)],
        scratch_shapes=[pltpu.VMEM((DN_HEADS * DN_KEY_DIM, LANES), f32)],
        compiler_params=_cparams(("arbitrary",)),
    )(a_log, dt_bias, norm_w, qkv_act, proj, ba, states, inverses, do)


def _ada_fwd(c_all, w_loc, b_loc):
    def body(c_ref, w_ref, b_ref, o_ref):
        o_ref[...] = _bdot(_silu(c_ref[...]), w_ref[...]) + b_ref[...]

    return pl.pallas_call(body, name="ada_fwd", out_shape=jax.ShapeDtypeStruct((c_all.shape[0], w_loc.shape[1]), f32),
                          compiler_params=_cparams())(c_all, w_loc, b_loc)


def _ada_bwd(c_all, dmod_cols):
    def body(c_ref, d_ref, o_ref):
        o_ref[...] = _bdot_tn(_silu(c_ref[...]), d_ref[...])

    return pl.pallas_call(body, name="ada_bwd",
                          out_shape=jax.ShapeDtypeStruct((c_all.shape[1], dmod_cols.shape[1]), f32),
                          compiler_params=_cparams())(c_all, dmod_cols)


def _sum_devices(parts):
    def body(p_ref, o_ref):
        acc = p_ref[0:1, :]
        for d in range(1, N_DEV):
            acc = acc + p_ref[d:d + 1, :]
        o_ref[...] = acc

    return pl.pallas_call(body, name="sum_small", out_shape=jax.ShapeDtypeStruct((1, parts.shape[1]), f32),
                          compiler_params=_cparams())(parts)


def _adam_math(w, g, m, v):
    m2 = ADAM_B1 * m + (1.0 - ADAM_B1) * g
    v2 = ADAM_B2 * v + (1.0 - ADAM_B2) * jnp.square(g)
    m_hat = m2 / (1.0 - ADAM_B1 ** ADAM_STEP)
    v_hat = v2 / (1.0 - ADAM_B2 ** ADAM_STEP)
    delta = -ADAM_LR * (m_hat / (jnp.sqrt(v_hat) + ADAM_EPS) + ADAM_WD * w)
    return delta, m2, v2


def _row_tile(rows):
    return _pick(rows, (256, 128, 64, 32, 16, 8))


def _adamw(w, g, m, v, name):
    rows, cols = w.shape
    tr = _row_tile(rows)

    def body(w_ref, g_ref, m_ref, v_ref, d_ref, m2_ref, v2_ref):
        d_ref[...], m2_ref[...], v2_ref[...] = _adam_math(w_ref[...], g_ref[...], m_ref[...], v_ref[...])

    tile = pl.BlockSpec((tr, cols), lambda i: (i, 0))
    return pl.pallas_call(body, name=name, grid=(rows // tr,), in_specs=[tile] * 4, out_specs=[tile] * 3,
                          out_shape=[jax.ShapeDtypeStruct(w.shape, f32)] * 3,
                          compiler_params=_cparams(("parallel",)))(w, g, m, v)


def _sum_adamw(parts, w, m, v, name):
    rows, cols = w.shape
    tr = _row_tile(rows)

    def body(p_ref, w_ref, m_ref, v_ref, g_ref, d_ref, m2_ref, v2_ref):
        g = p_ref[0].astype(f32)
        for d in range(1, N_DEV):
            g = g + p_ref[d].astype(f32)
        g_ref[...] = g
        d_ref[...], m2_ref[...], v2_ref[...] = _adam_math(w_ref[...], g, m_ref[...], v_ref[...])

    tile = pl.BlockSpec((tr, cols), lambda i: (i, 0))
    return pl.pallas_call(body, name=name, grid=(rows // tr,),
                          in_specs=[pl.BlockSpec((N_DEV, tr, cols), lambda i: (0, i, 0)), tile, tile, tile],
                          out_specs=[tile] * 4, out_shape=[jax.ShapeDtypeStruct(w.shape, f32)] * 4,
                          compiler_params=_cparams(("parallel",)))(parts, w, m, v)


def _pad_lanes(a, width):
    return jnp.pad(a, ((0, 0), (0, width - a.shape[1])))


def _cols_by_device(full):
    r, c = full.shape
    return jnp.moveaxis(full.reshape(r, N_DEV, c // N_DEV), 1, 0)


def _cols_from_devices(parts):
    d, r, n = parts.shape
    return jnp.moveaxis(parts, 0, 1).reshape(r, d * n)


def kernel(x, c, w_ada, b_ada, norm1_w, w_in, dn_conv_w, dn_A_log, dn_dt_bias, dn_norm_w, w_proj_sb, w_proj_dn, w_out, norm2_w, w_ffn_in, ffn_conv_w, ffn_conv_b, w_ffn_out, final_norm_w, loss_target, m_w_ada, m_b_ada, m_norm1_w, m_w_in, m_dn_conv_w, m_dn_A_log, m_dn_dt_bias, m_dn_norm_w, m_w_proj_sb, m_w_proj_dn, m_w_out, m_norm2_w, m_w_ffn_in, m_ffn_conv_w, m_ffn_conv_b, m_w_ffn_out, m_final_norm_w, v_w_ada, v_b_ada, v_norm1_w, v_w_in, v_dn_conv_w, v_dn_A_log, v_dn_dt_bias, v_dn_norm_w, v_w_proj_sb, v_w_proj_dn, v_w_out, v_norm2_w, v_w_ffn_in, v_ffn_conv_w, v_ffn_conv_b, v_w_ffn_out, v_final_norm_w):
    d = D_MODEL
    me = 4 * lax.axis_index("x") + 2 * lax.axis_index("y") + lax.axis_index("c")
    xs = x[0]
    target = loss_target[0]
    n_ada = w_ada.shape[2]
    n_dnc = dn_conv_w.shape[2]
    n_ffc = ffn_conv_w.shape[2]

    small = jnp.concatenate([c, dn_conv_w[0].reshape(1, -1), ffn_conv_w[0].reshape(1, -1)], axis=1)
    small = _pad_lanes(small, -(-small.shape[1] // LANES) * LANES)
    small_g, w_in_g = _all_gather([small, w_in[0].astype(bf16)], "gather_w_in")
    later = [w_proj_sb[0].astype(bf16), w_proj_dn[0].astype(bf16), w_out[0].astype(bf16),
             w_ffn_in[0].astype(bf16), w_ffn_out[0].astype(bf16)]
    gather_later = _SideComm(_gather_protocol, later, _gathered_shapes(later))
    small_g = small_g[:, 0, :]
    c_all = small_g[:, :d]
    dn_cw = _cols_from_devices(small_g[:, d:d + DN_CONV_WIDTH * n_dnc].reshape(N_DEV, DN_CONV_WIDTH, n_dnc))
    o2 = d + DN_CONV_WIDTH * n_dnc
    ffn_cw = _cols_from_devices(small_g[:, o2:o2 + FFN_CONV_WIDTH * n_ffc].reshape(N_DEV, FFN_CONV_WIDTH, n_ffc))

    w_in_full = _cols_from_devices(w_in_g)
    r_sb, r_dn, r_z = 3 * SB_WIDTH, 3 * SB_WIDTH + DN_CONV_CH, 3 * SB_WIDTH + DN_CONV_CH + DN_V_WIDTH
    r_g = r_z + 2 * DN_HEADS
    w_main = jnp.concatenate([w_in_full[:, r_g:], w_in_full[:, r_sb:r_dn], w_in_full[:, r_dn:r_z],
                              w_in_full[:, :r_sb]], axis=1)
    w_ba = _pad_lanes(w_in_full[:, r_z:r_g], LANES)

    b_loc = lax.dynamic_slice(b_ada, (0, me * n_ada), (1, n_ada))
    mod_part = _ada_fwd(c_all, w_ada[0], b_loc)
    (mod_g,) = _all_gather([mod_part], "gather_mod")
    mod = lax.dynamic_index_in_dim(mod_g, me, axis=1, keepdims=False).reshape(1, N_DEV * n_ada)
    shift1, scale1, gate1, shift2, scale2, gate2 = [mod[:, i * d:(i + 1) * d] for i in range(6)]

    logit_lanes = ((0, 0), (GDN_LOGIT_LANE, LANES - GDN_LOGIT_LANE - DN_HEADS))
    a_log = jnp.pad(dn_A_log, logit_lanes)
    dt_b = jnp.pad(dn_dt_bias, logit_lanes)

    (h,) = _stage_fwd(_f_normmod, [norm1_w, shift1, scale1], [xs], [bf16], "norm1_fwd")
    proj = _mm(h, w_main, name="in_proj")
    ba = _mm(h, w_ba, name="in_proj_ba")
    k16, v16 = _sb_prepare(proj)
    o_a, sb_runs, w_psb_g, w_pdn_g, w_out_g, w_fin_g, w_fout_g = _sb_attention_fwd2(
        proj, k16, v16, side=gather_later)
    w_psb = _cols_from_devices(w_psb_g)
    w_pdn = w_pdn_g.reshape(DN_V_WIDTH, d)
    w_o = w_out_g.reshape(d, d)
    w_fin = _cols_from_devices(w_fin_g)
    w_fout = w_fout_g.reshape(D_FF, d)
    qkv_act = _dn_conv_fwd(proj, dn_cw)
    o_b, states, dn_inverses = _gdn_fwd(a_log, dt_b, dn_norm_w, qkv_act, proj, ba)
    pa = _mm(o_a, w_psb, name="proj_sb")
    pb = _mm(o_b, w_pdn, name="proj_dn")
    gates = [(proj, d, OFF_GA // d), (proj, d, OFF_GB // d)]
    (merged,) = _stage_fwd(_f_merge, [], gates + [pa, pb], [bf16], "merge_fwd")
    ao = _mm(merged, w_o, name="out_proj")
    mid_params = [gate1, norm2_w, shift2, scale2]
    x1, h2 = _stage_fwd(_f_residual_normmod, mid_params, [xs, ao], [f32, bf16], "resid1_norm2_fwd")
    u_pre = _mm(h2, w_fin, name="ffn_in")
    act, u_conv = _ffn_conv_fwd(u_pre, ffn_cw, ffn_conv_b)
    fo = _mm(act, w_fout, name="ffn_out")

    loss_p, d_gate2, d_wf, dx2, dfo = _loss_and_grads(gate2, final_norm_w.reshape(1, d), x1, fo, target)
    dact = _mm(dfo, w_fout, tb=True, name="ffn_out_dx")
    g_w_fout = _mm(act, dfo, ta=True, name="ffn_out_dw")
    du, dbg, dbu = _ffn_conv_bwd_act(u_conv, dact)
    du_pre, d_ffn_cw = _conv_bwd(du, u_pre, 0, ffn_cw, "ffn_conv_bwd")
    dh2 = _mm(du_pre, w_fin, tb=True, name="ffn_in_dx")
    g_w_fin = _mm(h2, du_pre, ta=True, name="ffn_in_dw")
    (d_gate1, d_n2w, d_shift2, d_scale2), (dx1, dao) = _stage_bwd(
        _f_residual_normmod, mid_params, [xs, ao], [dx2, dh2], [f32, bf16], "resid1_norm2_bwd")
    dmerged = _mm(dao, w_o, tb=True, name="out_proj_dx")
    g_w_o = _mm(merged, dao, ta=True, name="out_proj_dw")
    _, (dga, dgb, dpa, dpb) = _stage_bwd(_f_merge, [], gates + [pa, pb], [dmerged], [bf16] * 4, "merge_bwd")
    do_a = _mm(dpa, w_psb, tb=True, name="proj_sb_dx")
    g_w_psb = _mm(o_a, dpa, ta=True, name="proj_sb_dw")
    do_b = _mm(dpb, w_pdn, tb=True, name="proj_dn_dx")
    g_w_pdn = _mm(o_b, dpb, ta=True, name="proj_dn_dw")
    early = [_cols_by_device(g_w_psb).astype(bf16),
             g_w_pdn.reshape(N_DEV, DN_V_WIDTH // N_DEV, d).astype(bf16),
             g_w_o.reshape(N_DEV, d // N_DEV, d).astype(bf16),
             _cols_by_device(g_w_fin).astype(bf16),
             g_w_fout.reshape(N_DEV, D_FF // N_DEV, d).astype(bf16)]
    exchange_early = _SideComm(_exchange_protocol, early, [jax.ShapeDtypeStruct(a.shape, a.dtype) for a in early])
    dq, dk, dv, *recv_early = _sb_attention_bwd2(proj, k16, v16, sb_runs, do_a, side=exchange_early)
    d_alog, d_dtb, d_dnw, dqkv_act, dz, dba = _gdn_bwd(a_log, dt_b, dn_norm_w, qkv_act, proj, ba, states,
                                                       dn_inverses, do_b)
    d_conv_out = _dn_conv_bwd_act(proj, dn_cw, dqkv_act)
    d_dn_pre, d_dn_cw = _conv_bwd(d_conv_out, proj, OFF_DN // TCONV_C, dn_cw, "dn_conv_bwd")
    dproj = jnp.concatenate([dga, dgb, d_dn_pre, dz, dq, dk, dv], axis=1)
    g_w_main = _mm(h, dproj, ta=True, name="in_proj_dw")
    g_w_ba = _mm(h, dba, ta=True, name="in_proj_ba_dw")
    g_w_in_full = jnp.concatenate([g_w_main[:, OFF_SBQ:], g_w_main[:, OFF_DN:OFF_Z], g_w_main[:, OFF_Z:OFF_SBQ],
                                   g_w_ba[:, :2 * DN_HEADS], g_w_main[:, :OFF_DN]], axis=1)
    w_in_parts = _cols_by_device(g_w_in_full).astype(bf16)
    exchange_w_in = _SideComm(_exchange_protocol, [w_in_parts], [jax.ShapeDtypeStruct(w_in_parts.shape, bf16)])
    dh, recv_w_in = _mm(dproj, w_main, tb=True, name="in_proj_dx", side=exchange_w_in)
    dh_ba = _mm(dba, w_ba, tb=True, name="in_proj_ba_dx")
    (d_n1w, d_shift1, d_scale1), (grad_x,) = _stage_bwd(
        _f_normmod, [norm1_w, shift1, scale1], [xs], [[dh, dh_ba]], [f32], "norm1_bwd", residual=(0, dx1))

    dmod = jnp.concatenate([d_shift1, d_scale1, d_gate1, d_shift2, d_scale2, d_gate2], axis=1)
    d_ffn_cb = jnp.concatenate([dbg, dbu], axis=1)
    small_parts = jnp.concatenate(
        [loss_p, dmod, d_n1w, d_alog, d_dtb, d_dnw, d_n2w, d_ffn_cb, d_wf,
         d_dn_cw.reshape(1, -1), d_ffn_cw.reshape(1, -1)], axis=1)
    (small_parts_g,) = _all_gather([small_parts], "gather_small_grads")
    tot = _sum_devices(small_parts_g[:, 0, :])
    offs = {}
    pos = 0
    for nm, width in (("loss", LANES), ("b_ada", 6 * d), ("norm1_w", d), ("dn_A_log", LANES), ("dn_dt_bias", LANES),
                      ("dn_norm_w", LANES), ("norm2_w", d), ("ffn_conv_b", 2 * D_FF), ("final_norm_w", d),
                      ("dn_conv_w", DN_CONV_WIDTH * DN_CONV_CH), ("ffn_conv_w", FFN_CONV_WIDTH * 2 * D_FF)):
        offs[nm] = (pos, width)
        pos += width
    seg = lambda nm: tot[:, offs[nm][0]:offs[nm][0] + offs[nm][1]]
    loss = tot[0, 0]
    g_b_ada = seg("b_ada")
    g_norm1 = seg("norm1_w")
    g_alog = seg("dn_A_log")[:, GDN_LOGIT_LANE:GDN_LOGIT_LANE + DN_HEADS]
    g_dtb = seg("dn_dt_bias")[:, GDN_LOGIT_LANE:GDN_LOGIT_LANE + DN_HEADS]
    g_dnw = seg("dn_norm_w")
    g_norm2 = seg("norm2_w")
    g_ffn_cb = seg("ffn_conv_b")
    g_fnw = seg("final_norm_w")
    g_dn_cw = lax.dynamic_slice(seg("dn_conv_w").reshape(DN_CONV_WIDTH, DN_CONV_CH), (0, me * n_dnc),
                                (DN_CONV_WIDTH, n_dnc))
    g_ffn_cw = lax.dynamic_slice(seg("ffn_conv_w").reshape(FFN_CONV_WIDTH, 2 * D_FF), (0, me * n_ffc),
                                 (FFN_CONV_WIDTH, n_ffc))

    dmod_all = small_parts_g[:, 0, offs["b_ada"][0]:offs["b_ada"][0] + 6 * d]
    g_w_ada = _ada_bwd(c_all, lax.dynamic_slice(dmod_all, (0, me * n_ada), (N_DEV, n_ada)))

    def pack(parts):
        flat = [p.reshape(1, -1) for p in parts]
        flat = [_pad_lanes(p, -(-p.shape[1] // LANES) * LANES) for p in flat]
        return jnp.concatenate(flat, axis=1), [p.shape[1] for p in flat]

    small_names_g = [g_b_ada, g_norm1, g_alog, g_dtb, g_dnw, g_norm2, g_ffn_cb, g_fnw, g_dn_cw, g_ffn_cw]
    small_w = [b_ada, norm1_w, dn_A_log, dn_dt_bias, dn_norm_w, norm2_w, ffn_conv_b, final_norm_w, dn_conv_w[0], ffn_conv_w[0]]
    small_m = [m_b_ada, m_norm1_w, m_dn_A_log, m_dn_dt_bias, m_dn_norm_w, m_norm2_w, m_ffn_conv_b, m_final_norm_w, m_dn_conv_w[0], m_ffn_conv_w[0]]
    small_v = [v_b_ada, v_norm1_w, v_dn_A_log, v_dn_dt_bias, v_dn_norm_w, v_norm2_w, v_ffn_conv_b, v_final_norm_w, v_dn_conv_w[0], v_ffn_conv_w[0]]
    pg, widths = pack(small_names_g)
    pw, _ = pack(small_w)
    pm, _ = pack(small_m)
    pv, _ = pack(small_v)
    s_delta, s_m, s_v = _adamw(pw, pg, pm, pv, "adamw_small")

    def unpack(flat):
        out, pos = [], 0
        for ref_arr, width in zip(small_w, widths):
            out.append(flat[:, pos:pos + ref_arr.size].reshape(ref_arr.shape))
            pos += width
        return out

    small_grads = [g.reshape(w_.shape) for g, w_ in zip(small_names_g, small_w)]
    small_delta, small_newm, small_newv = unpack(s_delta), unpack(s_m), unpack(s_v)

    ada_delta, ada_m, ada_v = _adamw(w_ada[0], g_w_ada, m_w_ada[0], v_w_ada[0], "adamw_ada")

    recv = [recv_w_in] + list(recv_early)
    big = {}
    for nm, parts, w_, m_, v_ in (("w_in", recv[0], w_in, m_w_in, v_w_in),
                                  ("w_proj_sb", recv[1], w_proj_sb, m_w_proj_sb, v_w_proj_sb),
                                  ("w_proj_dn", recv[2], w_proj_dn, m_w_proj_dn, v_w_proj_dn),
                                  ("w_out", recv[3], w_out, m_w_out, v_w_out),
                                  ("w_ffn_in", recv[4], w_ffn_in, m_w_ffn_in, v_w_ffn_in),
                                  ("w_ffn_out", recv[5], w_ffn_out, m_w_ffn_out, v_w_ffn_out)):
        big[nm] = [t[None] for t in _sum_adamw(parts, w_[0], m_[0], v_[0], "adamw_" + nm)]

    sg = dict(zip(["b_ada", "norm1_w", "dn_A_log", "dn_dt_bias", "dn_norm_w", "norm2_w", "ffn_conv_b", "final_norm_w",
                   "dn_conv_w", "ffn_conv_w"], range(10)))

    def small_out(table, nm):
        val = table[sg[nm]]
        return val[None] if nm in ("dn_conv_w", "ffn_conv_w") else val

    order = ["w_ada", "b_ada", "norm1_w", "w_in", "dn_conv_w", "dn_A_log", "dn_dt_bias", "dn_norm_w", "w_proj_sb",
             "w_proj_dn", "w_out", "norm2_w", "w_ffn_in", "ffn_conv_w", "ffn_conv_b", "w_ffn_out", "final_norm_w"]
    groups = []
    for k, small_table in enumerate((small_grads, small_delta, small_newm, small_newv)):
        row = []
        for nm in order:
            if nm == "w_ada":
                row.append((g_w_ada, ada_delta, ada_m, ada_v)[k][None])
            elif nm in big:
                row.append(big[nm][k])
            else:
                row.append(small_out(small_table, nm))
        groups.append(row)
    return (loss, grad_x[None], *groups[0], *groups[1], *groups[2], *groups[3])
```
